```python
import math
import numpy as np
import jax
import jax.numpy as jnp
from jax import lax

D_MODEL = 1024
BATCH = 16
SEQ = 256
DEPTH = 2
DEC_BATCH = 2
DEC_SEQ = 1024
PAST_LEN = 256

GRID_W = 64
N_DIR = 2
CHUNK = 64
EPS = 1e-6
F_FLOOR = 1e-6
BR_WIDTH = D_MODEL // 2
A_WIDTH = BR_WIDTH
A_HEADS = 4
A_DK = A_WIDTH // A_HEADS
A_DV = A_WIDTH // A_HEADS
B_WIDTH = BR_WIDTH
B_GROUP = 16
B_GROUPS = B_WIDTH // B_GROUP
B_STATE = 64
DT_MIN = 0.001
DT_MAX = 0.1
C_WIDTH = BR_WIDTH
C_HEADS = 4
C_DK = C_WIDTH // C_HEADS
C_DV = C_WIDTH // C_HEADS
CONV_W = 3
N_BRANCH = 3
SPLITS = (A_WIDTH, A_WIDTH, A_WIDTH, A_WIDTH, A_WIDTH, B_WIDTH, C_WIDTH, C_WIDTH, C_WIDTH, C_WIDTH, N_DIR * C_HEADS, N_DIR * C_HEADS, N_BRANCH * D_MODEL)
N_IN = 5 * A_WIDTH + B_WIDTH + 4 * C_WIDTH + 2 * N_DIR * C_HEADS + N_BRANCH * D_MODEL
D_FF = 11 * D_MODEL // 4
N_EXPERTS = 8
TOP_K = 2
MOE_FF = D_FF // 2
N_DENSE = (DEPTH + 1) // 2
N_MOE = DEPTH // 2

kernel_name = 'hybrid_hgrn2_s5_gdn_diffusion_step'


def _rmsnorm(x, g):
    xf = x.astype(jnp.float32)
    y = xf * lax.rsqrt(jnp.mean(xf * xf, axis=-1, keepdims=True) + EPS)
    return y.astype(x.dtype) * g


def _heads(x, h):
    b, t, _ = x.shape
    return x.reshape(b, t, h, -1).transpose(0, 2, 1, 3)


def _merge_heads(x):
    b, h, t, d = x.shape
    return x.transpose(0, 2, 1, 3).reshape(b, t, h * d)


def _flip(x):
    return x[:, :, ::-1]


def _chunks(x):
    b, h, t = x.shape[:3]
    return jnp.moveaxis(x.reshape((b, h, t // CHUNK, CHUNK) + x.shape[3:]), 2, 0)


def _unchunk(x):
    n, b, h, c, d = x.shape
    return jnp.moveaxis(x, 0, 2).reshape(b, h, n * c, d)


def _to_column_major(h):
    b, t, d = h.shape
    rows = t // GRID_W
    return h.reshape(b, rows, GRID_W, d).transpose(0, 2, 1, 3).reshape(b, t, d)


def _to_row_major(h):
    b, t, d = h.shape
    rows = t // GRID_W
    return h.reshape(b, GRID_W, rows, d).transpose(0, 2, 1, 3).reshape(b, t, d)


def _masked_exp(mask, x):
    return jnp.where(mask, jnp.exp(jnp.where(mask, x, 0.0)), 0.0)


def _gla_chunked(q, k, v, log_f, s0):
    incl = jnp.tril(jnp.ones((CHUNK, CHUNK), bool))[:, :, None]

    def step(s, inp):
        q_i, k_i, v_i, lf_i = inp
        b = jnp.cumsum(lf_i, axis=2)
        rel = b[:, :, :, None, :] - b[:, :, None, :, :]
        decay = _masked_exp(incl, rel)
        att = jnp.einsum('bhtd,bhsd,bhtsd->bhts', q_i, k_i, decay)
        o = jnp.einsum('bhts,bhsv->bhtv', att, v_i) + jnp.einsum('bhtd,bhdv->bhtv', q_i * jnp.exp(b), s)
        b_last = b[:, :, -1:, :]
        s = jnp.exp(b_last[:, :, 0, :])[..., None] * s + jnp.einsum('bhsd,bhsv->bhdv', k_i * jnp.exp(b_last - b), v_i)
        return s, o

    s_fin, o = lax.scan(step, s0, (_chunks(q), _chunks(k), _chunks(v), _chunks(log_f)))
    return _unchunk(o), s_fin


def _gated_delta_chunked(q, k, v, beta, log_a, s0):
    incl = jnp.tril(jnp.ones((CHUNK, CHUNK), bool))
    strict = jnp.tril(jnp.ones((CHUNK, CHUNK), bool), -1)
    qc, kc, vc = _chunks(q), _chunks(k), _chunks(v)
    bc = _chunks(beta[..., None])[..., 0]
    g = jnp.cumsum(_chunks(log_a[..., None])[..., 0], axis=-1)
    gamma = _masked_exp(incl, g[..., :, None] - g[..., None, :])
    m = jnp.where(strict, bc[..., :, None] * jnp.einsum('nbhtd,nbhsd->nbhts', kc, kc) * gamma, 0.0)
    tmat = m + jnp.eye(CHUNK, dtype=m.dtype)
    w_v = lax.linalg.triangular_solve(tmat, bc[..., None] * vc, left_side=True, lower=True)
    w_k = lax.linalg.triangular_solve(tmat, (bc * jnp.exp(g))[..., None] * kc, left_side=True, lower=True)
    qk = jnp.einsum('nbhtd,nbhsd->nbhts', qc, kc) * gamma

    def step(s, inp):
        q_i, k_i, g_i, qk_i, wv_i, wk_i = inp
        v_new = wv_i - jnp.einsum('bhtd,bhdv->bhtv', wk_i, s)
        o = jnp.einsum('bhtd,bhdv->bhtv', q_i * jnp.exp(g_i)[..., None], s) + jnp.einsum('bhts,bhsv->bhtv', qk_i, v_new)
        g_last = g_i[..., -1:]
        s = jnp.exp(g_last)[..., None] * s + jnp.einsum('bhsd,bhsv->bhdv', k_i * jnp.exp(g_last - g_i)[..., None], v_new)
        return s, o

    s_fin, o = lax.scan(step, s0, (qc, kc, g, qk, w_v, w_k))
    return _unchunk(o), s_fin


def _hgrn2(q_in, f_fwd, f_bwd, i_in, g_in, lb, norm_g, s0):
    q = _heads(jax.nn.silu(q_in), A_HEADS).astype(jnp.float32) * A_DK ** -0.5
    v = _heads(i_in, A_HEADS).astype(jnp.float32)
    s0 = s0.astype(jnp.float32)
    outs, finals = [], []
    for d, z_in in enumerate((f_fwd, f_bwd)):
        z = _heads(z_in, A_HEADS).astype(jnp.float32)
        lbd = lb[d].reshape(A_HEADS, A_DK)[None, :, None, :]
        f = lbd + (1.0 - lbd) * jax.nn.sigmoid(z)
        log_f = jnp.log(jnp.maximum(f, F_FLOOR))
        k = (1.0 - lbd) * jax.nn.sigmoid(-z)
        if d == 0:
            o, s = _gla_chunked(q, k, v, log_f, s0[:, d])
        else:
            o, s = _gla_chunked(_flip(q), _flip(k), _flip(v), _flip(log_f), s0[:, d])
            o = _flip(o)
        outs.append(o)
        finals.append(s)
    o = _rmsnorm(outs[0] + outs[1], norm_g)
    y = _merge_heads(o) * jax.nn.silu(g_in.astype(jnp.float32))
    return y, jnp.stack(finals, axis=1)


def _complex_affine_combine(e1, e2):
    a1r, a1i, b1r, b1i = e1
    a2r, a2i, b2r, b2i = e2
    return (a2r * a1r - a2i * a1i, a2r * a1i + a2i * a1r,
            a2r * b1r - a2i * b1i + b2r, a2r * b1i + a2i * b1r + b2i)


def _s5(u, a_re, a_im, log_dt, b_re, b_im, c_re, c_im, d_skip, glu_w, glu_b, s0):
    bsz, t, w = u.shape
    uf = u.astype(jnp.float32)
    ug = uf.reshape(bsz, t, B_GROUPS, B_GROUP)
    a_re = a_re.astype(jnp.float32)
    a_im = a_im.astype(jnp.float32)
    dt = jnp.exp(log_dt.astype(jnp.float32))[..., None]
    mag = jnp.exp(dt * a_re)
    lam_re = mag * jnp.cos(dt * a_im)
    lam_im = mag * jnp.sin(dt * a_im)
    den = a_re * a_re + a_im * a_im
    co_re = ((lam_re - 1.0) * a_re + lam_im * a_im) / den
    co_im = (lam_im * a_re - (lam_re - 1.0) * a_im) / den
    bb_re = co_re[..., None] * b_re - co_im[..., None] * b_im
    bb_im = co_re[..., None] * b_im + co_im[..., None] * b_re
    s0 = s0.astype(jnp.float32)
    y = uf * d_skip
    finals = []
    for d in range(N_DIR):
        ud = ug[:, ::-1] if d == 1 else ug
        x_re = jnp.einsum('btgi,gpi->btgp', ud, bb_re[d])
        x_im = jnp.einsum('btgi,gpi->btgp', ud, bb_im[d])
        s_re, s_im = s0[:, d, ..., 0], s0[:, d, ..., 1]
        x_re = x_re.at[:, 0].add(lam_re[d] * s_re - lam_im[d] * s_im)
        x_im = x_im.at[:, 0].add(lam_re[d] * s_im + lam_im[d] * s_re)
        a_re_t = jnp.broadcast_to(lam_re[d], x_re.shape)
        a_im_t = jnp.broadcast_to(lam_im[d], x_im.shape)
        _, _, h_re, h_im = lax.associative_scan(_complex_affine_combine, (a_re_t, a_im_t, x_re, x_im), axis=1)
        yd = jnp.einsum('btgp,gip->btgi', h_re, c_re[d]) - jnp.einsum('btgp,gip->btgi', h_im, c_im[d])
        if d == 1:
            yd = yd[:, ::-1]
        y = y + yd.reshape(bsz, t, w)
        finals.append(jnp.stack([h_re[:, -1], h_im[:, -1]], axis=-1))
    z = y @ glu_w + glu_b
    val, gate = jnp.split(z, 2, axis=-1)
    return val * jax.nn.sigmoid(gate), jnp.stack(finals, axis=1)


def _dwconv(x, w):
    pad = CONV_W // 2
    return lax.conv_general_dilated(x, w[:, None, :], window_strides=(1,), padding=[(pad, pad)],
                                    dimension_numbers=('NWC', 'WIO', 'NWC'), feature_group_count=x.shape[-1])


def _l2norm(x):
    return x * lax.rsqrt(jnp.sum(x * x, axis=-1, keepdims=True) + EPS)


def _gdn(q_in, k_in, v_in, g_in, alpha_in, beta_in, conv_w, a_log, dt_bias, norm_g, s0):
    b, t, _ = q_in.shape
    q = _l2norm(_heads(jax.nn.silu(_dwconv(q_in, conv_w[0])), C_HEADS).astype(jnp.float32)) * C_DK ** -0.5
    k = _l2norm(_heads(jax.nn.silu(_dwconv(k_in, conv_w[1])), C_HEADS).astype(jnp.float32))
    v = _heads(jax.nn.silu(_dwconv(v_in, conv_w[2])), C_HEADS).astype(jnp.float32)
    alpha = alpha_in.astype(jnp.float32).reshape(b, t, N_DIR, C_HEADS).transpose(2, 0, 3, 1)
    beta = jax.nn.sigmoid(beta_in.astype(jnp.float32).reshape(b, t, N_DIR, C_HEADS).transpose(2, 0, 3, 1))
    log_a = -jnp.exp(a_log.astype(jnp.float32))[:, None, :, None] * jax.nn.softplus(alpha + dt_bias[:, None, :, None])
    s0 = s0.astype(jnp.float32)
    outs, finals = [], []
    for d in range(N_DIR):
        if d == 0:
            o, s = _gated_delta_chunked(q, k, v, beta[d], log_a[d], s0[:, d])
        else:
            o, s = _gated_delta_chunked(_flip(q), _flip(k), _flip(v), _flip(beta[d]), _flip(log_a[d]), s0[:, d])
            o = _flip(o)
        outs.append(o)
        finals.append(s)
    o = _rmsnorm(outs[0] + outs[1], norm_g)
    y = _merge_heads(o) * jax.nn.silu(g_in.astype(jnp.float32))
    return y, jnp.stack(finals, axis=1)


def _mixer(h, l, lb_l, s_hgrn0, s_s50, s_gdn0, P):
    b, t, _ = h.shape
    idx = np.cumsum(SPLITS)[:-1].tolist()
    (aq, af, ab, ai, ag, bu, cq, ck, cv, cg, c_alpha, c_beta, merge) = jnp.split(h @ P['w_in'][l], idx, axis=-1)
    ya, sa = _hgrn2(aq, af, ab, ai, ag, lb_l, P['hgrn_norm'][l], s_hgrn0)
    yb, sb = _s5(bu, P['s5_a_re'][l], P['s5_a_im'][l], P['s5_log_dt'][l], P['s5_b_re'][l], P['s5_b_im'][l],
                 P['s5_c_re'][l], P['s5_c_im'][l], P['s5_d'][l], P['s5_glu_w'][l], P['s5_glu_b'][l], s_s50)
    yc, sc = _gdn(cq, ck, cv, cg, c_alpha, c_beta, P['gdn_conv'][l], P['gdn_a_log'][l], P['gdn_dt_bias'][l],
                  P['gdn_norm'][l], s_gdn0)
    branches = jnp.stack([ya, yb, yc], axis=2).astype(h.dtype)
    proj = jnp.einsum('btnw,nwd->btnd', branches, P['w_branch'][l])
    gates = jax.nn.sigmoid(merge.reshape(b, t, N_BRANCH, D_MODEL))
    y = jnp.einsum('btnd,de->bte', proj * gates, P['w_out'][l])
    return y, (sa, sb, sc)


def _swiglu(h, wg, wu, wd):
    return (jax.nn.silu(h @ wg) * (h @ wu)) @ wd


def _moe(h, router, wg, wu, wd):
    logits = (h @ router).astype(jnp.float32)
    top_val, top_idx = lax.top_k(logits, TOP_K)
    top_w = jax.nn.softmax(top_val, axis=-1)
    gates = jnp.sum(jax.nn.one_hot(top_idx, N_EXPERTS, dtype=jnp.float32) * top_w[..., None], axis=-2).astype(h.dtype)
    hidden = jax.nn.silu(jnp.einsum('btd,edf->btef', h, wg)) * jnp.einsum('btd,edf->btef', h, wu)
    return jnp.einsum('btef,efd->btd', hidden * gates[..., None], wd)


def _layer(x, mod, l, column_major, lb_l, s_hgrn0, s_s50, s_gdn0, P):
    shift_m, scale_m, gate_m, shift_f, scale_f, gate_f = jnp.split(mod, 6, axis=-1)
    g = P['norm_g'][l]
    h = _rmsnorm(x, g[0]) * (1.0 + scale_m) + shift_m
    if column_major:
        h = _to_column_major(h)
    m, states = _mixer(h, l, lb_l, s_hgrn0, s_s50, s_gdn0, P)
    if column_major:
        m = _to_row_major(m)
    x = x + gate_m * _rmsnorm(m, g[1])
    h = _rmsnorm(x, g[2]) * (1.0 + scale_f) + shift_f
    if l % 2 == 0:
        f = _swiglu(h, P['ffn_wg'][l // 2], P['ffn_wu'][l // 2], P['ffn_wd'][l // 2])
    else:
        f = _moe(h, P['moe_router'][l // 2], P['moe_wg'][l // 2], P['moe_wu'][l // 2], P['moe_wd'][l // 2])
    x = x + gate_f * _rmsnorm(f, g[3])
    return x, states


def setup_inputs(seed: int = 0) -> dict:
    key = jax.random.key(seed)
    ks = jax.random.split(key, 40)
    f32 = jnp.float32

    def nrm(i, shape, scale):
        return scale * jax.random.normal(ks[i], shape, f32)

    s5_shape = (DEPTH, N_DIR, B_GROUPS, B_STATE)
    dt_gdn = jnp.exp(jax.random.uniform(ks[25], (DEPTH, N_DIR, C_HEADS), f32, math.log(DT_MIN), math.log(DT_MAX)))
    return {
        'x_prompt': nrm(0, (BATCH, SEQ, D_MODEL), 1.0),
        'x_sample': nrm(1, (DEC_BATCH, DEC_SEQ, D_MODEL), 1.0),
        'state_hgrn': nrm(2, (DEC_BATCH, DEPTH, N_DIR, A_HEADS, A_DK, A_DV), 0.5),
        'state_s5': nrm(3, (DEC_BATCH, DEPTH, N_DIR, B_GROUPS, B_STATE, 2), 0.1),
        'state_gdn': nrm(4, (DEC_BATCH, DEPTH, N_DIR, C_HEADS, C_DK, C_DV), 0.3),
        'c': nrm(5, (DEC_BATCH, D_MODEL), 1.0),
        'c_ctx': nrm(6, (D_MODEL,), 1.0),
        'ada_w': nrm(7, (DEPTH, D_MODEL, 6 * D_MODEL), 0.5 * D_MODEL ** -0.5),
        'ada_b': nrm(8, (DEPTH, 6 * D_MODEL), 0.02),
        'norm_g': 1.0 + nrm(9, (DEPTH, 4, D_MODEL), 0.02),
        'w_in': nrm(10, (DEPTH, D_MODEL, N_IN), D_MODEL ** -0.5),
        'hgrn_lb': nrm(11, (DEPTH, N_DIR, A_WIDTH), 0.5),
        'hgrn_norm': 1.0 + nrm(12, (DEPTH, A_DV), 0.02),
        's5_a_re': -0.5 + nrm(13, s5_shape, 0.01),
        's5_a_im': math.pi * jnp.arange(B_STATE, dtype=f32) + nrm(14, s5_shape, 0.01),
        's5_log_dt': jax.random.uniform(ks[15], (DEPTH, N_DIR, B_GROUPS), f32, math.log(DT_MIN), math.log(DT_MAX)),
        's5_b_re': nrm(16, (DEPTH, N_DIR, B_GROUPS, B_STATE, B_GROUP), (2 * B_GROUP) ** -0.5),
        's5_b_im': nrm(17, (DEPTH, N_DIR, B_GROUPS, B_STATE, B_GROUP), (2 * B_GROUP) ** -0.5),
        's5_c_re': nrm(18, (DEPTH, N_DIR, B_GROUPS, B_GROUP, B_STATE), (2 * B_STATE) ** -0.5),
        's5_c_im': nrm(19, (DEPTH, N_DIR, B_GROUPS, B_GROUP, B_STATE), (2 * B_STATE) ** -0.5),
        's5_d': nrm(20, (DEPTH, B_WIDTH), 1.0),
        's5_glu_w': nrm(21, (DEPTH, B_WIDTH, 2 * B_WIDTH), B_WIDTH ** -0.5),
        's5_glu_b': nrm(22, (DEPTH, 2 * B_WIDTH), 0.02),
        'gdn_conv': nrm(23, (DEPTH, 3, CONV_W, C_WIDTH), CONV_W ** -0.5),
        'gdn_a_log': jnp.log(jax.random.uniform(ks[24], (DEPTH, N_DIR, C_HEADS), f32, 1.0, 16.0)),
        'gdn_dt_bias': dt_gdn + jnp.log(-jnp.expm1(-dt_gdn)),
        'gdn_norm': 1.0 + nrm(26, (DEPTH, C_DV), 0.02),
        'w_branch': nrm(27, (DEPTH, N_BRANCH, BR_WIDTH, D_MODEL), BR_WIDTH ** -0.5),
        'w_out': nrm(28, (DEPTH, D_MODEL, D_MODEL), D_MODEL ** -0.5),
        'ffn_wg': nrm(29, (N_DENSE, D_MODEL, D_FF), D_MODEL ** -0.5),
        'ffn_wu': nrm(30, (N_DENSE, D_MODEL, D_FF), D_MODEL ** -0.5),
        'ffn_wd': nrm(31, (N_DENSE, D_FF, D_MODEL), D_FF ** -0.5),
        'moe_router': nrm(32, (N_MOE, D_MODEL, N_EXPERTS), D_MODEL ** -0.5),
        'moe_wg': nrm(33, (N_MOE, N_EXPERTS, D_MODEL, MOE_FF), D_MODEL ** -0.5),
        'moe_wu': nrm(34, (N_MOE, N_EXPERTS, D_MODEL, MOE_FF), D_MODEL ** -0.5),
        'moe_wd': nrm(35, (N_MOE, N_EXPERTS, MOE_FF, D_MODEL), MOE_FF ** -0.5),
    }


def reference(x_prompt, x_sample, state_hgrn, state_s5, state_gdn, c, c_ctx, ada_w, ada_b, norm_g, w_in,
              hgrn_lb, hgrn_norm, s5_a_re, s5_a_im, s5_log_dt, s5_b_re, s5_b_im, s5_c_re, s5_c_im, s5_d,
              s5_glu_w, s5_glu_b, gdn_conv, gdn_a_log, gdn_dt_bias, gdn_norm, w_branch, w_out,
              ffn_wg, ffn_wu, ffn_wd, moe_router, moe_wg, moe_wu, moe_wd):
    P = dict(ada_w=ada_w, ada_b=ada_b, norm_g=norm_g, w_in=w_in, hgrn_norm=hgrn_norm,
             s5_a_re=s5_a_re, s5_a_im=s5_a_im, s5_log_dt=s5_log_dt, s5_b_re=s5_b_re, s5_b_im=s5_b_im,
             s5_c_re=s5_c_re, s5_c_im=s5_c_im, s5_d=s5_d, s5_glu_w=s5_glu_w, s5_glu_b=s5_glu_b,
             gdn_conv=gdn_conv, gdn_a_log=gdn_a_log, gdn_dt_bias=gdn_dt_bias, gdn_norm=gdn_norm,
             w_branch=w_branch, w_out=w_out, ffn_wg=ffn_wg, ffn_wu=ffn_wu, ffn_wd=ffn_wd,
             moe_router=moe_router, moe_wg=moe_wg, moe_wu=moe_wu, moe_wd=moe_wd)
    lb_soft = jax.nn.softmax(hgrn_lb.astype(jnp.float32), axis=0)
    lb_all = jnp.cumsum(lb_soft, axis=0) - lb_soft[:1]

    bp = x_prompt.shape[0]
    z_hgrn = jnp.zeros((bp, N_DIR, A_HEADS, A_DK, A_DV), jnp.float32)
    z_s5 = jnp.zeros((bp, N_DIR, B_GROUPS, B_STATE, 2), jnp.float32)
    z_gdn = jnp.zeros((bp, N_DIR, C_HEADS, C_DK, C_DV), jnp.float32)
    xp = x_prompt
    st_hgrn, st_s5, st_gdn = [], [], []
    for l in range(DEPTH):
        mod = (jax.nn.silu(c_ctx) @ ada_w[l] + ada_b[l])[None, None, :]
        xp, (sa, sb, sc) = _layer(xp, mod, l, False, lb_all[l], z_hgrn, z_s5, z_gdn, P)
        st_hgrn.append(sa)
        st_s5.append(sb)
        st_gdn.append(sc)
    new_state_hgrn = jnp.stack(st_hgrn, axis=1).astype(x_prompt.dtype)
    new_state_s5 = jnp.stack(st_s5, axis=1).astype(x_prompt.dtype)
    new_state_gdn = jnp.stack(st_gdn, axis=1).astype(x_prompt.dtype)

    xs = x_sample
    for l in range(DEPTH):
        mod = (jax.nn.silu(c) @ ada_w[l] + ada_b[l])[:, None, :]
        xs, _ = _layer(xs, mod, l, l % 2 == 1, lb_all[l], state_hgrn[:, l], state_s5[:, l], state_gdn[:, l], P)

    return (xp, xs, new_state_hgrn, new_state_s5, new_state_gdn)
```

```python
import functools
import math

import jax
import jax.numpy as jnp
from jax import lax
from jax.experimental import pallas as pl
from jax.experimental.pallas import tpu as pltpu

F32 = jnp.float32
BF16 = jnp.bfloat16

EPS = 1e-6
F_FLOOR = 1e-6
LANES = 128
GRID_W = 64
HEAD_D = 128
N_HEADS = 4
BR_WIDTH = 512
S5_GROUPS = 32
S5_GROUP = 16
S5_STATE = 64
S5_GB = 8
S5_SEG = 64
GLA_CHUNK = 256
GDN_CHUNK = 64
TM = 512
VMEM_LIMIT = 56 * 1024 * 1024


def _cp(sem, vmem=VMEM_LIMIT):
    return pltpu.CompilerParams(dimension_semantics=sem, vmem_limit_bytes=vmem)


def _dot(a, b):
    return jnp.dot(a, b, preferred_element_type=F32)


def _dot_nt(a, b):
    return lax.dot_general(a, b, (((1,), (1,)), ((), ())), preferred_element_type=F32)


def _dot_tn(a, b):
    return lax.dot_general(a, b, (((0,), (0,)), ((), ())), preferred_element_type=F32)


def _silu(x):
    return x * jax.nn.sigmoid(x)


def _rms(x, g):
    return x * lax.rsqrt(jnp.mean(x * x, axis=-1, keepdims=True) + EPS) * g


def _exact_mm(mat, x):
    hi = x.astype(BF16)
    r1 = x - hi.astype(F32)
    mid = r1.astype(BF16)
    lo = (r1 - mid.astype(F32)).astype(BF16)
    return _dot(mat, hi) + _dot(mat, mid) + _dot(mat, lo)


def _exact_mm_nt(mat, x):
    hi = x.astype(BF16)
    r1 = x - hi.astype(F32)
    mid = r1.astype(BF16)
    lo = (r1 - mid.astype(F32)).astype(BF16)
    return _dot_nt(mat, hi) + _dot_nt(mat, mid) + _dot_nt(mat, lo)


def _ada_kernel(c_ref, w_ref, b_ref, o_ref):
    c = c_ref[...]
    a = _silu(c).astype(BF16)
    o_ref[...] = _dot(a, w_ref[...].astype(BF16)) + b_ref[...]


def _ada_mod(c8, ada_w, ada_b):
    depth, d, n6 = ada_w.shape
    tn = 512
    return pl.pallas_call(
        _ada_kernel,
        grid=(depth, n6 // tn),
        in_specs=[
            pl.BlockSpec((8, d), lambda l, j: (0, 0)),
            pl.BlockSpec((None, d, tn), lambda l, j: (l, 0, j)),
            pl.BlockSpec((None, 1, tn), lambda l, j: (l, 0, j)),
        ],
        out_specs=pl.BlockSpec((None, 8, tn), lambda l, j: (l, 0, j)),
        out_shape=jax.ShapeDtypeStruct((depth, 8, n6), F32),
        compiler_params=_cp(("parallel", "parallel")),
        name="ada_mod",
    )(c8, ada_w, ada_b.reshape(depth, 1, n6))


def _resnorm_kernel(*refs, has_y, has_h):
    it = iter(refs)
    x_ref = next(it)
    y_ref = next(it) if has_y else None
    mv_ref = next(it)
    xo_ref = next(it) if has_y else None
    h_ref = next(it) if has_h else None
    x = x_ref[...]
    mv = mv_ref[...]
    if has_y:
        x = x + mv[0:1] * y_ref[...]
        xo_ref[...] = x
    if has_h:
        h = _rms(x, mv[3:4]) * (1.0 + mv[1:2]) + mv[2:3]
        h_ref[...] = h.astype(BF16)


def _resnorm(x, y, modv, n_prompt, seq_s, has_h):
    n, d = x.shape
    has_y = y is not None

    def group(i):
        return jnp.where(i * TM < n_prompt, 0, 1 + (i * TM - n_prompt) // seq_s)

    tile = pl.BlockSpec((TM, d), lambda i: (i, 0))
    in_specs = [tile] + ([tile] if has_y else []) + [pl.BlockSpec((None, 8, d), lambda i: (group(i), 0, 0))]
    out_specs, out_shape = [], []
    if has_y:
        out_specs.append(tile)
        out_shape.append(jax.ShapeDtypeStruct((n, d), F32))
    if has_h:
        out_specs.append(tile)
        out_shape.append(jax.ShapeDtypeStruct((n, d), BF16))
    args = [x] + ([y] if has_y else []) + [modv]
    outs = pl.pallas_call(
        functools.partial(_resnorm_kernel, has_y=has_y, has_h=has_h),
        grid=(n // TM,),
        in_specs=in_specs,
        out_specs=out_specs,
        out_shape=out_shape,
        compiler_params=_cp(("parallel",)),
        name="resnorm",
    )(*args)
    return outs


def _mm_kernel(x_ref, w_ref, o_ref, wb_ref):
    @pl.when(pl.program_id(1) == 0)
    def _():
        wb_ref[...] = w_ref[...].astype(BF16)

    o_ref[...] = _dot(x_ref[...], wb_ref[...])


def _mm(x, w, l, col0, ncols, tn, name):
    m, k = x.shape
    j0 = col0 // tn
    if l is None:
        wspec = pl.BlockSpec((k, tn), lambda j, i: (0, j + j0))
    else:
        wspec = pl.BlockSpec((None, k, tn), lambda j, i: (l, 0, j + j0))
    return pl.pallas_call(
        _mm_kernel,
        grid=(ncols // tn, m // TM),
        in_specs=[pl.BlockSpec((TM, k), lambda j, i: (i, 0)), wspec],
        out_specs=pl.BlockSpec((TM, tn), lambda j, i: (i, j)),
        out_shape=jax.ShapeDtypeStruct((m, ncols), F32),
        scratch_shapes=[pltpu.VMEM((k, tn), BF16)],
        compiler_params=_cp(("arbitrary", "arbitrary")),
        name=name,
    )(x, w)


def _gla_intra(q, k, b, d, c):
    rowi = lax.broadcasted_iota(jnp.int32, (c, LANES), 0)
    ri = lax.broadcasted_iota(jnp.int32, (c, c), 0)
    ci = lax.broadcasted_iota(jnp.int32, (c, c), 1)
    a = jnp.zeros((c, c), F32)
    size = c
    while size >= 8:
        half = size // 2
        nb = c // size
        b3 = b.reshape(nb, size, LANES)
        pos = rowi & (size - 1)
        if d == 0:
            ref = b3[:, half - 1:half, :]
            qsel = pos >= half
        else:
            ref = b3[:, half:half + 1, :]
            qsel = pos < half
        ref = jnp.broadcast_to(ref, (nb, size, LANES)).reshape(c, LANES)
        qs = jnp.where(qsel, q * jnp.exp(jnp.minimum(b - ref, 0.0)), 0.0).astype(BF16)
        ks = jnp.where(qsel, 0.0, k * jnp.exp(jnp.minimum(ref - b, 0.0))).astype(BF16)
        p = _dot_nt(qs, ks)
        if nb > 1:
            shift = int(math.log2(size))
            p = jnp.where((ri >> shift) == (ci >> shift), p, 0.0)
        a = a + p
        size = half
    b3 = b.reshape(c // 8, 8, LANES)
    lo, hi = (0, 4) if d == 0 else (3, 7)
    ref_lo = jnp.broadcast_to(b3[:, lo:lo + 1, :], (c // 8, 8, LANES)).reshape(c, LANES)
    ref_hi = jnp.broadcast_to(b3[:, hi:hi + 1, :], (c // 8, 8, LANES)).reshape(c, LANES)
    ref = jnp.where((rowi & 7) >= 4, ref_hi, ref_lo)
    qs = (q * jnp.exp(jnp.minimum(b - ref, 0.0))).astype(BF16)
    ks = (k * jnp.exp(ref - b)).astype(BF16)
    p = _dot_nt(qs, ks)
    keep = ((ri >> 2) == (ci >> 2)) & ((ci <= ri) if d == 0 else (ci >= ri))
    return a + jnp.where(keep, p, 0.0)


def _hgrn_kernel(aq_ref, af_ref, ab_ref, ai_ref, ag_ref, lb_ref, g_ref, s0_ref, y_ref, s_ref, o_acc, *, t, c):
    nchunk = t // c
    ri = lax.broadcasted_iota(jnp.int32, (c, c), 0)
    ci = lax.broadcasted_iota(jnp.int32, (c, c), 1)
    scale = HEAD_D ** -0.5
    for d in (0, 1):
        z_ref = af_ref if d == 0 else ab_ref
        lbd = lb_ref[d:d + 1, :]
        tri = jnp.where((ci <= ri) if d == 0 else (ci >= ri), 1.0, 0.0).astype(BF16)

        def chunk(i, st, d=d, z_ref=z_ref, lbd=lbd, tri=tri):
            cidx = i if d == 0 else nchunk - 1 - i
            r0 = pl.multiple_of(cidx * c, c)
            z = z_ref[pl.ds(r0, c), :]
            f = lbd + (1.0 - lbd) * jax.nn.sigmoid(z)
            lf = jnp.log(jnp.maximum(f, F_FLOOR))
            k = (1.0 - lbd) * jax.nn.sigmoid(-z)
            q = _silu(aq_ref[pl.ds(r0, c), :]) * scale
            v = ai_ref[pl.ds(r0, c), :].astype(BF16)
            b = _exact_mm(tri, lf)
            a = _gla_intra(q, k, b, d, c)
            o = _dot(a.astype(BF16), v) + _dot_nt((q * jnp.exp(b)).astype(BF16), st.astype(BF16))
            if d == 0:
                o_acc[pl.ds(r0, c), :] = o
            else:
                o_acc[pl.ds(r0, c), :] += o
            blast = b[c - 1:c, :] if d == 0 else b[0:1, :]
            kd = (k * jnp.exp(blast - b)).astype(BF16)
            return st * jnp.exp(blast) + _dot_tn(v, kd)

        st = s0_ref[0, d, 0].T
        if nchunk == 1:
            st = chunk(0, st)
        else:
            st = lax.fori_loop(0, nchunk, chunk, st)
        s_ref[0, d, 0] = st.T
    o = o_acc[...]
    y_ref[...] = _rms(o, g_ref[...]) * _silu(ag_ref[...])


def _hgrn(p_main, lb, norm_g, s0, row_blk0, nseq, t):
    c = min(GLA_CHUNK, t)

    def col(base):
        return pl.BlockSpec((t, HEAD_D), lambda s, h: (row_blk0 + s, base + h))

    st_spec = pl.BlockSpec((1, 2, 1, HEAD_D, HEAD_D), lambda s, h: (s, 0, h, 0, 0))
    return pl.pallas_call(
        functools.partial(_hgrn_kernel, t=t, c=c),
        grid=(nseq, N_HEADS),
        in_specs=[col(0), col(4), col(8), col(12), col(16),
                  pl.BlockSpec((2, HEAD_D), lambda s, h: (0, h)),
                  pl.BlockSpec((1, HEAD_D), lambda s, h: (0, 0)),
                  st_spec],
        out_specs=[pl.BlockSpec((t, HEAD_D), lambda s, h: (s, h)), st_spec],
        out_shape=[jax.ShapeDtypeStruct((nseq * t, BR_WIDTH), F32),
                   jax.ShapeDtypeStruct((nseq, 2, N_HEADS, HEAD_D, HEAD_D), F32)],
        scratch_shapes=[pltpu.VMEM((t, HEAD_D), F32)],
        compiler_params=_cp(("parallel", "parallel")),
        name="hgrn2",
    )(p_main, p_main, p_main, p_main, p_main, lb, norm_g.reshape(1, HEAD_D), s0)


def _dot_hp(a, b):
    ah = a.astype(BF16)
    al = (a - ah.astype(F32)).astype(BF16)
    bh = b.astype(BF16)
    bl = (b - bh.astype(F32)).astype(BF16)
    return _dot(jnp.concatenate([ah, al, ah, al], axis=1), jnp.concatenate([bh, bh, bl, bl], axis=0))


def _neumann_apply(m, x, c):
    w = x - _dot_hp(m, x)
    p = m
    n = 2
    while n < c:
        p = _dot_hp(p, p)
        w = w + _dot_hp(p, w)
        n *= 2
    return w


def _gdn_kernel(cq_ref, ck_ref, cv_ref, cg_ref, ab_ref, cw_ref, alog_ref, dtb_ref, g_ref, s0_ref,
                y_ref, s_ref, q_s, k_s, v_s, la_s, be_s, o_acc, *, t, c):
    h = pl.program_id(1)
    nchunk = t // c
    rowt = lax.broadcasted_iota(jnp.int32, (t, HEAD_D), 0)
    lane = lax.broadcasted_iota(jnp.int32, (t, LANES), 1)
    ri = lax.broadcasted_iota(jnp.int32, (c, c), 0)
    ci = lax.broadcasted_iota(jnp.int32, (c, c), 1)
    lane0 = jnp.where(lax.broadcasted_iota(jnp.int32, (c, LANES), 1) == 0, 1.0, 0.0).astype(BF16)

    def conv(x_ref, j):
        x = x_ref[...]
        w = cw_ref[j]
        xm = jnp.where(rowt == 0, 0.0, pltpu.roll(x, 1, 0))
        xp = jnp.where(rowt == t - 1, 0.0, pltpu.roll(x, t - 1, 0))
        return xm * w[0:1] + x * w[1:2] + xp * w[2:3]

    q = _silu(conv(cq_ref, 0))
    q_s[...] = q * lax.rsqrt(jnp.sum(q * q, axis=-1, keepdims=True) + EPS) * (HEAD_D ** -0.5)
    k = _silu(conv(ck_ref, 1))
    k_s[...] = k * lax.rsqrt(jnp.sum(k * k, axis=-1, keepdims=True) + EPS)
    v_s[...] = _silu(conv(cv_ref, 2))
    ab = ab_ref[...]

    for d in (0, 1):
        a_col = jnp.sum(jnp.where(lane == d * N_HEADS + h, ab, 0.0), axis=-1, keepdims=True)
        b_col = jnp.sum(jnp.where(lane == 2 * N_HEADS + d * N_HEADS + h, ab, 0.0), axis=-1, keepdims=True)
        coef = -jnp.exp(jnp.full((1, LANES), alog_ref[d, h], F32))
        xs = a_col + dtb_ref[d, h]
        softplus = jnp.maximum(xs, 0.0) + jnp.log1p(jnp.exp(-jnp.abs(xs)))
        la_s[...] = coef * jnp.broadcast_to(softplus, (t, LANES))
        be_s[...] = jnp.broadcast_to(jax.nn.sigmoid(b_col), (t, LANES))
        incl = (ci <= ri) if d == 0 else (ci >= ri)
        strict = (ci < ri) if d == 0 else (ci > ri)
        tri = jnp.where(incl, 1.0, 0.0).astype(BF16)

        def chunk(i, s, d=d, incl=incl, strict=strict, tri=tri):
            cidx = i if d == 0 else nchunk - 1 - i
            r0 = pl.multiple_of(cidx * c, c)
            qc = q_s[pl.ds(r0, c), :]
            kc = k_s[pl.ds(r0, c), :]
            vc = v_s[pl.ds(r0, c), :]
            be = be_s[pl.ds(r0, c), :]
            g = _exact_mm(tri, la_s[pl.ds(r0, c), :])
            gcol = g[:, :c]
            grow = _exact_mm_nt(lane0, g)
            gamma = jnp.where(incl, jnp.exp(jnp.minimum(gcol - grow, 0.0)), 0.0)
            kb = kc.astype(BF16)
            m = jnp.where(strict, be[:, :c] * _dot_nt(kb, kb) * gamma, 0.0)
            eg = jnp.exp(g)
            x = jnp.concatenate([be * vc, be * eg * kc], axis=1)
            w = _neumann_apply(m, x, c)
            qk = (_dot_nt(qc.astype(BF16), kb) * gamma).astype(BF16)
            sb = s.astype(BF16)
            vnew = w[:, :HEAD_D] - _dot(w[:, HEAD_D:].astype(BF16), sb)
            vnb = vnew.astype(BF16)
            o = _dot((qc * eg).astype(BF16), sb) + _dot(qk, vnb)
            if d == 0:
                o_acc[pl.ds(r0, c), :] = o
            else:
                o_acc[pl.ds(r0, c), :] += o
            glast = g[c - 1:c, :] if d == 0 else g[0:1, :]
            kd = (kc * jnp.exp(glast - g)).astype(BF16)
            return s * jnp.exp(glast) + _dot_tn(kd, vnb)

        s_ref[0, d, 0] = lax.fori_loop(0, nchunk, chunk, s0_ref[0, d, 0])

    y_ref[...] = _rms(o_acc[...], g_ref[...]) * _silu(cg_ref[...])


def _gdn(p_main, p_ab, conv_w, a_log, dt_bias, norm_g, s0, row_blk0, nseq, t):
    c = GDN_CHUNK

    def col(base):
        return pl.BlockSpec((t, HEAD_D), lambda s, h: (row_blk0 + s, base + h))

    st_spec = pl.BlockSpec((1, 2, 1, HEAD_D, HEAD_D), lambda s, h: (s, 0, h, 0, 0))
    smem = pl.BlockSpec(memory_space=pltpu.SMEM)
    seq_buf = pltpu.VMEM((t, HEAD_D), F32)
    return pl.pallas_call(
        functools.partial(_gdn_kernel, t=t, c=c),
        grid=(nseq, N_HEADS),
        in_specs=[col(24), col(28), col(32), col(36),
                  pl.BlockSpec((t, LANES), lambda s, h: (row_blk0 + s, 0)),
                  pl.BlockSpec((3, 3, HEAD_D), lambda s, h: (0, 0, h)),
                  smem, smem,
                  pl.BlockSpec((1, HEAD_D), lambda s, h: (0, 0)),
                  st_spec],
        out_specs=[pl.BlockSpec((t, HEAD_D), lambda s, h: (s, h)), st_spec],
        out_shape=[jax.ShapeDtypeStruct((nseq * t, BR_WIDTH), F32),
                   jax.ShapeDtypeStruct((nseq, 2, N_HEADS, HEAD_D, HEAD_D), F32)],
        scratch_shapes=[seq_buf] * 6,
        compiler_params=_cp(("parallel", "parallel")),
        name="gdn",
    )(p_main, p_main, p_main, p_main, p_ab, conv_w, a_log, dt_bias, norm_g.reshape(1, HEAD_D), s0)


def _cmul(ar, ai, br, bi):
    return ar * br - ai * bi, ar * bi + ai * br


def _s5_kernel(u_ref, bb_ref, cc_ref, lam_ref, dsk_ref, s0_ref, y_ref, sf_ref, up_s, x_s, yp_s,
               *, tb, seg, segs_per_seq):
    r_par = tb // seg
    half = S5_GB * S5_STATE
    for r in range(seg):
        up_s[r * r_par:(r + 1) * r_par, :] = u_ref[pl.ds(r, r_par, stride=seg), :]
    ub = up_s[...].astype(BF16)
    sf_ref[...] = jnp.zeros(sf_ref.shape, F32)
    nseq = r_par // segs_per_seq
    for d in (0, 1):
        x_s[...] = _dot(ub, bb_ref[0, d])
        lre = lam_ref[0, d, 0:1, :]
        lim = lam_ref[0, d, 1:2, :]

        def local_step(i, carry, d=d, lre=lre, lim=lim):
            hre, him = carry
            r = i if d == 0 else seg - 1 - i
            off = pl.multiple_of(r * r_par, r_par)
            nre = lre * hre - lim * him + x_s[pl.ds(off, r_par), 0:half]
            nim = lre * him + lim * hre + x_s[pl.ds(off, r_par), half:2 * half]
            x_s[pl.ds(off, r_par), 0:half] = nre
            x_s[pl.ds(off, r_par), half:2 * half] = nim
            return nre, nim

        zero = jnp.zeros((r_par, half), F32)
        lax.fori_loop(0, seg, local_step, (zero, zero))

        pre, pim = lre, lim
        n = 1
        while n < seg:
            pre, pim = _cmul(pre, pim, pre, pim)
            n *= 2
        end_row = (seg - 1) * r_par if d == 0 else 0
        ere = x_s[end_row:end_row + r_par, 0:half]
        eim = x_s[end_row:end_row + r_par, half:2 * half]
        hin = [None] * r_par
        order = range(r_par) if d == 0 else range(r_par - 1, -1, -1)
        for j in order:
            sq, pos = divmod(j, segs_per_seq)
            first = pos == (0 if d == 0 else segs_per_seq - 1)
            if first:
                hin[j] = (s0_ref[0, 0, d, sq:sq + 1, 0:half], s0_ref[0, 0, d, sq:sq + 1, half:2 * half])
            else:
                pj = j - 1 if d == 0 else j + 1
                cre, cim = _cmul(pre, pim, hin[pj][0], hin[pj][1])
                hin[j] = (cre + ere[pj:pj + 1], cim + eim[pj:pj + 1])
        hre_in = jnp.concatenate([hj[0] for hj in hin], axis=0)
        him_in = jnp.concatenate([hj[1] for hj in hin], axis=0)

        def carry_step(i, pw, d=d, lre=lre, lim=lim, hre_in=hre_in, him_in=him_in):
            wre, wim = pw
            r = i if d == 0 else seg - 1 - i
            off = pl.multiple_of(r * r_par, r_par)
            are, aim = _cmul(wre, wim, hre_in, him_in)
            x_s[pl.ds(off, r_par), 0:half] += are
            x_s[pl.ds(off, r_par), half:2 * half] += aim
            return _cmul(wre, wim, lre, lim)

        lax.fori_loop(0, seg, carry_step, (lre, lim))

        for sq in range(nseq):
            j = (sq + 1) * segs_per_seq - 1 if d == 0 else sq * segs_per_seq
            row = end_row + j
            sf_ref[0, 0, d, sq:sq + 1, :] = x_s[row:row + 1, :]
        yp = _dot(x_s[...].astype(BF16), cc_ref[0, d])
        if d == 0:
            yp_s[...] = yp + up_s[...] * dsk_ref[...]
        else:
            yp_s[...] += yp
    for r in range(seg):
        y_ref[pl.ds(r, r_par, stride=seg), :] = yp_s[r * r_par:(r + 1) * r_par, :]


def _s5(p_main, bb, cc, lam, dskip, s0, row_blk0, nblk, tb, t):
    ngb = S5_GROUPS // S5_GB
    w2 = 2 * S5_GB * S5_STATE
    seg = S5_SEG
    return pl.pallas_call(
        functools.partial(_s5_kernel, tb=tb, seg=seg, segs_per_seq=t // seg),
        grid=(nblk, ngb),
        in_specs=[
            pl.BlockSpec((tb, LANES), lambda i, gb: (row_blk0 + i, 20 + gb)),
            pl.BlockSpec((1, 2, LANES, w2), lambda i, gb: (gb, 0, 0, 0)),
            pl.BlockSpec((1, 2, w2, LANES), lambda i, gb: (gb, 0, 0, 0)),
            pl.BlockSpec((1, 2, 2, w2 // 2), lambda i, gb: (gb, 0, 0, 0)),
            pl.BlockSpec((1, LANES), lambda i, gb: (0, gb)),
            pl.BlockSpec((1, 1, 2, 8, w2), lambda i, gb: (i, gb, 0, 0, 0)),
        ],
        out_specs=[pl.BlockSpec((tb, LANES), lambda i, gb: (i, gb)),
                   pl.BlockSpec((1, 1, 2, 8, w2), lambda i, gb: (i, gb, 0, 0, 0))],
        out_shape=[jax.ShapeDtypeStruct((nblk * tb, BR_WIDTH), F32),
                   jax.ShapeDtypeStruct((nblk, ngb, 2, 8, w2), F32)],
        scratch_shapes=[pltpu.VMEM((tb, LANES), F32), pltpu.VMEM((tb, w2), F32), pltpu.VMEM((tb, LANES), F32)],
        compiler_params=_cp(("parallel", "parallel")),
        name="s5",
    )(p_main, bb, cc, lam, dskip.reshape(1, BR_WIDTH), s0)


def _s5_params(a_re, a_im, log_dt, b_re, b_im, c_re, c_im):
    dt = jnp.exp(log_dt)[..., None]
    mag = jnp.exp(dt * a_re)
    lam_re = mag * jnp.cos(dt * a_im)
    lam_im = mag * jnp.sin(dt * a_im)
    den = a_re * a_re + a_im * a_im
    co_re = ((lam_re - 1.0) * a_re + lam_im * a_im) / den
    co_im = (lam_im * a_re - (lam_re - 1.0) * a_im) / den
    bb_re = co_re[..., None] * b_re - co_im[..., None] * b_im
    bb_im = co_re[..., None] * b_im + co_im[..., None] * b_re
    ngb = S5_GROUPS // S5_GB
    eye = jnp.eye(S5_GB, dtype=F32)

    def pack_in(w):
        w = w.reshape(2, ngb, S5_GB, S5_STATE, S5_GROUP)
        blk = jnp.einsum('dbgpi,gh->bdgihp', w, eye)
        return blk.reshape(ngb, 2, S5_GB * S5_GROUP, S5_GB * S5_STATE)

    def pack_out(w):
        w = w.reshape(2, ngb, S5_GB, S5_GROUP, S5_STATE)
        blk = jnp.einsum('dbgip,gh->bdgphi', w, eye)
        return blk.reshape(ngb, 2, S5_GB * S5_STATE, S5_GB * S5_GROUP)

    bb = jnp.concatenate([pack_in(bb_re), pack_in(bb_im)], axis=-1).astype(BF16)
    cc = jnp.concatenate([pack_out(c_re), -pack_out(c_im)], axis=-2).astype(BF16)
    lam = jnp.stack([lam_re, lam_im], axis=1)
    lam = lam.reshape(2, 2, ngb, S5_GB * S5_STATE).transpose(2, 0, 1, 3)
    return bb, cc, lam


def _glu_kernel(y_ref, w_ref, b_ref, o_ref, wb_ref):
    @pl.when(pl.program_id(0) == 0)
    def _():
        wb_ref[...] = w_ref[...].astype(BF16)

    z = _dot(y_ref[...].astype(BF16), wb_ref[...]) + b_ref[...]
    half = z.shape[1] // 2
    o_ref[...] = z[:, :half] * jax.nn.sigmoid(z[:, half:])


def _glu(y, w, b, l):
    n, k = y.shape
    n2 = w.shape[-1]
    return pl.pallas_call(
        _glu_kernel,
        grid=(n // TM,),
        in_specs=[pl.BlockSpec((TM, k), lambda i: (i, 0)),
                  pl.BlockSpec((None, k, n2), lambda i: (l, 0, 0)),
                  pl.BlockSpec((None, 1, n2), lambda i: (l, 0, 0))],
        out_specs=pl.BlockSpec((TM, n2 // 2), lambda i: (i, 0)),
        out_shape=jax.ShapeDtypeStruct((n, n2 // 2), F32),
        scratch_shapes=[pltpu.VMEM((k, n2), BF16)],
        compiler_params=_cp(("arbitrary",)),
        name="s5_glu",
    )(y, w, b.reshape(b.shape[0], 1, n2))


def _merge_kernel(ya_ref, yb_ref, yc_ref, mg_ref, wbr_ref, wout_ref, g_ref, o_ref, wbr_s, wout_s):
    @pl.when(pl.program_id(0) == 0)
    def _():
        wbr_s[...] = wbr_ref[...].astype(BF16)
        wout_s[...] = wout_ref[...].astype(BF16)

    d = wout_s.shape[0]
    acc = None
    for n, br_ref in enumerate((ya_ref, yb_ref, yc_ref)):
        proj = _dot(br_ref[...].astype(BF16), wbr_s[n])
        term = proj * jax.nn.sigmoid(mg_ref[:, n * d:(n + 1) * d])
        acc = term if acc is None else acc + term
    y = _dot(acc.astype(BF16), wout_s[...])
    o_ref[...] = _rms(y, g_ref[...])


def _merge(ya, yb, yc, mg, w_branch, w_out, g, l):
    n = ya.shape[0]
    d = w_out.shape[-1]
    tm = 256
    br = pl.BlockSpec((tm, BR_WIDTH), lambda i: (i, 0))
    return pl.pallas_call(
        _merge_kernel,
        grid=(n // tm,),
        in_specs=[br, br, br,
                  pl.BlockSpec((tm, 3 * d), lambda i: (i, 0)),
                  pl.BlockSpec((None, 3, BR_WIDTH, d), lambda i: (l, 0, 0, 0)),
                  pl.BlockSpec((None, d, d), lambda i: (l, 0, 0)),
                  pl.BlockSpec((1, d), lambda i: (0, 0))],
        out_specs=pl.BlockSpec((tm, d), lambda i: (i, 0)),
        out_shape=jax.ShapeDtypeStruct((n, d), F32),
        scratch_shapes=[pltpu.VMEM((3, BR_WIDTH, d), BF16), pltpu.VMEM((d, d), BF16)],
        compiler_params=_cp(("arbitrary",)),
        name="merge",
    )(ya, yb, yc, mg, w_branch, w_out, g.reshape(1, d))


def _swiglu_kernel(*refs, gated):
    if gated:
        x_ref, wg_ref, wu_ref, gt_ref, o_ref, wg_s, wu_s = refs
    else:
        x_ref, wg_ref, wu_ref, o_ref, wg_s, wu_s = refs

    @pl.when(pl.program_id(1) == 0)
    def _():
        wg_s[...] = wg_ref[...].astype(BF16)
        wu_s[...] = wu_ref[...].astype(BF16)

    x = x_ref[...]
    hid = _silu(_dot(x, wg_s[...])) * _dot(x, wu_s[...])
    if gated:
        gt = gt_ref[...]
        lane = lax.broadcasted_iota(jnp.int32, gt.shape, 1)
        hid = hid * jnp.sum(jnp.where(lane == pl.program_id(0), gt, 0.0), axis=-1, keepdims=True)
    o_ref[...] = hid.astype(BF16)


def _swiglu_up(x, wg, wu, e, f, gates=None):
    n, d = x.shape
    gated = gates is not None
    if gated:
        ne = wg.shape[1]
        wspec = pl.BlockSpec((None, None, d, f), lambda j, i: (e, j, 0, 0))
    else:
        ne = wg.shape[-1] // f
        wspec = pl.BlockSpec((None, d, f), lambda j, i: (e, 0, j))
    in_specs = [pl.BlockSpec((TM, d), lambda j, i: (i, 0)), wspec, wspec]
    args = [x, wg, wu]
    if gated:
        in_specs.append(pl.BlockSpec((TM, LANES), lambda j, i: (i, 0)))
        args.append(gates)
    return pl.pallas_call(
        functools.partial(_swiglu_kernel, gated=gated),
        grid=(ne, n // TM),
        in_specs=in_specs,
        out_specs=pl.BlockSpec((TM, f), lambda j, i: (i, j)),
        out_shape=jax.ShapeDtypeStruct((n, ne * f), BF16),
        scratch_shapes=[pltpu.VMEM((d, f), BF16), pltpu.VMEM((d, f), BF16)],
        compiler_params=_cp(("arbitrary", "arbitrary")),
        name="swiglu_up",
    )(*args)


def _down_kernel(x_ref, w_ref, g_ref, o_ref, acc):
    k = pl.program_id(1)
    part = _dot(x_ref[...], w_ref[...].astype(BF16))

    @pl.when(k == 0)
    def _():
        acc[...] = part

    @pl.when(k > 0)
    def _():
        acc[...] += part

    @pl.when(k == pl.num_programs(1) - 1)
    def _():
        o_ref[...] = _rms(acc[...], g_ref[...])


def _down_norm(hid, wd, e, g, tm):
    n = hid.shape[0]
    _, ne, f, d = wd.shape
    return pl.pallas_call(
        _down_kernel,
        grid=(n // tm, ne),
        in_specs=[pl.BlockSpec((tm, f), lambda i, k: (i, k)),
                  pl.BlockSpec((None, None, f, d), lambda i, k: (e, k, 0, 0)),
                  pl.BlockSpec((1, d), lambda i, k: (0, 0))],
        out_specs=pl.BlockSpec((tm, d), lambda i, k: (i, 0)),
        out_shape=jax.ShapeDtypeStruct((n, d), F32),
        scratch_shapes=[pltpu.VMEM((tm, d), F32)],
        compiler_params=_cp(("parallel", "arbitrary")),
        name="down_norm",
    )(hid, wd, g.reshape(1, d))


def _router_kernel(x_ref, w_ref, o_ref, *, ne):
    logits = _dot(x_ref[...], w_ref[...].astype(BF16))
    lane = lax.broadcasted_iota(jnp.int32, logits.shape, 1).astype(F32)
    neg = jnp.float32(-jnp.inf)
    logits = jnp.where(lane < ne, logits, neg)
    m1 = jnp.max(logits, axis=-1, keepdims=True)
    i1 = jnp.min(jnp.where(logits == m1, lane, float(LANES)), axis=-1, keepdims=True)
    rest = jnp.where(lane == i1, neg, logits)
    m2 = jnp.max(rest, axis=-1, keepdims=True)
    i2 = jnp.min(jnp.where(rest == m2, lane, float(LANES)), axis=-1, keepdims=True)
    e2 = jnp.exp(m2 - m1)
    w1 = 1.0 / (1.0 + e2)
    w2 = e2 / (1.0 + e2)
    o_ref[...] = jnp.where(lane == i1, w1, 0.0) + jnp.where(lane == i2, w2, 0.0)


def _router(x, w_pad, ne):
    n, d = x.shape
    return pl.pallas_call(
        functools.partial(_router_kernel, ne=ne),
        grid=(n // TM,),
        in_specs=[pl.BlockSpec((TM, d), lambda i: (i, 0)),
                  pl.BlockSpec((d, LANES), lambda i: (0, 0))],
        out_specs=pl.BlockSpec((TM, LANES), lambda i: (i, 0)),
        out_shape=jax.ShapeDtypeStruct((n, LANES), F32),
        compiler_params=_cp(("parallel",)),
        name="router",
    )(x, w_pad)


def _to_column_major(x, b, t):
    d = x.shape[-1]
    return x.reshape(b, t // GRID_W, GRID_W, d).transpose(0, 2, 1, 3).reshape(b * t, d)


def _to_row_major(x, b, t):
    d = x.shape[-1]
    return x.reshape(b, GRID_W, t // GRID_W, d).transpose(0, 2, 1, 3).reshape(b * t, d)


def kernel(x_prompt, x_sample, state_hgrn, state_s5, state_gdn, c, c_ctx, ada_w, ada_b, norm_g, w_in, hgrn_lb, hgrn_norm, s5_a_re, s5_a_im, s5_log_dt, s5_b_re, s5_b_im, s5_c_re, s5_c_im, s5_d, s5_glu_w, s5_glu_b, gdn_conv, gdn_a_log, gdn_dt_bias, gdn_norm, w_branch, w_out, ffn_wg, ffn_wu, ffn_wd, moe_router, moe_wg, moe_wu, moe_wd):
    bp, tp, d = x_prompt.shape
    bs, ts, _ = x_sample.shape
    depth = w_in.shape[0]
    n_p, n_s = bp * tp, bs * ts
    ne = moe_router.shape[-1]
    main_cols = 10 * BR_WIDTH
    ab_cols = 4 * N_HEADS

    c8 = jnp.zeros((8, d), F32).at[0].set(c_ctx).at[1:1 + bs].set(c)
    mod_all = _ada_mod(c8, ada_w, ada_b)

    lb_soft = jax.nn.softmax(hgrn_lb.astype(F32), axis=0)
    lb_all = jnp.cumsum(lb_soft, axis=0) - lb_soft[:1]

    mods = mod_all[:, :1 + bs].reshape(depth, 1 + bs, 6, d)
    zero_row = jnp.zeros((1 + bs, d), F32)

    def modv(gate, scale, shift, g):
        rows = [zero_row if sel is None else mods[sel[0], :, sel[1]] for sel in (gate, scale, shift)]
        rows.append(zero_row if g is None else jnp.broadcast_to(g, (1 + bs, d)))
        return jnp.stack(rows + [zero_row] * 4, axis=1)

    x = jnp.concatenate([x_prompt.reshape(n_p, d), x_sample.reshape(n_s, d)], axis=0)
    (h,) = _resnorm(x, None, modv(None, (0, 1), (0, 0), norm_g[0, 0]), n_p, ts, True)

    z_hgrn = jnp.zeros((bp, 2, N_HEADS, HEAD_D, HEAD_D), F32)
    ngb = S5_GROUPS // S5_GB
    w2 = 2 * S5_GB * S5_STATE
    st_hgrn, st_s5, st_gdn = [], [], []
    s5_tb = 1024

    for l in range(depth):
        col_major = l % 2 == 1
        if col_major:
            h = jnp.concatenate([h[:n_p], _to_column_major(h[n_p:], bs, ts)], axis=0)
        p_main = _mm(h, w_in, l, 0, main_cols, BR_WIDTH, "w_in_main")
        w_ab = jnp.pad(w_in[l, :, main_cols:main_cols + ab_cols], ((0, 0), (0, LANES - ab_cols)))
        p_ab = _mm(h, w_ab, None, 0, LANES, LANES, "w_in_ab")
        p_mg = _mm(h, w_in[l, :, main_cols + ab_cols:], None, 0, 3 * d, BR_WIDTH, "w_in_merge")

        ya_p, sa = _hgrn(p_main, lb_all[l], hgrn_norm[l], z_hgrn, 0, bp, tp)
        ya_s, _ = _hgrn(p_main, lb_all[l], hgrn_norm[l], state_hgrn[:, l].astype(F32), n_p // ts, bs, ts)
        st_hgrn.append(sa)

        bb, cc, lam = _s5_params(s5_a_re[l], s5_a_im[l], s5_log_dt[l], s5_b_re[l], s5_b_im[l],
                                 s5_c_re[l], s5_c_im[l])
        nblk_p = n_p // s5_tb
        seq_per_blk = s5_tb // tp
        yb_p, sb = _s5(p_main, bb, cc, lam, s5_d[l], jnp.zeros((nblk_p, ngb, 2, 8, w2), F32),
                       0, nblk_p, s5_tb, tp)
        s0s = state_s5[:, l].astype(F32)
        s0s = s0s.reshape(bs, 2, ngb, S5_GB * S5_STATE, 2).transpose(0, 2, 1, 4, 3).reshape(bs, ngb, 2, 1, w2)
        s0s = jnp.pad(s0s, ((0, 0), (0, 0), (0, 0), (0, 7), (0, 0)))
        yb_s, _ = _s5(p_main, bb, cc, lam, s5_d[l], s0s, n_p // s5_tb, n_s // s5_tb, s5_tb, ts)
        yb = _glu(jnp.concatenate([yb_p, yb_s], axis=0), s5_glu_w, s5_glu_b, l)
        sb = sb[:, :, :, :seq_per_blk].reshape(nblk_p, ngb, 2, seq_per_blk, 2, S5_GB, S5_STATE)
        sb = sb.transpose(0, 3, 2, 1, 5, 6, 4).reshape(bp, 2, S5_GROUPS, S5_STATE, 2)
        st_s5.append(sb)

        yc_p, sc = _gdn(p_main, p_ab, gdn_conv[l], gdn_a_log[l], gdn_dt_bias[l], gdn_norm[l], z_hgrn, 0, bp, tp)
        yc_s, _ = _gdn(p_main, p_ab, gdn_conv[l], gdn_a_log[l], gdn_dt_bias[l], gdn_norm[l],
                       state_gdn[:, l].astype(F32), n_p // ts, bs, ts)
        st_gdn.append(sc)

        ya = jnp.concatenate([ya_p, ya_s], axis=0)
        yc = jnp.concatenate([yc_p, yc_s], axis=0)
        r_m = _merge(ya, yb, yc, p_mg, w_branch, w_out, norm_g[l, 1], l)
        if col_major:
            r_m = jnp.concatenate([r_m[:n_p], _to_row_major(r_m[n_p:], bs, ts)], axis=0)
        x, h2 = _resnorm(x, r_m, modv((l, 2), (l, 4), (l, 3), norm_g[l, 2]), n_p, ts, True)

        if l % 2 == 0:
            e = l // 2
            hid = _swiglu_up(h2, ffn_wg, ffn_wu, e, ffn_wg.shape[-1] // 2)
            r_f = _down_norm(hid, ffn_wd[:, None], e, norm_g[l, 3], TM)
        else:
            e = l // 2
            gates = _router(h2, jnp.pad(moe_router[e], ((0, 0), (0, LANES - ne))), ne)
            hid = _swiglu_up(h2, moe_wg, moe_wu, e, moe_wg.shape[-1], gates)
            r_f = _down_norm(hid, moe_wd, e, norm_g[l, 3], 1024)

        if l + 1 < depth:
            x, h = _resnorm(x, r_f, modv((l, 5), (l + 1, 1), (l + 1, 0), norm_g[l + 1, 0]), n_p, ts, True)
        else:
            (x,) = _resnorm(x, r_f, modv((l, 5), None, None, None), n_p, ts, False)

    y_prompt = x[:n_p].reshape(bp, tp, d)
    y_sample = x[n_p:].reshape(bs, ts, d)
    new_hgrn = jnp.stack(st_hgrn, axis=1).astype(x_prompt.dtype)
    new_s5 = jnp.stack(st_s5, axis=1).astype(x_prompt.dtype)
    new_gdn = jnp.stack(st_gdn, axis=1).astype(x_prompt.dtype)
    return (y_prompt, y_sample, new_hgrn, new_s5, new_gdn)
```

```python
import functools
import math

import jax
import jax.numpy as jnp
from jax import lax
from jax.experimental import pallas as pl
from jax.experimental.pallas import tpu as pltpu

F32 = jnp.float32
BF16 = jnp.bfloat16

EPS = 1e-6
F_FLOOR = 1e-6
LANES = 128
GRID_W = 64
HEAD_D = 128
N_HEADS = 4
BR_WIDTH = 512
S5_GROUPS = 32
S5_GROUP = 16
S5_STATE = 64
S5_GB = 8
S5_SEG = 64
GLA_CHUNK = 256
GDN_CHUNK = 64
GDN_SPAN = 256
TM = 512
VMEM_LIMIT = 56 * 1024 * 1024


def _cp(sem, vmem=VMEM_LIMIT):
    return pltpu.CompilerParams(dimension_semantics=sem, vmem_limit_bytes=vmem)


def _dot(a, b):
    return jnp.dot(a, b, preferred_element_type=F32)


def _dot_nt(a, b):
    return lax.dot_general(a, b, (((1,), (1,)), ((), ())), preferred_element_type=F32)


def _dot_tn(a, b):
    return lax.dot_general(a, b, (((0,), (0,)), ((), ())), preferred_element_type=F32)


def _silu(x):
    return x * jax.nn.sigmoid(x)


def _rms(x, g):
    return x * lax.rsqrt(jnp.mean(x * x, axis=-1, keepdims=True) + EPS) * g


def _exact_mm(mat, x):
    hi = x.astype(BF16)
    r1 = x - hi.astype(F32)
    mid = r1.astype(BF16)
    lo = (r1 - mid.astype(F32)).astype(BF16)
    return _dot(mat, hi) + _dot(mat, mid) + _dot(mat, lo)


def _exact_mm_nt(mat, x):
    hi = x.astype(BF16)
    r1 = x - hi.astype(F32)
    mid = r1.astype(BF16)
    lo = (r1 - mid.astype(F32)).astype(BF16)
    return _dot_nt(mat, hi) + _dot_nt(mat, mid) + _dot_nt(mat, lo)


def _ada_kernel(c_ref, w_ref, b_ref, o_ref):
    c = c_ref[...]
    a = _silu(c).astype(BF16)
    o_ref[...] = _dot(a, w_ref[...].astype(BF16)) + b_ref[...]


def _ada_mod(c8, ada_w, ada_b):
    depth, d, n6 = ada_w.shape
    tn = 512
    return pl.pallas_call(
        _ada_kernel,
        grid=(depth, n6 // tn),
        in_specs=[
            pl.BlockSpec((8, d), lambda l, j: (0, 0)),
            pl.BlockSpec((None, d, tn), lambda l, j: (l, 0, j)),
            pl.BlockSpec((None, 1, tn), lambda l, j: (l, 0, j)),
        ],
        out_specs=pl.BlockSpec((None, 8, tn), lambda l, j: (l, 0, j)),
        out_shape=jax.ShapeDtypeStruct((depth, 8, n6), F32),
        compiler_params=_cp(("parallel", "parallel")),
        name="ada_mod",
    )(c8, ada_w, ada_b.reshape(depth, 1, n6))


def _resnorm_kernel(*refs, has_y, has_h):
    it = iter(refs)
    x_ref = next(it)
    y_ref = next(it) if has_y else None
    mv_ref = next(it)
    xo_ref = next(it) if has_y else None
    h_ref = next(it) if has_h else None
    x = x_ref[...]
    mv = mv_ref[...]
    if has_y:
        x = x + mv[0:1] * y_ref[...]
        xo_ref[...] = x
    if has_h:
        h = _rms(x, mv[3:4]) * (1.0 + mv[1:2]) + mv[2:3]
        h_ref[...] = h.astype(BF16)


def _resnorm(x, y, modv, n_prompt, seq_s, has_h):
    n, d = x.shape
    has_y = y is not None

    def group(i):
        return jnp.where(i * TM < n_prompt, 0, 1 + (i * TM - n_prompt) // seq_s)

    tile = pl.BlockSpec((TM, d), lambda i: (i, 0))
    in_specs = [tile] + ([tile] if has_y else []) + [pl.BlockSpec((None, 8, d), lambda i: (group(i), 0, 0))]
    out_specs, out_shape = [], []
    if has_y:
        out_specs.append(tile)
        out_shape.append(jax.ShapeDtypeStruct((n, d), F32))
    if has_h:
        out_specs.append(tile)
        out_shape.append(jax.ShapeDtypeStruct((n, d), BF16))
    args = [x] + ([y] if has_y else []) + [modv]
    outs = pl.pallas_call(
        functools.partial(_resnorm_kernel, has_y=has_y, has_h=has_h),
        grid=(n // TM,),
        in_specs=in_specs,
        out_specs=out_specs,
        out_shape=out_shape,
        compiler_params=_cp(("parallel",)),
        name="resnorm",
    )(*args)
    return outs


def _mm_kernel(x_ref, w_ref, o_ref, wb_ref):
    @pl.when(pl.program_id(1) == 0)
    def _():
        wb_ref[...] = w_ref[...].astype(BF16)

    o_ref[...] = _dot(x_ref[...], wb_ref[...])


def _mm(x, w, l, col0, ncols, tn, name):
    m, k = x.shape
    j0 = col0 // tn
    if l is None:
        wspec = pl.BlockSpec((k, tn), lambda j, i: (0, j + j0))
    else:
        wspec = pl.BlockSpec((None, k, tn), lambda j, i: (l, 0, j + j0))
    return pl.pallas_call(
        _mm_kernel,
        grid=(ncols // tn, m // TM),
        in_specs=[pl.BlockSpec((TM, k), lambda j, i: (i, 0)), wspec],
        out_specs=pl.BlockSpec((TM, tn), lambda j, i: (i, j)),
        out_shape=jax.ShapeDtypeStruct((m, ncols), F32),
        scratch_shapes=[pltpu.VMEM((k, tn), BF16)],
        compiler_params=_cp(("arbitrary", "arbitrary")),
        name=name,
    )(x, w)


def _gla_intra(q, k, b, d, c):
    rowi = lax.broadcasted_iota(jnp.int32, (c, LANES), 0)
    ri = lax.broadcasted_iota(jnp.int32, (c, c), 0)
    ci = lax.broadcasted_iota(jnp.int32, (c, c), 1)
    a = jnp.zeros((c, c), F32)
    size = c
    while size >= 8:
        half = size // 2
        nb = c // size
        b3 = b.reshape(nb, size, LANES)
        pos = rowi & (size - 1)
        if d == 0:
            ref = b3[:, half - 1:half, :]
            qsel = pos >= half
        else:
            ref = b3[:, half:half + 1, :]
            qsel = pos < half
        ref = jnp.broadcast_to(ref, (nb, size, LANES)).reshape(c, LANES)
        qs = jnp.where(qsel, q * jnp.exp(jnp.minimum(b - ref, 0.0)), 0.0).astype(BF16)
        ks = jnp.where(qsel, 0.0, k * jnp.exp(jnp.minimum(ref - b, 0.0))).astype(BF16)
        p = _dot_nt(qs, ks)
        if nb > 1:
            shift = int(math.log2(size))
            p = jnp.where((ri >> shift) == (ci >> shift), p, 0.0)
        a = a + p
        size = half
    b3 = b.reshape(c // 8, 8, LANES)
    lo, hi = (0, 4) if d == 0 else (3, 7)
    ref_lo = jnp.broadcast_to(b3[:, lo:lo + 1, :], (c // 8, 8, LANES)).reshape(c, LANES)
    ref_hi = jnp.broadcast_to(b3[:, hi:hi + 1, :], (c // 8, 8, LANES)).reshape(c, LANES)
    ref = jnp.where((rowi & 7) >= 4, ref_hi, ref_lo)
    qs = (q * jnp.exp(jnp.minimum(b - ref, 0.0))).astype(BF16)
    ks = (k * jnp.exp(ref - b)).astype(BF16)
    p = _dot_nt(qs, ks)
    keep = ((ri >> 2) == (ci >> 2)) & ((ci <= ri) if d == 0 else (ci >= ri))
    return a + jnp.where(keep, p, 0.0)


def _hgrn_kernel(aq_ref, af_ref, ab_ref, ai_ref, ag_ref, lb_ref, g_ref, s0_ref, y_ref, s_ref, o_acc, *, t, c):
    nchunk = t // c
    ri = lax.broadcasted_iota(jnp.int32, (c, c), 0)
    ci = lax.broadcasted_iota(jnp.int32, (c, c), 1)
    scale = HEAD_D ** -0.5
    for d in (0, 1):
        z_ref = af_ref if d == 0 else ab_ref
        lbd = lb_ref[d:d + 1, :]
        tri = jnp.where((ci <= ri) if d == 0 else (ci >= ri), 1.0, 0.0).astype(BF16)

        def chunk(i, st, d=d, z_ref=z_ref, lbd=lbd, tri=tri):
            cidx = i if d == 0 else nchunk - 1 - i
            r0 = pl.multiple_of(cidx * c, c)
            z = z_ref[pl.ds(r0, c), :]
            f = lbd + (1.0 - lbd) * jax.nn.sigmoid(z)
            lf = jnp.log(jnp.maximum(f, F_FLOOR))
            k = (1.0 - lbd) * jax.nn.sigmoid(-z)
            q = _silu(aq_ref[pl.ds(r0, c), :]) * scale
            v = ai_ref[pl.ds(r0, c), :].astype(BF16)
            b = _exact_mm(tri, lf)
            a = _gla_intra(q, k, b, d, c)
            o = _dot(a.astype(BF16), v) + _dot_nt((q * jnp.exp(b)).astype(BF16), st.astype(BF16))
            if d == 0:
                o_acc[pl.ds(r0, c), :] = o
            else:
                o_acc[pl.ds(r0, c), :] += o
            blast = b[c - 1:c, :] if d == 0 else b[0:1, :]
            kd = (k * jnp.exp(blast - b)).astype(BF16)
            return st * jnp.exp(blast) + _dot_tn(v, kd)

        st = s0_ref[0, d, 0].T
        if nchunk == 1:
            st = chunk(0, st)
        else:
            st = lax.fori_loop(0, nchunk, chunk, st)
        s_ref[0, d, 0] = st.T
    o = o_acc[...]
    y_ref[...] = _rms(o, g_ref[...]) * _silu(ag_ref[...])


def _hgrn(p_main, lb, norm_g, s0, row_blk0, nseq, t):
    c = min(GLA_CHUNK, t)

    def col(base):
        return pl.BlockSpec((t, HEAD_D), lambda s, h: (row_blk0 + s, base + h))

    st_spec = pl.BlockSpec((1, 2, 1, HEAD_D, HEAD_D), lambda s, h: (s, 0, h, 0, 0))
    return pl.pallas_call(
        functools.partial(_hgrn_kernel, t=t, c=c),
        grid=(nseq, N_HEADS),
        in_specs=[col(0), col(4), col(8), col(12), col(16),
                  pl.BlockSpec((2, HEAD_D), lambda s, h: (0, h)),
                  pl.BlockSpec((1, HEAD_D), lambda s, h: (0, 0)),
                  st_spec],
        out_specs=[pl.BlockSpec((t, HEAD_D), lambda s, h: (s, h)), st_spec],
        out_shape=[jax.ShapeDtypeStruct((nseq * t, BR_WIDTH), F32),
                   jax.ShapeDtypeStruct((nseq, 2, N_HEADS, HEAD_D, HEAD_D), F32)],
        scratch_shapes=[pltpu.VMEM((t, HEAD_D), F32)],
        compiler_params=_cp(("parallel", "parallel")),
        name="hgrn2",
    )(p_main, p_main, p_main, p_main, p_main, lb, norm_g.reshape(1, HEAD_D), s0)


def _dot_b(a, b):
    return _dot(a.astype(BF16), b.astype(BF16))


def _tri_solve(ms, xs, c):
    ri = lax.broadcasted_iota(jnp.int32, (c, c), 0)
    ci = lax.broadcasted_iota(jnp.int32, (c, c), 1)
    eye = jnp.where(ri == ci, 1.0, 0.0)
    m8 = [jnp.where((ri >> 3) == (ci >> 3), m, 0.0) for m in ms]
    ts = [eye - m for m in m8]
    ps = [_dot_b(m, m) for m in m8]
    ts = [t + _dot_b(t, p) for t, p in zip(ts, ps)]
    ps = [_dot_b(p, p) for p in ps]
    ts = [t + _dot_b(t, p) for t, p in zip(ts, ps)]
    shift = 3
    while (1 << shift) < c:
        join = ((ri >> (shift + 1)) == (ci >> (shift + 1))) & ((ri >> shift) != (ci >> shift))
        lts = [_dot_b(jnp.where(join, m, 0.0), t) for m, t in zip(ms, ts)]
        ts = [t - _dot_b(t, lt) for t, lt in zip(ts, lts)]
        shift += 1
    return [_dot_b(t, x) for t, x in zip(ts, xs)]


def _chunk_cumsum(x, d, c):
    n = x.shape[0]
    pos = lax.broadcasted_iota(jnp.int32, x.shape, 0) & (c - 1)
    s = 1
    while s < c:
        if d == 0:
            x = x + jnp.where(pos >= s, pltpu.roll(x, s, 0), 0.0)
        else:
            x = x + jnp.where(pos < c - s, pltpu.roll(x, n - s, 0), 0.0)
        s *= 2
    return x


def _gdn_kernel(cq_ref, ck_ref, cv_ref, cg_ref, ab_ref, cw_ref, alog_ref, dtb_ref, g_ref, s0_ref,
                y_ref, s_ref, q_s, k_s, v_s, la_s, be_s, w_s, qk_s, qe_s, kd_s, dec_s, o_acc, *, t, c, span):
    h = pl.program_id(1)
    nchunk = t // c
    cps = span // c
    rowt = lax.broadcasted_iota(jnp.int32, (t, HEAD_D), 0)
    lane = lax.broadcasted_iota(jnp.int32, (t, LANES), 1)
    ri = lax.broadcasted_iota(jnp.int32, (span, span), 0)
    ci = lax.broadcasted_iota(jnp.int32, (span, span), 1)
    shift = int(math.log2(c))
    same = (ri >> shift) == (ci >> shift)

    def conv(x_ref, j):
        x = x_ref[...]
        w = cw_ref[j]
        xm = jnp.where(rowt == 0, 0.0, pltpu.roll(x, 1, 0))
        xp = jnp.where(rowt == t - 1, 0.0, pltpu.roll(x, t - 1, 0))
        return xm * w[0:1] + x * w[1:2] + xp * w[2:3]

    q = _silu(conv(cq_ref, 0))
    q_s[...] = q * lax.rsqrt(jnp.sum(q * q, axis=-1, keepdims=True) + EPS) * (HEAD_D ** -0.5)
    k = _silu(conv(ck_ref, 1))
    k_s[...] = k * lax.rsqrt(jnp.sum(k * k, axis=-1, keepdims=True) + EPS)
    v_s[...] = _silu(conv(cv_ref, 2))
    ab = ab_ref[...]

    for d in (0, 1):
        a_col = jnp.sum(jnp.where(lane == d * N_HEADS + h, ab, 0.0), axis=-1, keepdims=True)
        b_col = jnp.sum(jnp.where(lane == 2 * N_HEADS + d * N_HEADS + h, ab, 0.0), axis=-1, keepdims=True)
        coef = -jnp.exp(jnp.full((1, LANES), alog_ref[d, h], F32))
        xs = a_col + dtb_ref[d, h]
        softplus = jnp.maximum(xs, 0.0) + jnp.log1p(jnp.exp(-jnp.abs(xs)))
        la_s[d] = coef * jnp.broadcast_to(softplus, (t, LANES))
        be_s[d] = jnp.broadcast_to(jax.nn.sigmoid(b_col), (t, LANES))
    o_acc[...] = jnp.zeros((t, HEAD_D), F32)

    def wy_span(sp, carry):
        r0 = pl.multiple_of(sp * span, span)
        qsp = q_s[pl.ds(r0, span), :]
        ksp = k_s[pl.ds(r0, span), :]
        vsp = v_s[pl.ds(r0, span), :]
        kb = ksp.astype(BF16)
        kk = _dot_nt(kb, kb)
        qk = _dot_nt(qsp.astype(BF16), kb)
        ms, xs, dst = [], [], []
        for d in (0, 1):
            be = be_s[d, pl.ds(r0, span), :]
            g = _chunk_cumsum(la_s[d, pl.ds(r0, span), :], d, c)
            gt = g.T
            gcol = jnp.concatenate([g] * (span // LANES), axis=1)
            grow = jnp.concatenate([gt] * (span // LANES), axis=0)
            incl = same & ((ci <= ri) if d == 0 else (ci >= ri))
            gamma = jnp.where(incl, jnp.exp(jnp.minimum(gcol - grow, 0.0)), 0.0)
            bcol = jnp.concatenate([be] * (span // LANES), axis=1)
            m_sp = jnp.where(ci == ri, 0.0, bcol * kk * gamma)
            qkg = (qk * gamma).astype(BF16)
            eg = jnp.exp(g)
            xv = be * vsp
            xk = be * eg * ksp
            qe_s[d, pl.ds(r0, span), :] = (qsp * eg).astype(BF16)
            for j in range(cps):
                lo = j * c
                rj = pl.multiple_of(r0 + lo, c)
                ms.append(m_sp[lo:lo + c, lo:lo + c])
                xs.append(jnp.concatenate([xv[lo:lo + c], xk[lo:lo + c]], axis=1))
                dst.append((d, rj))
                qk_s[d, pl.ds(rj, c), 0:c] = qkg[lo:lo + c, lo:lo + c]
                gj = g[lo:lo + c]
                glast = gj[c - 1:c] if d == 0 else gj[0:1]
                kd_s[d, pl.ds(rj, c), :] = (ksp[lo:lo + c] * jnp.exp(glast - gj)).astype(BF16)
                dec_s[d, pl.ds(sp * cps + j, 1), :] = jnp.exp(glast)
        for (d, rj), w in zip(dst, _tri_solve(ms, xs, c)):
            w_s[d, pl.ds(rj, c), :] = w
        return carry

    if t == span:
        wy_span(0, 0)
    else:
        lax.fori_loop(0, t // span, wy_span, 0)

    def scan_step(i, carry):
        cidx = (i, nchunk - 1 - i)
        r0 = [pl.multiple_of(ck * c, c) for ck in cidx]
        sb = [s.astype(BF16) for s in carry]
        ws = [w_s[d, pl.ds(r0[d], c), :] for d in (0, 1)]
        wks = [_dot(ws[d][:, HEAD_D:].astype(BF16), sb[d]) for d in (0, 1)]
        oq = [_dot(qe_s[d, pl.ds(r0[d], c), :], sb[d]) for d in (0, 1)]
        vnb = [(ws[d][:, :HEAD_D] - wks[d]).astype(BF16) for d in (0, 1)]
        new = [carry[d] * dec_s[d, pl.ds(cidx[d], 1), :] + _dot_tn(kd_s[d, pl.ds(r0[d], c), :], vnb[d])
               for d in (0, 1)]
        for d in (0, 1):
            o_acc[pl.ds(r0[d], c), :] += oq[d] + _dot(qk_s[d, pl.ds(r0[d], c), 0:c], vnb[d])
        return tuple(new)

    s_f, s_b = lax.fori_loop(0, nchunk, scan_step, (s0_ref[0, 0, 0], s0_ref[0, 1, 0]),
                             unroll=nchunk <= 4)
    s_ref[0, 0, 0] = s_f
    s_ref[0, 1, 0] = s_b
    y_ref[...] = _rms(o_acc[...], g_ref[...]) * _silu(cg_ref[...])


def _gdn(p_main, p_ab, conv_w, a_log, dt_bias, norm_g, s0, row_blk0, nseq, t):
    c = GDN_CHUNK

    def col(base):
        return pl.BlockSpec((t, HEAD_D), lambda s, h: (row_blk0 + s, base + h))

    st_spec = pl.BlockSpec((1, 2, 1, HEAD_D, HEAD_D), lambda s, h: (s, 0, h, 0, 0))
    smem = pl.BlockSpec(memory_space=pltpu.SMEM)
    seq_buf = pltpu.VMEM((t, HEAD_D), F32)
    dir_f32 = pltpu.VMEM((2, t, LANES), F32)
    dir_bf16 = pltpu.VMEM((2, t, LANES), BF16)
    scratch = [seq_buf, seq_buf, seq_buf, dir_f32, dir_f32,
               pltpu.VMEM((2, t, 2 * HEAD_D), F32), dir_bf16, dir_bf16, dir_bf16,
               pltpu.VMEM((2, max(t // c, 8), LANES), F32), seq_buf]
    return pl.pallas_call(
        functools.partial(_gdn_kernel, t=t, c=c, span=GDN_SPAN),
        grid=(nseq, N_HEADS),
        in_specs=[col(24), col(28), col(32), col(36),
                  pl.BlockSpec((t, LANES), lambda s, h: (row_blk0 + s, 0)),
                  pl.BlockSpec((3, 3, HEAD_D), lambda s, h: (0, 0, h)),
                  smem, smem,
                  pl.BlockSpec((1, HEAD_D), lambda s, h: (0, 0)),
                  st_spec],
        out_specs=[pl.BlockSpec((t, HEAD_D), lambda s, h: (s, h)), st_spec],
        out_shape=[jax.ShapeDtypeStruct((nseq * t, BR_WIDTH), F32),
                   jax.ShapeDtypeStruct((nseq, 2, N_HEADS, HEAD_D, HEAD_D), F32)],
        scratch_shapes=scratch,
        compiler_params=_cp(("parallel", "parallel")),
        name="gdn",
    )(p_main, p_main, p_main, p_main, p_ab, conv_w, a_log, dt_bias, norm_g.reshape(1, HEAD_D), s0)


def _cmul(ar, ai, br, bi):
    return ar * br - ai * bi, ar * bi + ai * br


def _s5_kernel(u_ref, bb_ref, cc_ref, lam_ref, dsk_ref, s0_ref, y_ref, sf_ref, up_s, x_s, yp_s,
               *, tb, seg, segs_per_seq):
    r_par = tb // seg
    half = S5_GB * S5_STATE
    for r in range(seg):
        up_s[r * r_par:(r + 1) * r_par, :] = u_ref[pl.ds(r, r_par, stride=seg), :]
    ub = up_s[...].astype(BF16)
    sf_ref[...] = jnp.zeros(sf_ref.shape, F32)
    nseq = r_par // segs_per_seq
    for d in (0, 1):
        x_s[...] = _dot(ub, bb_ref[0, d])
        lre = lam_ref[0, d, 0:1, :]
        lim = lam_ref[0, d, 1:2, :]

        def local_step(i, carry, d=d, lre=lre, lim=lim):
            hre, him = carry
            r = i if d == 0 else seg - 1 - i
            off = pl.multiple_of(r * r_par, r_par)
            nre = lre * hre - lim * him + x_s[pl.ds(off, r_par), 0:half]
            nim = lre * him + lim * hre + x_s[pl.ds(off, r_par), half:2 * half]
            x_s[pl.ds(off, r_par), 0:half] = nre
            x_s[pl.ds(off, r_par), half:2 * half] = nim
            return nre, nim

        zero = jnp.zeros((r_par, half), F32)
        lax.fori_loop(0, seg, local_step, (zero, zero))

        pre, pim = lre, lim
        n = 1
        while n < seg:
            pre, pim = _cmul(pre, pim, pre, pim)
            n *= 2
        end_row = (seg - 1) * r_par if d == 0 else 0
        ere = x_s[end_row:end_row + r_par, 0:half]
        eim = x_s[end_row:end_row + r_par, half:2 * half]
        hin = [None] * r_par
        order = range(r_par) if d == 0 else range(r_par - 1, -1, -1)
        for j in order:
            sq, pos = divmod(j, segs_per_seq)
            first = pos == (0 if d == 0 else segs_per_seq - 1)
            if first:
                hin[j] = (s0_ref[0, 0, d, sq:sq + 1, 0:half], s0_ref[0, 0, d, sq:sq + 1, half:2 * half])
            else:
                pj = j - 1 if d == 0 else j + 1
                cre, cim = _cmul(pre, pim, hin[pj][0], hin[pj][1])
                hin[j] = (cre + ere[pj:pj + 1], cim + eim[pj:pj + 1])
        hre_in = jnp.concatenate([hj[0] for hj in hin], axis=0)
        him_in = jnp.concatenate([hj[1] for hj in hin], axis=0)

        def carry_step(i, pw, d=d, lre=lre, lim=lim, hre_in=hre_in, him_in=him_in):
            wre, wim = pw
            r = i if d == 0 else seg - 1 - i
            off = pl.multiple_of(r * r_par, r_par)
            are, aim = _cmul(wre, wim, hre_in, him_in)
            x_s[pl.ds(off, r_par), 0:half] += are
            x_s[pl.ds(off, r_par), half:2 * half] += aim
            return _cmul(wre, wim, lre, lim)

        lax.fori_loop(0, seg, carry_step, (lre, lim))

        for sq in range(nseq):
            j = (sq + 1) * segs_per_seq - 1 if d == 0 else sq * segs_per_seq
            row = end_row + j
            sf_ref[0, 0, d, sq:sq + 1, :] = x_s[row:row + 1, :]
        yp = _dot(x_s[...].astype(BF16), cc_ref[0, d])
        if d == 0:
            yp_s[...] = yp + up_s[...] * dsk_ref[...]
        else:
            yp_s[...] += yp
    for r in range(seg):
        y_ref[pl.ds(r, r_par, stride=seg), :] = yp_s[r * r_par:(r + 1) * r_par, :]


def _s5(p_main, bb, cc, lam, dskip, s0, row_blk0, nblk, tb, t):
    ngb = S5_GROUPS // S5_GB
    w2 = 2 * S5_GB * S5_STATE
    seg = S5_SEG
    return pl.pallas_call(
        functools.partial(_s5_kernel, tb=tb, seg=seg, segs_per_seq=t // seg),
        grid=(nblk, ngb),
        in_specs=[
            pl.BlockSpec((tb, LANES), lambda i, gb: (row_blk0 + i, 20 + gb)),
            pl.BlockSpec((1, 2, LANES, w2), lambda i, gb: (gb, 0, 0, 0)),
            pl.BlockSpec((1, 2, w2, LANES), lambda i, gb: (gb, 0, 0, 0)),
            pl.BlockSpec((1, 2, 2, w2 // 2), lambda i, gb: (gb, 0, 0, 0)),
            pl.BlockSpec((1, LANES), lambda i, gb: (0, gb)),
            pl.BlockSpec((1, 1, 2, 8, w2), lambda i, gb: (i, gb, 0, 0, 0)),
        ],
        out_specs=[pl.BlockSpec((tb, LANES), lambda i, gb: (i, gb)),
                   pl.BlockSpec((1, 1, 2, 8, w2), lambda i, gb: (i, gb, 0, 0, 0))],
        out_shape=[jax.ShapeDtypeStruct((nblk * tb, BR_WIDTH), F32),
                   jax.ShapeDtypeStruct((nblk, ngb, 2, 8, w2), F32)],
        scratch_shapes=[pltpu.VMEM((tb, LANES), F32), pltpu.VMEM((tb, w2), F32), pltpu.VMEM((tb, LANES), F32)],
        compiler_params=_cp(("parallel", "parallel")),
        name="s5",
    )(p_main, bb, cc, lam, dskip.reshape(1, BR_WIDTH), s0)


def _s5_params(a_re, a_im, log_dt, b_re, b_im, c_re, c_im):
    dt = jnp.exp(log_dt)[..., None]
    mag = jnp.exp(dt * a_re)
    lam_re = mag * jnp.cos(dt * a_im)
    lam_im = mag * jnp.sin(dt * a_im)
    den = a_re * a_re + a_im * a_im
    co_re = ((lam_re - 1.0) * a_re + lam_im * a_im) / den
    co_im = (lam_im * a_re - (lam_re - 1.0) * a_im) / den
    bb_re = co_re[..., None] * b_re - co_im[..., None] * b_im
    bb_im = co_re[..., None] * b_im + co_im[..., None] * b_re
    ngb = S5_GROUPS // S5_GB
    eye = jnp.eye(S5_GB, dtype=F32)

    def pack_in(w):
        w = w.reshape(2, ngb, S5_GB, S5_STATE, S5_GROUP)
        blk = jnp.einsum('dbgpi,gh->bdgihp', w, eye)
        return blk.reshape(ngb, 2, S5_GB * S5_GROUP, S5_GB * S5_STATE)

    def pack_out(w):
        w = w.reshape(2, ngb, S5_GB, S5_GROUP, S5_STATE)
        blk = jnp.einsum('dbgip,gh->bdgphi', w, eye)
        return blk.reshape(ngb, 2, S5_GB * S5_STATE, S5_GB * S5_GROUP)

    bb = jnp.concatenate([pack_in(bb_re), pack_in(bb_im)], axis=-1).astype(BF16)
    cc = jnp.concatenate([pack_out(c_re), -pack_out(c_im)], axis=-2).astype(BF16)
    lam = jnp.stack([lam_re, lam_im], axis=1)
    lam = lam.reshape(2, 2, ngb, S5_GB * S5_STATE).transpose(2, 0, 1, 3)
    return bb, cc, lam


def _glu_kernel(y_ref, w_ref, b_ref, o_ref, wb_ref):
    @pl.when(pl.program_id(0) == 0)
    def _():
        wb_ref[...] = w_ref[...].astype(BF16)

    z = _dot(y_ref[...].astype(BF16), wb_ref[...]) + b_ref[...]
    half = z.shape[1] // 2
    o_ref[...] = z[:, :half] * jax.nn.sigmoid(z[:, half:])


def _glu(y, w, b, l):
    n, k = y.shape
    n2 = w.shape[-1]
    return pl.pallas_call(
        _glu_kernel,
        grid=(n // TM,),
        in_specs=[pl.BlockSpec((TM, k), lambda i: (i, 0)),
                  pl.BlockSpec((None, k, n2), lambda i: (l, 0, 0)),
                  pl.BlockSpec((None, 1, n2), lambda i: (l, 0, 0))],
        out_specs=pl.BlockSpec((TM, n2 // 2), lambda i: (i, 0)),
        out_shape=jax.ShapeDtypeStruct((n, n2 // 2), F32),
        scratch_shapes=[pltpu.VMEM((k, n2), BF16)],
        compiler_params=_cp(("arbitrary",)),
        name="s5_glu",
    )(y, w, b.reshape(b.shape[0], 1, n2))


def _merge_kernel(ya_ref, yb_ref, yc_ref, mg_ref, wbr_ref, wout_ref, g_ref, o_ref, wbr_s, wout_s):
    @pl.when(pl.program_id(0) == 0)
    def _():
        wbr_s[...] = wbr_ref[...].astype(BF16)
        wout_s[...] = wout_ref[...].astype(BF16)

    d = wout_s.shape[0]
    acc = None
    for n, br_ref in enumerate((ya_ref, yb_ref, yc_ref)):
        proj = _dot(br_ref[...].astype(BF16), wbr_s[n])
        term = proj * jax.nn.sigmoid(mg_ref[:, n * d:(n + 1) * d])
        acc = term if acc is None else acc + term
    y = _dot(acc.astype(BF16), wout_s[...])
    o_ref[...] = _rms(y, g_ref[...])


def _merge(ya, yb, yc, mg, w_branch, w_out, g, l):
    n = ya.shape[0]
    d = w_out.shape[-1]
    tm = 256
    br = pl.BlockSpec((tm, BR_WIDTH), lambda i: (i, 0))
    return pl.pallas_call(
        _merge_kernel,
        grid=(n // tm,),
        in_specs=[br, br, br,
                  pl.BlockSpec((tm, 3 * d), lambda i: (i, 0)),
                  pl.BlockSpec((None, 3, BR_WIDTH, d), lambda i: (l, 0, 0, 0)),
                  pl.BlockSpec((None, d, d), lambda i: (l, 0, 0)),
                  pl.BlockSpec((1, d), lambda i: (0, 0))],
        out_specs=pl.BlockSpec((tm, d), lambda i: (i, 0)),
        out_shape=jax.ShapeDtypeStruct((n, d), F32),
        scratch_shapes=[pltpu.VMEM((3, BR_WIDTH, d), BF16), pltpu.VMEM((d, d), BF16)],
        compiler_params=_cp(("arbitrary",)),
        name="merge",
    )(ya, yb, yc, mg, w_branch, w_out, g.reshape(1, d))


def _swiglu_kernel(*refs, gated):
    if gated:
        x_ref, wg_ref, wu_ref, gt_ref, o_ref, wg_s, wu_s = refs
    else:
        x_ref, wg_ref, wu_ref, o_ref, wg_s, wu_s = refs

    @pl.when(pl.program_id(1) == 0)
    def _():
        wg_s[...] = wg_ref[...].astype(BF16)
        wu_s[...] = wu_ref[...].astype(BF16)

    x = x_ref[...]
    hid = _silu(_dot(x, wg_s[...])) * _dot(x, wu_s[...])
    if gated:
        gt = gt_ref[...]
        lane = lax.broadcasted_iota(jnp.int32, gt.shape, 1)
        hid = hid * jnp.sum(jnp.where(lane == pl.program_id(0), gt, 0.0), axis=-1, keepdims=True)
    o_ref[...] = hid.astype(BF16)


def _swiglu_up(x, wg, wu, e, f, gates=None):
    n, d = x.shape
    gated = gates is not None
    if gated:
        ne = wg.shape[1]
        wspec = pl.BlockSpec((None, None, d, f), lambda j, i: (e, j, 0, 0))
    else:
        ne = wg.shape[-1] // f
        wspec = pl.BlockSpec((None, d, f), lambda j, i: (e, 0, j))
    in_specs = [pl.BlockSpec((TM, d), lambda j, i: (i, 0)), wspec, wspec]
    args = [x, wg, wu]
    if gated:
        in_specs.append(pl.BlockSpec((TM, LANES), lambda j, i: (i, 0)))
        args.append(gates)
    return pl.pallas_call(
        functools.partial(_swiglu_kernel, gated=gated),
        grid=(ne, n // TM),
        in_specs=in_specs,
        out_specs=pl.BlockSpec((TM, f), lambda j, i: (i, j)),
        out_shape=jax.ShapeDtypeStruct((n, ne * f), BF16),
        scratch_shapes=[pltpu.VMEM((d, f), BF16), pltpu.VMEM((d, f), BF16)],
        compiler_params=_cp(("arbitrary", "arbitrary")),
        name="swiglu_up",
    )(*args)


def _down_kernel(x_ref, w_ref, g_ref, o_ref, acc):
    k = pl.program_id(1)
    part = _dot(x_ref[...], w_ref[...].astype(BF16))

    @pl.when(k == 0)
    def _():
        acc[...] = part

    @pl.when(k > 0)
    def _():
        acc[...] += part

    @pl.when(k == pl.num_programs(1) - 1)
    def _():
        o_ref[...] = _rms(acc[...], g_ref[...])


def _down_norm(hid, wd, e, g, tm):
    n = hid.shape[0]
    _, ne, f, d = wd.shape
    return pl.pallas_call(
        _down_kernel,
        grid=(n // tm, ne),
        in_specs=[pl.BlockSpec((tm, f), lambda i, k: (i, k)),
                  pl.BlockSpec((None, None, f, d), lambda i, k: (e, k, 0, 0)),
                  pl.BlockSpec((1, d), lambda i, k: (0, 0))],
        out_specs=pl.BlockSpec((tm, d), lambda i, k: (i, 0)),
        out_shape=jax.ShapeDtypeStruct((n, d), F32),
        scratch_shapes=[pltpu.VMEM((tm, d), F32)],
        compiler_params=_cp(("parallel", "arbitrary")),
        name="down_norm",
    )(hid, wd, g.reshape(1, d))


def _router_kernel(x_ref, w_ref, o_ref, *, ne):
    logits = _dot(x_ref[...], w_ref[...].astype(BF16))
    lane = lax.broadcasted_iota(jnp.int32, logits.shape, 1).astype(F32)
    neg = jnp.float32(-jnp.inf)
    logits = jnp.where(lane < ne, logits, neg)
    m1 = jnp.max(logits, axis=-1, keepdims=True)
    i1 = jnp.min(jnp.where(logits == m1, lane, float(LANES)), axis=-1, keepdims=True)
    rest = jnp.where(lane == i1, neg, logits)
    m2 = jnp.max(rest, axis=-1, keepdims=True)
    i2 = jnp.min(jnp.where(rest == m2, lane, float(LANES)), axis=-1, keepdims=True)
    e2 = jnp.exp(m2 - m1)
    w1 = 1.0 / (1.0 + e2)
    w2 = e2 / (1.0 + e2)
    o_ref[...] = jnp.where(lane == i1, w1, 0.0) + jnp.where(lane == i2, w2, 0.0)


def _router(x, w_pad, ne):
    n, d = x.shape
    return pl.pallas_call(
        functools.partial(_router_kernel, ne=ne),
        grid=(n // TM,),
        in_specs=[pl.BlockSpec((TM, d), lambda i: (i, 0)),
                  pl.BlockSpec((d, LANES), lambda i: (0, 0))],
        out_specs=pl.BlockSpec((TM, LANES), lambda i: (i, 0)),
        out_shape=jax.ShapeDtypeStruct((n, LANES), F32),
        compiler_params=_cp(("parallel",)),
        name="router",
    )(x, w_pad)


def _to_column_major(x, b, t):
    d = x.shape[-1]
    return x.reshape(b, t // GRID_W, GRID_W, d).transpose(0, 2, 1, 3).reshape(b * t, d)


def _to_row_major(x, b, t):
    d = x.shape[-1]
    return x.reshape(b, GRID_W, t // GRID_W, d).transpose(0, 2, 1, 3).reshape(b * t, d)


def kernel(x_prompt, x_sample, state_hgrn, state_s5, state_gdn, c, c_ctx, ada_w, ada_b, norm_g, w_in, hgrn_lb, hgrn_norm, s5_a_re, s5_a_im, s5_log_dt, s5_b_re, s5_b_im, s5_c_re, s5_c_im, s5_d, s5_glu_w, s5_glu_b, gdn_conv, gdn_a_log, gdn_dt_bias, gdn_norm, w_branch, w_out, ffn_wg, ffn_wu, ffn_wd, moe_router, moe_wg, moe_wu, moe_wd):
    bp, tp, d = x_prompt.shape
    bs, ts, _ = x_sample.shape
    depth = w_in.shape[0]
    n_p, n_s = bp * tp, bs * ts
    ne = moe_router.shape[-1]
    main_cols = 10 * BR_WIDTH
    ab_cols = 4 * N_HEADS

    c8 = jnp.zeros((8, d), F32).at[0].set(c_ctx).at[1:1 + bs].set(c)
    mod_all = _ada_mod(c8, ada_w, ada_b)

    lb_soft = jax.nn.softmax(hgrn_lb.astype(F32), axis=0)
    lb_all = jnp.cumsum(lb_soft, axis=0) - lb_soft[:1]

    mods = mod_all[:, :1 + bs].reshape(depth, 1 + bs, 6, d)
    zero_row = jnp.zeros((1 + bs, d), F32)

    def modv(gate, scale, shift, g):
        rows = [zero_row if sel is None else mods[sel[0], :, sel[1]] for sel in (gate, scale, shift)]
        rows.append(zero_row if g is None else jnp.broadcast_to(g, (1 + bs, d)))
        return jnp.stack(rows + [zero_row] * 4, axis=1)

    x = jnp.concatenate([x_prompt.reshape(n_p, d), x_sample.reshape(n_s, d)], axis=0)
    (h,) = _resnorm(x, None, modv(None, (0, 1), (0, 0), norm_g[0, 0]), n_p, ts, True)

    z_hgrn = jnp.zeros((bp, 2, N_HEADS, HEAD_D, HEAD_D), F32)
    ngb = S5_GROUPS // S5_GB
    w2 = 2 * S5_GB * S5_STATE
    st_hgrn, st_s5, st_gdn = [], [], []
    s5_tb = 1024

    for l in range(depth):
        col_major = l % 2 == 1
        if col_major:
            h = jnp.concatenate([h[:n_p], _to_column_major(h[n_p:], bs, ts)], axis=0)
        p_main = _mm(h, w_in, l, 0, main_cols, BR_WIDTH, "w_in_main")
        w_ab = jnp.pad(w_in[l, :, main_cols:main_cols + ab_cols], ((0, 0), (0, LANES - ab_cols)))
        p_ab = _mm(h, w_ab, None, 0, LANES, LANES, "w_in_ab")
        p_mg = _mm(h, w_in[l, :, main_cols + ab_cols:], None, 0, 3 * d, BR_WIDTH, "w_in_merge")

        ya_p, sa = _hgrn(p_main, lb_all[l], hgrn_norm[l], z_hgrn, 0, bp, tp)
        ya_s, _ = _hgrn(p_main, lb_all[l], hgrn_norm[l], state_hgrn[:, l].astype(F32), n_p // ts, bs, ts)
        st_hgrn.append(sa)

        bb, cc, lam = _s5_params(s5_a_re[l], s5_a_im[l], s5_log_dt[l], s5_b_re[l], s5_b_im[l],
                                 s5_c_re[l], s5_c_im[l])
        nblk_p = n_p // s5_tb
        seq_per_blk = s5_tb // tp
        yb_p, sb = _s5(p_main, bb, cc, lam, s5_d[l], jnp.zeros((nblk_p, ngb, 2, 8, w2), F32),
                       0, nblk_p, s5_tb, tp)
        s0s = state_s5[:, l].astype(F32)
        s0s = s0s.reshape(bs, 2, ngb, S5_GB * S5_STATE, 2).transpose(0, 2, 1, 4, 3).reshape(bs, ngb, 2, 1, w2)
        s0s = jnp.pad(s0s, ((0, 0), (0, 0), (0, 0), (0, 7), (0, 0)))
        yb_s, _ = _s5(p_main, bb, cc, lam, s5_d[l], s0s, n_p // s5_tb, n_s // s5_tb, s5_tb, ts)
        yb = _glu(jnp.concatenate([yb_p, yb_s], axis=0), s5_glu_w, s5_glu_b, l)
        sb = sb[:, :, :, :seq_per_blk].reshape(nblk_p, ngb, 2, seq_per_blk, 2, S5_GB, S5_STATE)
        sb = sb.transpose(0, 3, 2, 1, 5, 6, 4).reshape(bp, 2, S5_GROUPS, S5_STATE, 2)
        st_s5.append(sb)

        yc_p, sc = _gdn(p_main, p_ab, gdn_conv[l], gdn_a_log[l], gdn_dt_bias[l], gdn_norm[l], z_hgrn, 0, bp, tp)
        yc_s, _ = _gdn(p_main, p_ab, gdn_conv[l], gdn_a_log[l], gdn_dt_bias[l], gdn_norm[l],
                       state_gdn[:, l].astype(F32), n_p // ts, bs, ts)
        st_gdn.append(sc)

        ya = jnp.concatenate([ya_p, ya_s], axis=0)
        yc = jnp.concatenate([yc_p, yc_s], axis=0)
        r_m = _merge(ya, yb, yc, p_mg, w_branch, w_out, norm_g[l, 1], l)
        if col_major:
            r_m = jnp.concatenate([r_m[:n_p], _to_row_major(r_m[n_p:], bs, ts)], axis=0)
        x, h2 = _resnorm(x, r_m, modv((l, 2), (l, 4), (l, 3), norm_g[l, 2]), n_p, ts, True)

        if l % 2 == 0:
            e = l // 2
            hid = _swiglu_up(h2, ffn_wg, ffn_wu, e, ffn_wg.shape[-1] // 2)
            r_f = _down_norm(hid, ffn_wd[:, None], e, norm_g[l, 3], TM)
        else:
            e = l // 2
            gates = _router(h2, jnp.pad(moe_router[e], ((0, 0), (0, LANES - ne))), ne)
            hid = _swiglu_up(h2, moe_wg, moe_wu, e, moe_wg.shape[-1], gates)
            r_f = _down_norm(hid, moe_wd, e, norm_g[l, 3], 1024)

        if l + 1 < depth:
            x, h = _resnorm(x, r_f, modv((l, 5), (l + 1, 1), (l + 1, 0), norm_g[l + 1, 0]), n_p, ts, True)
        else:
            (x,) = _resnorm(x, r_f, modv((l, 5), None, None, None), n_p, ts, False)

    y_prompt = x[:n_p].reshape(bp, tp, d)
    y_sample = x[n_p:].reshape(bs, ts, d)
    new_hgrn = jnp.stack(st_hgrn, axis=1).astype(x_prompt.dtype)
    new_s5 = jnp.stack(st_s5, axis=1).astype(x_prompt.dtype)
    new_gdn = jnp.stack(st_gdn, axis=1).astype(x_prompt.dtype)
    return (y_prompt, y_sample, new_hgrn, new_s5, new_gdn)
```

```python
import functools
import math

import jax
import jax.numpy as jnp
from jax import lax
from jax.experimental import pallas as pl
from jax.experimental.pallas import tpu as pltpu

F32 = jnp.float32
BF16 = jnp.bfloat16

EPS = 1e-6
F_FLOOR = 1e-6
LANES = 128
GRID_W = 64
HEAD_D = 128
N_HEADS = 4
BR_WIDTH = 512
S5_GROUPS = 32
S5_GROUP = 16
S5_STATE = 64
S5_GB = 8
S5_SEG = 64
GLA_CHUNK = 256
GDN_CHUNK = 64
GDN_SPAN = 256
TM = 512
VMEM_LIMIT = 56 * 1024 * 1024


def _cp(sem, vmem=VMEM_LIMIT):
    return pltpu.CompilerParams(dimension_semantics=sem, vmem_limit_bytes=vmem)


def _dot(a, b):
    return jnp.dot(a, b, preferred_element_type=F32)


def _dot_nt(a, b):
    return lax.dot_general(a, b, (((1,), (1,)), ((), ())), preferred_element_type=F32)


def _dot_tn(a, b):
    return lax.dot_general(a, b, (((0,), (0,)), ((), ())), preferred_element_type=F32)


def _silu(x):
    return x * jax.nn.sigmoid(x)


def _rms(x, g):
    return x * lax.rsqrt(jnp.mean(x * x, axis=-1, keepdims=True) + EPS) * g


def _exact_mm(mat, x):
    hi = x.astype(BF16)
    r1 = x - hi.astype(F32)
    mid = r1.astype(BF16)
    lo = (r1 - mid.astype(F32)).astype(BF16)
    return _dot(mat, hi) + _dot(mat, mid) + _dot(mat, lo)


def _exact_mm_nt(mat, x):
    hi = x.astype(BF16)
    r1 = x - hi.astype(F32)
    mid = r1.astype(BF16)
    lo = (r1 - mid.astype(F32)).astype(BF16)
    return _dot_nt(mat, hi) + _dot_nt(mat, mid) + _dot_nt(mat, lo)


def _ada_kernel(c_ref, w_ref, b_ref, o_ref):
    c = c_ref[...]
    a = _silu(c).astype(BF16)
    o_ref[...] = _dot(a, w_ref[...].astype(BF16)) + b_ref[...]


def _ada_mod(c8, ada_w, ada_b):
    depth, d, n6 = ada_w.shape
    tn = 512
    return pl.pallas_call(
        _ada_kernel,
        grid=(depth, n6 // tn),
        in_specs=[
            pl.BlockSpec((8, d), lambda l, j: (0, 0)),
            pl.BlockSpec((None, d, tn), lambda l, j: (l, 0, j)),
            pl.BlockSpec((None, 1, tn), lambda l, j: (l, 0, j)),
        ],
        out_specs=pl.BlockSpec((None, 8, tn), lambda l, j: (l, 0, j)),
        out_shape=jax.ShapeDtypeStruct((depth, 8, n6), F32),
        compiler_params=_cp(("parallel", "parallel")),
        name="ada_mod",
    )(c8, ada_w, ada_b.reshape(depth, 1, n6))


def _resnorm_kernel(*refs, has_y, has_h):
    it = iter(refs)
    x_ref = next(it)
    y_ref = next(it) if has_y else None
    mv_ref = next(it)
    xo_ref = next(it) if has_y else None
    h_ref = next(it) if has_h else None
    x = x_ref[...]
    mv = mv_ref[...]
    if has_y:
        x = x + mv[0:1] * y_ref[...]
        xo_ref[...] = x
    if has_h:
        h = _rms(x, mv[3:4]) * (1.0 + mv[1:2]) + mv[2:3]
        h_ref[...] = h.astype(BF16)


def _resnorm(x, y, modv, n_prompt, seq_s, has_h, rows=None):
    d = x.shape[1]
    row0, n = (0, x.shape[0]) if rows is None else rows
    i0 = row0 // TM
    has_y = y is not None

    def group(i):
        r = (i + i0) * TM
        return jnp.where(r < n_prompt, 0, 1 + (r - n_prompt) // seq_s)

    tile = pl.BlockSpec((TM, d), lambda i: (i + i0, 0))
    in_specs = [tile] + ([tile] if has_y else []) + [pl.BlockSpec((None, 8, d), lambda i: (group(i), 0, 0))]
    out_tile = pl.BlockSpec((TM, d), lambda i: (i, 0))
    out_specs, out_shape = [], []
    if has_y:
        out_specs.append(out_tile)
        out_shape.append(jax.ShapeDtypeStruct((n, d), F32))
    if has_h:
        out_specs.append(out_tile)
        out_shape.append(jax.ShapeDtypeStruct((n, d), BF16))
    args = [x] + ([y] if has_y else []) + [modv]
    outs = pl.pallas_call(
        functools.partial(_resnorm_kernel, has_y=has_y, has_h=has_h),
        grid=(n // TM,),
        in_specs=in_specs,
        out_specs=out_specs,
        out_shape=out_shape,
        compiler_params=_cp(("parallel",)),
        name="resnorm",
    )(*args)
    return outs


def _mm_kernel(x_ref, w_ref, o_ref, wb_ref):
    @pl.when(pl.program_id(1) == 0)
    def _():
        wb_ref[...] = w_ref[...].astype(BF16)

    o_ref[...] = _dot(x_ref[...], wb_ref[...])


def _mm(x, w, l, col0, ncols, tn, name, tm=TM):
    m, k = x.shape
    j0 = col0 // tn
    if l is None:
        wspec = pl.BlockSpec((k, tn), lambda j, i: (0, j + j0))
    else:
        wspec = pl.BlockSpec((None, k, tn), lambda j, i: (l, 0, j + j0))
    return pl.pallas_call(
        _mm_kernel,
        grid=(ncols // tn, m // tm),
        in_specs=[pl.BlockSpec((tm, k), lambda j, i: (i, 0)), wspec],
        out_specs=pl.BlockSpec((tm, tn), lambda j, i: (i, j)),
        out_shape=jax.ShapeDtypeStruct((m, ncols), F32),
        scratch_shapes=[pltpu.VMEM((k, tn), BF16)],
        compiler_params=_cp(("arbitrary", "arbitrary")),
        name=name,
    )(x, w)


def _gla_intra(q, k, b, d, c):
    rowi = lax.broadcasted_iota(jnp.int32, (c, LANES), 0)
    ri = lax.broadcasted_iota(jnp.int32, (c, c), 0)
    ci = lax.broadcasted_iota(jnp.int32, (c, c), 1)
    a = jnp.zeros((c, c), F32)
    size = c
    while size >= 8:
        half = size // 2
        nb = c // size
        b3 = b.reshape(nb, size, LANES)
        pos = rowi & (size - 1)
        if d == 0:
            ref = b3[:, half - 1:half, :]
            qsel = pos >= half
        else:
            ref = b3[:, half:half + 1, :]
            qsel = pos < half
        ref = jnp.broadcast_to(ref, (nb, size, LANES)).reshape(c, LANES)
        qs = jnp.where(qsel, q * jnp.exp(jnp.minimum(b - ref, 0.0)), 0.0).astype(BF16)
        ks = jnp.where(qsel, 0.0, k * jnp.exp(jnp.minimum(ref - b, 0.0))).astype(BF16)
        p = _dot_nt(qs, ks)
        if nb > 1:
            shift = int(math.log2(size))
            p = jnp.where((ri >> shift) == (ci >> shift), p, 0.0)
        a = a + p
        size = half
    b3 = b.reshape(c // 8, 8, LANES)
    lo, hi = (0, 4) if d == 0 else (3, 7)
    ref_lo = jnp.broadcast_to(b3[:, lo:lo + 1, :], (c // 8, 8, LANES)).reshape(c, LANES)
    ref_hi = jnp.broadcast_to(b3[:, hi:hi + 1, :], (c // 8, 8, LANES)).reshape(c, LANES)
    ref = jnp.where((rowi & 7) >= 4, ref_hi, ref_lo)
    qs = (q * jnp.exp(jnp.minimum(b - ref, 0.0))).astype(BF16)
    ks = (k * jnp.exp(ref - b)).astype(BF16)
    p = _dot_nt(qs, ks)
    keep = ((ri >> 2) == (ci >> 2)) & ((ci <= ri) if d == 0 else (ci >= ri))
    return a + jnp.where(keep, p, 0.0)


def _hgrn_kernel(aq_ref, af_ref, ab_ref, ai_ref, ag_ref, lb_ref, g_ref, s0_ref, y_ref, s_ref, o_acc, *, t, c):
    nchunk = t // c
    ri = lax.broadcasted_iota(jnp.int32, (c, c), 0)
    ci = lax.broadcasted_iota(jnp.int32, (c, c), 1)
    scale = HEAD_D ** -0.5
    for d in (0, 1):
        z_ref = af_ref if d == 0 else ab_ref
        lbd = lb_ref[d:d + 1, :]
        tri = jnp.where((ci <= ri) if d == 0 else (ci >= ri), 1.0, 0.0).astype(BF16)

        def chunk(i, st, d=d, z_ref=z_ref, lbd=lbd, tri=tri):
            cidx = i if d == 0 else nchunk - 1 - i
            r0 = pl.multiple_of(cidx * c, c)
            z = z_ref[pl.ds(r0, c), :]
            f = lbd + (1.0 - lbd) * jax.nn.sigmoid(z)
            lf = jnp.log(jnp.maximum(f, F_FLOOR))
            k = (1.0 - lbd) * jax.nn.sigmoid(-z)
            q = _silu(aq_ref[pl.ds(r0, c), :]) * scale
            v = ai_ref[pl.ds(r0, c), :].astype(BF16)
            b = _exact_mm(tri, lf)
            a = _gla_intra(q, k, b, d, c)
            o = _dot(a.astype(BF16), v) + _dot_nt((q * jnp.exp(b)).astype(BF16), st.astype(BF16))
            if d == 0:
                o_acc[pl.ds(r0, c), :] = o
            else:
                o_acc[pl.ds(r0, c), :] += o
            blast = b[c - 1:c, :] if d == 0 else b[0:1, :]
            kd = (k * jnp.exp(blast - b)).astype(BF16)
            return st * jnp.exp(blast) + _dot_tn(v, kd)

        st = s0_ref[0, d, 0].T
        if nchunk == 1:
            st = chunk(0, st)
        else:
            st = lax.fori_loop(0, nchunk, chunk, st)
        s_ref[0, d, 0] = st.T
    o = o_acc[...]
    y_ref[...] = _rms(o, g_ref[...]) * _silu(ag_ref[...])


def _shared_out_call(kernel_fn, y_prev, in_specs, args, **kw):
    if y_prev is None:
        return pl.pallas_call(kernel_fn, in_specs=in_specs, **kw)(*args)
    n_in = len(in_specs)

    def body(*refs):
        kernel_fn(*refs[:n_in], *refs[n_in + 1:])

    return pl.pallas_call(body, in_specs=in_specs + [pl.BlockSpec(memory_space=pl.ANY)],
                          input_output_aliases={n_in: 0}, **kw)(*args, y_prev)


def _hgrn(p_main, lb, norm_g, s0, row_blk0, nseq, t, y_prev=None):
    c = min(GLA_CHUNK, t)

    def col(base):
        return pl.BlockSpec((t, HEAD_D), lambda s, h: (row_blk0 + s, base + h))

    st_spec = pl.BlockSpec((1, 2, 1, HEAD_D, HEAD_D), lambda s, h: (s, 0, h, 0, 0))
    return _shared_out_call(
        functools.partial(_hgrn_kernel, t=t, c=c), y_prev,
        [col(0), col(4), col(8), col(12), col(16),
         pl.BlockSpec((2, HEAD_D), lambda s, h: (0, h)),
         pl.BlockSpec((1, HEAD_D), lambda s, h: (0, 0)),
         st_spec],
        (p_main, p_main, p_main, p_main, p_main, lb, norm_g.reshape(1, HEAD_D), s0),
        grid=(nseq, N_HEADS),
        out_specs=[pl.BlockSpec((t, HEAD_D), lambda s, h: (row_blk0 + s, h)), st_spec],
        out_shape=[jax.ShapeDtypeStruct((p_main.shape[0], BR_WIDTH), F32),
                   jax.ShapeDtypeStruct((nseq, 2, N_HEADS, HEAD_D, HEAD_D), F32)],
        scratch_shapes=[pltpu.VMEM((t, HEAD_D), F32)],
        compiler_params=_cp(("parallel", "parallel")),
        name="hgrn2",
    )


def _dot_b(a, b):
    return _dot(a.astype(BF16), b.astype(BF16))


def _tri_solve(ms, xs, c):
    ri = lax.broadcasted_iota(jnp.int32, (c, c), 0)
    ci = lax.broadcasted_iota(jnp.int32, (c, c), 1)
    eye = jnp.where(ri == ci, 1.0, 0.0)
    m8 = [jnp.where((ri >> 3) == (ci >> 3), m, 0.0) for m in ms]
    ts = [eye - m for m in m8]
    ps = [_dot_b(m, m) for m in m8]
    ts = [t + _dot_b(t, p) for t, p in zip(ts, ps)]
    ps = [_dot_b(p, p) for p in ps]
    ts = [t + _dot_b(t, p) for t, p in zip(ts, ps)]
    shift = 3
    while (1 << shift) < c:
        join = ((ri >> (shift + 1)) == (ci >> (shift + 1))) & ((ri >> shift) != (ci >> shift))
        lts = [_dot_b(jnp.where(join, m, 0.0), t) for m, t in zip(ms, ts)]
        ts = [t - _dot_b(t, lt) for t, lt in zip(ts, lts)]
        shift += 1
    return [_dot_b(t, x) for t, x in zip(ts, xs)]


def _chunk_cumsum(x, d, c):
    n = x.shape[0]
    pos = lax.broadcasted_iota(jnp.int32, x.shape, 0) & (c - 1)
    s = 1
    while s < c:
        if d == 0:
            x = x + jnp.where(pos >= s, pltpu.roll(x, s, 0), 0.0)
        else:
            x = x + jnp.where(pos < c - s, pltpu.roll(x, n - s, 0), 0.0)
        s *= 2
    return x


def _gdn_kernel(cq_ref, ck_ref, cv_ref, cg_ref, ab_ref, cw_ref, alog_ref, dtb_ref, g_ref, s0_ref,
                y_ref, s_ref, q_s, k_s, v_s, la_s, be_s, w_s, qk_s, qe_s, kd_s, dec_s, o_acc, *, t, c, span):
    h = pl.program_id(1)
    nchunk = t // c
    cps = span // c
    rowt = lax.broadcasted_iota(jnp.int32, (t, HEAD_D), 0)
    lane = lax.broadcasted_iota(jnp.int32, (t, LANES), 1)
    ri = lax.broadcasted_iota(jnp.int32, (span, span), 0)
    ci = lax.broadcasted_iota(jnp.int32, (span, span), 1)
    shift = int(math.log2(c))
    same = (ri >> shift) == (ci >> shift)

    def conv(x_ref, j):
        x = x_ref[...]
        w = cw_ref[j]
        xm = jnp.where(rowt == 0, 0.0, pltpu.roll(x, 1, 0))
        xp = jnp.where(rowt == t - 1, 0.0, pltpu.roll(x, t - 1, 0))
        return xm * w[0:1] + x * w[1:2] + xp * w[2:3]

    q = _silu(conv(cq_ref, 0))
    q_s[...] = q * lax.rsqrt(jnp.sum(q * q, axis=-1, keepdims=True) + EPS) * (HEAD_D ** -0.5)
    k = _silu(conv(ck_ref, 1))
    k_s[...] = k * lax.rsqrt(jnp.sum(k * k, axis=-1, keepdims=True) + EPS)
    v_s[...] = _silu(conv(cv_ref, 2))
    ab = ab_ref[...]

    for d in (0, 1):
        a_col = jnp.sum(jnp.where(lane == d * N_HEADS + h, ab, 0.0), axis=-1, keepdims=True)
        b_col = jnp.sum(jnp.where(lane == 2 * N_HEADS + d * N_HEADS + h, ab, 0.0), axis=-1, keepdims=True)
        coef = -jnp.exp(jnp.full((1, LANES), alog_ref[d, h], F32))
        xs = a_col + dtb_ref[d, h]
        softplus = jnp.maximum(xs, 0.0) + jnp.log1p(jnp.exp(-jnp.abs(xs)))
        la_s[d] = coef * jnp.broadcast_to(softplus, (t, LANES))
        be_s[d] = jnp.broadcast_to(jax.nn.sigmoid(b_col), (t, LANES))
    o_acc[...] = jnp.zeros((t, HEAD_D), F32)

    def wy_span(sp, carry):
        r0 = pl.multiple_of(sp * span, span)
        qsp = q_s[pl.ds(r0, span), :]
        ksp = k_s[pl.ds(r0, span), :]
        vsp = v_s[pl.ds(r0, span), :]
        kb = ksp.astype(BF16)
        kk = _dot_nt(kb, kb)
        qk = _dot_nt(qsp.astype(BF16), kb)
        ms, xs, dst = [], [], []
        for d in (0, 1):
            be = be_s[d, pl.ds(r0, span), :]
            g = _chunk_cumsum(la_s[d, pl.ds(r0, span), :], d, c)
            gt = g.T
            gcol = jnp.concatenate([g] * (span // LANES), axis=1)
            grow = jnp.concatenate([gt] * (span // LANES), axis=0)
            incl = same & ((ci <= ri) if d == 0 else (ci >= ri))
            gamma = jnp.where(incl, jnp.exp(jnp.minimum(gcol - grow, 0.0)), 0.0)
            bcol = jnp.concatenate([be] * (span // LANES), axis=1)
            m_sp = jnp.where(ci == ri, 0.0, bcol * kk * gamma)
            qkg = (qk * gamma).astype(BF16)
            eg = jnp.exp(g)
            xv = be * vsp
            xk = be * eg * ksp
            qe_s[d, pl.ds(r0, span), :] = (qsp * eg).astype(BF16)
            for j in range(cps):
                lo = j * c
                rj = pl.multiple_of(r0 + lo, c)
                ms.append(m_sp[lo:lo + c, lo:lo + c])
                xs.append(jnp.concatenate([xv[lo:lo + c], xk[lo:lo + c]], axis=1))
                dst.append((d, rj))
                qk_s[d, pl.ds(rj, c), 0:c] = qkg[lo:lo + c, lo:lo + c]
                gj = g[lo:lo + c]
                glast = gj[c - 1:c] if d == 0 else gj[0:1]
                kd_s[d, pl.ds(rj, c), :] = (ksp[lo:lo + c] * jnp.exp(glast - gj)).astype(BF16)
                dec_s[d, pl.ds(sp * cps + j, 1), :] = jnp.exp(glast)
        for (d, rj), w in zip(dst, _tri_solve(ms, xs, c)):
            w_s[d, pl.ds(rj, c), :] = w
        return carry

    if t == span:
        wy_span(0, 0)
    else:
        lax.fori_loop(0, t // span, wy_span, 0)

    def scan_step(i, carry):
        cidx = (i, nchunk - 1 - i)
        r0 = [pl.multiple_of(ck * c, c) for ck in cidx]
        sb = [s.astype(BF16) for s in carry]
        ws = [w_s[d, pl.ds(r0[d], c), :] for d in (0, 1)]
        wks = [_dot(ws[d][:, HEAD_D:].astype(BF16), sb[d]) for d in (0, 1)]
        oq = [_dot(qe_s[d, pl.ds(r0[d], c), :], sb[d]) for d in (0, 1)]
        vnb = [(ws[d][:, :HEAD_D] - wks[d]).astype(BF16) for d in (0, 1)]
        new = [carry[d] * dec_s[d, pl.ds(cidx[d], 1), :] + _dot_tn(kd_s[d, pl.ds(r0[d], c), :], vnb[d])
               for d in (0, 1)]
        for d in (0, 1):
            o_acc[pl.ds(r0[d], c), :] += oq[d] + _dot(qk_s[d, pl.ds(r0[d], c), 0:c], vnb[d])
        return tuple(new)

    s_f, s_b = lax.fori_loop(0, nchunk, scan_step, (s0_ref[0, 0, 0], s0_ref[0, 1, 0]),
                             unroll=nchunk <= 4)
    s_ref[0, 0, 0] = s_f
    s_ref[0, 1, 0] = s_b
    y_ref[...] = _rms(o_acc[...], g_ref[...]) * _silu(cg_ref[...])


def _gdn(p_main, p_ab, conv_w, a_log, dt_bias, norm_g, s0, row_blk0, nseq, t, y_prev=None):
    c = GDN_CHUNK

    def col(base):
        return pl.BlockSpec((t, HEAD_D), lambda s, h: (row_blk0 + s, base + h))

    st_spec = pl.BlockSpec((1, 2, 1, HEAD_D, HEAD_D), lambda s, h: (s, 0, h, 0, 0))
    smem = pl.BlockSpec(memory_space=pltpu.SMEM)
    seq_buf = pltpu.VMEM((t, HEAD_D), F32)
    dir_f32 = pltpu.VMEM((2, t, LANES), F32)
    dir_bf16 = pltpu.VMEM((2, t, LANES), BF16)
    scratch = [seq_buf, seq_buf, seq_buf, dir_f32, dir_f32,
               pltpu.VMEM((2, t, 2 * HEAD_D), F32), dir_bf16, dir_bf16, dir_bf16,
               pltpu.VMEM((2, max(t // c, 8), LANES), F32), seq_buf]
    return _shared_out_call(
        functools.partial(_gdn_kernel, t=t, c=c, span=GDN_SPAN), y_prev,
        [col(24), col(28), col(32), col(36),
         pl.BlockSpec((t, LANES), lambda s, h: (row_blk0 + s, 0)),
         pl.BlockSpec((3, 3, HEAD_D), lambda s, h: (0, 0, h)),
         smem, smem,
         pl.BlockSpec((1, HEAD_D), lambda s, h: (0, 0)),
         st_spec],
        (p_main, p_main, p_main, p_main, p_ab, conv_w, a_log, dt_bias, norm_g.reshape(1, HEAD_D), s0),
        grid=(nseq, N_HEADS),
        out_specs=[pl.BlockSpec((t, HEAD_D), lambda s, h: (row_blk0 + s, h)), st_spec],
        out_shape=[jax.ShapeDtypeStruct((p_main.shape[0], BR_WIDTH), F32),
                   jax.ShapeDtypeStruct((nseq, 2, N_HEADS, HEAD_D, HEAD_D), F32)],
        scratch_shapes=scratch,
        compiler_params=_cp(("parallel", "parallel")),
        name="gdn",
    )


def _cmul(ar, ai, br, bi):
    return ar * br - ai * bi, ar * bi + ai * br


def _s5_kernel(u_ref, bb_ref, cc_ref, lam_ref, dsk_ref, s0_ref, y_ref, sf_ref, up_s, x_s, yp_s,
               *, tb, seg, segs_per_seq, zero_init):
    r_par = tb // seg
    half = S5_GB * S5_STATE
    for r in range(seg):
        up_s[r * r_par:(r + 1) * r_par, :] = u_ref[pl.ds(r, r_par, stride=seg), :]
    ub = up_s[...].astype(BF16)
    sf_ref[...] = jnp.zeros(sf_ref.shape, F32)
    nseq = r_par // segs_per_seq
    lam = [(lam_ref[0, d, 0:1, :], lam_ref[0, d, 1:2, :]) for d in (0, 1)]
    for d in (0, 1):
        x_s[d] = _dot(ub, bb_ref[0, d])

    def offsets(i):
        return [pl.multiple_of(r * r_par, r_par) for r in (i, seg - 1 - i)]

    def local_step(i, carry):
        off = offsets(i)
        new = []
        for d in (0, 1):
            (lre, lim), (hre, him) = lam[d], carry[d]
            nre = lre * hre - lim * him + x_s[d, pl.ds(off[d], r_par), 0:half]
            nim = lre * him + lim * hre + x_s[d, pl.ds(off[d], r_par), half:2 * half]
            x_s[d, pl.ds(off[d], r_par), 0:half] = nre
            x_s[d, pl.ds(off[d], r_par), half:2 * half] = nim
            new.append((nre, nim))
        return tuple(new)

    zero = jnp.zeros((r_par, half), F32)
    lax.fori_loop(0, seg, local_step, ((zero, zero), (zero, zero)))

    end_rows = ((seg - 1) * r_par, 0)
    needs_carry = not (zero_init and segs_per_seq == 1)
    hins = []
    for d in (0, 1):
        if not needs_carry:
            break
        lre, lim = lam[d]
        pre, pim = lre, lim
        n = 1
        while n < seg:
            pre, pim = _cmul(pre, pim, pre, pim)
            n *= 2
        end_row = end_rows[d]
        ere = x_s[d, end_row:end_row + r_par, 0:half]
        eim = x_s[d, end_row:end_row + r_par, half:2 * half]
        hin = [None] * r_par
        order = range(r_par) if d == 0 else range(r_par - 1, -1, -1)
        for j in order:
            sq, pos = divmod(j, segs_per_seq)
            first = pos == (0 if d == 0 else segs_per_seq - 1)
            if first:
                hin[j] = (s0_ref[0, 0, d, sq:sq + 1, 0:half], s0_ref[0, 0, d, sq:sq + 1, half:2 * half])
            else:
                pj = j - 1 if d == 0 else j + 1
                cre, cim = _cmul(pre, pim, hin[pj][0], hin[pj][1])
                hin[j] = (cre + ere[pj:pj + 1], cim + eim[pj:pj + 1])
        hins.append((jnp.concatenate([hj[0] for hj in hin], axis=0),
                     jnp.concatenate([hj[1] for hj in hin], axis=0)))

    def carry_step(i, pws):
        off = offsets(i)
        new = []
        for d in (0, 1):
            (lre, lim), (wre, wim) = lam[d], pws[d]
            are, aim = _cmul(wre, wim, hins[d][0], hins[d][1])
            x_s[d, pl.ds(off[d], r_par), 0:half] += are
            x_s[d, pl.ds(off[d], r_par), half:2 * half] += aim
            new.append(_cmul(wre, wim, lre, lim))
        return tuple(new)

    if needs_carry:
        lax.fori_loop(0, seg, carry_step, (lam[0], lam[1]))

    for d in (0, 1):
        for sq in range(nseq):
            j = (sq + 1) * segs_per_seq - 1 if d == 0 else sq * segs_per_seq
            row = end_rows[d] + j
            sf_ref[0, 0, d, sq:sq + 1, :] = x_s[d, row:row + 1, :]
    yp_s[...] = (_dot(x_s[0].astype(BF16), cc_ref[0, 0]) + _dot(x_s[1].astype(BF16), cc_ref[0, 1])
                 + up_s[...] * dsk_ref[...])
    for r in range(seg):
        y_ref[pl.ds(r, r_par, stride=seg), :] = yp_s[r * r_par:(r + 1) * r_par, :]


def _s5(p_main, bb, cc, lam, dskip, s0, row_blk0, nblk, tb, t, seg, zero_init, y_prev=None):
    ngb = S5_GROUPS // S5_GB
    w2 = 2 * S5_GB * S5_STATE
    return _shared_out_call(
        functools.partial(_s5_kernel, tb=tb, seg=seg, segs_per_seq=t // seg, zero_init=zero_init), y_prev,
        [
            pl.BlockSpec((tb, LANES), lambda i, gb: (row_blk0 + i, 20 + gb)),
            pl.BlockSpec((1, 2, LANES, w2), lambda i, gb: (gb, 0, 0, 0)),
            pl.BlockSpec((1, 2, w2, LANES), lambda i, gb: (gb, 0, 0, 0)),
            pl.BlockSpec((1, 2, 2, w2 // 2), lambda i, gb: (gb, 0, 0, 0)),
            pl.BlockSpec((1, LANES), lambda i, gb: (0, gb)),
            pl.BlockSpec((1, 1, 2, 8, w2), lambda i, gb: (i, gb, 0, 0, 0)),
        ],
        (p_main, bb, cc, lam, dskip.reshape(1, BR_WIDTH), s0),
        grid=(nblk, ngb),
        out_specs=[pl.BlockSpec((tb, LANES), lambda i, gb: (row_blk0 + i, gb)),
                   pl.BlockSpec((1, 1, 2, 8, w2), lambda i, gb: (i, gb, 0, 0, 0))],
        out_shape=[jax.ShapeDtypeStruct((p_main.shape[0], BR_WIDTH), F32),
                   jax.ShapeDtypeStruct((nblk, ngb, 2, 8, w2), F32)],
        scratch_shapes=[pltpu.VMEM((tb, LANES), F32), pltpu.VMEM((2, tb, w2), F32), pltpu.VMEM((tb, LANES), F32)],
        compiler_params=_cp(("parallel", "parallel")),
        name="s5",
    )


def _s5_params(a_re, a_im, log_dt, b_re, b_im, c_re, c_im):
    dt = jnp.exp(log_dt)[..., None]
    mag = jnp.exp(dt * a_re)
    lam_re = mag * jnp.cos(dt * a_im)
    lam_im = mag * jnp.sin(dt * a_im)
    den = a_re * a_re + a_im * a_im
    co_re = ((lam_re - 1.0) * a_re + lam_im * a_im) / den
    co_im = (lam_im * a_re - (lam_re - 1.0) * a_im) / den
    bb_re = co_re[..., None] * b_re - co_im[..., None] * b_im
    bb_im = co_re[..., None] * b_im + co_im[..., None] * b_re
    ngb = S5_GROUPS // S5_GB
    eye = jnp.eye(S5_GB, dtype=F32)

    def pack_in(w):
        w = w.reshape(2, ngb, S5_GB, S5_STATE, S5_GROUP)
        blk = jnp.einsum('dbgpi,gh->bdgihp', w, eye)
        return blk.reshape(ngb, 2, S5_GB * S5_GROUP, S5_GB * S5_STATE)

    def pack_out(w):
        w = w.reshape(2, ngb, S5_GB, S5_GROUP, S5_STATE)
        blk = jnp.einsum('dbgip,gh->bdgphi', w, eye)
        return blk.reshape(ngb, 2, S5_GB * S5_STATE, S5_GB * S5_GROUP)

    bb = jnp.concatenate([pack_in(bb_re), pack_in(bb_im)], axis=-1).astype(BF16)
    cc = jnp.concatenate([pack_out(c_re), -pack_out(c_im)], axis=-2).astype(BF16)
    lam = jnp.stack([lam_re, lam_im], axis=1)
    lam = lam.reshape(2, 2, ngb, S5_GB * S5_STATE).transpose(2, 0, 1, 3)
    return bb, cc, lam


def _glu_kernel(y_ref, w_ref, b_ref, o_ref, wb_ref):
    @pl.when(pl.program_id(0) == 0)
    def _():
        wb_ref[...] = w_ref[...].astype(BF16)

    z = _dot(y_ref[...].astype(BF16), wb_ref[...]) + b_ref[...]
    half = z.shape[1] // 2
    o_ref[...] = z[:, :half] * jax.nn.sigmoid(z[:, half:])


def _glu(y, w, b, l):
    n, k = y.shape
    n2 = w.shape[-1]
    return pl.pallas_call(
        _glu_kernel,
        grid=(n // TM,),
        in_specs=[pl.BlockSpec((TM, k), lambda i: (i, 0)),
                  pl.BlockSpec((None, k, n2), lambda i: (l, 0, 0)),
                  pl.BlockSpec((None, 1, n2), lambda i: (l, 0, 0))],
        out_specs=pl.BlockSpec((TM, n2 // 2), lambda i: (i, 0)),
        out_shape=jax.ShapeDtypeStruct((n, n2 // 2), F32),
        scratch_shapes=[pltpu.VMEM((k, n2), BF16)],
        compiler_params=_cp(("arbitrary",)),
        name="s5_glu",
    )(y, w, b.reshape(b.shape[0], 1, n2))


def _merge_kernel(ya_ref, yb_ref, yc_ref, mg_ref, wbr_ref, wout_ref, g_ref, o_ref, wbr_s, wout_s):
    @pl.when(pl.program_id(0) == 0)
    def _():
        wbr_s[...] = wbr_ref[...].astype(BF16)
        wout_s[...] = wout_ref[...].astype(BF16)

    d = wout_s.shape[0]
    acc = None
    for n, br_ref in enumerate((ya_ref, yb_ref, yc_ref)):
        proj = _dot(br_ref[...].astype(BF16), wbr_s[n])
        term = proj * jax.nn.sigmoid(mg_ref[:, n * d:(n + 1) * d])
        acc = term if acc is None else acc + term
    y = _dot(acc.astype(BF16), wout_s[...])
    o_ref[...] = _rms(y, g_ref[...])


def _merge(ya, yb, yc, mg, w_branch, w_out, g, l):
    n = ya.shape[0]
    d = w_out.shape[-1]
    tm = 256
    br = pl.BlockSpec((tm, BR_WIDTH), lambda i: (i, 0))
    return pl.pallas_call(
        _merge_kernel,
        grid=(n // tm,),
        in_specs=[br, br, br,
                  pl.BlockSpec((tm, 3 * d), lambda i: (i, 0)),
                  pl.BlockSpec((None, 3, BR_WIDTH, d), lambda i: (l, 0, 0, 0)),
                  pl.BlockSpec((None, d, d), lambda i: (l, 0, 0)),
                  pl.BlockSpec((1, d), lambda i: (0, 0))],
        out_specs=pl.BlockSpec((tm, d), lambda i: (i, 0)),
        out_shape=jax.ShapeDtypeStruct((n, d), F32),
        scratch_shapes=[pltpu.VMEM((3, BR_WIDTH, d), BF16), pltpu.VMEM((d, d), BF16)],
        compiler_params=_cp(("arbitrary",)),
        name="merge",
    )(ya, yb, yc, mg, w_branch, w_out, g.reshape(1, d))


def _swiglu_kernel(*refs, gated):
    if gated:
        x_ref, wg_ref, wu_ref, gt_ref, o_ref, wg_s, wu_s = refs
    else:
        x_ref, wg_ref, wu_ref, o_ref, wg_s, wu_s = refs

    @pl.when(pl.program_id(1) == 0)
    def _():
        wg_s[...] = wg_ref[...].astype(BF16)
        wu_s[...] = wu_ref[...].astype(BF16)

    x = x_ref[...]
    hid = _silu(_dot(x, wg_s[...])) * _dot(x, wu_s[...])
    if gated:
        gt = gt_ref[...]
        lane = lax.broadcasted_iota(jnp.int32, gt.shape, 1)
        hid = hid * jnp.sum(jnp.where(lane == pl.program_id(0), gt, 0.0), axis=-1, keepdims=True)
    o_ref[...] = hid.astype(BF16)


def _swiglu_up(x, wg, wu, e, f, gates=None):
    n, d = x.shape
    gated = gates is not None
    if gated:
        ne = wg.shape[1]
        wspec = pl.BlockSpec((None, None, d, f), lambda j, i: (e, j, 0, 0))
    else:
        ne = wg.shape[-1] // f
        wspec = pl.BlockSpec((None, d, f), lambda j, i: (e, 0, j))
    in_specs = [pl.BlockSpec((TM, d), lambda j, i: (i, 0)), wspec, wspec]
    args = [x, wg, wu]
    if gated:
        in_specs.append(pl.BlockSpec((TM, LANES), lambda j, i: (i, 0)))
        args.append(gates)
    return pl.pallas_call(
        functools.partial(_swiglu_kernel, gated=gated),
        grid=(ne, n // TM),
        in_specs=in_specs,
        out_specs=pl.BlockSpec((TM, f), lambda j, i: (i, j)),
        out_shape=jax.ShapeDtypeStruct((n, ne * f), BF16),
        scratch_shapes=[pltpu.VMEM((d, f), BF16), pltpu.VMEM((d, f), BF16)],
        compiler_params=_cp(("arbitrary", "arbitrary")),
        name="swiglu_up",
    )(*args)


def _down_kernel(x_ref, w_ref, g_ref, o_ref, acc):
    k = pl.program_id(1)
    part = _dot(x_ref[...], w_ref[...].astype(BF16))

    @pl.when(k == 0)
    def _():
        acc[...] = part

    @pl.when(k > 0)
    def _():
        acc[...] += part

    @pl.when(k == pl.num_programs(1) - 1)
    def _():
        o_ref[...] = _rms(acc[...], g_ref[...])


def _down_norm(hid, wd, e, g, tm):
    n = hid.shape[0]
    _, ne, f, d = wd.shape
    return pl.pallas_call(
        _down_kernel,
        grid=(n // tm, ne),
        in_specs=[pl.BlockSpec((tm, f), lambda i, k: (i, k)),
                  pl.BlockSpec((None, None, f, d), lambda i, k: (e, k, 0, 0)),
                  pl.BlockSpec((1, d), lambda i, k: (0, 0))],
        out_specs=pl.BlockSpec((tm, d), lambda i, k: (i, 0)),
        out_shape=jax.ShapeDtypeStruct((n, d), F32),
        scratch_shapes=[pltpu.VMEM((tm, d), F32)],
        compiler_params=_cp(("parallel", "arbitrary")),
        name="down_norm",
    )(hid, wd, g.reshape(1, d))


def _router_kernel(x_ref, w_ref, o_ref, *, ne):
    logits = _dot(x_ref[...], w_ref[...].astype(BF16))
    lane = lax.broadcasted_iota(jnp.int32, logits.shape, 1).astype(F32)
    neg = jnp.float32(-jnp.inf)
    logits = jnp.where(lane < ne, logits, neg)
    m1 = jnp.max(logits, axis=-1, keepdims=True)
    i1 = jnp.min(jnp.where(logits == m1, lane, float(LANES)), axis=-1, keepdims=True)
    rest = jnp.where(lane == i1, neg, logits)
    m2 = jnp.max(rest, axis=-1, keepdims=True)
    i2 = jnp.min(jnp.where(rest == m2, lane, float(LANES)), axis=-1, keepdims=True)
    e2 = jnp.exp(m2 - m1)
    w1 = 1.0 / (1.0 + e2)
    w2 = e2 / (1.0 + e2)
    o_ref[...] = jnp.where(lane == i1, w1, 0.0) + jnp.where(lane == i2, w2, 0.0)


def _router(x, w_pad, ne):
    n, d = x.shape
    return pl.pallas_call(
        functools.partial(_router_kernel, ne=ne),
        grid=(n // TM,),
        in_specs=[pl.BlockSpec((TM, d), lambda i: (i, 0)),
                  pl.BlockSpec((d, LANES), lambda i: (0, 0))],
        out_specs=pl.BlockSpec((TM, LANES), lambda i: (i, 0)),
        out_shape=jax.ShapeDtypeStruct((n, LANES), F32),
        compiler_params=_cp(("parallel",)),
        name="router",
    )(x, w_pad)


def _to_column_major(x, b, t):
    d = x.shape[-1]
    return x.reshape(b, t // GRID_W, GRID_W, d).transpose(0, 2, 1, 3).reshape(b * t, d)


def _to_row_major(x, b, t):
    d = x.shape[-1]
    return x.reshape(b, GRID_W, t // GRID_W, d).transpose(0, 2, 1, 3).reshape(b * t, d)


def kernel(x_prompt, x_sample, state_hgrn, state_s5, state_gdn, c, c_ctx, ada_w, ada_b, norm_g, w_in, hgrn_lb, hgrn_norm, s5_a_re, s5_a_im, s5_log_dt, s5_b_re, s5_b_im, s5_c_re, s5_c_im, s5_d, s5_glu_w, s5_glu_b, gdn_conv, gdn_a_log, gdn_dt_bias, gdn_norm, w_branch, w_out, ffn_wg, ffn_wu, ffn_wd, moe_router, moe_wg, moe_wu, moe_wd):
    bp, tp, d = x_prompt.shape
    bs, ts, _ = x_sample.shape
    depth = w_in.shape[0]
    n_p, n_s = bp * tp, bs * ts
    ne = moe_router.shape[-1]
    main_cols = 10 * BR_WIDTH
    ab_cols = 4 * N_HEADS

    c8 = jnp.zeros((8, d), F32).at[0].set(c_ctx).at[1:1 + bs].set(c)
    mod_all = _ada_mod(c8, ada_w, ada_b)

    lb_soft = jax.nn.softmax(hgrn_lb.astype(F32), axis=0)
    lb_all = jnp.cumsum(lb_soft, axis=0) - lb_soft[:1]

    mods = mod_all[:, :1 + bs].reshape(depth, 1 + bs, 6, d)
    zero_row = jnp.zeros((1 + bs, d), F32)

    def modv(gate, scale, shift, g):
        rows = [zero_row if sel is None else mods[sel[0], :, sel[1]] for sel in (gate, scale, shift)]
        rows.append(zero_row if g is None else jnp.broadcast_to(g, (1 + bs, d)))
        return jnp.stack(rows + [zero_row] * 4, axis=1)

    x = jnp.concatenate([x_prompt.reshape(n_p, d), x_sample.reshape(n_s, d)], axis=0)
    (h,) = _resnorm(x, None, modv(None, (0, 1), (0, 0), norm_g[0, 0]), n_p, ts, True)

    z_hgrn = jnp.zeros((bp, 2, N_HEADS, HEAD_D, HEAD_D), F32)
    ngb = S5_GROUPS // S5_GB
    w2 = 2 * S5_GB * S5_STATE
    st_hgrn, st_s5, st_gdn = [], [], []
    s5_tb = 1024

    for l in range(depth):
        col_major = l % 2 == 1
        if col_major:
            h = jnp.concatenate([h[:n_p], _to_column_major(h[n_p:], bs, ts)], axis=0)
        p_main = _mm(h, w_in, l, 0, main_cols, 2 * BR_WIDTH, "w_in_main", tm=1024)
        w_ab = jnp.pad(w_in[l, :, main_cols:main_cols + ab_cols], ((0, 0), (0, LANES - ab_cols)))
        p_ab = _mm(h, w_ab, None, 0, LANES, LANES, "w_in_ab", tm=1024)
        p_mg = _mm(h, w_in[l, :, main_cols + ab_cols:], None, 0, 3 * d, 2 * BR_WIDTH, "w_in_merge", tm=1024)

        ya, sa = _hgrn(p_main, lb_all[l], hgrn_norm[l], z_hgrn, 0, bp, tp)
        ya, _ = _hgrn(p_main, lb_all[l], hgrn_norm[l], state_hgrn[:, l].astype(F32), n_p // ts, bs, ts, ya)
        st_hgrn.append(sa)

        bb, cc, lam = _s5_params(s5_a_re[l], s5_a_im[l], s5_log_dt[l], s5_b_re[l], s5_b_im[l],
                                 s5_c_re[l], s5_c_im[l])
        seq_per_blk = 8
        tb_p = seq_per_blk * tp
        nblk_p = n_p // tb_p
        yb, sb = _s5(p_main, bb, cc, lam, s5_d[l], jnp.zeros((nblk_p, ngb, 2, 8, w2), F32),
                     0, nblk_p, tb_p, tp, tp, True)
        s0s = state_s5[:, l].astype(F32)
        s0s = s0s.reshape(bs, 2, ngb, S5_GB * S5_STATE, 2).transpose(0, 2, 1, 4, 3).reshape(bs, ngb, 2, 1, w2)
        s0s = jnp.pad(s0s, ((0, 0), (0, 0), (0, 0), (0, 7), (0, 0)))
        yb, _ = _s5(p_main, bb, cc, lam, s5_d[l], s0s, n_p // s5_tb, n_s // s5_tb, s5_tb, ts, S5_SEG, False, yb)
        yb = _glu(yb, s5_glu_w, s5_glu_b, l)
        sb = sb[:, :, :, :seq_per_blk].reshape(nblk_p, ngb, 2, seq_per_blk, 2, S5_GB, S5_STATE)
        sb = sb.transpose(0, 3, 2, 1, 5, 6, 4).reshape(bp, 2, S5_GROUPS, S5_STATE, 2)
        st_s5.append(sb)

        yc, sc = _gdn(p_main, p_ab, gdn_conv[l], gdn_a_log[l], gdn_dt_bias[l], gdn_norm[l], z_hgrn, 0, bp, tp)
        yc, _ = _gdn(p_main, p_ab, gdn_conv[l], gdn_a_log[l], gdn_dt_bias[l], gdn_norm[l],
                     state_gdn[:, l].astype(F32), n_p // ts, bs, ts, yc)
        st_gdn.append(sc)

        r_m = _merge(ya, yb, yc, p_mg, w_branch, w_out, norm_g[l, 1], l)
        if col_major:
            r_m = jnp.concatenate([r_m[:n_p], _to_row_major(r_m[n_p:], bs, ts)], axis=0)
        x, h2 = _resnorm(x, r_m, modv((l, 2), (l, 4), (l, 3), norm_g[l, 2]), n_p, ts, True)

        if l % 2 == 0:
            e = l // 2
            hid = _swiglu_up(h2, ffn_wg, ffn_wu, e, ffn_wg.shape[-1] // 2)
            r_f = _down_norm(hid, ffn_wd[:, None], e, norm_g[l, 3], TM)
        else:
            e = l // 2
            gates = _router(h2, jnp.pad(moe_router[e], ((0, 0), (0, LANES - ne))), ne)
            hid = _swiglu_up(h2, moe_wg, moe_wu, e, moe_wg.shape[-1], gates)
            r_f = _down_norm(hid, moe_wd, e, norm_g[l, 3], 1024)

        if l + 1 < depth:
            x, h = _resnorm(x, r_f, modv((l, 5), (l + 1, 1), (l + 1, 0), norm_g[l + 1, 0]), n_p, ts, True)
        else:
            mv = modv((l, 5), None, None, None)
            (y_prompt,) = _resnorm(x, r_f, mv, n_p, ts, False, rows=(0, n_p))
            (y_sample,) = _resnorm(x, r_f, mv, n_p, ts, False, rows=(n_p, n_s))

    y_prompt = y_prompt.reshape(bp, tp, d)
    y_sample = y_sample.reshape(bs, ts, d)
    new_hgrn = jnp.stack(st_hgrn, axis=1).astype(x_prompt.dtype)
    new_s5 = jnp.stack(st_s5, axis=1).astype(x_prompt.dtype)
    new_gdn = jnp.stack(st_gdn, axis=1).astype(x_prompt.dtype)
    return (y_prompt, y_sample, new_hgrn, new_s5, new_gdn)
```

```python
import functools
import math

import jax
import jax.numpy as jnp
from jax import lax
from jax.experimental import pallas as pl
from jax.experimental.pallas import tpu as pltpu

F32 = jnp.float32
BF16 = jnp.bfloat16

EPS = 1e-6
F_FLOOR = 1e-6
LANES = 128
GRID_W = 64
HEAD_D = 128
N_HEADS = 4
BR_WIDTH = 512
S5_GROUPS = 32
S5_GROUP = 16
S5_STATE = 64
S5_GB = 8
S5_SEG = 64
GLA_CHUNK = 256
GDN_CHUNK = 64
GDN_SPAN = 256
GDN_HEADS_PER_STEP = 4
GDN_MAX_ROWS_PER_STEP = 2048
TM = 512
VMEM_LIMIT = 56 * 1024 * 1024


def _cp(sem, vmem=VMEM_LIMIT):
    return pltpu.CompilerParams(dimension_semantics=sem, vmem_limit_bytes=vmem)


def _dot(a, b):
    return jnp.dot(a, b, preferred_element_type=F32)


def _dot_nt(a, b):
    return lax.dot_general(a, b, (((1,), (1,)), ((), ())), preferred_element_type=F32)


def _dot_tn(a, b):
    return lax.dot_general(a, b, (((0,), (0,)), ((), ())), preferred_element_type=F32)


def _silu(x):
    return x * jax.nn.sigmoid(x)


def _rms(x, g):
    return x * lax.rsqrt(jnp.mean(x * x, axis=-1, keepdims=True) + EPS) * g


def _exact_mm(mat, x):
    hi = x.astype(BF16)
    r1 = x - hi.astype(F32)
    mid = r1.astype(BF16)
    lo = (r1 - mid.astype(F32)).astype(BF16)
    return _dot(mat, hi) + _dot(mat, mid) + _dot(mat, lo)


def _exact_mm_nt(mat, x):
    hi = x.astype(BF16)
    r1 = x - hi.astype(F32)
    mid = r1.astype(BF16)
    lo = (r1 - mid.astype(F32)).astype(BF16)
    return _dot_nt(mat, hi) + _dot_nt(mat, mid) + _dot_nt(mat, lo)


def _ada_kernel(c_ref, w_ref, b_ref, o_ref):
    c = c_ref[...]
    a = _silu(c).astype(BF16)
    o_ref[...] = _dot(a, w_ref[...].astype(BF16)) + b_ref[...]


def _ada_mod(c8, ada_w, ada_b):
    depth, d, n6 = ada_w.shape
    tn = 512
    return pl.pallas_call(
        _ada_kernel,
        grid=(depth, n6 // tn),
        in_specs=[
            pl.BlockSpec((8, d), lambda l, j: (0, 0)),
            pl.BlockSpec((None, d, tn), lambda l, j: (l, 0, j)),
            pl.BlockSpec((None, 1, tn), lambda l, j: (l, 0, j)),
        ],
        out_specs=pl.BlockSpec((None, 8, tn), lambda l, j: (l, 0, j)),
        out_shape=jax.ShapeDtypeStruct((depth, 8, n6), F32),
        compiler_params=_cp(("parallel", "parallel")),
        name="ada_mod",
    )(c8, ada_w, ada_b.reshape(depth, 1, n6))


def _resnorm_kernel(*refs, has_y, has_h):
    it = iter(refs)
    x_ref = next(it)
    y_ref = next(it) if has_y else None
    mv_ref = next(it)
    xo_ref = next(it) if has_y else None
    h_ref = next(it) if has_h else None
    x = x_ref[...]
    mv = mv_ref[...]
    if has_y:
        x = x + mv[0:1] * y_ref[...]
        xo_ref[...] = x
    if has_h:
        h = _rms(x, mv[3:4]) * (1.0 + mv[1:2]) + mv[2:3]
        h_ref[...] = h.astype(BF16)


def _resnorm(x, y, modv, n_prompt, seq_s, has_h, rows=None):
    d = x.shape[1]
    row0, n = (0, x.shape[0]) if rows is None else rows
    i0 = row0 // TM
    has_y = y is not None

    def group(i):
        r = (i + i0) * TM
        return jnp.where(r < n_prompt, 0, 1 + (r - n_prompt) // seq_s)

    tile = pl.BlockSpec((TM, d), lambda i: (i + i0, 0))
    in_specs = [tile] + ([tile] if has_y else []) + [pl.BlockSpec((None, 8, d), lambda i: (group(i), 0, 0))]
    out_tile = pl.BlockSpec((TM, d), lambda i: (i, 0))
    out_specs, out_shape = [], []
    if has_y:
        out_specs.append(out_tile)
        out_shape.append(jax.ShapeDtypeStruct((n, d), F32))
    if has_h:
        out_specs.append(out_tile)
        out_shape.append(jax.ShapeDtypeStruct((n, d), BF16))
    args = [x] + ([y] if has_y else []) + [modv]
    outs = pl.pallas_call(
        functools.partial(_resnorm_kernel, has_y=has_y, has_h=has_h),
        grid=(n // TM,),
        in_specs=in_specs,
        out_specs=out_specs,
        out_shape=out_shape,
        compiler_params=_cp(("parallel",)),
        name="resnorm",
    )(*args)
    return outs


def _mm_kernel(x_ref, w_ref, o_ref, wb_ref):
    @pl.when(pl.program_id(1) == 0)
    def _():
        wb_ref[...] = w_ref[...].astype(BF16)

    o_ref[...] = _dot_nt(x_ref[...], wb_ref[...])


def _mm_t(x, wt, l, nrows, tn, name, tm=TM):
    m, k = x.shape
    if l is None:
        wspec = pl.BlockSpec((tn, k), lambda j, i: (j, 0))
    else:
        wspec = pl.BlockSpec((None, tn, k), lambda j, i: (l, j, 0))
    return pl.pallas_call(
        _mm_kernel,
        grid=(nrows // tn, m // tm),
        in_specs=[pl.BlockSpec((tm, k), lambda j, i: (i, 0)), wspec],
        out_specs=pl.BlockSpec((tm, tn), lambda j, i: (i, j)),
        out_shape=jax.ShapeDtypeStruct((m, nrows), F32),
        scratch_shapes=[pltpu.VMEM((tn, k), BF16)],
        compiler_params=_cp(("arbitrary", "arbitrary")),
        name=name,
    )(x, wt)


def _gla_intra(q, k, b, d, c):
    rowi = lax.broadcasted_iota(jnp.int32, (c, LANES), 0)
    ri = lax.broadcasted_iota(jnp.int32, (c, c), 0)
    ci = lax.broadcasted_iota(jnp.int32, (c, c), 1)
    a = jnp.zeros((c, c), F32)
    size = c
    while size >= 8:
        half = size // 2
        nb = c // size
        b3 = b.reshape(nb, size, LANES)
        pos = rowi & (size - 1)
        if d == 0:
            ref = b3[:, half - 1:half, :]
            qsel = pos >= half
        else:
            ref = b3[:, half:half + 1, :]
            qsel = pos < half
        ref = jnp.broadcast_to(ref, (nb, size, LANES)).reshape(c, LANES)
        qs = jnp.where(qsel, q * jnp.exp(jnp.minimum(b - ref, 0.0)), 0.0).astype(BF16)
        ks = jnp.where(qsel, 0.0, k * jnp.exp(jnp.minimum(ref - b, 0.0))).astype(BF16)
        p = _dot_nt(qs, ks)
        if nb > 1:
            shift = int(math.log2(size))
            p = jnp.where((ri >> shift) == (ci >> shift), p, 0.0)
        a = a + p
        size = half
    b3 = b.reshape(c // 8, 8, LANES)
    lo, hi = (0, 4) if d == 0 else (3, 7)
    ref_lo = jnp.broadcast_to(b3[:, lo:lo + 1, :], (c // 8, 8, LANES)).reshape(c, LANES)
    ref_hi = jnp.broadcast_to(b3[:, hi:hi + 1, :], (c // 8, 8, LANES)).reshape(c, LANES)
    ref = jnp.where((rowi & 7) >= 4, ref_hi, ref_lo)
    qs = (q * jnp.exp(jnp.minimum(b - ref, 0.0))).astype(BF16)
    ks = (k * jnp.exp(ref - b)).astype(BF16)
    p = _dot_nt(qs, ks)
    keep = ((ri >> 2) == (ci >> 2)) & ((ci <= ri) if d == 0 else (ci >= ri))
    return a + jnp.where(keep, p, 0.0)


def _hgrn_kernel(aq_ref, af_ref, ab_ref, ai_ref, ag_ref, lb_ref, g_ref, s0_ref, y_ref, s_ref, o_acc, *, t, c):
    nchunk = t // c
    ri = lax.broadcasted_iota(jnp.int32, (c, c), 0)
    ci = lax.broadcasted_iota(jnp.int32, (c, c), 1)
    scale = HEAD_D ** -0.5
    for d in (0, 1):
        z_ref = af_ref if d == 0 else ab_ref
        lbd = lb_ref[d:d + 1, :]
        tri = jnp.where((ci <= ri) if d == 0 else (ci >= ri), 1.0, 0.0).astype(BF16)

        def chunk(i, st, d=d, z_ref=z_ref, lbd=lbd, tri=tri):
            cidx = i if d == 0 else nchunk - 1 - i
            r0 = pl.multiple_of(cidx * c, c)
            z = z_ref[pl.ds(r0, c), :]
            f = lbd + (1.0 - lbd) * jax.nn.sigmoid(z)
            lf = jnp.log(jnp.maximum(f, F_FLOOR))
            k = (1.0 - lbd) * jax.nn.sigmoid(-z)
            q = _silu(aq_ref[pl.ds(r0, c), :]) * scale
            v = ai_ref[pl.ds(r0, c), :].astype(BF16)
            b = _exact_mm(tri, lf)
            a = _gla_intra(q, k, b, d, c)
            o = _dot(a.astype(BF16), v) + _dot_nt((q * jnp.exp(b)).astype(BF16), st.astype(BF16))
            if d == 0:
                o_acc[pl.ds(r0, c), :] = o
            else:
                o_acc[pl.ds(r0, c), :] += o
            blast = b[c - 1:c, :] if d == 0 else b[0:1, :]
            kd = (k * jnp.exp(blast - b)).astype(BF16)
            return st * jnp.exp(blast) + _dot_tn(v, kd)

        st = s0_ref[0, d, 0].T
        if nchunk == 1:
            st = chunk(0, st)
        else:
            st = lax.fori_loop(0, nchunk, chunk, st)
        s_ref[0, d, 0] = st.T
    o = o_acc[...]
    y_ref[...] = _rms(o, g_ref[...]) * _silu(ag_ref[...])


def _shared_out_call(kernel_fn, prevs, in_specs, args, **kw):
    extra = [p for p in prevs if p is not None]
    if not extra:
        return pl.pallas_call(kernel_fn, in_specs=in_specs, **kw)(*args)
    n_in = len(in_specs)
    aliases, k = {}, 0
    for out_idx, p in enumerate(prevs):
        if p is not None:
            aliases[n_in + k] = out_idx
            k += 1

    def body(*refs):
        kernel_fn(*refs[:n_in], *refs[n_in + len(extra):])

    return pl.pallas_call(body, in_specs=in_specs + [pl.BlockSpec(memory_space=pl.ANY)] * len(extra),
                          input_output_aliases=aliases, **kw)(*args, *extra)


def _state_out(nseq, slot, nslots, hg):
    spec = pl.BlockSpec((1, None, 2, hg, HEAD_D, HEAD_D), lambda s, h: (s, slot, 0, h, 0, 0))
    return spec, jax.ShapeDtypeStruct((nseq, nslots, 2, N_HEADS, HEAD_D, HEAD_D), F32)


def _hgrn(p_main, lb, norm_g, s0, row_blk0, nseq, t, y_prev=None, s_prev=None, slot=0, nslots=1):
    c = min(GLA_CHUNK, t)

    def col(base):
        return pl.BlockSpec((t, HEAD_D), lambda s, h: (row_blk0 + s, base + h))

    st_spec = pl.BlockSpec((1, 2, 1, HEAD_D, HEAD_D), lambda s, h: (s, 0, h, 0, 0))
    so_spec, so_shape = _state_out(nseq, slot, nslots, 1)
    return _shared_out_call(
        functools.partial(_hgrn_kernel, t=t, c=c), [y_prev, s_prev],
        [col(0), col(4), col(8), col(12), col(16),
         pl.BlockSpec((2, HEAD_D), lambda s, h: (0, h)),
         pl.BlockSpec((1, HEAD_D), lambda s, h: (0, 0)),
         st_spec],
        (p_main, p_main, p_main, p_main, p_main, lb, norm_g.reshape(1, HEAD_D), s0),
        grid=(nseq, N_HEADS),
        out_specs=[pl.BlockSpec((t, HEAD_D), lambda s, h: (row_blk0 + s, h)), so_spec],
        out_shape=[jax.ShapeDtypeStruct((p_main.shape[0], BR_WIDTH), F32), so_shape],
        scratch_shapes=[pltpu.VMEM((t, HEAD_D), F32)],
        compiler_params=_cp(("parallel", "parallel")),
        name="hgrn2",
    )


def _dot_b(a, b):
    return _dot(a.astype(BF16), b.astype(BF16))


def _tri_solve(ms, xs, c):
    ri = lax.broadcasted_iota(jnp.int32, (c, c), 0)
    ci = lax.broadcasted_iota(jnp.int32, (c, c), 1)
    eye = jnp.where(ri == ci, 1.0, 0.0)
    m8 = [jnp.where((ri >> 3) == (ci >> 3), m, 0.0) for m in ms]
    ts = [eye - m for m in m8]
    ps = [_dot_b(m, m) for m in m8]
    ts = [t + _dot_b(t, p) for t, p in zip(ts, ps)]
    ps = [_dot_b(p, p) for p in ps]
    ts = [t + _dot_b(t, p) for t, p in zip(ts, ps)]
    shift = 3
    while (1 << shift) < c:
        join = ((ri >> (shift + 1)) == (ci >> (shift + 1))) & ((ri >> shift) != (ci >> shift))
        lts = [_dot_b(jnp.where(join, m, 0.0), t) for m, t in zip(ms, ts)]
        ts = [t - _dot_b(t, lt) for t, lt in zip(ts, lts)]
        shift += 1
    return [_dot_b(t, x) for t, x in zip(ts, xs)]


def _chunk_cumsum(x, d, c):
    n = x.shape[0]
    pos = lax.broadcasted_iota(jnp.int32, x.shape, 0) & (c - 1)
    s = 1
    while s < c:
        if d == 0:
            x = x + jnp.where(pos >= s, pltpu.roll(x, s, 0), 0.0)
        else:
            x = x + jnp.where(pos < c - s, pltpu.roll(x, n - s, 0), 0.0)
        s *= 2
    return x


def _gdn_kernel(cq_ref, ck_ref, cv_ref, cg_ref, ab_ref, cw_ref, alog_ref, dtb_ref, g_ref, s0_ref,
                y_ref, s_ref, q_s, k_s, v_s, la_s, be_s, w_s, qk_s, qe_s, kd_s, dec_s, o_acc,
                *, t, c, span, hg):
    nchunk = t // c
    cps = span // c
    rowt = lax.broadcasted_iota(jnp.int32, (t, HEAD_D), 0)
    lane = lax.broadcasted_iota(jnp.int32, (t, LANES), 1)
    ri = lax.broadcasted_iota(jnp.int32, (span, span), 0)
    ci = lax.broadcasted_iota(jnp.int32, (span, span), 1)
    shift = int(math.log2(c))
    same = (ri >> shift) == (ci >> shift)
    heads = range(hg)
    dirs = (0, 1)

    def conv(x_ref, j, hh):
        x = x_ref[:, hh * HEAD_D:(hh + 1) * HEAD_D]
        w = cw_ref[j][:, hh * HEAD_D:(hh + 1) * HEAD_D]
        xm = jnp.where(rowt == 0, 0.0, pltpu.roll(x, 1, 0))
        xp = jnp.where(rowt == t - 1, 0.0, pltpu.roll(x, t - 1, 0))
        return xm * w[0:1] + x * w[1:2] + xp * w[2:3]

    ab = ab_ref[...]
    for hh in heads:
        h = pl.program_id(1) * hg + hh
        q = _silu(conv(cq_ref, 0, hh))
        q_s[hh] = q * lax.rsqrt(jnp.sum(q * q, axis=-1, keepdims=True) + EPS) * (HEAD_D ** -0.5)
        k = _silu(conv(ck_ref, 1, hh))
        k_s[hh] = k * lax.rsqrt(jnp.sum(k * k, axis=-1, keepdims=True) + EPS)
        v_s[hh] = _silu(conv(cv_ref, 2, hh))
        for d in dirs:
            a_col = jnp.sum(jnp.where(lane == d * N_HEADS + h, ab, 0.0), axis=-1, keepdims=True)
            b_col = jnp.sum(jnp.where(lane == 2 * N_HEADS + d * N_HEADS + h, ab, 0.0), axis=-1, keepdims=True)
            coef = -jnp.exp(jnp.full((1, LANES), alog_ref[d, h], F32))
            xs = a_col + dtb_ref[d, h]
            softplus = jnp.maximum(xs, 0.0) + jnp.log1p(jnp.exp(-jnp.abs(xs)))
            la_s[hh, d] = coef * jnp.broadcast_to(softplus, (t, LANES))
            be_s[hh, d] = jnp.broadcast_to(jax.nn.sigmoid(b_col), (t, LANES))
    o_acc[...] = jnp.zeros(o_acc.shape, F32)

    def wy_span(sp, carry):
        r0 = pl.multiple_of(sp * span, span)
        ms, xs, dst = [], [], []
        for hh in heads:
            qsp = q_s[hh, pl.ds(r0, span), :]
            ksp = k_s[hh, pl.ds(r0, span), :]
            vsp = v_s[hh, pl.ds(r0, span), :]
            kb = ksp.astype(BF16)
            kk = _dot_nt(kb, kb)
            qk = _dot_nt(qsp.astype(BF16), kb)
            for d in dirs:
                be = be_s[hh, d, pl.ds(r0, span), :]
                g = _chunk_cumsum(la_s[hh, d, pl.ds(r0, span), :], d, c)
                gt = g.T
                gcol = jnp.concatenate([g] * (span // LANES), axis=1)
                grow = jnp.concatenate([gt] * (span // LANES), axis=0)
                incl = same & ((ci <= ri) if d == 0 else (ci >= ri))
                gamma = jnp.where(incl, jnp.exp(jnp.minimum(gcol - grow, 0.0)), 0.0)
                bcol = jnp.concatenate([be] * (span // LANES), axis=1)
                m_sp = jnp.where(ci == ri, 0.0, bcol * kk * gamma)
                qkg = (qk * gamma).astype(BF16)
                eg = jnp.exp(g)
                xv = be * vsp
                xk = be * eg * ksp
                qe_s[hh, d, pl.ds(r0, span), :] = (qsp * eg).astype(BF16)
                for j in range(cps):
                    lo = j * c
                    rj = pl.multiple_of(r0 + lo, c)
                    ms.append(m_sp[lo:lo + c, lo:lo + c])
                    xs.append(jnp.concatenate([xv[lo:lo + c], xk[lo:lo + c]], axis=1))
                    dst.append((hh, d, rj))
                    qk_s[hh, d, pl.ds(rj, c), 0:c] = qkg[lo:lo + c, lo:lo + c]
                    gj = g[lo:lo + c]
                    glast = gj[c - 1:c] if d == 0 else gj[0:1]
                    kd_s[hh, d, pl.ds(rj, c), :] = (ksp[lo:lo + c] * jnp.exp(glast - gj)).astype(BF16)
                    dec_s[hh, d, pl.ds(sp * cps + j, 1), :] = jnp.exp(glast)
        for (hh, d, rj), w in zip(dst, _tri_solve(ms, xs, c)):
            w_s[hh, d, pl.ds(rj, c), :] = w
        return carry

    if t == span:
        wy_span(0, 0)
    else:
        lax.fori_loop(0, t // span, wy_span, 0)

    chains = [(hh, d) for hh in heads for d in dirs]

    def scan_step(i, carry):
        cidx = (i, nchunk - 1 - i)
        r0 = [pl.multiple_of(ck * c, c) for ck in cidx]
        sb = [s.astype(BF16) for s in carry]
        ws = [w_s[hh, d, pl.ds(r0[d], c), :] for hh, d in chains]
        wks = [_dot(w[:, HEAD_D:].astype(BF16), s) for w, s in zip(ws, sb)]
        oq = [_dot(qe_s[hh, d, pl.ds(r0[d], c), :], s) for (hh, d), s in zip(chains, sb)]
        vnb = [(w[:, :HEAD_D] - wk).astype(BF16) for w, wk in zip(ws, wks)]
        new = [s * dec_s[hh, d, pl.ds(cidx[d], 1), :] + _dot_tn(kd_s[hh, d, pl.ds(r0[d], c), :], v)
               for (hh, d), s, v in zip(chains, carry, vnb)]
        for (hh, d), o, v in zip(chains, oq, vnb):
            o_acc[hh, pl.ds(r0[d], c), :] += o + _dot(qk_s[hh, d, pl.ds(r0[d], c), 0:c], v)
        return tuple(new)

    finals = lax.fori_loop(0, nchunk, scan_step, tuple(s0_ref[0, d, hh] for hh, d in chains),
                           unroll=nchunk <= 4)
    for (hh, d), s in zip(chains, finals):
        s_ref[0, d, hh] = s
    for hh in heads:
        sl = slice(hh * HEAD_D, (hh + 1) * HEAD_D)
        y_ref[:, sl] = _rms(o_acc[hh], g_ref[...]) * _silu(cg_ref[:, sl])


def _gdn(p_main, p_ab, conv_w, a_log, dt_bias, norm_g, s0, row_blk0, nseq, t, y_prev=None, s_prev=None,
         slot=0, nslots=1):
    c = GDN_CHUNK
    hg = GDN_HEADS_PER_STEP if t * GDN_HEADS_PER_STEP <= GDN_MAX_ROWS_PER_STEP else GDN_HEADS_PER_STEP // 2
    wid = hg * HEAD_D
    blk = BR_WIDTH // wid

    def col(base):
        return pl.BlockSpec((t, wid), lambda s, h: (row_blk0 + s, base * blk + h))

    st_spec = pl.BlockSpec((1, 2, hg, HEAD_D, HEAD_D), lambda s, h: (s, 0, h, 0, 0))
    smem = pl.BlockSpec(memory_space=pltpu.SMEM)
    seq_buf = pltpu.VMEM((hg, t, HEAD_D), F32)
    dir_f32 = pltpu.VMEM((hg, 2, t, LANES), F32)
    dir_bf16 = pltpu.VMEM((hg, 2, t, LANES), BF16)
    scratch = [seq_buf, seq_buf, seq_buf, dir_f32, dir_f32,
               pltpu.VMEM((hg, 2, t, 2 * HEAD_D), F32), dir_bf16, dir_bf16, dir_bf16,
               pltpu.VMEM((hg, 2, max(t // c, 8), LANES), F32), seq_buf]
    so_spec, so_shape = _state_out(nseq, slot, nslots, hg)
    return _shared_out_call(
        functools.partial(_gdn_kernel, t=t, c=c, span=GDN_SPAN, hg=hg), [y_prev, s_prev],
        [col(6), col(7), col(8), col(9),
         pl.BlockSpec((t, LANES), lambda s, h: (row_blk0 + s, 0)),
         pl.BlockSpec((3, 3, wid), lambda s, h: (0, 0, h)),
         smem, smem,
         pl.BlockSpec((1, HEAD_D), lambda s, h: (0, 0)),
         st_spec],
        (p_main, p_main, p_main, p_main, p_ab, conv_w, a_log, dt_bias, norm_g.reshape(1, HEAD_D), s0),
        grid=(nseq, N_HEADS // hg),
        out_specs=[pl.BlockSpec((t, wid), lambda s, h: (row_blk0 + s, h)), so_spec],
        out_shape=[jax.ShapeDtypeStruct((p_main.shape[0], BR_WIDTH), F32), so_shape],
        scratch_shapes=scratch,
        compiler_params=_cp(("parallel", "parallel")),
        name="gdn",
    )


def _cmul(ar, ai, br, bi):
    return ar * br - ai * bi, ar * bi + ai * br


def _s5_kernel(u_ref, bb_ref, cc_ref, lam_ref, dsk_ref, s0_ref, y_ref, sf_ref, up_s, x_s, yp_s,
               *, tb, seg, segs_per_seq, zero_init):
    r_par = tb // seg
    half = S5_GB * S5_STATE
    for r in range(seg):
        up_s[r * r_par:(r + 1) * r_par, :] = u_ref[pl.ds(r, r_par, stride=seg), :]
    ub = up_s[...].astype(BF16)
    sf_ref[...] = jnp.zeros(sf_ref.shape, F32)
    nseq = r_par // segs_per_seq
    lam = [(lam_ref[0, d, 0:1, :], lam_ref[0, d, 1:2, :]) for d in (0, 1)]
    for d in (0, 1):
        x_s[d] = _dot(ub, bb_ref[0, d])

    def offsets(i):
        return [pl.multiple_of(r * r_par, r_par) for r in (i, seg - 1 - i)]

    def local_step(i, carry):
        off = offsets(i)
        new = []
        for d in (0, 1):
            (lre, lim), (hre, him) = lam[d], carry[d]
            nre = lre * hre - lim * him + x_s[d, pl.ds(off[d], r_par), 0:half]
            nim = lre * him + lim * hre + x_s[d, pl.ds(off[d], r_par), half:2 * half]
            x_s[d, pl.ds(off[d], r_par), 0:half] = nre
            x_s[d, pl.ds(off[d], r_par), half:2 * half] = nim
            new.append((nre, nim))
        return tuple(new)

    zero = jnp.zeros((r_par, half), F32)
    lax.fori_loop(0, seg, local_step, ((zero, zero), (zero, zero)))

    end_rows = ((seg - 1) * r_par, 0)
    needs_carry = not (zero_init and segs_per_seq == 1)
    hins = []
    for d in (0, 1):
        if not needs_carry:
            break
        lre, lim = lam[d]
        pre, pim = lre, lim
        n = 1
        while n < seg:
            pre, pim = _cmul(pre, pim, pre, pim)
            n *= 2
        end_row = end_rows[d]
        ere = x_s[d, end_row:end_row + r_par, 0:half]
        eim = x_s[d, end_row:end_row + r_par, half:2 * half]
        hin = [None] * r_par
        order = range(r_par) if d == 0 else range(r_par - 1, -1, -1)
        for j in order:
            sq, pos = divmod(j, segs_per_seq)
            first = pos == (0 if d == 0 else segs_per_seq - 1)
            if first:
                hin[j] = (s0_ref[0, 0, d, sq:sq + 1, 0:half], s0_ref[0, 0, d, sq:sq + 1, half:2 * half])
            else:
                pj = j - 1 if d == 0 else j + 1
                cre, cim = _cmul(pre, pim, hin[pj][0], hin[pj][1])
                hin[j] = (cre + ere[pj:pj + 1], cim + eim[pj:pj + 1])
        hins.append((jnp.concatenate([hj[0] for hj in hin], axis=0),
                     jnp.concatenate([hj[1] for hj in hin], axis=0)))

    def carry_step(i, pws):
        off = offsets(i)
        new = []
        for d in (0, 1):
            (lre, lim), (wre, wim) = lam[d], pws[d]
            are, aim = _cmul(wre, wim, hins[d][0], hins[d][1])
            x_s[d, pl.ds(off[d], r_par), 0:half] += are
            x_s[d, pl.ds(off[d], r_par), half:2 * half] += aim
            new.append(_cmul(wre, wim, lre, lim))
        return tuple(new)

    if needs_carry:
        lax.fori_loop(0, seg, carry_step, (lam[0], lam[1]))

    for d in (0, 1):
        for sq in range(nseq):
            j = (sq + 1) * segs_per_seq - 1 if d == 0 else sq * segs_per_seq
            row = end_rows[d] + j
            sf_ref[0, 0, d, sq:sq + 1, :] = x_s[d, row:row + 1, :]
    yp_s[...] = (_dot(x_s[0].astype(BF16), cc_ref[0, 0]) + _dot(x_s[1].astype(BF16), cc_ref[0, 1])
                 + up_s[...] * dsk_ref[...])
    for r in range(seg):
        y_ref[pl.ds(r, r_par, stride=seg), :] = yp_s[r * r_par:(r + 1) * r_par, :]


def _s5(p_main, bb, cc, lam, dskip, s0, row_blk0, nblk, tb, t, seg, zero_init, y_prev=None):
    ngb = S5_GROUPS // S5_GB
    w2 = 2 * S5_GB * S5_STATE
    return _shared_out_call(
        functools.partial(_s5_kernel, tb=tb, seg=seg, segs_per_seq=t // seg, zero_init=zero_init), [y_prev, None],
        [
            pl.BlockSpec((tb, LANES), lambda i, gb: (row_blk0 + i, 20 + gb)),
            pl.BlockSpec((1, 2, LANES, w2), lambda i, gb: (gb, 0, 0, 0)),
            pl.BlockSpec((1, 2, w2, LANES), lambda i, gb: (gb, 0, 0, 0)),
            pl.BlockSpec((1, 2, 2, w2 // 2), lambda i, gb: (gb, 0, 0, 0)),
            pl.BlockSpec((1, LANES), lambda i, gb: (0, gb)),
            pl.BlockSpec((1, 1, 2, 8, w2), lambda i, gb: (i, gb, 0, 0, 0)),
        ],
        (p_main, bb, cc, lam, dskip.reshape(1, BR_WIDTH), s0),
        grid=(nblk, ngb),
        out_specs=[pl.BlockSpec((tb, LANES), lambda i, gb: (row_blk0 + i, gb)),
                   pl.BlockSpec((1, 1, 2, 8, w2), lambda i, gb: (i, gb, 0, 0, 0))],
        out_shape=[jax.ShapeDtypeStruct((p_main.shape[0], BR_WIDTH), F32),
                   jax.ShapeDtypeStruct((nblk, ngb, 2, 8, w2), F32)],
        scratch_shapes=[pltpu.VMEM((tb, LANES), F32), pltpu.VMEM((2, tb, w2), F32), pltpu.VMEM((tb, LANES), F32)],
        compiler_params=_cp(("parallel", "parallel")),
        name="s5",
    )


def _s5_params(a_re, a_im, log_dt, b_re, b_im, c_re, c_im):
    dt = jnp.exp(log_dt)[..., None]
    mag = jnp.exp(dt * a_re)
    lam_re = mag * jnp.cos(dt * a_im)
    lam_im = mag * jnp.sin(dt * a_im)
    den = a_re * a_re + a_im * a_im
    co_re = ((lam_re - 1.0) * a_re + lam_im * a_im) / den
    co_im = (lam_im * a_re - (lam_re - 1.0) * a_im) / den
    bb_re = co_re[..., None] * b_re - co_im[..., None] * b_im
    bb_im = co_re[..., None] * b_im + co_im[..., None] * b_re
    ngb = S5_GROUPS // S5_GB
    eye = jnp.eye(S5_GB, dtype=F32)

    def pack_in(w):
        w = w.reshape(2, ngb, S5_GB, S5_STATE, S5_GROUP)
        blk = jnp.einsum('dbgpi,gh->bdgihp', w, eye)
        return blk.reshape(ngb, 2, S5_GB * S5_GROUP, S5_GB * S5_STATE)

    def pack_out(w):
        w = w.reshape(2, ngb, S5_GB, S5_GROUP, S5_STATE)
        blk = jnp.einsum('dbgip,gh->bdgphi', w, eye)
        return blk.reshape(ngb, 2, S5_GB * S5_STATE, S5_GB * S5_GROUP)

    bb = jnp.concatenate([pack_in(bb_re), pack_in(bb_im)], axis=-1).astype(BF16)
    cc = jnp.concatenate([pack_out(c_re), -pack_out(c_im)], axis=-2).astype(BF16)
    lam = jnp.stack([lam_re, lam_im], axis=1)
    lam = lam.reshape(2, 2, ngb, S5_GB * S5_STATE).transpose(2, 0, 1, 3)
    return bb, cc, lam


def _glu_kernel(y_ref, w_ref, b_ref, o_ref, wb_ref):
    @pl.when(pl.program_id(0) == 0)
    def _():
        wb_ref[...] = w_ref[...].astype(BF16)

    z = _dot(y_ref[...].astype(BF16), wb_ref[...]) + b_ref[...]
    half = z.shape[1] // 2
    o_ref[...] = z[:, :half] * jax.nn.sigmoid(z[:, half:])


def _glu(y, w, b, l):
    n, k = y.shape
    n2 = w.shape[-1]
    return pl.pallas_call(
        _glu_kernel,
        grid=(n // TM,),
        in_specs=[pl.BlockSpec((TM, k), lambda i: (i, 0)),
                  pl.BlockSpec((None, k, n2), lambda i: (l, 0, 0)),
                  pl.BlockSpec((None, 1, n2), lambda i: (l, 0, 0))],
        out_specs=pl.BlockSpec((TM, n2 // 2), lambda i: (i, 0)),
        out_shape=jax.ShapeDtypeStruct((n, n2 // 2), F32),
        scratch_shapes=[pltpu.VMEM((k, n2), BF16)],
        compiler_params=_cp(("arbitrary",)),
        name="s5_glu",
    )(y, w, b.reshape(b.shape[0], 1, n2))


def _merge_kernel(ya_ref, yb_ref, yc_ref, mg_ref, wbr_ref, wout_ref, g_ref, o_ref, wbr_s, wout_s):
    @pl.when(pl.program_id(0) == 0)
    def _():
        wbr_s[...] = wbr_ref[...].astype(BF16)
        wout_s[...] = wout_ref[...].astype(BF16)

    d = wout_s.shape[0]
    acc = None
    for n, br_ref in enumerate((ya_ref, yb_ref, yc_ref)):
        proj = _dot(br_ref[...].astype(BF16), wbr_s[n])
        term = proj * jax.nn.sigmoid(mg_ref[:, n * d:(n + 1) * d])
        acc = term if acc is None else acc + term
    y = _dot(acc.astype(BF16), wout_s[...])
    o_ref[...] = _rms(y, g_ref[...])


def _merge(ya, yb, yc, mg, w_branch, w_out, g, l):
    n = ya.shape[0]
    d = w_out.shape[-1]
    tm = 256
    br = pl.BlockSpec((tm, BR_WIDTH), lambda i: (i, 0))
    return pl.pallas_call(
        _merge_kernel,
        grid=(n // tm,),
        in_specs=[br, br, br,
                  pl.BlockSpec((tm, 3 * d), lambda i: (i, 0)),
                  pl.BlockSpec((None, 3, BR_WIDTH, d), lambda i: (l, 0, 0, 0)),
                  pl.BlockSpec((None, d, d), lambda i: (l, 0, 0)),
                  pl.BlockSpec((1, d), lambda i: (0, 0))],
        out_specs=pl.BlockSpec((tm, d), lambda i: (i, 0)),
        out_shape=jax.ShapeDtypeStruct((n, d), F32),
        scratch_shapes=[pltpu.VMEM((3, BR_WIDTH, d), BF16), pltpu.VMEM((d, d), BF16)],
        compiler_params=_cp(("arbitrary",)),
        name="merge",
    )(ya, yb, yc, mg, w_branch, w_out, g.reshape(1, d))


def _swiglu_kernel(*refs, gated):
    if gated:
        x_ref, wg_ref, wu_ref, gt_ref, o_ref, wg_s, wu_s = refs
    else:
        x_ref, wg_ref, wu_ref, o_ref, wg_s, wu_s = refs

    @pl.when(pl.program_id(1) == 0)
    def _():
        wg_s[...] = wg_ref[...].astype(BF16)
        wu_s[...] = wu_ref[...].astype(BF16)

    x = x_ref[...]
    hid = _silu(_dot(x, wg_s[...])) * _dot(x, wu_s[...])
    if gated:
        gt = gt_ref[...]
        lane = lax.broadcasted_iota(jnp.int32, gt.shape, 1)
        hid = hid * jnp.sum(jnp.where(lane == pl.program_id(0), gt, 0.0), axis=-1, keepdims=True)
    o_ref[...] = hid.astype(BF16)


def _swiglu_up(x, wg, wu, e, f, gates=None):
    n, d = x.shape
    gated = gates is not None
    if gated:
        ne = wg.shape[1]
        wspec = pl.BlockSpec((None, None, d, f), lambda j, i: (e, j, 0, 0))
    else:
        ne = wg.shape[-1] // f
        wspec = pl.BlockSpec((None, d, f), lambda j, i: (e, 0, j))
    in_specs = [pl.BlockSpec((TM, d), lambda j, i: (i, 0)), wspec, wspec]
    args = [x, wg, wu]
    if gated:
        in_specs.append(pl.BlockSpec((TM, LANES), lambda j, i: (i, 0)))
        args.append(gates)
    return pl.pallas_call(
        functools.partial(_swiglu_kernel, gated=gated),
        grid=(ne, n // TM),
        in_specs=in_specs,
        out_specs=pl.BlockSpec((TM, f), lambda j, i: (i, j)),
        out_shape=jax.ShapeDtypeStruct((n, ne * f), BF16),
        scratch_shapes=[pltpu.VMEM((d, f), BF16), pltpu.VMEM((d, f), BF16)],
        compiler_params=_cp(("arbitrary", "arbitrary")),
        name="swiglu_up",
    )(*args)


def _down_kernel(x_ref, w_ref, g_ref, o_ref, acc):
    k = pl.program_id(1)
    part = _dot(x_ref[...], w_ref[...].astype(BF16))

    @pl.when(k == 0)
    def _():
        acc[...] = part

    @pl.when(k > 0)
    def _():
        acc[...] += part

    @pl.when(k == pl.num_programs(1) - 1)
    def _():
        o_ref[...] = _rms(acc[...], g_ref[...])


def _down_norm(hid, wd, e, g, tm):
    n = hid.shape[0]
    _, ne, f, d = wd.shape
    return pl.pallas_call(
        _down_kernel,
        grid=(n // tm, ne),
        in_specs=[pl.BlockSpec((tm, f), lambda i, k: (i, k)),
                  pl.BlockSpec((None, None, f, d), lambda i, k: (e, k, 0, 0)),
                  pl.BlockSpec((1, d), lambda i, k: (0, 0))],
        out_specs=pl.BlockSpec((tm, d), lambda i, k: (i, 0)),
        out_shape=jax.ShapeDtypeStruct((n, d), F32),
        scratch_shapes=[pltpu.VMEM((tm, d), F32)],
        compiler_params=_cp(("parallel", "arbitrary")),
        name="down_norm",
    )(hid, wd, g.reshape(1, d))


def _router_kernel(x_ref, w_ref, o_ref, *, ne):
    logits = _dot(x_ref[...], w_ref[...].astype(BF16))
    lane = lax.broadcasted_iota(jnp.int32, logits.shape, 1).astype(F32)
    neg = jnp.float32(-jnp.inf)
    logits = jnp.where(lane < ne, logits, neg)
    m1 = jnp.max(logits, axis=-1, keepdims=True)
    i1 = jnp.min(jnp.where(logits == m1, lane, float(LANES)), axis=-1, keepdims=True)
    rest = jnp.where(lane == i1, neg, logits)
    m2 = jnp.max(rest, axis=-1, keepdims=True)
    i2 = jnp.min(jnp.where(rest == m2, lane, float(LANES)), axis=-1, keepdims=True)
    e2 = jnp.exp(m2 - m1)
    w1 = 1.0 / (1.0 + e2)
    w2 = e2 / (1.0 + e2)
    o_ref[...] = jnp.where(lane == i1, w1, 0.0) + jnp.where(lane == i2, w2, 0.0)


def _router(x, w_pad, ne):
    n, d = x.shape
    return pl.pallas_call(
        functools.partial(_router_kernel, ne=ne),
        grid=(n // TM,),
        in_specs=[pl.BlockSpec((TM, d), lambda i: (i, 0)),
                  pl.BlockSpec((d, LANES), lambda i: (0, 0))],
        out_specs=pl.BlockSpec((TM, LANES), lambda i: (i, 0)),
        out_shape=jax.ShapeDtypeStruct((n, LANES), F32),
        compiler_params=_cp(("parallel",)),
        name="router",
    )(x, w_pad)


def _to_column_major(x, b, t):
    d = x.shape[-1]
    return x.reshape(b, t // GRID_W, GRID_W, d).transpose(0, 2, 1, 3).reshape(b * t, d)


def _to_row_major(x, b, t):
    d = x.shape[-1]
    return x.reshape(b, GRID_W, t // GRID_W, d).transpose(0, 2, 1, 3).reshape(b * t, d)


def kernel(x_prompt, x_sample, state_hgrn, state_s5, state_gdn, c, c_ctx, ada_w, ada_b, norm_g, w_in, hgrn_lb, hgrn_norm, s5_a_re, s5_a_im, s5_log_dt, s5_b_re, s5_b_im, s5_c_re, s5_c_im, s5_d, s5_glu_w, s5_glu_b, gdn_conv, gdn_a_log, gdn_dt_bias, gdn_norm, w_branch, w_out, ffn_wg, ffn_wu, ffn_wd, moe_router, moe_wg, moe_wu, moe_wd):
    bp, tp, d = x_prompt.shape
    bs, ts, _ = x_sample.shape
    depth = w_in.shape[0]
    n_p, n_s = bp * tp, bs * ts
    ne = moe_router.shape[-1]
    main_cols = 10 * BR_WIDTH
    ab_cols = 4 * N_HEADS

    c8 = jnp.zeros((8, d), F32).at[0].set(c_ctx).at[1:1 + bs].set(c)
    mod_all = _ada_mod(c8, ada_w, ada_b)

    lb_soft = jax.nn.softmax(hgrn_lb.astype(F32), axis=0)
    lb_all = jnp.cumsum(lb_soft, axis=0) - lb_soft[:1]

    mods = mod_all[:, :1 + bs].reshape(depth, 1 + bs, 6, d)
    zero_row = jnp.zeros((1 + bs, d), F32)

    def modv(gate, scale, shift, g):
        rows = [zero_row if sel is None else mods[sel[0], :, sel[1]] for sel in (gate, scale, shift)]
        rows.append(zero_row if g is None else jnp.broadcast_to(g, (1 + bs, d)))
        return jnp.stack(rows + [zero_row] * 4, axis=1)

    x = jnp.concatenate([x_prompt.reshape(n_p, d), x_sample.reshape(n_s, d)], axis=0)
    (h,) = _resnorm(x, None, modv(None, (0, 1), (0, 0), norm_g[0, 0]), n_p, ts, True)

    z_hgrn = jnp.zeros((bp, 2, N_HEADS, HEAD_D, HEAD_D), F32)
    ngb = S5_GROUPS // S5_GB
    w2 = 2 * S5_GB * S5_STATE
    st_s5 = []
    new_hgrn = new_gdn = None
    w_in_t = jnp.swapaxes(w_in, 1, 2)
    s5_tb = 1024

    for l in range(depth):
        col_major = l % 2 == 1
        if col_major:
            h = jnp.concatenate([h[:n_p], _to_column_major(h[n_p:], bs, ts)], axis=0)
        p_main = _mm_t(h, w_in_t, l, main_cols, 2 * BR_WIDTH, "w_in_main", tm=1024)
        w_ab = jnp.pad(w_in_t[l, main_cols:main_cols + ab_cols], ((0, LANES - ab_cols), (0, 0)))
        p_ab = _mm_t(h, w_ab, None, LANES, LANES, "w_in_ab", tm=1024)
        p_mg = _mm_t(h, w_in_t[l, main_cols + ab_cols:], None, 3 * d, 2 * BR_WIDTH, "w_in_merge", tm=1024)

        ya, new_hgrn = _hgrn(p_main, lb_all[l], hgrn_norm[l], z_hgrn, 0, bp, tp, None, new_hgrn, l, depth)
        ya, _ = _hgrn(p_main, lb_all[l], hgrn_norm[l], state_hgrn[:, l].astype(F32), n_p // ts, bs, ts, ya)

        bb, cc, lam = _s5_params(s5_a_re[l], s5_a_im[l], s5_log_dt[l], s5_b_re[l], s5_b_im[l],
                                 s5_c_re[l], s5_c_im[l])
        seq_per_blk = 8
        tb_p = seq_per_blk * tp
        nblk_p = n_p // tb_p
        yb, sb = _s5(p_main, bb, cc, lam, s5_d[l], jnp.zeros((nblk_p, ngb, 2, 8, w2), F32),
                     0, nblk_p, tb_p, tp, tp, True)
        s0s = state_s5[:, l].astype(F32)
        s0s = s0s.reshape(bs, 2, ngb, S5_GB * S5_STATE, 2).transpose(0, 2, 1, 4, 3).reshape(bs, ngb, 2, 1, w2)
        s0s = jnp.pad(s0s, ((0, 0), (0, 0), (0, 0), (0, 7), (0, 0)))
        yb, _ = _s5(p_main, bb, cc, lam, s5_d[l], s0s, n_p // s5_tb, n_s // s5_tb, s5_tb, ts, S5_SEG, False, yb)
        yb = _glu(yb, s5_glu_w, s5_glu_b, l)
        sb = sb[:, :, :, :seq_per_blk].reshape(nblk_p, ngb, 2, seq_per_blk, 2, S5_GB, S5_STATE)
        sb = sb.transpose(0, 3, 2, 1, 5, 6, 4).reshape(bp, 2, S5_GROUPS, S5_STATE, 2)
        st_s5.append(sb)

        yc, new_gdn = _gdn(p_main, p_ab, gdn_conv[l], gdn_a_log[l], gdn_dt_bias[l], gdn_norm[l], z_hgrn, 0, bp, tp,
                           None, new_gdn, l, depth)
        yc, _ = _gdn(p_main, p_ab, gdn_conv[l], gdn_a_log[l], gdn_dt_bias[l], gdn_norm[l],
                     state_gdn[:, l].astype(F32), n_p // ts, bs, ts, yc)

        r_m = _merge(ya, yb, yc, p_mg, w_branch, w_out, norm_g[l, 1], l)
        if col_major:
            r_m = jnp.concatenate([r_m[:n_p], _to_row_major(r_m[n_p:], bs, ts)], axis=0)
        x, h2 = _resnorm(x, r_m, modv((l, 2), (l, 4), (l, 3), norm_g[l, 2]), n_p, ts, True)

        if l % 2 == 0:
            e = l // 2
            hid = _swiglu_up(h2, ffn_wg, ffn_wu, e, ffn_wg.shape[-1] // 2)
            r_f = _down_norm(hid, ffn_wd[:, None], e, norm_g[l, 3], TM)
        else:
            e = l // 2
            gates = _router(h2, jnp.pad(moe_router[e], ((0, 0), (0, LANES - ne))), ne)
            hid = _swiglu_up(h2, moe_wg, moe_wu, e, moe_wg.shape[-1], gates)
            r_f = _down_norm(hid, moe_wd, e, norm_g[l, 3], 1024)

        if l + 1 < depth:
            x, h = _resnorm(x, r_f, modv((l, 5), (l + 1, 1), (l + 1, 0), norm_g[l + 1, 0]), n_p, ts, True)
        else:
            mv = modv((l, 5), None, None, None)
            (y_prompt,) = _resnorm(x, r_f, mv, n_p, ts, False, rows=(0, n_p))
            (y_sample,) = _resnorm(x, r_f, mv, n_p, ts, False, rows=(n_p, n_s))

    y_prompt = y_prompt.reshape(bp, tp, d)
    y_sample = y_sample.reshape(bs, ts, d)
    new_s5 = jnp.stack(st_s5, axis=1).astype(x_prompt.dtype)
    return (y_prompt, y_sample, new_hgrn.astype(x_prompt.dtype), new_s5, new_gdn.astype(x_prompt.dtype))
```

```python
import functools
import math

import jax
import jax.numpy as jnp
from jax import lax
from jax.experimental import pallas as pl
from jax.experimental.pallas import tpu as pltpu

F32 = jnp.float32
BF16 = jnp.bfloat16

EPS = 1e-6
F_FLOOR = 1e-6
LANES = 128
GRID_W = 64
HEAD_D = 128
N_HEADS = 4
BR_WIDTH = 512
S5_GROUPS = 32
S5_GROUP = 16
S5_STATE = 64
S5_GB = 8
S5_SEG = 64
GLA_CHUNK = 256
GDN_CHUNK = 64
GDN_SPAN = 256
GDN_HEADS_PER_STEP = 4
GDN_MAX_ROWS_PER_STEP = 2048
TM = 512
VMEM_LIMIT = 56 * 1024 * 1024


def _cp(sem, vmem=VMEM_LIMIT):
    return pltpu.CompilerParams(dimension_semantics=sem, vmem_limit_bytes=vmem)


def _dot(a, b):
    return jnp.dot(a, b, preferred_element_type=F32)


def _dot_nt(a, b):
    return lax.dot_general(a, b, (((1,), (1,)), ((), ())), preferred_element_type=F32)


def _dot_tn(a, b):
    return lax.dot_general(a, b, (((0,), (0,)), ((), ())), preferred_element_type=F32)


def _silu(x):
    return x * jax.nn.sigmoid(x)


def _rms(x, g):
    return x * lax.rsqrt(jnp.mean(x * x, axis=-1, keepdims=True) + EPS) * g


def _chunk_cumsum(x, d, c):
    n = x.shape[0]
    pos = lax.broadcasted_iota(jnp.int32, x.shape, 0) & (c - 1)
    s = 1
    while s < c:
        if d == 0:
            x = x + jnp.where(pos >= s, pltpu.roll(x, s, 0), 0.0)
        else:
            x = x + jnp.where(pos < c - s, pltpu.roll(x, n - s, 0), 0.0)
        s *= 2
    return x


def _ada_kernel(c_ref, w_ref, b_ref, o_ref):
    c = c_ref[...]
    a = _silu(c).astype(BF16)
    o_ref[...] = _dot(a, w_ref[...].astype(BF16)) + b_ref[...]


def _ada_mod(c8, ada_w, ada_b):
    depth, d, n6 = ada_w.shape
    tn = 512
    return pl.pallas_call(
        _ada_kernel,
        grid=(depth, n6 // tn),
        in_specs=[
            pl.BlockSpec((8, d), lambda l, j: (0, 0)),
            pl.BlockSpec((None, d, tn), lambda l, j: (l, 0, j)),
            pl.BlockSpec((None, 1, tn), lambda l, j: (l, 0, j)),
        ],
        out_specs=pl.BlockSpec((None, 8, tn), lambda l, j: (l, 0, j)),
        out_shape=jax.ShapeDtypeStruct((depth, 8, n6), F32),
        compiler_params=_cp(("parallel", "parallel")),
        name="ada_mod",
    )(c8, ada_w, ada_b.reshape(depth, 1, n6))


def _resnorm_kernel(*refs, has_y, has_h):
    it = iter(refs)
    x_ref = next(it)
    y_ref = next(it) if has_y else None
    mv_ref = next(it)
    xo_ref = next(it) if has_y else None
    h_ref = next(it) if has_h else None
    x = x_ref[...]
    mv = mv_ref[...]
    if has_y:
        x = x + mv[0:1] * y_ref[...]
        xo_ref[...] = x
    if has_h:
        h = _rms(x, mv[3:4]) * (1.0 + mv[1:2]) + mv[2:3]
        h_ref[...] = h.astype(BF16)


def _resnorm(x, y, modv, n_prompt, seq_s, has_h, rows=None):
    d = x.shape[1]
    row0, n = (0, x.shape[0]) if rows is None else rows
    i0 = row0 // TM
    has_y = y is not None

    def group(i):
        r = (i + i0) * TM
        return jnp.where(r < n_prompt, 0, 1 + (r - n_prompt) // seq_s)

    tile = pl.BlockSpec((TM, d), lambda i: (i + i0, 0))
    in_specs = [tile] + ([tile] if has_y else []) + [pl.BlockSpec((None, 8, d), lambda i: (group(i), 0, 0))]
    out_tile = pl.BlockSpec((TM, d), lambda i: (i, 0))
    out_specs, out_shape = [], []
    if has_y:
        out_specs.append(out_tile)
        out_shape.append(jax.ShapeDtypeStruct((n, d), F32))
    if has_h:
        out_specs.append(out_tile)
        out_shape.append(jax.ShapeDtypeStruct((n, d), BF16))
    args = [x] + ([y] if has_y else []) + [modv]
    outs = pl.pallas_call(
        functools.partial(_resnorm_kernel, has_y=has_y, has_h=has_h),
        grid=(n // TM,),
        in_specs=in_specs,
        out_specs=out_specs,
        out_shape=out_shape,
        compiler_params=_cp(("parallel",)),
        name="resnorm",
    )(*args)
    return outs


def _mm_kernel(x_ref, w_ref, o_ref, wb_ref):
    @pl.when(pl.program_id(1) == 0)
    def _():
        wb_ref[...] = w_ref[...].astype(BF16)

    o_ref[...] = _dot_nt(x_ref[...], wb_ref[...])


def _mm_t(x, wt, l, nrows, tn, name, tm=TM):
    m, k = x.shape
    if l is None:
        wspec = pl.BlockSpec((tn, k), lambda j, i: (j, 0))
    else:
        wspec = pl.BlockSpec((None, tn, k), lambda j, i: (l, j, 0))
    return pl.pallas_call(
        _mm_kernel,
        grid=(nrows // tn, m // tm),
        in_specs=[pl.BlockSpec((tm, k), lambda j, i: (i, 0)), wspec],
        out_specs=pl.BlockSpec((tm, tn), lambda j, i: (i, j)),
        out_shape=jax.ShapeDtypeStruct((m, nrows), F32),
        scratch_shapes=[pltpu.VMEM((tn, k), BF16)],
        compiler_params=_cp(("arbitrary", "arbitrary")),
        name=name,
    )(x, wt)


def _gla_intra(q, k, b, d, c):
    rowi = lax.broadcasted_iota(jnp.int32, (c, LANES), 0)
    ri = lax.broadcasted_iota(jnp.int32, (c, c), 0)
    ci = lax.broadcasted_iota(jnp.int32, (c, c), 1)
    a = jnp.zeros((c, c), F32)
    size = c
    while size >= 8:
        half = size // 2
        nb = c // size
        b3 = b.reshape(nb, size, LANES)
        pos = rowi & (size - 1)
        if d == 0:
            ref = b3[:, half - 1:half, :]
            qsel = pos >= half
        else:
            ref = b3[:, half:half + 1, :]
            qsel = pos < half
        ref = jnp.broadcast_to(ref, (nb, size, LANES)).reshape(c, LANES)
        z = jnp.where(qsel, q, k) * jnp.exp(-jnp.abs(b - ref))
        qs = jnp.where(qsel, z, 0.0).astype(BF16)
        ks = jnp.where(qsel, 0.0, z).astype(BF16)
        p = _dot_nt(qs, ks)
        if nb > 1:
            shift = int(math.log2(size))
            p = jnp.where((ri >> shift) == (ci >> shift), p, 0.0)
        a = a + p
        size = half
    b3 = b.reshape(c // 8, 8, LANES)
    lo, hi = (0, 4) if d == 0 else (3, 7)
    ref_lo = jnp.broadcast_to(b3[:, lo:lo + 1, :], (c // 8, 8, LANES)).reshape(c, LANES)
    ref_hi = jnp.broadcast_to(b3[:, hi:hi + 1, :], (c // 8, 8, LANES)).reshape(c, LANES)
    ref = jnp.where((rowi & 7) >= 4, ref_hi, ref_lo)
    qs = (q * jnp.exp(jnp.minimum(b - ref, 0.0))).astype(BF16)
    ks = (k * jnp.exp(ref - b)).astype(BF16)
    p = _dot_nt(qs, ks)
    keep = ((ri >> 2) == (ci >> 2)) & ((ci <= ri) if d == 0 else (ci >= ri))
    return a + jnp.where(keep, p, 0.0)


def _hgrn_kernel(aq_ref, af_ref, ab_ref, ai_ref, ag_ref, lb_ref, g_ref, s0_ref, y_ref, s_ref, o_acc, *, t, c):
    nchunk = t // c
    scale = HEAD_D ** -0.5
    for d in (0, 1):
        z_ref = af_ref if d == 0 else ab_ref
        lbd = lb_ref[d:d + 1, :]

        def chunk(i, st, d=d, z_ref=z_ref, lbd=lbd):
            cidx = i if d == 0 else nchunk - 1 - i
            r0 = pl.multiple_of(cidx * c, c)
            z = z_ref[pl.ds(r0, c), :]
            f = lbd + (1.0 - lbd) * jax.nn.sigmoid(z)
            lf = jnp.log(jnp.maximum(f, F_FLOOR))
            k = (1.0 - lbd) * jax.nn.sigmoid(-z)
            q = _silu(aq_ref[pl.ds(r0, c), :]) * scale
            v = ai_ref[pl.ds(r0, c), :].astype(BF16)
            b = _chunk_cumsum(lf, d, c)
            a = _gla_intra(q, k, b, d, c)
            o = _dot(a.astype(BF16), v) + _dot_nt((q * jnp.exp(b)).astype(BF16), st.astype(BF16))
            if d == 0:
                o_acc[pl.ds(r0, c), :] = o
            else:
                o_acc[pl.ds(r0, c), :] += o
            blast = b[c - 1:c, :] if d == 0 else b[0:1, :]
            kd = (k * jnp.exp(blast - b)).astype(BF16)
            return st * jnp.exp(blast) + _dot_tn(v, kd)

        st = s0_ref[0, d, 0].T
        if nchunk == 1:
            st = chunk(0, st)
        else:
            st = lax.fori_loop(0, nchunk, chunk, st)
        s_ref[0, d, 0] = st.T
    o = o_acc[...]
    y_ref[...] = _rms(o, g_ref[...]) * _silu(ag_ref[...])


def _shared_out_call(kernel_fn, prevs, in_specs, args, **kw):
    extra = [p for p in prevs if p is not None]
    if not extra:
        return pl.pallas_call(kernel_fn, in_specs=in_specs, **kw)(*args)
    n_in = len(in_specs)
    aliases, k = {}, 0
    for out_idx, p in enumerate(prevs):
        if p is not None:
            aliases[n_in + k] = out_idx
            k += 1

    def body(*refs):
        kernel_fn(*refs[:n_in], *refs[n_in + len(extra):])

    return pl.pallas_call(body, in_specs=in_specs + [pl.BlockSpec(memory_space=pl.ANY)] * len(extra),
                          input_output_aliases=aliases, **kw)(*args, *extra)


def _state_out(nseq, slot, nslots, hg):
    spec = pl.BlockSpec((1, None, 2, hg, HEAD_D, HEAD_D), lambda s, h: (s, slot, 0, h, 0, 0))
    return spec, jax.ShapeDtypeStruct((nseq, nslots, 2, N_HEADS, HEAD_D, HEAD_D), F32)


def _hgrn(p_main, lb, norm_g, s0, row_blk0, nseq, t, y_prev=None, s_prev=None, slot=0, nslots=1):
    c = min(GLA_CHUNK, t)

    def col(base):
        return pl.BlockSpec((t, HEAD_D), lambda s, h: (row_blk0 + s, base + h))

    st_spec = pl.BlockSpec((1, 2, 1, HEAD_D, HEAD_D), lambda s, h: (s, 0, h, 0, 0))
    so_spec, so_shape = _state_out(nseq, slot, nslots, 1)
    return _shared_out_call(
        functools.partial(_hgrn_kernel, t=t, c=c), [y_prev, s_prev],
        [col(0), col(4), col(8), col(12), col(16),
         pl.BlockSpec((2, HEAD_D), lambda s, h: (0, h)),
         pl.BlockSpec((1, HEAD_D), lambda s, h: (0, 0)),
         st_spec],
        (p_main, p_main, p_main, p_main, p_main, lb, norm_g.reshape(1, HEAD_D), s0),
        grid=(nseq, N_HEADS),
        out_specs=[pl.BlockSpec((t, HEAD_D), lambda s, h: (row_blk0 + s, h)), so_spec],
        out_shape=[jax.ShapeDtypeStruct((p_main.shape[0], BR_WIDTH), F32), so_shape],
        scratch_shapes=[pltpu.VMEM((t, HEAD_D), F32)],
        compiler_params=_cp(("parallel", "parallel")),
        name="hgrn2",
    )


def _dot_b(a, b):
    return _dot(a.astype(BF16), b.astype(BF16))


def _tri_solve(ms, xs, c):
    ri = lax.broadcasted_iota(jnp.int32, (c, c), 0)
    ci = lax.broadcasted_iota(jnp.int32, (c, c), 1)
    eye = jnp.where(ri == ci, 1.0, 0.0)
    m8 = [jnp.where((ri >> 3) == (ci >> 3), m, 0.0) for m in ms]
    ts = [eye - m for m in m8]
    ps = [_dot_b(m, m) for m in m8]
    ts = [t + _dot_b(t, p) for t, p in zip(ts, ps)]
    ps = [_dot_b(p, p) for p in ps]
    ts = [t + _dot_b(t, p) for t, p in zip(ts, ps)]
    shift = 3
    while (1 << shift) < c:
        join = ((ri >> (shift + 1)) == (ci >> (shift + 1))) & ((ri >> shift) != (ci >> shift))
        lts = [_dot_b(jnp.where(join, m, 0.0), t) for m, t in zip(ms, ts)]
        ts = [t - _dot_b(t, lt) for t, lt in zip(ts, lts)]
        shift += 1
    return [_dot_b(t, x) for t, x in zip(ts, xs)]


def _gdn_kernel(cq_ref, ck_ref, cv_ref, cg_ref, ab_ref, cw_ref, alog_ref, dtb_ref, g_ref, s0_ref,
                y_ref, s_ref, q_s, k_s, v_s, la_s, be_s, w_s, qk_s, qe_s, kd_s, dec_s, o_acc,
                *, t, c, span, hg):
    nchunk = t // c
    cps = span // c
    rowt = lax.broadcasted_iota(jnp.int32, (t, HEAD_D), 0)
    lane = lax.broadcasted_iota(jnp.int32, (t, LANES), 1)
    ri = lax.broadcasted_iota(jnp.int32, (span, span), 0)
    ci = lax.broadcasted_iota(jnp.int32, (span, span), 1)
    shift = int(math.log2(c))
    same = (ri >> shift) == (ci >> shift)
    heads = range(hg)
    dirs = (0, 1)

    def conv(x_ref, j, hh):
        x = x_ref[:, hh * HEAD_D:(hh + 1) * HEAD_D]
        w = cw_ref[j][:, hh * HEAD_D:(hh + 1) * HEAD_D]
        xm = jnp.where(rowt == 0, 0.0, pltpu.roll(x, 1, 0))
        xp = jnp.where(rowt == t - 1, 0.0, pltpu.roll(x, t - 1, 0))
        return xm * w[0:1] + x * w[1:2] + xp * w[2:3]

    ab = ab_ref[...]
    for hh in heads:
        h = pl.program_id(1) * hg + hh
        q = _silu(conv(cq_ref, 0, hh))
        q_s[hh] = q * lax.rsqrt(jnp.sum(q * q, axis=-1, keepdims=True) + EPS) * (HEAD_D ** -0.5)
        k = _silu(conv(ck_ref, 1, hh))
        k_s[hh] = k * lax.rsqrt(jnp.sum(k * k, axis=-1, keepdims=True) + EPS)
        v_s[hh] = _silu(conv(cv_ref, 2, hh))
        for d in dirs:
            a_col = jnp.sum(jnp.where(lane == d * N_HEADS + h, ab, 0.0), axis=-1, keepdims=True)
            b_col = jnp.sum(jnp.where(lane == 2 * N_HEADS + d * N_HEADS + h, ab, 0.0), axis=-1, keepdims=True)
            coef = -jnp.exp(jnp.full((1, LANES), alog_ref[d, h], F32))
            xs = a_col + dtb_ref[d, h]
            softplus = jnp.maximum(xs, 0.0) + jnp.log1p(jnp.exp(-jnp.abs(xs)))
            la_s[hh, d] = coef * jnp.broadcast_to(softplus, (t, LANES))
            be_s[hh, d] = jnp.broadcast_to(jax.nn.sigmoid(b_col), (t, LANES))
    o_acc[...] = jnp.zeros(o_acc.shape, F32)

    def wy_span(sp, carry):
        r0 = pl.multiple_of(sp * span, span)
        ms, xs, dst = [], [], []
        for hh in heads:
            qsp = q_s[hh, pl.ds(r0, span), :]
            ksp = k_s[hh, pl.ds(r0, span), :]
            vsp = v_s[hh, pl.ds(r0, span), :]
            kb = ksp.astype(BF16)
            kk = _dot_nt(kb, kb)
            qk = _dot_nt(qsp.astype(BF16), kb)
            for d in dirs:
                be = be_s[hh, d, pl.ds(r0, span), :]
                g = _chunk_cumsum(la_s[hh, d, pl.ds(r0, span), :], d, c)
                gt = g.T
                gcol = jnp.concatenate([g] * (span // LANES), axis=1)
                grow = jnp.concatenate([gt] * (span // LANES), axis=0)
                incl = same & ((ci <= ri) if d == 0 else (ci >= ri))
                gamma = jnp.where(incl, jnp.exp(jnp.minimum(gcol - grow, 0.0)), 0.0)
                bcol = jnp.concatenate([be] * (span // LANES), axis=1)
                m_sp = jnp.where(ci == ri, 0.0, bcol * kk * gamma)
                qkg = (qk * gamma).astype(BF16)
                eg = jnp.exp(g)
                xv = be * vsp
                xk = be * eg * ksp
                qe_s[hh, d, pl.ds(r0, span), :] = (qsp * eg).astype(BF16)
                for j in range(cps):
                    lo = j * c
                    rj = pl.multiple_of(r0 + lo, c)
                    ms.append(m_sp[lo:lo + c, lo:lo + c])
                    xs.append(jnp.concatenate([xv[lo:lo + c], xk[lo:lo + c]], axis=1))
                    dst.append((hh, d, rj))
                    qk_s[hh, d, pl.ds(rj, c), 0:c] = qkg[lo:lo + c, lo:lo + c]
                    gj = g[lo:lo + c]
                    glast = gj[c - 1:c] if d == 0 else gj[0:1]
                    kd_s[hh, d, pl.ds(rj, c), :] = (ksp[lo:lo + c] * jnp.exp(glast - gj)).astype(BF16)
                    dec_s[hh, d, pl.ds(sp * cps + j, 1), :] = jnp.exp(glast)
        for (hh, d, rj), w in zip(dst, _tri_solve(ms, xs, c)):
            w_s[hh, d, pl.ds(rj, c), :] = w
        return carry

    if t == span:
        wy_span(0, 0)
    else:
        lax.fori_loop(0, t // span, wy_span, 0)

    chains = [(hh, d) for hh in heads for d in dirs]

    def scan_step(i, carry):
        cidx = (i, nchunk - 1 - i)
        r0 = [pl.multiple_of(ck * c, c) for ck in cidx]
        sb = [s.astype(BF16) for s in carry]
        ws = [w_s[hh, d, pl.ds(r0[d], c), :] for hh, d in chains]
        wks = [_dot(w[:, HEAD_D:].astype(BF16), s) for w, s in zip(ws, sb)]
        oq = [_dot(qe_s[hh, d, pl.ds(r0[d], c), :], s) for (hh, d), s in zip(chains, sb)]
        vnb = [(w[:, :HEAD_D] - wk).astype(BF16) for w, wk in zip(ws, wks)]
        new = [s * dec_s[hh, d, pl.ds(cidx[d], 1), :] + _dot_tn(kd_s[hh, d, pl.ds(r0[d], c), :], v)
               for (hh, d), s, v in zip(chains, carry, vnb)]
        for (hh, d), o, v in zip(chains, oq, vnb):
            o_acc[hh, pl.ds(r0[d], c), :] += o + _dot(qk_s[hh, d, pl.ds(r0[d], c), 0:c], v)
        return tuple(new)

    finals = lax.fori_loop(0, nchunk, scan_step, tuple(s0_ref[0, d, hh] for hh, d in chains),
                           unroll=nchunk <= 4)
    for (hh, d), s in zip(chains, finals):
        s_ref[0, d, hh] = s
    for hh in heads:
        sl = slice(hh * HEAD_D, (hh + 1) * HEAD_D)
        y_ref[:, sl] = _rms(o_acc[hh], g_ref[...]) * _silu(cg_ref[:, sl])


def _gdn(p_main, p_ab, conv_w, a_log, dt_bias, norm_g, s0, row_blk0, nseq, t, y_prev=None, s_prev=None,
         slot=0, nslots=1):
    c = GDN_CHUNK
    hg = GDN_HEADS_PER_STEP if t * GDN_HEADS_PER_STEP <= GDN_MAX_ROWS_PER_STEP else GDN_HEADS_PER_STEP // 2
    wid = hg * HEAD_D
    blk = BR_WIDTH // wid

    def col(base):
        return pl.BlockSpec((t, wid), lambda s, h: (row_blk0 + s, base * blk + h))

    st_spec = pl.BlockSpec((1, 2, hg, HEAD_D, HEAD_D), lambda s, h: (s, 0, h, 0, 0))
    smem = pl.BlockSpec(memory_space=pltpu.SMEM)
    seq_buf = pltpu.VMEM((hg, t, HEAD_D), F32)
    dir_f32 = pltpu.VMEM((hg, 2, t, LANES), F32)
    dir_bf16 = pltpu.VMEM((hg, 2, t, LANES), BF16)
    scratch = [seq_buf, seq_buf, seq_buf, dir_f32, dir_f32,
               pltpu.VMEM((hg, 2, t, 2 * HEAD_D), F32), dir_bf16, dir_bf16, dir_bf16,
               pltpu.VMEM((hg, 2, max(t // c, 8), LANES), F32), seq_buf]
    so_spec, so_shape = _state_out(nseq, slot, nslots, hg)
    return _shared_out_call(
        functools.partial(_gdn_kernel, t=t, c=c, span=GDN_SPAN, hg=hg), [y_prev, s_prev],
        [col(6), col(7), col(8), col(9),
         pl.BlockSpec((t, LANES), lambda s, h: (row_blk0 + s, 0)),
         pl.BlockSpec((3, 3, wid), lambda s, h: (0, 0, h)),
         smem, smem,
         pl.BlockSpec((1, HEAD_D), lambda s, h: (0, 0)),
         st_spec],
        (p_main, p_main, p_main, p_main, p_ab, conv_w, a_log, dt_bias, norm_g.reshape(1, HEAD_D), s0),
        grid=(nseq, N_HEADS // hg),
        out_specs=[pl.BlockSpec((t, wid), lambda s, h: (row_blk0 + s, h)), so_spec],
        out_shape=[jax.ShapeDtypeStruct((p_main.shape[0], BR_WIDTH), F32), so_shape],
        scratch_shapes=scratch,
        compiler_params=_cp(("parallel", "parallel")),
        name="gdn",
    )


def _cmul(ar, ai, br, bi):
    return ar * br - ai * bi, ar * bi + ai * br


def _s5_kernel(u_ref, bb_ref, cc_ref, lam_ref, dsk_ref, s0_ref, y_ref, sf_ref, up_s, x_s, yp_s,
               *, tb, seg, segs_per_seq, zero_init):
    r_par = tb // seg
    half = S5_GB * S5_STATE
    for r in range(seg):
        up_s[r * r_par:(r + 1) * r_par, :] = u_ref[pl.ds(r, r_par, stride=seg), :]
    ub = up_s[...].astype(BF16)
    sf_ref[...] = jnp.zeros(sf_ref.shape, F32)
    nseq = r_par // segs_per_seq
    lam = [(lam_ref[0, d, 0:1, :], lam_ref[0, d, 1:2, :]) for d in (0, 1)]
    for d in (0, 1):
        x_s[d] = _dot(ub, bb_ref[0, d])

    def offsets(i):
        return [pl.multiple_of(r * r_par, r_par) for r in (i, seg - 1 - i)]

    def local_step(i, carry):
        off = offsets(i)
        new = []
        for d in (0, 1):
            (lre, lim), (hre, him) = lam[d], carry[d]
            nre = lre * hre - lim * him + x_s[d, pl.ds(off[d], r_par), 0:half]
            nim = lre * him + lim * hre + x_s[d, pl.ds(off[d], r_par), half:2 * half]
            x_s[d, pl.ds(off[d], r_par), 0:half] = nre
            x_s[d, pl.ds(off[d], r_par), half:2 * half] = nim
            new.append((nre, nim))
        return tuple(new)

    zero = jnp.zeros((r_par, half), F32)
    lax.fori_loop(0, seg, local_step, ((zero, zero), (zero, zero)))

    end_rows = ((seg - 1) * r_par, 0)
    needs_carry = not (zero_init and segs_per_seq == 1)
    hins = []
    for d in (0, 1):
        if not needs_carry:
            break
        lre, lim = lam[d]
        pre, pim = lre, lim
        n = 1
        while n < seg:
            pre, pim = _cmul(pre, pim, pre, pim)
            n *= 2
        end_row = end_rows[d]
        ere = x_s[d, end_row:end_row + r_par, 0:half]
        eim = x_s[d, end_row:end_row + r_par, half:2 * half]
        hin = [None] * r_par
        order = range(r_par) if d == 0 else range(r_par - 1, -1, -1)
        for j in order:
            sq, pos = divmod(j, segs_per_seq)
            first = pos == (0 if d == 0 else segs_per_seq - 1)
            if first:
                hin[j] = (s0_ref[0, 0, d, sq:sq + 1, 0:half], s0_ref[0, 0, d, sq:sq + 1, half:2 * half])
            else:
                pj = j - 1 if d == 0 else j + 1
                cre, cim = _cmul(pre, pim, hin[pj][0], hin[pj][1])
                hin[j] = (cre + ere[pj:pj + 1], cim + eim[pj:pj + 1])
        hins.append((jnp.concatenate([hj[0] for hj in hin], axis=0),
                     jnp.concatenate([hj[1] for hj in hin], axis=0)))

    def carry_step(i, pws):
        off = offsets(i)
        new = []
        for d in (0, 1):
            (lre, lim), (wre, wim) = lam[d], pws[d]
            are, aim = _cmul(wre, wim, hins[d][0], hins[d][1])
            x_s[d, pl.ds(off[d], r_par), 0:half] += are
            x_s[d, pl.ds(off[d], r_par), half:2 * half] += aim
            new.append(_cmul(wre, wim, lre, lim))
        return tuple(new)

    if needs_carry:
        lax.fori_loop(0, seg, carry_step, (lam[0], lam[1]))

    for d in (0, 1):
        for sq in range(nseq):
            j = (sq + 1) * segs_per_seq - 1 if d == 0 else sq * segs_per_seq
            row = end_rows[d] + j
            sf_ref[0, 0, d, sq:sq + 1, :] = x_s[d, row:row + 1, :]
    yp_s[...] = (_dot(x_s[0].astype(BF16), cc_ref[0, 0]) + _dot(x_s[1].astype(BF16), cc_ref[0, 1])
                 + up_s[...] * dsk_ref[...])
    for r in range(seg):
        y_ref[pl.ds(r, r_par, stride=seg), :] = yp_s[r * r_par:(r + 1) * r_par, :]


def _s5(p_main, bb, cc, lam, dskip, s0, row_blk0, nblk, tb, t, seg, zero_init, y_prev=None):
    ngb = S5_GROUPS // S5_GB
    w2 = 2 * S5_GB * S5_STATE
    return _shared_out_call(
        functools.partial(_s5_kernel, tb=tb, seg=seg, segs_per_seq=t // seg, zero_init=zero_init), [y_prev, None],
        [
            pl.BlockSpec((tb, LANES), lambda i, gb: (row_blk0 + i, 20 + gb)),
            pl.BlockSpec((1, 2, LANES, w2), lambda i, gb: (gb, 0, 0, 0)),
            pl.BlockSpec((1, 2, w2, LANES), lambda i, gb: (gb, 0, 0, 0)),
            pl.BlockSpec((1, 2, 2, w2 // 2), lambda i, gb: (gb, 0, 0, 0)),
            pl.BlockSpec((1, LANES), lambda i, gb: (0, gb)),
            pl.BlockSpec((1, 1, 2, 8, w2), lambda i, gb: (i, gb, 0, 0, 0)),
        ],
        (p_main, bb, cc, lam, dskip.reshape(1, BR_WIDTH), s0),
        grid=(nblk, ngb),
        out_specs=[pl.BlockSpec((tb, LANES), lambda i, gb: (row_blk0 + i, gb)),
                   pl.BlockSpec((1, 1, 2, 8, w2), lambda i, gb: (i, gb, 0, 0, 0))],
        out_shape=[jax.ShapeDtypeStruct((p_main.shape[0], BR_WIDTH), F32),
                   jax.ShapeDtypeStruct((nblk, ngb, 2, 8, w2), F32)],
        scratch_shapes=[pltpu.VMEM((tb, LANES), F32), pltpu.VMEM((2, tb, w2), F32), pltpu.VMEM((tb, LANES), F32)],
        compiler_params=_cp(("parallel", "parallel")),
        name="s5",
    )


def _s5_params(a_re, a_im, log_dt, b_re, b_im, c_re, c_im):
    dt = jnp.exp(log_dt)[..., None]
    mag = jnp.exp(dt * a_re)
    lam_re = mag * jnp.cos(dt * a_im)
    lam_im = mag * jnp.sin(dt * a_im)
    den = a_re * a_re + a_im * a_im
    co_re = ((lam_re - 1.0) * a_re + lam_im * a_im) / den
    co_im = (lam_im * a_re - (lam_re - 1.0) * a_im) / den
    bb_re = co_re[..., None] * b_re - co_im[..., None] * b_im
    bb_im = co_re[..., None] * b_im + co_im[..., None] * b_re
    ngb = S5_GROUPS // S5_GB
    eye = jnp.eye(S5_GB, dtype=F32)

    def pack_in(w):
        w = w.reshape(2, ngb, S5_GB, S5_STATE, S5_GROUP)
        blk = jnp.einsum('dbgpi,gh->bdgihp', w, eye)
        return blk.reshape(ngb, 2, S5_GB * S5_GROUP, S5_GB * S5_STATE)

    def pack_out(w):
        w = w.reshape(2, ngb, S5_GB, S5_GROUP, S5_STATE)
        blk = jnp.einsum('dbgip,gh->bdgphi', w, eye)
        return blk.reshape(ngb, 2, S5_GB * S5_STATE, S5_GB * S5_GROUP)

    bb = jnp.concatenate([pack_in(bb_re), pack_in(bb_im)], axis=-1).astype(BF16)
    cc = jnp.concatenate([pack_out(c_re), -pack_out(c_im)], axis=-2).astype(BF16)
    lam = jnp.stack([lam_re, lam_im], axis=1)
    lam = lam.reshape(2, 2, ngb, S5_GB * S5_STATE).transpose(2, 0, 1, 3)
    return bb, cc, lam


def _glu_kernel(y_ref, w_ref, b_ref, o_ref, wb_ref):
    @pl.when(pl.program_id(0) == 0)
    def _():
        wb_ref[...] = w_ref[...].astype(BF16)

    z = _dot(y_ref[...].astype(BF16), wb_ref[...]) + b_ref[...]
    half = z.shape[1] // 2
    o_ref[...] = z[:, :half] * jax.nn.sigmoid(z[:, half:])


def _glu(y, w, b, l):
    n, k = y.shape
    n2 = w.shape[-1]
    return pl.pallas_call(
        _glu_kernel,
        grid=(n // TM,),
        in_specs=[pl.BlockSpec((TM, k), lambda i: (i, 0)),
                  pl.BlockSpec((None, k, n2), lambda i: (l, 0, 0)),
                  pl.BlockSpec((None, 1, n2), lambda i: (l, 0, 0))],
        out_specs=pl.BlockSpec((TM, n2 // 2), lambda i: (i, 0)),
        out_shape=jax.ShapeDtypeStruct((n, n2 // 2), F32),
        scratch_shapes=[pltpu.VMEM((k, n2), BF16)],
        compiler_params=_cp(("arbitrary",)),
        name="s5_glu",
    )(y, w, b.reshape(b.shape[0], 1, n2))


def _merge_kernel(ya_ref, yb_ref, yc_ref, mg_ref, wbr_ref, wout_ref, g_ref, o_ref, wbr_s, wout_s):
    @pl.when(pl.program_id(0) == 0)
    def _():
        wbr_s[...] = wbr_ref[...].astype(BF16)
        wout_s[...] = wout_ref[...].astype(BF16)

    d = wout_s.shape[0]
    acc = None
    for n, br_ref in enumerate((ya_ref, yb_ref, yc_ref)):
        proj = _dot(br_ref[...].astype(BF16), wbr_s[n])
        term = proj * jax.nn.sigmoid(mg_ref[:, n * d:(n + 1) * d])
        acc = term if acc is None else acc + term
    y = _dot(acc.astype(BF16), wout_s[...])
    o_ref[...] = _rms(y, g_ref[...])


def _merge(ya, yb, yc, mg, w_branch, w_out, g, l):
    n = ya.shape[0]
    d = w_out.shape[-1]
    tm = 256
    br = pl.BlockSpec((tm, BR_WIDTH), lambda i: (i, 0))
    return pl.pallas_call(
        _merge_kernel,
        grid=(n // tm,),
        in_specs=[br, br, br,
                  pl.BlockSpec((tm, 3 * d), lambda i: (i, 0)),
                  pl.BlockSpec((None, 3, BR_WIDTH, d), lambda i: (l, 0, 0, 0)),
                  pl.BlockSpec((None, d, d), lambda i: (l, 0, 0)),
                  pl.BlockSpec((1, d), lambda i: (0, 0))],
        out_specs=pl.BlockSpec((tm, d), lambda i: (i, 0)),
        out_shape=jax.ShapeDtypeStruct((n, d), F32),
        scratch_shapes=[pltpu.VMEM((3, BR_WIDTH, d), BF16), pltpu.VMEM((d, d), BF16)],
        compiler_params=_cp(("arbitrary",)),
        name="merge",
    )(ya, yb, yc, mg, w_branch, w_out, g.reshape(1, d))


def _swiglu_kernel(*refs, gated):
    if gated:
        x_ref, wg_ref, wu_ref, gt_ref, o_ref, wg_s, wu_s = refs
    else:
        x_ref, wg_ref, wu_ref, o_ref, wg_s, wu_s = refs

    @pl.when(pl.program_id(1) == 0)
    def _():
        wg_s[...] = wg_ref[...].astype(BF16)
        wu_s[...] = wu_ref[...].astype(BF16)

    x = x_ref[...]
    hid = _silu(_dot(x, wg_s[...])) * _dot(x, wu_s[...])
    if gated:
        gt = gt_ref[...]
        lane = lax.broadcasted_iota(jnp.int32, gt.shape, 1)
        hid = hid * jnp.sum(jnp.where(lane == pl.program_id(0), gt, 0.0), axis=-1, keepdims=True)
    o_ref[...] = hid.astype(BF16)


def _swiglu_up(x, wg, wu, e, f, gates=None):
    n, d = x.shape
    gated = gates is not None
    if gated:
        ne = wg.shape[1]
        wspec = pl.BlockSpec((None, None, d, f), lambda j, i: (e, j, 0, 0))
    else:
        ne = wg.shape[-1] // f
        wspec = pl.BlockSpec((None, d, f), lambda j, i: (e, 0, j))
    in_specs = [pl.BlockSpec((TM, d), lambda j, i: (i, 0)), wspec, wspec]
    args = [x, wg, wu]
    if gated:
        in_specs.append(pl.BlockSpec((TM, LANES), lambda j, i: (i, 0)))
        args.append(gates)
    return pl.pallas_call(
        functools.partial(_swiglu_kernel, gated=gated),
        grid=(ne, n // TM),
        in_specs=in_specs,
        out_specs=pl.BlockSpec((TM, f), lambda j, i: (i, j)),
        out_shape=jax.ShapeDtypeStruct((n, ne * f), BF16),
        scratch_shapes=[pltpu.VMEM((d, f), BF16), pltpu.VMEM((d, f), BF16)],
        compiler_params=_cp(("arbitrary", "arbitrary")),
        name="swiglu_up",
    )(*args)


def _down_kernel(x_ref, w_ref, g_ref, o_ref, acc):
    k = pl.program_id(1)
    part = _dot(x_ref[...], w_ref[...].astype(BF16))

    @pl.when(k == 0)
    def _():
        acc[...] = part

    @pl.when(k > 0)
    def _():
        acc[...] += part

    @pl.when(k == pl.num_programs(1) - 1)
    def _():
        o_ref[...] = _rms(acc[...], g_ref[...])


def _down_norm(hid, wd, e, g, tm):
    n = hid.shape[0]
    _, ne, f, d = wd.shape
    return pl.pallas_call(
        _down_kernel,
        grid=(n // tm, ne),
        in_specs=[pl.BlockSpec((tm, f), lambda i, k: (i, k)),
                  pl.BlockSpec((None, None, f, d), lambda i, k: (e, k, 0, 0)),
                  pl.BlockSpec((1, d), lambda i, k: (0, 0))],
        out_specs=pl.BlockSpec((tm, d), lambda i, k: (i, 0)),
        out_shape=jax.ShapeDtypeStruct((n, d), F32),
        scratch_shapes=[pltpu.VMEM((tm, d), F32)],
        compiler_params=_cp(("parallel", "arbitrary")),
        name="down_norm",
    )(hid, wd, g.reshape(1, d))


def _router_kernel(x_ref, w_ref, o_ref, *, ne):
    logits = _dot(x_ref[...], w_ref[...].astype(BF16))
    lane = lax.broadcasted_iota(jnp.int32, logits.shape, 1).astype(F32)
    neg = jnp.float32(-jnp.inf)
    logits = jnp.where(lane < ne, logits, neg)
    m1 = jnp.max(logits, axis=-1, keepdims=True)
    i1 = jnp.min(jnp.where(logits == m1, lane, float(LANES)), axis=-1, keepdims=True)
    rest = jnp.where(lane == i1, neg, logits)
    m2 = jnp.max(rest, axis=-1, keepdims=True)
    i2 = jnp.min(jnp.where(rest == m2, lane, float(LANES)), axis=-1, keepdims=True)
    e2 = jnp.exp(m2 - m1)
    w1 = 1.0 / (1.0 + e2)
    w2 = e2 / (1.0 + e2)
    o_ref[...] = jnp.where(lane == i1, w1, 0.0) + jnp.where(lane == i2, w2, 0.0)


def _router(x, w_pad, ne):
    n, d = x.shape
    return pl.pallas_call(
        functools.partial(_router_kernel, ne=ne),
        grid=(n // TM,),
        in_specs=[pl.BlockSpec((TM, d), lambda i: (i, 0)),
                  pl.BlockSpec((d, LANES), lambda i: (0, 0))],
        out_specs=pl.BlockSpec((TM, LANES), lambda i: (i, 0)),
        out_shape=jax.ShapeDtypeStruct((n, LANES), F32),
        compiler_params=_cp(("parallel",)),
        name="router",
    )(x, w_pad)


def _to_column_major(x, b, t):
    d = x.shape[-1]
    return x.reshape(b, t // GRID_W, GRID_W, d).transpose(0, 2, 1, 3).reshape(b * t, d)


def _to_row_major(x, b, t):
    d = x.shape[-1]
    return x.reshape(b, GRID_W, t // GRID_W, d).transpose(0, 2, 1, 3).reshape(b * t, d)


def kernel(x_prompt, x_sample, state_hgrn, state_s5, state_gdn, c, c_ctx, ada_w, ada_b, norm_g, w_in, hgrn_lb, hgrn_norm, s5_a_re, s5_a_im, s5_log_dt, s5_b_re, s5_b_im, s5_c_re, s5_c_im, s5_d, s5_glu_w, s5_glu_b, gdn_conv, gdn_a_log, gdn_dt_bias, gdn_norm, w_branch, w_out, ffn_wg, ffn_wu, ffn_wd, moe_router, moe_wg, moe_wu, moe_wd):
    bp, tp, d = x_prompt.shape
    bs, ts, _ = x_sample.shape
    depth = w_in.shape[0]
    n_p, n_s = bp * tp, bs * ts
    ne = moe_router.shape[-1]
    main_cols = 10 * BR_WIDTH
    ab_cols = 4 * N_HEADS

    c8 = jnp.zeros((8, d), F32).at[0].set(c_ctx).at[1:1 + bs].set(c)
    mod_all = _ada_mod(c8, ada_w, ada_b)

    lb_soft = jax.nn.softmax(hgrn_lb.astype(F32), axis=0)
    lb_all = jnp.cumsum(lb_soft, axis=0) - lb_soft[:1]

    mods = mod_all[:, :1 + bs].reshape(depth, 1 + bs, 6, d)
    zero_row = jnp.zeros((1 + bs, d), F32)

    def modv(gate, scale, shift, g):
        rows = [zero_row if sel is None else mods[sel[0], :, sel[1]] for sel in (gate, scale, shift)]
        rows.append(zero_row if g is None else jnp.broadcast_to(g, (1 + bs, d)))
        return jnp.stack(rows + [zero_row] * 4, axis=1)

    x = jnp.concatenate([x_prompt.reshape(n_p, d), x_sample.reshape(n_s, d)], axis=0)
    (h,) = _resnorm(x, None, modv(None, (0, 1), (0, 0), norm_g[0, 0]), n_p, ts, True)

    z_hgrn = jnp.zeros((bp, 2, N_HEADS, HEAD_D, HEAD_D), F32)
    ngb = S5_GROUPS // S5_GB
    w2 = 2 * S5_GB * S5_STATE
    st_s5 = []
    y_zero = jnp.zeros((n_p + n_s, BR_WIDTH), F32)
    new_hgrn = new_gdn = jnp.zeros((bp, depth, 2, N_HEADS, HEAD_D, HEAD_D), F32)
    w_in_t = jnp.swapaxes(w_in, 1, 2)
    s5_tb = 1024

    for l in range(depth):
        col_major = l % 2 == 1
        if col_major:
            h = jnp.concatenate([h[:n_p], _to_column_major(h[n_p:], bs, ts)], axis=0)
        p_main = _mm_t(h, w_in_t, l, main_cols, 2 * BR_WIDTH, "w_in_main", tm=1024)
        w_ab = jnp.pad(w_in_t[l, main_cols:main_cols + ab_cols], ((0, LANES - ab_cols), (0, 0)))
        p_ab = _mm_t(h, w_ab, None, LANES, LANES, "w_in_ab", tm=1024)
        p_mg = _mm_t(h, w_in_t[l, main_cols + ab_cols:], None, 3 * d, 2 * BR_WIDTH, "w_in_merge", tm=1024)

        ya, new_hgrn = _hgrn(p_main, lb_all[l], hgrn_norm[l], z_hgrn, 0, bp, tp, y_zero, new_hgrn, l, depth)
        ya, _ = _hgrn(p_main, lb_all[l], hgrn_norm[l], state_hgrn[:, l].astype(F32), n_p // ts, bs, ts, ya)

        bb, cc, lam = _s5_params(s5_a_re[l], s5_a_im[l], s5_log_dt[l], s5_b_re[l], s5_b_im[l],
                                 s5_c_re[l], s5_c_im[l])
        seq_per_blk = 8
        tb_p = seq_per_blk * tp
        nblk_p = n_p // tb_p
        yb, sb = _s5(p_main, bb, cc, lam, s5_d[l], jnp.zeros((nblk_p, ngb, 2, 8, w2), F32),
                     0, nblk_p, tb_p, tp, tp, True, y_zero)
        s0s = state_s5[:, l].astype(F32)
        s0s = s0s.reshape(bs, 2, ngb, S5_GB * S5_STATE, 2).transpose(0, 2, 1, 4, 3).reshape(bs, ngb, 2, 1, w2)
        s0s = jnp.pad(s0s, ((0, 0), (0, 0), (0, 0), (0, 7), (0, 0)))
        yb, _ = _s5(p_main, bb, cc, lam, s5_d[l], s0s, n_p // s5_tb, n_s // s5_tb, s5_tb, ts, S5_SEG, False, yb)
        yb = _glu(yb, s5_glu_w, s5_glu_b, l)
        sb = sb[:, :, :, :seq_per_blk].reshape(nblk_p, ngb, 2, seq_per_blk, 2, S5_GB, S5_STATE)
        sb = sb.transpose(0, 3, 2, 1, 5, 6, 4).reshape(bp, 2, S5_GROUPS, S5_STATE, 2)
        st_s5.append(sb)

        yc, new_gdn = _gdn(p_main, p_ab, gdn_conv[l], gdn_a_log[l], gdn_dt_bias[l], gdn_norm[l], z_hgrn, 0, bp, tp,
                           y_zero, new_gdn, l, depth)
        yc, _ = _gdn(p_main, p_ab, gdn_conv[l], gdn_a_log[l], gdn_dt_bias[l], gdn_norm[l],
                     state_gdn[:, l].astype(F32), n_p // ts, bs, ts, yc)

        r_m = _merge(ya, yb, yc, p_mg, w_branch, w_out, norm_g[l, 1], l)
        if col_major:
            r_m = jnp.concatenate([r_m[:n_p], _to_row_major(r_m[n_p:], bs, ts)], axis=0)
        x, h2 = _resnorm(x, r_m, modv((l, 2), (l, 4), (l, 3), norm_g[l, 2]), n_p, ts, True)

        if l % 2 == 0:
            e = l // 2
            hid = _swiglu_up(h2, ffn_wg, ffn_wu, e, ffn_wg.shape[-1] // 2)
            r_f = _down_norm(hid, ffn_wd[:, None], e, norm_g[l, 3], TM)
        else:
            e = l // 2
            gates = _router(h2, jnp.pad(moe_router[e], ((0, 0), (0, LANES - ne))), ne)
            hid = _swiglu_up(h2, moe_wg, moe_wu, e, moe_wg.shape[-1], gates)
            r_f = _down_norm(hid, moe_wd, e, norm_g[l, 3], 1024)

        if l + 1 < depth:
            x, h = _resnorm(x, r_f, modv((l, 5), (l + 1, 1), (l + 1, 0), norm_g[l + 1, 0]), n_p, ts, True)
        else:
            mv = modv((l, 5), None, None, None)
            (y_prompt,) = _resnorm(x, r_f, mv, n_p, ts, False, rows=(0, n_p))
            (y_sample,) = _resnorm(x, r_f, mv, n_p, ts, False, rows=(n_p, n_s))

    y_prompt = y_prompt.reshape(bp, tp, d)
    y_sample = y_sample.reshape(bs, ts, d)
    new_s5 = jnp.stack(st_s5, axis=1).astype(x_prompt.dtype)
    return (y_prompt, y_sample, new_hgrn.astype(x_prompt.dtype), new_s5, new_gdn.astype(x_prompt.dtype))
```

```python
import functools
import math

import jax
import jax.numpy as jnp
from jax import lax
from jax.experimental import pallas as pl
from jax.experimental.pallas import tpu as pltpu

F32 = jnp.float32
BF16 = jnp.bfloat16

EPS = 1e-6
F_FLOOR = 1e-6
LANES = 128
GRID_W = 64
HEAD_D = 128
N_HEADS = 4
BR_WIDTH = 512
S5_GROUPS = 32
S5_GROUP = 16
S5_STATE = 64
S5_GB = 8
S5_SEG = 64
GLA_CHUNK = 256
GDN_CHUNK = 64
GDN_SPAN = 256
GDN_HEADS_PER_STEP = 4
GDN_MAX_ROWS_PER_STEP = 2048
TM = 512
MOE_TILE = 256
VMEM_LIMIT = 56 * 1024 * 1024


def _cp(sem, vmem=VMEM_LIMIT):
    return pltpu.CompilerParams(dimension_semantics=sem, vmem_limit_bytes=vmem)


def _dot(a, b):
    return jnp.dot(a, b, preferred_element_type=F32)


def _dot_nt(a, b):
    return lax.dot_general(a, b, (((1,), (1,)), ((), ())), preferred_element_type=F32)


def _dot_tn(a, b):
    return lax.dot_general(a, b, (((0,), (0,)), ((), ())), preferred_element_type=F32)


def _silu(x):
    return x * jax.nn.sigmoid(x)


def _rms(x, g):
    return x * lax.rsqrt(jnp.mean(x * x, axis=-1, keepdims=True) + EPS) * g


def _chunk_cumsum(x, d, c):
    n = x.shape[0]
    pos = lax.broadcasted_iota(jnp.int32, x.shape, 0) & (c - 1)
    s = 1
    while s < c:
        if d == 0:
            x = x + jnp.where(pos >= s, pltpu.roll(x, s, 0), 0.0)
        else:
            x = x + jnp.where(pos < c - s, pltpu.roll(x, n - s, 0), 0.0)
        s *= 2
    return x


def _ada_kernel(c_ref, w_ref, b_ref, o_ref):
    c = c_ref[...]
    a = _silu(c).astype(BF16)
    o_ref[...] = _dot(a, w_ref[...].astype(BF16)) + b_ref[...]


def _ada_mod(c8, ada_w, ada_b):
    depth, d, n6 = ada_w.shape
    tn = 512
    return pl.pallas_call(
        _ada_kernel,
        grid=(depth, n6 // tn),
        in_specs=[
            pl.BlockSpec((8, d), lambda l, j: (0, 0)),
            pl.BlockSpec((None, d, tn), lambda l, j: (l, 0, j)),
            pl.BlockSpec((None, 1, tn), lambda l, j: (l, 0, j)),
        ],
        out_specs=pl.BlockSpec((None, 8, tn), lambda l, j: (l, 0, j)),
        out_shape=jax.ShapeDtypeStruct((depth, 8, n6), F32),
        compiler_params=_cp(("parallel", "parallel")),
        name="ada_mod",
    )(c8, ada_w, ada_b.reshape(depth, 1, n6))


def _resnorm_kernel(*refs, has_y, has_h):
    it = iter(refs)
    x_ref = next(it)
    y_ref = next(it) if has_y else None
    mv_ref = next(it)
    xo_ref = next(it) if has_y else None
    h_ref = next(it) if has_h else None
    x = x_ref[...]
    mv = mv_ref[...]
    if has_y:
        x = x + mv[0:1] * y_ref[...]
        xo_ref[...] = x
    if has_h:
        h = _rms(x, mv[3:4]) * (1.0 + mv[1:2]) + mv[2:3]
        h_ref[...] = h.astype(h_ref.dtype)


def _resnorm(x, y, modv, n_prompt, seq_s, has_h, rows=None, h_dtype=BF16):
    d = x.shape[1]
    row0, n = (0, x.shape[0]) if rows is None else rows
    i0 = row0 // TM
    has_y = y is not None

    def group(i):
        r = (i + i0) * TM
        return jnp.where(r < n_prompt, 0, 1 + (r - n_prompt) // seq_s)

    tile = pl.BlockSpec((TM, d), lambda i: (i + i0, 0))
    in_specs = [tile] + ([tile] if has_y else []) + [pl.BlockSpec((None, 8, d), lambda i: (group(i), 0, 0))]
    out_tile = pl.BlockSpec((TM, d), lambda i: (i, 0))
    out_specs, out_shape = [], []
    if has_y:
        out_specs.append(out_tile)
        out_shape.append(jax.ShapeDtypeStruct((n, d), F32))
    if has_h:
        out_specs.append(out_tile)
        out_shape.append(jax.ShapeDtypeStruct((n, d), h_dtype))
    args = [x] + ([y] if has_y else []) + [modv]
    outs = pl.pallas_call(
        functools.partial(_resnorm_kernel, has_y=has_y, has_h=has_h),
        grid=(n // TM,),
        in_specs=in_specs,
        out_specs=out_specs,
        out_shape=out_shape,
        compiler_params=_cp(("parallel",)),
        name="resnorm",
    )(*args)
    return outs


def _mm_kernel(x_ref, w_ref, o_ref, wb_ref):
    @pl.when(pl.program_id(1) == 0)
    def _():
        wb_ref[...] = w_ref[...].astype(BF16)

    o_ref[...] = _dot_nt(x_ref[...], wb_ref[...])


def _mm_t(x, wt, l, nrows, tn, name, tm=TM):
    m, k = x.shape
    if l is None:
        wspec = pl.BlockSpec((tn, k), lambda j, i: (j, 0))
    else:
        wspec = pl.BlockSpec((None, tn, k), lambda j, i: (l, j, 0))
    return pl.pallas_call(
        _mm_kernel,
        grid=(nrows // tn, m // tm),
        in_specs=[pl.BlockSpec((tm, k), lambda j, i: (i, 0)), wspec],
        out_specs=pl.BlockSpec((tm, tn), lambda j, i: (i, j)),
        out_shape=jax.ShapeDtypeStruct((m, nrows), F32),
        scratch_shapes=[pltpu.VMEM((tn, k), BF16)],
        compiler_params=_cp(("arbitrary", "arbitrary")),
        name=name,
    )(x, wt)


def _gla_intra(q, k, b, d, c):
    rowi = lax.broadcasted_iota(jnp.int32, (c, LANES), 0)
    ri = lax.broadcasted_iota(jnp.int32, (c, c), 0)
    ci = lax.broadcasted_iota(jnp.int32, (c, c), 1)
    a = jnp.zeros((c, c), F32)
    size = c
    while size >= 8:
        half = size // 2
        nb = c // size
        b3 = b.reshape(nb, size, LANES)
        pos = rowi & (size - 1)
        if d == 0:
            ref = b3[:, half - 1:half, :]
            qsel = pos >= half
        else:
            ref = b3[:, half:half + 1, :]
            qsel = pos < half
        ref = jnp.broadcast_to(ref, (nb, size, LANES)).reshape(c, LANES)
        z = jnp.where(qsel, q, k) * jnp.exp(-jnp.abs(b - ref))
        qs = jnp.where(qsel, z, 0.0).astype(BF16)
        ks = jnp.where(qsel, 0.0, z).astype(BF16)
        p = _dot_nt(qs, ks)
        if nb > 1:
            shift = int(math.log2(size))
            p = jnp.where((ri >> shift) == (ci >> shift), p, 0.0)
        a = a + p
        size = half
    b3 = b.reshape(c // 8, 8, LANES)
    lo, hi = (0, 4) if d == 0 else (3, 7)
    ref_lo = jnp.broadcast_to(b3[:, lo:lo + 1, :], (c // 8, 8, LANES)).reshape(c, LANES)
    ref_hi = jnp.broadcast_to(b3[:, hi:hi + 1, :], (c // 8, 8, LANES)).reshape(c, LANES)
    ref = jnp.where((rowi & 7) >= 4, ref_hi, ref_lo)
    qs = (q * jnp.exp(jnp.minimum(b - ref, 0.0))).astype(BF16)
    ks = (k * jnp.exp(ref - b)).astype(BF16)
    p = _dot_nt(qs, ks)
    keep = ((ri >> 2) == (ci >> 2)) & ((ci <= ri) if d == 0 else (ci >= ri))
    return a + jnp.where(keep, p, 0.0)


def _hgrn_kernel(aq_ref, af_ref, ab_ref, ai_ref, ag_ref, lb_ref, g_ref, s0_ref, y_ref, s_ref, o_acc, *, t, c):
    nchunk = t // c
    scale = HEAD_D ** -0.5
    for d in (0, 1):
        z_ref = af_ref if d == 0 else ab_ref
        lbd = lb_ref[d:d + 1, :]

        def chunk(i, st, d=d, z_ref=z_ref, lbd=lbd):
            cidx = i if d == 0 else nchunk - 1 - i
            r0 = pl.multiple_of(cidx * c, c)
            z = z_ref[pl.ds(r0, c), :]
            f = lbd + (1.0 - lbd) * jax.nn.sigmoid(z)
            lf = jnp.log(jnp.maximum(f, F_FLOOR))
            k = (1.0 - lbd) * jax.nn.sigmoid(-z)
            q = _silu(aq_ref[pl.ds(r0, c), :]) * scale
            v = ai_ref[pl.ds(r0, c), :].astype(BF16)
            b = _chunk_cumsum(lf, d, c)
            a = _gla_intra(q, k, b, d, c)
            o = _dot(a.astype(BF16), v) + _dot_nt((q * jnp.exp(b)).astype(BF16), st.astype(BF16))
            if d == 0:
                o_acc[pl.ds(r0, c), :] = o
            else:
                o_acc[pl.ds(r0, c), :] += o
            blast = b[c - 1:c, :] if d == 0 else b[0:1, :]
            kd = (k * jnp.exp(blast - b)).astype(BF16)
            return st * jnp.exp(blast) + _dot_tn(v, kd)

        st = s0_ref[0, d, 0].T
        if nchunk == 1:
            st = chunk(0, st)
        else:
            st = lax.fori_loop(0, nchunk, chunk, st)
        s_ref[0, d, 0] = st.T
    o = o_acc[...]
    y_ref[...] = _rms(o, g_ref[...]) * _silu(ag_ref[...])


def _shared_out_call(kernel_fn, prevs, in_specs, args, **kw):
    extra = [p for p in prevs if p is not None]
    if not extra:
        return pl.pallas_call(kernel_fn, in_specs=in_specs, **kw)(*args)
    n_in = len(in_specs)
    aliases, k = {}, 0
    for out_idx, p in enumerate(prevs):
        if p is not None:
            aliases[n_in + k] = out_idx
            k += 1

    def body(*refs):
        kernel_fn(*refs[:n_in], *refs[n_in + len(extra):])

    return pl.pallas_call(body, in_specs=in_specs + [pl.BlockSpec(memory_space=pl.ANY)] * len(extra),
                          input_output_aliases=aliases, **kw)(*args, *extra)


def _state_out(nseq, slot, nslots, hg):
    spec = pl.BlockSpec((1, None, 2, hg, HEAD_D, HEAD_D), lambda s, h: (s, slot, 0, h, 0, 0))
    return spec, jax.ShapeDtypeStruct((nseq, nslots, 2, N_HEADS, HEAD_D, HEAD_D), F32)


def _hgrn(p_main, lb, norm_g, s0, row_blk0, nseq, t, y_prev=None, s_prev=None, slot=0, nslots=1):
    c = min(GLA_CHUNK, t)

    def col(base):
        return pl.BlockSpec((t, HEAD_D), lambda s, h: (row_blk0 + s, base + h))

    st_spec = pl.BlockSpec((1, 2, 1, HEAD_D, HEAD_D), lambda s, h: (s, 0, h, 0, 0))
    so_spec, so_shape = _state_out(nseq, slot, nslots, 1)
    return _shared_out_call(
        functools.partial(_hgrn_kernel, t=t, c=c), [y_prev, s_prev],
        [col(0), col(4), col(8), col(12), col(16),
         pl.BlockSpec((2, HEAD_D), lambda s, h: (0, h)),
         pl.BlockSpec((1, HEAD_D), lambda s, h: (0, 0)),
         st_spec],
        (p_main, p_main, p_main, p_main, p_main, lb, norm_g.reshape(1, HEAD_D), s0),
        grid=(nseq, N_HEADS),
        out_specs=[pl.BlockSpec((t, HEAD_D), lambda s, h: (row_blk0 + s, h)), so_spec],
        out_shape=[jax.ShapeDtypeStruct((p_main.shape[0], BR_WIDTH), F32), so_shape],
        scratch_shapes=[pltpu.VMEM((t, HEAD_D), F32)],
        compiler_params=_cp(("parallel", "parallel")),
        name="hgrn2",
    )


def _dot_b(a, b):
    return _dot(a.astype(BF16), b.astype(BF16))


def _tri_solve(ms, xs, c):
    ri = lax.broadcasted_iota(jnp.int32, (c, c), 0)
    ci = lax.broadcasted_iota(jnp.int32, (c, c), 1)
    eye = jnp.where(ri == ci, 1.0, 0.0)
    m8 = [jnp.where((ri >> 3) == (ci >> 3), m, 0.0) for m in ms]
    ts = [eye - m for m in m8]
    ps = [_dot_b(m, m) for m in m8]
    ts = [t + _dot_b(t, p) for t, p in zip(ts, ps)]
    ps = [_dot_b(p, p) for p in ps]
    ts = [t + _dot_b(t, p) for t, p in zip(ts, ps)]
    shift = 3
    while (1 << shift) < c:
        join = ((ri >> (shift + 1)) == (ci >> (shift + 1))) & ((ri >> shift) != (ci >> shift))
        lts = [_dot_b(jnp.where(join, m, 0.0), t) for m, t in zip(ms, ts)]
        ts = [t - _dot_b(t, lt) for t, lt in zip(ts, lts)]
        shift += 1
    return [_dot_b(t, x) for t, x in zip(ts, xs)]


def _gdn_kernel(cq_ref, ck_ref, cv_ref, cg_ref, ab_ref, cw_ref, alog_ref, dtb_ref, g_ref, s0_ref,
                y_ref, s_ref, q_s, k_s, v_s, la_s, be_s, w_s, qk_s, qe_s, kd_s, dec_s, o_acc,
                *, t, c, span, hg):
    nchunk = t // c
    cps = span // c
    rowt = lax.broadcasted_iota(jnp.int32, (t, HEAD_D), 0)
    lane = lax.broadcasted_iota(jnp.int32, (t, LANES), 1)
    ri = lax.broadcasted_iota(jnp.int32, (span, span), 0)
    ci = lax.broadcasted_iota(jnp.int32, (span, span), 1)
    shift = int(math.log2(c))
    same = (ri >> shift) == (ci >> shift)
    heads = range(hg)
    dirs = (0, 1)

    def conv(x_ref, j, hh):
        x = x_ref[:, hh * HEAD_D:(hh + 1) * HEAD_D]
        w = cw_ref[j][:, hh * HEAD_D:(hh + 1) * HEAD_D]
        xm = jnp.where(rowt == 0, 0.0, pltpu.roll(x, 1, 0))
        xp = jnp.where(rowt == t - 1, 0.0, pltpu.roll(x, t - 1, 0))
        return xm * w[0:1] + x * w[1:2] + xp * w[2:3]

    ab = ab_ref[...]
    for hh in heads:
        h = pl.program_id(1) * hg + hh
        q = _silu(conv(cq_ref, 0, hh))
        q_s[hh] = q * lax.rsqrt(jnp.sum(q * q, axis=-1, keepdims=True) + EPS) * (HEAD_D ** -0.5)
        k = _silu(conv(ck_ref, 1, hh))
        k_s[hh] = k * lax.rsqrt(jnp.sum(k * k, axis=-1, keepdims=True) + EPS)
        v_s[hh] = _silu(conv(cv_ref, 2, hh))
        for d in dirs:
            a_col = jnp.sum(jnp.where(lane == d * N_HEADS + h, ab, 0.0), axis=-1, keepdims=True)
            b_col = jnp.sum(jnp.where(lane == 2 * N_HEADS + d * N_HEADS + h, ab, 0.0), axis=-1, keepdims=True)
            coef = -jnp.exp(jnp.full((1, LANES), alog_ref[d, h], F32))
            xs = a_col + dtb_ref[d, h]
            softplus = jnp.maximum(xs, 0.0) + jnp.log1p(jnp.exp(-jnp.abs(xs)))
            la_s[hh, d] = coef * jnp.broadcast_to(softplus, (t, LANES))
            be_s[hh, d] = jnp.broadcast_to(jax.nn.sigmoid(b_col), (t, LANES))
    o_acc[...] = jnp.zeros(o_acc.shape, F32)

    def wy_span(sp, carry):
        r0 = pl.multiple_of(sp * span, span)
        ms, xs, dst = [], [], []
        for hh in heads:
            qsp = q_s[hh, pl.ds(r0, span), :]
            ksp = k_s[hh, pl.ds(r0, span), :]
            vsp = v_s[hh, pl.ds(r0, span), :]
            kb = ksp.astype(BF16)
            kk = _dot_nt(kb, kb)
            qk = _dot_nt(qsp.astype(BF16), kb)
            for d in dirs:
                be = be_s[hh, d, pl.ds(r0, span), :]
                g = _chunk_cumsum(la_s[hh, d, pl.ds(r0, span), :], d, c)
                gt = g.T
                gcol = jnp.concatenate([g] * (span // LANES), axis=1)
                grow = jnp.concatenate([gt] * (span // LANES), axis=0)
                incl = same & ((ci <= ri) if d == 0 else (ci >= ri))
                gamma = jnp.where(incl, jnp.exp(jnp.minimum(gcol - grow, 0.0)), 0.0)
                bcol = jnp.concatenate([be] * (span // LANES), axis=1)
                m_sp = jnp.where(ci == ri, 0.0, bcol * kk * gamma)
                qkg = (qk * gamma).astype(BF16)
                eg = jnp.exp(g)
                xv = be * vsp
                xk = be * eg * ksp
                qe_s[hh, d, pl.ds(r0, span), :] = (qsp * eg).astype(BF16)
                for j in range(cps):
                    lo = j * c
                    rj = pl.multiple_of(r0 + lo, c)
                    ms.append(m_sp[lo:lo + c, lo:lo + c])
                    xs.append(jnp.concatenate([xv[lo:lo + c], xk[lo:lo + c]], axis=1))
                    dst.append((hh, d, rj))
                    qk_s[hh, d, pl.ds(rj, c), 0:c] = qkg[lo:lo + c, lo:lo + c]
                    gj = g[lo:lo + c]
                    glast = gj[c - 1:c] if d == 0 else gj[0:1]
                    kd_s[hh, d, pl.ds(rj, c), :] = (ksp[lo:lo + c] * jnp.exp(glast - gj)).astype(BF16)
                    dec_s[hh, d, pl.ds(sp * cps + j, 1), :] = jnp.exp(glast)
        for (hh, d, rj), w in zip(dst, _tri_solve(ms, xs, c)):
            w_s[hh, d, pl.ds(rj, c), :] = w
        return carry

    if t == span:
        wy_span(0, 0)
    else:
        lax.fori_loop(0, t // span, wy_span, 0)

    chains = [(hh, d) for hh in heads for d in dirs]

    def scan_step(i, carry):
        cidx = (i, nchunk - 1 - i)
        r0 = [pl.multiple_of(ck * c, c) for ck in cidx]
        sb = [s.astype(BF16) for s in carry]
        ws = [w_s[hh, d, pl.ds(r0[d], c), :] for hh, d in chains]
        wks = [_dot(w[:, HEAD_D:].astype(BF16), s) for w, s in zip(ws, sb)]
        oq = [_dot(qe_s[hh, d, pl.ds(r0[d], c), :], s) for (hh, d), s in zip(chains, sb)]
        vnb = [(w[:, :HEAD_D] - wk).astype(BF16) for w, wk in zip(ws, wks)]
        new = [s * dec_s[hh, d, pl.ds(cidx[d], 1), :] + _dot_tn(kd_s[hh, d, pl.ds(r0[d], c), :], v)
               for (hh, d), s, v in zip(chains, carry, vnb)]
        for (hh, d), o, v in zip(chains, oq, vnb):
            o_acc[hh, pl.ds(r0[d], c), :] += o + _dot(qk_s[hh, d, pl.ds(r0[d], c), 0:c], v)
        return tuple(new)

    finals = lax.fori_loop(0, nchunk, scan_step, tuple(s0_ref[0, d, hh] for hh, d in chains),
                           unroll=nchunk <= 4)
    for (hh, d), s in zip(chains, finals):
        s_ref[0, d, hh] = s
    for hh in heads:
        sl = slice(hh * HEAD_D, (hh + 1) * HEAD_D)
        y_ref[:, sl] = _rms(o_acc[hh], g_ref[...]) * _silu(cg_ref[:, sl])


def _gdn(p_main, p_ab, conv_w, a_log, dt_bias, norm_g, s0, row_blk0, nseq, t, y_prev=None, s_prev=None,
         slot=0, nslots=1):
    c = GDN_CHUNK
    hg = GDN_HEADS_PER_STEP if t * GDN_HEADS_PER_STEP <= GDN_MAX_ROWS_PER_STEP else GDN_HEADS_PER_STEP // 2
    wid = hg * HEAD_D
    blk = BR_WIDTH // wid

    def col(base):
        return pl.BlockSpec((t, wid), lambda s, h: (row_blk0 + s, base * blk + h))

    st_spec = pl.BlockSpec((1, 2, hg, HEAD_D, HEAD_D), lambda s, h: (s, 0, h, 0, 0))
    smem = pl.BlockSpec(memory_space=pltpu.SMEM)
    seq_buf = pltpu.VMEM((hg, t, HEAD_D), F32)
    dir_f32 = pltpu.VMEM((hg, 2, t, LANES), F32)
    dir_bf16 = pltpu.VMEM((hg, 2, t, LANES), BF16)
    scratch = [seq_buf, seq_buf, seq_buf, dir_f32, dir_f32,
               pltpu.VMEM((hg, 2, t, 2 * HEAD_D), F32), dir_bf16, dir_bf16, dir_bf16,
               pltpu.VMEM((hg, 2, max(t // c, 8), LANES), F32), seq_buf]
    so_spec, so_shape = _state_out(nseq, slot, nslots, hg)
    return _shared_out_call(
        functools.partial(_gdn_kernel, t=t, c=c, span=GDN_SPAN, hg=hg), [y_prev, s_prev],
        [col(6), col(7), col(8), col(9),
         pl.BlockSpec((t, LANES), lambda s, h: (row_blk0 + s, 0)),
         pl.BlockSpec((3, 3, wid), lambda s, h: (0, 0, h)),
         smem, smem,
         pl.BlockSpec((1, HEAD_D), lambda s, h: (0, 0)),
         st_spec],
        (p_main, p_main, p_main, p_main, p_ab, conv_w, a_log, dt_bias, norm_g.reshape(1, HEAD_D), s0),
        grid=(nseq, N_HEADS // hg),
        out_specs=[pl.BlockSpec((t, wid), lambda s, h: (row_blk0 + s, h)), so_spec],
        out_shape=[jax.ShapeDtypeStruct((p_main.shape[0], BR_WIDTH), F32), so_shape],
        scratch_shapes=scratch,
        compiler_params=_cp(("parallel", "parallel")),
        name="gdn",
    )


def _cmul(ar, ai, br, bi):
    return ar * br - ai * bi, ar * bi + ai * br


def _s5_kernel(u_ref, bb_ref, cc_ref, lam_ref, dsk_ref, s0_ref, y_ref, sf_ref, up_s, x_s, yp_s,
               *, tb, seg, segs_per_seq, zero_init):
    r_par = tb // seg
    half = S5_GB * S5_STATE
    for r in range(seg):
        up_s[r * r_par:(r + 1) * r_par, :] = u_ref[pl.ds(r, r_par, stride=seg), :]
    ub = up_s[...].astype(BF16)
    sf_ref[...] = jnp.zeros(sf_ref.shape, F32)
    nseq = r_par // segs_per_seq
    lam = [(lam_ref[0, d, 0:1, :], lam_ref[0, d, 1:2, :]) for d in (0, 1)]
    for d in (0, 1):
        x_s[d] = _dot(ub, bb_ref[0, d])

    def offsets(i):
        return [pl.multiple_of(r * r_par, r_par) for r in (i, seg - 1 - i)]

    def local_step(i, carry):
        off = offsets(i)
        new = []
        for d in (0, 1):
            (lre, lim), (hre, him) = lam[d], carry[d]
            nre = lre * hre - lim * him + x_s[d, pl.ds(off[d], r_par), 0:half]
            nim = lre * him + lim * hre + x_s[d, pl.ds(off[d], r_par), half:2 * half]
            x_s[d, pl.ds(off[d], r_par), 0:half] = nre
            x_s[d, pl.ds(off[d], r_par), half:2 * half] = nim
            new.append((nre, nim))
        return tuple(new)

    zero = jnp.zeros((r_par, half), F32)
    lax.fori_loop(0, seg, local_step, ((zero, zero), (zero, zero)))

    end_rows = ((seg - 1) * r_par, 0)
    needs_carry = not (zero_init and segs_per_seq == 1)
    hins = []
    for d in (0, 1):
        if not needs_carry:
            break
        lre, lim = lam[d]
        pre, pim = lre, lim
        n = 1
        while n < seg:
            pre, pim = _cmul(pre, pim, pre, pim)
            n *= 2
        end_row = end_rows[d]
        ere = x_s[d, end_row:end_row + r_par, 0:half]
        eim = x_s[d, end_row:end_row + r_par, half:2 * half]
        hin = [None] * r_par
        order = range(r_par) if d == 0 else range(r_par - 1, -1, -1)
        for j in order:
            sq, pos = divmod(j, segs_per_seq)
            first = pos == (0 if d == 0 else segs_per_seq - 1)
            if first:
                hin[j] = (s0_ref[0, 0, d, sq:sq + 1, 0:half], s0_ref[0, 0, d, sq:sq + 1, half:2 * half])
            else:
                pj = j - 1 if d == 0 else j + 1
                cre, cim = _cmul(pre, pim, hin[pj][0], hin[pj][1])
                hin[j] = (cre + ere[pj:pj + 1], cim + eim[pj:pj + 1])
        hins.append((jnp.concatenate([hj[0] for hj in hin], axis=0),
                     jnp.concatenate([hj[1] for hj in hin], axis=0)))

    def carry_step(i, pws):
        off = offsets(i)
        new = []
        for d in (0, 1):
            (lre, lim), (wre, wim) = lam[d], pws[d]
            are, aim = _cmul(wre, wim, hins[d][0], hins[d][1])
            x_s[d, pl.ds(off[d], r_par), 0:half] += are
            x_s[d, pl.ds(off[d], r_par), half:2 * half] += aim
            new.append(_cmul(wre, wim, lre, lim))
        return tuple(new)

    if needs_carry:
        lax.fori_loop(0, seg, carry_step, (lam[0], lam[1]))

    for d in (0, 1):
        for sq in range(nseq):
            j = (sq + 1) * segs_per_seq - 1 if d == 0 else sq * segs_per_seq
            row = end_rows[d] + j
            sf_ref[0, 0, d, sq:sq + 1, :] = x_s[d, row:row + 1, :]
    yp_s[...] = (_dot(x_s[0].astype(BF16), cc_ref[0, 0]) + _dot(x_s[1].astype(BF16), cc_ref[0, 1])
                 + up_s[...] * dsk_ref[...])
    for r in range(seg):
        y_ref[pl.ds(r, r_par, stride=seg), :] = yp_s[r * r_par:(r + 1) * r_par, :]


def _s5(p_main, bb, cc, lam, dskip, s0, row_blk0, nblk, tb, t, seg, zero_init, y_prev=None):
    ngb = S5_GROUPS // S5_GB
    w2 = 2 * S5_GB * S5_STATE
    return _shared_out_call(
        functools.partial(_s5_kernel, tb=tb, seg=seg, segs_per_seq=t // seg, zero_init=zero_init), [y_prev, None],
        [
            pl.BlockSpec((tb, LANES), lambda i, gb: (row_blk0 + i, 20 + gb)),
            pl.BlockSpec((1, 2, LANES, w2), lambda i, gb: (gb, 0, 0, 0)),
            pl.BlockSpec((1, 2, w2, LANES), lambda i, gb: (gb, 0, 0, 0)),
            pl.BlockSpec((1, 2, 2, w2 // 2), lambda i, gb: (gb, 0, 0, 0)),
            pl.BlockSpec((1, LANES), lambda i, gb: (0, gb)),
            pl.BlockSpec((1, 1, 2, 8, w2), lambda i, gb: (i, gb, 0, 0, 0)),
        ],
        (p_main, bb, cc, lam, dskip.reshape(1, BR_WIDTH), s0),
        grid=(nblk, ngb),
        out_specs=[pl.BlockSpec((tb, LANES), lambda i, gb: (row_blk0 + i, gb)),
                   pl.BlockSpec((1, 1, 2, 8, w2), lambda i, gb: (i, gb, 0, 0, 0))],
        out_shape=[jax.ShapeDtypeStruct((p_main.shape[0], BR_WIDTH), F32),
                   jax.ShapeDtypeStruct((nblk, ngb, 2, 8, w2), F32)],
        scratch_shapes=[pltpu.VMEM((tb, LANES), F32), pltpu.VMEM((2, tb, w2), F32), pltpu.VMEM((tb, LANES), F32)],
        compiler_params=_cp(("parallel", "parallel")),
        name="s5",
    )


def _s5_params(a_re, a_im, log_dt, b_re, b_im, c_re, c_im):
    dt = jnp.exp(log_dt)[..., None]
    mag = jnp.exp(dt * a_re)
    lam_re = mag * jnp.cos(dt * a_im)
    lam_im = mag * jnp.sin(dt * a_im)
    den = a_re * a_re + a_im * a_im
    co_re = ((lam_re - 1.0) * a_re + lam_im * a_im) / den
    co_im = (lam_im * a_re - (lam_re - 1.0) * a_im) / den
    bb_re = co_re[..., None] * b_re - co_im[..., None] * b_im
    bb_im = co_re[..., None] * b_im + co_im[..., None] * b_re
    ngb = S5_GROUPS // S5_GB
    eye = jnp.eye(S5_GB, dtype=F32)

    def pack_in(w):
        w = w.reshape(2, ngb, S5_GB, S5_STATE, S5_GROUP)
        blk = jnp.einsum('dbgpi,gh->bdgihp', w, eye)
        return blk.reshape(ngb, 2, S5_GB * S5_GROUP, S5_GB * S5_STATE)

    def pack_out(w):
        w = w.reshape(2, ngb, S5_GB, S5_GROUP, S5_STATE)
        blk = jnp.einsum('dbgip,gh->bdgphi', w, eye)
        return blk.reshape(ngb, 2, S5_GB * S5_STATE, S5_GB * S5_GROUP)

    bb = jnp.concatenate([pack_in(bb_re), pack_in(bb_im)], axis=-1).astype(BF16)
    cc = jnp.concatenate([pack_out(c_re), -pack_out(c_im)], axis=-2).astype(BF16)
    lam = jnp.stack([lam_re, lam_im], axis=1)
    lam = lam.reshape(2, 2, ngb, S5_GB * S5_STATE).transpose(2, 0, 1, 3)
    return bb, cc, lam


def _glu_kernel(y_ref, w_ref, b_ref, o_ref, wb_ref):
    @pl.when(pl.program_id(0) == 0)
    def _():
        wb_ref[...] = w_ref[...].astype(BF16)

    z = _dot(y_ref[...].astype(BF16), wb_ref[...]) + b_ref[...]
    half = z.shape[1] // 2
    o_ref[...] = z[:, :half] * jax.nn.sigmoid(z[:, half:])


def _glu(y, w, b, l):
    n, k = y.shape
    n2 = w.shape[-1]
    return pl.pallas_call(
        _glu_kernel,
        grid=(n // TM,),
        in_specs=[pl.BlockSpec((TM, k), lambda i: (i, 0)),
                  pl.BlockSpec((None, k, n2), lambda i: (l, 0, 0)),
                  pl.BlockSpec((None, 1, n2), lambda i: (l, 0, 0))],
        out_specs=pl.BlockSpec((TM, n2 // 2), lambda i: (i, 0)),
        out_shape=jax.ShapeDtypeStruct((n, n2 // 2), F32),
        scratch_shapes=[pltpu.VMEM((k, n2), BF16)],
        compiler_params=_cp(("arbitrary",)),
        name="s5_glu",
    )(y, w, b.reshape(b.shape[0], 1, n2))


def _merge_kernel(ya_ref, yb_ref, yc_ref, mg_ref, wbr_ref, wout_ref, g_ref, o_ref, wbr_s, wout_s):
    @pl.when(pl.program_id(0) == 0)
    def _():
        wbr_s[...] = wbr_ref[...].astype(BF16)
        wout_s[...] = wout_ref[...].astype(BF16)

    d = wout_s.shape[0]
    acc = None
    for n, br_ref in enumerate((ya_ref, yb_ref, yc_ref)):
        proj = _dot(br_ref[...].astype(BF16), wbr_s[n])
        term = proj * jax.nn.sigmoid(mg_ref[:, n * d:(n + 1) * d])
        acc = term if acc is None else acc + term
    y = _dot(acc.astype(BF16), wout_s[...])
    o_ref[...] = _rms(y, g_ref[...])


def _merge(ya, yb, yc, mg, w_branch, w_out, g, l):
    n = ya.shape[0]
    d = w_out.shape[-1]
    tm = 256
    br = pl.BlockSpec((tm, BR_WIDTH), lambda i: (i, 0))
    return pl.pallas_call(
        _merge_kernel,
        grid=(n // tm,),
        in_specs=[br, br, br,
                  pl.BlockSpec((tm, 3 * d), lambda i: (i, 0)),
                  pl.BlockSpec((None, 3, BR_WIDTH, d), lambda i: (l, 0, 0, 0)),
                  pl.BlockSpec((None, d, d), lambda i: (l, 0, 0)),
                  pl.BlockSpec((1, d), lambda i: (0, 0))],
        out_specs=pl.BlockSpec((tm, d), lambda i: (i, 0)),
        out_shape=jax.ShapeDtypeStruct((n, d), F32),
        scratch_shapes=[pltpu.VMEM((3, BR_WIDTH, d), BF16), pltpu.VMEM((d, d), BF16)],
        compiler_params=_cp(("arbitrary",)),
        name="merge",
    )(ya, yb, yc, mg, w_branch, w_out, g.reshape(1, d))


def _swiglu_kernel(*refs, gated):
    if gated:
        x_ref, wg_ref, wu_ref, gt_ref, o_ref, wg_s, wu_s = refs
    else:
        x_ref, wg_ref, wu_ref, o_ref, wg_s, wu_s = refs

    @pl.when(pl.program_id(1) == 0)
    def _():
        wg_s[...] = wg_ref[...].astype(BF16)
        wu_s[...] = wu_ref[...].astype(BF16)

    x = x_ref[...]
    hid = _silu(_dot(x, wg_s[...])) * _dot(x, wu_s[...])
    if gated:
        gt = gt_ref[...]
        lane = lax.broadcasted_iota(jnp.int32, gt.shape, 1)
        hid = hid * jnp.sum(jnp.where(lane == pl.program_id(0), gt, 0.0), axis=-1, keepdims=True)
    o_ref[...] = hid.astype(BF16)


def _swiglu_up(x, wg, wu, e, f, gates=None):
    n, d = x.shape
    gated = gates is not None
    if gated:
        ne = wg.shape[1]
        wspec = pl.BlockSpec((None, None, d, f), lambda j, i: (e, j, 0, 0))
    else:
        ne = wg.shape[-1] // f
        wspec = pl.BlockSpec((None, d, f), lambda j, i: (e, 0, j))
    in_specs = [pl.BlockSpec((TM, d), lambda j, i: (i, 0)), wspec, wspec]
    args = [x, wg, wu]
    if gated:
        in_specs.append(pl.BlockSpec((TM, LANES), lambda j, i: (i, 0)))
        args.append(gates)
    return pl.pallas_call(
        functools.partial(_swiglu_kernel, gated=gated),
        grid=(ne, n // TM),
        in_specs=in_specs,
        out_specs=pl.BlockSpec((TM, f), lambda j, i: (i, j)),
        out_shape=jax.ShapeDtypeStruct((n, ne * f), BF16),
        scratch_shapes=[pltpu.VMEM((d, f), BF16), pltpu.VMEM((d, f), BF16)],
        compiler_params=_cp(("arbitrary", "arbitrary")),
        name="swiglu_up",
    )(*args)


def _down_kernel(x_ref, w_ref, g_ref, o_ref, acc):
    k = pl.program_id(1)
    part = _dot(x_ref[...], w_ref[...].astype(BF16))

    @pl.when(k == 0)
    def _():
        acc[...] = part

    @pl.when(k > 0)
    def _():
        acc[...] += part

    @pl.when(k == pl.num_programs(1) - 1)
    def _():
        o_ref[...] = _rms(acc[...], g_ref[...])


def _down_norm(hid, wd, e, g, tm):
    n = hid.shape[0]
    _, ne, f, d = wd.shape
    return pl.pallas_call(
        _down_kernel,
        grid=(n // tm, ne),
        in_specs=[pl.BlockSpec((tm, f), lambda i, k: (i, k)),
                  pl.BlockSpec((None, None, f, d), lambda i, k: (e, k, 0, 0)),
                  pl.BlockSpec((1, d), lambda i, k: (0, 0))],
        out_specs=pl.BlockSpec((tm, d), lambda i, k: (i, 0)),
        out_shape=jax.ShapeDtypeStruct((n, d), F32),
        scratch_shapes=[pltpu.VMEM((tm, d), F32)],
        compiler_params=_cp(("parallel", "arbitrary")),
        name="down_norm",
    )(hid, wd, g.reshape(1, d))


def _router_kernel(x_ref, w_ref, aux_ref, cnt_ref, base_s, *, ne):
    @pl.when(pl.program_id(0) == 0)
    def _():
        base_s[...] = jnp.zeros(base_s.shape, F32)

    w = w_ref[...]
    w_hi = w.astype(BF16)
    w_lo = (w - w_hi.astype(F32)).astype(BF16)
    x = x_ref[...].astype(BF16)
    logits = _dot(x, w_hi) + _dot(x, w_lo)
    tm = logits.shape[0]
    lane = lax.broadcasted_iota(jnp.int32, logits.shape, 1).astype(F32)
    neg = jnp.float32(-jnp.inf)
    logits = jnp.where(lane < ne, logits, neg)
    m1 = jnp.max(logits, axis=-1, keepdims=True)
    i1 = jnp.min(jnp.where(logits == m1, lane, float(LANES)), axis=-1, keepdims=True)
    rest = jnp.where(lane == i1, neg, logits)
    m2 = jnp.max(rest, axis=-1, keepdims=True)
    i2 = jnp.min(jnp.where(rest == m2, lane, float(LANES)), axis=-1, keepdims=True)
    e2 = jnp.exp(m2 - m1)
    w1 = 1.0 / (1.0 + e2)
    w2 = e2 / (1.0 + e2)
    sel = jnp.where((lane == i1) | (lane == i2), 1.0, 0.0)
    ri = lax.broadcasted_iota(jnp.int32, (tm, tm), 0)
    ci = lax.broadcasted_iota(jnp.int32, (tm, tm), 1)
    before = jnp.where(ci < ri, 1.0, 0.0).astype(BF16)
    rank = _dot(before, sel.astype(BF16)) + base_s[0:1, :]
    r1 = jnp.sum(jnp.where(lane == i1, rank, 0.0), axis=-1, keepdims=True)
    r2 = jnp.sum(jnp.where(lane == i2, rank, 0.0), axis=-1, keepdims=True)
    aux = jnp.zeros(logits.shape, F32)
    for k, v in enumerate((i1, i2, r1, r2, w1, w2)):
        aux = jnp.where(lane == float(k), v, aux)
    aux_ref[...] = aux
    base_s[...] = base_s[...] + jnp.sum(sel, axis=0, keepdims=True)
    cnt_ref[...] = base_s[...]


def _router(x, w_pad, ne):
    n, d = x.shape
    return pl.pallas_call(
        functools.partial(_router_kernel, ne=ne),
        grid=(n // TM,),
        in_specs=[pl.BlockSpec((TM, d), lambda i: (i, 0)),
                  pl.BlockSpec((d, LANES), lambda i: (0, 0))],
        out_specs=[pl.BlockSpec((TM, LANES), lambda i: (i, 0)),
                   pl.BlockSpec((8, LANES), lambda i: (0, 0))],
        out_shape=[jax.ShapeDtypeStruct((n, LANES), F32), jax.ShapeDtypeStruct((8, LANES), F32)],
        scratch_shapes=[pltpu.VMEM((8, LANES), F32)],
        compiler_params=_cp(("arbitrary",)),
        name="router",
    )(x, w_pad)


def _moe_plan(aux, cnt, n, ne, tile, max_tiles):
    e = aux[:, 0:2].astype(jnp.int32)
    r = aux[:, 2:4].astype(jnp.int32)
    c = cnt[0, :ne].astype(jnp.int32)
    pc = (c + tile - 1) // tile * tile
    ends = jnp.cumsum(pc)
    slot = ((ends - pc)[e] + r).reshape(-1)
    tok = jnp.broadcast_to(jnp.arange(n, dtype=jnp.int32)[:, None], (n, 2)).reshape(-1)
    m = max_tiles * tile
    slot_tok = jnp.full((m,), n, jnp.int32).at[slot].set(tok)
    slot_gate = jnp.zeros((m,), F32).at[slot].set(aux[:, 4:6].reshape(-1))
    tstart = jnp.arange(max_tiles, dtype=jnp.int32) * tile
    tile_e = jnp.minimum(jnp.sum(tstart[:, None] >= ends[None, :], axis=1), ne - 1).astype(jnp.int32)
    tile_valid = (tstart < ends[-1]).astype(jnp.int32)
    return slot_tok.reshape(max_tiles, 1, tile), slot_gate.reshape(max_tiles, 1, tile), tile_e, tile_valid


def _expert_changed(t, te_ref):
    return jnp.logical_or(t == 0, te_ref[t] != te_ref[jnp.maximum(t - 1, 0)])


def _moe_up_kernel(te_ref, tv_ref, idx_ref, h_ref, wg_ref, wu_ref, o_ref, xbuf, wg_s, wu_s, *, tile):
    t = pl.program_id(0)
    valid = tv_ref[t] == 1

    @pl.when(jnp.logical_and(valid, _expert_changed(t, te_ref)))
    def _():
        wg_s[...] = wg_ref[...].astype(BF16)
        wu_s[...] = wu_ref[...].astype(BF16)

    @pl.when(valid)
    def _():
        last = h_ref.shape[0] - 1

        def gather(j, carry):
            xbuf[pl.ds(j, 1), :] = h_ref[pl.ds(jnp.minimum(idx_ref[0, 0, j], last), 1), :]
            return carry

        lax.fori_loop(0, tile, gather, 0, unroll=8)
        x = xbuf[...].astype(BF16)
        o_ref[...] = (_silu(_dot(x, wg_s[...])) * _dot(x, wu_s[...])).astype(BF16)

    @pl.when(jnp.logical_not(valid))
    def _():
        o_ref[...] = jnp.zeros(o_ref.shape, BF16)


def _moe_up(h, slot_tok, tile_e, tile_valid, wg, wu, e_layer, tile):
    max_tiles = slot_tok.shape[0]
    _, _, d, f = wg.shape
    wspec = pl.BlockSpec((None, None, d, f), lambda t, te, tv: (e_layer, te[t], 0, 0),
                         pipeline_mode=pl.Buffered(1))
    grid_spec = pltpu.PrefetchScalarGridSpec(
        num_scalar_prefetch=2,
        grid=(max_tiles,),
        in_specs=[pl.BlockSpec((1, 1, tile), lambda t, te, tv: (t, 0, 0), memory_space=pltpu.SMEM),
                  pl.BlockSpec(h.shape, lambda t, te, tv: (0, 0), pipeline_mode=pl.Buffered(1)),
                  wspec, wspec],
        out_specs=pl.BlockSpec((tile, f), lambda t, te, tv: (t, 0)),
        scratch_shapes=[pltpu.VMEM((tile, d), F32),
                        pltpu.VMEM((d, f), BF16), pltpu.VMEM((d, f), BF16)],
    )
    return pl.pallas_call(
        functools.partial(_moe_up_kernel, tile=tile),
        grid_spec=grid_spec,
        out_shape=jax.ShapeDtypeStruct((max_tiles * tile, f), BF16),
        compiler_params=_cp(("arbitrary",)),
        name="moe_up",
    )(tile_e, tile_valid, slot_tok, h, wg, wu)


def _moe_down_kernel(te_ref, tv_ref, idx_ref, gt_ref, hid_ref, wd_ref, g_ref, o_ref, acc, ybuf, tbuf, wd_s,
                     *, tile, n_tiles, tm_out):
    t = pl.program_id(0)
    tt = jnp.minimum(t, n_tiles - 1)

    @pl.when(t == 0)
    def _():
        acc[...] = jnp.zeros(acc.shape, F32)

    is_tile = jnp.logical_and(t < n_tiles, tv_ref[tt] == 1)

    @pl.when(jnp.logical_and(is_tile, _expert_changed(tt, te_ref)))
    def _():
        wd_s[...] = wd_ref[...].astype(BF16)

    @pl.when(is_tile)
    def _():
        ybuf[...] = _dot(hid_ref[...], wd_s[...])

        def fetch(j, carry):
            tbuf[pl.ds(j, 1), :] = acc[pl.ds(idx_ref[0, 0, j], 1), :]
            return carry

        def put(j, carry):
            acc[pl.ds(idx_ref[0, 0, j], 1), :] = tbuf[pl.ds(j, 1), :] + gt_ref[0, 0, j] * ybuf[pl.ds(j, 1), :]
            return carry

        lax.fori_loop(0, tile, fetch, 0, unroll=8)
        lax.fori_loop(0, tile, put, 0, unroll=8)

    @pl.when(t >= n_tiles)
    def _():
        r0 = pl.multiple_of((t - n_tiles) * tm_out, tm_out)
        o_ref[...] = _rms(acc[pl.ds(r0, tm_out), :], g_ref[...])


def _moe_down(hid, slot_tok, slot_gate, tile_e, tile_valid, wd, e_layer, g, n, tile):
    n_tiles = slot_tok.shape[0]
    _, _, f, d = wd.shape
    tm_out = TM
    n_flush = n // tm_out

    def tix(t):
        return jnp.minimum(t, n_tiles - 1)

    smem_tile = pl.BlockSpec((1, 1, tile), lambda t, te, tv: (tix(t), 0, 0), memory_space=pltpu.SMEM)
    grid_spec = pltpu.PrefetchScalarGridSpec(
        num_scalar_prefetch=2,
        grid=(n_tiles + n_flush,),
        in_specs=[smem_tile, smem_tile,
                  pl.BlockSpec((tile, f), lambda t, te, tv: (tix(t), 0)),
                  pl.BlockSpec((None, None, f, d), lambda t, te, tv: (e_layer, te[tix(t)], 0, 0)),
                  pl.BlockSpec((1, d), lambda t, te, tv: (0, 0))],
        out_specs=pl.BlockSpec((tm_out, d), lambda t, te, tv: (jnp.maximum(t - n_tiles, 0), 0)),
        scratch_shapes=[pltpu.VMEM((n + 8, d), F32), pltpu.VMEM((tile, d), F32), pltpu.VMEM((tile, d), F32),
                        pltpu.VMEM((f, d), BF16)],
    )
    return pl.pallas_call(
        functools.partial(_moe_down_kernel, tile=tile, n_tiles=n_tiles, tm_out=tm_out),
        grid_spec=grid_spec,
        out_shape=jax.ShapeDtypeStruct((n, d), F32),
        compiler_params=_cp(("arbitrary",)),
        name="moe_down",
    )(tile_e, tile_valid, slot_tok, slot_gate, hid, wd, g.reshape(1, d))


def _to_column_major(x, b, t):
    d = x.shape[-1]
    return x.reshape(b, t // GRID_W, GRID_W, d).transpose(0, 2, 1, 3).reshape(b * t, d)


def _to_row_major(x, b, t):
    d = x.shape[-1]
    return x.reshape(b, GRID_W, t // GRID_W, d).transpose(0, 2, 1, 3).reshape(b * t, d)


def kernel(x_prompt, x_sample, state_hgrn, state_s5, state_gdn, c, c_ctx, ada_w, ada_b, norm_g, w_in, hgrn_lb, hgrn_norm, s5_a_re, s5_a_im, s5_log_dt, s5_b_re, s5_b_im, s5_c_re, s5_c_im, s5_d, s5_glu_w, s5_glu_b, gdn_conv, gdn_a_log, gdn_dt_bias, gdn_norm, w_branch, w_out, ffn_wg, ffn_wu, ffn_wd, moe_router, moe_wg, moe_wu, moe_wd):
    bp, tp, d = x_prompt.shape
    bs, ts, _ = x_sample.shape
    depth = w_in.shape[0]
    n_p, n_s = bp * tp, bs * ts
    ne = moe_router.shape[-1]
    main_cols = 10 * BR_WIDTH
    ab_cols = 4 * N_HEADS

    c8 = jnp.zeros((8, d), F32).at[0].set(c_ctx).at[1:1 + bs].set(c)
    mod_all = _ada_mod(c8, ada_w, ada_b)

    lb_soft = jax.nn.softmax(hgrn_lb.astype(F32), axis=0)
    lb_all = jnp.cumsum(lb_soft, axis=0) - lb_soft[:1]

    mods = mod_all[:, :1 + bs].reshape(depth, 1 + bs, 6, d)
    zero_row = jnp.zeros((1 + bs, d), F32)

    def modv(gate, scale, shift, g):
        rows = [zero_row if sel is None else mods[sel[0], :, sel[1]] for sel in (gate, scale, shift)]
        rows.append(zero_row if g is None else jnp.broadcast_to(g, (1 + bs, d)))
        return jnp.stack(rows + [zero_row] * 4, axis=1)

    x = jnp.concatenate([x_prompt.reshape(n_p, d), x_sample.reshape(n_s, d)], axis=0)
    (h,) = _resnorm(x, None, modv(None, (0, 1), (0, 0), norm_g[0, 0]), n_p, ts, True)

    z_hgrn = jnp.zeros((bp, 2, N_HEADS, HEAD_D, HEAD_D), F32)
    ngb = S5_GROUPS // S5_GB
    w2 = 2 * S5_GB * S5_STATE
    st_s5 = []
    y_zero = jnp.zeros((n_p + n_s, BR_WIDTH), F32)
    new_hgrn = new_gdn = jnp.zeros((bp, depth, 2, N_HEADS, HEAD_D, HEAD_D), F32)
    w_in_t = jnp.swapaxes(w_in, 1, 2)
    s5_tb = 1024

    for l in range(depth):
        col_major = l % 2 == 1
        if col_major:
            h = jnp.concatenate([h[:n_p], _to_column_major(h[n_p:], bs, ts)], axis=0)
        p_main = _mm_t(h, w_in_t, l, main_cols, 2 * BR_WIDTH, "w_in_main", tm=1024)
        w_ab = jnp.pad(w_in_t[l, main_cols:main_cols + ab_cols], ((0, LANES - ab_cols), (0, 0)))
        p_ab = _mm_t(h, w_ab, None, LANES, LANES, "w_in_ab", tm=1024)
        p_mg = _mm_t(h, w_in_t[l, main_cols + ab_cols:], None, 3 * d, 2 * BR_WIDTH, "w_in_merge", tm=1024)

        ya, new_hgrn = _hgrn(p_main, lb_all[l], hgrn_norm[l], z_hgrn, 0, bp, tp, y_zero, new_hgrn, l, depth)
        ya, _ = _hgrn(p_main, lb_all[l], hgrn_norm[l], state_hgrn[:, l].astype(F32), n_p // ts, bs, ts, ya)

        bb, cc, lam = _s5_params(s5_a_re[l], s5_a_im[l], s5_log_dt[l], s5_b_re[l], s5_b_im[l],
                                 s5_c_re[l], s5_c_im[l])
        seq_per_blk = 8
        tb_p = seq_per_blk * tp
        nblk_p = n_p // tb_p
        yb, sb = _s5(p_main, bb, cc, lam, s5_d[l], jnp.zeros((nblk_p, ngb, 2, 8, w2), F32),
                     0, nblk_p, tb_p, tp, tp, True, y_zero)
        s0s = state_s5[:, l].astype(F32)
        s0s = s0s.reshape(bs, 2, ngb, S5_GB * S5_STATE, 2).transpose(0, 2, 1, 4, 3).reshape(bs, ngb, 2, 1, w2)
        s0s = jnp.pad(s0s, ((0, 0), (0, 0), (0, 0), (0, 7), (0, 0)))
        yb, _ = _s5(p_main, bb, cc, lam, s5_d[l], s0s, n_p // s5_tb, n_s // s5_tb, s5_tb, ts, S5_SEG, False, yb)
        yb = _glu(yb, s5_glu_w, s5_glu_b, l)
        sb = sb[:, :, :, :seq_per_blk].reshape(nblk_p, ngb, 2, seq_per_blk, 2, S5_GB, S5_STATE)
        sb = sb.transpose(0, 3, 2, 1, 5, 6, 4).reshape(bp, 2, S5_GROUPS, S5_STATE, 2)
        st_s5.append(sb)

        yc, new_gdn = _gdn(p_main, p_ab, gdn_conv[l], gdn_a_log[l], gdn_dt_bias[l], gdn_norm[l], z_hgrn, 0, bp, tp,
                           y_zero, new_gdn, l, depth)
        yc, _ = _gdn(p_main, p_ab, gdn_conv[l], gdn_a_log[l], gdn_dt_bias[l], gdn_norm[l],
                     state_gdn[:, l].astype(F32), n_p // ts, bs, ts, yc)

        r_m = _merge(ya, yb, yc, p_mg, w_branch, w_out, norm_g[l, 1], l)
        if col_major:
            r_m = jnp.concatenate([r_m[:n_p], _to_row_major(r_m[n_p:], bs, ts)], axis=0)
        x, h2 = _resnorm(x, r_m, modv((l, 2), (l, 4), (l, 3), norm_g[l, 2]), n_p, ts, True,
                         h_dtype=BF16 if l % 2 == 0 else F32)

        if l % 2 == 0:
            e = l // 2
            hid = _swiglu_up(h2, ffn_wg, ffn_wu, e, ffn_wg.shape[-1] // 2)
            r_f = _down_norm(hid, ffn_wd[:, None], e, norm_g[l, 3], TM)
        else:
            e = l // 2
            n_tok = n_p + n_s
            max_tiles = 2 * n_tok // MOE_TILE + ne
            aux, cnt = _router(h2, jnp.pad(moe_router[e], ((0, 0), (0, LANES - ne))), ne)
            slot_tok, slot_gate, tile_e, tile_valid = _moe_plan(aux, cnt, n_tok, ne, MOE_TILE, max_tiles)
            hid = _moe_up(h2, slot_tok, tile_e, tile_valid, moe_wg, moe_wu, e, MOE_TILE)
            r_f = _moe_down(hid, slot_tok, slot_gate, tile_e, tile_valid, moe_wd, e, norm_g[l, 3], n_tok, MOE_TILE)

        if l + 1 < depth:
            x, h = _resnorm(x, r_f, modv((l, 5), (l + 1, 1), (l + 1, 0), norm_g[l + 1, 0]), n_p, ts, True)
        else:
            mv = modv((l, 5), None, None, None)
            (y_prompt,) = _resnorm(x, r_f, mv, n_p, ts, False, rows=(0, n_p))
            (y_sample,) = _resnorm(x, r_f, mv, n_p, ts, False, rows=(n_p, n_s))

    y_prompt = y_prompt.reshape(bp, tp, d)
    y_sample = y_sample.reshape(bs, ts, d)
    new_s5 = jnp.stack(st_s5, axis=1).astype(x_prompt.dtype)
    return (y_prompt, y_sample, new_hgrn.astype(x_prompt.dtype), new_s5, new_gdn.astype(x_prompt.dtype))
```

```python
import functools
import math

import jax
import jax.numpy as jnp
from jax import lax
from jax.experimental import pallas as pl
from jax.experimental.pallas import tpu as pltpu

F32 = jnp.float32
BF16 = jnp.bfloat16

EPS = 1e-6
F_FLOOR = 1e-6
LANES = 128
GRID_W = 64
HEAD_D = 128
N_HEADS = 4
BR_WIDTH = 512
S5_GROUPS = 32
S5_GROUP = 16
S5_STATE = 64
S5_GB = 8
S5_SEG = 64
GLA_CHUNK = 256
GDN_CHUNK = 64
GDN_SPAN = 256
GDN_HEADS_PER_STEP = 4
GDN_MAX_ROWS_PER_STEP = 2048
TM = 512
MOE_TILE = 256
VMEM_LIMIT = 56 * 1024 * 1024


def _cp(sem, vmem=VMEM_LIMIT):
    return pltpu.CompilerParams(dimension_semantics=sem, vmem_limit_bytes=vmem)


def _dot(a, b):
    return jnp.dot(a, b, preferred_element_type=F32)


def _dot_nt(a, b):
    return lax.dot_general(a, b, (((1,), (1,)), ((), ())), preferred_element_type=F32)


def _dot_tn(a, b):
    return lax.dot_general(a, b, (((0,), (0,)), ((), ())), preferred_element_type=F32)


def _silu(x):
    return x * jax.nn.sigmoid(x)


def _rms(x, g):
    return x * lax.rsqrt(jnp.mean(x * x, axis=-1, keepdims=True) + EPS) * g


def _chunk_cumsum(x, d, c):
    n = x.shape[0]
    pos = lax.broadcasted_iota(jnp.int32, x.shape, 0) & (c - 1)
    s = 1
    while s < c:
        if d == 0:
            x = x + jnp.where(pos >= s, pltpu.roll(x, s, 0), 0.0)
        else:
            x = x + jnp.where(pos < c - s, pltpu.roll(x, n - s, 0), 0.0)
        s *= 2
    return x


def _ada_kernel(c_ref, w_ref, b_ref, o_ref):
    c = c_ref[...]
    a = _silu(c).astype(BF16)
    o_ref[...] = _dot(a, w_ref[...].astype(BF16)) + b_ref[...]


def _ada_mod(c8, ada_w, ada_b):
    depth, d, n6 = ada_w.shape
    tn = 512
    return pl.pallas_call(
        _ada_kernel,
        grid=(depth, n6 // tn),
        in_specs=[
            pl.BlockSpec((8, d), lambda l, j: (0, 0)),
            pl.BlockSpec((None, d, tn), lambda l, j: (l, 0, j)),
            pl.BlockSpec((None, 1, tn), lambda l, j: (l, 0, j)),
        ],
        out_specs=pl.BlockSpec((None, 8, tn), lambda l, j: (l, 0, j)),
        out_shape=jax.ShapeDtypeStruct((depth, 8, n6), F32),
        compiler_params=_cp(("parallel", "parallel")),
        name="ada_mod",
    )(c8, ada_w, ada_b.reshape(depth, 1, n6))


def _resnorm_kernel(*refs, has_y, has_h):
    it = iter(refs)
    x_ref = next(it)
    y_ref = next(it) if has_y else None
    mv_ref = next(it)
    xo_ref = next(it) if has_y else None
    h_ref = next(it) if has_h else None
    x = x_ref[...]
    mv = mv_ref[...]
    if has_y:
        x = x + mv[0:1] * y_ref[...]
        xo_ref[...] = x
    if has_h:
        h = _rms(x, mv[3:4]) * (1.0 + mv[1:2]) + mv[2:3]
        h_ref[...] = h.astype(h_ref.dtype)


def _resnorm(x, y, modv, n_prompt, seq_s, has_h, rows=None, h_dtype=BF16):
    d = x.shape[1]
    row0, n = (0, x.shape[0]) if rows is None else rows
    i0 = row0 // TM
    has_y = y is not None

    def group(i):
        r = (i + i0) * TM
        return jnp.where(r < n_prompt, 0, 1 + (r - n_prompt) // seq_s)

    tile = pl.BlockSpec((TM, d), lambda i: (i + i0, 0))
    in_specs = [tile] + ([tile] if has_y else []) + [pl.BlockSpec((None, 8, d), lambda i: (group(i), 0, 0))]
    out_tile = pl.BlockSpec((TM, d), lambda i: (i, 0))
    out_specs, out_shape = [], []
    if has_y:
        out_specs.append(out_tile)
        out_shape.append(jax.ShapeDtypeStruct((n, d), F32))
    if has_h:
        out_specs.append(out_tile)
        out_shape.append(jax.ShapeDtypeStruct((n, d), h_dtype))
    args = [x] + ([y] if has_y else []) + [modv]
    outs = pl.pallas_call(
        functools.partial(_resnorm_kernel, has_y=has_y, has_h=has_h),
        grid=(n // TM,),
        in_specs=in_specs,
        out_specs=out_specs,
        out_shape=out_shape,
        compiler_params=_cp(("parallel",)),
        name="resnorm",
    )(*args)
    return outs


def _mm_kernel(x_ref, w_ref, o_ref, wb_ref):
    @pl.when(pl.program_id(1) == 0)
    def _():
        wb_ref[...] = w_ref[...].astype(BF16)

    o_ref[...] = _dot_nt(x_ref[...], wb_ref[...])


def _mm_t(x, wt, l, nrows, tn, name, tm=TM):
    m, k = x.shape
    if l is None:
        wspec = pl.BlockSpec((tn, k), lambda j, i: (j, 0))
    else:
        wspec = pl.BlockSpec((None, tn, k), lambda j, i: (l, j, 0))
    return pl.pallas_call(
        _mm_kernel,
        grid=(nrows // tn, m // tm),
        in_specs=[pl.BlockSpec((tm, k), lambda j, i: (i, 0)), wspec],
        out_specs=pl.BlockSpec((tm, tn), lambda j, i: (i, j)),
        out_shape=jax.ShapeDtypeStruct((m, nrows), F32),
        scratch_shapes=[pltpu.VMEM((tn, k), BF16)],
        compiler_params=_cp(("arbitrary", "arbitrary")),
        name=name,
    )(x, wt)


def _gla_intra(q, k, b, d, c):
    rowi = lax.broadcasted_iota(jnp.int32, (c, LANES), 0)
    ri = lax.broadcasted_iota(jnp.int32, (c, c), 0)
    ci = lax.broadcasted_iota(jnp.int32, (c, c), 1)
    a = jnp.zeros((c, c), F32)
    size = c
    while size >= 8:
        half = size // 2
        nb = c // size
        b3 = b.reshape(nb, size, LANES)
        pos = rowi & (size - 1)
        if d == 0:
            ref = b3[:, half - 1:half, :]
            qsel = pos >= half
        else:
            ref = b3[:, half:half + 1, :]
            qsel = pos < half
        ref = jnp.broadcast_to(ref, (nb, size, LANES)).reshape(c, LANES)
        z = jnp.where(qsel, q, k) * jnp.exp(-jnp.abs(b - ref))
        qs = jnp.where(qsel, z, 0.0).astype(BF16)
        ks = jnp.where(qsel, 0.0, z).astype(BF16)
        p = _dot_nt(qs, ks)
        if nb > 1:
            shift = int(math.log2(size))
            p = jnp.where((ri >> shift) == (ci >> shift), p, 0.0)
        a = a + p
        size = half
    b3 = b.reshape(c // 8, 8, LANES)
    lo, hi = (0, 4) if d == 0 else (3, 7)
    ref_lo = jnp.broadcast_to(b3[:, lo:lo + 1, :], (c // 8, 8, LANES)).reshape(c, LANES)
    ref_hi = jnp.broadcast_to(b3[:, hi:hi + 1, :], (c // 8, 8, LANES)).reshape(c, LANES)
    ref = jnp.where((rowi & 7) >= 4, ref_hi, ref_lo)
    qs = (q * jnp.exp(jnp.minimum(b - ref, 0.0))).astype(BF16)
    ks = (k * jnp.exp(ref - b)).astype(BF16)
    p = _dot_nt(qs, ks)
    keep = ((ri >> 2) == (ci >> 2)) & ((ci <= ri) if d == 0 else (ci >= ri))
    return a + jnp.where(keep, p, 0.0)


def _load_column_major(dst_ref, src_ref, t, stage=None):
    rows = t // GRID_W
    for k in range(src_ref.shape[1] // LANES):
        lanes = slice(k * LANES, (k + 1) * LANES)
        if stage is None:
            strided = src_ref
        else:
            stage[...] = src_ref[:, lanes]
            strided = stage
        for col in range(GRID_W):
            dst_ref[col * rows:(col + 1) * rows, lanes] = strided[pl.ds(col, rows, stride=GRID_W), :]


def _store_row_major(dst_ref, y, t, stage=None):
    rows = t // GRID_W
    for k in range(y.shape[1] // LANES):
        lanes = slice(k * LANES, (k + 1) * LANES)
        strided = dst_ref if stage is None else stage
        for col in range(GRID_W):
            strided[pl.ds(col, rows, stride=GRID_W), :] = y[col * rows:(col + 1) * rows, lanes]
        if stage is not None:
            dst_ref[:, lanes] = stage[...]


def _hgrn_kernel(aq_ref, af_ref, ab_ref, ai_ref, ag_ref, lb_ref, g_ref, s0_ref, y_ref, s_ref, o_acc, *perm_bufs,
                 t, c):
    if perm_bufs:
        srcs = (aq_ref, af_ref, ab_ref, ai_ref, ag_ref)
        for dst, src in zip(perm_bufs, srcs):
            _load_column_major(dst, src, t)
        aq_ref, af_ref, ab_ref, ai_ref, ag_ref = perm_bufs
    nchunk = t // c
    scale = HEAD_D ** -0.5
    for d in (0, 1):
        z_ref = af_ref if d == 0 else ab_ref
        lbd = lb_ref[d:d + 1, :]

        def chunk(i, st, d=d, z_ref=z_ref, lbd=lbd):
            cidx = i if d == 0 else nchunk - 1 - i
            r0 = pl.multiple_of(cidx * c, c)
            z = z_ref[pl.ds(r0, c), :]
            f = lbd + (1.0 - lbd) * jax.nn.sigmoid(z)
            lf = jnp.log(jnp.maximum(f, F_FLOOR))
            k = (1.0 - lbd) * jax.nn.sigmoid(-z)
            q = _silu(aq_ref[pl.ds(r0, c), :]) * scale
            v = ai_ref[pl.ds(r0, c), :].astype(BF16)
            b = _chunk_cumsum(lf, d, c)
            a = _gla_intra(q, k, b, d, c)
            o = _dot(a.astype(BF16), v) + _dot_nt((q * jnp.exp(b)).astype(BF16), st.astype(BF16))
            if d == 0:
                o_acc[pl.ds(r0, c), :] = o
            else:
                o_acc[pl.ds(r0, c), :] += o
            blast = b[c - 1:c, :] if d == 0 else b[0:1, :]
            kd = (k * jnp.exp(blast - b)).astype(BF16)
            return st * jnp.exp(blast) + _dot_tn(v, kd)

        st = s0_ref[0, d, 0].T
        if nchunk == 1:
            st = chunk(0, st)
        else:
            st = lax.fori_loop(0, nchunk, chunk, st)
        s_ref[0, d, 0] = st.T
    y = _rms(o_acc[...], g_ref[...]) * _silu(ag_ref[...])
    if perm_bufs:
        _store_row_major(y_ref, y, t)
    else:
        y_ref[...] = y


def _shared_out_call(kernel_fn, prevs, in_specs, args, **kw):
    extra = [p for p in prevs if p is not None]
    if not extra:
        return pl.pallas_call(kernel_fn, in_specs=in_specs, **kw)(*args)
    n_in = len(in_specs)
    aliases, k = {}, 0
    for out_idx, p in enumerate(prevs):
        if p is not None:
            aliases[n_in + k] = out_idx
            k += 1

    def body(*refs):
        kernel_fn(*refs[:n_in], *refs[n_in + len(extra):])

    return pl.pallas_call(body, in_specs=in_specs + [pl.BlockSpec(memory_space=pl.ANY)] * len(extra),
                          input_output_aliases=aliases, **kw)(*args, *extra)


def _state_out(nseq, slot, nslots, hg):
    spec = pl.BlockSpec((1, None, 2, hg, HEAD_D, HEAD_D), lambda s, h: (s, slot, 0, h, 0, 0))
    return spec, jax.ShapeDtypeStruct((nseq, nslots, 2, N_HEADS, HEAD_D, HEAD_D), F32)


def _hgrn(p_main, lb, norm_g, s0, row_blk0, nseq, t, y_prev=None, s_prev=None, slot=0, nslots=1,
          col_major=False):
    c = min(GLA_CHUNK, t)
    seq_buf = pltpu.VMEM((t, HEAD_D), F32)

    def col(base):
        return pl.BlockSpec((t, HEAD_D), lambda s, h: (row_blk0 + s, base + h))

    st_spec = pl.BlockSpec((1, 2, 1, HEAD_D, HEAD_D), lambda s, h: (s, 0, h, 0, 0))
    so_spec, so_shape = _state_out(nseq, slot, nslots, 1)
    return _shared_out_call(
        functools.partial(_hgrn_kernel, t=t, c=c), [y_prev, s_prev],
        [col(0), col(4), col(8), col(12), col(16),
         pl.BlockSpec((2, HEAD_D), lambda s, h: (0, h)),
         pl.BlockSpec((1, HEAD_D), lambda s, h: (0, 0)),
         st_spec],
        (p_main, p_main, p_main, p_main, p_main, lb, norm_g.reshape(1, HEAD_D), s0),
        grid=(nseq, N_HEADS),
        out_specs=[pl.BlockSpec((t, HEAD_D), lambda s, h: (row_blk0 + s, h)), so_spec],
        out_shape=[jax.ShapeDtypeStruct((p_main.shape[0], BR_WIDTH), F32), so_shape],
        scratch_shapes=[seq_buf] * (6 if col_major else 1),
        compiler_params=_cp(("parallel", "parallel")),
        name="hgrn2",
    )


def _dot_b(a, b):
    return _dot(a.astype(BF16), b.astype(BF16))


def _tri_solve(ms, xs, c):
    ri = lax.broadcasted_iota(jnp.int32, (c, c), 0)
    ci = lax.broadcasted_iota(jnp.int32, (c, c), 1)
    eye = jnp.where(ri == ci, 1.0, 0.0)
    m8 = [jnp.where((ri >> 3) == (ci >> 3), m, 0.0) for m in ms]
    ts = [eye - m for m in m8]
    ps = [_dot_b(m, m) for m in m8]
    ts = [t + _dot_b(t, p) for t, p in zip(ts, ps)]
    ps = [_dot_b(p, p) for p in ps]
    ts = [t + _dot_b(t, p) for t, p in zip(ts, ps)]
    shift = 3
    while (1 << shift) < c:
        join = ((ri >> (shift + 1)) == (ci >> (shift + 1))) & ((ri >> shift) != (ci >> shift))
        lts = [_dot_b(jnp.where(join, m, 0.0), t) for m, t in zip(ms, ts)]
        ts = [t - _dot_b(t, lt) for t, lt in zip(ts, lts)]
        shift += 1
    return [_dot_b(t, x) for t, x in zip(ts, xs)]


def _gdn_kernel(cq_ref, ck_ref, cv_ref, cg_ref, ab_ref, cw_ref, alog_ref, dtb_ref, g_ref, s0_ref,
                y_ref, s_ref, q_s, k_s, v_s, la_s, be_s, w_s, qk_s, qe_s, kd_s, dec_s, o_acc, *perm_bufs,
                t, c, span, hg):
    stage = None
    if perm_bufs:
        *perm_bufs, stage = perm_bufs
        srcs = (cq_ref, ck_ref, cv_ref, cg_ref, ab_ref)
        for dst, src in zip(perm_bufs, srcs):
            _load_column_major(dst, src, t, stage)
        cq_ref, ck_ref, cv_ref, cg_ref, ab_ref = perm_bufs
    nchunk = t // c
    cps = span // c
    rowt = lax.broadcasted_iota(jnp.int32, (t, HEAD_D), 0)
    lane = lax.broadcasted_iota(jnp.int32, (t, LANES), 1)
    ri = lax.broadcasted_iota(jnp.int32, (span, span), 0)
    ci = lax.broadcasted_iota(jnp.int32, (span, span), 1)
    shift = int(math.log2(c))
    same = (ri >> shift) == (ci >> shift)
    heads = range(hg)
    dirs = (0, 1)

    def conv(x_ref, j, hh):
        x = x_ref[:, hh * HEAD_D:(hh + 1) * HEAD_D]
        w = cw_ref[j][:, hh * HEAD_D:(hh + 1) * HEAD_D]
        xm = jnp.where(rowt == 0, 0.0, pltpu.roll(x, 1, 0))
        xp = jnp.where(rowt == t - 1, 0.0, pltpu.roll(x, t - 1, 0))
        return xm * w[0:1] + x * w[1:2] + xp * w[2:3]

    ab = ab_ref[...]
    for hh in heads:
        h = pl.program_id(1) * hg + hh
        q = _silu(conv(cq_ref, 0, hh))
        q_s[hh] = q * lax.rsqrt(jnp.sum(q * q, axis=-1, keepdims=True) + EPS) * (HEAD_D ** -0.5)
        k = _silu(conv(ck_ref, 1, hh))
        k_s[hh] = k * lax.rsqrt(jnp.sum(k * k, axis=-1, keepdims=True) + EPS)
        v_s[hh] = _silu(conv(cv_ref, 2, hh))
        for d in dirs:
            a_col = jnp.sum(jnp.where(lane == d * N_HEADS + h, ab, 0.0), axis=-1, keepdims=True)
            b_col = jnp.sum(jnp.where(lane == 2 * N_HEADS + d * N_HEADS + h, ab, 0.0), axis=-1, keepdims=True)
            coef = -jnp.exp(jnp.full((1, LANES), alog_ref[d, h], F32))
            xs = a_col + dtb_ref[d, h]
            softplus = jnp.maximum(xs, 0.0) + jnp.log1p(jnp.exp(-jnp.abs(xs)))
            la_s[hh, d] = coef * jnp.broadcast_to(softplus, (t, LANES))
            be_s[hh, d] = jnp.broadcast_to(jax.nn.sigmoid(b_col), (t, LANES))
    o_acc[...] = jnp.zeros(o_acc.shape, F32)

    def wy_span(sp, carry):
        r0 = pl.multiple_of(sp * span, span)
        ms, xs, dst = [], [], []
        for hh in heads:
            qsp = q_s[hh, pl.ds(r0, span), :]
            ksp = k_s[hh, pl.ds(r0, span), :]
            vsp = v_s[hh, pl.ds(r0, span), :]
            kb = ksp.astype(BF16)
            kk = _dot_nt(kb, kb)
            qk = _dot_nt(qsp.astype(BF16), kb)
            for d in dirs:
                be = be_s[hh, d, pl.ds(r0, span), :]
                g = _chunk_cumsum(la_s[hh, d, pl.ds(r0, span), :], d, c)
                gt = g.T
                gcol = jnp.concatenate([g] * (span // LANES), axis=1)
                grow = jnp.concatenate([gt] * (span // LANES), axis=0)
                incl = same & ((ci <= ri) if d == 0 else (ci >= ri))
                gamma = jnp.where(incl, jnp.exp(jnp.minimum(gcol - grow, 0.0)), 0.0)
                bcol = jnp.concatenate([be] * (span // LANES), axis=1)
                m_sp = jnp.where(ci == ri, 0.0, bcol * kk * gamma)
                qkg = (qk * gamma).astype(BF16)
                eg = jnp.exp(g)
                xv = be * vsp
                xk = be * eg * ksp
                qe_s[hh, d, pl.ds(r0, span), :] = (qsp * eg).astype(BF16)
                for j in range(cps):
                    lo = j * c
                    rj = pl.multiple_of(r0 + lo, c)
                    ms.append(m_sp[lo:lo + c, lo:lo + c])
                    xs.append(jnp.concatenate([xv[lo:lo + c], xk[lo:lo + c]], axis=1))
                    dst.append((hh, d, rj))
                    qk_s[hh, d, pl.ds(rj, c), 0:c] = qkg[lo:lo + c, lo:lo + c]
                    gj = g[lo:lo + c]
                    glast = gj[c - 1:c] if d == 0 else gj[0:1]
                    kd_s[hh, d, pl.ds(rj, c), :] = (ksp[lo:lo + c] * jnp.exp(glast - gj)).astype(BF16)
                    dec_s[hh, d, pl.ds(sp * cps + j, 1), :] = jnp.exp(glast)
        for (hh, d, rj), w in zip(dst, _tri_solve(ms, xs, c)):
            w_s[hh, d, pl.ds(rj, c), :] = w
        return carry

    if t == span:
        wy_span(0, 0)
    else:
        lax.fori_loop(0, t // span, wy_span, 0)

    chains = [(hh, d) for hh in heads for d in dirs]

    def scan_step(i, carry):
        cidx = (i, nchunk - 1 - i)
        r0 = [pl.multiple_of(ck * c, c) for ck in cidx]
        sb = [s.astype(BF16) for s in carry]
        ws = [w_s[hh, d, pl.ds(r0[d], c), :] for hh, d in chains]
        wks = [_dot(w[:, HEAD_D:].astype(BF16), s) for w, s in zip(ws, sb)]
        oq = [_dot(qe_s[hh, d, pl.ds(r0[d], c), :], s) for (hh, d), s in zip(chains, sb)]
        vnb = [(w[:, :HEAD_D] - wk).astype(BF16) for w, wk in zip(ws, wks)]
        new = [s * dec_s[hh, d, pl.ds(cidx[d], 1), :] + _dot_tn(kd_s[hh, d, pl.ds(r0[d], c), :], v)
               for (hh, d), s, v in zip(chains, carry, vnb)]
        for (hh, d), o, v in zip(chains, oq, vnb):
            o_acc[hh, pl.ds(r0[d], c), :] += o + _dot(qk_s[hh, d, pl.ds(r0[d], c), 0:c], v)
        return tuple(new)

    finals = lax.fori_loop(0, nchunk, scan_step, tuple(s0_ref[0, d, hh] for hh, d in chains),
                           unroll=nchunk <= 4)
    for (hh, d), s in zip(chains, finals):
        s_ref[0, d, hh] = s
    ys = []
    for hh in heads:
        sl = slice(hh * HEAD_D, (hh + 1) * HEAD_D)
        ys.append(_rms(o_acc[hh], g_ref[...]) * _silu(cg_ref[:, sl]))
    y = jnp.concatenate(ys, axis=1)
    if perm_bufs:
        _store_row_major(y_ref, y, t, stage)
    else:
        y_ref[...] = y


def _gdn(p_main, p_ab, conv_w, a_log, dt_bias, norm_g, s0, row_blk0, nseq, t, y_prev=None, s_prev=None,
         slot=0, nslots=1, col_major=False):
    c = GDN_CHUNK
    hg = GDN_HEADS_PER_STEP if t * GDN_HEADS_PER_STEP <= GDN_MAX_ROWS_PER_STEP else GDN_HEADS_PER_STEP // 2
    wid = hg * HEAD_D
    blk = BR_WIDTH // wid

    def col(base):
        return pl.BlockSpec((t, wid), lambda s, h: (row_blk0 + s, base * blk + h))

    st_spec = pl.BlockSpec((1, 2, hg, HEAD_D, HEAD_D), lambda s, h: (s, 0, h, 0, 0))
    smem = pl.BlockSpec(memory_space=pltpu.SMEM)
    seq_buf = pltpu.VMEM((hg, t, HEAD_D), F32)
    dir_f32 = pltpu.VMEM((hg, 2, t, LANES), F32)
    dir_bf16 = pltpu.VMEM((hg, 2, t, LANES), BF16)
    scratch = [seq_buf, seq_buf, seq_buf, dir_f32, dir_f32,
               pltpu.VMEM((hg, 2, t, 2 * HEAD_D), F32), dir_bf16, dir_bf16, dir_bf16,
               pltpu.VMEM((hg, 2, max(t // c, 8), LANES), F32), seq_buf]
    if col_major:
        scratch += [pltpu.VMEM((t, wid), F32)] * 4 + [pltpu.VMEM((t, LANES), F32)] * 2
    so_spec, so_shape = _state_out(nseq, slot, nslots, hg)
    return _shared_out_call(
        functools.partial(_gdn_kernel, t=t, c=c, span=GDN_SPAN, hg=hg), [y_prev, s_prev],
        [col(6), col(7), col(8), col(9),
         pl.BlockSpec((t, LANES), lambda s, h: (row_blk0 + s, 0)),
         pl.BlockSpec((3, 3, wid), lambda s, h: (0, 0, h)),
         smem, smem,
         pl.BlockSpec((1, HEAD_D), lambda s, h: (0, 0)),
         st_spec],
        (p_main, p_main, p_main, p_main, p_ab, conv_w, a_log, dt_bias, norm_g.reshape(1, HEAD_D), s0),
        grid=(nseq, N_HEADS // hg),
        out_specs=[pl.BlockSpec((t, wid), lambda s, h: (row_blk0 + s, h)), so_spec],
        out_shape=[jax.ShapeDtypeStruct((p_main.shape[0], BR_WIDTH), F32), so_shape],
        scratch_shapes=scratch,
        compiler_params=_cp(("parallel", "parallel")),
        name="gdn",
    )


def _cmul(ar, ai, br, bi):
    return ar * br - ai * bi, ar * bi + ai * br


def _s5_kernel(u_ref, bb_ref, cc_ref, lam_ref, dsk_ref, s0_ref, y_ref, sf_ref, up_s, x_s, yp_s,
               *, tb, seg, segs_per_seq, zero_init, col_major):
    r_par = tb // seg
    half = S5_GB * S5_STATE
    if col_major:
        rows = tb // GRID_W
        assert segs_per_seq == r_par and seg % rows == 0

        def step_rows(r):
            return pl.ds((r % rows) * GRID_W + r // rows, r_par, stride=seg // rows)
    else:
        def step_rows(r):
            return pl.ds(r, r_par, stride=seg)

    for r in range(seg):
        up_s[r * r_par:(r + 1) * r_par, :] = u_ref[step_rows(r), :]
    ub = up_s[...].astype(BF16)
    sf_ref[...] = jnp.zeros(sf_ref.shape, F32)
    nseq = r_par // segs_per_seq
    lam = [(lam_ref[0, d, 0:1, :], lam_ref[0, d, 1:2, :]) for d in (0, 1)]
    for d in (0, 1):
        x_s[d] = _dot(ub, bb_ref[0, d])

    def offsets(i):
        return [pl.multiple_of(r * r_par, r_par) for r in (i, seg - 1 - i)]

    def local_step(i, carry):
        off = offsets(i)
        new = []
        for d in (0, 1):
            (lre, lim), (hre, him) = lam[d], carry[d]
            nre = lre * hre - lim * him + x_s[d, pl.ds(off[d], r_par), 0:half]
            nim = lre * him + lim * hre + x_s[d, pl.ds(off[d], r_par), half:2 * half]
            x_s[d, pl.ds(off[d], r_par), 0:half] = nre
            x_s[d, pl.ds(off[d], r_par), half:2 * half] = nim
            new.append((nre, nim))
        return tuple(new)

    zero = jnp.zeros((r_par, half), F32)
    lax.fori_loop(0, seg, local_step, ((zero, zero), (zero, zero)))

    end_rows = ((seg - 1) * r_par, 0)
    needs_carry = not (zero_init and segs_per_seq == 1)
    hins = []
    for d in (0, 1):
        if not needs_carry:
            break
        lre, lim = lam[d]
        pre, pim = lre, lim
        n = 1
        while n < seg:
            pre, pim = _cmul(pre, pim, pre, pim)
            n *= 2
        end_row = end_rows[d]
        ere = x_s[d, end_row:end_row + r_par, 0:half]
        eim = x_s[d, end_row:end_row + r_par, half:2 * half]
        hin = [None] * r_par
        order = range(r_par) if d == 0 else range(r_par - 1, -1, -1)
        for j in order:
            sq, pos = divmod(j, segs_per_seq)
            first = pos == (0 if d == 0 else segs_per_seq - 1)
            if first:
                hin[j] = (s0_ref[0, 0, d, sq:sq + 1, 0:half], s0_ref[0, 0, d, sq:sq + 1, half:2 * half])
            else:
                pj = j - 1 if d == 0 else j + 1
                cre, cim = _cmul(pre, pim, hin[pj][0], hin[pj][1])
                hin[j] = (cre + ere[pj:pj + 1], cim + eim[pj:pj + 1])
        hins.append((jnp.concatenate([hj[0] for hj in hin], axis=0),
                     jnp.concatenate([hj[1] for hj in hin], axis=0)))

    def carry_step(i, pws):
        off = offsets(i)
        new = []
        for d in (0, 1):
            (lre, lim), (wre, wim) = lam[d], pws[d]
            are, aim = _cmul(wre, wim, hins[d][0], hins[d][1])
            x_s[d, pl.ds(off[d], r_par), 0:half] += are
            x_s[d, pl.ds(off[d], r_par), half:2 * half] += aim
            new.append(_cmul(wre, wim, lre, lim))
        return tuple(new)

    if needs_carry:
        lax.fori_loop(0, seg, carry_step, (lam[0], lam[1]))

    for d in (0, 1):
        for sq in range(nseq):
            j = (sq + 1) * segs_per_seq - 1 if d == 0 else sq * segs_per_seq
            row = end_rows[d] + j
            sf_ref[0, 0, d, sq:sq + 1, :] = x_s[d, row:row + 1, :]
    yp_s[...] = (_dot(x_s[0].astype(BF16), cc_ref[0, 0]) + _dot(x_s[1].astype(BF16), cc_ref[0, 1])
                 + up_s[...] * dsk_ref[...])
    for r in range(seg):
        y_ref[step_rows(r), :] = yp_s[r * r_par:(r + 1) * r_par, :]


def _s5(p_main, bb, cc, lam, dskip, s0, row_blk0, nblk, tb, t, seg, zero_init, y_prev=None, col_major=False):
    ngb = S5_GROUPS // S5_GB
    w2 = 2 * S5_GB * S5_STATE
    return _shared_out_call(
        functools.partial(_s5_kernel, tb=tb, seg=seg, segs_per_seq=t // seg, zero_init=zero_init,
                          col_major=col_major), [y_prev, None],
        [
            pl.BlockSpec((tb, LANES), lambda i, gb: (row_blk0 + i, 20 + gb)),
            pl.BlockSpec((1, 2, LANES, w2), lambda i, gb: (gb, 0, 0, 0)),
            pl.BlockSpec((1, 2, w2, LANES), lambda i, gb: (gb, 0, 0, 0)),
            pl.BlockSpec((1, 2, 2, w2 // 2), lambda i, gb: (gb, 0, 0, 0)),
            pl.BlockSpec((1, LANES), lambda i, gb: (0, gb)),
            pl.BlockSpec((1, 1, 2, 8, w2), lambda i, gb: (i, gb, 0, 0, 0)),
        ],
        (p_main, bb, cc, lam, dskip.reshape(1, BR_WIDTH), s0),
        grid=(nblk, ngb),
        out_specs=[pl.BlockSpec((tb, LANES), lambda i, gb: (row_blk0 + i, gb)),
                   pl.BlockSpec((1, 1, 2, 8, w2), lambda i, gb: (i, gb, 0, 0, 0))],
        out_shape=[jax.ShapeDtypeStruct((p_main.shape[0], BR_WIDTH), F32),
                   jax.ShapeDtypeStruct((nblk, ngb, 2, 8, w2), F32)],
        scratch_shapes=[pltpu.VMEM((tb, LANES), F32), pltpu.VMEM((2, tb, w2), F32), pltpu.VMEM((tb, LANES), F32)],
        compiler_params=_cp(("parallel", "parallel")),
        name="s5",
    )


def _s5_params(a_re, a_im, log_dt, b_re, b_im, c_re, c_im):
    dt = jnp.exp(log_dt)[..., None]
    mag = jnp.exp(dt * a_re)
    lam_re = mag * jnp.cos(dt * a_im)
    lam_im = mag * jnp.sin(dt * a_im)
    den = a_re * a_re + a_im * a_im
    co_re = ((lam_re - 1.0) * a_re + lam_im * a_im) / den
    co_im = (lam_im * a_re - (lam_re - 1.0) * a_im) / den
    bb_re = co_re[..., None] * b_re - co_im[..., None] * b_im
    bb_im = co_re[..., None] * b_im + co_im[..., None] * b_re
    ngb = S5_GROUPS // S5_GB
    eye = jnp.eye(S5_GB, dtype=F32)

    def pack_in(w):
        w = w.reshape(2, ngb, S5_GB, S5_STATE, S5_GROUP)
        blk = jnp.einsum('dbgpi,gh->bdgihp', w, eye)
        return blk.reshape(ngb, 2, S5_GB * S5_GROUP, S5_GB * S5_STATE)

    def pack_out(w):
        w = w.reshape(2, ngb, S5_GB, S5_GROUP, S5_STATE)
        blk = jnp.einsum('dbgip,gh->bdgphi', w, eye)
        return blk.reshape(ngb, 2, S5_GB * S5_STATE, S5_GB * S5_GROUP)

    bb = jnp.concatenate([pack_in(bb_re), pack_in(bb_im)], axis=-1).astype(BF16)
    cc = jnp.concatenate([pack_out(c_re), -pack_out(c_im)], axis=-2).astype(BF16)
    lam = jnp.stack([lam_re, lam_im], axis=1)
    lam = lam.reshape(2, 2, ngb, S5_GB * S5_STATE).transpose(2, 0, 1, 3)
    return bb, cc, lam


def _glu_kernel(y_ref, w_ref, b_ref, o_ref, wb_ref):
    @pl.when(pl.program_id(0) == 0)
    def _():
        wb_ref[...] = w_ref[...].astype(BF16)

    z = _dot(y_ref[...].astype(BF16), wb_ref[...]) + b_ref[...]
    half = z.shape[1] // 2
    o_ref[...] = z[:, :half] * jax.nn.sigmoid(z[:, half:])


def _glu(y, w, b, l):
    n, k = y.shape
    n2 = w.shape[-1]
    return pl.pallas_call(
        _glu_kernel,
        grid=(n // TM,),
        in_specs=[pl.BlockSpec((TM, k), lambda i: (i, 0)),
                  pl.BlockSpec((None, k, n2), lambda i: (l, 0, 0)),
                  pl.BlockSpec((None, 1, n2), lambda i: (l, 0, 0))],
        out_specs=pl.BlockSpec((TM, n2 // 2), lambda i: (i, 0)),
        out_shape=jax.ShapeDtypeStruct((n, n2 // 2), F32),
        scratch_shapes=[pltpu.VMEM((k, n2), BF16)],
        compiler_params=_cp(("arbitrary",)),
        name="s5_glu",
    )(y, w, b.reshape(b.shape[0], 1, n2))


def _merge_kernel(ya_ref, yb_ref, yc_ref, mg_ref, wbr_ref, wout_ref, g_ref, o_ref, wbr_s, wout_s):
    @pl.when(pl.program_id(0) == 0)
    def _():
        wbr_s[...] = wbr_ref[...].astype(BF16)
        wout_s[...] = wout_ref[...].astype(BF16)

    d = wout_s.shape[0]
    acc = None
    for n, br_ref in enumerate((ya_ref, yb_ref, yc_ref)):
        proj = _dot(br_ref[...].astype(BF16), wbr_s[n])
        term = proj * jax.nn.sigmoid(mg_ref[:, n * d:(n + 1) * d])
        acc = term if acc is None else acc + term
    y = _dot(acc.astype(BF16), wout_s[...])
    o_ref[...] = _rms(y, g_ref[...])


def _merge(ya, yb, yc, mg, w_branch, w_out, g, l):
    n = ya.shape[0]
    d = w_out.shape[-1]
    tm = 256
    br = pl.BlockSpec((tm, BR_WIDTH), lambda i: (i, 0))
    return pl.pallas_call(
        _merge_kernel,
        grid=(n // tm,),
        in_specs=[br, br, br,
                  pl.BlockSpec((tm, 3 * d), lambda i: (i, 0)),
                  pl.BlockSpec((None, 3, BR_WIDTH, d), lambda i: (l, 0, 0, 0)),
                  pl.BlockSpec((None, d, d), lambda i: (l, 0, 0)),
                  pl.BlockSpec((1, d), lambda i: (0, 0))],
        out_specs=pl.BlockSpec((tm, d), lambda i: (i, 0)),
        out_shape=jax.ShapeDtypeStruct((n, d), F32),
        scratch_shapes=[pltpu.VMEM((3, BR_WIDTH, d), BF16), pltpu.VMEM((d, d), BF16)],
        compiler_params=_cp(("arbitrary",)),
        name="merge",
    )(ya, yb, yc, mg, w_branch, w_out, g.reshape(1, d))


def _swiglu_kernel(*refs, gated):
    if gated:
        x_ref, wg_ref, wu_ref, gt_ref, o_ref, wg_s, wu_s = refs
    else:
        x_ref, wg_ref, wu_ref, o_ref, wg_s, wu_s = refs

    @pl.when(pl.program_id(1) == 0)
    def _():
        wg_s[...] = wg_ref[...].astype(BF16)
        wu_s[...] = wu_ref[...].astype(BF16)

    x = x_ref[...]
    hid = _silu(_dot(x, wg_s[...])) * _dot(x, wu_s[...])
    if gated:
        gt = gt_ref[...]
        lane = lax.broadcasted_iota(jnp.int32, gt.shape, 1)
        hid = hid * jnp.sum(jnp.where(lane == pl.program_id(0), gt, 0.0), axis=-1, keepdims=True)
    o_ref[...] = hid.astype(BF16)


def _swiglu_up(x, wg, wu, e, f, gates=None):
    n, d = x.shape
    gated = gates is not None
    if gated:
        ne = wg.shape[1]
        wspec = pl.BlockSpec((None, None, d, f), lambda j, i: (e, j, 0, 0))
    else:
        ne = wg.shape[-1] // f
        wspec = pl.BlockSpec((None, d, f), lambda j, i: (e, 0, j))
    in_specs = [pl.BlockSpec((TM, d), lambda j, i: (i, 0)), wspec, wspec]
    args = [x, wg, wu]
    if gated:
        in_specs.append(pl.BlockSpec((TM, LANES), lambda j, i: (i, 0)))
        args.append(gates)
    return pl.pallas_call(
        functools.partial(_swiglu_kernel, gated=gated),
        grid=(ne, n // TM),
        in_specs=in_specs,
        out_specs=pl.BlockSpec((TM, f), lambda j, i: (i, j)),
        out_shape=jax.ShapeDtypeStruct((n, ne * f), BF16),
        scratch_shapes=[pltpu.VMEM((d, f), BF16), pltpu.VMEM((d, f), BF16)],
        compiler_params=_cp(("arbitrary", "arbitrary")),
        name="swiglu_up",
    )(*args)


def _down_kernel(x_ref, w_ref, g_ref, o_ref, acc):
    k = pl.program_id(1)
    part = _dot(x_ref[...], w_ref[...].astype(BF16))

    @pl.when(k == 0)
    def _():
        acc[...] = part

    @pl.when(k > 0)
    def _():
        acc[...] += part

    @pl.when(k == pl.num_programs(1) - 1)
    def _():
        o_ref[...] = _rms(acc[...], g_ref[...])


def _down_norm(hid, wd, e, g, tm):
    n = hid.shape[0]
    _, ne, f, d = wd.shape
    return pl.pallas_call(
        _down_kernel,
        grid=(n // tm, ne),
        in_specs=[pl.BlockSpec((tm, f), lambda i, k: (i, k)),
                  pl.BlockSpec((None, None, f, d), lambda i, k: (e, k, 0, 0)),
                  pl.BlockSpec((1, d), lambda i, k: (0, 0))],
        out_specs=pl.BlockSpec((tm, d), lambda i, k: (i, 0)),
        out_shape=jax.ShapeDtypeStruct((n, d), F32),
        scratch_shapes=[pltpu.VMEM((tm, d), F32)],
        compiler_params=_cp(("parallel", "arbitrary")),
        name="down_norm",
    )(hid, wd, g.reshape(1, d))


def _router_kernel(x_ref, w_ref, aux_ref, cnt_ref, base_s, *, ne):
    @pl.when(pl.program_id(0) == 0)
    def _():
        base_s[...] = jnp.zeros(base_s.shape, F32)

    w = w_ref[...]
    w_hi = w.astype(BF16)
    w_lo = (w - w_hi.astype(F32)).astype(BF16)
    x = x_ref[...].astype(BF16)
    logits = _dot(x, w_hi) + _dot(x, w_lo)
    tm = logits.shape[0]
    lane = lax.broadcasted_iota(jnp.int32, logits.shape, 1).astype(F32)
    neg = jnp.float32(-jnp.inf)
    logits = jnp.where(lane < ne, logits, neg)
    m1 = jnp.max(logits, axis=-1, keepdims=True)
    i1 = jnp.min(jnp.where(logits == m1, lane, float(LANES)), axis=-1, keepdims=True)
    rest = jnp.where(lane == i1, neg, logits)
    m2 = jnp.max(rest, axis=-1, keepdims=True)
    i2 = jnp.min(jnp.where(rest == m2, lane, float(LANES)), axis=-1, keepdims=True)
    e2 = jnp.exp(m2 - m1)
    w1 = 1.0 / (1.0 + e2)
    w2 = e2 / (1.0 + e2)
    sel = jnp.where((lane == i1) | (lane == i2), 1.0, 0.0)
    ri = lax.broadcasted_iota(jnp.int32, (tm, tm), 0)
    ci = lax.broadcasted_iota(jnp.int32, (tm, tm), 1)
    before = jnp.where(ci < ri, 1.0, 0.0).astype(BF16)
    rank = _dot(before, sel.astype(BF16)) + base_s[0:1, :]
    r1 = jnp.sum(jnp.where(lane == i1, rank, 0.0), axis=-1, keepdims=True)
    r2 = jnp.sum(jnp.where(lane == i2, rank, 0.0), axis=-1, keepdims=True)
    aux = jnp.zeros(logits.shape, F32)
    for k, v in enumerate((i1, i2, r1, r2, w1, w2)):
        aux = jnp.where(lane == float(k), v, aux)
    aux_ref[...] = aux
    base_s[...] = base_s[...] + jnp.sum(sel, axis=0, keepdims=True)
    cnt_ref[...] = base_s[...]


def _router(x, w_pad, ne):
    n, d = x.shape
    return pl.pallas_call(
        functools.partial(_router_kernel, ne=ne),
        grid=(n // TM,),
        in_specs=[pl.BlockSpec((TM, d), lambda i: (i, 0)),
                  pl.BlockSpec((d, LANES), lambda i: (0, 0))],
        out_specs=[pl.BlockSpec((TM, LANES), lambda i: (i, 0)),
                   pl.BlockSpec((8, LANES), lambda i: (0, 0))],
        out_shape=[jax.ShapeDtypeStruct((n, LANES), F32), jax.ShapeDtypeStruct((8, LANES), F32)],
        scratch_shapes=[pltpu.VMEM((8, LANES), F32)],
        compiler_params=_cp(("arbitrary",)),
        name="router",
    )(x, w_pad)


def _moe_plan(aux, cnt, n, ne, tile, max_tiles):
    e = aux[:, 0:2].astype(jnp.int32)
    r = aux[:, 2:4].astype(jnp.int32)
    c = cnt[0, :ne].astype(jnp.int32)
    pc = (c + tile - 1) // tile * tile
    ends = jnp.cumsum(pc)
    slot = ((ends - pc)[e] + r).reshape(-1)
    tok = jnp.broadcast_to(jnp.arange(n, dtype=F32)[:, None], (n, 2)).reshape(-1)
    m = max_tiles * tile
    pad = jnp.broadcast_to(jnp.array([float(n), 0.0], F32), (m, 2))
    table = pad.at[slot].set(jnp.stack([tok, aux[:, 4:6].reshape(-1)], axis=1))
    slot_tok = table[:, 0].astype(jnp.int32)
    slot_gate = table[:, 1]
    tstart = jnp.arange(max_tiles, dtype=jnp.int32) * tile
    tile_e = jnp.minimum(jnp.sum(tstart[:, None] >= ends[None, :], axis=1), ne - 1).astype(jnp.int32)
    tile_valid = (tstart < ends[-1]).astype(jnp.int32)
    return slot_tok.reshape(max_tiles, 1, tile), slot_gate.reshape(max_tiles, 1, tile), tile_e, tile_valid


def _expert_changed(t, te_ref):
    return jnp.logical_or(t == 0, te_ref[t] != te_ref[jnp.maximum(t - 1, 0)])


def _moe_up_kernel(te_ref, tv_ref, idx_ref, h_ref, wg_ref, wu_ref, o_ref, xbuf, wg_s, wu_s, *, tile):
    t = pl.program_id(0)
    valid = tv_ref[t] == 1

    @pl.when(jnp.logical_and(valid, _expert_changed(t, te_ref)))
    def _():
        wg_s[...] = wg_ref[...].astype(BF16)
        wu_s[...] = wu_ref[...].astype(BF16)

    @pl.when(valid)
    def _():
        last = h_ref.shape[0] - 1

        def gather(j, carry):
            xbuf[pl.ds(j, 1), :] = h_ref[pl.ds(jnp.minimum(idx_ref[0, 0, j], last), 1), :]
            return carry

        lax.fori_loop(0, tile, gather, 0, unroll=8)
        x = xbuf[...].astype(BF16)
        o_ref[...] = (_silu(_dot(x, wg_s[...])) * _dot(x, wu_s[...])).astype(BF16)

    @pl.when(jnp.logical_not(valid))
    def _():
        o_ref[...] = jnp.zeros(o_ref.shape, BF16)


def _moe_up(h, slot_tok, tile_e, tile_valid, wg, wu, e_layer, tile):
    max_tiles = slot_tok.shape[0]
    _, _, d, f = wg.shape
    wspec = pl.BlockSpec((None, None, d, f), lambda t, te, tv: (e_layer, te[t], 0, 0),
                         pipeline_mode=pl.Buffered(1))
    grid_spec = pltpu.PrefetchScalarGridSpec(
        num_scalar_prefetch=2,
        grid=(max_tiles,),
        in_specs=[pl.BlockSpec((1, 1, tile), lambda t, te, tv: (t, 0, 0), memory_space=pltpu.SMEM),
                  pl.BlockSpec(h.shape, lambda t, te, tv: (0, 0), pipeline_mode=pl.Buffered(1)),
                  wspec, wspec],
        out_specs=pl.BlockSpec((tile, f), lambda t, te, tv: (t, 0)),
        scratch_shapes=[pltpu.VMEM((tile, d), F32),
                        pltpu.VMEM((d, f), BF16), pltpu.VMEM((d, f), BF16)],
    )
    return pl.pallas_call(
        functools.partial(_moe_up_kernel, tile=tile),
        grid_spec=grid_spec,
        out_shape=jax.ShapeDtypeStruct((max_tiles * tile, f), BF16),
        compiler_params=_cp(("arbitrary",)),
        name="moe_up",
    )(tile_e, tile_valid, slot_tok, h, wg, wu)


def _moe_down_kernel(te_ref, tv_ref, idx_ref, gt_ref, hid_ref, wd_ref, g_ref, o_ref, acc, ybuf, tbuf, wd_s,
                     *, tile, n_tiles, tm_out):
    t = pl.program_id(0)
    tt = jnp.minimum(t, n_tiles - 1)

    @pl.when(t == 0)
    def _():
        acc[...] = jnp.zeros(acc.shape, F32)

    is_tile = jnp.logical_and(t < n_tiles, tv_ref[tt] == 1)

    @pl.when(jnp.logical_and(is_tile, _expert_changed(tt, te_ref)))
    def _():
        wd_s[...] = wd_ref[...].astype(BF16)

    @pl.when(is_tile)
    def _():
        ybuf[...] = _dot(hid_ref[...], wd_s[...])

        def fetch(j, carry):
            tbuf[pl.ds(j, 1), :] = acc[pl.ds(idx_ref[0, 0, j], 1), :]
            return carry

        def put(j, carry):
            acc[pl.ds(idx_ref[0, 0, j], 1), :] = tbuf[pl.ds(j, 1), :] + gt_ref[0, 0, j] * ybuf[pl.ds(j, 1), :]
            return carry

        lax.fori_loop(0, tile, fetch, 0, unroll=8)
        lax.fori_loop(0, tile, put, 0, unroll=8)

    @pl.when(t >= n_tiles)
    def _():
        r0 = pl.multiple_of((t - n_tiles) * tm_out, tm_out)
        o_ref[...] = _rms(acc[pl.ds(r0, tm_out), :], g_ref[...])


def _moe_down(hid, slot_tok, slot_gate, tile_e, tile_valid, wd, e_layer, g, n, tile):
    n_tiles = slot_tok.shape[0]
    _, _, f, d = wd.shape
    tm_out = TM
    n_flush = n // tm_out

    def tix(t):
        return jnp.minimum(t, n_tiles - 1)

    smem_tile = pl.BlockSpec((1, 1, tile), lambda t, te, tv: (tix(t), 0, 0), memory_space=pltpu.SMEM)
    grid_spec = pltpu.PrefetchScalarGridSpec(
        num_scalar_prefetch=2,
        grid=(n_tiles + n_flush,),
        in_specs=[smem_tile, smem_tile,
                  pl.BlockSpec((tile, f), lambda t, te, tv: (tix(t), 0)),
                  pl.BlockSpec((None, None, f, d), lambda t, te, tv: (e_layer, te[tix(t)], 0, 0)),
                  pl.BlockSpec((1, d), lambda t, te, tv: (0, 0))],
        out_specs=pl.BlockSpec((tm_out, d), lambda t, te, tv: (jnp.maximum(t - n_tiles, 0), 0)),
        scratch_shapes=[pltpu.VMEM((n + 8, d), F32), pltpu.VMEM((tile, d), F32), pltpu.VMEM((tile, d), F32),
                        pltpu.VMEM((f, d), BF16)],
    )
    return pl.pallas_call(
        functools.partial(_moe_down_kernel, tile=tile, n_tiles=n_tiles, tm_out=tm_out),
        grid_spec=grid_spec,
        out_shape=jax.ShapeDtypeStruct((n, d), F32),
        compiler_params=_cp(("arbitrary",)),
        name="moe_down",
    )(tile_e, tile_valid, slot_tok, slot_gate, hid, wd, g.reshape(1, d))


def kernel(x_prompt, x_sample, state_hgrn, state_s5, state_gdn, c, c_ctx, ada_w, ada_b, norm_g, w_in, hgrn_lb, hgrn_norm, s5_a_re, s5_a_im, s5_log_dt, s5_b_re, s5_b_im, s5_c_re, s5_c_im, s5_d, s5_glu_w, s5_glu_b, gdn_conv, gdn_a_log, gdn_dt_bias, gdn_norm, w_branch, w_out, ffn_wg, ffn_wu, ffn_wd, moe_router, moe_wg, moe_wu, moe_wd):
    bp, tp, d = x_prompt.shape
    bs, ts, _ = x_sample.shape
    depth = w_in.shape[0]
    n_p, n_s = bp * tp, bs * ts
    ne = moe_router.shape[-1]
    main_cols = 10 * BR_WIDTH
    ab_cols = 4 * N_HEADS

    c8 = jnp.zeros((8, d), F32).at[0].set(c_ctx).at[1:1 + bs].set(c)
    mod_all = _ada_mod(c8, ada_w, ada_b)

    lb_soft = jax.nn.softmax(hgrn_lb.astype(F32), axis=0)
    lb_all = jnp.cumsum(lb_soft, axis=0) - lb_soft[:1]

    mods = mod_all[:, :1 + bs].reshape(depth, 1 + bs, 6, d)
    zero_row = jnp.zeros((1 + bs, d), F32)

    def modv(gate, scale, shift, g):
        rows = [zero_row if sel is None else mods[sel[0], :, sel[1]] for sel in (gate, scale, shift)]
        rows.append(zero_row if g is None else jnp.broadcast_to(g, (1 + bs, d)))
        return jnp.stack(rows + [zero_row] * 4, axis=1)

    x = jnp.concatenate([x_prompt.reshape(n_p, d), x_sample.reshape(n_s, d)], axis=0)
    (h,) = _resnorm(x, None, modv(None, (0, 1), (0, 0), norm_g[0, 0]), n_p, ts, True)

    z_hgrn = jnp.zeros((bp, 2, N_HEADS, HEAD_D, HEAD_D), F32)
    ngb = S5_GROUPS // S5_GB
    w2 = 2 * S5_GB * S5_STATE
    st_s5 = []
    y_zero = jnp.zeros((n_p + n_s, BR_WIDTH), F32)
    new_hgrn = new_gdn = jnp.zeros((bp, depth, 2, N_HEADS, HEAD_D, HEAD_D), F32)
    w_in_t = jnp.swapaxes(w_in, 1, 2)
    s5_tb = 1024

    for l in range(depth):
        col_major = l % 2 == 1
        p_main = _mm_t(h, w_in_t, l, main_cols, 2 * BR_WIDTH, "w_in_main", tm=1024)
        w_ab = jnp.pad(w_in_t[l, main_cols:main_cols + ab_cols], ((0, LANES - ab_cols), (0, 0)))
        p_ab = _mm_t(h, w_ab, None, LANES, LANES, "w_in_ab", tm=1024)
        p_mg = _mm_t(h, w_in_t[l, main_cols + ab_cols:], None, 3 * d, 2 * BR_WIDTH, "w_in_merge", tm=1024)

        ya, new_hgrn = _hgrn(p_main, lb_all[l], hgrn_norm[l], z_hgrn, 0, bp, tp, y_zero, new_hgrn, l, depth)
        ya, _ = _hgrn(p_main, lb_all[l], hgrn_norm[l], state_hgrn[:, l].astype(F32), n_p // ts, bs, ts, ya,
                      col_major=col_major)

        bb, cc, lam = _s5_params(s5_a_re[l], s5_a_im[l], s5_log_dt[l], s5_b_re[l], s5_b_im[l],
                                 s5_c_re[l], s5_c_im[l])
        seq_per_blk = 8
        tb_p = seq_per_blk * tp
        nblk_p = n_p // tb_p
        yb, sb = _s5(p_main, bb, cc, lam, s5_d[l], jnp.zeros((nblk_p, ngb, 2, 8, w2), F32),
                     0, nblk_p, tb_p, tp, tp, True, y_zero)
        s0s = state_s5[:, l].astype(F32)
        s0s = s0s.reshape(bs, 2, ngb, S5_GB * S5_STATE, 2).transpose(0, 2, 1, 4, 3).reshape(bs, ngb, 2, 1, w2)
        s0s = jnp.pad(s0s, ((0, 0), (0, 0), (0, 0), (0, 7), (0, 0)))
        yb, _ = _s5(p_main, bb, cc, lam, s5_d[l], s0s, n_p // s5_tb, n_s // s5_tb, s5_tb, ts, S5_SEG, False, yb,
                    col_major=col_major)
        yb = _glu(yb, s5_glu_w, s5_glu_b, l)
        sb = sb[:, :, :, :seq_per_blk].reshape(nblk_p, ngb, 2, seq_per_blk, 2, S5_GB, S5_STATE)
        sb = sb.transpose(0, 3, 2, 1, 5, 6, 4).reshape(bp, 2, S5_GROUPS, S5_STATE, 2)
        st_s5.append(sb)

        yc, new_gdn = _gdn(p_main, p_ab, gdn_conv[l], gdn_a_log[l], gdn_dt_bias[l], gdn_norm[l], z_hgrn, 0, bp, tp,
                           y_zero, new_gdn, l, depth)
        yc, _ = _gdn(p_main, p_ab, gdn_conv[l], gdn_a_log[l], gdn_dt_bias[l], gdn_norm[l],
                     state_gdn[:, l].astype(F32), n_p // ts, bs, ts, yc, col_major=col_major)

        r_m = _merge(ya, yb, yc, p_mg, w_branch, w_out, norm_g[l, 1], l)
        x, h2 = _resnorm(x, r_m, modv((l, 2), (l, 4), (l, 3), norm_g[l, 2]), n_p, ts, True,
                         h_dtype=BF16 if l % 2 == 0 else F32)

        if l % 2 == 0:
            e = l // 2
            hid = _swiglu_up(h2, ffn_wg, ffn_wu, e, ffn_wg.shape[-1] // 2)
            r_f = _down_norm(hid, ffn_wd[:, None], e, norm_g[l, 3], TM)
        else:
            e = l // 2
            n_tok = n_p + n_s
            max_tiles = 2 * n_tok // MOE_TILE + ne
            aux, cnt = _router(h2, jnp.pad(moe_router[e], ((0, 0), (0, LANES - ne))), ne)
            slot_tok, slot_gate, tile_e, tile_valid = _moe_plan(aux, cnt, n_tok, ne, MOE_TILE, max_tiles)
            hid = _moe_up(h2, slot_tok, tile_e, tile_valid, moe_wg, moe_wu, e, MOE_TILE)
            r_f = _moe_down(hid, slot_tok, slot_gate, tile_e, tile_valid, moe_wd, e, norm_g[l, 3], n_tok, MOE_TILE)

        if l + 1 < depth:
            x, h = _resnorm(x, r_f, modv((l, 5), (l + 1, 1), (l + 1, 0), norm_g[l + 1, 0]), n_p, ts, True)
        else:
            mv = modv((l, 5), None, None, None)
            (y_prompt,) = _resnorm(x, r_f, mv, n_p, ts, False, rows=(0, n_p))
            (y_sample,) = _resnorm(x, r_f, mv, n_p, ts, False, rows=(n_p, n_s))

    y_prompt = y_prompt.reshape(bp, tp, d)
    y_sample = y_sample.reshape(bs, ts, d)
    new_s5 = jnp.stack(st_s5, axis=1).astype(x_prompt.dtype)
    return (y_prompt, y_sample, new_hgrn.astype(x_prompt.dtype), new_s5, new_gdn.astype(x_prompt.dtype))
```

```python
import functools
import math

import jax
import jax.numpy as jnp
from jax import lax
from jax.experimental import pallas as pl
from jax.experimental.pallas import tpu as pltpu

F32 = jnp.float32
BF16 = jnp.bfloat16

EPS = 1e-6
F_FLOOR = 1e-6
LANES = 128
GRID_W = 64
HEAD_D = 128
N_HEADS = 4
BR_WIDTH = 512
S5_GROUPS = 32
S5_GROUP = 16
S5_STATE = 64
S5_GB = 8
S5_SEG = 64
GLA_CHUNK = 256
GDN_CHUNK = 64
GDN_SPAN = 256
GDN_HEADS_PER_STEP = 4
GDN_MAX_ROWS_PER_STEP = 2048
TM = 512
MOE_TILE = 256
VMEM_LIMIT = 56 * 1024 * 1024


def _cp(sem, vmem=VMEM_LIMIT):
    return pltpu.CompilerParams(dimension_semantics=sem, vmem_limit_bytes=vmem)


def _dot(a, b):
    return jnp.dot(a, b, preferred_element_type=F32)


def _dot_nt(a, b):
    return lax.dot_general(a, b, (((1,), (1,)), ((), ())), preferred_element_type=F32)


def _dot_tn(a, b):
    return lax.dot_general(a, b, (((0,), (0,)), ((), ())), preferred_element_type=F32)


def _silu(x):
    return x * jax.nn.sigmoid(x)


def _rms(x, g):
    return x * lax.rsqrt(jnp.mean(x * x, axis=-1, keepdims=True) + EPS) * g


def _chunk_cumsum(x, d, c):
    n = x.shape[0]
    pos = lax.broadcasted_iota(jnp.int32, x.shape, 0) & (c - 1)
    s = 1
    while s < c:
        if d == 0:
            x = x + jnp.where(pos >= s, pltpu.roll(x, s, 0), 0.0)
        else:
            x = x + jnp.where(pos < c - s, pltpu.roll(x, n - s, 0), 0.0)
        s *= 2
    return x


def _ada_kernel(c_ref, w_ref, b_ref, o_ref):
    c = c_ref[...]
    a = _silu(c).astype(BF16)
    o_ref[...] = _dot(a, w_ref[...].astype(BF16)) + b_ref[...]


def _ada_mod(c8, ada_w, ada_b):
    depth, d, n6 = ada_w.shape
    tn = 512
    return pl.pallas_call(
        _ada_kernel,
        grid=(depth, n6 // tn),
        in_specs=[
            pl.BlockSpec((8, d), lambda l, j: (0, 0)),
            pl.BlockSpec((None, d, tn), lambda l, j: (l, 0, j)),
            pl.BlockSpec((None, 1, tn), lambda l, j: (l, 0, j)),
        ],
        out_specs=pl.BlockSpec((None, 8, tn), lambda l, j: (l, 0, j)),
        out_shape=jax.ShapeDtypeStruct((depth, 8, n6), F32),
        compiler_params=_cp(("parallel", "parallel")),
        name="ada_mod",
    )(c8, ada_w, ada_b.reshape(depth, 1, n6))


def _resnorm_kernel(*refs, has_y, has_h):
    it = iter(refs)
    x_ref = next(it)
    y_ref = next(it) if has_y else None
    mv_ref = next(it)
    xo_ref = next(it) if has_y else None
    h_ref = next(it) if has_h else None
    x = x_ref[...]
    mv = mv_ref[...]
    if has_y:
        x = x + mv[0:1] * y_ref[...]
        xo_ref[...] = x
    if has_h:
        h = _rms(x, mv[3:4]) * (1.0 + mv[1:2]) + mv[2:3]
        h_ref[...] = h.astype(h_ref.dtype)


def _resnorm(x, y, modv, n_prompt, seq_s, has_h, rows=None, h_dtype=BF16):
    d = x.shape[1]
    row0, n = (0, x.shape[0]) if rows is None else rows
    i0 = row0 // TM
    has_y = y is not None

    def group(i):
        r = (i + i0) * TM
        return jnp.where(r < n_prompt, 0, 1 + (r - n_prompt) // seq_s)

    tile = pl.BlockSpec((TM, d), lambda i: (i + i0, 0))
    in_specs = [tile] + ([tile] if has_y else []) + [pl.BlockSpec((None, 8, d), lambda i: (group(i), 0, 0))]
    out_tile = pl.BlockSpec((TM, d), lambda i: (i, 0))
    out_specs, out_shape = [], []
    if has_y:
        out_specs.append(out_tile)
        out_shape.append(jax.ShapeDtypeStruct((n, d), F32))
    if has_h:
        out_specs.append(out_tile)
        out_shape.append(jax.ShapeDtypeStruct((n, d), h_dtype))
    args = [x] + ([y] if has_y else []) + [modv]
    outs = pl.pallas_call(
        functools.partial(_resnorm_kernel, has_y=has_y, has_h=has_h),
        grid=(n // TM,),
        in_specs=in_specs,
        out_specs=out_specs,
        out_shape=out_shape,
        compiler_params=_cp(("parallel",)),
        name="resnorm",
    )(*args)
    return outs


def _mm_kernel(x_ref, w_ref, o_ref, wb_ref):
    @pl.when(pl.program_id(1) == 0)
    def _():
        wb_ref[...] = w_ref[...].astype(BF16)

    o_ref[...] = _dot_nt(x_ref[...], wb_ref[...])


def _mm_t(x, wt, l, nrows, tn, name, tm=TM):
    m, k = x.shape
    if l is None:
        wspec = pl.BlockSpec((tn, k), lambda j, i: (j, 0))
    else:
        wspec = pl.BlockSpec((None, tn, k), lambda j, i: (l, j, 0))
    return pl.pallas_call(
        _mm_kernel,
        grid=(nrows // tn, m // tm),
        in_specs=[pl.BlockSpec((tm, k), lambda j, i: (i, 0)), wspec],
        out_specs=pl.BlockSpec((tm, tn), lambda j, i: (i, j)),
        out_shape=jax.ShapeDtypeStruct((m, nrows), F32),
        scratch_shapes=[pltpu.VMEM((tn, k), BF16)],
        compiler_params=_cp(("arbitrary", "arbitrary")),
        name=name,
    )(x, wt)


def _gla_intra(q, k, b, d, c):
    rowi = lax.broadcasted_iota(jnp.int32, (c, LANES), 0)
    ri = lax.broadcasted_iota(jnp.int32, (c, c), 0)
    ci = lax.broadcasted_iota(jnp.int32, (c, c), 1)
    a = jnp.zeros((c, c), F32)
    size = c
    while size >= 8:
        half = size // 2
        nb = c // size
        b3 = b.reshape(nb, size, LANES)
        pos = rowi & (size - 1)
        if d == 0:
            ref = b3[:, half - 1:half, :]
            qsel = pos >= half
        else:
            ref = b3[:, half:half + 1, :]
            qsel = pos < half
        ref = jnp.broadcast_to(ref, (nb, size, LANES)).reshape(c, LANES)
        z = jnp.where(qsel, q, k) * jnp.exp(-jnp.abs(b - ref))
        qs = jnp.where(qsel, z, 0.0).astype(BF16)
        ks = jnp.where(qsel, 0.0, z).astype(BF16)
        p = _dot_nt(qs, ks)
        if nb > 1:
            shift = int(math.log2(size))
            p = jnp.where((ri >> shift) == (ci >> shift), p, 0.0)
        a = a + p
        size = half
    b3 = b.reshape(c // 8, 8, LANES)
    lo, hi = (0, 4) if d == 0 else (3, 7)
    ref_lo = jnp.broadcast_to(b3[:, lo:lo + 1, :], (c // 8, 8, LANES)).reshape(c, LANES)
    ref_hi = jnp.broadcast_to(b3[:, hi:hi + 1, :], (c // 8, 8, LANES)).reshape(c, LANES)
    ref = jnp.where((rowi & 7) >= 4, ref_hi, ref_lo)
    qs = (q * jnp.exp(jnp.minimum(b - ref, 0.0))).astype(BF16)
    ks = (k * jnp.exp(ref - b)).astype(BF16)
    p = _dot_nt(qs, ks)
    keep = ((ri >> 2) == (ci >> 2)) & ((ci <= ri) if d == 0 else (ci >= ri))
    return a + jnp.where(keep, p, 0.0)


def _load_column_major(dst_ref, src_ref, t, stage=None):
    rows = t // GRID_W
    for k in range(src_ref.shape[1] // LANES):
        lanes = slice(k * LANES, (k + 1) * LANES)
        if stage is None:
            strided = src_ref
        else:
            stage[...] = src_ref[:, lanes]
            strided = stage
        for col in range(GRID_W):
            dst_ref[col * rows:(col + 1) * rows, lanes] = strided[pl.ds(col, rows, stride=GRID_W), :]


def _store_row_major(dst_ref, y, t, stage=None):
    rows = t // GRID_W
    for k in range(y.shape[1] // LANES):
        lanes = slice(k * LANES, (k + 1) * LANES)
        strided = dst_ref if stage is None else stage
        for col in range(GRID_W):
            strided[pl.ds(col, rows, stride=GRID_W), :] = y[col * rows:(col + 1) * rows, lanes]
        if stage is not None:
            dst_ref[:, lanes] = stage[...]


def _hgrn_kernel(aq_ref, af_ref, ab_ref, ai_ref, ag_ref, lb_ref, g_ref, s0_ref, y_ref, s_ref, o_acc, *perm_bufs,
                 t, c):
    if perm_bufs:
        srcs = (aq_ref, af_ref, ab_ref, ai_ref, ag_ref)
        for dst, src in zip(perm_bufs, srcs):
            _load_column_major(dst, src, t)
        aq_ref, af_ref, ab_ref, ai_ref, ag_ref = perm_bufs
    nchunk = t // c
    scale = HEAD_D ** -0.5
    for d in (0, 1):
        z_ref = af_ref if d == 0 else ab_ref
        lbd = lb_ref[d:d + 1, :]

        def chunk(i, st, d=d, z_ref=z_ref, lbd=lbd):
            cidx = i if d == 0 else nchunk - 1 - i
            r0 = pl.multiple_of(cidx * c, c)
            z = z_ref[pl.ds(r0, c), :]
            f = lbd + (1.0 - lbd) * jax.nn.sigmoid(z)
            lf = jnp.log(jnp.maximum(f, F_FLOOR))
            k = (1.0 - lbd) * jax.nn.sigmoid(-z)
            q = _silu(aq_ref[pl.ds(r0, c), :]) * scale
            v = ai_ref[pl.ds(r0, c), :].astype(BF16)
            b = _chunk_cumsum(lf, d, c)
            a = _gla_intra(q, k, b, d, c)
            o = _dot(a.astype(BF16), v) + _dot_nt((q * jnp.exp(b)).astype(BF16), st.astype(BF16))
            if d == 0:
                o_acc[pl.ds(r0, c), :] = o
            else:
                o_acc[pl.ds(r0, c), :] += o
            blast = b[c - 1:c, :] if d == 0 else b[0:1, :]
            kd = (k * jnp.exp(blast - b)).astype(BF16)
            return st * jnp.exp(blast) + _dot_tn(v, kd)

        st = s0_ref[0, d, 0].T
        if nchunk == 1:
            st = chunk(0, st)
        else:
            st = lax.fori_loop(0, nchunk, chunk, st)
        s_ref[0, d, 0] = st.T
    y = _rms(o_acc[...], g_ref[...]) * _silu(ag_ref[...])
    if perm_bufs:
        _store_row_major(y_ref, y, t)
    else:
        y_ref[...] = y


def _shared_out_call(kernel_fn, prevs, in_specs, args, **kw):
    extra = [p for p in prevs if p is not None]
    if not extra:
        return pl.pallas_call(kernel_fn, in_specs=in_specs, **kw)(*args)
    n_in = len(in_specs)
    aliases, k = {}, 0
    for out_idx, p in enumerate(prevs):
        if p is not None:
            aliases[n_in + k] = out_idx
            k += 1

    def body(*refs):
        kernel_fn(*refs[:n_in], *refs[n_in + len(extra):])

    return pl.pallas_call(body, in_specs=in_specs + [pl.BlockSpec(memory_space=pl.ANY)] * len(extra),
                          input_output_aliases=aliases, **kw)(*args, *extra)


def _state_out(nseq, slot, nslots, hg):
    spec = pl.BlockSpec((1, None, 2, hg, HEAD_D, HEAD_D), lambda s, h: (s, slot, 0, h, 0, 0))
    return spec, jax.ShapeDtypeStruct((nseq, nslots, 2, N_HEADS, HEAD_D, HEAD_D), F32)


def _hgrn(p_main, lb, norm_g, s0, row_blk0, nseq, t, y_prev=None, s_prev=None, slot=0, nslots=1,
          col_major=False):
    c = min(GLA_CHUNK, t)
    seq_buf = pltpu.VMEM((t, HEAD_D), F32)

    def col(base):
        return pl.BlockSpec((t, HEAD_D), lambda s, h: (row_blk0 + s, base + h))

    st_spec = pl.BlockSpec((1, 2, 1, HEAD_D, HEAD_D), lambda s, h: (s, 0, h, 0, 0))
    so_spec, so_shape = _state_out(nseq, slot, nslots, 1)
    return _shared_out_call(
        functools.partial(_hgrn_kernel, t=t, c=c), [y_prev, s_prev],
        [col(0), col(4), col(8), col(12), col(16),
         pl.BlockSpec((2, HEAD_D), lambda s, h: (0, h)),
         pl.BlockSpec((1, HEAD_D), lambda s, h: (0, 0)),
         st_spec],
        (p_main, p_main, p_main, p_main, p_main, lb, norm_g.reshape(1, HEAD_D), s0),
        grid=(nseq, N_HEADS),
        out_specs=[pl.BlockSpec((t, HEAD_D), lambda s, h: (row_blk0 + s, h)), so_spec],
        out_shape=[jax.ShapeDtypeStruct((p_main.shape[0], BR_WIDTH), F32), so_shape],
        scratch_shapes=[seq_buf] * (6 if col_major else 1),
        compiler_params=_cp(("parallel", "parallel")),
        name="hgrn2",
    )


def _dot_b(a, b):
    return _dot(a.astype(BF16), b.astype(BF16))


def _tri_solve(ms, xs, c):
    ri = lax.broadcasted_iota(jnp.int32, (c, c), 0)
    ci = lax.broadcasted_iota(jnp.int32, (c, c), 1)
    eye = jnp.where(ri == ci, 1.0, 0.0)
    m8 = [jnp.where((ri >> 3) == (ci >> 3), m, 0.0) for m in ms]
    ts = [eye - m for m in m8]
    ps = [_dot_b(m, m) for m in m8]
    ts = [t + _dot_b(t, p) for t, p in zip(ts, ps)]
    ps = [_dot_b(p, p) for p in ps]
    ts = [t + _dot_b(t, p) for t, p in zip(ts, ps)]
    shift = 3
    while (1 << shift) < c:
        join = ((ri >> (shift + 1)) == (ci >> (shift + 1))) & ((ri >> shift) != (ci >> shift))
        lts = [_dot_b(jnp.where(join, m, 0.0), t) for m, t in zip(ms, ts)]
        ts = [t - _dot_b(t, lt) for t, lt in zip(ts, lts)]
        shift += 1
    return [_dot_b(t, x) for t, x in zip(ts, xs)]


def _gdn_kernel(cq_ref, ck_ref, cv_ref, cg_ref, ab_ref, cw_ref, alog_ref, dtb_ref, g_ref, s0_ref,
                y_ref, s_ref, q_s, k_s, v_s, la_s, be_s, w_s, qk_s, qe_s, kd_s, dec_s, o_acc, *perm_bufs,
                t, c, span, hg):
    stage = None
    if perm_bufs:
        *perm_bufs, stage = perm_bufs
        srcs = (cq_ref, ck_ref, cv_ref, cg_ref, ab_ref)
        for dst, src in zip(perm_bufs, srcs):
            _load_column_major(dst, src, t, stage)
        cq_ref, ck_ref, cv_ref, cg_ref, ab_ref = perm_bufs
    nchunk = t // c
    cps = span // c
    rowt = lax.broadcasted_iota(jnp.int32, (t, HEAD_D), 0)
    lane = lax.broadcasted_iota(jnp.int32, (t, LANES), 1)
    ri = lax.broadcasted_iota(jnp.int32, (span, span), 0)
    ci = lax.broadcasted_iota(jnp.int32, (span, span), 1)
    shift = int(math.log2(c))
    same = (ri >> shift) == (ci >> shift)
    heads = range(hg)
    dirs = (0, 1)

    def conv(x_ref, j, hh):
        x = x_ref[:, hh * HEAD_D:(hh + 1) * HEAD_D]
        w = cw_ref[j][:, hh * HEAD_D:(hh + 1) * HEAD_D]
        xm = jnp.where(rowt == 0, 0.0, pltpu.roll(x, 1, 0))
        xp = jnp.where(rowt == t - 1, 0.0, pltpu.roll(x, t - 1, 0))
        return xm * w[0:1] + x * w[1:2] + xp * w[2:3]

    ab = ab_ref[...]
    for hh in heads:
        h = pl.program_id(1) * hg + hh
        q = _silu(conv(cq_ref, 0, hh))
        q_s[hh] = q * lax.rsqrt(jnp.sum(q * q, axis=-1, keepdims=True) + EPS) * (HEAD_D ** -0.5)
        k = _silu(conv(ck_ref, 1, hh))
        k_s[hh] = k * lax.rsqrt(jnp.sum(k * k, axis=-1, keepdims=True) + EPS)
        v_s[hh] = _silu(conv(cv_ref, 2, hh))
        for d in dirs:
            a_col = jnp.sum(jnp.where(lane == d * N_HEADS + h, ab, 0.0), axis=-1, keepdims=True)
            b_col = jnp.sum(jnp.where(lane == 2 * N_HEADS + d * N_HEADS + h, ab, 0.0), axis=-1, keepdims=True)
            coef = -jnp.exp(jnp.full((1, LANES), alog_ref[d, h], F32))
            xs = a_col + dtb_ref[d, h]
            softplus = jnp.maximum(xs, 0.0) + jnp.log1p(jnp.exp(-jnp.abs(xs)))
            la_s[hh, d] = coef * jnp.broadcast_to(softplus, (t, LANES))
            be_s[hh, d] = jnp.broadcast_to(jax.nn.sigmoid(b_col), (t, LANES))
    o_acc[...] = jnp.zeros(o_acc.shape, F32)

    def wy_span(sp, carry):
        r0 = pl.multiple_of(sp * span, span)
        ms, xs, dst = [], [], []
        for hh in heads:
            qsp = q_s[hh, pl.ds(r0, span), :]
            ksp = k_s[hh, pl.ds(r0, span), :]
            vsp = v_s[hh, pl.ds(r0, span), :]
            kb = ksp.astype(BF16)
            kk = _dot_nt(kb, kb)
            qk = _dot_nt(qsp.astype(BF16), kb)
            for d in dirs:
                be = be_s[hh, d, pl.ds(r0, span), :]
                g = _chunk_cumsum(la_s[hh, d, pl.ds(r0, span), :], d, c)
                gt = g.T
                gcol = jnp.concatenate([g] * (span // LANES), axis=1)
                grow = jnp.concatenate([gt] * (span // LANES), axis=0)
                incl = same & ((ci <= ri) if d == 0 else (ci >= ri))
                gamma = jnp.where(incl, jnp.exp(jnp.minimum(gcol - grow, 0.0)), 0.0)
                bcol = jnp.concatenate([be] * (span // LANES), axis=1)
                m_sp = jnp.where(ci == ri, 0.0, bcol * kk * gamma)
                qkg = (qk * gamma).astype(BF16)
                eg = jnp.exp(g)
                xv = be * vsp
                xk = be * eg * ksp
                qe_s[hh, d, pl.ds(r0, span), :] = (qsp * eg).astype(BF16)
                for j in range(cps):
                    lo = j * c
                    rj = pl.multiple_of(r0 + lo, c)
                    ms.append(m_sp[lo:lo + c, lo:lo + c])
                    xs.append(jnp.concatenate([xv[lo:lo + c], xk[lo:lo + c]], axis=1))
                    dst.append((hh, d, rj))
                    qk_s[hh, d, pl.ds(rj, c), 0:c] = qkg[lo:lo + c, lo:lo + c]
                    gj = g[lo:lo + c]
                    glast = gj[c - 1:c] if d == 0 else gj[0:1]
                    kd_s[hh, d, pl.ds(rj, c), :] = (ksp[lo:lo + c] * jnp.exp(glast - gj)).astype(BF16)
                    dec_s[hh, d, pl.ds(sp * cps + j, 1), :] = jnp.exp(glast)
        for (hh, d, rj), w in zip(dst, _tri_solve(ms, xs, c)):
            w_s[hh, d, pl.ds(rj, c), :] = w
        return carry

    if t == span:
        wy_span(0, 0)
    else:
        lax.fori_loop(0, t // span, wy_span, 0)

    chains = [(hh, d) for hh in heads for d in dirs]

    def scan_step(i, carry):
        cidx = (i, nchunk - 1 - i)
        r0 = [pl.multiple_of(ck * c, c) for ck in cidx]
        sb = [s.astype(BF16) for s in carry]
        ws = [w_s[hh, d, pl.ds(r0[d], c), :] for hh, d in chains]
        wks = [_dot(w[:, HEAD_D:].astype(BF16), s) for w, s in zip(ws, sb)]
        oq = [_dot(qe_s[hh, d, pl.ds(r0[d], c), :], s) for (hh, d), s in zip(chains, sb)]
        vnb = [(w[:, :HEAD_D] - wk).astype(BF16) for w, wk in zip(ws, wks)]
        new = [s * dec_s[hh, d, pl.ds(cidx[d], 1), :] + _dot_tn(kd_s[hh, d, pl.ds(r0[d], c), :], v)
               for (hh, d), s, v in zip(chains, carry, vnb)]
        for (hh, d), o, v in zip(chains, oq, vnb):
            o_acc[hh, pl.ds(r0[d], c), :] += o + _dot(qk_s[hh, d, pl.ds(r0[d], c), 0:c], v)
        return tuple(new)

    finals = lax.fori_loop(0, nchunk, scan_step, tuple(s0_ref[0, d, hh] for hh, d in chains),
                           unroll=nchunk <= 4)
    for (hh, d), s in zip(chains, finals):
        s_ref[0, d, hh] = s
    ys = []
    for hh in heads:
        sl = slice(hh * HEAD_D, (hh + 1) * HEAD_D)
        ys.append(_rms(o_acc[hh], g_ref[...]) * _silu(cg_ref[:, sl]))
    y = jnp.concatenate(ys, axis=1)
    if perm_bufs:
        _store_row_major(y_ref, y, t, stage)
    else:
        y_ref[...] = y


def _gdn(p_main, p_ab, conv_w, a_log, dt_bias, norm_g, s0, row_blk0, nseq, t, y_prev=None, s_prev=None,
         slot=0, nslots=1, col_major=False):
    c = GDN_CHUNK
    hg = GDN_HEADS_PER_STEP if t * GDN_HEADS_PER_STEP <= GDN_MAX_ROWS_PER_STEP else GDN_HEADS_PER_STEP // 2
    wid = hg * HEAD_D
    blk = BR_WIDTH // wid

    def col(base):
        return pl.BlockSpec((t, wid), lambda s, h: (row_blk0 + s, base * blk + h))

    st_spec = pl.BlockSpec((1, 2, hg, HEAD_D, HEAD_D), lambda s, h: (s, 0, h, 0, 0))
    smem = pl.BlockSpec(memory_space=pltpu.SMEM)
    seq_buf = pltpu.VMEM((hg, t, HEAD_D), F32)
    dir_f32 = pltpu.VMEM((hg, 2, t, LANES), F32)
    dir_bf16 = pltpu.VMEM((hg, 2, t, LANES), BF16)
    scratch = [seq_buf, seq_buf, seq_buf, dir_f32, dir_f32,
               pltpu.VMEM((hg, 2, t, 2 * HEAD_D), F32), dir_bf16, dir_bf16, dir_bf16,
               pltpu.VMEM((hg, 2, max(t // c, 8), LANES), F32), seq_buf]
    if col_major:
        scratch += [pltpu.VMEM((t, wid), F32)] * 4 + [pltpu.VMEM((t, LANES), F32)] * 2
    so_spec, so_shape = _state_out(nseq, slot, nslots, hg)
    return _shared_out_call(
        functools.partial(_gdn_kernel, t=t, c=c, span=GDN_SPAN, hg=hg), [y_prev, s_prev],
        [col(6), col(7), col(8), col(9),
         pl.BlockSpec((t, LANES), lambda s, h: (row_blk0 + s, 0)),
         pl.BlockSpec((3, 3, wid), lambda s, h: (0, 0, h)),
         smem, smem,
         pl.BlockSpec((1, HEAD_D), lambda s, h: (0, 0)),
         st_spec],
        (p_main, p_main, p_main, p_main, p_ab, conv_w, a_log, dt_bias, norm_g.reshape(1, HEAD_D), s0),
        grid=(nseq, N_HEADS // hg),
        out_specs=[pl.BlockSpec((t, wid), lambda s, h: (row_blk0 + s, h)), so_spec],
        out_shape=[jax.ShapeDtypeStruct((p_main.shape[0], BR_WIDTH), F32), so_shape],
        scratch_shapes=scratch,
        compiler_params=_cp(("parallel", "parallel")),
        name="gdn",
    )


def _cmul(ar, ai, br, bi):
    return ar * br - ai * bi, ar * bi + ai * br


def _s5_kernel(u_ref, bb_ref, cc_ref, lam_ref, dsk_ref, s0_ref, y_ref, sf_ref, up_s, x_s, yp_s,
               *, tb, seg, segs_per_seq, zero_init, col_major):
    r_par = tb // seg
    half = S5_GB * S5_STATE
    if col_major:
        rows = tb // GRID_W
        assert segs_per_seq == r_par and seg % rows == 0

        def step_rows(r):
            return pl.ds((r % rows) * GRID_W + r // rows, r_par, stride=seg // rows)
    else:
        def step_rows(r):
            return pl.ds(r, r_par, stride=seg)

    for r in range(seg):
        up_s[r * r_par:(r + 1) * r_par, :] = u_ref[step_rows(r), :]
    ub = up_s[...].astype(BF16)
    sf_ref[...] = jnp.zeros(sf_ref.shape, F32)
    nseq = r_par // segs_per_seq
    lam = [(lam_ref[0, d, 0:1, :], lam_ref[0, d, 1:2, :]) for d in (0, 1)]
    for d in (0, 1):
        x_s[d] = _dot(ub, bb_ref[0, d])

    def offsets(i):
        return [pl.multiple_of(r * r_par, r_par) for r in (i, seg - 1 - i)]

    def local_step(i, carry):
        off = offsets(i)
        new = []
        for d in (0, 1):
            (lre, lim), (hre, him) = lam[d], carry[d]
            nre = lre * hre - lim * him + x_s[d, pl.ds(off[d], r_par), 0:half]
            nim = lre * him + lim * hre + x_s[d, pl.ds(off[d], r_par), half:2 * half]
            x_s[d, pl.ds(off[d], r_par), 0:half] = nre
            x_s[d, pl.ds(off[d], r_par), half:2 * half] = nim
            new.append((nre, nim))
        return tuple(new)

    zero = jnp.zeros((r_par, half), F32)
    lax.fori_loop(0, seg, local_step, ((zero, zero), (zero, zero)))

    end_rows = ((seg - 1) * r_par, 0)
    needs_carry = not (zero_init and segs_per_seq == 1)
    hins = []
    for d in (0, 1):
        if not needs_carry:
            break
        lre, lim = lam[d]
        pre, pim = lre, lim
        n = 1
        while n < seg:
            pre, pim = _cmul(pre, pim, pre, pim)
            n *= 2
        end_row = end_rows[d]
        ere = x_s[d, end_row:end_row + r_par, 0:half]
        eim = x_s[d, end_row:end_row + r_par, half:2 * half]
        hin = [None] * r_par
        order = range(r_par) if d == 0 else range(r_par - 1, -1, -1)
        for j in order:
            sq, pos = divmod(j, segs_per_seq)
            first = pos == (0 if d == 0 else segs_per_seq - 1)
            if first:
                hin[j] = (s0_ref[0, 0, d, sq:sq + 1, 0:half], s0_ref[0, 0, d, sq:sq + 1, half:2 * half])
            else:
                pj = j - 1 if d == 0 else j + 1
                cre, cim = _cmul(pre, pim, hin[pj][0], hin[pj][1])
                hin[j] = (cre + ere[pj:pj + 1], cim + eim[pj:pj + 1])
        hins.append((jnp.concatenate([hj[0] for hj in hin], axis=0),
                     jnp.concatenate([hj[1] for hj in hin], axis=0)))

    def carry_step(i, pws):
        off = offsets(i)
        new = []
        for d in (0, 1):
            (lre, lim), (wre, wim) = lam[d], pws[d]
            are, aim = _cmul(wre, wim, hins[d][0], hins[d][1])
            x_s[d, pl.ds(off[d], r_par), 0:half] += are
            x_s[d, pl.ds(off[d], r_par), half:2 * half] += aim
            new.append(_cmul(wre, wim, lre, lim))
        return tuple(new)

    if needs_carry:
        lax.fori_loop(0, seg, carry_step, (lam[0], lam[1]))

    for d in (0, 1):
        for sq in range(nseq):
            j = (sq + 1) * segs_per_seq - 1 if d == 0 else sq * segs_per_seq
            row = end_rows[d] + j
            sf_ref[0, 0, d, sq:sq + 1, :] = x_s[d, row:row + 1, :]
    yp_s[...] = (_dot(x_s[0].astype(BF16), cc_ref[0, 0]) + _dot(x_s[1].astype(BF16), cc_ref[0, 1])
                 + up_s[...] * dsk_ref[...])
    for r in range(seg):
        y_ref[step_rows(r), :] = yp_s[r * r_par:(r + 1) * r_par, :]


def _s5(p_main, bb, cc, lam, dskip, s0, row_blk0, nblk, tb, t, seg, zero_init, y_prev=None, col_major=False):
    ngb = S5_GROUPS // S5_GB
    w2 = 2 * S5_GB * S5_STATE
    return _shared_out_call(
        functools.partial(_s5_kernel, tb=tb, seg=seg, segs_per_seq=t // seg, zero_init=zero_init,
                          col_major=col_major), [y_prev, None],
        [
            pl.BlockSpec((tb, LANES), lambda i, gb: (row_blk0 + i, 20 + gb)),
            pl.BlockSpec((1, 2, LANES, w2), lambda i, gb: (gb, 0, 0, 0)),
            pl.BlockSpec((1, 2, w2, LANES), lambda i, gb: (gb, 0, 0, 0)),
            pl.BlockSpec((1, 2, 2, w2 // 2), lambda i, gb: (gb, 0, 0, 0)),
            pl.BlockSpec((1, LANES), lambda i, gb: (0, gb)),
            pl.BlockSpec((1, 1, 2, 8, w2), lambda i, gb: (i, gb, 0, 0, 0)),
        ],
        (p_main, bb, cc, lam, dskip.reshape(1, BR_WIDTH), s0),
        grid=(nblk, ngb),
        out_specs=[pl.BlockSpec((tb, LANES), lambda i, gb: (row_blk0 + i, gb)),
                   pl.BlockSpec((1, 1, 2, 8, w2), lambda i, gb: (i, gb, 0, 0, 0))],
        out_shape=[jax.ShapeDtypeStruct((p_main.shape[0], BR_WIDTH), F32),
                   jax.ShapeDtypeStruct((nblk, ngb, 2, 8, w2), F32)],
        scratch_shapes=[pltpu.VMEM((tb, LANES), F32), pltpu.VMEM((2, tb, w2), F32), pltpu.VMEM((tb, LANES), F32)],
        compiler_params=_cp(("parallel", "parallel")),
        name="s5",
    )


def _s5_params(a_re, a_im, log_dt, b_re, b_im, c_re, c_im):
    dt = jnp.exp(log_dt)[..., None]
    mag = jnp.exp(dt * a_re)
    lam_re = mag * jnp.cos(dt * a_im)
    lam_im = mag * jnp.sin(dt * a_im)
    den = a_re * a_re + a_im * a_im
    co_re = ((lam_re - 1.0) * a_re + lam_im * a_im) / den
    co_im = (lam_im * a_re - (lam_re - 1.0) * a_im) / den
    bb_re = co_re[..., None] * b_re - co_im[..., None] * b_im
    bb_im = co_re[..., None] * b_im + co_im[..., None] * b_re
    ngb = S5_GROUPS // S5_GB
    eye = jnp.eye(S5_GB, dtype=F32)

    def pack_in(w):
        w = w.reshape(2, ngb, S5_GB, S5_STATE, S5_GROUP)
        blk = jnp.einsum('dbgpi,gh->bdgihp', w, eye)
        return blk.reshape(ngb, 2, S5_GB * S5_GROUP, S5_GB * S5_STATE)

    def pack_out(w):
        w = w.reshape(2, ngb, S5_GB, S5_GROUP, S5_STATE)
        blk = jnp.einsum('dbgip,gh->bdgphi', w, eye)
        return blk.reshape(ngb, 2, S5_GB * S5_STATE, S5_GB * S5_GROUP)

    bb = jnp.concatenate([pack_in(bb_re), pack_in(bb_im)], axis=-1).astype(BF16)
    cc = jnp.concatenate([pack_out(c_re), -pack_out(c_im)], axis=-2).astype(BF16)
    lam = jnp.stack([lam_re, lam_im], axis=1)
    lam = lam.reshape(2, 2, ngb, S5_GB * S5_STATE).transpose(2, 0, 1, 3)
    return bb, cc, lam


def _glu_kernel(y_ref, w_ref, b_ref, o_ref, wb_ref):
    @pl.when(pl.program_id(0) == 0)
    def _():
        wb_ref[...] = w_ref[...].astype(BF16)

    z = _dot(y_ref[...].astype(BF16), wb_ref[...]) + b_ref[...]
    half = z.shape[1] // 2
    o_ref[...] = z[:, :half] * jax.nn.sigmoid(z[:, half:])


def _glu(y, w, b, l):
    n, k = y.shape
    n2 = w.shape[-1]
    return pl.pallas_call(
        _glu_kernel,
        grid=(n // TM,),
        in_specs=[pl.BlockSpec((TM, k), lambda i: (i, 0)),
                  pl.BlockSpec((None, k, n2), lambda i: (l, 0, 0)),
                  pl.BlockSpec((None, 1, n2), lambda i: (l, 0, 0))],
        out_specs=pl.BlockSpec((TM, n2 // 2), lambda i: (i, 0)),
        out_shape=jax.ShapeDtypeStruct((n, n2 // 2), F32),
        scratch_shapes=[pltpu.VMEM((k, n2), BF16)],
        compiler_params=_cp(("arbitrary",)),
        name="s5_glu",
    )(y, w, b.reshape(b.shape[0], 1, n2))


def _merge_kernel(ya_ref, yb_ref, yc_ref, mg_ref, wbr_ref, wout_ref, g_ref, o_ref, wbr_s, wout_s):
    @pl.when(pl.program_id(0) == 0)
    def _():
        wbr_s[...] = wbr_ref[...].astype(BF16)
        wout_s[...] = wout_ref[...].astype(BF16)

    d = wout_s.shape[0]
    acc = None
    for n, br_ref in enumerate((ya_ref, yb_ref, yc_ref)):
        proj = _dot(br_ref[...].astype(BF16), wbr_s[n])
        term = proj * jax.nn.sigmoid(mg_ref[:, n * d:(n + 1) * d])
        acc = term if acc is None else acc + term
    y = _dot(acc.astype(BF16), wout_s[...])
    o_ref[...] = _rms(y, g_ref[...])


def _merge(ya, yb, yc, mg, w_branch, w_out, g, l):
    n = ya.shape[0]
    d = w_out.shape[-1]
    tm = 256
    br = pl.BlockSpec((tm, BR_WIDTH), lambda i: (i, 0))
    return pl.pallas_call(
        _merge_kernel,
        grid=(n // tm,),
        in_specs=[br, br, br,
                  pl.BlockSpec((tm, 3 * d), lambda i: (i, 0)),
                  pl.BlockSpec((None, 3, BR_WIDTH, d), lambda i: (l, 0, 0, 0)),
                  pl.BlockSpec((None, d, d), lambda i: (l, 0, 0)),
                  pl.BlockSpec((1, d), lambda i: (0, 0))],
        out_specs=pl.BlockSpec((tm, d), lambda i: (i, 0)),
        out_shape=jax.ShapeDtypeStruct((n, d), F32),
        scratch_shapes=[pltpu.VMEM((3, BR_WIDTH, d), BF16), pltpu.VMEM((d, d), BF16)],
        compiler_params=_cp(("arbitrary",)),
        name="merge",
    )(ya, yb, yc, mg, w_branch, w_out, g.reshape(1, d))


def _swiglu_kernel(*refs, gated):
    if gated:
        x_ref, wg_ref, wu_ref, gt_ref, o_ref, wg_s, wu_s = refs
    else:
        x_ref, wg_ref, wu_ref, o_ref, wg_s, wu_s = refs

    @pl.when(pl.program_id(1) == 0)
    def _():
        wg_s[...] = wg_ref[...].astype(BF16)
        wu_s[...] = wu_ref[...].astype(BF16)

    x = x_ref[...]
    hid = _silu(_dot(x, wg_s[...])) * _dot(x, wu_s[...])
    if gated:
        gt = gt_ref[...]
        lane = lax.broadcasted_iota(jnp.int32, gt.shape, 1)
        hid = hid * jnp.sum(jnp.where(lane == pl.program_id(0), gt, 0.0), axis=-1, keepdims=True)
    o_ref[...] = hid.astype(BF16)


def _swiglu_up(x, wg, wu, e, f, gates=None):
    n, d = x.shape
    gated = gates is not None
    if gated:
        ne = wg.shape[1]
        wspec = pl.BlockSpec((None, None, d, f), lambda j, i: (e, j, 0, 0))
    else:
        ne = wg.shape[-1] // f
        wspec = pl.BlockSpec((None, d, f), lambda j, i: (e, 0, j))
    in_specs = [pl.BlockSpec((TM, d), lambda j, i: (i, 0)), wspec, wspec]
    args = [x, wg, wu]
    if gated:
        in_specs.append(pl.BlockSpec((TM, LANES), lambda j, i: (i, 0)))
        args.append(gates)
    return pl.pallas_call(
        functools.partial(_swiglu_kernel, gated=gated),
        grid=(ne, n // TM),
        in_specs=in_specs,
        out_specs=pl.BlockSpec((TM, f), lambda j, i: (i, j)),
        out_shape=jax.ShapeDtypeStruct((n, ne * f), BF16),
        scratch_shapes=[pltpu.VMEM((d, f), BF16), pltpu.VMEM((d, f), BF16)],
        compiler_params=_cp(("arbitrary", "arbitrary")),
        name="swiglu_up",
    )(*args)


def _down_kernel(x_ref, w_ref, g_ref, o_ref, acc):
    k = pl.program_id(1)
    part = _dot(x_ref[...], w_ref[...].astype(BF16))

    @pl.when(k == 0)
    def _():
        acc[...] = part

    @pl.when(k > 0)
    def _():
        acc[...] += part

    @pl.when(k == pl.num_programs(1) - 1)
    def _():
        o_ref[...] = _rms(acc[...], g_ref[...])


def _down_norm(hid, wd, e, g, tm):
    n = hid.shape[0]
    _, ne, f, d = wd.shape
    return pl.pallas_call(
        _down_kernel,
        grid=(n // tm, ne),
        in_specs=[pl.BlockSpec((tm, f), lambda i, k: (i, k)),
                  pl.BlockSpec((None, None, f, d), lambda i, k: (e, k, 0, 0)),
                  pl.BlockSpec((1, d), lambda i, k: (0, 0))],
        out_specs=pl.BlockSpec((tm, d), lambda i, k: (i, 0)),
        out_shape=jax.ShapeDtypeStruct((n, d), F32),
        scratch_shapes=[pltpu.VMEM((tm, d), F32)],
        compiler_params=_cp(("parallel", "arbitrary")),
        name="down_norm",
    )(hid, wd, g.reshape(1, d))


def _router_kernel(x_ref, w_ref, aux_ref, cnt_ref, base_s, *, ne):
    @pl.when(pl.program_id(0) == 0)
    def _():
        base_s[...] = jnp.zeros(base_s.shape, F32)

    w = w_ref[...]
    w_hi = w.astype(BF16)
    w_lo = (w - w_hi.astype(F32)).astype(BF16)
    x = x_ref[...].astype(BF16)
    logits = _dot(x, w_hi) + _dot(x, w_lo)
    tm = logits.shape[0]
    lane = lax.broadcasted_iota(jnp.int32, logits.shape, 1).astype(F32)
    neg = jnp.float32(-jnp.inf)
    logits = jnp.where(lane < ne, logits, neg)
    m1 = jnp.max(logits, axis=-1, keepdims=True)
    i1 = jnp.min(jnp.where(logits == m1, lane, float(LANES)), axis=-1, keepdims=True)
    rest = jnp.where(lane == i1, neg, logits)
    m2 = jnp.max(rest, axis=-1, keepdims=True)
    i2 = jnp.min(jnp.where(rest == m2, lane, float(LANES)), axis=-1, keepdims=True)
    e2 = jnp.exp(m2 - m1)
    w1 = 1.0 / (1.0 + e2)
    w2 = e2 / (1.0 + e2)
    sel = jnp.where((lane == i1) | (lane == i2), 1.0, 0.0)
    ri = lax.broadcasted_iota(jnp.int32, (tm, tm), 0)
    ci = lax.broadcasted_iota(jnp.int32, (tm, tm), 1)
    before = jnp.where(ci < ri, 1.0, 0.0).astype(BF16)
    rank = _dot(before, sel.astype(BF16)) + base_s[0:1, :]
    r1 = jnp.sum(jnp.where(lane == i1, rank, 0.0), axis=-1, keepdims=True)
    r2 = jnp.sum(jnp.where(lane == i2, rank, 0.0), axis=-1, keepdims=True)
    aux = jnp.zeros(logits.shape, F32)
    for k, v in enumerate((i1, i2, r1, r2, w1, w2)):
        aux = jnp.where(lane == float(k), v, aux)
    aux_ref[...] = aux
    base_s[...] = base_s[...] + jnp.sum(sel, axis=0, keepdims=True)
    cnt_ref[...] = base_s[...]


def _router(x, w_pad, ne):
    n, d = x.shape
    return pl.pallas_call(
        functools.partial(_router_kernel, ne=ne),
        grid=(n // TM,),
        in_specs=[pl.BlockSpec((TM, d), lambda i: (i, 0)),
                  pl.BlockSpec((d, LANES), lambda i: (0, 0))],
        out_specs=[pl.BlockSpec((TM, LANES), lambda i: (i, 0)),
                   pl.BlockSpec((8, LANES), lambda i: (0, 0))],
        out_shape=[jax.ShapeDtypeStruct((n, LANES), F32), jax.ShapeDtypeStruct((8, LANES), F32)],
        scratch_shapes=[pltpu.VMEM((8, LANES), F32)],
        compiler_params=_cp(("arbitrary",)),
        name="router",
    )(x, w_pad)


def _moe_plan(aux, cnt, n, ne, tile, max_tiles):
    e = aux[:, 0:2].astype(jnp.int32)
    r = aux[:, 2:4].astype(jnp.int32)
    c = cnt[0, :ne].astype(jnp.int32)
    pc = (c + tile - 1) // tile * tile
    ends = jnp.cumsum(pc)
    slot = ((ends - pc)[e] + r).reshape(-1)
    tok = jnp.broadcast_to(jnp.arange(n, dtype=F32)[:, None], (n, 2)).reshape(-1)
    m = max_tiles * tile
    pad = jnp.broadcast_to(jnp.array([float(n), 0.0], F32), (m, 2))
    table = pad.at[slot].set(jnp.stack([tok, aux[:, 4:6].reshape(-1)], axis=1))
    slot_tok = table[:, 0].astype(jnp.int32)
    slot_gate = table[:, 1]
    tstart = jnp.arange(max_tiles, dtype=jnp.int32) * tile
    tile_e = jnp.minimum(jnp.sum(tstart[:, None] >= ends[None, :], axis=1), ne - 1).astype(jnp.int32)
    tile_valid = (tstart < ends[-1]).astype(jnp.int32)
    return slot_tok.reshape(max_tiles, 1, tile), slot_gate.reshape(max_tiles, 1, tile), tile_e, tile_valid


def _expert_changed(t, te_ref):
    return jnp.logical_or(t == 0, te_ref[t] != te_ref[jnp.maximum(t - 1, 0)])


def _moe_up_kernel(te_ref, tv_ref, idx_ref, nxt_ref, h_ref, wg_ref, wu_ref, o_ref, xbuf, wg_s, wu_s, *, tile):
    t = pl.program_id(0)
    valid = tv_ref[t] == 1
    last = h_ref.shape[0] - 1
    cur = t % 2

    def gather_row(src_idx_ref, buf, j):
        xbuf[buf, pl.ds(j, 1), :] = h_ref[pl.ds(jnp.minimum(src_idx_ref[0, 0, j], last), 1), :]

    @pl.when(t == 0)
    def _():
        def first(j, carry):
            gather_row(idx_ref, 0, j)
            return carry

        lax.fori_loop(0, tile, first, 0, unroll=8)

    @pl.when(jnp.logical_and(valid, _expert_changed(t, te_ref)))
    def _():
        wg_s[...] = wg_ref[...].astype(BF16)
        wu_s[...] = wu_ref[...].astype(BF16)

    @pl.when(valid)
    def _():
        x = xbuf[cur].astype(BF16)
        o_ref[...] = (_silu(_dot(x, wg_s[...])) * _dot(x, wu_s[...])).astype(BF16)
        for j in range(tile):
            gather_row(nxt_ref, 1 - cur, j)

    @pl.when(jnp.logical_not(valid))
    def _():
        o_ref[...] = jnp.zeros(o_ref.shape, BF16)


def _moe_up(h, slot_tok, tile_e, tile_valid, wg, wu, e_layer, tile):
    max_tiles = slot_tok.shape[0]
    _, _, d, f = wg.shape
    wspec = pl.BlockSpec((None, None, d, f), lambda t, te, tv: (e_layer, te[t], 0, 0),
                         pipeline_mode=pl.Buffered(1))
    grid_spec = pltpu.PrefetchScalarGridSpec(
        num_scalar_prefetch=2,
        grid=(max_tiles,),
        in_specs=[pl.BlockSpec((1, 1, tile), lambda t, te, tv: (t, 0, 0), memory_space=pltpu.SMEM),
                  pl.BlockSpec((1, 1, tile), lambda t, te, tv: (jnp.minimum(t + 1, max_tiles - 1), 0, 0),
                               memory_space=pltpu.SMEM),
                  pl.BlockSpec(h.shape, lambda t, te, tv: (0, 0), pipeline_mode=pl.Buffered(1)),
                  wspec, wspec],
        out_specs=pl.BlockSpec((tile, f), lambda t, te, tv: (t, 0)),
        scratch_shapes=[pltpu.VMEM((2, tile, d), F32),
                        pltpu.VMEM((d, f), BF16), pltpu.VMEM((d, f), BF16)],
    )
    return pl.pallas_call(
        functools.partial(_moe_up_kernel, tile=tile),
        grid_spec=grid_spec,
        out_shape=jax.ShapeDtypeStruct((max_tiles * tile, f), BF16),
        compiler_params=_cp(("arbitrary",)),
        name="moe_up",
    )(tile_e, tile_valid, slot_tok, slot_tok, h, wg, wu)


def _moe_down_kernel(te_ref, tv_ref, idx_ref, gt_ref, hid_ref, wd_ref, g_ref, o_ref, acc, ybuf, tbuf, wd_s,
                     *, tile, n_tiles, tm_out):
    t = pl.program_id(0)
    tt = jnp.minimum(t, n_tiles - 1)

    @pl.when(t == 0)
    def _():
        acc[...] = jnp.zeros(acc.shape, F32)

    is_tile = jnp.logical_and(t < n_tiles, tv_ref[tt] == 1)

    @pl.when(jnp.logical_and(is_tile, _expert_changed(tt, te_ref)))
    def _():
        wd_s[...] = wd_ref[...].astype(BF16)

    @pl.when(is_tile)
    def _():
        for j in range(tile):
            tbuf[j:j + 1, :] = acc[pl.ds(idx_ref[0, 0, j], 1), :]
        ybuf[...] = _dot(hid_ref[...], wd_s[...])

        def put(j, carry):
            acc[pl.ds(idx_ref[0, 0, j], 1), :] = tbuf[pl.ds(j, 1), :] + gt_ref[0, 0, j] * ybuf[pl.ds(j, 1), :]
            return carry

        lax.fori_loop(0, tile, put, 0, unroll=8)

    @pl.when(t >= n_tiles)
    def _():
        r0 = pl.multiple_of((t - n_tiles) * tm_out, tm_out)
        o_ref[...] = _rms(acc[pl.ds(r0, tm_out), :], g_ref[...])


def _moe_down(hid, slot_tok, slot_gate, tile_e, tile_valid, wd, e_layer, g, n, tile):
    n_tiles = slot_tok.shape[0]
    _, _, f, d = wd.shape
    tm_out = TM
    n_flush = n // tm_out

    def tix(t):
        return jnp.minimum(t, n_tiles - 1)

    smem_tile = pl.BlockSpec((1, 1, tile), lambda t, te, tv: (tix(t), 0, 0), memory_space=pltpu.SMEM)
    grid_spec = pltpu.PrefetchScalarGridSpec(
        num_scalar_prefetch=2,
        grid=(n_tiles + n_flush,),
        in_specs=[smem_tile, smem_tile,
                  pl.BlockSpec((tile, f), lambda t, te, tv: (tix(t), 0)),
                  pl.BlockSpec((None, None, f, d), lambda t, te, tv: (e_layer, te[tix(t)], 0, 0)),
                  pl.BlockSpec((1, d), lambda t, te, tv: (0, 0))],
        out_specs=pl.BlockSpec((tm_out, d), lambda t, te, tv: (jnp.maximum(t - n_tiles, 0), 0)),
        scratch_shapes=[pltpu.VMEM((n + 8, d), F32), pltpu.VMEM((tile, d), F32), pltpu.VMEM((tile, d), F32),
                        pltpu.VMEM((f, d), BF16)],
    )
    return pl.pallas_call(
        functools.partial(_moe_down_kernel, tile=tile, n_tiles=n_tiles, tm_out=tm_out),
        grid_spec=grid_spec,
        out_shape=jax.ShapeDtypeStruct((n, d), F32),
        compiler_params=_cp(("arbitrary",)),
        name="moe_down",
    )(tile_e, tile_valid, slot_tok, slot_gate, hid, wd, g.reshape(1, d))


def kernel(x_prompt, x_sample, state_hgrn, state_s5, state_gdn, c, c_ctx, ada_w, ada_b, norm_g, w_in, hgrn_lb, hgrn_norm, s5_a_re, s5_a_im, s5_log_dt, s5_b_re, s5_b_im, s5_c_re, s5_c_im, s5_d, s5_glu_w, s5_glu_b, gdn_conv, gdn_a_log, gdn_dt_bias, gdn_norm, w_branch, w_out, ffn_wg, ffn_wu, ffn_wd, moe_router, moe_wg, moe_wu, moe_wd):
    bp, tp, d = x_prompt.shape
    bs, ts, _ = x_sample.shape
    depth = w_in.shape[0]
    n_p, n_s = bp * tp, bs * ts
    ne = moe_router.shape[-1]
    main_cols = 10 * BR_WIDTH
    ab_cols = 4 * N_HEADS

    c8 = jnp.zeros((8, d), F32).at[0].set(c_ctx).at[1:1 + bs].set(c)
    mod_all = _ada_mod(c8, ada_w, ada_b)

    lb_soft = jax.nn.softmax(hgrn_lb.astype(F32), axis=0)
    lb_all = jnp.cumsum(lb_soft, axis=0) - lb_soft[:1]

    mods = mod_all[:, :1 + bs].reshape(depth, 1 + bs, 6, d)
    zero_row = jnp.zeros((1 + bs, d), F32)

    def modv(gate, scale, shift, g):
        rows = [zero_row if sel is None else mods[sel[0], :, sel[1]] for sel in (gate, scale, shift)]
        rows.append(zero_row if g is None else jnp.broadcast_to(g, (1 + bs, d)))
        return jnp.stack(rows + [zero_row] * 4, axis=1)

    x = jnp.concatenate([x_prompt.reshape(n_p, d), x_sample.reshape(n_s, d)], axis=0)
    (h,) = _resnorm(x, None, modv(None, (0, 1), (0, 0), norm_g[0, 0]), n_p, ts, True)

    z_hgrn = jnp.zeros((bp, 2, N_HEADS, HEAD_D, HEAD_D), F32)
    ngb = S5_GROUPS // S5_GB
    w2 = 2 * S5_GB * S5_STATE
    st_s5 = []
    y_zero = jnp.zeros((n_p + n_s, BR_WIDTH), F32)
    new_hgrn = new_gdn = jnp.zeros((bp, depth, 2, N_HEADS, HEAD_D, HEAD_D), F32)
    w_in_t = jnp.swapaxes(w_in, 1, 2)
    s5_tb = 1024

    for l in range(depth):
        col_major = l % 2 == 1
        p_main = _mm_t(h, w_in_t, l, main_cols, 2 * BR_WIDTH, "w_in_main", tm=1024)
        w_ab = jnp.pad(w_in_t[l, main_cols:main_cols + ab_cols], ((0, LANES - ab_cols), (0, 0)))
        p_ab = _mm_t(h, w_ab, None, LANES, LANES, "w_in_ab", tm=1024)
        p_mg = _mm_t(h, w_in_t[l, main_cols + ab_cols:], None, 3 * d, 2 * BR_WIDTH, "w_in_merge", tm=1024)

        ya, new_hgrn = _hgrn(p_main, lb_all[l], hgrn_norm[l], z_hgrn, 0, bp, tp, y_zero, new_hgrn, l, depth)
        ya, _ = _hgrn(p_main, lb_all[l], hgrn_norm[l], state_hgrn[:, l].astype(F32), n_p // ts, bs, ts, ya,
                      col_major=col_major)

        bb, cc, lam = _s5_params(s5_a_re[l], s5_a_im[l], s5_log_dt[l], s5_b_re[l], s5_b_im[l],
                                 s5_c_re[l], s5_c_im[l])
        seq_per_blk = 8
        tb_p = seq_per_blk * tp
        nblk_p = n_p // tb_p
        yb, sb = _s5(p_main, bb, cc, lam, s5_d[l], jnp.zeros((nblk_p, ngb, 2, 8, w2), F32),
                     0, nblk_p, tb_p, tp, tp, True, y_zero)
        s0s = state_s5[:, l].astype(F32)
        s0s = s0s.reshape(bs, 2, ngb, S5_GB * S5_STATE, 2).transpose(0, 2, 1, 4, 3).reshape(bs, ngb, 2, 1, w2)
        s0s = jnp.pad(s0s, ((0, 0), (0, 0), (0, 0), (0, 7), (0, 0)))
        yb, _ = _s5(p_main, bb, cc, lam, s5_d[l], s0s, n_p // s5_tb, n_s // s5_tb, s5_tb, ts, S5_SEG, False, yb,
                    col_major=col_major)
        yb = _glu(yb, s5_glu_w, s5_glu_b, l)
        sb = sb[:, :, :, :seq_per_blk].reshape(nblk_p, ngb, 2, seq_per_blk, 2, S5_GB, S5_STATE)
        sb = sb.transpose(0, 3, 2, 1, 5, 6, 4).reshape(bp, 2, S5_GROUPS, S5_STATE, 2)
        st_s5.append(sb)

        yc, new_gdn = _gdn(p_main, p_ab, gdn_conv[l], gdn_a_log[l], gdn_dt_bias[l], gdn_norm[l], z_hgrn, 0, bp, tp,
                           y_zero, new_gdn, l, depth)
        yc, _ = _gdn(p_main, p_ab, gdn_conv[l], gdn_a_log[l], gdn_dt_bias[l], gdn_norm[l],
                     state_gdn[:, l].astype(F32), n_p // ts, bs, ts, yc, col_major=col_major)

        r_m = _merge(ya, yb, yc, p_mg, w_branch, w_out, norm_g[l, 1], l)
        x, h2 = _resnorm(x, r_m, modv((l, 2), (l, 4), (l, 3), norm_g[l, 2]), n_p, ts, True,
                         h_dtype=BF16 if l % 2 == 0 else F32)

        if l % 2 == 0:
            e = l // 2
            hid = _swiglu_up(h2, ffn_wg, ffn_wu, e, ffn_wg.shape[-1] // 2)
            r_f = _down_norm(hid, ffn_wd[:, None], e, norm_g[l, 3], TM)
        else:
            e = l // 2
            n_tok = n_p + n_s
            max_tiles = 2 * n_tok // MOE_TILE + ne
            aux, cnt = _router(h2, jnp.pad(moe_router[e], ((0, 0), (0, LANES - ne))), ne)
            slot_tok, slot_gate, tile_e, tile_valid = _moe_plan(aux, cnt, n_tok, ne, MOE_TILE, max_tiles)
            hid = _moe_up(h2, slot_tok, tile_e, tile_valid, moe_wg, moe_wu, e, MOE_TILE)
            r_f = _moe_down(hid, slot_tok, slot_gate, tile_e, tile_valid, moe_wd, e, norm_g[l, 3], n_tok, MOE_TILE)

        if l + 1 < depth:
            x, h = _resnorm(x, r_f, modv((l, 5), (l + 1, 1), (l + 1, 0), norm_g[l + 1, 0]), n_p, ts, True)
        else:
            mv = modv((l, 5), None, None, None)
            (y_prompt,) = _resnorm(x, r_f, mv, n_p, ts, False, rows=(0, n_p))
            (y_sample,) = _resnorm(x, r_f, mv, n_p, ts, False, rows=(n_p, n_s))

    y_prompt = y_prompt.reshape(bp, tp, d)
    y_sample = y_sample.reshape(bs, ts, d)
    new_s5 = jnp.stack(st_s5, axis=1).astype(x_prompt.dtype)
    return (y_prompt, y_sample, new_hgrn.astype(x_prompt.dtype), new_s5, new_gdn.astype(x_prompt.dtype))
```

```python
import functools
import math

import jax
import jax.numpy as jnp
from jax import lax
from jax.experimental import pallas as pl
from jax.experimental.pallas import tpu as pltpu

F32 = jnp.float32
BF16 = jnp.bfloat16

EPS = 1e-6
F_FLOOR = 1e-6
LANES = 128
GRID_W = 64
HEAD_D = 128
N_HEADS = 4
BR_WIDTH = 512
S5_GROUPS = 32
S5_GROUP = 16
S5_STATE = 64
S5_GB = 8
S5_SEG = 64
GLA_CHUNK = 256
GDN_CHUNK = 64
GDN_SPAN = 256
GDN_HEADS_PER_STEP = 4
GDN_MAX_ROWS_PER_STEP = 2048
TM = 512
MOE_TILE = 256
VMEM_LIMIT = 56 * 1024 * 1024


def _cp(sem, vmem=VMEM_LIMIT):
    return pltpu.CompilerParams(dimension_semantics=sem, vmem_limit_bytes=vmem)


def _dot(a, b):
    return jnp.dot(a, b, preferred_element_type=F32)


def _dot_nt(a, b):
    return lax.dot_general(a, b, (((1,), (1,)), ((), ())), preferred_element_type=F32)


def _dot_tn(a, b):
    return lax.dot_general(a, b, (((0,), (0,)), ((), ())), preferred_element_type=F32)


def _silu(x):
    return x * jax.nn.sigmoid(x)


def _rms(x, g):
    return x * lax.rsqrt(jnp.mean(x * x, axis=-1, keepdims=True) + EPS) * g


def _chunk_cumsum(x, d, c):
    n = x.shape[0]
    pos = lax.broadcasted_iota(jnp.int32, x.shape, 0) & (c - 1)
    s = 1
    while s < c:
        if d == 0:
            x = x + jnp.where(pos >= s, pltpu.roll(x, s, 0), 0.0)
        else:
            x = x + jnp.where(pos < c - s, pltpu.roll(x, n - s, 0), 0.0)
        s *= 2
    return x


def _ada_kernel(c_ref, w_ref, b_ref, o_ref):
    c = c_ref[...]
    a = _silu(c).astype(BF16)
    o_ref[...] = _dot(a, w_ref[...].astype(BF16)) + b_ref[...]


def _ada_mod(c8, ada_w, ada_b):
    depth, d, n6 = ada_w.shape
    tn = 512
    return pl.pallas_call(
        _ada_kernel,
        grid=(depth, n6 // tn),
        in_specs=[
            pl.BlockSpec((8, d), lambda l, j: (0, 0)),
            pl.BlockSpec((None, d, tn), lambda l, j: (l, 0, j)),
            pl.BlockSpec((None, 1, tn), lambda l, j: (l, 0, j)),
        ],
        out_specs=pl.BlockSpec((None, 8, tn), lambda l, j: (l, 0, j)),
        out_shape=jax.ShapeDtypeStruct((depth, 8, n6), F32),
        compiler_params=_cp(("parallel", "parallel")),
        name="ada_mod",
    )(c8, ada_w, ada_b.reshape(depth, 1, n6))


def _resnorm_kernel(*refs, has_y, has_h):
    it = iter(refs)
    x_ref = next(it)
    y_ref = next(it) if has_y else None
    mv_ref = next(it)
    xo_ref = next(it) if has_y else None
    h_ref = next(it) if has_h else None
    x = x_ref[...]
    mv = mv_ref[...]
    if has_y:
        x = x + mv[0:1] * y_ref[...]
        xo_ref[...] = x
    if has_h:
        h = _rms(x, mv[3:4]) * (1.0 + mv[1:2]) + mv[2:3]
        h_ref[...] = h.astype(h_ref.dtype)


def _resnorm(x, y, modv, n_prompt, seq_s, has_h, rows=None, h_dtype=BF16):
    d = x.shape[1]
    row0, n = (0, x.shape[0]) if rows is None else rows
    i0 = row0 // TM
    has_y = y is not None

    def group(i):
        r = (i + i0) * TM
        return jnp.where(r < n_prompt, 0, 1 + (r - n_prompt) // seq_s)

    tile = pl.BlockSpec((TM, d), lambda i: (i + i0, 0))
    in_specs = [tile] + ([tile] if has_y else []) + [pl.BlockSpec((None, 8, d), lambda i: (group(i), 0, 0))]
    out_tile = pl.BlockSpec((TM, d), lambda i: (i, 0))
    out_specs, out_shape = [], []
    if has_y:
        out_specs.append(out_tile)
        out_shape.append(jax.ShapeDtypeStruct((n, d), F32))
    if has_h:
        out_specs.append(out_tile)
        out_shape.append(jax.ShapeDtypeStruct((n, d), h_dtype))
    args = [x] + ([y] if has_y else []) + [modv]
    outs = pl.pallas_call(
        functools.partial(_resnorm_kernel, has_y=has_y, has_h=has_h),
        grid=(n // TM,),
        in_specs=in_specs,
        out_specs=out_specs,
        out_shape=out_shape,
        compiler_params=_cp(("parallel",)),
        name="resnorm",
    )(*args)
    return outs


def _mm_kernel(x_ref, w_ref, o_ref, wb_ref):
    @pl.when(pl.program_id(1) == 0)
    def _():
        wb_ref[...] = w_ref[...].astype(BF16)

    o_ref[...] = _dot_nt(x_ref[...], wb_ref[...])


def _mm_t(x, wt, l, nrows, tn, name, tm=TM):
    m, k = x.shape
    if l is None:
        wspec = pl.BlockSpec((tn, k), lambda j, i: (j, 0))
    else:
        wspec = pl.BlockSpec((None, tn, k), lambda j, i: (l, j, 0))
    return pl.pallas_call(
        _mm_kernel,
        grid=(nrows // tn, m // tm),
        in_specs=[pl.BlockSpec((tm, k), lambda j, i: (i, 0)), wspec],
        out_specs=pl.BlockSpec((tm, tn), lambda j, i: (i, j)),
        out_shape=jax.ShapeDtypeStruct((m, nrows), F32),
        scratch_shapes=[pltpu.VMEM((tn, k), BF16)],
        compiler_params=_cp(("arbitrary", "arbitrary")),
        name=name,
    )(x, wt)


def _gla_intra(q, k, b, d, c):
    rowi = lax.broadcasted_iota(jnp.int32, (c, LANES), 0)
    ri = lax.broadcasted_iota(jnp.int32, (c, c), 0)
    ci = lax.broadcasted_iota(jnp.int32, (c, c), 1)
    a = jnp.zeros((c, c), F32)
    size = c
    while size >= 8:
        half = size // 2
        nb = c // size
        b3 = b.reshape(nb, size, LANES)
        pos = rowi & (size - 1)
        if d == 0:
            ref = b3[:, half - 1:half, :]
            qsel = pos >= half
        else:
            ref = b3[:, half:half + 1, :]
            qsel = pos < half
        ref = jnp.broadcast_to(ref, (nb, size, LANES)).reshape(c, LANES)
        z = jnp.where(qsel, q, k) * jnp.exp(-jnp.abs(b - ref))
        qs = jnp.where(qsel, z, 0.0).astype(BF16)
        ks = jnp.where(qsel, 0.0, z).astype(BF16)
        p = _dot_nt(qs, ks)
        if nb > 1:
            shift = int(math.log2(size))
            p = jnp.where((ri >> shift) == (ci >> shift), p, 0.0)
        a = a + p
        size = half
    b3 = b.reshape(c // 8, 8, LANES)
    lo, hi = (0, 4) if d == 0 else (3, 7)
    ref_lo = jnp.broadcast_to(b3[:, lo:lo + 1, :], (c // 8, 8, LANES)).reshape(c, LANES)
    ref_hi = jnp.broadcast_to(b3[:, hi:hi + 1, :], (c // 8, 8, LANES)).reshape(c, LANES)
    ref = jnp.where((rowi & 7) >= 4, ref_hi, ref_lo)
    qs = (q * jnp.exp(jnp.minimum(b - ref, 0.0))).astype(BF16)
    ks = (k * jnp.exp(ref - b)).astype(BF16)
    p = _dot_nt(qs, ks)
    keep = ((ri >> 2) == (ci >> 2)) & ((ci <= ri) if d == 0 else (ci >= ri))
    return a + jnp.where(keep, p, 0.0)


def _load_column_major(dst_ref, src_ref, t, stage=None):
    rows = t // GRID_W
    for k in range(src_ref.shape[1] // LANES):
        lanes = slice(k * LANES, (k + 1) * LANES)
        if stage is None:
            strided = src_ref
        else:
            stage[...] = src_ref[:, lanes]
            strided = stage
        for col in range(GRID_W):
            dst_ref[col * rows:(col + 1) * rows, lanes] = strided[pl.ds(col, rows, stride=GRID_W), :]


def _store_row_major(dst_ref, y, t, stage=None):
    rows = t // GRID_W
    for k in range(y.shape[1] // LANES):
        lanes = slice(k * LANES, (k + 1) * LANES)
        strided = dst_ref if stage is None else stage
        for col in range(GRID_W):
            strided[pl.ds(col, rows, stride=GRID_W), :] = y[col * rows:(col + 1) * rows, lanes]
        if stage is not None:
            dst_ref[:, lanes] = stage[...]


def _hgrn_kernel(aq_ref, af_ref, ab_ref, ai_ref, ag_ref, lb_ref, g_ref, s0_ref, y_ref, s_ref, o_acc, *perm_bufs,
                 t, c):
    if perm_bufs:
        srcs = (aq_ref, af_ref, ab_ref, ai_ref, ag_ref)
        for dst, src in zip(perm_bufs, srcs):
            _load_column_major(dst, src, t)
        aq_ref, af_ref, ab_ref, ai_ref, ag_ref = perm_bufs
    nchunk = t // c
    scale = HEAD_D ** -0.5
    for d in (0, 1):
        z_ref = af_ref if d == 0 else ab_ref
        lbd = lb_ref[d:d + 1, :]

        def chunk(i, st, d=d, z_ref=z_ref, lbd=lbd):
            cidx = i if d == 0 else nchunk - 1 - i
            r0 = pl.multiple_of(cidx * c, c)
            z = z_ref[pl.ds(r0, c), :]
            f = lbd + (1.0 - lbd) * jax.nn.sigmoid(z)
            lf = jnp.log(jnp.maximum(f, F_FLOOR))
            k = (1.0 - lbd) * jax.nn.sigmoid(-z)
            q = _silu(aq_ref[pl.ds(r0, c), :]) * scale
            v = ai_ref[pl.ds(r0, c), :].astype(BF16)
            b = _chunk_cumsum(lf, d, c)
            a = _gla_intra(q, k, b, d, c)
            o = _dot(a.astype(BF16), v) + _dot_nt((q * jnp.exp(b)).astype(BF16), st.astype(BF16))
            if d == 0:
                o_acc[pl.ds(r0, c), :] = o
            else:
                o_acc[pl.ds(r0, c), :] += o
            blast = b[c - 1:c, :] if d == 0 else b[0:1, :]
            kd = (k * jnp.exp(blast - b)).astype(BF16)
            return st * jnp.exp(blast) + _dot_tn(v, kd)

        st = s0_ref[0, d, 0].T
        if nchunk == 1:
            st = chunk(0, st)
        else:
            st = lax.fori_loop(0, nchunk, chunk, st)
        s_ref[0, d, 0] = st.T
    y = _rms(o_acc[...], g_ref[...]) * _silu(ag_ref[...])
    if perm_bufs:
        _store_row_major(y_ref, y, t)
    else:
        y_ref[...] = y


def _shared_out_call(kernel_fn, prevs, in_specs, args, **kw):
    extra = [p for p in prevs if p is not None]
    if not extra:
        return pl.pallas_call(kernel_fn, in_specs=in_specs, **kw)(*args)
    n_in = len(in_specs)
    aliases, k = {}, 0
    for out_idx, p in enumerate(prevs):
        if p is not None:
            aliases[n_in + k] = out_idx
            k += 1

    def body(*refs):
        kernel_fn(*refs[:n_in], *refs[n_in + len(extra):])

    return pl.pallas_call(body, in_specs=in_specs + [pl.BlockSpec(memory_space=pl.ANY)] * len(extra),
                          input_output_aliases=aliases, **kw)(*args, *extra)


def _state_out(nseq, slot, nslots, hg):
    spec = pl.BlockSpec((1, None, 2, hg, HEAD_D, HEAD_D), lambda s, h: (s, slot, 0, h, 0, 0))
    return spec, jax.ShapeDtypeStruct((nseq, nslots, 2, N_HEADS, HEAD_D, HEAD_D), F32)


def _hgrn(p_main, lb, norm_g, s0, row_blk0, nseq, t, y_prev=None, s_prev=None, slot=0, nslots=1,
          col_major=False):
    c = min(GLA_CHUNK, t)
    seq_buf = pltpu.VMEM((t, HEAD_D), F32)

    def col(base):
        return pl.BlockSpec((t, HEAD_D), lambda s, h: (row_blk0 + s, base + h))

    st_spec = pl.BlockSpec((1, 2, 1, HEAD_D, HEAD_D), lambda s, h: (s, 0, h, 0, 0))
    so_spec, so_shape = _state_out(nseq, slot, nslots, 1)
    return _shared_out_call(
        functools.partial(_hgrn_kernel, t=t, c=c), [y_prev, s_prev],
        [col(0), col(4), col(8), col(12), col(16),
         pl.BlockSpec((2, HEAD_D), lambda s, h: (0, h)),
         pl.BlockSpec((1, HEAD_D), lambda s, h: (0, 0)),
         st_spec],
        (p_main, p_main, p_main, p_main, p_main, lb, norm_g.reshape(1, HEAD_D), s0),
        grid=(nseq, N_HEADS),
        out_specs=[pl.BlockSpec((t, HEAD_D), lambda s, h: (row_blk0 + s, h)), so_spec],
        out_shape=[jax.ShapeDtypeStruct((p_main.shape[0], BR_WIDTH), F32), so_shape],
        scratch_shapes=[seq_buf] * (6 if col_major else 1),
        compiler_params=_cp(("parallel", "parallel")),
        name="hgrn2",
    )


def _dot_b(a, b):
    return _dot(a.astype(BF16), b.astype(BF16))


def _tri_solve(ms, xs, c):
    ri = lax.broadcasted_iota(jnp.int32, (c, c), 0)
    ci = lax.broadcasted_iota(jnp.int32, (c, c), 1)
    eye = jnp.where(ri == ci, 1.0, 0.0)
    m8 = [jnp.where((ri >> 3) == (ci >> 3), m, 0.0) for m in ms]
    ts = [eye - m for m in m8]
    ps = [_dot_b(m, m) for m in m8]
    ts = [t + _dot_b(t, p) for t, p in zip(ts, ps)]
    ps = [_dot_b(p, p) for p in ps]
    ts = [t + _dot_b(t, p) for t, p in zip(ts, ps)]
    shift = 3
    while (1 << shift) < c:
        join = ((ri >> (shift + 1)) == (ci >> (shift + 1))) & ((ri >> shift) != (ci >> shift))
        lts = [_dot_b(jnp.where(join, m, 0.0), t) for m, t in zip(ms, ts)]
        ts = [t - _dot_b(t, lt) for t, lt in zip(ts, lts)]
        shift += 1
    return [_dot_b(t, x) for t, x in zip(ts, xs)]


def _gdn_kernel(cq_ref, ck_ref, cv_ref, cg_ref, ab_ref, cw_ref, alog_ref, dtb_ref, g_ref, s0_ref,
                y_ref, s_ref, q_s, k_s, v_s, la_s, be_s, w_s, qk_s, qe_s, kd_s, dec_s, o_acc, *perm_bufs,
                t, c, span, hg):
    stage = None
    if perm_bufs:
        *perm_bufs, stage = perm_bufs
        srcs = (cq_ref, ck_ref, cv_ref, cg_ref, ab_ref)
        for dst, src in zip(perm_bufs, srcs):
            _load_column_major(dst, src, t, stage)
        cq_ref, ck_ref, cv_ref, cg_ref, ab_ref = perm_bufs
    nchunk = t // c
    cps = span // c
    rowt = lax.broadcasted_iota(jnp.int32, (t, HEAD_D), 0)
    lane = lax.broadcasted_iota(jnp.int32, (t, LANES), 1)
    ri = lax.broadcasted_iota(jnp.int32, (span, span), 0)
    ci = lax.broadcasted_iota(jnp.int32, (span, span), 1)
    shift = int(math.log2(c))
    same = (ri >> shift) == (ci >> shift)
    heads = range(hg)
    dirs = (0, 1)

    def conv(x_ref, j, hh):
        x = x_ref[:, hh * HEAD_D:(hh + 1) * HEAD_D]
        w = cw_ref[j][:, hh * HEAD_D:(hh + 1) * HEAD_D]
        xm = jnp.where(rowt == 0, 0.0, pltpu.roll(x, 1, 0))
        xp = jnp.where(rowt == t - 1, 0.0, pltpu.roll(x, t - 1, 0))
        return xm * w[0:1] + x * w[1:2] + xp * w[2:3]

    ab = ab_ref[...]
    for hh in heads:
        h = pl.program_id(1) * hg + hh
        q = _silu(conv(cq_ref, 0, hh))
        q_s[hh] = q * lax.rsqrt(jnp.sum(q * q, axis=-1, keepdims=True) + EPS) * (HEAD_D ** -0.5)
        k = _silu(conv(ck_ref, 1, hh))
        k_s[hh] = k * lax.rsqrt(jnp.sum(k * k, axis=-1, keepdims=True) + EPS)
        v_s[hh] = _silu(conv(cv_ref, 2, hh))
        for d in dirs:
            a_col = jnp.sum(jnp.where(lane == d * N_HEADS + h, ab, 0.0), axis=-1, keepdims=True)
            b_col = jnp.sum(jnp.where(lane == 2 * N_HEADS + d * N_HEADS + h, ab, 0.0), axis=-1, keepdims=True)
            coef = -jnp.exp(jnp.full((1, LANES), alog_ref[d, h], F32))
            xs = a_col + dtb_ref[d, h]
            softplus = jnp.maximum(xs, 0.0) + jnp.log1p(jnp.exp(-jnp.abs(xs)))
            la_s[hh, d] = coef * jnp.broadcast_to(softplus, (t, LANES))
            be_s[hh, d] = jnp.broadcast_to(jax.nn.sigmoid(b_col), (t, LANES))
    o_acc[...] = jnp.zeros(o_acc.shape, F32)

    def wy_span(sp, carry):
        r0 = pl.multiple_of(sp * span, span)
        ms, xs, dst = [], [], []
        for hh in heads:
            qsp = q_s[hh, pl.ds(r0, span), :]
            ksp = k_s[hh, pl.ds(r0, span), :]
            vsp = v_s[hh, pl.ds(r0, span), :]
            kb = ksp.astype(BF16)
            kk = _dot_nt(kb, kb)
            qk = _dot_nt(qsp.astype(BF16), kb)
            for d in dirs:
                be = be_s[hh, d, pl.ds(r0, span), :]
                g = _chunk_cumsum(la_s[hh, d, pl.ds(r0, span), :], d, c)
                gt = g.T
                gcol = jnp.concatenate([g] * (span // LANES), axis=1)
                grow = jnp.concatenate([gt] * (span // LANES), axis=0)
                incl = same & ((ci <= ri) if d == 0 else (ci >= ri))
                gamma = jnp.where(incl, jnp.exp(jnp.minimum(gcol - grow, 0.0)), 0.0)
                bcol = jnp.concatenate([be] * (span // LANES), axis=1)
                m_sp = jnp.where(ci == ri, 0.0, bcol * kk * gamma)
                qkg = (qk * gamma).astype(BF16)
                eg = jnp.exp(g)
                xv = be * vsp
                xk = be * eg * ksp
                qe_s[hh, d, pl.ds(r0, span), :] = (qsp * eg).astype(BF16)
                for j in range(cps):
                    lo = j * c
                    rj = pl.multiple_of(r0 + lo, c)
                    ms.append(m_sp[lo:lo + c, lo:lo + c])
                    xs.append(jnp.concatenate([xv[lo:lo + c], xk[lo:lo + c]], axis=1))
                    dst.append((hh, d, rj))
                    qk_s[hh, d, pl.ds(rj, c), 0:c] = qkg[lo:lo + c, lo:lo + c]
                    gj = g[lo:lo + c]
                    glast = gj[c - 1:c] if d == 0 else gj[0:1]
                    kd_s[hh, d, pl.ds(rj, c), :] = (ksp[lo:lo + c] * jnp.exp(glast - gj)).astype(BF16)
                    dec_s[hh, d, pl.ds(sp * cps + j, 1), :] = jnp.exp(glast)
        for (hh, d, rj), w in zip(dst, _tri_solve(ms, xs, c)):
            w_s[hh, d, pl.ds(rj, c), :] = w
        return carry

    if t == span:
        wy_span(0, 0)
    else:
        lax.fori_loop(0, t // span, wy_span, 0)

    chains = [(hh, d) for hh in heads for d in dirs]

    def scan_step(i, carry):
        cidx = (i, nchunk - 1 - i)
        r0 = [pl.multiple_of(ck * c, c) for ck in cidx]
        sb = [s.astype(BF16) for s in carry]
        ws = [w_s[hh, d, pl.ds(r0[d], c), :] for hh, d in chains]
        wks = [_dot(w[:, HEAD_D:].astype(BF16), s) for w, s in zip(ws, sb)]
        oq = [_dot(qe_s[hh, d, pl.ds(r0[d], c), :], s) for (hh, d), s in zip(chains, sb)]
        vnb = [(w[:, :HEAD_D] - wk).astype(BF16) for w, wk in zip(ws, wks)]
        new = [s * dec_s[hh, d, pl.ds(cidx[d], 1), :] + _dot_tn(kd_s[hh, d, pl.ds(r0[d], c), :], v)
               for (hh, d), s, v in zip(chains, carry, vnb)]
        for (hh, d), o, v in zip(chains, oq, vnb):
            o_acc[hh, pl.ds(r0[d], c), :] += o + _dot(qk_s[hh, d, pl.ds(r0[d], c), 0:c], v)
        return tuple(new)

    finals = lax.fori_loop(0, nchunk, scan_step, tuple(s0_ref[0, d, hh] for hh, d in chains),
                           unroll=nchunk <= 4)
    for (hh, d), s in zip(chains, finals):
        s_ref[0, d, hh] = s
    ys = []
    for hh in heads:
        sl = slice(hh * HEAD_D, (hh + 1) * HEAD_D)
        ys.append(_rms(o_acc[hh], g_ref[...]) * _silu(cg_ref[:, sl]))
    y = jnp.concatenate(ys, axis=1)
    if perm_bufs:
        _store_row_major(y_ref, y, t, stage)
    else:
        y_ref[...] = y


def _gdn(p_main, p_ab, conv_w, a_log, dt_bias, norm_g, s0, row_blk0, nseq, t, y_prev=None, s_prev=None,
         slot=0, nslots=1, col_major=False):
    c = GDN_CHUNK
    hg = GDN_HEADS_PER_STEP if t * GDN_HEADS_PER_STEP <= GDN_MAX_ROWS_PER_STEP else GDN_HEADS_PER_STEP // 2
    wid = hg * HEAD_D
    blk = BR_WIDTH // wid

    def col(base):
        return pl.BlockSpec((t, wid), lambda s, h: (row_blk0 + s, base * blk + h))

    st_spec = pl.BlockSpec((1, 2, hg, HEAD_D, HEAD_D), lambda s, h: (s, 0, h, 0, 0))
    smem = pl.BlockSpec(memory_space=pltpu.SMEM)
    seq_buf = pltpu.VMEM((hg, t, HEAD_D), F32)
    dir_f32 = pltpu.VMEM((hg, 2, t, LANES), F32)
    dir_bf16 = pltpu.VMEM((hg, 2, t, LANES), BF16)
    scratch = [seq_buf, seq_buf, seq_buf, dir_f32, dir_f32,
               pltpu.VMEM((hg, 2, t, 2 * HEAD_D), F32), dir_bf16, dir_bf16, dir_bf16,
               pltpu.VMEM((hg, 2, max(t // c, 8), LANES), F32), seq_buf]
    if col_major:
        scratch += [pltpu.VMEM((t, wid), F32)] * 4 + [pltpu.VMEM((t, LANES), F32)] * 2
    so_spec, so_shape = _state_out(nseq, slot, nslots, hg)
    return _shared_out_call(
        functools.partial(_gdn_kernel, t=t, c=c, span=GDN_SPAN, hg=hg), [y_prev, s_prev],
        [col(6), col(7), col(8), col(9),
         pl.BlockSpec((t, LANES), lambda s, h: (row_blk0 + s, 0)),
         pl.BlockSpec((3, 3, wid), lambda s, h: (0, 0, h)),
         smem, smem,
         pl.BlockSpec((1, HEAD_D), lambda s, h: (0, 0)),
         st_spec],
        (p_main, p_main, p_main, p_main, p_ab, conv_w, a_log, dt_bias, norm_g.reshape(1, HEAD_D), s0),
        grid=(nseq, N_HEADS // hg),
        out_specs=[pl.BlockSpec((t, wid), lambda s, h: (row_blk0 + s, h)), so_spec],
        out_shape=[jax.ShapeDtypeStruct((p_main.shape[0], BR_WIDTH), F32), so_shape],
        scratch_shapes=scratch,
        compiler_params=_cp(("parallel", "parallel")),
        name="gdn",
    )


def _cmul(ar, ai, br, bi):
    return ar * br - ai * bi, ar * bi + ai * br


def _s5_kernel(u_ref, bb_ref, cc_ref, lam_ref, dsk_ref, s0_ref, y_ref, sf_ref, up_s, x_s, yp_s,
               *, tb, seg, segs_per_seq, zero_init, col_major):
    r_par = tb // seg
    half = S5_GB * S5_STATE
    if col_major:
        rows = tb // GRID_W
        assert segs_per_seq == r_par and seg % rows == 0

        def step_rows(r):
            return pl.ds((r % rows) * GRID_W + r // rows, r_par, stride=seg // rows)
    else:
        def step_rows(r):
            return pl.ds(r, r_par, stride=seg)

    for r in range(seg):
        up_s[r * r_par:(r + 1) * r_par, :] = u_ref[step_rows(r), :]
    ub = up_s[...].astype(BF16)
    sf_ref[...] = jnp.zeros(sf_ref.shape, F32)
    nseq = r_par // segs_per_seq
    lam = [(lam_ref[0, d, 0:1, :], lam_ref[0, d, 1:2, :]) for d in (0, 1)]
    for d in (0, 1):
        x_s[d] = _dot(ub, bb_ref[0, d])

    def offsets(i):
        return [pl.multiple_of(r * r_par, r_par) for r in (i, seg - 1 - i)]

    def local_step(i, carry):
        off = offsets(i)
        new = []
        for d in (0, 1):
            (lre, lim), (hre, him) = lam[d], carry[d]
            nre = lre * hre - lim * him + x_s[d, pl.ds(off[d], r_par), 0:half]
            nim = lre * him + lim * hre + x_s[d, pl.ds(off[d], r_par), half:2 * half]
            x_s[d, pl.ds(off[d], r_par), 0:half] = nre
            x_s[d, pl.ds(off[d], r_par), half:2 * half] = nim
            new.append((nre, nim))
        return tuple(new)

    zero = jnp.zeros((r_par, half), F32)
    lax.fori_loop(0, seg, local_step, ((zero, zero), (zero, zero)), unroll=4)

    end_rows = ((seg - 1) * r_par, 0)
    needs_carry = not (zero_init and segs_per_seq == 1)
    hins = []
    for d in (0, 1):
        if not needs_carry:
            break
        lre, lim = lam[d]
        pre, pim = lre, lim
        n = 1
        while n < seg:
            pre, pim = _cmul(pre, pim, pre, pim)
            n *= 2
        end_row = end_rows[d]
        ere = x_s[d, end_row:end_row + r_par, 0:half]
        eim = x_s[d, end_row:end_row + r_par, half:2 * half]
        hin = [None] * r_par
        order = range(r_par) if d == 0 else range(r_par - 1, -1, -1)
        for j in order:
            sq, pos = divmod(j, segs_per_seq)
            first = pos == (0 if d == 0 else segs_per_seq - 1)
            if first:
                hin[j] = (s0_ref[0, 0, d, sq:sq + 1, 0:half], s0_ref[0, 0, d, sq:sq + 1, half:2 * half])
            else:
                pj = j - 1 if d == 0 else j + 1
                cre, cim = _cmul(pre, pim, hin[pj][0], hin[pj][1])
                hin[j] = (cre + ere[pj:pj + 1], cim + eim[pj:pj + 1])
        hins.append((jnp.concatenate([hj[0] for hj in hin], axis=0),
                     jnp.concatenate([hj[1] for hj in hin], axis=0)))

    def carry_step(i, pws):
        off = offsets(i)
        new = []
        for d in (0, 1):
            (lre, lim), (wre, wim) = lam[d], pws[d]
            are, aim = _cmul(wre, wim, hins[d][0], hins[d][1])
            x_s[d, pl.ds(off[d], r_par), 0:half] += are
            x_s[d, pl.ds(off[d], r_par), half:2 * half] += aim
            new.append(_cmul(wre, wim, lre, lim))
        return tuple(new)

    if needs_carry:
        lax.fori_loop(0, seg, carry_step, (lam[0], lam[1]), unroll=4)

    for d in (0, 1):
        for sq in range(nseq):
            j = (sq + 1) * segs_per_seq - 1 if d == 0 else sq * segs_per_seq
            row = end_rows[d] + j
            sf_ref[0, 0, d, sq:sq + 1, :] = x_s[d, row:row + 1, :]
    yp_s[...] = (_dot(x_s[0].astype(BF16), cc_ref[0, 0]) + _dot(x_s[1].astype(BF16), cc_ref[0, 1])
                 + up_s[...] * dsk_ref[...])
    for r in range(seg):
        y_ref[step_rows(r), :] = yp_s[r * r_par:(r + 1) * r_par, :]


def _s5(p_main, bb, cc, lam, dskip, s0, row_blk0, nblk, tb, t, seg, zero_init, y_prev=None, col_major=False):
    ngb = S5_GROUPS // S5_GB
    w2 = 2 * S5_GB * S5_STATE
    return _shared_out_call(
        functools.partial(_s5_kernel, tb=tb, seg=seg, segs_per_seq=t // seg, zero_init=zero_init,
                          col_major=col_major), [y_prev, None],
        [
            pl.BlockSpec((tb, LANES), lambda i, gb: (row_blk0 + i, 20 + gb)),
            pl.BlockSpec((1, 2, LANES, w2), lambda i, gb: (gb, 0, 0, 0)),
            pl.BlockSpec((1, 2, w2, LANES), lambda i, gb: (gb, 0, 0, 0)),
            pl.BlockSpec((1, 2, 2, w2 // 2), lambda i, gb: (gb, 0, 0, 0)),
            pl.BlockSpec((1, LANES), lambda i, gb: (0, gb)),
            pl.BlockSpec((1, 1, 2, 8, w2), lambda i, gb: (i, gb, 0, 0, 0)),
        ],
        (p_main, bb, cc, lam, dskip.reshape(1, BR_WIDTH), s0),
        grid=(nblk, ngb),
        out_specs=[pl.BlockSpec((tb, LANES), lambda i, gb: (row_blk0 + i, gb)),
                   pl.BlockSpec((1, 1, 2, 8, w2), lambda i, gb: (i, gb, 0, 0, 0))],
        out_shape=[jax.ShapeDtypeStruct((p_main.shape[0], BR_WIDTH), F32),
                   jax.ShapeDtypeStruct((nblk, ngb, 2, 8, w2), F32)],
        scratch_shapes=[pltpu.VMEM((tb, LANES), F32), pltpu.VMEM((2, tb, w2), F32), pltpu.VMEM((tb, LANES), F32)],
        compiler_params=_cp(("parallel", "parallel")),
        name="s5",
    )


def _s5_params(a_re, a_im, log_dt, b_re, b_im, c_re, c_im):
    dt = jnp.exp(log_dt)[..., None]
    mag = jnp.exp(dt * a_re)
    lam_re = mag * jnp.cos(dt * a_im)
    lam_im = mag * jnp.sin(dt * a_im)
    den = a_re * a_re + a_im * a_im
    co_re = ((lam_re - 1.0) * a_re + lam_im * a_im) / den
    co_im = (lam_im * a_re - (lam_re - 1.0) * a_im) / den
    bb_re = co_re[..., None] * b_re - co_im[..., None] * b_im
    bb_im = co_re[..., None] * b_im + co_im[..., None] * b_re
    ngb = S5_GROUPS // S5_GB
    eye = jnp.eye(S5_GB, dtype=F32)

    def pack_in(w):
        w = w.reshape(2, ngb, S5_GB, S5_STATE, S5_GROUP)
        blk = jnp.einsum('dbgpi,gh->bdgihp', w, eye)
        return blk.reshape(ngb, 2, S5_GB * S5_GROUP, S5_GB * S5_STATE)

    def pack_out(w):
        w = w.reshape(2, ngb, S5_GB, S5_GROUP, S5_STATE)
        blk = jnp.einsum('dbgip,gh->bdgphi', w, eye)
        return blk.reshape(ngb, 2, S5_GB * S5_STATE, S5_GB * S5_GROUP)

    bb = jnp.concatenate([pack_in(bb_re), pack_in(bb_im)], axis=-1).astype(BF16)
    cc = jnp.concatenate([pack_out(c_re), -pack_out(c_im)], axis=-2).astype(BF16)
    lam = jnp.stack([lam_re, lam_im], axis=1)
    lam = lam.reshape(2, 2, ngb, S5_GB * S5_STATE).transpose(2, 0, 1, 3)
    return bb, cc, lam


def _glu_kernel(y_ref, w_ref, b_ref, o_ref, wb_ref):
    @pl.when(pl.program_id(0) == 0)
    def _():
        wb_ref[...] = w_ref[...].astype(BF16)

    z = _dot(y_ref[...].astype(BF16), wb_ref[...]) + b_ref[...]
    half = z.shape[1] // 2
    o_ref[...] = z[:, :half] * jax.nn.sigmoid(z[:, half:])


def _glu(y, w, b, l):
    n, k = y.shape
    n2 = w.shape[-1]
    return pl.pallas_call(
        _glu_kernel,
        grid=(n // TM,),
        in_specs=[pl.BlockSpec((TM, k), lambda i: (i, 0)),
                  pl.BlockSpec((None, k, n2), lambda i: (l, 0, 0)),
                  pl.BlockSpec((None, 1, n2), lambda i: (l, 0, 0))],
        out_specs=pl.BlockSpec((TM, n2 // 2), lambda i: (i, 0)),
        out_shape=jax.ShapeDtypeStruct((n, n2 // 2), F32),
        scratch_shapes=[pltpu.VMEM((k, n2), BF16)],
        compiler_params=_cp(("arbitrary",)),
        name="s5_glu",
    )(y, w, b.reshape(b.shape[0], 1, n2))


def _residual_epilogue(y, g_ref, x_ref, mv_ref, xo_ref, h_ref):
    mv = mv_ref[...]
    x = x_ref[...] + mv[0:1] * _rms(y, g_ref[...])
    xo_ref[...] = x
    h_ref[...] = (_rms(x, mv[3:4]) * (1.0 + mv[1:2]) + mv[2:3]).astype(h_ref.dtype)


def _group_of_tile(tm, n_prompt, seq_s):
    def group(i):
        return jnp.where(i * tm < n_prompt, 0, 1 + (i * tm - n_prompt) // seq_s)
    return group


def _merge_kernel(ya_ref, yb_ref, yc_ref, mg_ref, wbr_ref, wout_ref, g_ref, x_ref, mv_ref, xo_ref, h_ref,
                  wbr_s, wout_s):
    @pl.when(pl.program_id(0) == 0)
    def _():
        wbr_s[...] = wbr_ref[...].astype(BF16)
        wout_s[...] = wout_ref[...].astype(BF16)

    d = wout_s.shape[0]
    acc = None
    for n, br_ref in enumerate((ya_ref, yb_ref, yc_ref)):
        proj = _dot(br_ref[...].astype(BF16), wbr_s[n])
        term = proj * jax.nn.sigmoid(mg_ref[:, n * d:(n + 1) * d])
        acc = term if acc is None else acc + term
    y = _dot(acc.astype(BF16), wout_s[...])
    _residual_epilogue(y, g_ref, x_ref, mv_ref, xo_ref, h_ref)


def _merge(ya, yb, yc, mg, w_branch, w_out, g, l, x, modv, n_prompt, seq_s, h_dtype):
    n = ya.shape[0]
    d = w_out.shape[-1]
    tm = 256
    group = _group_of_tile(tm, n_prompt, seq_s)
    br = pl.BlockSpec((tm, BR_WIDTH), lambda i: (i, 0))
    tile = pl.BlockSpec((tm, d), lambda i: (i, 0))
    return pl.pallas_call(
        _merge_kernel,
        grid=(n // tm,),
        in_specs=[br, br, br,
                  pl.BlockSpec((tm, 3 * d), lambda i: (i, 0)),
                  pl.BlockSpec((None, 3, BR_WIDTH, d), lambda i: (l, 0, 0, 0)),
                  pl.BlockSpec((None, d, d), lambda i: (l, 0, 0)),
                  pl.BlockSpec((1, d), lambda i: (0, 0)),
                  tile,
                  pl.BlockSpec((None, 8, d), lambda i: (group(i), 0, 0))],
        out_specs=[tile, tile],
        out_shape=[jax.ShapeDtypeStruct((n, d), F32), jax.ShapeDtypeStruct((n, d), h_dtype)],
        scratch_shapes=[pltpu.VMEM((3, BR_WIDTH, d), BF16), pltpu.VMEM((d, d), BF16)],
        compiler_params=_cp(("arbitrary",)),
        name="merge",
    )(ya, yb, yc, mg, w_branch, w_out, g.reshape(1, d), x, modv)


def _swiglu_kernel(*refs, gated):
    if gated:
        x_ref, wg_ref, wu_ref, gt_ref, o_ref, wg_s, wu_s = refs
    else:
        x_ref, wg_ref, wu_ref, o_ref, wg_s, wu_s = refs

    @pl.when(pl.program_id(1) == 0)
    def _():
        wg_s[...] = wg_ref[...].astype(BF16)
        wu_s[...] = wu_ref[...].astype(BF16)

    x = x_ref[...]
    hid = _silu(_dot(x, wg_s[...])) * _dot(x, wu_s[...])
    if gated:
        gt = gt_ref[...]
        lane = lax.broadcasted_iota(jnp.int32, gt.shape, 1)
        hid = hid * jnp.sum(jnp.where(lane == pl.program_id(0), gt, 0.0), axis=-1, keepdims=True)
    o_ref[...] = hid.astype(BF16)


def _swiglu_up(x, wg, wu, e, f, gates=None):
    n, d = x.shape
    gated = gates is not None
    if gated:
        ne = wg.shape[1]
        wspec = pl.BlockSpec((None, None, d, f), lambda j, i: (e, j, 0, 0))
    else:
        ne = wg.shape[-1] // f
        wspec = pl.BlockSpec((None, d, f), lambda j, i: (e, 0, j))
    in_specs = [pl.BlockSpec((TM, d), lambda j, i: (i, 0)), wspec, wspec]
    args = [x, wg, wu]
    if gated:
        in_specs.append(pl.BlockSpec((TM, LANES), lambda j, i: (i, 0)))
        args.append(gates)
    return pl.pallas_call(
        functools.partial(_swiglu_kernel, gated=gated),
        grid=(ne, n // TM),
        in_specs=in_specs,
        out_specs=pl.BlockSpec((TM, f), lambda j, i: (i, j)),
        out_shape=jax.ShapeDtypeStruct((n, ne * f), BF16),
        scratch_shapes=[pltpu.VMEM((d, f), BF16), pltpu.VMEM((d, f), BF16)],
        compiler_params=_cp(("arbitrary", "arbitrary")),
        name="swiglu_up",
    )(*args)


def _down_kernel(hid_ref, w_ref, g_ref, x_ref, mv_ref, xo_ref, h_ref, w_s):
    @pl.when(pl.program_id(0) == 0)
    def _():
        w_s[...] = w_ref[...].astype(BF16)

    _residual_epilogue(_dot(hid_ref[...], w_s[...]), g_ref, x_ref, mv_ref, xo_ref, h_ref)


def _down_res(hid, wd, e, g, x, modv, n_prompt, seq_s, h_dtype):
    n, f = hid.shape
    d = wd.shape[-1]
    group = _group_of_tile(TM, n_prompt, seq_s)
    tile = pl.BlockSpec((TM, d), lambda i: (i, 0))
    return pl.pallas_call(
        _down_kernel,
        grid=(n // TM,),
        in_specs=[pl.BlockSpec((TM, f), lambda i: (i, 0)),
                  pl.BlockSpec((None, f, d), lambda i: (e, 0, 0)),
                  pl.BlockSpec((1, d), lambda i: (0, 0)),
                  tile,
                  pl.BlockSpec((None, 8, d), lambda i: (group(i), 0, 0))],
        out_specs=[tile, tile],
        out_shape=[jax.ShapeDtypeStruct((n, d), F32), jax.ShapeDtypeStruct((n, d), h_dtype)],
        scratch_shapes=[pltpu.VMEM((f, d), BF16)],
        compiler_params=_cp(("arbitrary",)),
        name="down_res",
    )(hid, wd, g.reshape(1, d), x, modv)


def _router_kernel(x_ref, w_ref, aux_ref, cnt_ref, base_s, *, ne):
    @pl.when(pl.program_id(0) == 0)
    def _():
        base_s[...] = jnp.zeros(base_s.shape, F32)

    w = w_ref[...]
    w_hi = w.astype(BF16)
    w_lo = (w - w_hi.astype(F32)).astype(BF16)
    x = x_ref[...].astype(BF16)
    logits = _dot(x, w_hi) + _dot(x, w_lo)
    tm = logits.shape[0]
    lane = lax.broadcasted_iota(jnp.int32, logits.shape, 1).astype(F32)
    neg = jnp.float32(-jnp.inf)
    logits = jnp.where(lane < ne, logits, neg)
    m1 = jnp.max(logits, axis=-1, keepdims=True)
    i1 = jnp.min(jnp.where(logits == m1, lane, float(LANES)), axis=-1, keepdims=True)
    rest = jnp.where(lane == i1, neg, logits)
    m2 = jnp.max(rest, axis=-1, keepdims=True)
    i2 = jnp.min(jnp.where(rest == m2, lane, float(LANES)), axis=-1, keepdims=True)
    e2 = jnp.exp(m2 - m1)
    w1 = 1.0 / (1.0 + e2)
    w2 = e2 / (1.0 + e2)
    sel = jnp.where((lane == i1) | (lane == i2), 1.0, 0.0)
    ri = lax.broadcasted_iota(jnp.int32, (tm, tm), 0)
    ci = lax.broadcasted_iota(jnp.int32, (tm, tm), 1)
    before = jnp.where(ci < ri, 1.0, 0.0).astype(BF16)
    rank = _dot(before, sel.astype(BF16)) + base_s[0:1, :]
    r1 = jnp.sum(jnp.where(lane == i1, rank, 0.0), axis=-1, keepdims=True)
    r2 = jnp.sum(jnp.where(lane == i2, rank, 0.0), axis=-1, keepdims=True)
    aux = jnp.zeros(logits.shape, F32)
    for k, v in enumerate((i1, i2, r1, r2, w1, w2)):
        aux = jnp.where(lane == float(k), v, aux)
    aux_ref[...] = aux
    base_s[...] = base_s[...] + jnp.sum(sel, axis=0, keepdims=True)
    cnt_ref[...] = base_s[...]


def _router(x, w_pad, ne):
    n, d = x.shape
    return pl.pallas_call(
        functools.partial(_router_kernel, ne=ne),
        grid=(n // TM,),
        in_specs=[pl.BlockSpec((TM, d), lambda i: (i, 0)),
                  pl.BlockSpec((d, LANES), lambda i: (0, 0))],
        out_specs=[pl.BlockSpec((TM, LANES), lambda i: (i, 0)),
                   pl.BlockSpec((8, LANES), lambda i: (0, 0))],
        out_shape=[jax.ShapeDtypeStruct((n, LANES), F32), jax.ShapeDtypeStruct((8, LANES), F32)],
        scratch_shapes=[pltpu.VMEM((8, LANES), F32)],
        compiler_params=_cp(("arbitrary",)),
        name="router",
    )(x, w_pad)


def _moe_plan(aux, cnt, n, ne, tile, max_tiles):
    e = aux[:, 0:2].astype(jnp.int32)
    r = aux[:, 2:4].astype(jnp.int32)
    c = cnt[0, :ne].astype(jnp.int32)
    pc = (c + tile - 1) // tile * tile
    ends = jnp.cumsum(pc)
    slot = ((ends - pc)[e] + r).reshape(-1)
    tok = jnp.broadcast_to(jnp.arange(n, dtype=F32)[:, None], (n, 2)).reshape(-1)
    m = max_tiles * tile
    pad = jnp.broadcast_to(jnp.array([float(n), 0.0], F32), (m, 2))
    table = pad.at[slot].set(jnp.stack([tok, aux[:, 4:6].reshape(-1)], axis=1))
    slot_tok = table[:, 0].astype(jnp.int32)
    slot_gate = table[:, 1]
    tstart = jnp.arange(max_tiles, dtype=jnp.int32) * tile
    tile_e = jnp.minimum(jnp.sum(tstart[:, None] >= ends[None, :], axis=1), ne - 1).astype(jnp.int32)
    tile_valid = (tstart < ends[-1]).astype(jnp.int32)
    return slot_tok.reshape(max_tiles, 1, tile), slot_gate.reshape(max_tiles, 1, tile), tile_e, tile_valid


def _expert_changed(t, te_ref):
    return jnp.logical_or(t == 0, te_ref[t] != te_ref[jnp.maximum(t - 1, 0)])


def _moe_up_kernel(te_ref, tv_ref, idx_ref, nxt_ref, h_ref, wg_ref, wu_ref, o_ref, xbuf, wg_s, wu_s, *, tile):
    t = pl.program_id(0)
    valid = tv_ref[t] == 1
    last = h_ref.shape[0] - 1
    cur = t % 2

    def gather_row(src_idx_ref, buf, j):
        xbuf[buf, pl.ds(j, 1), :] = h_ref[pl.ds(jnp.minimum(src_idx_ref[0, 0, j], last), 1), :]

    @pl.when(t == 0)
    def _():
        def first(j, carry):
            gather_row(idx_ref, 0, j)
            return carry

        lax.fori_loop(0, tile, first, 0, unroll=8)

    @pl.when(jnp.logical_and(valid, _expert_changed(t, te_ref)))
    def _():
        wg_s[...] = wg_ref[...].astype(BF16)
        wu_s[...] = wu_ref[...].astype(BF16)

    @pl.when(valid)
    def _():
        x = xbuf[cur].astype(BF16)
        o_ref[...] = (_silu(_dot(x, wg_s[...])) * _dot(x, wu_s[...])).astype(BF16)
        for j in range(tile):
            gather_row(nxt_ref, 1 - cur, j)

    @pl.when(jnp.logical_not(valid))
    def _():
        o_ref[...] = jnp.zeros(o_ref.shape, BF16)


def _moe_up(h, slot_tok, tile_e, tile_valid, wg, wu, e_layer, tile):
    max_tiles = slot_tok.shape[0]
    _, _, d, f = wg.shape
    wspec = pl.BlockSpec((None, None, d, f), lambda t, te, tv: (e_layer, te[t], 0, 0),
                         pipeline_mode=pl.Buffered(1))
    grid_spec = pltpu.PrefetchScalarGridSpec(
        num_scalar_prefetch=2,
        grid=(max_tiles,),
        in_specs=[pl.BlockSpec((1, 1, tile), lambda t, te, tv: (t, 0, 0), memory_space=pltpu.SMEM),
                  pl.BlockSpec((1, 1, tile), lambda t, te, tv: (jnp.minimum(t + 1, max_tiles - 1), 0, 0),
                               memory_space=pltpu.SMEM),
                  pl.BlockSpec(h.shape, lambda t, te, tv: (0, 0), pipeline_mode=pl.Buffered(1)),
                  wspec, wspec],
        out_specs=pl.BlockSpec((tile, f), lambda t, te, tv: (t, 0)),
        scratch_shapes=[pltpu.VMEM((2, tile, d), F32),
                        pltpu.VMEM((d, f), BF16), pltpu.VMEM((d, f), BF16)],
    )
    return pl.pallas_call(
        functools.partial(_moe_up_kernel, tile=tile),
        grid_spec=grid_spec,
        out_shape=jax.ShapeDtypeStruct((max_tiles * tile, f), BF16),
        compiler_params=_cp(("arbitrary",)),
        name="moe_up",
    )(tile_e, tile_valid, slot_tok, slot_tok, h, wg, wu)


def _moe_down_kernel(te_ref, tv_ref, idx_ref, gt_ref, hid_ref, wd_ref, g_ref, o_ref, acc, ybuf, tbuf, wd_s,
                     *, tile, n_tiles, tm_out):
    t = pl.program_id(0)
    tt = jnp.minimum(t, n_tiles - 1)

    @pl.when(t == 0)
    def _():
        acc[...] = jnp.zeros(acc.shape, F32)

    is_tile = jnp.logical_and(t < n_tiles, tv_ref[tt] == 1)

    @pl.when(jnp.logical_and(is_tile, _expert_changed(tt, te_ref)))
    def _():
        wd_s[...] = wd_ref[...].astype(BF16)

    @pl.when(is_tile)
    def _():
        for j in range(tile):
            tbuf[j:j + 1, :] = acc[pl.ds(idx_ref[0, 0, j], 1), :]
        ybuf[...] = _dot(hid_ref[...], wd_s[...])

        def put(j, carry):
            acc[pl.ds(idx_ref[0, 0, j], 1), :] = tbuf[pl.ds(j, 1), :] + gt_ref[0, 0, j] * ybuf[pl.ds(j, 1), :]
            return carry

        lax.fori_loop(0, tile, put, 0, unroll=8)

    @pl.when(t >= n_tiles)
    def _():
        r0 = pl.multiple_of((t - n_tiles) * tm_out, tm_out)
        o_ref[...] = _rms(acc[pl.ds(r0, tm_out), :], g_ref[...])


def _moe_down(hid, slot_tok, slot_gate, tile_e, tile_valid, wd, e_layer, g, n, tile):
    n_tiles = slot_tok.shape[0]
    _, _, f, d = wd.shape
    tm_out = TM
    n_flush = n // tm_out

    def tix(t):
        return jnp.minimum(t, n_tiles - 1)

    smem_tile = pl.BlockSpec((1, 1, tile), lambda t, te, tv: (tix(t), 0, 0), memory_space=pltpu.SMEM)
    grid_spec = pltpu.PrefetchScalarGridSpec(
        num_scalar_prefetch=2,
        grid=(n_tiles + n_flush,),
        in_specs=[smem_tile, smem_tile,
                  pl.BlockSpec((tile, f), lambda t, te, tv: (tix(t), 0)),
                  pl.BlockSpec((None, None, f, d), lambda t, te, tv: (e_layer, te[tix(t)], 0, 0)),
                  pl.BlockSpec((1, d), lambda t, te, tv: (0, 0))],
        out_specs=pl.BlockSpec((tm_out, d), lambda t, te, tv: (jnp.maximum(t - n_tiles, 0), 0)),
        scratch_shapes=[pltpu.VMEM((n + 8, d), F32), pltpu.VMEM((tile, d), F32), pltpu.VMEM((tile, d), F32),
                        pltpu.VMEM((f, d), BF16)],
    )
    return pl.pallas_call(
        functools.partial(_moe_down_kernel, tile=tile, n_tiles=n_tiles, tm_out=tm_out),
        grid_spec=grid_spec,
        out_shape=jax.ShapeDtypeStruct((n, d), F32),
        compiler_params=_cp(("arbitrary",)),
        name="moe_down",
    )(tile_e, tile_valid, slot_tok, slot_gate, hid, wd, g.reshape(1, d))


def kernel(x_prompt, x_sample, state_hgrn, state_s5, state_gdn, c, c_ctx, ada_w, ada_b, norm_g, w_in, hgrn_lb, hgrn_norm, s5_a_re, s5_a_im, s5_log_dt, s5_b_re, s5_b_im, s5_c_re, s5_c_im, s5_d, s5_glu_w, s5_glu_b, gdn_conv, gdn_a_log, gdn_dt_bias, gdn_norm, w_branch, w_out, ffn_wg, ffn_wu, ffn_wd, moe_router, moe_wg, moe_wu, moe_wd):
    bp, tp, d = x_prompt.shape
    bs, ts, _ = x_sample.shape
    depth = w_in.shape[0]
    n_p, n_s = bp * tp, bs * ts
    ne = moe_router.shape[-1]
    main_cols = 10 * BR_WIDTH
    ab_cols = 4 * N_HEADS

    c8 = jnp.zeros((8, d), F32).at[0].set(c_ctx).at[1:1 + bs].set(c)
    mod_all = _ada_mod(c8, ada_w, ada_b)

    lb_soft = jax.nn.softmax(hgrn_lb.astype(F32), axis=0)
    lb_all = jnp.cumsum(lb_soft, axis=0) - lb_soft[:1]

    mods = mod_all[:, :1 + bs].reshape(depth, 1 + bs, 6, d)
    zero_row = jnp.zeros((1 + bs, d), F32)

    def modv(gate, scale, shift, g):
        rows = [zero_row if sel is None else mods[sel[0], :, sel[1]] for sel in (gate, scale, shift)]
        rows.append(zero_row if g is None else jnp.broadcast_to(g, (1 + bs, d)))
        return jnp.stack(rows + [zero_row] * 4, axis=1)

    x = jnp.concatenate([x_prompt.reshape(n_p, d), x_sample.reshape(n_s, d)], axis=0)
    (h,) = _resnorm(x, None, modv(None, (0, 1), (0, 0), norm_g[0, 0]), n_p, ts, True)

    z_hgrn = jnp.zeros((bp, 2, N_HEADS, HEAD_D, HEAD_D), F32)
    ngb = S5_GROUPS // S5_GB
    w2 = 2 * S5_GB * S5_STATE
    st_s5 = []
    y_zero = jnp.zeros((n_p + n_s, BR_WIDTH), F32)
    new_hgrn = new_gdn = jnp.zeros((bp, depth, 2, N_HEADS, HEAD_D, HEAD_D), F32)
    w_in_t = jnp.swapaxes(w_in, 1, 2)
    s5_tb = 1024

    for l in range(depth):
        col_major = l % 2 == 1
        p_main = _mm_t(h, w_in_t, l, main_cols, 2 * BR_WIDTH, "w_in_main", tm=1024)
        w_ab = jnp.pad(w_in_t[l, main_cols:main_cols + ab_cols], ((0, LANES - ab_cols), (0, 0)))
        p_ab = _mm_t(h, w_ab, None, LANES, LANES, "w_in_ab", tm=1024)
        p_mg = _mm_t(h, w_in_t[l, main_cols + ab_cols:], None, 3 * d, 2 * BR_WIDTH, "w_in_merge", tm=1024)

        ya, new_hgrn = _hgrn(p_main, lb_all[l], hgrn_norm[l], z_hgrn, 0, bp, tp, y_zero, new_hgrn, l, depth)
        ya, _ = _hgrn(p_main, lb_all[l], hgrn_norm[l], state_hgrn[:, l].astype(F32), n_p // ts, bs, ts, ya,
                      col_major=col_major)

        bb, cc, lam = _s5_params(s5_a_re[l], s5_a_im[l], s5_log_dt[l], s5_b_re[l], s5_b_im[l],
                                 s5_c_re[l], s5_c_im[l])
        seq_per_blk = 8
        tb_p = seq_per_blk * tp
        nblk_p = n_p // tb_p
        yb, sb = _s5(p_main, bb, cc, lam, s5_d[l], jnp.zeros((nblk_p, ngb, 2, 8, w2), F32),
                     0, nblk_p, tb_p, tp, tp, True, y_zero)
        s0s = state_s5[:, l].astype(F32)
        s0s = s0s.reshape(bs, 2, ngb, S5_GB * S5_STATE, 2).transpose(0, 2, 1, 4, 3).reshape(bs, ngb, 2, 1, w2)
        s0s = jnp.pad(s0s, ((0, 0), (0, 0), (0, 0), (0, 7), (0, 0)))
        yb, _ = _s5(p_main, bb, cc, lam, s5_d[l], s0s, n_p // s5_tb, n_s // s5_tb, s5_tb, ts, S5_SEG, False, yb,
                    col_major=col_major)
        yb = _glu(yb, s5_glu_w, s5_glu_b, l)
        sb = sb[:, :, :, :seq_per_blk].reshape(nblk_p, ngb, 2, seq_per_blk, 2, S5_GB, S5_STATE)
        sb = sb.transpose(0, 3, 2, 1, 5, 6, 4).reshape(bp, 2, S5_GROUPS, S5_STATE, 2)
        st_s5.append(sb)

        yc, new_gdn = _gdn(p_main, p_ab, gdn_conv[l], gdn_a_log[l], gdn_dt_bias[l], gdn_norm[l], z_hgrn, 0, bp, tp,
                           y_zero, new_gdn, l, depth)
        yc, _ = _gdn(p_main, p_ab, gdn_conv[l], gdn_a_log[l], gdn_dt_bias[l], gdn_norm[l],
                     state_gdn[:, l].astype(F32), n_p // ts, bs, ts, yc, col_major=col_major)

        x, h2 = _merge(ya, yb, yc, p_mg, w_branch, w_out, norm_g[l, 1], l, x,
                       modv((l, 2), (l, 4), (l, 3), norm_g[l, 2]), n_p, ts, BF16 if l % 2 == 0 else F32)
        last = l + 1 == depth
        next_mod = modv((l, 5), None, None, None) if last else modv((l, 5), (l + 1, 1), (l + 1, 0), norm_g[l + 1, 0])

        if l % 2 == 0:
            e = l // 2
            hid = _swiglu_up(h2, ffn_wg, ffn_wu, e, ffn_wg.shape[-1] // 2)
            x, h = _down_res(hid, ffn_wd, e, norm_g[l, 3], x, next_mod, n_p, ts, BF16)
            if last:
                y_prompt, y_sample = x[:n_p], x[n_p:]
            continue
        else:
            e = l // 2
            n_tok = n_p + n_s
            max_tiles = 2 * n_tok // MOE_TILE + ne
            aux, cnt = _router(h2, jnp.pad(moe_router[e], ((0, 0), (0, LANES - ne))), ne)
            slot_tok, slot_gate, tile_e, tile_valid = _moe_plan(aux, cnt, n_tok, ne, MOE_TILE, max_tiles)
            hid = _moe_up(h2, slot_tok, tile_e, tile_valid, moe_wg, moe_wu, e, MOE_TILE)
            r_f = _moe_down(hid, slot_tok, slot_gate, tile_e, tile_valid, moe_wd, e, norm_g[l, 3], n_tok, MOE_TILE)

        if not last:
            x, h = _resnorm(x, r_f, next_mod, n_p, ts, True)
        else:
            (y_prompt,) = _resnorm(x, r_f, next_mod, n_p, ts, False, rows=(0, n_p))
            (y_sample,) = _resnorm(x, r_f, next_mod, n_p, ts, False, rows=(n_p, n_s))

    y_prompt = y_prompt.reshape(bp, tp, d)
    y_sample = y_sample.reshape(bs, ts, d)
    new_s5 = jnp.stack(st_s5, axis=1).astype(x_prompt.dtype)
    return (y_prompt, y_sample, new_hgrn.astype(x_prompt.dtype), new_s5, new_gdn.astype(x_prompt.dtype))
```

```python
import functools
import math

import jax
import jax.numpy as jnp
from jax import lax
from jax.experimental import pallas as pl
from jax.experimental.pallas import tpu as pltpu

F32 = jnp.float32
BF16 = jnp.bfloat16

EPS = 1e-6
F_FLOOR = 1e-6
LOG2E = 1.4426950408889634
LANES = 128
GRID_W = 64
HEAD_D = 128
N_HEADS = 4
BR_WIDTH = 512
S5_GROUPS = 32
S5_GROUP = 16
S5_STATE = 64
S5_GB = 8
S5_SEG = 64
GLA_CHUNK = 256
GDN_CHUNK = 64
GDN_SPAN = 256
GDN_HEADS_PER_STEP = 4
GDN_MAX_ROWS_PER_STEP = 2048
TM = 512
MOE_TILE = 256
VMEM_LIMIT = 56 * 1024 * 1024


def _cp(sem, vmem=VMEM_LIMIT):
    return pltpu.CompilerParams(dimension_semantics=sem, vmem_limit_bytes=vmem)


def _dot(a, b):
    return jnp.dot(a, b, preferred_element_type=F32)


def _dot_nt(a, b):
    return lax.dot_general(a, b, (((1,), (1,)), ((), ())), preferred_element_type=F32)


def _dot_tn(a, b):
    return lax.dot_general(a, b, (((0,), (0,)), ((), ())), preferred_element_type=F32)


def _silu(x):
    return x * jax.nn.sigmoid(x)


def _rms(x, g):
    return x * lax.rsqrt(jnp.mean(x * x, axis=-1, keepdims=True) + EPS) * g


def _chunk_cumsum(x, d, c):
    n = x.shape[0]
    pos = lax.broadcasted_iota(jnp.int32, x.shape, 0) & (c - 1)
    s = 1
    while s < c:
        if d == 0:
            x = x + jnp.where(pos >= s, pltpu.roll(x, s, 0), 0.0)
        else:
            x = x + jnp.where(pos < c - s, pltpu.roll(x, n - s, 0), 0.0)
        s *= 2
    return x


def _ada_kernel(c_ref, w_ref, b_ref, o_ref):
    c = c_ref[...]
    a = _silu(c).astype(BF16)
    o_ref[...] = _dot(a, w_ref[...].astype(BF16)) + b_ref[...]


def _ada_mod(c8, ada_w, ada_b):
    depth, d, n6 = ada_w.shape
    tn = 512
    return pl.pallas_call(
        _ada_kernel,
        grid=(depth, n6 // tn),
        in_specs=[
            pl.BlockSpec((8, d), lambda l, j: (0, 0)),
            pl.BlockSpec((None, d, tn), lambda l, j: (l, 0, j)),
            pl.BlockSpec((None, 1, tn), lambda l, j: (l, 0, j)),
        ],
        out_specs=pl.BlockSpec((None, 8, tn), lambda l, j: (l, 0, j)),
        out_shape=jax.ShapeDtypeStruct((depth, 8, n6), F32),
        compiler_params=_cp(("parallel", "parallel")),
        name="ada_mod",
    )(c8, ada_w, ada_b.reshape(depth, 1, n6))


def _resnorm_kernel(*refs, has_y, has_h):
    it = iter(refs)
    x_ref = next(it)
    y_ref = next(it) if has_y else None
    mv_ref = next(it)
    xo_ref = next(it) if has_y else None
    h_ref = next(it) if has_h else None
    x = x_ref[...]
    mv = mv_ref[...]
    if has_y:
        x = x + mv[0:1] * y_ref[...]
        xo_ref[...] = x
    if has_h:
        h = _rms(x, mv[3:4]) * (1.0 + mv[1:2]) + mv[2:3]
        h_ref[...] = h.astype(h_ref.dtype)


def _resnorm(x, y, modv, n_prompt, seq_s, has_h, rows=None, h_dtype=BF16):
    d = x.shape[1]
    row0, n = (0, x.shape[0]) if rows is None else rows
    i0 = row0 // TM
    has_y = y is not None

    def group(i):
        r = (i + i0) * TM
        return jnp.where(r < n_prompt, 0, 1 + (r - n_prompt) // seq_s)

    tile = pl.BlockSpec((TM, d), lambda i: (i + i0, 0))
    in_specs = [tile] + ([tile] if has_y else []) + [pl.BlockSpec((None, 8, d), lambda i: (group(i), 0, 0))]
    out_tile = pl.BlockSpec((TM, d), lambda i: (i, 0))
    out_specs, out_shape = [], []
    if has_y:
        out_specs.append(out_tile)
        out_shape.append(jax.ShapeDtypeStruct((n, d), F32))
    if has_h:
        out_specs.append(out_tile)
        out_shape.append(jax.ShapeDtypeStruct((n, d), h_dtype))
    args = [x] + ([y] if has_y else []) + [modv]
    outs = pl.pallas_call(
        functools.partial(_resnorm_kernel, has_y=has_y, has_h=has_h),
        grid=(n // TM,),
        in_specs=in_specs,
        out_specs=out_specs,
        out_shape=out_shape,
        compiler_params=_cp(("parallel",)),
        name="resnorm",
    )(*args)
    return outs


def _mm_kernel(x_ref, w_ref, o_ref, wb_ref):
    @pl.when(pl.program_id(1) == 0)
    def _():
        wb_ref[...] = w_ref[...].astype(BF16)

    o_ref[...] = _dot_nt(x_ref[...], wb_ref[...]).astype(o_ref.dtype)


def _mm_t(x, wt, l, nrows, tn, name, tm=TM, out_dtype=F32):
    m, k = x.shape
    if l is None:
        wspec = pl.BlockSpec((tn, k), lambda j, i: (j, 0))
    else:
        wspec = pl.BlockSpec((None, tn, k), lambda j, i: (l, j, 0))
    return pl.pallas_call(
        _mm_kernel,
        grid=(nrows // tn, m // tm),
        in_specs=[pl.BlockSpec((tm, k), lambda j, i: (i, 0)), wspec],
        out_specs=pl.BlockSpec((tm, tn), lambda j, i: (i, j)),
        out_shape=jax.ShapeDtypeStruct((m, nrows), out_dtype),
        scratch_shapes=[pltpu.VMEM((tn, k), BF16)],
        compiler_params=_cp(("arbitrary", "arbitrary")),
        name=name,
    )(x, wt)


def _gla_intra(q, k, b, d, c):
    rowi = lax.broadcasted_iota(jnp.int32, (c, LANES), 0)
    ri = lax.broadcasted_iota(jnp.int32, (c, c), 0)
    ci = lax.broadcasted_iota(jnp.int32, (c, c), 1)
    a = jnp.zeros((c, c), F32)
    size = c
    while size >= 8:
        half = size // 2
        nb = c // size
        b3 = b.reshape(nb, size, LANES)
        pos = rowi & (size - 1)
        if d == 0:
            ref = b3[:, half - 1:half, :]
            qsel = pos >= half
        else:
            ref = b3[:, half:half + 1, :]
            qsel = pos < half
        ref = jnp.broadcast_to(ref, (nb, size, LANES)).reshape(c, LANES)
        z = jnp.where(qsel, q, k) * jnp.exp2(-jnp.abs(b - ref))
        qs = jnp.where(qsel, z, 0.0).astype(BF16)
        ks = jnp.where(qsel, 0.0, z).astype(BF16)
        p = _dot_nt(qs, ks)
        if nb > 1:
            shift = int(math.log2(size))
            p = jnp.where((ri >> shift) == (ci >> shift), p, 0.0)
        a = a + p
        size = half
    b3 = b.reshape(c // 8, 8, LANES)
    lo, hi = (0, 4) if d == 0 else (3, 7)
    ref_lo = jnp.broadcast_to(b3[:, lo:lo + 1, :], (c // 8, 8, LANES)).reshape(c, LANES)
    ref_hi = jnp.broadcast_to(b3[:, hi:hi + 1, :], (c // 8, 8, LANES)).reshape(c, LANES)
    ref = jnp.where((rowi & 7) >= 4, ref_hi, ref_lo)
    qs = (q * jnp.exp2(jnp.minimum(b - ref, 0.0))).astype(BF16)
    ks = (k * jnp.exp2(ref - b)).astype(BF16)
    p = _dot_nt(qs, ks)
    keep = ((ri >> 2) == (ci >> 2)) & ((ci <= ri) if d == 0 else (ci >= ri))
    return a + jnp.where(keep, p, 0.0)


def _load_column_major(dst_ref, src_ref, t, stage=None):
    rows = t // GRID_W
    for k in range(src_ref.shape[1] // LANES):
        lanes = slice(k * LANES, (k + 1) * LANES)
        if stage is None:
            strided = src_ref
        else:
            stage[...] = src_ref[:, lanes]
            strided = stage
        for col in range(GRID_W):
            dst_ref[col * rows:(col + 1) * rows, lanes] = strided[pl.ds(col, rows, stride=GRID_W), :]


def _store_row_major(dst_ref, y, t, stage=None):
    rows = t // GRID_W
    for k in range(y.shape[1] // LANES):
        lanes = slice(k * LANES, (k + 1) * LANES)
        strided = dst_ref if stage is None else stage
        for col in range(GRID_W):
            strided[pl.ds(col, rows, stride=GRID_W), :] = y[col * rows:(col + 1) * rows, lanes]
        if stage is not None:
            dst_ref[:, lanes] = stage[...]


def _hgrn_kernel(aq_ref, af_ref, ab_ref, ai_ref, ag_ref, lb_ref, g_ref, s0_ref, y_ref, s_ref, o_acc, *perm_bufs,
                 t, c):
    if perm_bufs:
        srcs = (aq_ref, af_ref, ab_ref, ai_ref, ag_ref)
        for dst, src in zip(perm_bufs, srcs):
            _load_column_major(dst, src, t)
        aq_ref, af_ref, ab_ref, ai_ref, ag_ref = perm_bufs
    nchunk = t // c
    scale = HEAD_D ** -0.5
    for d in (0, 1):
        z_ref = af_ref if d == 0 else ab_ref
        lbd = lb_ref[d:d + 1, :]

        def chunk(i, st, d=d, z_ref=z_ref, lbd=lbd):
            cidx = i if d == 0 else nchunk - 1 - i
            r0 = pl.multiple_of(cidx * c, c)
            z = z_ref[pl.ds(r0, c), :]
            f = lbd + (1.0 - lbd) * jax.nn.sigmoid(z)
            lf = jnp.log2(jnp.maximum(f, F_FLOOR))
            k = (1.0 - lbd) * jax.nn.sigmoid(-z)
            q = _silu(aq_ref[pl.ds(r0, c), :]) * scale
            v = ai_ref[pl.ds(r0, c), :].astype(BF16)
            b = _chunk_cumsum(lf, d, c)
            a = _gla_intra(q, k, b, d, c)
            o = _dot(a.astype(BF16), v) + _dot_nt((q * jnp.exp2(b)).astype(BF16), st.astype(BF16))
            if d == 0:
                o_acc[pl.ds(r0, c), :] = o
            else:
                o_acc[pl.ds(r0, c), :] += o
            blast = b[c - 1:c, :] if d == 0 else b[0:1, :]
            kd = (k * jnp.exp2(blast - b)).astype(BF16)
            return st * jnp.exp2(blast) + _dot_tn(v, kd)

        st = s0_ref[0, d, 0].T
        if nchunk == 1:
            st = chunk(0, st)
        else:
            st = lax.fori_loop(0, nchunk, chunk, st)
        s_ref[0, d, 0] = st.T
    y = _rms(o_acc[...], g_ref[...]) * _silu(ag_ref[...])
    if perm_bufs:
        _store_row_major(y_ref, y, t)
    else:
        y_ref[...] = y


def _shared_out_call(kernel_fn, prevs, in_specs, args, **kw):
    extra = [p for p in prevs if p is not None]
    if not extra:
        return pl.pallas_call(kernel_fn, in_specs=in_specs, **kw)(*args)
    n_in = len(in_specs)
    aliases, k = {}, 0
    for out_idx, p in enumerate(prevs):
        if p is not None:
            aliases[n_in + k] = out_idx
            k += 1

    def body(*refs):
        kernel_fn(*refs[:n_in], *refs[n_in + len(extra):])

    return pl.pallas_call(body, in_specs=in_specs + [pl.BlockSpec(memory_space=pl.ANY)] * len(extra),
                          input_output_aliases=aliases, **kw)(*args, *extra)


def _state_out(nseq, slot, nslots, hg):
    spec = pl.BlockSpec((1, None, 2, hg, HEAD_D, HEAD_D), lambda s, h: (s, slot, 0, h, 0, 0))
    return spec, jax.ShapeDtypeStruct((nseq, nslots, 2, N_HEADS, HEAD_D, HEAD_D), F32)


def _hgrn(p_main, lb, norm_g, s0, row_blk0, nseq, t, y_prev=None, s_prev=None, slot=0, nslots=1,
          col_major=False):
    c = min(GLA_CHUNK, t)
    seq_buf = pltpu.VMEM((t, HEAD_D), F32)

    def col(base):
        return pl.BlockSpec((t, HEAD_D), lambda s, h: (row_blk0 + s, base + h))

    st_spec = pl.BlockSpec((1, 2, 1, HEAD_D, HEAD_D), lambda s, h: (s, 0, h, 0, 0))
    so_spec, so_shape = _state_out(nseq, slot, nslots, 1)
    return _shared_out_call(
        functools.partial(_hgrn_kernel, t=t, c=c), [y_prev, s_prev],
        [col(0), col(4), col(8), col(12), col(16),
         pl.BlockSpec((2, HEAD_D), lambda s, h: (0, h)),
         pl.BlockSpec((1, HEAD_D), lambda s, h: (0, 0)),
         st_spec],
        (p_main, p_main, p_main, p_main, p_main, lb, norm_g.reshape(1, HEAD_D), s0),
        grid=(nseq, N_HEADS),
        out_specs=[pl.BlockSpec((t, HEAD_D), lambda s, h: (row_blk0 + s, h)), so_spec],
        out_shape=[jax.ShapeDtypeStruct((p_main.shape[0], BR_WIDTH), F32), so_shape],
        scratch_shapes=[seq_buf] * (6 if col_major else 1),
        compiler_params=_cp(("parallel", "parallel")),
        name="hgrn2",
    )


def _dot_b(a, b):
    return _dot(a.astype(BF16), b.astype(BF16))


def _tri_solve(ms, xs, c):
    ri = lax.broadcasted_iota(jnp.int32, (c, c), 0)
    ci = lax.broadcasted_iota(jnp.int32, (c, c), 1)
    eye = jnp.where(ri == ci, 1.0, 0.0)
    m8 = [jnp.where((ri >> 3) == (ci >> 3), m, 0.0) for m in ms]
    ts = [eye - m for m in m8]
    ps = [_dot_b(m, m) for m in m8]
    ts = [t + _dot_b(t, p) for t, p in zip(ts, ps)]
    ps = [_dot_b(p, p) for p in ps]
    ts = [t + _dot_b(t, p) for t, p in zip(ts, ps)]
    shift = 3
    while (1 << shift) < c:
        join = ((ri >> (shift + 1)) == (ci >> (shift + 1))) & ((ri >> shift) != (ci >> shift))
        lts = [_dot_b(jnp.where(join, m, 0.0), t) for m, t in zip(ms, ts)]
        ts = [t - _dot_b(t, lt) for t, lt in zip(ts, lts)]
        shift += 1
    return [_dot_b(t, x) for t, x in zip(ts, xs)]


def _gdn_kernel(cq_ref, ck_ref, cv_ref, cg_ref, ab_ref, cw_ref, alog_ref, dtb_ref, g_ref, s0_ref,
                y_ref, s_ref, q_s, k_s, v_s, la_s, be_s, w_s, qk_s, qe_s, kd_s, dec_s, o_acc, *perm_bufs,
                t, c, span, hg):
    stage = None
    if perm_bufs:
        *perm_bufs, stage = perm_bufs
        srcs = (cq_ref, ck_ref, cv_ref, cg_ref, ab_ref)
        for dst, src in zip(perm_bufs, srcs):
            _load_column_major(dst, src, t, stage)
        cq_ref, ck_ref, cv_ref, cg_ref, ab_ref = perm_bufs
    nchunk = t // c
    cps = span // c
    rowt = lax.broadcasted_iota(jnp.int32, (t, HEAD_D), 0)
    lane = lax.broadcasted_iota(jnp.int32, (t, LANES), 1)
    ri = lax.broadcasted_iota(jnp.int32, (span, span), 0)
    ci = lax.broadcasted_iota(jnp.int32, (span, span), 1)
    shift = int(math.log2(c))
    same = (ri >> shift) == (ci >> shift)
    heads = range(hg)
    dirs = (0, 1)

    def conv(x_ref, j, hh):
        x = x_ref[:, hh * HEAD_D:(hh + 1) * HEAD_D]
        w = cw_ref[j][:, hh * HEAD_D:(hh + 1) * HEAD_D]
        xm = jnp.where(rowt == 0, 0.0, pltpu.roll(x, 1, 0))
        xp = jnp.where(rowt == t - 1, 0.0, pltpu.roll(x, t - 1, 0))
        return xm * w[0:1] + x * w[1:2] + xp * w[2:3]

    ab = ab_ref[...]
    lane1 = lax.broadcasted_iota(jnp.int32, (1, LANES), 1)
    alog_l = jnp.zeros((1, LANES), F32)
    dtb_l = jnp.zeros((1, LANES), F32)
    for d in dirs:
        for hd in range(N_HEADS):
            alog_l = jnp.where(lane1 == d * N_HEADS + hd, alog_ref[d, hd], alog_l)
            dtb_l = jnp.where(lane1 == d * N_HEADS + hd, dtb_ref[d, hd], dtb_l)
    xs = ab + dtb_l
    log_a_all = (-LOG2E) * jnp.exp(alog_l) * (jnp.maximum(xs, 0.0) + jnp.log1p(jnp.exp(-jnp.abs(xs))))
    beta_all = jax.nn.sigmoid(ab)
    for hh in heads:
        h = pl.program_id(1) * hg + hh
        q = _silu(conv(cq_ref, 0, hh))
        q_s[hh] = q * lax.rsqrt(jnp.sum(q * q, axis=-1, keepdims=True) + EPS) * (HEAD_D ** -0.5)
        k = _silu(conv(ck_ref, 1, hh))
        k_s[hh] = k * lax.rsqrt(jnp.sum(k * k, axis=-1, keepdims=True) + EPS)
        v_s[hh] = _silu(conv(cv_ref, 2, hh))
        for d in dirs:
            la_col = jnp.sum(jnp.where(lane == d * N_HEADS + h, log_a_all, 0.0), axis=-1, keepdims=True)
            be_col = jnp.sum(jnp.where(lane == 2 * N_HEADS + d * N_HEADS + h, beta_all, 0.0), axis=-1,
                             keepdims=True)
            la_s[hh, d] = jnp.broadcast_to(la_col, (t, LANES))
            be_s[hh, d] = jnp.broadcast_to(be_col, (t, LANES))
    o_acc[...] = jnp.zeros(o_acc.shape, F32)

    def wy_span(sp, carry):
        r0 = pl.multiple_of(sp * span, span)
        ms, xs, dst = [], [], []
        for hh in heads:
            qsp = q_s[hh, pl.ds(r0, span), :]
            ksp = k_s[hh, pl.ds(r0, span), :]
            vsp = v_s[hh, pl.ds(r0, span), :]
            kb = ksp.astype(BF16)
            kk = _dot_nt(kb, kb)
            qk = _dot_nt(qsp.astype(BF16), kb)
            for d in dirs:
                be = be_s[hh, d, pl.ds(r0, span), :]
                g = _chunk_cumsum(la_s[hh, d, pl.ds(r0, span), :], d, c)
                gt = g.T
                gcol = jnp.concatenate([g] * (span // LANES), axis=1)
                grow = jnp.concatenate([gt] * (span // LANES), axis=0)
                incl = same & ((ci <= ri) if d == 0 else (ci >= ri))
                gamma = jnp.where(incl, jnp.exp2(jnp.minimum(gcol - grow, 0.0)), 0.0)
                bcol = jnp.concatenate([be] * (span // LANES), axis=1)
                m_sp = jnp.where(ci == ri, 0.0, bcol * kk * gamma)
                qkg = (qk * gamma).astype(BF16)
                eg = jnp.exp2(g)
                xv = be * vsp
                xk = be * eg * ksp
                qe_s[hh, d, pl.ds(r0, span), :] = (qsp * eg).astype(BF16)
                for j in range(cps):
                    lo = j * c
                    rj = pl.multiple_of(r0 + lo, c)
                    ms.append(m_sp[lo:lo + c, lo:lo + c])
                    xs.append(jnp.concatenate([xv[lo:lo + c], xk[lo:lo + c]], axis=1))
                    dst.append((hh, d, rj))
                    qk_s[hh, d, pl.ds(rj, c), 0:c] = qkg[lo:lo + c, lo:lo + c]
                    gj = g[lo:lo + c]
                    glast = gj[c - 1:c] if d == 0 else gj[0:1]
                    kd_s[hh, d, pl.ds(rj, c), :] = (ksp[lo:lo + c] * jnp.exp2(glast - gj)).astype(BF16)
                    dec_s[hh, d, pl.ds(sp * cps + j, 1), :] = jnp.exp2(glast)
        for (hh, d, rj), w in zip(dst, _tri_solve(ms, xs, c)):
            w_s[hh, d, pl.ds(rj, c), :] = w
        return carry

    if t == span:
        wy_span(0, 0)
    else:
        lax.fori_loop(0, t // span, wy_span, 0)

    chains = [(hh, d) for hh in heads for d in dirs]

    def scan_step(i, carry):
        cidx = (i, nchunk - 1 - i)
        r0 = [pl.multiple_of(ck * c, c) for ck in cidx]
        sb = [s.astype(BF16) for s in carry]
        ws = [w_s[hh, d, pl.ds(r0[d], c), :] for hh, d in chains]
        wks = [_dot(w[:, HEAD_D:].astype(BF16), s) for w, s in zip(ws, sb)]
        oq = [_dot(qe_s[hh, d, pl.ds(r0[d], c), :], s) for (hh, d), s in zip(chains, sb)]
        vnb = [(w[:, :HEAD_D] - wk).astype(BF16) for w, wk in zip(ws, wks)]
        new = [s * dec_s[hh, d, pl.ds(cidx[d], 1), :] + _dot_tn(kd_s[hh, d, pl.ds(r0[d], c), :], v)
               for (hh, d), s, v in zip(chains, carry, vnb)]
        for (hh, d), o, v in zip(chains, oq, vnb):
            o_acc[hh, pl.ds(r0[d], c), :] += o + _dot(qk_s[hh, d, pl.ds(r0[d], c), 0:c], v)
        return tuple(new)

    finals = lax.fori_loop(0, nchunk, scan_step, tuple(s0_ref[0, d, hh] for hh, d in chains),
                           unroll=nchunk <= 4)
    for (hh, d), s in zip(chains, finals):
        s_ref[0, d, hh] = s
    ys = []
    for hh in heads:
        sl = slice(hh * HEAD_D, (hh + 1) * HEAD_D)
        ys.append(_rms(o_acc[hh], g_ref[...]) * _silu(cg_ref[:, sl]))
    y = jnp.concatenate(ys, axis=1)
    if perm_bufs:
        _store_row_major(y_ref, y, t, stage)
    else:
        y_ref[...] = y


def _gdn(p_main, p_ab, conv_w, a_log, dt_bias, norm_g, s0, row_blk0, nseq, t, y_prev=None, s_prev=None,
         slot=0, nslots=1, col_major=False):
    c = GDN_CHUNK
    hg = GDN_HEADS_PER_STEP if t * GDN_HEADS_PER_STEP <= GDN_MAX_ROWS_PER_STEP else GDN_HEADS_PER_STEP // 2
    wid = hg * HEAD_D
    blk = BR_WIDTH // wid

    def col(base):
        return pl.BlockSpec((t, wid), lambda s, h: (row_blk0 + s, base * blk + h))

    st_spec = pl.BlockSpec((1, 2, hg, HEAD_D, HEAD_D), lambda s, h: (s, 0, h, 0, 0))
    smem = pl.BlockSpec(memory_space=pltpu.SMEM)
    seq_buf = pltpu.VMEM((hg, t, HEAD_D), F32)
    dir_f32 = pltpu.VMEM((hg, 2, t, LANES), F32)
    dir_bf16 = pltpu.VMEM((hg, 2, t, LANES), BF16)
    scratch = [seq_buf, seq_buf, seq_buf, dir_f32, dir_f32,
               pltpu.VMEM((hg, 2, t, 2 * HEAD_D), F32), dir_bf16, dir_bf16, dir_bf16,
               pltpu.VMEM((hg, 2, max(t // c, 8), LANES), F32), seq_buf]
    if col_major:
        scratch += [pltpu.VMEM((t, wid), F32)] * 4 + [pltpu.VMEM((t, LANES), F32)] * 2
    so_spec, so_shape = _state_out(nseq, slot, nslots, hg)
    return _shared_out_call(
        functools.partial(_gdn_kernel, t=t, c=c, span=GDN_SPAN, hg=hg), [y_prev, s_prev],
        [col(6), col(7), col(8), col(9),
         pl.BlockSpec((t, LANES), lambda s, h: (row_blk0 + s, 0)),
         pl.BlockSpec((3, 3, wid), lambda s, h: (0, 0, h)),
         smem, smem,
         pl.BlockSpec((1, HEAD_D), lambda s, h: (0, 0)),
         st_spec],
        (p_main, p_main, p_main, p_main, p_ab, conv_w, a_log, dt_bias, norm_g.reshape(1, HEAD_D), s0),
        grid=(nseq, N_HEADS // hg),
        out_specs=[pl.BlockSpec((t, wid), lambda s, h: (row_blk0 + s, h)), so_spec],
        out_shape=[jax.ShapeDtypeStruct((p_main.shape[0], BR_WIDTH), F32), so_shape],
        scratch_shapes=scratch,
        compiler_params=_cp(("parallel", "parallel")),
        name="gdn",
    )


def _cmul(ar, ai, br, bi):
    return ar * br - ai * bi, ar * bi + ai * br


def _s5_kernel(u_ref, bb_ref, cc_ref, lam_ref, dsk_ref, s0_ref, y_ref, sf_ref, up_s, x_s, yp_s,
               *, tb, seg, segs_per_seq, zero_init, col_major):
    r_par = tb // seg
    half = S5_GB * S5_STATE
    if col_major:
        rows = tb // GRID_W
        assert segs_per_seq == r_par and seg % rows == 0

        def step_rows(r):
            return pl.ds((r % rows) * GRID_W + r // rows, r_par, stride=seg // rows)
    else:
        def step_rows(r):
            return pl.ds(r, r_par, stride=seg)

    for r in range(seg):
        up_s[r * r_par:(r + 1) * r_par, :] = u_ref[step_rows(r), :]
    ub = up_s[...].astype(BF16)
    sf_ref[...] = jnp.zeros(sf_ref.shape, F32)
    nseq = r_par // segs_per_seq
    lam = [(lam_ref[0, d, 0:1, :], lam_ref[0, d, 1:2, :]) for d in (0, 1)]
    for d in (0, 1):
        x_s[d] = _dot(ub, bb_ref[0, d])

    def offsets(i):
        return [pl.multiple_of(r * r_par, r_par) for r in (i, seg - 1 - i)]

    def local_step(i, carry):
        off = offsets(i)
        new = []
        for d in (0, 1):
            (lre, lim), (hre, him) = lam[d], carry[d]
            nre = lre * hre - lim * him + x_s[d, pl.ds(off[d], r_par), 0:half]
            nim = lre * him + lim * hre + x_s[d, pl.ds(off[d], r_par), half:2 * half]
            x_s[d, pl.ds(off[d], r_par), 0:half] = nre
            x_s[d, pl.ds(off[d], r_par), half:2 * half] = nim
            new.append((nre, nim))
        return tuple(new)

    zero = jnp.zeros((r_par, half), F32)
    lax.fori_loop(0, seg, local_step, ((zero, zero), (zero, zero)), unroll=4)

    end_rows = ((seg - 1) * r_par, 0)
    needs_carry = not (zero_init and segs_per_seq == 1)
    hins = []
    for d in (0, 1):
        if not needs_carry:
            break
        lre, lim = lam[d]
        pre, pim = lre, lim
        n = 1
        while n < seg:
            pre, pim = _cmul(pre, pim, pre, pim)
            n *= 2
        end_row = end_rows[d]
        ere = x_s[d, end_row:end_row + r_par, 0:half]
        eim = x_s[d, end_row:end_row + r_par, half:2 * half]
        hin = [None] * r_par
        order = range(r_par) if d == 0 else range(r_par - 1, -1, -1)
        for j in order:
            sq, pos = divmod(j, segs_per_seq)
            first = pos == (0 if d == 0 else segs_per_seq - 1)
            if first:
                hin[j] = (s0_ref[0, 0, d, sq:sq + 1, 0:half], s0_ref[0, 0, d, sq:sq + 1, half:2 * half])
            else:
                pj = j - 1 if d == 0 else j + 1
                cre, cim = _cmul(pre, pim, hin[pj][0], hin[pj][1])
                hin[j] = (cre + ere[pj:pj + 1], cim + eim[pj:pj + 1])
        hins.append((jnp.concatenate([hj[0] for hj in hin], axis=0),
                     jnp.concatenate([hj[1] for hj in hin], axis=0)))

    def carry_step(i, pws):
        off = offsets(i)
        new = []
        for d in (0, 1):
            (lre, lim), (wre, wim) = lam[d], pws[d]
            are, aim = _cmul(wre, wim, hins[d][0], hins[d][1])
            x_s[d, pl.ds(off[d], r_par), 0:half] += are
            x_s[d, pl.ds(off[d], r_par), half:2 * half] += aim
            new.append(_cmul(wre, wim, lre, lim))
        return tuple(new)

    if needs_carry:
        lax.fori_loop(0, seg, carry_step, (lam[0], lam[1]), unroll=4)

    for d in (0, 1):
        for sq in range(nseq):
            j = (sq + 1) * segs_per_seq - 1 if d == 0 else sq * segs_per_seq
            row = end_rows[d] + j
            sf_ref[0, 0, d, sq:sq + 1, :] = x_s[d, row:row + 1, :]
    yp_s[...] = (_dot(x_s[0].astype(BF16), cc_ref[0, 0]) + _dot(x_s[1].astype(BF16), cc_ref[0, 1])
                 + up_s[...] * dsk_ref[...])
    for r in range(seg):
        y_ref[step_rows(r), :] = yp_s[r * r_par:(r + 1) * r_par, :]


def _s5(p_main, bb, cc, lam, dskip, s0, row_blk0, nblk, tb, t, seg, zero_init, y_prev=None, col_major=False):
    ngb = S5_GROUPS // S5_GB
    w2 = 2 * S5_GB * S5_STATE
    return _shared_out_call(
        functools.partial(_s5_kernel, tb=tb, seg=seg, segs_per_seq=t // seg, zero_init=zero_init,
                          col_major=col_major), [y_prev, None],
        [
            pl.BlockSpec((tb, LANES), lambda i, gb: (row_blk0 + i, 20 + gb)),
            pl.BlockSpec((1, 2, LANES, w2), lambda i, gb: (gb, 0, 0, 0)),
            pl.BlockSpec((1, 2, w2, LANES), lambda i, gb: (gb, 0, 0, 0)),
            pl.BlockSpec((1, 2, 2, w2 // 2), lambda i, gb: (gb, 0, 0, 0)),
            pl.BlockSpec((1, LANES), lambda i, gb: (0, gb)),
            pl.BlockSpec((1, 1, 2, 8, w2), lambda i, gb: (i, gb, 0, 0, 0)),
        ],
        (p_main, bb, cc, lam, dskip.reshape(1, BR_WIDTH), s0),
        grid=(nblk, ngb),
        out_specs=[pl.BlockSpec((tb, LANES), lambda i, gb: (row_blk0 + i, gb)),
                   pl.BlockSpec((1, 1, 2, 8, w2), lambda i, gb: (i, gb, 0, 0, 0))],
        out_shape=[jax.ShapeDtypeStruct((p_main.shape[0], BR_WIDTH), F32),
                   jax.ShapeDtypeStruct((nblk, ngb, 2, 8, w2), F32)],
        scratch_shapes=[pltpu.VMEM((tb, LANES), F32), pltpu.VMEM((2, tb, w2), F32), pltpu.VMEM((tb, LANES), F32)],
        compiler_params=_cp(("parallel", "parallel")),
        name="s5",
    )


def _s5_params(a_re, a_im, log_dt, b_re, b_im, c_re, c_im):
    dt = jnp.exp(log_dt)[..., None]
    mag = jnp.exp(dt * a_re)
    lam_re = mag * jnp.cos(dt * a_im)
    lam_im = mag * jnp.sin(dt * a_im)
    den = a_re * a_re + a_im * a_im
    co_re = ((lam_re - 1.0) * a_re + lam_im * a_im) / den
    co_im = (lam_im * a_re - (lam_re - 1.0) * a_im) / den
    bb_re = co_re[..., None] * b_re - co_im[..., None] * b_im
    bb_im = co_re[..., None] * b_im + co_im[..., None] * b_re
    ngb = S5_GROUPS // S5_GB
    eye = jnp.eye(S5_GB, dtype=F32)

    def pack_in(w):
        w = w.reshape(2, ngb, S5_GB, S5_STATE, S5_GROUP)
        blk = jnp.einsum('dbgpi,gh->bdgihp', w, eye)
        return blk.reshape(ngb, 2, S5_GB * S5_GROUP, S5_GB * S5_STATE)

    def pack_out(w):
        w = w.reshape(2, ngb, S5_GB, S5_GROUP, S5_STATE)
        blk = jnp.einsum('dbgip,gh->bdgphi', w, eye)
        return blk.reshape(ngb, 2, S5_GB * S5_STATE, S5_GB * S5_GROUP)

    bb = jnp.concatenate([pack_in(bb_re), pack_in(bb_im)], axis=-1).astype(BF16)
    cc = jnp.concatenate([pack_out(c_re), -pack_out(c_im)], axis=-2).astype(BF16)
    lam = jnp.stack([lam_re, lam_im], axis=1)
    lam = lam.reshape(2, 2, ngb, S5_GB * S5_STATE).transpose(2, 0, 1, 3)
    return bb, cc, lam


def _glu_kernel(y_ref, w_ref, b_ref, o_ref, wb_ref):
    @pl.when(pl.program_id(0) == 0)
    def _():
        wb_ref[...] = w_ref[...].astype(BF16)

    z = _dot(y_ref[...].astype(BF16), wb_ref[...]) + b_ref[...]
    half = z.shape[1] // 2
    o_ref[...] = z[:, :half] * jax.nn.sigmoid(z[:, half:])


def _glu(y, w, b, l):
    n, k = y.shape
    n2 = w.shape[-1]
    return pl.pallas_call(
        _glu_kernel,
        grid=(n // TM,),
        in_specs=[pl.BlockSpec((TM, k), lambda i: (i, 0)),
                  pl.BlockSpec((None, k, n2), lambda i: (l, 0, 0)),
                  pl.BlockSpec((None, 1, n2), lambda i: (l, 0, 0))],
        out_specs=pl.BlockSpec((TM, n2 // 2), lambda i: (i, 0)),
        out_shape=jax.ShapeDtypeStruct((n, n2 // 2), F32),
        scratch_shapes=[pltpu.VMEM((k, n2), BF16)],
        compiler_params=_cp(("arbitrary",)),
        name="s5_glu",
    )(y, w, b.reshape(b.shape[0], 1, n2))


def _residual_epilogue(y, g_ref, x_ref, mv_ref, xo_ref, h_ref):
    mv = mv_ref[...]
    x = x_ref[...] + mv[0:1] * _rms(y, g_ref[...])
    xo_ref[...] = x
    h_ref[...] = (_rms(x, mv[3:4]) * (1.0 + mv[1:2]) + mv[2:3]).astype(h_ref.dtype)


def _group_of_tile(tm, n_prompt, seq_s):
    def group(i):
        return jnp.where(i * tm < n_prompt, 0, 1 + (i * tm - n_prompt) // seq_s)
    return group


def _merge_kernel(ya_ref, yb_ref, yc_ref, mg_ref, wbr_ref, wout_ref, g_ref, x_ref, mv_ref, xo_ref, h_ref,
                  wbr_s, wout_s):
    @pl.when(pl.program_id(0) == 0)
    def _():
        wbr_s[...] = wbr_ref[...].astype(BF16)
        wout_s[...] = wout_ref[...].astype(BF16)

    d = wout_s.shape[0]
    acc = None
    for n, br_ref in enumerate((ya_ref, yb_ref, yc_ref)):
        proj = _dot(br_ref[...].astype(BF16), wbr_s[n])
        term = proj * jax.nn.sigmoid(mg_ref[:, n * d:(n + 1) * d].astype(F32))
        acc = term if acc is None else acc + term
    y = _dot(acc.astype(BF16), wout_s[...])
    _residual_epilogue(y, g_ref, x_ref, mv_ref, xo_ref, h_ref)


def _merge(ya, yb, yc, mg, w_branch, w_out, g, l, x, modv, n_prompt, seq_s, h_dtype):
    n = ya.shape[0]
    d = w_out.shape[-1]
    tm = 256
    group = _group_of_tile(tm, n_prompt, seq_s)
    br = pl.BlockSpec((tm, BR_WIDTH), lambda i: (i, 0))
    tile = pl.BlockSpec((tm, d), lambda i: (i, 0))
    return pl.pallas_call(
        _merge_kernel,
        grid=(n // tm,),
        in_specs=[br, br, br,
                  pl.BlockSpec((tm, 3 * d), lambda i: (i, 0)),
                  pl.BlockSpec((None, 3, BR_WIDTH, d), lambda i: (l, 0, 0, 0)),
                  pl.BlockSpec((None, d, d), lambda i: (l, 0, 0)),
                  pl.BlockSpec((1, d), lambda i: (0, 0)),
                  tile,
                  pl.BlockSpec((None, 8, d), lambda i: (group(i), 0, 0))],
        out_specs=[tile, tile],
        out_shape=[jax.ShapeDtypeStruct((n, d), F32), jax.ShapeDtypeStruct((n, d), h_dtype)],
        scratch_shapes=[pltpu.VMEM((3, BR_WIDTH, d), BF16), pltpu.VMEM((d, d), BF16)],
        compiler_params=_cp(("arbitrary",)),
        name="merge",
    )(ya, yb, yc, mg, w_branch, w_out, g.reshape(1, d), x, modv)


def _swiglu_kernel(*refs, gated):
    if gated:
        x_ref, wg_ref, wu_ref, gt_ref, o_ref, wg_s, wu_s = refs
    else:
        x_ref, wg_ref, wu_ref, o_ref, wg_s, wu_s = refs

    @pl.when(pl.program_id(1) == 0)
    def _():
        wg_s[...] = wg_ref[...].astype(BF16)
        wu_s[...] = wu_ref[...].astype(BF16)

    x = x_ref[...]
    hid = _silu(_dot(x, wg_s[...])) * _dot(x, wu_s[...])
    if gated:
        gt = gt_ref[...]
        lane = lax.broadcasted_iota(jnp.int32, gt.shape, 1)
        hid = hid * jnp.sum(jnp.where(lane == pl.program_id(0), gt, 0.0), axis=-1, keepdims=True)
    o_ref[...] = hid.astype(BF16)


def _swiglu_up(x, wg, wu, e, f, gates=None):
    n, d = x.shape
    gated = gates is not None
    if gated:
        ne = wg.shape[1]
        wspec = pl.BlockSpec((None, None, d, f), lambda j, i: (e, j, 0, 0))
    else:
        ne = wg.shape[-1] // f
        wspec = pl.BlockSpec((None, d, f), lambda j, i: (e, 0, j))
    in_specs = [pl.BlockSpec((TM, d), lambda j, i: (i, 0)), wspec, wspec]
    args = [x, wg, wu]
    if gated:
        in_specs.append(pl.BlockSpec((TM, LANES), lambda j, i: (i, 0)))
        args.append(gates)
    return pl.pallas_call(
        functools.partial(_swiglu_kernel, gated=gated),
        grid=(ne, n // TM),
        in_specs=in_specs,
        out_specs=pl.BlockSpec((TM, f), lambda j, i: (i, j)),
        out_shape=jax.ShapeDtypeStruct((n, ne * f), BF16),
        scratch_shapes=[pltpu.VMEM((d, f), BF16), pltpu.VMEM((d, f), BF16)],
        compiler_params=_cp(("arbitrary", "arbitrary")),
        name="swiglu_up",
    )(*args)


def _down_kernel(hid_ref, w_ref, g_ref, x_ref, mv_ref, xo_ref, h_ref, w_s):
    @pl.when(pl.program_id(0) == 0)
    def _():
        w_s[...] = w_ref[...].astype(BF16)

    _residual_epilogue(_dot(hid_ref[...], w_s[...]), g_ref, x_ref, mv_ref, xo_ref, h_ref)


def _down_res(hid, wd, e, g, x, modv, n_prompt, seq_s, h_dtype):
    n, f = hid.shape
    d = wd.shape[-1]
    group = _group_of_tile(TM, n_prompt, seq_s)
    tile = pl.BlockSpec((TM, d), lambda i: (i, 0))
    return pl.pallas_call(
        _down_kernel,
        grid=(n // TM,),
        in_specs=[pl.BlockSpec((TM, f), lambda i: (i, 0)),
                  pl.BlockSpec((None, f, d), lambda i: (e, 0, 0)),
                  pl.BlockSpec((1, d), lambda i: (0, 0)),
                  tile,
                  pl.BlockSpec((None, 8, d), lambda i: (group(i), 0, 0))],
        out_specs=[tile, tile],
        out_shape=[jax.ShapeDtypeStruct((n, d), F32), jax.ShapeDtypeStruct((n, d), h_dtype)],
        scratch_shapes=[pltpu.VMEM((f, d), BF16)],
        compiler_params=_cp(("arbitrary",)),
        name="down_res",
    )(hid, wd, g.reshape(1, d), x, modv)


def _router_kernel(x_ref, w_ref, aux_ref, cnt_ref, base_s, *, ne):
    @pl.when(pl.program_id(0) == 0)
    def _():
        base_s[...] = jnp.zeros(base_s.shape, F32)

    w = w_ref[...]
    w_hi = w.astype(BF16)
    w_lo = (w - w_hi.astype(F32)).astype(BF16)
    x = x_ref[...].astype(BF16)
    logits = _dot(x, w_hi) + _dot(x, w_lo)
    tm = logits.shape[0]
    lane = lax.broadcasted_iota(jnp.int32, logits.shape, 1).astype(F32)
    neg = jnp.float32(-jnp.inf)
    logits = jnp.where(lane < ne, logits, neg)
    m1 = jnp.max(logits, axis=-1, keepdims=True)
    i1 = jnp.min(jnp.where(logits == m1, lane, float(LANES)), axis=-1, keepdims=True)
    rest = jnp.where(lane == i1, neg, logits)
    m2 = jnp.max(rest, axis=-1, keepdims=True)
    i2 = jnp.min(jnp.where(rest == m2, lane, float(LANES)), axis=-1, keepdims=True)
    e2 = jnp.exp(m2 - m1)
    w1 = 1.0 / (1.0 + e2)
    w2 = e2 / (1.0 + e2)
    sel = jnp.where((lane == i1) | (lane == i2), 1.0, 0.0)
    ri = lax.broadcasted_iota(jnp.int32, (tm, tm), 0)
    ci = lax.broadcasted_iota(jnp.int32, (tm, tm), 1)
    before = jnp.where(ci < ri, 1.0, 0.0).astype(BF16)
    rank = _dot(before, sel.astype(BF16)) + base_s[0:1, :]
    r1 = jnp.sum(jnp.where(lane == i1, rank, 0.0), axis=-1, keepdims=True)
    r2 = jnp.sum(jnp.where(lane == i2, rank, 0.0), axis=-1, keepdims=True)
    aux = jnp.zeros(logits.shape, F32)
    for k, v in enumerate((i1, i2, r1, r2, w1, w2)):
        aux = jnp.where(lane == float(k), v, aux)
    aux_ref[...] = aux
    base_s[...] = base_s[...] + jnp.sum(sel, axis=0, keepdims=True)
    cnt_ref[...] = base_s[...]


def _router(x, w_pad, ne):
    n, d = x.shape
    return pl.pallas_call(
        functools.partial(_router_kernel, ne=ne),
        grid=(n // TM,),
        in_specs=[pl.BlockSpec((TM, d), lambda i: (i, 0)),
                  pl.BlockSpec((d, LANES), lambda i: (0, 0))],
        out_specs=[pl.BlockSpec((TM, LANES), lambda i: (i, 0)),
                   pl.BlockSpec((8, LANES), lambda i: (0, 0))],
        out_shape=[jax.ShapeDtypeStruct((n, LANES), F32), jax.ShapeDtypeStruct((8, LANES), F32)],
        scratch_shapes=[pltpu.VMEM((8, LANES), F32)],
        compiler_params=_cp(("arbitrary",)),
        name="router",
    )(x, w_pad)


def _moe_plan(aux, cnt, n, ne, tile, max_tiles):
    e = aux[:, 0:2].astype(jnp.int32)
    r = aux[:, 2:4].astype(jnp.int32)
    c = cnt[0, :ne].astype(jnp.int32)
    pc = (c + tile - 1) // tile * tile
    ends = jnp.cumsum(pc)
    slot = ((ends - pc)[e] + r).reshape(-1)
    tok = jnp.broadcast_to(jnp.arange(n, dtype=F32)[:, None], (n, 2)).reshape(-1)
    m = max_tiles * tile
    pad = jnp.broadcast_to(jnp.array([float(n), 0.0], F32), (m, 2))
    table = pad.at[slot].set(jnp.stack([tok, aux[:, 4:6].reshape(-1)], axis=1))
    slot_tok = table[:, 0].astype(jnp.int32)
    slot_gate = table[:, 1]
    tstart = jnp.arange(max_tiles, dtype=jnp.int32) * tile
    tile_e = jnp.minimum(jnp.sum(tstart[:, None] >= ends[None, :], axis=1), ne - 1).astype(jnp.int32)
    tile_valid = (tstart < ends[-1]).astype(jnp.int32)
    return slot_tok.reshape(max_tiles, 1, tile), slot_gate.reshape(max_tiles, 1, tile), tile_e, tile_valid


def _expert_changed(t, te_ref):
    return jnp.logical_or(t == 0, te_ref[t] != te_ref[jnp.maximum(t - 1, 0)])


def _moe_up_kernel(te_ref, tv_ref, idx_ref, nxt_ref, h_ref, wg_ref, wu_ref, o_ref, xbuf, wg_s, wu_s, *, tile):
    t = pl.program_id(0)
    valid = tv_ref[t] == 1
    last = h_ref.shape[0] - 1
    cur = t % 2

    def gather_row(src_idx_ref, buf, j):
        xbuf[buf, pl.ds(j, 1), :] = h_ref[pl.ds(jnp.minimum(src_idx_ref[0, 0, j], last), 1), :]

    @pl.when(t == 0)
    def _():
        def first(j, carry):
            gather_row(idx_ref, 0, j)
            return carry

        lax.fori_loop(0, tile, first, 0, unroll=8)

    @pl.when(jnp.logical_and(valid, _expert_changed(t, te_ref)))
    def _():
        wg_s[...] = wg_ref[...].astype(BF16)
        wu_s[...] = wu_ref[...].astype(BF16)

    @pl.when(valid)
    def _():
        x = xbuf[cur].astype(BF16)
        o_ref[...] = (_silu(_dot(x, wg_s[...])) * _dot(x, wu_s[...])).astype(BF16)
        for j in range(tile):
            gather_row(nxt_ref, 1 - cur, j)

    @pl.when(jnp.logical_not(valid))
    def _():
        o_ref[...] = jnp.zeros(o_ref.shape, BF16)


def _moe_up(h, slot_tok, tile_e, tile_valid, wg, wu, e_layer, tile):
    max_tiles = slot_tok.shape[0]
    _, _, d, f = wg.shape
    wspec = pl.BlockSpec((None, None, d, f), lambda t, te, tv: (e_layer, te[t], 0, 0),
                         pipeline_mode=pl.Buffered(1))
    grid_spec = pltpu.PrefetchScalarGridSpec(
        num_scalar_prefetch=2,
        grid=(max_tiles,),
        in_specs=[pl.BlockSpec((1, 1, tile), lambda t, te, tv: (t, 0, 0), memory_space=pltpu.SMEM),
                  pl.BlockSpec((1, 1, tile), lambda t, te, tv: (jnp.minimum(t + 1, max_tiles - 1), 0, 0),
                               memory_space=pltpu.SMEM),
                  pl.BlockSpec(h.shape, lambda t, te, tv: (0, 0), pipeline_mode=pl.Buffered(1)),
                  wspec, wspec],
        out_specs=pl.BlockSpec((tile, f), lambda t, te, tv: (t, 0)),
        scratch_shapes=[pltpu.VMEM((2, tile, d), F32),
                        pltpu.VMEM((d, f), BF16), pltpu.VMEM((d, f), BF16)],
    )
    return pl.pallas_call(
        functools.partial(_moe_up_kernel, tile=tile),
        grid_spec=grid_spec,
        out_shape=jax.ShapeDtypeStruct((max_tiles * tile, f), BF16),
        compiler_params=_cp(("arbitrary",)),
        name="moe_up",
    )(tile_e, tile_valid, slot_tok, slot_tok, h, wg, wu)


def _moe_down_kernel(te_ref, tv_ref, idx_ref, gt_ref, hid_ref, wd_ref, g_ref, o_ref, acc, ybuf, tbuf, wd_s,
                     *, tile, n_tiles, tm_out):
    t = pl.program_id(0)
    tt = jnp.minimum(t, n_tiles - 1)

    @pl.when(t == 0)
    def _():
        acc[...] = jnp.zeros(acc.shape, F32)

    is_tile = jnp.logical_and(t < n_tiles, tv_ref[tt] == 1)

    @pl.when(jnp.logical_and(is_tile, _expert_changed(tt, te_ref)))
    def _():
        wd_s[...] = wd_ref[...].astype(BF16)

    @pl.when(is_tile)
    def _():
        for j in range(tile):
            tbuf[j:j + 1, :] = acc[pl.ds(idx_ref[0, 0, j], 1), :]
        ybuf[...] = _dot(hid_ref[...], wd_s[...])

        def put(j, carry):
            acc[pl.ds(idx_ref[0, 0, j], 1), :] = tbuf[pl.ds(j, 1), :] + gt_ref[0, 0, j] * ybuf[pl.ds(j, 1), :]
            return carry

        lax.fori_loop(0, tile, put, 0, unroll=8)

    @pl.when(t >= n_tiles)
    def _():
        r0 = pl.multiple_of((t - n_tiles) * tm_out, tm_out)
        o_ref[...] = _rms(acc[pl.ds(r0, tm_out), :], g_ref[...])


def _moe_down(hid, slot_tok, slot_gate, tile_e, tile_valid, wd, e_layer, g, n, tile):
    n_tiles = slot_tok.shape[0]
    _, _, f, d = wd.shape
    tm_out = TM
    n_flush = n // tm_out

    def tix(t):
        return jnp.minimum(t, n_tiles - 1)

    smem_tile = pl.BlockSpec((1, 1, tile), lambda t, te, tv: (tix(t), 0, 0), memory_space=pltpu.SMEM)
    grid_spec = pltpu.PrefetchScalarGridSpec(
        num_scalar_prefetch=2,
        grid=(n_tiles + n_flush,),
        in_specs=[smem_tile, smem_tile,
                  pl.BlockSpec((tile, f), lambda t, te, tv: (tix(t), 0)),
                  pl.BlockSpec((None, None, f, d), lambda t, te, tv: (e_layer, te[tix(t)], 0, 0)),
                  pl.BlockSpec((1, d), lambda t, te, tv: (0, 0))],
        out_specs=pl.BlockSpec((tm_out, d), lambda t, te, tv: (jnp.maximum(t - n_tiles, 0), 0)),
        scratch_shapes=[pltpu.VMEM((n + 8, d), F32), pltpu.VMEM((tile, d), F32), pltpu.VMEM((tile, d), F32),
                        pltpu.VMEM((f, d), BF16)],
    )
    return pl.pallas_call(
        functools.partial(_moe_down_kernel, tile=tile, n_tiles=n_tiles, tm_out=tm_out),
        grid_spec=grid_spec,
        out_shape=jax.ShapeDtypeStruct((n, d), F32),
        compiler_params=_cp(("arbitrary",)),
        name="moe_down",
    )(tile_e, tile_valid, slot_tok, slot_gate, hid, wd, g.reshape(1, d))


def kernel(x_prompt, x_sample, state_hgrn, state_s5, state_gdn, c, c_ctx, ada_w, ada_b, norm_g, w_in, hgrn_lb, hgrn_norm, s5_a_re, s5_a_im, s5_log_dt, s5_b_re, s5_b_im, s5_c_re, s5_c_im, s5_d, s5_glu_w, s5_glu_b, gdn_conv, gdn_a_log, gdn_dt_bias, gdn_norm, w_branch, w_out, ffn_wg, ffn_wu, ffn_wd, moe_router, moe_wg, moe_wu, moe_wd):
    bp, tp, d = x_prompt.shape
    bs, ts, _ = x_sample.shape
    depth = w_in.shape[0]
    n_p, n_s = bp * tp, bs * ts
    ne = moe_router.shape[-1]
    main_cols = 10 * BR_WIDTH
    ab_cols = 4 * N_HEADS

    c8 = jnp.zeros((8, d), F32).at[0].set(c_ctx).at[1:1 + bs].set(c)
    mod_all = _ada_mod(c8, ada_w, ada_b)

    lb_soft = jax.nn.softmax(hgrn_lb.astype(F32), axis=0)
    lb_all = jnp.cumsum(lb_soft, axis=0) - lb_soft[:1]

    mods = mod_all[:, :1 + bs].reshape(depth, 1 + bs, 6, d)
    zero_row = jnp.zeros((1 + bs, d), F32)

    def modv(gate, scale, shift, g):
        rows = [zero_row if sel is None else mods[sel[0], :, sel[1]] for sel in (gate, scale, shift)]
        rows.append(zero_row if g is None else jnp.broadcast_to(g, (1 + bs, d)))
        return jnp.stack(rows + [zero_row] * 4, axis=1)

    x = jnp.concatenate([x_prompt.reshape(n_p, d), x_sample.reshape(n_s, d)], axis=0)
    (h,) = _resnorm(x, None, modv(None, (0, 1), (0, 0), norm_g[0, 0]), n_p, ts, True)

    z_hgrn = jnp.zeros((bp, 2, N_HEADS, HEAD_D, HEAD_D), F32)
    ngb = S5_GROUPS // S5_GB
    w2 = 2 * S5_GB * S5_STATE
    st_s5 = []
    y_zero = jnp.zeros((n_p + n_s, BR_WIDTH), F32)
    new_hgrn = new_gdn = jnp.zeros((bp, depth, 2, N_HEADS, HEAD_D, HEAD_D), F32)
    w_in_t = jnp.swapaxes(w_in, 1, 2)
    s5_tb = 1024

    for l in range(depth):
        col_major = l % 2 == 1
        p_main = _mm_t(h, w_in_t, l, main_cols, 2 * BR_WIDTH, "w_in_main", tm=1024)
        w_ab = jnp.pad(w_in_t[l, main_cols:main_cols + ab_cols], ((0, LANES - ab_cols), (0, 0)))
        p_ab = _mm_t(h, w_ab, None, LANES, LANES, "w_in_ab", tm=1024)
        p_mg = _mm_t(h, w_in_t[l, main_cols + ab_cols:], None, 3 * d, 2 * BR_WIDTH, "w_in_merge", tm=1024,
                     out_dtype=BF16)

        ya, new_hgrn = _hgrn(p_main, lb_all[l], hgrn_norm[l], z_hgrn, 0, bp, tp, y_zero, new_hgrn, l, depth)
        ya, _ = _hgrn(p_main, lb_all[l], hgrn_norm[l], state_hgrn[:, l].astype(F32), n_p // ts, bs, ts, ya,
                      col_major=col_major)

        bb, cc, lam = _s5_params(s5_a_re[l], s5_a_im[l], s5_log_dt[l], s5_b_re[l], s5_b_im[l],
                                 s5_c_re[l], s5_c_im[l])
        seq_per_blk = 8
        tb_p = seq_per_blk * tp
        nblk_p = n_p // tb_p
        yb, sb = _s5(p_main, bb, cc, lam, s5_d[l], jnp.zeros((nblk_p, ngb, 2, 8, w2), F32),
                     0, nblk_p, tb_p, tp, tp, True, y_zero)
        s0s = state_s5[:, l].astype(F32)
        s0s = s0s.reshape(bs, 2, ngb, S5_GB * S5_STATE, 2).transpose(0, 2, 1, 4, 3).reshape(bs, ngb, 2, 1, w2)
        s0s = jnp.pad(s0s, ((0, 0), (0, 0), (0, 0), (0, 7), (0, 0)))
        yb, _ = _s5(p_main, bb, cc, lam, s5_d[l], s0s, n_p // s5_tb, n_s // s5_tb, s5_tb, ts, S5_SEG, False, yb,
                    col_major=col_major)
        yb = _glu(yb, s5_glu_w, s5_glu_b, l)
        sb = sb[:, :, :, :seq_per_blk].reshape(nblk_p, ngb, 2, seq_per_blk, 2, S5_GB, S5_STATE)
        sb = sb.transpose(0, 3, 2, 1, 5, 6, 4).reshape(bp, 2, S5_GROUPS, S5_STATE, 2)
        st_s5.append(sb)

        yc, new_gdn = _gdn(p_main, p_ab, gdn_conv[l], gdn_a_log[l], gdn_dt_bias[l], gdn_norm[l], z_hgrn, 0, bp, tp,
                           y_zero, new_gdn, l, depth)
        yc, _ = _gdn(p_main, p_ab, gdn_conv[l], gdn_a_log[l], gdn_dt_bias[l], gdn_norm[l],
                     state_gdn[:, l].astype(F32), n_p // ts, bs, ts, yc, col_major=col_major)

        x, h2 = _merge(ya, yb, yc, p_mg, w_branch, w_out, norm_g[l, 1], l, x,
                       modv((l, 2), (l, 4), (l, 3), norm_g[l, 2]), n_p, ts, BF16 if l % 2 == 0 else F32)
        last = l + 1 == depth
        next_mod = modv((l, 5), None, None, None) if last else modv((l, 5), (l + 1, 1), (l + 1, 0), norm_g[l + 1, 0])

        if l % 2 == 0:
            e = l // 2
            hid = _swiglu_up(h2, ffn_wg, ffn_wu, e, ffn_wg.shape[-1] // 2)
            x, h = _down_res(hid, ffn_wd, e, norm_g[l, 3], x, next_mod, n_p, ts, BF16)
            if last:
                y_prompt, y_sample = x[:n_p], x[n_p:]
            continue
        else:
            e = l // 2
            n_tok = n_p + n_s
            max_tiles = 2 * n_tok // MOE_TILE + ne
            aux, cnt = _router(h2, jnp.pad(moe_router[e], ((0, 0), (0, LANES - ne))), ne)
            slot_tok, slot_gate, tile_e, tile_valid = _moe_plan(aux, cnt, n_tok, ne, MOE_TILE, max_tiles)
            hid = _moe_up(h2, slot_tok, tile_e, tile_valid, moe_wg, moe_wu, e, MOE_TILE)
            r_f = _moe_down(hid, slot_tok, slot_gate, tile_e, tile_valid, moe_wd, e, norm_g[l, 3], n_tok, MOE_TILE)

        if not last:
            x, h = _resnorm(x, r_f, next_mod, n_p, ts, True)
        else:
            (y_prompt,) = _resnorm(x, r_f, next_mod, n_p, ts, False, rows=(0, n_p))
            (y_sample,) = _resnorm(x, r_f, next_mod, n_p, ts, False, rows=(n_p, n_s))

    y_prompt = y_prompt.reshape(bp, tp, d)
    y_sample = y_sample.reshape(bs, ts, d)
    new_s5 = jnp.stack(st_s5, axis=1).astype(x_prompt.dtype)
    return (y_prompt, y_sample, new_hgrn.astype(x_prompt.dtype), new_s5, new_gdn.astype(x_prompt.dtype))
```

```python
import functools
import math

import jax
import jax.numpy as jnp
from jax import lax
from jax.experimental import pallas as pl
from jax.experimental.pallas import tpu as pltpu

F32 = jnp.float32
BF16 = jnp.bfloat16

EPS = 1e-6
F_FLOOR = 1e-6
LOG2E = 1.4426950408889634
LANES = 128
GRID_W = 64
HEAD_D = 128
N_HEADS = 4
BR_WIDTH = 512
S5_GROUPS = 32
S5_GROUP = 16
S5_STATE = 64
S5_GB = 8
S5_SEG = 64
GLA_CHUNK = 256
GDN_CHUNK = 64
GDN_SPAN = 256
GDN_HEADS_PER_STEP = 4
GDN_MAX_ROWS_PER_STEP = 2048
TM = 512
MOE_TILE = 256
VMEM_LIMIT = 56 * 1024 * 1024


def _cp(sem, vmem=VMEM_LIMIT):
    return pltpu.CompilerParams(dimension_semantics=sem, vmem_limit_bytes=vmem)


def _dot(a, b):
    return jnp.dot(a, b, preferred_element_type=F32)


def _dot_nt(a, b):
    return lax.dot_general(a, b, (((1,), (1,)), ((), ())), preferred_element_type=F32)


def _dot_tn(a, b):
    return lax.dot_general(a, b, (((0,), (0,)), ((), ())), preferred_element_type=F32)


def _silu(x):
    return x * jax.nn.sigmoid(x)


def _rms(x, g):
    return x * lax.rsqrt(jnp.mean(x * x, axis=-1, keepdims=True) + EPS) * g


def _chunk_cumsum(x, d, c):
    n = x.shape[0]
    pos = lax.broadcasted_iota(jnp.int32, x.shape, 0) & (c - 1)
    s = 1
    while s < c:
        if d == 0:
            x = x + jnp.where(pos >= s, pltpu.roll(x, s, 0), 0.0)
        else:
            x = x + jnp.where(pos < c - s, pltpu.roll(x, n - s, 0), 0.0)
        s *= 2
    return x


def _ada_kernel(c_ref, w_ref, b_ref, o_ref):
    c = c_ref[...]
    a = _silu(c).astype(BF16)
    o_ref[...] = _dot(a, w_ref[...].astype(BF16)) + b_ref[...]


def _ada_mod(c8, ada_w, ada_b):
    depth, d, n6 = ada_w.shape
    tn = 512
    return pl.pallas_call(
        _ada_kernel,
        grid=(depth, n6 // tn),
        in_specs=[
            pl.BlockSpec((8, d), lambda l, j: (0, 0)),
            pl.BlockSpec((None, d, tn), lambda l, j: (l, 0, j)),
            pl.BlockSpec((None, 1, tn), lambda l, j: (l, 0, j)),
        ],
        out_specs=pl.BlockSpec((None, 8, tn), lambda l, j: (l, 0, j)),
        out_shape=jax.ShapeDtypeStruct((depth, 8, n6), F32),
        compiler_params=_cp(("parallel", "parallel")),
        name="ada_mod",
    )(c8, ada_w, ada_b.reshape(depth, 1, n6))


def _resnorm_kernel(*refs, has_y, has_h):
    it = iter(refs)
    x_ref = next(it)
    y_ref = next(it) if has_y else None
    mv_ref = next(it)
    xo_ref = next(it) if has_y else None
    h_ref = next(it) if has_h else None
    x = x_ref[...]
    mv = mv_ref[...]
    if has_y:
        x = x + mv[0:1] * y_ref[...]
        xo_ref[...] = x
    if has_h:
        h = _rms(x, mv[3:4]) * (1.0 + mv[1:2]) + mv[2:3]
        h_ref[...] = h.astype(h_ref.dtype)


def _resnorm(x, y, modv, n_prompt, seq_s, has_h, rows=None, h_dtype=BF16):
    d = x.shape[1]
    row0, n = (0, x.shape[0]) if rows is None else rows
    i0 = row0 // TM
    has_y = y is not None

    def group(i):
        r = (i + i0) * TM
        return jnp.where(r < n_prompt, 0, 1 + (r - n_prompt) // seq_s)

    tile = pl.BlockSpec((TM, d), lambda i: (i + i0, 0))
    in_specs = [tile] + ([tile] if has_y else []) + [pl.BlockSpec((None, 8, d), lambda i: (group(i), 0, 0))]
    out_tile = pl.BlockSpec((TM, d), lambda i: (i, 0))
    out_specs, out_shape = [], []
    if has_y:
        out_specs.append(out_tile)
        out_shape.append(jax.ShapeDtypeStruct((n, d), F32))
    if has_h:
        out_specs.append(out_tile)
        out_shape.append(jax.ShapeDtypeStruct((n, d), h_dtype))
    args = [x] + ([y] if has_y else []) + [modv]
    outs = pl.pallas_call(
        functools.partial(_resnorm_kernel, has_y=has_y, has_h=has_h),
        grid=(n // TM,),
        in_specs=in_specs,
        out_specs=out_specs,
        out_shape=out_shape,
        compiler_params=_cp(("parallel",)),
        name="resnorm",
    )(*args)
    return outs


def _mm_kernel(x_ref, w_ref, o_ref, wb_ref):
    @pl.when(pl.program_id(1) == 0)
    def _():
        wb_ref[...] = w_ref[...].astype(BF16)

    o_ref[...] = _dot_nt(x_ref[...], wb_ref[...]).astype(o_ref.dtype)


def _mm_t(x, wt, l, nrows, tn, name, tm=TM, out_dtype=F32):
    m, k = x.shape
    if l is None:
        wspec = pl.BlockSpec((tn, k), lambda j, i: (j, 0))
    else:
        wspec = pl.BlockSpec((None, tn, k), lambda j, i: (l, j, 0))
    return pl.pallas_call(
        _mm_kernel,
        grid=(nrows // tn, m // tm),
        in_specs=[pl.BlockSpec((tm, k), lambda j, i: (i, 0)), wspec],
        out_specs=pl.BlockSpec((tm, tn), lambda j, i: (i, j)),
        out_shape=jax.ShapeDtypeStruct((m, nrows), out_dtype),
        scratch_shapes=[pltpu.VMEM((tn, k), BF16)],
        compiler_params=_cp(("arbitrary", "arbitrary")),
        name=name,
    )(x, wt)


def _gla_intra(q, k, b, d, c):
    rowi = lax.broadcasted_iota(jnp.int32, (c, LANES), 0)
    ri = lax.broadcasted_iota(jnp.int32, (c, c), 0)
    ci = lax.broadcasted_iota(jnp.int32, (c, c), 1)
    a = jnp.zeros((c, c), F32)
    size = c
    while size >= 8:
        half = size // 2
        nb = c // size
        b3 = b.reshape(nb, size, LANES)
        pos = rowi & (size - 1)
        if d == 0:
            ref = b3[:, half - 1:half, :]
            qsel = pos >= half
        else:
            ref = b3[:, half:half + 1, :]
            qsel = pos < half
        ref = jnp.broadcast_to(ref, (nb, size, LANES)).reshape(c, LANES)
        z = jnp.where(qsel, q, k) * jnp.exp2(-jnp.abs(b - ref))
        qs = jnp.where(qsel, z, 0.0).astype(BF16)
        ks = jnp.where(qsel, 0.0, z).astype(BF16)
        p = _dot_nt(qs, ks)
        if nb > 1:
            shift = int(math.log2(size))
            p = jnp.where((ri >> shift) == (ci >> shift), p, 0.0)
        a = a + p
        size = half
    b3 = b.reshape(c // 8, 8, LANES)
    lo, hi = (0, 4) if d == 0 else (3, 7)
    ref_lo = jnp.broadcast_to(b3[:, lo:lo + 1, :], (c // 8, 8, LANES)).reshape(c, LANES)
    ref_hi = jnp.broadcast_to(b3[:, hi:hi + 1, :], (c // 8, 8, LANES)).reshape(c, LANES)
    ref = jnp.where((rowi & 7) >= 4, ref_hi, ref_lo)
    qs = (q * jnp.exp2(jnp.minimum(b - ref, 0.0))).astype(BF16)
    ks = (k * jnp.exp2(ref - b)).astype(BF16)
    p = _dot_nt(qs, ks)
    keep = ((ri >> 2) == (ci >> 2)) & ((ci <= ri) if d == 0 else (ci >= ri))
    return a + jnp.where(keep, p, 0.0)


def _load_column_major(dst_ref, src_ref, t, stage=None):
    rows = t // GRID_W
    for k in range(src_ref.shape[1] // LANES):
        lanes = slice(k * LANES, (k + 1) * LANES)
        if stage is None:
            strided = src_ref
        else:
            stage[...] = src_ref[:, lanes]
            strided = stage
        for col in range(GRID_W):
            dst_ref[col * rows:(col + 1) * rows, lanes] = strided[pl.ds(col, rows, stride=GRID_W), :]


def _store_row_major(dst_ref, y, t, stage=None):
    rows = t // GRID_W
    for k in range(y.shape[1] // LANES):
        lanes = slice(k * LANES, (k + 1) * LANES)
        strided = dst_ref if stage is None else stage
        for col in range(GRID_W):
            strided[pl.ds(col, rows, stride=GRID_W), :] = y[col * rows:(col + 1) * rows, lanes]
        if stage is not None:
            dst_ref[:, lanes] = stage[...].astype(dst_ref.dtype)


def _hgrn_kernel(aq_ref, af_ref, ab_ref, ai_ref, ag_ref, lb_ref, g_ref, s0_ref, y_ref, s_ref, o_acc, *perm_bufs,
                 t, c):
    stage = None
    if perm_bufs:
        *perm_bufs, stage = perm_bufs
        srcs = (aq_ref, af_ref, ab_ref, ai_ref, ag_ref)
        for dst, src in zip(perm_bufs, srcs):
            _load_column_major(dst, src, t)
        aq_ref, af_ref, ab_ref, ai_ref, ag_ref = perm_bufs
    nchunk = t // c
    scale = HEAD_D ** -0.5
    for d in (0, 1):
        z_ref = af_ref if d == 0 else ab_ref
        lbd = lb_ref[d:d + 1, :]

        def chunk(i, st, d=d, z_ref=z_ref, lbd=lbd):
            cidx = i if d == 0 else nchunk - 1 - i
            r0 = pl.multiple_of(cidx * c, c)
            z = z_ref[pl.ds(r0, c), :]
            f = lbd + (1.0 - lbd) * jax.nn.sigmoid(z)
            lf = jnp.log2(jnp.maximum(f, F_FLOOR))
            k = (1.0 - lbd) * jax.nn.sigmoid(-z)
            q = _silu(aq_ref[pl.ds(r0, c), :]) * scale
            v = ai_ref[pl.ds(r0, c), :].astype(BF16)
            b = _chunk_cumsum(lf, d, c)
            a = _gla_intra(q, k, b, d, c)
            o = _dot(a.astype(BF16), v) + _dot_nt((q * jnp.exp2(b)).astype(BF16), st.astype(BF16))
            if d == 0:
                o_acc[pl.ds(r0, c), :] = o
            else:
                o_acc[pl.ds(r0, c), :] += o
            blast = b[c - 1:c, :] if d == 0 else b[0:1, :]
            kd = (k * jnp.exp2(blast - b)).astype(BF16)
            return st * jnp.exp2(blast) + _dot_tn(v, kd)

        st = s0_ref[0, d, 0].T
        if nchunk == 1:
            st = chunk(0, st)
        else:
            st = lax.fori_loop(0, nchunk, chunk, st)
        s_ref[0, d, 0] = st.T
    y = _rms(o_acc[...], g_ref[...]) * _silu(ag_ref[...])
    if perm_bufs:
        _store_row_major(y_ref, y, t, stage)
    else:
        y_ref[...] = y.astype(y_ref.dtype)


def _shared_out_call(kernel_fn, prevs, in_specs, args, **kw):
    extra = [p for p in prevs if p is not None]
    if not extra:
        return pl.pallas_call(kernel_fn, in_specs=in_specs, **kw)(*args)
    n_in = len(in_specs)
    aliases, k = {}, 0
    for out_idx, p in enumerate(prevs):
        if p is not None:
            aliases[n_in + k] = out_idx
            k += 1

    def body(*refs):
        kernel_fn(*refs[:n_in], *refs[n_in + len(extra):])

    return pl.pallas_call(body, in_specs=in_specs + [pl.BlockSpec(memory_space=pl.ANY)] * len(extra),
                          input_output_aliases=aliases, **kw)(*args, *extra)


def _state_out(nseq, slot, nslots, hg):
    spec = pl.BlockSpec((1, None, 2, hg, HEAD_D, HEAD_D), lambda s, h: (s, slot, 0, h, 0, 0))
    return spec, jax.ShapeDtypeStruct((nseq, nslots, 2, N_HEADS, HEAD_D, HEAD_D), F32)


def _hgrn(p_main, lb, norm_g, s0, row_blk0, nseq, t, y_prev, col0, s_prev=None, slot=0, nslots=1,
          col_major=False):
    c = min(GLA_CHUNK, t)
    seq_buf = pltpu.VMEM((t, HEAD_D), F32)

    def col(base):
        return pl.BlockSpec((t, HEAD_D), lambda s, h: (row_blk0 + s, base + h))

    st_spec = pl.BlockSpec((1, 2, 1, HEAD_D, HEAD_D), lambda s, h: (s, 0, h, 0, 0))
    so_spec, so_shape = _state_out(nseq, slot, nslots, 1)
    return _shared_out_call(
        functools.partial(_hgrn_kernel, t=t, c=c), [y_prev, s_prev],
        [col(0), col(4), col(8), col(12), col(16),
         pl.BlockSpec((2, HEAD_D), lambda s, h: (0, h)),
         pl.BlockSpec((1, HEAD_D), lambda s, h: (0, 0)),
         st_spec],
        (p_main, p_main, p_main, p_main, p_main, lb, norm_g.reshape(1, HEAD_D), s0),
        grid=(nseq, N_HEADS),
        out_specs=[pl.BlockSpec((t, HEAD_D), lambda s, h: (row_blk0 + s, col0 + h)), so_spec],
        out_shape=[jax.ShapeDtypeStruct(y_prev.shape, y_prev.dtype), so_shape],
        scratch_shapes=[seq_buf] * (7 if col_major else 1),
        compiler_params=_cp(("parallel", "parallel")),
        name="hgrn2",
    )


def _dot_b(a, b):
    return _dot(a.astype(BF16), b.astype(BF16))


def _tri_solve(ms, xs, c):
    ri = lax.broadcasted_iota(jnp.int32, (c, c), 0)
    ci = lax.broadcasted_iota(jnp.int32, (c, c), 1)
    eye = jnp.where(ri == ci, 1.0, 0.0)
    m8 = [jnp.where((ri >> 3) == (ci >> 3), m, 0.0) for m in ms]
    ts = [eye - m for m in m8]
    ps = [_dot_b(m, m) for m in m8]
    ts = [t + _dot_b(t, p) for t, p in zip(ts, ps)]
    ps = [_dot_b(p, p) for p in ps]
    ts = [t + _dot_b(t, p) for t, p in zip(ts, ps)]
    shift = 3
    while (1 << shift) < c:
        join = ((ri >> (shift + 1)) == (ci >> (shift + 1))) & ((ri >> shift) != (ci >> shift))
        lts = [_dot_b(jnp.where(join, m, 0.0), t) for m, t in zip(ms, ts)]
        ts = [t - _dot_b(t, lt) for t, lt in zip(ts, lts)]
        shift += 1
    return [_dot_b(t, x) for t, x in zip(ts, xs)]


def _gdn_kernel(cq_ref, ck_ref, cv_ref, cg_ref, ab_ref, cw_ref, alog_ref, dtb_ref, g_ref, s0_ref,
                y_ref, s_ref, q_s, k_s, v_s, la_s, be_s, w_s, qk_s, qe_s, kd_s, dec_s, o_acc, *perm_bufs,
                t, c, span, hg):
    stage = None
    if perm_bufs:
        *perm_bufs, stage = perm_bufs
        srcs = (cq_ref, ck_ref, cv_ref, cg_ref, ab_ref)
        for dst, src in zip(perm_bufs, srcs):
            _load_column_major(dst, src, t, stage)
        cq_ref, ck_ref, cv_ref, cg_ref, ab_ref = perm_bufs
    nchunk = t // c
    cps = span // c
    rowt = lax.broadcasted_iota(jnp.int32, (t, HEAD_D), 0)
    lane = lax.broadcasted_iota(jnp.int32, (t, LANES), 1)
    ri = lax.broadcasted_iota(jnp.int32, (span, span), 0)
    ci = lax.broadcasted_iota(jnp.int32, (span, span), 1)
    shift = int(math.log2(c))
    same = (ri >> shift) == (ci >> shift)
    heads = range(hg)
    dirs = (0, 1)

    def conv(x_ref, j, hh):
        x = x_ref[:, hh * HEAD_D:(hh + 1) * HEAD_D]
        w = cw_ref[j][:, hh * HEAD_D:(hh + 1) * HEAD_D]
        xm = jnp.where(rowt == 0, 0.0, pltpu.roll(x, 1, 0))
        xp = jnp.where(rowt == t - 1, 0.0, pltpu.roll(x, t - 1, 0))
        return xm * w[0:1] + x * w[1:2] + xp * w[2:3]

    ab = ab_ref[...]
    lane1 = lax.broadcasted_iota(jnp.int32, (1, LANES), 1)
    alog_l = jnp.zeros((1, LANES), F32)
    dtb_l = jnp.zeros((1, LANES), F32)
    for d in dirs:
        for hd in range(N_HEADS):
            alog_l = jnp.where(lane1 == d * N_HEADS + hd, alog_ref[d, hd], alog_l)
            dtb_l = jnp.where(lane1 == d * N_HEADS + hd, dtb_ref[d, hd], dtb_l)
    xs = ab + dtb_l
    log_a_all = (-LOG2E) * jnp.exp(alog_l) * (jnp.maximum(xs, 0.0) + jnp.log1p(jnp.exp(-jnp.abs(xs))))
    beta_all = jax.nn.sigmoid(ab)
    for hh in heads:
        h = pl.program_id(1) * hg + hh
        q = _silu(conv(cq_ref, 0, hh))
        q_s[hh] = q * lax.rsqrt(jnp.sum(q * q, axis=-1, keepdims=True) + EPS) * (HEAD_D ** -0.5)
        k = _silu(conv(ck_ref, 1, hh))
        k_s[hh] = k * lax.rsqrt(jnp.sum(k * k, axis=-1, keepdims=True) + EPS)
        v_s[hh] = _silu(conv(cv_ref, 2, hh))
        for d in dirs:
            la_col = jnp.sum(jnp.where(lane == d * N_HEADS + h, log_a_all, 0.0), axis=-1, keepdims=True)
            be_col = jnp.sum(jnp.where(lane == 2 * N_HEADS + d * N_HEADS + h, beta_all, 0.0), axis=-1,
                             keepdims=True)
            la_s[hh, d] = jnp.broadcast_to(la_col, (t, LANES))
            be_s[hh, d] = jnp.broadcast_to(be_col, (t, LANES))
    o_acc[...] = jnp.zeros(o_acc.shape, F32)

    def wy_span(sp, carry):
        r0 = pl.multiple_of(sp * span, span)
        ms, xs, dst = [], [], []
        for hh in heads:
            qsp = q_s[hh, pl.ds(r0, span), :]
            ksp = k_s[hh, pl.ds(r0, span), :]
            vsp = v_s[hh, pl.ds(r0, span), :]
            kb = ksp.astype(BF16)
            kk = _dot_nt(kb, kb)
            qk = _dot_nt(qsp.astype(BF16), kb)
            for d in dirs:
                be = be_s[hh, d, pl.ds(r0, span), :]
                g = _chunk_cumsum(la_s[hh, d, pl.ds(r0, span), :], d, c)
                gt = g.T
                gcol = jnp.concatenate([g] * (span // LANES), axis=1)
                grow = jnp.concatenate([gt] * (span // LANES), axis=0)
                incl = same & ((ci <= ri) if d == 0 else (ci >= ri))
                gamma = jnp.where(incl, jnp.exp2(jnp.minimum(gcol - grow, 0.0)), 0.0)
                bcol = jnp.concatenate([be] * (span // LANES), axis=1)
                m_sp = jnp.where(ci == ri, 0.0, bcol * kk * gamma)
                qkg = (qk * gamma).astype(BF16)
                eg = jnp.exp2(g)
                xv = be * vsp
                xk = be * eg * ksp
                qe_s[hh, d, pl.ds(r0, span), :] = (qsp * eg).astype(BF16)
                for j in range(cps):
                    lo = j * c
                    rj = pl.multiple_of(r0 + lo, c)
                    ms.append(m_sp[lo:lo + c, lo:lo + c])
                    xs.append(jnp.concatenate([xv[lo:lo + c], xk[lo:lo + c]], axis=1))
                    dst.append((hh, d, rj))
                    qk_s[hh, d, pl.ds(rj, c), 0:c] = qkg[lo:lo + c, lo:lo + c]
                    gj = g[lo:lo + c]
                    glast = gj[c - 1:c] if d == 0 else gj[0:1]
                    kd_s[hh, d, pl.ds(rj, c), :] = (ksp[lo:lo + c] * jnp.exp2(glast - gj)).astype(BF16)
                    dec_s[hh, d, pl.ds(sp * cps + j, 1), :] = jnp.exp2(glast)
        for (hh, d, rj), w in zip(dst, _tri_solve(ms, xs, c)):
            w_s[hh, d, pl.ds(rj, c), :] = w
        return carry

    if t == span:
        wy_span(0, 0)
    else:
        lax.fori_loop(0, t // span, wy_span, 0)

    chains = [(hh, d) for hh in heads for d in dirs]

    def scan_step(i, carry):
        cidx = (i, nchunk - 1 - i)
        r0 = [pl.multiple_of(ck * c, c) for ck in cidx]
        sb = [s.astype(BF16) for s in carry]
        ws = [w_s[hh, d, pl.ds(r0[d], c), :] for hh, d in chains]
        wks = [_dot(w[:, HEAD_D:].astype(BF16), s) for w, s in zip(ws, sb)]
        oq = [_dot(qe_s[hh, d, pl.ds(r0[d], c), :], s) for (hh, d), s in zip(chains, sb)]
        vnb = [(w[:, :HEAD_D] - wk).astype(BF16) for w, wk in zip(ws, wks)]
        new = [s * dec_s[hh, d, pl.ds(cidx[d], 1), :] + _dot_tn(kd_s[hh, d, pl.ds(r0[d], c), :], v)
               for (hh, d), s, v in zip(chains, carry, vnb)]
        for (hh, d), o, v in zip(chains, oq, vnb):
            o_acc[hh, pl.ds(r0[d], c), :] += o + _dot(qk_s[hh, d, pl.ds(r0[d], c), 0:c], v)
        return tuple(new)

    finals = lax.fori_loop(0, nchunk, scan_step, tuple(s0_ref[0, d, hh] for hh, d in chains),
                           unroll=nchunk <= 4)
    for (hh, d), s in zip(chains, finals):
        s_ref[0, d, hh] = s
    ys = []
    for hh in heads:
        sl = slice(hh * HEAD_D, (hh + 1) * HEAD_D)
        ys.append(_rms(o_acc[hh], g_ref[...]) * _silu(cg_ref[:, sl]))
    y = jnp.concatenate(ys, axis=1)
    if perm_bufs:
        _store_row_major(y_ref, y, t, stage)
    else:
        y_ref[...] = y.astype(y_ref.dtype)


def _gdn(p_main, p_ab, conv_w, a_log, dt_bias, norm_g, s0, row_blk0, nseq, t, y_prev, col0, s_prev=None,
         slot=0, nslots=1, col_major=False):
    c = GDN_CHUNK
    hg = GDN_HEADS_PER_STEP if t * GDN_HEADS_PER_STEP <= GDN_MAX_ROWS_PER_STEP else GDN_HEADS_PER_STEP // 2
    wid = hg * HEAD_D
    blk = BR_WIDTH // wid

    def col(base):
        return pl.BlockSpec((t, wid), lambda s, h: (row_blk0 + s, base * blk + h))

    st_spec = pl.BlockSpec((1, 2, hg, HEAD_D, HEAD_D), lambda s, h: (s, 0, h, 0, 0))
    smem = pl.BlockSpec(memory_space=pltpu.SMEM)
    seq_buf = pltpu.VMEM((hg, t, HEAD_D), F32)
    dir_f32 = pltpu.VMEM((hg, 2, t, LANES), F32)
    dir_bf16 = pltpu.VMEM((hg, 2, t, LANES), BF16)
    scratch = [seq_buf, seq_buf, seq_buf, dir_f32, dir_f32,
               pltpu.VMEM((hg, 2, t, 2 * HEAD_D), F32), dir_bf16, dir_bf16, dir_bf16,
               pltpu.VMEM((hg, 2, max(t // c, 8), LANES), F32), seq_buf]
    if col_major:
        scratch += [pltpu.VMEM((t, wid), F32)] * 4 + [pltpu.VMEM((t, LANES), F32)] * 2
    so_spec, so_shape = _state_out(nseq, slot, nslots, hg)
    return _shared_out_call(
        functools.partial(_gdn_kernel, t=t, c=c, span=GDN_SPAN, hg=hg), [y_prev, s_prev],
        [col(6), col(7), col(8), col(9),
         pl.BlockSpec((t, LANES), lambda s, h: (row_blk0 + s, 0)),
         pl.BlockSpec((3, 3, wid), lambda s, h: (0, 0, h)),
         smem, smem,
         pl.BlockSpec((1, HEAD_D), lambda s, h: (0, 0)),
         st_spec],
        (p_main, p_main, p_main, p_main, p_ab, conv_w, a_log, dt_bias, norm_g.reshape(1, HEAD_D), s0),
        grid=(nseq, N_HEADS // hg),
        out_specs=[pl.BlockSpec((t, wid), lambda s, h: (row_blk0 + s, col0 * LANES // wid + h)), so_spec],
        out_shape=[jax.ShapeDtypeStruct(y_prev.shape, y_prev.dtype), so_shape],
        scratch_shapes=scratch,
        compiler_params=_cp(("parallel", "parallel")),
        name="gdn",
    )


def _cmul(ar, ai, br, bi):
    return ar * br - ai * bi, ar * bi + ai * br


def _s5_kernel(u_ref, bb_ref, cc_ref, lam_ref, dsk_ref, s0_ref, y_ref, sf_ref, up_s, x_s, yp_s,
               *, tb, seg, segs_per_seq, zero_init, col_major):
    r_par = tb // seg
    half = S5_GB * S5_STATE
    if col_major:
        rows = tb // GRID_W
        assert segs_per_seq == r_par and seg % rows == 0

        def step_rows(r):
            return pl.ds((r % rows) * GRID_W + r // rows, r_par, stride=seg // rows)
    else:
        def step_rows(r):
            return pl.ds(r, r_par, stride=seg)

    for r in range(seg):
        up_s[r * r_par:(r + 1) * r_par, :] = u_ref[step_rows(r), :]
    ub = up_s[...].astype(BF16)
    sf_ref[...] = jnp.zeros(sf_ref.shape, F32)
    nseq = r_par // segs_per_seq
    lam = [(lam_ref[0, d, 0:1, :], lam_ref[0, d, 1:2, :]) for d in (0, 1)]
    for d in (0, 1):
        x_s[d] = _dot(ub, bb_ref[0, d])

    def offsets(i):
        return [pl.multiple_of(r * r_par, r_par) for r in (i, seg - 1 - i)]

    def local_step(i, carry):
        off = offsets(i)
        new = []
        for d in (0, 1):
            (lre, lim), (hre, him) = lam[d], carry[d]
            nre = lre * hre - lim * him + x_s[d, pl.ds(off[d], r_par), 0:half]
            nim = lre * him + lim * hre + x_s[d, pl.ds(off[d], r_par), half:2 * half]
            x_s[d, pl.ds(off[d], r_par), 0:half] = nre
            x_s[d, pl.ds(off[d], r_par), half:2 * half] = nim
            new.append((nre, nim))
        return tuple(new)

    zero = jnp.zeros((r_par, half), F32)
    lax.fori_loop(0, seg, local_step, ((zero, zero), (zero, zero)), unroll=4)

    end_rows = ((seg - 1) * r_par, 0)
    needs_carry = not (zero_init and segs_per_seq == 1)
    hins = []
    for d in (0, 1):
        if not needs_carry:
            break
        lre, lim = lam[d]
        pre, pim = lre, lim
        n = 1
        while n < seg:
            pre, pim = _cmul(pre, pim, pre, pim)
            n *= 2
        end_row = end_rows[d]
        ere = x_s[d, end_row:end_row + r_par, 0:half]
        eim = x_s[d, end_row:end_row + r_par, half:2 * half]
        hin = [None] * r_par
        order = range(r_par) if d == 0 else range(r_par - 1, -1, -1)
        for j in order:
            sq, pos = divmod(j, segs_per_seq)
            first = pos == (0 if d == 0 else segs_per_seq - 1)
            if first:
                hin[j] = (s0_ref[0, 0, d, sq:sq + 1, 0:half], s0_ref[0, 0, d, sq:sq + 1, half:2 * half])
            else:
                pj = j - 1 if d == 0 else j + 1
                cre, cim = _cmul(pre, pim, hin[pj][0], hin[pj][1])
                hin[j] = (cre + ere[pj:pj + 1], cim + eim[pj:pj + 1])
        hins.append((jnp.concatenate([hj[0] for hj in hin], axis=0),
                     jnp.concatenate([hj[1] for hj in hin], axis=0)))

    def carry_step(i, pws):
        off = offsets(i)
        new = []
        for d in (0, 1):
            (lre, lim), (wre, wim) = lam[d], pws[d]
            are, aim = _cmul(wre, wim, hins[d][0], hins[d][1])
            x_s[d, pl.ds(off[d], r_par), 0:half] += are
            x_s[d, pl.ds(off[d], r_par), half:2 * half] += aim
            new.append(_cmul(wre, wim, lre, lim))
        return tuple(new)

    if needs_carry:
        lax.fori_loop(0, seg, carry_step, (lam[0], lam[1]), unroll=4)

    for d in (0, 1):
        for sq in range(nseq):
            j = (sq + 1) * segs_per_seq - 1 if d == 0 else sq * segs_per_seq
            row = end_rows[d] + j
            sf_ref[0, 0, d, sq:sq + 1, :] = x_s[d, row:row + 1, :]
    yp_s[...] = (_dot(x_s[0].astype(BF16), cc_ref[0, 0]) + _dot(x_s[1].astype(BF16), cc_ref[0, 1])
                 + up_s[...] * dsk_ref[...])
    for r in range(seg):
        up_s[step_rows(r), :] = yp_s[r * r_par:(r + 1) * r_par, :]
    y_ref[...] = up_s[...].astype(y_ref.dtype)


def _s5(p_main, bb, cc, lam, dskip, s0, row_blk0, nblk, tb, t, seg, zero_init, y_prev, col0, col_major=False):
    ngb = S5_GROUPS // S5_GB
    w2 = 2 * S5_GB * S5_STATE
    return _shared_out_call(
        functools.partial(_s5_kernel, tb=tb, seg=seg, segs_per_seq=t // seg, zero_init=zero_init,
                          col_major=col_major), [y_prev, None],
        [
            pl.BlockSpec((tb, LANES), lambda i, gb: (row_blk0 + i, 20 + gb)),
            pl.BlockSpec((1, 2, LANES, w2), lambda i, gb: (gb, 0, 0, 0)),
            pl.BlockSpec((1, 2, w2, LANES), lambda i, gb: (gb, 0, 0, 0)),
            pl.BlockSpec((1, 2, 2, w2 // 2), lambda i, gb: (gb, 0, 0, 0)),
            pl.BlockSpec((1, LANES), lambda i, gb: (0, gb)),
            pl.BlockSpec((1, 1, 2, 8, w2), lambda i, gb: (i, gb, 0, 0, 0)),
        ],
        (p_main, bb, cc, lam, dskip.reshape(1, BR_WIDTH), s0),
        grid=(nblk, ngb),
        out_specs=[pl.BlockSpec((tb, LANES), lambda i, gb: (row_blk0 + i, col0 + gb)),
                   pl.BlockSpec((1, 1, 2, 8, w2), lambda i, gb: (i, gb, 0, 0, 0))],
        out_shape=[jax.ShapeDtypeStruct(y_prev.shape, y_prev.dtype),
                   jax.ShapeDtypeStruct((nblk, ngb, 2, 8, w2), F32)],
        scratch_shapes=[pltpu.VMEM((tb, LANES), F32), pltpu.VMEM((2, tb, w2), F32), pltpu.VMEM((tb, LANES), F32)],
        compiler_params=_cp(("parallel", "parallel")),
        name="s5",
    )


def _s5_params(a_re, a_im, log_dt, b_re, b_im, c_re, c_im):
    dt = jnp.exp(log_dt)[..., None]
    mag = jnp.exp(dt * a_re)
    lam_re = mag * jnp.cos(dt * a_im)
    lam_im = mag * jnp.sin(dt * a_im)
    den = a_re * a_re + a_im * a_im
    co_re = ((lam_re - 1.0) * a_re + lam_im * a_im) / den
    co_im = (lam_im * a_re - (lam_re - 1.0) * a_im) / den
    bb_re = co_re[..., None] * b_re - co_im[..., None] * b_im
    bb_im = co_re[..., None] * b_im + co_im[..., None] * b_re
    ngb = S5_GROUPS // S5_GB
    eye = jnp.eye(S5_GB, dtype=F32)

    def pack_in(w):
        w = w.reshape(2, ngb, S5_GB, S5_STATE, S5_GROUP)
        blk = jnp.einsum('dbgpi,gh->bdgihp', w, eye)
        return blk.reshape(ngb, 2, S5_GB * S5_GROUP, S5_GB * S5_STATE)

    def pack_out(w):
        w = w.reshape(2, ngb, S5_GB, S5_GROUP, S5_STATE)
        blk = jnp.einsum('dbgip,gh->bdgphi', w, eye)
        return blk.reshape(ngb, 2, S5_GB * S5_STATE, S5_GB * S5_GROUP)

    bb = jnp.concatenate([pack_in(bb_re), pack_in(bb_im)], axis=-1).astype(BF16)
    cc = jnp.concatenate([pack_out(c_re), -pack_out(c_im)], axis=-2).astype(BF16)
    lam = jnp.stack([lam_re, lam_im], axis=1)
    lam = lam.reshape(2, 2, ngb, S5_GB * S5_STATE).transpose(2, 0, 1, 3)
    return bb, cc, lam


def _glu_kernel(y_ref, w_ref, b_ref, o_ref, wb_ref):
    @pl.when(pl.program_id(0) == 0)
    def _():
        wb_ref[...] = w_ref[...].astype(BF16)

    z = _dot(y_ref[...].astype(BF16), wb_ref[...]) + b_ref[...]
    half = z.shape[1] // 2
    o_ref[...] = (z[:, :half] * jax.nn.sigmoid(z[:, half:])).astype(o_ref.dtype)


def _glu(y, w, b, l, blk):
    n = y.shape[0]
    k, n2 = w.shape[-2:]
    assert n2 // 2 == k
    spec = pl.BlockSpec((TM, k), lambda i: (i, blk))
    return pl.pallas_call(
        _glu_kernel,
        grid=(n // TM,),
        in_specs=[spec,
                  pl.BlockSpec((None, k, n2), lambda i: (l, 0, 0)),
                  pl.BlockSpec((None, 1, n2), lambda i: (l, 0, 0))],
        out_specs=spec,
        out_shape=jax.ShapeDtypeStruct(y.shape, y.dtype),
        input_output_aliases={0: 0},
        scratch_shapes=[pltpu.VMEM((k, n2), BF16)],
        compiler_params=_cp(("arbitrary",)),
        name="s5_glu",
    )(y, w, b.reshape(b.shape[0], 1, n2))


def _residual_epilogue(y, g_ref, x_ref, mv_ref, xo_ref, h_ref):
    mv = mv_ref[...]
    x = x_ref[...] + mv[0:1] * _rms(y, g_ref[...])
    xo_ref[...] = x
    h_ref[...] = (_rms(x, mv[3:4]) * (1.0 + mv[1:2]) + mv[2:3]).astype(h_ref.dtype)


def _group_of_tile(tm, n_prompt, seq_s):
    def group(i):
        return jnp.where(i * tm < n_prompt, 0, 1 + (i * tm - n_prompt) // seq_s)
    return group


def _merge_kernel(ya_ref, yb_ref, yc_ref, mg_ref, wbr_ref, wout_ref, g_ref, x_ref, mv_ref, xo_ref, h_ref,
                  wbr_s, wout_s):
    @pl.when(pl.program_id(0) == 0)
    def _():
        wbr_s[...] = wbr_ref[...].astype(BF16)
        wout_s[...] = wout_ref[...].astype(BF16)

    d = wout_s.shape[0]
    acc = None
    for n, br_ref in enumerate((ya_ref, yb_ref, yc_ref)):
        proj = _dot(br_ref[...].astype(BF16), wbr_s[n])
        term = proj * jax.nn.sigmoid(mg_ref[:, n * d:(n + 1) * d].astype(F32))
        acc = term if acc is None else acc + term
    y = _dot(acc.astype(BF16), wout_s[...])
    _residual_epilogue(y, g_ref, x_ref, mv_ref, xo_ref, h_ref)


def _merge(branches, mg, w_branch, w_out, g, l, x, modv, n_prompt, seq_s, h_dtype):
    n = branches.shape[0]
    d = w_out.shape[-1]
    tm = 256
    group = _group_of_tile(tm, n_prompt, seq_s)
    tile = pl.BlockSpec((tm, d), lambda i: (i, 0))
    return pl.pallas_call(
        _merge_kernel,
        grid=(n // tm,),
        in_specs=[pl.BlockSpec((tm, BR_WIDTH), lambda i: (i, 0)),
                  pl.BlockSpec((tm, BR_WIDTH), lambda i: (i, 1)),
                  pl.BlockSpec((tm, BR_WIDTH), lambda i: (i, 2)),
                  pl.BlockSpec((tm, 3 * d), lambda i: (i, 0)),
                  pl.BlockSpec((None, 3, BR_WIDTH, d), lambda i: (l, 0, 0, 0)),
                  pl.BlockSpec((None, d, d), lambda i: (l, 0, 0)),
                  pl.BlockSpec((1, d), lambda i: (0, 0)),
                  tile,
                  pl.BlockSpec((None, 8, d), lambda i: (group(i), 0, 0))],
        out_specs=[tile, tile],
        out_shape=[jax.ShapeDtypeStruct((n, d), F32), jax.ShapeDtypeStruct((n, d), h_dtype)],
        scratch_shapes=[pltpu.VMEM((3, BR_WIDTH, d), BF16), pltpu.VMEM((d, d), BF16)],
        compiler_params=_cp(("arbitrary",)),
        name="merge",
    )(branches, branches, branches, mg, w_branch, w_out, g.reshape(1, d), x, modv)


def _swiglu_kernel(*refs, gated):
    if gated:
        x_ref, wg_ref, wu_ref, gt_ref, o_ref, wg_s, wu_s = refs
    else:
        x_ref, wg_ref, wu_ref, o_ref, wg_s, wu_s = refs

    @pl.when(pl.program_id(1) == 0)
    def _():
        wg_s[...] = wg_ref[...].astype(BF16)
        wu_s[...] = wu_ref[...].astype(BF16)

    x = x_ref[...]
    hid = _silu(_dot(x, wg_s[...])) * _dot(x, wu_s[...])
    if gated:
        gt = gt_ref[...]
        lane = lax.broadcasted_iota(jnp.int32, gt.shape, 1)
        hid = hid * jnp.sum(jnp.where(lane == pl.program_id(0), gt, 0.0), axis=-1, keepdims=True)
    o_ref[...] = hid.astype(BF16)


def _swiglu_up(x, wg, wu, e, f, gates=None):
    n, d = x.shape
    gated = gates is not None
    if gated:
        ne = wg.shape[1]
        wspec = pl.BlockSpec((None, None, d, f), lambda j, i: (e, j, 0, 0))
    else:
        ne = wg.shape[-1] // f
        wspec = pl.BlockSpec((None, d, f), lambda j, i: (e, 0, j))
    in_specs = [pl.BlockSpec((TM, d), lambda j, i: (i, 0)), wspec, wspec]
    args = [x, wg, wu]
    if gated:
        in_specs.append(pl.BlockSpec((TM, LANES), lambda j, i: (i, 0)))
        args.append(gates)
    return pl.pallas_call(
        functools.partial(_swiglu_kernel, gated=gated),
        grid=(ne, n // TM),
        in_specs=in_specs,
        out_specs=pl.BlockSpec((TM, f), lambda j, i: (i, j)),
        out_shape=jax.ShapeDtypeStruct((n, ne * f), BF16),
        scratch_shapes=[pltpu.VMEM((d, f), BF16), pltpu.VMEM((d, f), BF16)],
        compiler_params=_cp(("arbitrary", "arbitrary")),
        name="swiglu_up",
    )(*args)


def _down_kernel(hid_ref, w_ref, g_ref, x_ref, mv_ref, xo_ref, h_ref, w_s):
    @pl.when(pl.program_id(0) == 0)
    def _():
        w_s[...] = w_ref[...].astype(BF16)

    _residual_epilogue(_dot(hid_ref[...], w_s[...]), g_ref, x_ref, mv_ref, xo_ref, h_ref)


def _down_res(hid, wd, e, g, x, modv, n_prompt, seq_s, h_dtype):
    n, f = hid.shape
    d = wd.shape[-1]
    group = _group_of_tile(TM, n_prompt, seq_s)
    tile = pl.BlockSpec((TM, d), lambda i: (i, 0))
    return pl.pallas_call(
        _down_kernel,
        grid=(n // TM,),
        in_specs=[pl.BlockSpec((TM, f), lambda i: (i, 0)),
                  pl.BlockSpec((None, f, d), lambda i: (e, 0, 0)),
                  pl.BlockSpec((1, d), lambda i: (0, 0)),
                  tile,
                  pl.BlockSpec((None, 8, d), lambda i: (group(i), 0, 0))],
        out_specs=[tile, tile],
        out_shape=[jax.ShapeDtypeStruct((n, d), F32), jax.ShapeDtypeStruct((n, d), h_dtype)],
        scratch_shapes=[pltpu.VMEM((f, d), BF16)],
        compiler_params=_cp(("arbitrary",)),
        name="down_res",
    )(hid, wd, g.reshape(1, d), x, modv)


def _router_kernel(x_ref, w_ref, aux_ref, cnt_ref, base_s, *, ne):
    @pl.when(pl.program_id(0) == 0)
    def _():
        base_s[...] = jnp.zeros(base_s.shape, F32)

    w = w_ref[...]
    w_hi = w.astype(BF16)
    w_lo = (w - w_hi.astype(F32)).astype(BF16)
    x = x_ref[...].astype(BF16)
    logits = _dot(x, w_hi) + _dot(x, w_lo)
    tm = logits.shape[0]
    lane = lax.broadcasted_iota(jnp.int32, logits.shape, 1).astype(F32)
    neg = jnp.float32(-jnp.inf)
    logits = jnp.where(lane < ne, logits, neg)
    m1 = jnp.max(logits, axis=-1, keepdims=True)
    i1 = jnp.min(jnp.where(logits == m1, lane, float(LANES)), axis=-1, keepdims=True)
    rest = jnp.where(lane == i1, neg, logits)
    m2 = jnp.max(rest, axis=-1, keepdims=True)
    i2 = jnp.min(jnp.where(rest == m2, lane, float(LANES)), axis=-1, keepdims=True)
    e2 = jnp.exp(m2 - m1)
    w1 = 1.0 / (1.0 + e2)
    w2 = e2 / (1.0 + e2)
    sel = jnp.where((lane == i1) | (lane == i2), 1.0, 0.0)
    ri = lax.broadcasted_iota(jnp.int32, (tm, tm), 0)
    ci = lax.broadcasted_iota(jnp.int32, (tm, tm), 1)
    before = jnp.where(ci < ri, 1.0, 0.0).astype(BF16)
    rank = _dot(before, sel.astype(BF16)) + base_s[0:1, :]
    r1 = jnp.sum(jnp.where(lane == i1, rank, 0.0), axis=-1, keepdims=True)
    r2 = jnp.sum(jnp.where(lane == i2, rank, 0.0), axis=-1, keepdims=True)
    aux = jnp.zeros(logits.shape, F32)
    for k, v in enumerate((i1, i2, r1, r2, w1, w2)):
        aux = jnp.where(lane == float(k), v, aux)
    aux_ref[...] = aux
    base_s[...] = base_s[...] + jnp.sum(sel, axis=0, keepdims=True)
    cnt_ref[...] = base_s[...]


def _router(x, w_pad, ne):
    n, d = x.shape
    return pl.pallas_call(
        functools.partial(_router_kernel, ne=ne),
        grid=(n // TM,),
        in_specs=[pl.BlockSpec((TM, d), lambda i: (i, 0)),
                  pl.BlockSpec((d, LANES), lambda i: (0, 0))],
        out_specs=[pl.BlockSpec((TM, LANES), lambda i: (i, 0)),
                   pl.BlockSpec((8, LANES), lambda i: (0, 0))],
        out_shape=[jax.ShapeDtypeStruct((n, LANES), F32), jax.ShapeDtypeStruct((8, LANES), F32)],
        scratch_shapes=[pltpu.VMEM((8, LANES), F32)],
        compiler_params=_cp(("arbitrary",)),
        name="router",
    )(x, w_pad)


def _moe_plan(aux, cnt, n, ne, tile, max_tiles):
    e = aux[:, 0:2].astype(jnp.int32)
    r = aux[:, 2:4].astype(jnp.int32)
    c = cnt[0, :ne].astype(jnp.int32)
    pc = (c + tile - 1) // tile * tile
    ends = jnp.cumsum(pc)
    slot = ((ends - pc)[e] + r).reshape(-1)
    tok = jnp.broadcast_to(jnp.arange(n, dtype=F32)[:, None], (n, 2)).reshape(-1)
    m = max_tiles * tile
    pad = jnp.broadcast_to(jnp.array([float(n), 0.0], F32), (m, 2))
    table = pad.at[slot].set(jnp.stack([tok, aux[:, 4:6].reshape(-1)], axis=1))
    slot_tok = table[:, 0].astype(jnp.int32)
    slot_gate = table[:, 1]
    tstart = jnp.arange(max_tiles, dtype=jnp.int32) * tile
    tile_e = jnp.minimum(jnp.sum(tstart[:, None] >= ends[None, :], axis=1), ne - 1).astype(jnp.int32)
    tile_valid = (tstart < ends[-1]).astype(jnp.int32)
    return slot_tok.reshape(max_tiles, 1, tile), slot_gate.reshape(max_tiles, 1, tile), tile_e, tile_valid


def _expert_changed(t, te_ref):
    return jnp.logical_or(t == 0, te_ref[t] != te_ref[jnp.maximum(t - 1, 0)])


def _moe_up_kernel(te_ref, tv_ref, idx_ref, nxt_ref, h_ref, wg_ref, wu_ref, o_ref, xbuf, wg_s, wu_s, *, tile):
    t = pl.program_id(0)
    valid = tv_ref[t] == 1
    last = h_ref.shape[0] - 1
    cur = t % 2

    def gather_row(src_idx_ref, buf, j):
        xbuf[buf, pl.ds(j, 1), :] = h_ref[pl.ds(jnp.minimum(src_idx_ref[0, 0, j], last), 1), :]

    @pl.when(t == 0)
    def _():
        def first(j, carry):
            gather_row(idx_ref, 0, j)
            return carry

        lax.fori_loop(0, tile, first, 0, unroll=8)

    @pl.when(jnp.logical_and(valid, _expert_changed(t, te_ref)))
    def _():
        wg_s[...] = wg_ref[...].astype(BF16)
        wu_s[...] = wu_ref[...].astype(BF16)

    @pl.when(valid)
    def _():
        x = xbuf[cur].astype(BF16)
        o_ref[...] = (_silu(_dot(x, wg_s[...])) * _dot(x, wu_s[...])).astype(BF16)
        for j in range(tile):
            gather_row(nxt_ref, 1 - cur, j)

    @pl.when(jnp.logical_not(valid))
    def _():
        o_ref[...] = jnp.zeros(o_ref.shape, BF16)


def _moe_up(h, slot_tok, tile_e, tile_valid, wg, wu, e_layer, tile):
    max_tiles = slot_tok.shape[0]
    _, _, d, f = wg.shape
    wspec = pl.BlockSpec((None, None, d, f), lambda t, te, tv: (e_layer, te[t], 0, 0),
                         pipeline_mode=pl.Buffered(1))
    grid_spec = pltpu.PrefetchScalarGridSpec(
        num_scalar_prefetch=2,
        grid=(max_tiles,),
        in_specs=[pl.BlockSpec((1, 1, tile), lambda t, te, tv: (t, 0, 0), memory_space=pltpu.SMEM),
                  pl.BlockSpec((1, 1, tile), lambda t, te, tv: (jnp.minimum(t + 1, max_tiles - 1), 0, 0),
                               memory_space=pltpu.SMEM),
                  pl.BlockSpec(h.shape, lambda t, te, tv: (0, 0), pipeline_mode=pl.Buffered(1)),
                  wspec, wspec],
        out_specs=pl.BlockSpec((tile, f), lambda t, te, tv: (t, 0)),
        scratch_shapes=[pltpu.VMEM((2, tile, d), F32),
                        pltpu.VMEM((d, f), BF16), pltpu.VMEM((d, f), BF16)],
    )
    return pl.pallas_call(
        functools.partial(_moe_up_kernel, tile=tile),
        grid_spec=grid_spec,
        out_shape=jax.ShapeDtypeStruct((max_tiles * tile, f), BF16),
        compiler_params=_cp(("arbitrary",)),
        name="moe_up",
    )(tile_e, tile_valid, slot_tok, slot_tok, h, wg, wu)


def _moe_down_kernel(te_ref, tv_ref, idx_ref, gt_ref, hid_ref, wd_ref, g_ref, o_ref, acc, ybuf, tbuf, wd_s,
                     *, tile, n_tiles, tm_out):
    t = pl.program_id(0)
    tt = jnp.minimum(t, n_tiles - 1)

    @pl.when(t == 0)
    def _():
        acc[...] = jnp.zeros(acc.shape, F32)

    is_tile = jnp.logical_and(t < n_tiles, tv_ref[tt] == 1)

    @pl.when(jnp.logical_and(is_tile, _expert_changed(tt, te_ref)))
    def _():
        wd_s[...] = wd_ref[...].astype(BF16)

    @pl.when(is_tile)
    def _():
        for j in range(tile):
            tbuf[j:j + 1, :] = acc[pl.ds(idx_ref[0, 0, j], 1), :]
        ybuf[...] = _dot(hid_ref[...], wd_s[...])

        def put(j, carry):
            acc[pl.ds(idx_ref[0, 0, j], 1), :] = tbuf[pl.ds(j, 1), :] + gt_ref[0, 0, j] * ybuf[pl.ds(j, 1), :]
            return carry

        lax.fori_loop(0, tile, put, 0, unroll=8)

    @pl.when(t >= n_tiles)
    def _():
        r0 = pl.multiple_of((t - n_tiles) * tm_out, tm_out)
        o_ref[...] = _rms(acc[pl.ds(r0, tm_out), :], g_ref[...])


def _moe_down(hid, slot_tok, slot_gate, tile_e, tile_valid, wd, e_layer, g, n, tile):
    n_tiles = slot_tok.shape[0]
    _, _, f, d = wd.shape
    tm_out = TM
    n_flush = n // tm_out

    def tix(t):
        return jnp.minimum(t, n_tiles - 1)

    smem_tile = pl.BlockSpec((1, 1, tile), lambda t, te, tv: (tix(t), 0, 0), memory_space=pltpu.SMEM)
    grid_spec = pltpu.PrefetchScalarGridSpec(
        num_scalar_prefetch=2,
        grid=(n_tiles + n_flush,),
        in_specs=[smem_tile, smem_tile,
                  pl.BlockSpec((tile, f), lambda t, te, tv: (tix(t), 0)),
                  pl.BlockSpec((None, None, f, d), lambda t, te, tv: (e_layer, te[tix(t)], 0, 0)),
                  pl.BlockSpec((1, d), lambda t, te, tv: (0, 0))],
        out_specs=pl.BlockSpec((tm_out, d), lambda t, te, tv: (jnp.maximum(t - n_tiles, 0), 0)),
        scratch_shapes=[pltpu.VMEM((n + 8, d), F32), pltpu.VMEM((tile, d), F32), pltpu.VMEM((tile, d), F32),
                        pltpu.VMEM((f, d), BF16)],
    )
    return pl.pallas_call(
        functools.partial(_moe_down_kernel, tile=tile, n_tiles=n_tiles, tm_out=tm_out),
        grid_spec=grid_spec,
        out_shape=jax.ShapeDtypeStruct((n, d), F32),
        compiler_params=_cp(("arbitrary",)),
        name="moe_down",
    )(tile_e, tile_valid, slot_tok, slot_gate, hid, wd, g.reshape(1, d))


def kernel(x_prompt, x_sample, state_hgrn, state_s5, state_gdn, c, c_ctx, ada_w, ada_b, norm_g, w_in, hgrn_lb, hgrn_norm, s5_a_re, s5_a_im, s5_log_dt, s5_b_re, s5_b_im, s5_c_re, s5_c_im, s5_d, s5_glu_w, s5_glu_b, gdn_conv, gdn_a_log, gdn_dt_bias, gdn_norm, w_branch, w_out, ffn_wg, ffn_wu, ffn_wd, moe_router, moe_wg, moe_wu, moe_wd):
    bp, tp, d = x_prompt.shape
    bs, ts, _ = x_sample.shape
    depth = w_in.shape[0]
    n_p, n_s = bp * tp, bs * ts
    ne = moe_router.shape[-1]
    main_cols = 10 * BR_WIDTH
    ab_cols = 4 * N_HEADS

    c8 = jnp.zeros((8, d), F32).at[0].set(c_ctx).at[1:1 + bs].set(c)
    mod_all = _ada_mod(c8, ada_w, ada_b)

    lb_soft = jax.nn.softmax(hgrn_lb.astype(F32), axis=0)
    lb_all = jnp.cumsum(lb_soft, axis=0) - lb_soft[:1]

    mods = mod_all[:, :1 + bs].reshape(depth, 1 + bs, 6, d)
    zero_row = jnp.zeros((1 + bs, d), F32)

    def modv(gate, scale, shift, g):
        rows = [zero_row if sel is None else mods[sel[0], :, sel[1]] for sel in (gate, scale, shift)]
        rows.append(zero_row if g is None else jnp.broadcast_to(g, (1 + bs, d)))
        return jnp.stack(rows + [zero_row] * 4, axis=1)

    x = jnp.concatenate([x_prompt.reshape(n_p, d), x_sample.reshape(n_s, d)], axis=0)
    (h,) = _resnorm(x, None, modv(None, (0, 1), (0, 0), norm_g[0, 0]), n_p, ts, True)

    z_hgrn = jnp.zeros((bp, 2, N_HEADS, HEAD_D, HEAD_D), F32)
    ngb = S5_GROUPS // S5_GB
    w2 = 2 * S5_GB * S5_STATE
    st_s5 = []
    y_zero = jnp.zeros((n_p + n_s, 3 * BR_WIDTH), BF16)
    new_hgrn = new_gdn = jnp.zeros((bp, depth, 2, N_HEADS, HEAD_D, HEAD_D), F32)
    w_in_t = jnp.swapaxes(w_in, 1, 2)
    s5_tb = 1024

    for l in range(depth):
        col_major = l % 2 == 1
        p_main = _mm_t(h, w_in_t, l, main_cols, 2 * BR_WIDTH, "w_in_main", tm=1024)
        w_ab = jnp.pad(w_in_t[l, main_cols:main_cols + ab_cols], ((0, LANES - ab_cols), (0, 0)))
        p_ab = _mm_t(h, w_ab, None, LANES, LANES, "w_in_ab", tm=1024)
        p_mg = _mm_t(h, w_in_t[l, main_cols + ab_cols:], None, 3 * d, 2 * BR_WIDTH, "w_in_merge", tm=1024,
                     out_dtype=BF16)

        blk = BR_WIDTH // LANES
        br = y_zero

        br, new_hgrn = _hgrn(p_main, lb_all[l], hgrn_norm[l], z_hgrn, 0, bp, tp, br, 0, new_hgrn, l, depth)
        br, _ = _hgrn(p_main, lb_all[l], hgrn_norm[l], state_hgrn[:, l].astype(F32), n_p // ts, bs, ts, br, 0,
                      col_major=col_major)

        bb, cc, lam = _s5_params(s5_a_re[l], s5_a_im[l], s5_log_dt[l], s5_b_re[l], s5_b_im[l],
                                 s5_c_re[l], s5_c_im[l])
        seq_per_blk = 8
        tb_p = seq_per_blk * tp
        nblk_p = n_p // tb_p
        br, sb = _s5(p_main, bb, cc, lam, s5_d[l], jnp.zeros((nblk_p, ngb, 2, 8, w2), F32),
                     0, nblk_p, tb_p, tp, tp, True, br, blk)
        s0s = state_s5[:, l].astype(F32)
        s0s = s0s.reshape(bs, 2, ngb, S5_GB * S5_STATE, 2).transpose(0, 2, 1, 4, 3).reshape(bs, ngb, 2, 1, w2)
        s0s = jnp.pad(s0s, ((0, 0), (0, 0), (0, 0), (0, 7), (0, 0)))
        br, _ = _s5(p_main, bb, cc, lam, s5_d[l], s0s, n_p // s5_tb, n_s // s5_tb, s5_tb, ts, S5_SEG, False, br, blk,
                    col_major=col_major)
        br = _glu(br, s5_glu_w, s5_glu_b, l, 1)
        sb = sb[:, :, :, :seq_per_blk].reshape(nblk_p, ngb, 2, seq_per_blk, 2, S5_GB, S5_STATE)
        sb = sb.transpose(0, 3, 2, 1, 5, 6, 4).reshape(bp, 2, S5_GROUPS, S5_STATE, 2)
        st_s5.append(sb)

        br, new_gdn = _gdn(p_main, p_ab, gdn_conv[l], gdn_a_log[l], gdn_dt_bias[l], gdn_norm[l], z_hgrn, 0, bp, tp,
                           br, 2 * blk, new_gdn, l, depth)
        br, _ = _gdn(p_main, p_ab, gdn_conv[l], gdn_a_log[l], gdn_dt_bias[l], gdn_norm[l],
                     state_gdn[:, l].astype(F32), n_p // ts, bs, ts, br, 2 * blk, col_major=col_major)

        x, h2 = _merge(br, p_mg, w_branch, w_out, norm_g[l, 1], l, x,
                       modv((l, 2), (l, 4), (l, 3), norm_g[l, 2]), n_p, ts, BF16 if l % 2 == 0 else F32)
        last = l + 1 == depth
        next_mod = modv((l, 5), None, None, None) if last else modv((l, 5), (l + 1, 1), (l + 1, 0), norm_g[l + 1, 0])

        if l % 2 == 0:
            e = l // 2
            hid = _swiglu_up(h2, ffn_wg, ffn_wu, e, ffn_wg.shape[-1] // 2)
            x, h = _down_res(hid, ffn_wd, e, norm_g[l, 3], x, next_mod, n_p, ts, BF16)
            if last:
                y_prompt, y_sample = x[:n_p], x[n_p:]
            continue
        else:
            e = l // 2
            n_tok = n_p + n_s
            max_tiles = 2 * n_tok // MOE_TILE + ne
            aux, cnt = _router(h2, jnp.pad(moe_router[e], ((0, 0), (0, LANES - ne))), ne)
            slot_tok, slot_gate, tile_e, tile_valid = _moe_plan(aux, cnt, n_tok, ne, MOE_TILE, max_tiles)
            hid = _moe_up(h2, slot_tok, tile_e, tile_valid, moe_wg, moe_wu, e, MOE_TILE)
            r_f = _moe_down(hid, slot_tok, slot_gate, tile_e, tile_valid, moe_wd, e, norm_g[l, 3], n_tok, MOE_TILE)

        if not last:
            x, h = _resnorm(x, r_f, next_mod, n_p, ts, True)
        else:
            (y_prompt,) = _resnorm(x, r_f, next_mod, n_p, ts, False, rows=(0, n_p))
            (y_sample,) = _resnorm(x, r_f, next_mod, n_p, ts, False, rows=(n_p, n_s))

    y_prompt = y_prompt.reshape(bp, tp, d)
    y_sample = y_sample.reshape(bs, ts, d)
    new_s5 = jnp.stack(st_s5, axis=1).astype(x_prompt.dtype)
    return (y_prompt, y_sample, new_hgrn.astype(x_prompt.dtype), new_s5, new_gdn.astype(x_prompt.dtype))
```

```python
import functools
import math

import jax
import jax.numpy as jnp
from jax import lax
from jax.experimental import pallas as pl
from jax.experimental.pallas import tpu as pltpu

F32 = jnp.float32
BF16 = jnp.bfloat16

EPS = 1e-6
F_FLOOR = 1e-6
LOG2E = 1.4426950408889634
LANES = 128
GRID_W = 64
HEAD_D = 128
N_HEADS = 4
BR_WIDTH = 512
S5_GROUPS = 32
S5_GROUP = 16
S5_STATE = 64
S5_GB = 8
S5_SEG = 64
GLA_CHUNK = 256
GDN_CHUNK = 64
GDN_SPAN = 256
GDN_HEADS_PER_STEP = 4
GDN_MAX_ROWS_PER_STEP = 2048
TM = 512
MOE_TILE = 256
VMEM_LIMIT = 56 * 1024 * 1024
MOE_UP_VMEM_LIMIT = 61 * 1024 * 1024


def _cp(sem, vmem=VMEM_LIMIT):
    return pltpu.CompilerParams(dimension_semantics=sem, vmem_limit_bytes=vmem)


def _dot(a, b):
    return jnp.dot(a, b, preferred_element_type=F32)


def _dot_nt(a, b):
    return lax.dot_general(a, b, (((1,), (1,)), ((), ())), preferred_element_type=F32)


def _dot_tn(a, b):
    return lax.dot_general(a, b, (((0,), (0,)), ((), ())), preferred_element_type=F32)


def _silu(x):
    return x * jax.nn.sigmoid(x)


def _rms(x, g):
    return x * lax.rsqrt(jnp.mean(x * x, axis=-1, keepdims=True) + EPS) * g


def _chunk_cumsum(x, d, c):
    n = x.shape[0]
    pos = lax.broadcasted_iota(jnp.int32, x.shape, 0) & (c - 1)
    s = 1
    while s < c:
        if d == 0:
            x = x + jnp.where(pos >= s, pltpu.roll(x, s, 0), 0.0)
        else:
            x = x + jnp.where(pos < c - s, pltpu.roll(x, n - s, 0), 0.0)
        s *= 2
    return x


def _ada_kernel(c_ref, w_ref, b_ref, o_ref):
    c = c_ref[...]
    a = _silu(c).astype(BF16)
    o_ref[...] = _dot(a, w_ref[...].astype(BF16)) + b_ref[...]


def _ada_mod(c8, ada_w, ada_b):
    depth, d, n6 = ada_w.shape
    tn = 512
    return pl.pallas_call(
        _ada_kernel,
        grid=(depth, n6 // tn),
        in_specs=[
            pl.BlockSpec((8, d), lambda l, j: (0, 0)),
            pl.BlockSpec((None, d, tn), lambda l, j: (l, 0, j)),
            pl.BlockSpec((None, 1, tn), lambda l, j: (l, 0, j)),
        ],
        out_specs=pl.BlockSpec((None, 8, tn), lambda l, j: (l, 0, j)),
        out_shape=jax.ShapeDtypeStruct((depth, 8, n6), F32),
        compiler_params=_cp(("parallel", "parallel")),
        name="ada_mod",
    )(c8, ada_w, ada_b.reshape(depth, 1, n6))


def _two_source_specs(tm, d, na_tiles):
    return [pl.BlockSpec((tm, d), lambda i: (jnp.minimum(i, na_tiles - 1), 0)),
            pl.BlockSpec((tm, d), lambda i: (jnp.maximum(i - na_tiles, 0), 0))]


def _pick_source(xa_ref, xb_ref, na_tiles):
    return jnp.where(pl.program_id(0) < na_tiles, xa_ref[...], xb_ref[...])


def _resnorm_kernel(*refs, has_y, has_h, na_tiles):
    it = iter(refs)
    x_ref = next(it)
    xb_ref = next(it) if na_tiles else None
    y_ref = next(it) if has_y else None
    mv_ref = next(it)
    xo_ref = next(it) if has_y else None
    h_ref = next(it) if has_h else None
    x = _pick_source(x_ref, xb_ref, na_tiles) if na_tiles else x_ref[...]
    mv = mv_ref[...]
    if has_y:
        x = x + mv[0:1] * y_ref[...]
        xo_ref[...] = x
    if has_h:
        h = _rms(x, mv[3:4]) * (1.0 + mv[1:2]) + mv[2:3]
        h_ref[...] = h.astype(h_ref.dtype)


def _resnorm(x, y, modv, n_prompt, seq_s, has_h, rows=None, h_dtype=BF16):
    xs = x if isinstance(x, tuple) else (x,)
    d = xs[0].shape[1]
    row0, n = (0, sum(a.shape[0] for a in xs)) if rows is None else rows
    i0 = row0 // TM
    has_y = y is not None
    na_tiles = xs[0].shape[0] // TM if len(xs) == 2 else 0

    def group(i):
        r = (i + i0) * TM
        return jnp.where(r < n_prompt, 0, 1 + (r - n_prompt) // seq_s)

    tile = pl.BlockSpec((TM, d), lambda i: (i + i0, 0))
    x_specs = _two_source_specs(TM, d, na_tiles) if na_tiles else [tile]
    in_specs = x_specs + ([tile] if has_y else []) + [pl.BlockSpec((None, 8, d), lambda i: (group(i), 0, 0))]
    out_tile = pl.BlockSpec((TM, d), lambda i: (i, 0))
    out_specs, out_shape = [], []
    if has_y:
        out_specs.append(out_tile)
        out_shape.append(jax.ShapeDtypeStruct((n, d), F32))
    if has_h:
        out_specs.append(out_tile)
        out_shape.append(jax.ShapeDtypeStruct((n, d), h_dtype))
    args = list(xs) + ([y] if has_y else []) + [modv]
    outs = pl.pallas_call(
        functools.partial(_resnorm_kernel, has_y=has_y, has_h=has_h, na_tiles=na_tiles),
        grid=(n // TM,),
        in_specs=in_specs,
        out_specs=out_specs,
        out_shape=out_shape,
        compiler_params=_cp(("parallel",)),
        name="resnorm",
    )(*args)
    return outs


def _mm_kernel(x_ref, w_ref, o_ref, wb_ref):
    @pl.when(pl.program_id(1) == 0)
    def _():
        wb_ref[...] = w_ref[...].astype(BF16)

    o_ref[...] = _dot_nt(x_ref[...], wb_ref[...]).astype(o_ref.dtype)


def _mm_t(x, wt, l, nrows, tn, name, tm=TM, out_dtype=F32):
    m, k = x.shape
    if l is None:
        wspec = pl.BlockSpec((tn, k), lambda j, i: (j, 0))
    else:
        wspec = pl.BlockSpec((None, tn, k), lambda j, i: (l, j, 0))
    return pl.pallas_call(
        _mm_kernel,
        grid=(nrows // tn, m // tm),
        in_specs=[pl.BlockSpec((tm, k), lambda j, i: (i, 0)), wspec],
        out_specs=pl.BlockSpec((tm, tn), lambda j, i: (i, j)),
        out_shape=jax.ShapeDtypeStruct((m, nrows), out_dtype),
        scratch_shapes=[pltpu.VMEM((tn, k), BF16)],
        compiler_params=_cp(("arbitrary", "arbitrary")),
        name=name,
    )(x, wt)


def _gla_intra(q, k, b, d, c):
    rowi = lax.broadcasted_iota(jnp.int32, (c, LANES), 0)
    ri = lax.broadcasted_iota(jnp.int32, (c, c), 0)
    ci = lax.broadcasted_iota(jnp.int32, (c, c), 1)
    a = jnp.zeros((c, c), F32)
    size = c
    while size >= 8:
        half = size // 2
        nb = c // size
        b3 = b.reshape(nb, size, LANES)
        pos = rowi & (size - 1)
        if d == 0:
            ref = b3[:, half - 1:half, :]
            qsel = pos >= half
        else:
            ref = b3[:, half:half + 1, :]
            qsel = pos < half
        ref = jnp.broadcast_to(ref, (nb, size, LANES)).reshape(c, LANES)
        z = jnp.where(qsel, q, k) * jnp.exp2(-jnp.abs(b - ref))
        qs = jnp.where(qsel, z, 0.0).astype(BF16)
        ks = jnp.where(qsel, 0.0, z).astype(BF16)
        p = _dot_nt(qs, ks)
        if nb > 1:
            shift = int(math.log2(size))
            p = jnp.where((ri >> shift) == (ci >> shift), p, 0.0)
        a = a + p
        size = half
    b3 = b.reshape(c // 8, 8, LANES)
    lo, hi = (0, 4) if d == 0 else (3, 7)
    ref_lo = jnp.broadcast_to(b3[:, lo:lo + 1, :], (c // 8, 8, LANES)).reshape(c, LANES)
    ref_hi = jnp.broadcast_to(b3[:, hi:hi + 1, :], (c // 8, 8, LANES)).reshape(c, LANES)
    ref = jnp.where((rowi & 7) >= 4, ref_hi, ref_lo)
    qs = (q * jnp.exp2(jnp.minimum(b - ref, 0.0))).astype(BF16)
    ks = (k * jnp.exp2(ref - b)).astype(BF16)
    p = _dot_nt(qs, ks)
    keep = ((ri >> 2) == (ci >> 2)) & ((ci <= ri) if d == 0 else (ci >= ri))
    return a + jnp.where(keep, p, 0.0)


def _load_column_major(dst_ref, src_ref, t, stage=None):
    rows = t // GRID_W
    for k in range(src_ref.shape[1] // LANES):
        lanes = slice(k * LANES, (k + 1) * LANES)
        if stage is None:
            strided = src_ref
        else:
            stage[...] = src_ref[:, lanes]
            strided = stage
        for col in range(GRID_W):
            dst_ref[col * rows:(col + 1) * rows, lanes] = strided[pl.ds(col, rows, stride=GRID_W), :]


def _store_row_major(dst_ref, y, t, stage=None):
    rows = t // GRID_W
    for k in range(y.shape[1] // LANES):
        lanes = slice(k * LANES, (k + 1) * LANES)
        strided = dst_ref if stage is None else stage
        for col in range(GRID_W):
            strided[pl.ds(col, rows, stride=GRID_W), :] = y[col * rows:(col + 1) * rows, lanes]
        if stage is not None:
            dst_ref[:, lanes] = stage[...].astype(dst_ref.dtype)


def _hgrn_kernel(aq_ref, af_ref, ab_ref, ai_ref, ag_ref, lb_ref, g_ref, s0_ref, y_ref, s_ref, o_acc, *perm_bufs,
                 t, c):
    stage = None
    if perm_bufs:
        *perm_bufs, stage = perm_bufs
        srcs = (aq_ref, af_ref, ab_ref, ai_ref, ag_ref)
        for dst, src in zip(perm_bufs, srcs):
            _load_column_major(dst, src, t)
        aq_ref, af_ref, ab_ref, ai_ref, ag_ref = perm_bufs
    nchunk = t // c
    scale = HEAD_D ** -0.5
    for d in (0, 1):
        z_ref = af_ref if d == 0 else ab_ref
        lbd = lb_ref[d:d + 1, :]

        def chunk(i, st, d=d, z_ref=z_ref, lbd=lbd):
            cidx = i if d == 0 else nchunk - 1 - i
            r0 = pl.multiple_of(cidx * c, c)
            z = z_ref[pl.ds(r0, c), :]
            f = lbd + (1.0 - lbd) * jax.nn.sigmoid(z)
            lf = jnp.log2(jnp.maximum(f, F_FLOOR))
            k = (1.0 - lbd) * jax.nn.sigmoid(-z)
            q = _silu(aq_ref[pl.ds(r0, c), :]) * scale
            v = ai_ref[pl.ds(r0, c), :].astype(BF16)
            b = _chunk_cumsum(lf, d, c)
            a = _gla_intra(q, k, b, d, c)
            o = _dot(a.astype(BF16), v) + _dot_nt((q * jnp.exp2(b)).astype(BF16), st.astype(BF16))
            if d == 0:
                o_acc[pl.ds(r0, c), :] = o
            else:
                o_acc[pl.ds(r0, c), :] += o
            blast = b[c - 1:c, :] if d == 0 else b[0:1, :]
            kd = (k * jnp.exp2(blast - b)).astype(BF16)
            return st * jnp.exp2(blast) + _dot_tn(v, kd)

        st = s0_ref[0, d, 0].T
        if nchunk == 1:
            st = chunk(0, st)
        else:
            st = lax.fori_loop(0, nchunk, chunk, st)
        s_ref[0, d, 0] = st.T
    y = _rms(o_acc[...], g_ref[...]) * _silu(ag_ref[...])
    if perm_bufs:
        _store_row_major(y_ref, y, t, stage)
    else:
        y_ref[...] = y.astype(y_ref.dtype)


def _shared_out_call(kernel_fn, prevs, in_specs, args, **kw):
    extra = [p for p in prevs if p is not None]
    if not extra:
        return pl.pallas_call(kernel_fn, in_specs=in_specs, **kw)(*args)
    n_in = len(in_specs)
    aliases, k = {}, 0
    for out_idx, p in enumerate(prevs):
        if p is not None:
            aliases[n_in + k] = out_idx
            k += 1

    def body(*refs):
        kernel_fn(*refs[:n_in], *refs[n_in + len(extra):])

    return pl.pallas_call(body, in_specs=in_specs + [pl.BlockSpec(memory_space=pl.ANY)] * len(extra),
                          input_output_aliases=aliases, **kw)(*args, *extra)


def _state_out(nseq, slot, nslots, hg):
    spec = pl.BlockSpec((1, None, 2, hg, HEAD_D, HEAD_D), lambda s, h: (s, slot, 0, h, 0, 0))
    return spec, jax.ShapeDtypeStruct((nseq, nslots, 2, N_HEADS, HEAD_D, HEAD_D), F32)


def _hgrn(p_main, lb, norm_g, s0, row_blk0, nseq, t, y_prev, col0, s_prev=None, slot=0, nslots=1,
          col_major=False):
    c = min(GLA_CHUNK, t)
    seq_buf = pltpu.VMEM((t, HEAD_D), F32)

    def col(base):
        return pl.BlockSpec((t, HEAD_D), lambda s, h: (row_blk0 + s, base + h))

    st_spec = pl.BlockSpec((1, 2, 1, HEAD_D, HEAD_D), lambda s, h: (s, 0, h, 0, 0))
    so_spec, so_shape = _state_out(nseq, slot, nslots, 1)
    return _shared_out_call(
        functools.partial(_hgrn_kernel, t=t, c=c), [y_prev, s_prev],
        [col(0), col(4), col(8), col(12), col(16),
         pl.BlockSpec((2, HEAD_D), lambda s, h: (0, h)),
         pl.BlockSpec((1, HEAD_D), lambda s, h: (0, 0)),
         st_spec],
        (p_main, p_main, p_main, p_main, p_main, lb, norm_g.reshape(1, HEAD_D), s0),
        grid=(nseq, N_HEADS),
        out_specs=[pl.BlockSpec((t, HEAD_D), lambda s, h: (row_blk0 + s, col0 + h)), so_spec],
        out_shape=[jax.ShapeDtypeStruct(y_prev.shape, y_prev.dtype), so_shape],
        scratch_shapes=[seq_buf] * (7 if col_major else 1),
        compiler_params=_cp(("parallel", "parallel")),
        name="hgrn2",
    )


def _dot_b(a, b):
    return _dot(a.astype(BF16), b.astype(BF16))


def _tri_solve(ms, xs, c):
    ri = lax.broadcasted_iota(jnp.int32, (c, c), 0)
    ci = lax.broadcasted_iota(jnp.int32, (c, c), 1)
    eye = jnp.where(ri == ci, 1.0, 0.0)
    m8 = [jnp.where((ri >> 3) == (ci >> 3), m, 0.0) for m in ms]
    ts = [eye - m for m in m8]
    ps = [_dot_b(m, m) for m in m8]
    ts = [t + _dot_b(t, p) for t, p in zip(ts, ps)]
    ps = [_dot_b(p, p) for p in ps]
    ts = [t + _dot_b(t, p) for t, p in zip(ts, ps)]
    shift = 3
    while (1 << shift) < c:
        join = ((ri >> (shift + 1)) == (ci >> (shift + 1))) & ((ri >> shift) != (ci >> shift))
        lts = [_dot_b(jnp.where(join, m, 0.0), t) for m, t in zip(ms, ts)]
        ts = [t - _dot_b(t, lt) for t, lt in zip(ts, lts)]
        shift += 1
    return [_dot_b(t, x) for t, x in zip(ts, xs)]


def _gdn_kernel(cq_ref, ck_ref, cv_ref, cg_ref, ab_ref, cw_ref, alog_ref, dtb_ref, g_ref, s0_ref,
                y_ref, s_ref, q_s, k_s, v_s, la_s, be_s, w_s, qk_s, qe_s, kd_s, dec_s, o_acc, *perm_bufs,
                t, c, span, hg):
    stage = None
    if perm_bufs:
        *perm_bufs, stage = perm_bufs
        srcs = (cq_ref, ck_ref, cv_ref, cg_ref, ab_ref)
        for dst, src in zip(perm_bufs, srcs):
            _load_column_major(dst, src, t, stage)
        cq_ref, ck_ref, cv_ref, cg_ref, ab_ref = perm_bufs
    nchunk = t // c
    cps = span // c
    rowt = lax.broadcasted_iota(jnp.int32, (t, HEAD_D), 0)
    lane = lax.broadcasted_iota(jnp.int32, (t, LANES), 1)
    ri = lax.broadcasted_iota(jnp.int32, (span, span), 0)
    ci = lax.broadcasted_iota(jnp.int32, (span, span), 1)
    shift = int(math.log2(c))
    same = (ri >> shift) == (ci >> shift)
    heads = range(hg)
    dirs = (0, 1)

    def conv(x_ref, j, hh):
        x = x_ref[:, hh * HEAD_D:(hh + 1) * HEAD_D]
        w = cw_ref[j][:, hh * HEAD_D:(hh + 1) * HEAD_D]
        xm = jnp.where(rowt == 0, 0.0, pltpu.roll(x, 1, 0))
        xp = jnp.where(rowt == t - 1, 0.0, pltpu.roll(x, t - 1, 0))
        return xm * w[0:1] + x * w[1:2] + xp * w[2:3]

    ab = ab_ref[...]
    lane1 = lax.broadcasted_iota(jnp.int32, (1, LANES), 1)
    alog_l = jnp.zeros((1, LANES), F32)
    dtb_l = jnp.zeros((1, LANES), F32)
    for d in dirs:
        for hd in range(N_HEADS):
            alog_l = jnp.where(lane1 == d * N_HEADS + hd, alog_ref[d, hd], alog_l)
            dtb_l = jnp.where(lane1 == d * N_HEADS + hd, dtb_ref[d, hd], dtb_l)
    xs = ab + dtb_l
    log_a_all = (-LOG2E) * jnp.exp(alog_l) * (jnp.maximum(xs, 0.0) + jnp.log1p(jnp.exp(-jnp.abs(xs))))
    beta_all = jax.nn.sigmoid(ab)
    for hh in heads:
        h = pl.program_id(1) * hg + hh
        q = _silu(conv(cq_ref, 0, hh))
        q_s[hh] = q * lax.rsqrt(jnp.sum(q * q, axis=-1, keepdims=True) + EPS) * (HEAD_D ** -0.5)
        k = _silu(conv(ck_ref, 1, hh))
        k_s[hh] = k * lax.rsqrt(jnp.sum(k * k, axis=-1, keepdims=True) + EPS)
        v_s[hh] = _silu(conv(cv_ref, 2, hh))
        for d in dirs:
            la_col = jnp.sum(jnp.where(lane == d * N_HEADS + h, log_a_all, 0.0), axis=-1, keepdims=True)
            be_col = jnp.sum(jnp.where(lane == 2 * N_HEADS + d * N_HEADS + h, beta_all, 0.0), axis=-1,
                             keepdims=True)
            la_s[hh, d] = jnp.broadcast_to(la_col, (t, LANES))
            be_s[hh, d] = jnp.broadcast_to(be_col, (t, LANES))
    o_acc[...] = jnp.zeros(o_acc.shape, F32)

    def wy_span(sp, carry):
        r0 = pl.multiple_of(sp * span, span)
        ms, xs, dst = [], [], []
        for hh in heads:
            qsp = q_s[hh, pl.ds(r0, span), :]
            ksp = k_s[hh, pl.ds(r0, span), :]
            vsp = v_s[hh, pl.ds(r0, span), :]
            kb = ksp.astype(BF16)
            kk = _dot_nt(kb, kb)
            qk = _dot_nt(qsp.astype(BF16), kb)
            for d in dirs:
                be = be_s[hh, d, pl.ds(r0, span), :]
                g = _chunk_cumsum(la_s[hh, d, pl.ds(r0, span), :], d, c)
                gt = g.T
                gcol = jnp.concatenate([g] * (span // LANES), axis=1)
                grow = jnp.concatenate([gt] * (span // LANES), axis=0)
                incl = same & ((ci <= ri) if d == 0 else (ci >= ri))
                gamma = jnp.where(incl, jnp.exp2(jnp.minimum(gcol - grow, 0.0)), 0.0)
                bcol = jnp.concatenate([be] * (span // LANES), axis=1)
                m_sp = jnp.where(ci == ri, 0.0, bcol * kk * gamma)
                qkg = (qk * gamma).astype(BF16)
                eg = jnp.exp2(g)
                xv = be * vsp
                xk = be * eg * ksp
                qe_s[hh, d, pl.ds(r0, span), :] = (qsp * eg).astype(BF16)
                for j in range(cps):
                    lo = j * c
                    rj = pl.multiple_of(r0 + lo, c)
                    ms.append(m_sp[lo:lo + c, lo:lo + c])
                    xs.append(jnp.concatenate([xv[lo:lo + c], xk[lo:lo + c]], axis=1))
                    dst.append((hh, d, rj))
                    qk_s[hh, d, pl.ds(rj, c), 0:c] = qkg[lo:lo + c, lo:lo + c]
                    gj = g[lo:lo + c]
                    glast = gj[c - 1:c] if d == 0 else gj[0:1]
                    kd_s[hh, d, pl.ds(rj, c), :] = (ksp[lo:lo + c] * jnp.exp2(glast - gj)).astype(BF16)
                    dec_s[hh, d, pl.ds(sp * cps + j, 1), :] = jnp.exp2(glast)
        for (hh, d, rj), w in zip(dst, _tri_solve(ms, xs, c)):
            w_s[hh, d, pl.ds(rj, c), :] = w
        return carry

    if t == span:
        wy_span(0, 0)
    else:
        lax.fori_loop(0, t // span, wy_span, 0)

    chains = [(hh, d) for hh in heads for d in dirs]

    def scan_step(i, carry):
        cidx = (i, nchunk - 1 - i)
        r0 = [pl.multiple_of(ck * c, c) for ck in cidx]
        sb = [s.astype(BF16) for s in carry]
        ws = [w_s[hh, d, pl.ds(r0[d], c), :] for hh, d in chains]
        wks = [_dot(w[:, HEAD_D:].astype(BF16), s) for w, s in zip(ws, sb)]
        oq = [_dot(qe_s[hh, d, pl.ds(r0[d], c), :], s) for (hh, d), s in zip(chains, sb)]
        vnb = [(w[:, :HEAD_D] - wk).astype(BF16) for w, wk in zip(ws, wks)]
        new = [s * dec_s[hh, d, pl.ds(cidx[d], 1), :] + _dot_tn(kd_s[hh, d, pl.ds(r0[d], c), :], v)
               for (hh, d), s, v in zip(chains, carry, vnb)]
        for (hh, d), o, v in zip(chains, oq, vnb):
            o_acc[hh, pl.ds(r0[d], c), :] += o + _dot(qk_s[hh, d, pl.ds(r0[d], c), 0:c], v)
        return tuple(new)

    finals = lax.fori_loop(0, nchunk, scan_step, tuple(s0_ref[0, d, hh] for hh, d in chains),
                           unroll=nchunk <= 4)
    for (hh, d), s in zip(chains, finals):
        s_ref[0, d, hh] = s
    ys = []
    for hh in heads:
        sl = slice(hh * HEAD_D, (hh + 1) * HEAD_D)
        ys.append(_rms(o_acc[hh], g_ref[...]) * _silu(cg_ref[:, sl]))
    y = jnp.concatenate(ys, axis=1)
    if perm_bufs:
        _store_row_major(y_ref, y, t, stage)
    else:
        y_ref[...] = y.astype(y_ref.dtype)


def _gdn(p_main, p_ab, conv_w, a_log, dt_bias, norm_g, s0, row_blk0, nseq, t, y_prev, col0, s_prev=None,
         slot=0, nslots=1, col_major=False):
    c = GDN_CHUNK
    hg = GDN_HEADS_PER_STEP if t * GDN_HEADS_PER_STEP <= GDN_MAX_ROWS_PER_STEP else GDN_HEADS_PER_STEP // 2
    wid = hg * HEAD_D
    blk = BR_WIDTH // wid

    def col(base):
        return pl.BlockSpec((t, wid), lambda s, h: (row_blk0 + s, base * blk + h))

    st_spec = pl.BlockSpec((1, 2, hg, HEAD_D, HEAD_D), lambda s, h: (s, 0, h, 0, 0))
    smem = pl.BlockSpec(memory_space=pltpu.SMEM)
    seq_buf = pltpu.VMEM((hg, t, HEAD_D), F32)
    dir_f32 = pltpu.VMEM((hg, 2, t, LANES), F32)
    dir_bf16 = pltpu.VMEM((hg, 2, t, LANES), BF16)
    scratch = [seq_buf, seq_buf, seq_buf, dir_f32, dir_f32,
               pltpu.VMEM((hg, 2, t, 2 * HEAD_D), F32), dir_bf16, dir_bf16, dir_bf16,
               pltpu.VMEM((hg, 2, max(t // c, 8), LANES), F32), seq_buf]
    if col_major:
        scratch += [pltpu.VMEM((t, wid), F32)] * 4 + [pltpu.VMEM((t, LANES), F32)] * 2
    so_spec, so_shape = _state_out(nseq, slot, nslots, hg)
    return _shared_out_call(
        functools.partial(_gdn_kernel, t=t, c=c, span=GDN_SPAN, hg=hg), [y_prev, s_prev],
        [col(6), col(7), col(8), col(9),
         pl.BlockSpec((t, LANES), lambda s, h: (row_blk0 + s, 0)),
         pl.BlockSpec((3, 3, wid), lambda s, h: (0, 0, h)),
         smem, smem,
         pl.BlockSpec((1, HEAD_D), lambda s, h: (0, 0)),
         st_spec],
        (p_main, p_main, p_main, p_main, p_ab, conv_w, a_log, dt_bias, norm_g.reshape(1, HEAD_D), s0),
        grid=(nseq, N_HEADS // hg),
        out_specs=[pl.BlockSpec((t, wid), lambda s, h: (row_blk0 + s, col0 * LANES // wid + h)), so_spec],
        out_shape=[jax.ShapeDtypeStruct(y_prev.shape, y_prev.dtype), so_shape],
        scratch_shapes=scratch,
        compiler_params=_cp(("parallel", "parallel")),
        name="gdn",
    )


def _cmul(ar, ai, br, bi):
    return ar * br - ai * bi, ar * bi + ai * br


def _s5_kernel(u_ref, bb_ref, cc_ref, lam_ref, dsk_ref, s0_ref, y_ref, sf_ref, up_s, x_s, yp_s,
               *, tb, seg, segs_per_seq, zero_init, col_major):
    r_par = tb // seg
    half = S5_GB * S5_STATE
    if col_major:
        rows = tb // GRID_W
        assert segs_per_seq == r_par and seg % rows == 0

        def step_rows(r):
            return pl.ds((r % rows) * GRID_W + r // rows, r_par, stride=seg // rows)
    else:
        def step_rows(r):
            return pl.ds(r, r_par, stride=seg)

    for r in range(seg):
        up_s[r * r_par:(r + 1) * r_par, :] = u_ref[step_rows(r), :]
    ub = up_s[...].astype(BF16)
    sf_ref[...] = jnp.zeros(sf_ref.shape, F32)
    nseq = r_par // segs_per_seq
    lam = [(lam_ref[0, d, 0:1, :], lam_ref[0, d, 1:2, :]) for d in (0, 1)]
    for d in (0, 1):
        x_s[d] = _dot(ub, bb_ref[0, d])

    def offsets(i):
        return [pl.multiple_of(r * r_par, r_par) for r in (i, seg - 1 - i)]

    def local_step(i, carry):
        off = offsets(i)
        new = []
        for d in (0, 1):
            (lre, lim), (hre, him) = lam[d], carry[d]
            nre = lre * hre - lim * him + x_s[d, pl.ds(off[d], r_par), 0:half]
            nim = lre * him + lim * hre + x_s[d, pl.ds(off[d], r_par), half:2 * half]
            x_s[d, pl.ds(off[d], r_par), 0:half] = nre
            x_s[d, pl.ds(off[d], r_par), half:2 * half] = nim
            new.append((nre, nim))
        return tuple(new)

    zero = jnp.zeros((r_par, half), F32)
    lax.fori_loop(0, seg, local_step, ((zero, zero), (zero, zero)), unroll=4)

    end_rows = ((seg - 1) * r_par, 0)
    needs_carry = not (zero_init and segs_per_seq == 1)
    hins = []
    for d in (0, 1):
        if not needs_carry:
            break
        lre, lim = lam[d]
        pre, pim = lre, lim
        n = 1
        while n < seg:
            pre, pim = _cmul(pre, pim, pre, pim)
            n *= 2
        end_row = end_rows[d]
        ere = x_s[d, end_row:end_row + r_par, 0:half]
        eim = x_s[d, end_row:end_row + r_par, half:2 * half]
        hin = [None] * r_par
        order = range(r_par) if d == 0 else range(r_par - 1, -1, -1)
        for j in order:
            sq, pos = divmod(j, segs_per_seq)
            first = pos == (0 if d == 0 else segs_per_seq - 1)
            if first:
                hin[j] = (s0_ref[0, 0, d, sq:sq + 1, 0:half], s0_ref[0, 0, d, sq:sq + 1, half:2 * half])
            else:
                pj = j - 1 if d == 0 else j + 1
                cre, cim = _cmul(pre, pim, hin[pj][0], hin[pj][1])
                hin[j] = (cre + ere[pj:pj + 1], cim + eim[pj:pj + 1])
        hins.append((jnp.concatenate([hj[0] for hj in hin], axis=0),
                     jnp.concatenate([hj[1] for hj in hin], axis=0)))

    def carry_step(i, pws):
        off = offsets(i)
        new = []
        for d in (0, 1):
            (lre, lim), (wre, wim) = lam[d], pws[d]
            are, aim = _cmul(wre, wim, hins[d][0], hins[d][1])
            x_s[d, pl.ds(off[d], r_par), 0:half] += are
            x_s[d, pl.ds(off[d], r_par), half:2 * half] += aim
            new.append(_cmul(wre, wim, lre, lim))
        return tuple(new)

    if needs_carry:
        lax.fori_loop(0, seg, carry_step, (lam[0], lam[1]), unroll=4)

    for d in (0, 1):
        for sq in range(nseq):
            j = (sq + 1) * segs_per_seq - 1 if d == 0 else sq * segs_per_seq
            row = end_rows[d] + j
            sf_ref[0, 0, d, sq:sq + 1, :] = x_s[d, row:row + 1, :]
    yp_s[...] = (_dot(x_s[0].astype(BF16), cc_ref[0, 0]) + _dot(x_s[1].astype(BF16), cc_ref[0, 1])
                 + up_s[...] * dsk_ref[...])
    for r in range(seg):
        up_s[step_rows(r), :] = yp_s[r * r_par:(r + 1) * r_par, :]
    y_ref[...] = up_s[...].astype(y_ref.dtype)


def _s5(p_main, bb, cc, lam, dskip, s0, row_blk0, nblk, tb, t, seg, zero_init, y_prev, col0, col_major=False):
    ngb = S5_GROUPS // S5_GB
    w2 = 2 * S5_GB * S5_STATE
    return _shared_out_call(
        functools.partial(_s5_kernel, tb=tb, seg=seg, segs_per_seq=t // seg, zero_init=zero_init,
                          col_major=col_major), [y_prev, None],
        [
            pl.BlockSpec((tb, LANES), lambda i, gb: (row_blk0 + i, 20 + gb)),
            pl.BlockSpec((1, 2, LANES, w2), lambda i, gb: (gb, 0, 0, 0)),
            pl.BlockSpec((1, 2, w2, LANES), lambda i, gb: (gb, 0, 0, 0)),
            pl.BlockSpec((1, 2, 2, w2 // 2), lambda i, gb: (gb, 0, 0, 0)),
            pl.BlockSpec((1, LANES), lambda i, gb: (0, gb)),
            pl.BlockSpec((1, 1, 2, 8, w2), lambda i, gb: (i, gb, 0, 0, 0)),
        ],
        (p_main, bb, cc, lam, dskip.reshape(1, BR_WIDTH), s0),
        grid=(nblk, ngb),
        out_specs=[pl.BlockSpec((tb, LANES), lambda i, gb: (row_blk0 + i, col0 + gb)),
                   pl.BlockSpec((1, 1, 2, 8, w2), lambda i, gb: (i, gb, 0, 0, 0))],
        out_shape=[jax.ShapeDtypeStruct(y_prev.shape, y_prev.dtype),
                   jax.ShapeDtypeStruct((nblk, ngb, 2, 8, w2), F32)],
        scratch_shapes=[pltpu.VMEM((tb, LANES), F32), pltpu.VMEM((2, tb, w2), F32), pltpu.VMEM((tb, LANES), F32)],
        compiler_params=_cp(("parallel", "parallel")),
        name="s5",
    )


def _s5_params(a_re, a_im, log_dt, b_re, b_im, c_re, c_im):
    dt = jnp.exp(log_dt)[..., None]
    mag = jnp.exp(dt * a_re)
    lam_re = mag * jnp.cos(dt * a_im)
    lam_im = mag * jnp.sin(dt * a_im)
    den = a_re * a_re + a_im * a_im
    co_re = ((lam_re - 1.0) * a_re + lam_im * a_im) / den
    co_im = (lam_im * a_re - (lam_re - 1.0) * a_im) / den
    bb_re = co_re[..., None] * b_re - co_im[..., None] * b_im
    bb_im = co_re[..., None] * b_im + co_im[..., None] * b_re
    ngb = S5_GROUPS // S5_GB
    eye = jnp.eye(S5_GB, dtype=F32)

    def pack_in(w):
        w = w.reshape(2, ngb, S5_GB, S5_STATE, S5_GROUP)
        blk = jnp.einsum('dbgpi,gh->bdgihp', w, eye)
        return blk.reshape(ngb, 2, S5_GB * S5_GROUP, S5_GB * S5_STATE)

    def pack_out(w):
        w = w.reshape(2, ngb, S5_GB, S5_GROUP, S5_STATE)
        blk = jnp.einsum('dbgip,gh->bdgphi', w, eye)
        return blk.reshape(ngb, 2, S5_GB * S5_STATE, S5_GB * S5_GROUP)

    bb = jnp.concatenate([pack_in(bb_re), pack_in(bb_im)], axis=-1).astype(BF16)
    cc = jnp.concatenate([pack_out(c_re), -pack_out(c_im)], axis=-2).astype(BF16)
    lam = jnp.stack([lam_re, lam_im], axis=1)
    lam = lam.reshape(2, 2, ngb, S5_GB * S5_STATE).transpose(2, 0, 1, 3)
    return bb, cc, lam


def _glu_kernel(y_ref, w_ref, b_ref, o_ref, wb_ref):
    @pl.when(pl.program_id(0) == 0)
    def _():
        wb_ref[...] = w_ref[...].astype(BF16)

    z = _dot(y_ref[...].astype(BF16), wb_ref[...]) + b_ref[...]
    half = z.shape[1] // 2
    o_ref[...] = (z[:, :half] * jax.nn.sigmoid(z[:, half:])).astype(o_ref.dtype)


def _glu(y, w, b, l, blk):
    n = y.shape[0]
    k, n2 = w.shape[-2:]
    assert n2 // 2 == k
    spec = pl.BlockSpec((TM, k), lambda i: (i, blk))
    return pl.pallas_call(
        _glu_kernel,
        grid=(n // TM,),
        in_specs=[spec,
                  pl.BlockSpec((None, k, n2), lambda i: (l, 0, 0)),
                  pl.BlockSpec((None, 1, n2), lambda i: (l, 0, 0))],
        out_specs=spec,
        out_shape=jax.ShapeDtypeStruct(y.shape, y.dtype),
        input_output_aliases={0: 0},
        scratch_shapes=[pltpu.VMEM((k, n2), BF16)],
        compiler_params=_cp(("arbitrary",)),
        name="s5_glu",
    )(y, w, b.reshape(b.shape[0], 1, n2))


def _residual_epilogue(y, g_ref, x_old, mv_ref, xo_ref, h_ref):
    mv = mv_ref[...]
    x = x_old + mv[0:1] * _rms(y, g_ref[...])
    xo_ref[...] = x
    h_ref[...] = (_rms(x, mv[3:4]) * (1.0 + mv[1:2]) + mv[2:3]).astype(h_ref.dtype)


def _group_of_tile(tm, n_prompt, seq_s):
    def group(i):
        return jnp.where(i * tm < n_prompt, 0, 1 + (i * tm - n_prompt) // seq_s)
    return group


def _merge_kernel(ya_ref, yb_ref, yc_ref, mg_ref, wbr_ref, wout_ref, g_ref, *rest, na_tiles):
    if na_tiles:
        x_ref, xb_ref, mv_ref, xo_ref, h_ref, wbr_s, wout_s = rest
    else:
        x_ref, mv_ref, xo_ref, h_ref, wbr_s, wout_s = rest

    @pl.when(pl.program_id(0) == 0)
    def _():
        wbr_s[...] = wbr_ref[...].astype(BF16)
        wout_s[...] = wout_ref[...].astype(BF16)

    d = wout_s.shape[0]
    acc = None
    for n, br_ref in enumerate((ya_ref, yb_ref, yc_ref)):
        proj = _dot(br_ref[...].astype(BF16), wbr_s[n])
        term = proj * jax.nn.sigmoid(mg_ref[:, n * d:(n + 1) * d].astype(F32))
        acc = term if acc is None else acc + term
    y = _dot(acc.astype(BF16), wout_s[...])
    x_old = _pick_source(x_ref, xb_ref, na_tiles) if na_tiles else x_ref[...]
    _residual_epilogue(y, g_ref, x_old, mv_ref, xo_ref, h_ref)


def _merge(branches, mg, w_branch, w_out, g, l, x, modv, n_prompt, seq_s, h_dtype):
    n = branches.shape[0]
    d = w_out.shape[-1]
    tm = 256
    group = _group_of_tile(tm, n_prompt, seq_s)
    tile = pl.BlockSpec((tm, d), lambda i: (i, 0))
    xs = x if isinstance(x, tuple) else (x,)
    na_tiles = xs[0].shape[0] // tm if len(xs) == 2 else 0
    x_specs = _two_source_specs(tm, d, na_tiles) if na_tiles else [tile]
    return pl.pallas_call(
        functools.partial(_merge_kernel, na_tiles=na_tiles),
        grid=(n // tm,),
        in_specs=[pl.BlockSpec((tm, BR_WIDTH), lambda i: (i, 0)),
                  pl.BlockSpec((tm, BR_WIDTH), lambda i: (i, 1)),
                  pl.BlockSpec((tm, BR_WIDTH), lambda i: (i, 2)),
                  pl.BlockSpec((tm, 3 * d), lambda i: (i, 0)),
                  pl.BlockSpec((None, 3, BR_WIDTH, d), lambda i: (l, 0, 0, 0)),
                  pl.BlockSpec((None, d, d), lambda i: (l, 0, 0)),
                  pl.BlockSpec((1, d), lambda i: (0, 0)),
                  *x_specs,
                  pl.BlockSpec((None, 8, d), lambda i: (group(i), 0, 0))],
        out_specs=[tile, tile],
        out_shape=[jax.ShapeDtypeStruct((n, d), F32), jax.ShapeDtypeStruct((n, d), h_dtype)],
        scratch_shapes=[pltpu.VMEM((3, BR_WIDTH, d), BF16), pltpu.VMEM((d, d), BF16)],
        compiler_params=_cp(("arbitrary",)),
        name="merge",
    )(branches, branches, branches, mg, w_branch, w_out, g.reshape(1, d), *xs, modv)


def _swiglu_kernel(*refs, gated):
    if gated:
        x_ref, wg_ref, wu_ref, gt_ref, o_ref, wg_s, wu_s = refs
    else:
        x_ref, wg_ref, wu_ref, o_ref, wg_s, wu_s = refs

    @pl.when(pl.program_id(1) == 0)
    def _():
        wg_s[...] = wg_ref[...].astype(BF16)
        wu_s[...] = wu_ref[...].astype(BF16)

    x = x_ref[...]
    hid = _silu(_dot(x, wg_s[...])) * _dot(x, wu_s[...])
    if gated:
        gt = gt_ref[...]
        lane = lax.broadcasted_iota(jnp.int32, gt.shape, 1)
        hid = hid * jnp.sum(jnp.where(lane == pl.program_id(0), gt, 0.0), axis=-1, keepdims=True)
    o_ref[...] = hid.astype(BF16)


def _swiglu_up(x, wg, wu, e, f, gates=None):
    n, d = x.shape
    gated = gates is not None
    if gated:
        ne = wg.shape[1]
        wspec = pl.BlockSpec((None, None, d, f), lambda j, i: (e, j, 0, 0))
    else:
        ne = wg.shape[-1] // f
        wspec = pl.BlockSpec((None, d, f), lambda j, i: (e, 0, j))
    in_specs = [pl.BlockSpec((TM, d), lambda j, i: (i, 0)), wspec, wspec]
    args = [x, wg, wu]
    if gated:
        in_specs.append(pl.BlockSpec((TM, LANES), lambda j, i: (i, 0)))
        args.append(gates)
    return pl.pallas_call(
        functools.partial(_swiglu_kernel, gated=gated),
        grid=(ne, n // TM),
        in_specs=in_specs,
        out_specs=pl.BlockSpec((TM, f), lambda j, i: (i, j)),
        out_shape=jax.ShapeDtypeStruct((n, ne * f), BF16),
        scratch_shapes=[pltpu.VMEM((d, f), BF16), pltpu.VMEM((d, f), BF16)],
        compiler_params=_cp(("arbitrary", "arbitrary")),
        name="swiglu_up",
    )(*args)


def _down_kernel(hid_ref, w_ref, g_ref, x_ref, mv_ref, xo_ref, h_ref, w_s):
    @pl.when(pl.program_id(0) == 0)
    def _():
        w_s[...] = w_ref[...].astype(BF16)

    _residual_epilogue(_dot(hid_ref[...], w_s[...]), g_ref, x_ref[...], mv_ref, xo_ref, h_ref)


def _down_res(hid, wd, e, g, x, modv, n_prompt, seq_s, h_dtype):
    n, f = hid.shape
    d = wd.shape[-1]
    group = _group_of_tile(TM, n_prompt, seq_s)
    tile = pl.BlockSpec((TM, d), lambda i: (i, 0))
    return pl.pallas_call(
        _down_kernel,
        grid=(n // TM,),
        in_specs=[pl.BlockSpec((TM, f), lambda i: (i, 0)),
                  pl.BlockSpec((None, f, d), lambda i: (e, 0, 0)),
                  pl.BlockSpec((1, d), lambda i: (0, 0)),
                  tile,
                  pl.BlockSpec((None, 8, d), lambda i: (group(i), 0, 0))],
        out_specs=[tile, tile],
        out_shape=[jax.ShapeDtypeStruct((n, d), F32), jax.ShapeDtypeStruct((n, d), h_dtype)],
        scratch_shapes=[pltpu.VMEM((f, d), BF16)],
        compiler_params=_cp(("arbitrary",)),
        name="down_res",
    )(hid, wd, g.reshape(1, d), x, modv)


def _router_kernel(x_ref, w_ref, aux_ref, cnt_ref, base_s, *, ne):
    @pl.when(pl.program_id(0) == 0)
    def _():
        base_s[...] = jnp.zeros(base_s.shape, F32)

    w = w_ref[...]
    w_hi = w.astype(BF16)
    w_lo = (w - w_hi.astype(F32)).astype(BF16)
    x = x_ref[...].astype(BF16)
    logits = _dot(x, w_hi) + _dot(x, w_lo)
    tm = logits.shape[0]
    lane = lax.broadcasted_iota(jnp.int32, logits.shape, 1).astype(F32)
    neg = jnp.float32(-jnp.inf)
    logits = jnp.where(lane < ne, logits, neg)
    m1 = jnp.max(logits, axis=-1, keepdims=True)
    i1 = jnp.min(jnp.where(logits == m1, lane, float(LANES)), axis=-1, keepdims=True)
    rest = jnp.where(lane == i1, neg, logits)
    m2 = jnp.max(rest, axis=-1, keepdims=True)
    i2 = jnp.min(jnp.where(rest == m2, lane, float(LANES)), axis=-1, keepdims=True)
    e2 = jnp.exp(m2 - m1)
    w1 = 1.0 / (1.0 + e2)
    w2 = e2 / (1.0 + e2)
    sel = jnp.where((lane == i1) | (lane == i2), 1.0, 0.0)
    ri = lax.broadcasted_iota(jnp.int32, (tm, tm), 0)
    ci = lax.broadcasted_iota(jnp.int32, (tm, tm), 1)
    before = jnp.where(ci < ri, 1.0, 0.0).astype(BF16)
    rank = _dot(before, sel.astype(BF16)) + base_s[0:1, :]
    r1 = jnp.sum(jnp.where(lane == i1, rank, 0.0), axis=-1, keepdims=True)
    r2 = jnp.sum(jnp.where(lane == i2, rank, 0.0), axis=-1, keepdims=True)
    aux = jnp.zeros(logits.shape, F32)
    for k, v in enumerate((i1, i2, r1, r2, w1, w2)):
        aux = jnp.where(lane == float(k), v, aux)
    aux_ref[...] = aux
    base_s[...] = base_s[...] + jnp.sum(sel, axis=0, keepdims=True)
    cnt_ref[...] = base_s[...]


def _router(x, w_pad, ne):
    n, d = x.shape
    return pl.pallas_call(
        functools.partial(_router_kernel, ne=ne),
        grid=(n // TM,),
        in_specs=[pl.BlockSpec((TM, d), lambda i: (i, 0)),
                  pl.BlockSpec((d, LANES), lambda i: (0, 0))],
        out_specs=[pl.BlockSpec((TM, LANES), lambda i: (i, 0)),
                   pl.BlockSpec((8, LANES), lambda i: (0, 0))],
        out_shape=[jax.ShapeDtypeStruct((n, LANES), F32), jax.ShapeDtypeStruct((8, LANES), F32)],
        scratch_shapes=[pltpu.VMEM((8, LANES), F32)],
        compiler_params=_cp(("arbitrary",)),
        name="router",
    )(x, w_pad)


def _moe_plan(aux, cnt, n, ne, tile, max_tiles):
    e = aux[:, 0:2].astype(jnp.int32)
    r = aux[:, 2:4].astype(jnp.int32)
    c = cnt[0, :ne].astype(jnp.int32)
    pc = (c + tile - 1) // tile * tile
    ends = jnp.cumsum(pc)
    slot = ((ends - pc)[e] + r).reshape(-1)
    tok = jnp.broadcast_to(jnp.arange(n, dtype=F32)[:, None], (n, 2)).reshape(-1)
    m = max_tiles * tile
    pad = jnp.broadcast_to(jnp.array([float(n), 0.0], F32), (m, 2))
    table = pad.at[slot].set(jnp.stack([tok, aux[:, 4:6].reshape(-1)], axis=1))
    slot_tok = table[:, 0].astype(jnp.int32)
    slot_gate = table[:, 1]
    tstart = jnp.arange(max_tiles, dtype=jnp.int32) * tile
    tile_e = jnp.minimum(jnp.sum(tstart[:, None] >= ends[None, :], axis=1), ne - 1).astype(jnp.int32)
    tile_valid = (tstart < ends[-1]).astype(jnp.int32)
    return slot_tok.reshape(max_tiles, 1, tile), slot_gate.reshape(max_tiles, 1, tile), tile_e, tile_valid


def _expert_changed(t, te_ref):
    return jnp.logical_or(t == 0, te_ref[t] != te_ref[jnp.maximum(t - 1, 0)])


def _moe_up_kernel(te_ref, tv_ref, idx_ref, nxt_ref, h_ref, wg_ref, wu_ref, o_ref, xbuf, wg_s, wu_s, *, tile):
    t = pl.program_id(0)
    valid = tv_ref[t] == 1
    last = h_ref.shape[0] - 1
    cur = t % 2

    def gather_row(src_idx_ref, buf, j):
        xbuf[buf, pl.ds(j, 1), :] = h_ref[pl.ds(jnp.minimum(src_idx_ref[0, 0, j], last), 1), :]

    @pl.when(t == 0)
    def _():
        def first(j, carry):
            gather_row(idx_ref, 0, j)
            return carry

        lax.fori_loop(0, tile, first, 0, unroll=8)

    @pl.when(jnp.logical_and(valid, _expert_changed(t, te_ref)))
    def _():
        wg_s[...] = wg_ref[...].astype(BF16)
        wu_s[...] = wu_ref[...].astype(BF16)

    @pl.when(valid)
    def _():
        x = xbuf[cur].astype(BF16)
        o_ref[...] = (_silu(_dot(x, wg_s[...])) * _dot(x, wu_s[...])).astype(BF16)
        for j in range(tile):
            gather_row(nxt_ref, 1 - cur, j)

    @pl.when(jnp.logical_not(valid))
    def _():
        o_ref[...] = jnp.zeros(o_ref.shape, BF16)


def _moe_up(h, slot_tok, tile_e, tile_valid, wg, wu, e_layer, tile):
    max_tiles = slot_tok.shape[0]
    _, _, d, f = wg.shape
    wspec = pl.BlockSpec((None, None, d, f), lambda t, te, tv: (e_layer, te[t], 0, 0))
    grid_spec = pltpu.PrefetchScalarGridSpec(
        num_scalar_prefetch=2,
        grid=(max_tiles,),
        in_specs=[pl.BlockSpec((1, 1, tile), lambda t, te, tv: (t, 0, 0), memory_space=pltpu.SMEM),
                  pl.BlockSpec((1, 1, tile), lambda t, te, tv: (jnp.minimum(t + 1, max_tiles - 1), 0, 0),
                               memory_space=pltpu.SMEM),
                  pl.BlockSpec(h.shape, lambda t, te, tv: (0, 0), pipeline_mode=pl.Buffered(1)),
                  wspec, wspec],
        out_specs=pl.BlockSpec((tile, f), lambda t, te, tv: (t, 0)),
        scratch_shapes=[pltpu.VMEM((2, tile, d), F32),
                        pltpu.VMEM((d, f), BF16), pltpu.VMEM((d, f), BF16)],
    )
    return pl.pallas_call(
        functools.partial(_moe_up_kernel, tile=tile),
        grid_spec=grid_spec,
        out_shape=jax.ShapeDtypeStruct((max_tiles * tile, f), BF16),
        compiler_params=_cp(("arbitrary",), vmem=MOE_UP_VMEM_LIMIT),
        name="moe_up",
    )(tile_e, tile_valid, slot_tok, slot_tok, h, wg, wu)


def _moe_down_kernel(te_ref, tv_ref, idx_ref, gt_ref, hid_ref, wd_ref, g_ref, o_ref, acc, ybuf, tbuf, wd_s,
                     *, tile, n_tiles, tm_out):
    t = pl.program_id(0)
    tt = jnp.minimum(t, n_tiles - 1)

    @pl.when(t == 0)
    def _():
        acc[...] = jnp.zeros(acc.shape, F32)

    is_tile = jnp.logical_and(t < n_tiles, tv_ref[tt] == 1)

    @pl.when(jnp.logical_and(is_tile, _expert_changed(tt, te_ref)))
    def _():
        wd_s[...] = wd_ref[...].astype(BF16)

    @pl.when(is_tile)
    def _():
        for j in range(tile):
            tbuf[j:j + 1, :] = acc[pl.ds(idx_ref[0, 0, j], 1), :]
        ybuf[...] = _dot(hid_ref[...], wd_s[...])

        def put(j, carry):
            acc[pl.ds(idx_ref[0, 0, j], 1), :] = tbuf[pl.ds(j, 1), :] + gt_ref[0, 0, j] * ybuf[pl.ds(j, 1), :]
            return carry

        lax.fori_loop(0, tile, put, 0, unroll=8)

    @pl.when(t >= n_tiles)
    def _():
        r0 = pl.multiple_of((t - n_tiles) * tm_out, tm_out)
        o_ref[...] = _rms(acc[pl.ds(r0, tm_out), :], g_ref[...])


def _moe_down(hid, slot_tok, slot_gate, tile_e, tile_valid, wd, e_layer, g, n, tile):
    n_tiles = slot_tok.shape[0]
    _, _, f, d = wd.shape
    tm_out = TM
    n_flush = n // tm_out

    def tix(t):
        return jnp.minimum(t, n_tiles - 1)

    smem_tile = pl.BlockSpec((1, 1, tile), lambda t, te, tv: (tix(t), 0, 0), memory_space=pltpu.SMEM)
    grid_spec = pltpu.PrefetchScalarGridSpec(
        num_scalar_prefetch=2,
        grid=(n_tiles + n_flush,),
        in_specs=[smem_tile, smem_tile,
                  pl.BlockSpec((tile, f), lambda t, te, tv: (tix(t), 0)),
                  pl.BlockSpec((None, None, f, d), lambda t, te, tv: (e_layer, te[tix(t)], 0, 0)),
                  pl.BlockSpec((1, d), lambda t, te, tv: (0, 0))],
        out_specs=pl.BlockSpec((tm_out, d), lambda t, te, tv: (jnp.maximum(t - n_tiles, 0), 0)),
        scratch_shapes=[pltpu.VMEM((n + 8, d), F32), pltpu.VMEM((tile, d), F32), pltpu.VMEM((tile, d), F32),
                        pltpu.VMEM((f, d), BF16)],
    )
    return pl.pallas_call(
        functools.partial(_moe_down_kernel, tile=tile, n_tiles=n_tiles, tm_out=tm_out),
        grid_spec=grid_spec,
        out_shape=jax.ShapeDtypeStruct((n, d), F32),
        compiler_params=_cp(("arbitrary",)),
        name="moe_down",
    )(tile_e, tile_valid, slot_tok, slot_gate, hid, wd, g.reshape(1, d))


def kernel(x_prompt, x_sample, state_hgrn, state_s5, state_gdn, c, c_ctx, ada_w, ada_b, norm_g, w_in, hgrn_lb, hgrn_norm, s5_a_re, s5_a_im, s5_log_dt, s5_b_re, s5_b_im, s5_c_re, s5_c_im, s5_d, s5_glu_w, s5_glu_b, gdn_conv, gdn_a_log, gdn_dt_bias, gdn_norm, w_branch, w_out, ffn_wg, ffn_wu, ffn_wd, moe_router, moe_wg, moe_wu, moe_wd):
    bp, tp, d = x_prompt.shape
    bs, ts, _ = x_sample.shape
    depth = w_in.shape[0]
    n_p, n_s = bp * tp, bs * ts
    ne = moe_router.shape[-1]
    main_cols = 10 * BR_WIDTH
    ab_cols = 4 * N_HEADS

    c8 = jnp.zeros((8, d), F32).at[0].set(c_ctx).at[1:1 + bs].set(c)
    mod_all = _ada_mod(c8, ada_w, ada_b)

    lb_soft = jax.nn.softmax(hgrn_lb.astype(F32), axis=0)
    lb_all = jnp.cumsum(lb_soft, axis=0) - lb_soft[:1]

    mods = mod_all[:, :1 + bs].reshape(depth, 1 + bs, 6, d)
    zero_row = jnp.zeros((1 + bs, d), F32)

    def modv(gate, scale, shift, g):
        rows = [zero_row if sel is None else mods[sel[0], :, sel[1]] for sel in (gate, scale, shift)]
        rows.append(zero_row if g is None else jnp.broadcast_to(g, (1 + bs, d)))
        return jnp.stack(rows + [zero_row] * 4, axis=1)

    x = (x_prompt.reshape(n_p, d), x_sample.reshape(n_s, d))
    (h,) = _resnorm(x, None, modv(None, (0, 1), (0, 0), norm_g[0, 0]), n_p, ts, True)

    z_hgrn = jnp.zeros((bp, 2, N_HEADS, HEAD_D, HEAD_D), F32)
    ngb = S5_GROUPS // S5_GB
    w2 = 2 * S5_GB * S5_STATE
    st_s5 = []
    y_zero = jnp.zeros((n_p + n_s, 3 * BR_WIDTH), BF16)
    new_hgrn = new_gdn = jnp.zeros((bp, depth, 2, N_HEADS, HEAD_D, HEAD_D), F32)
    w_in_t = jnp.swapaxes(w_in, 1, 2)
    s5_tb = 1024

    for l in range(depth):
        col_major = l % 2 == 1
        p_main = _mm_t(h, w_in_t, l, main_cols, 2 * BR_WIDTH, "w_in_main", tm=1024)
        w_ab = jnp.pad(w_in_t[l, main_cols:main_cols + ab_cols], ((0, LANES - ab_cols), (0, 0)))
        p_ab = _mm_t(h, w_ab, None, LANES, LANES, "w_in_ab", tm=1024)
        p_mg = _mm_t(h, w_in_t[l, main_cols + ab_cols:], None, 3 * d, 2 * BR_WIDTH, "w_in_merge", tm=1024,
                     out_dtype=BF16)

        blk = BR_WIDTH // LANES
        br = y_zero

        br, new_hgrn = _hgrn(p_main, lb_all[l], hgrn_norm[l], z_hgrn, 0, bp, tp, br, 0, new_hgrn, l, depth)
        br, _ = _hgrn(p_main, lb_all[l], hgrn_norm[l], state_hgrn[:, l].astype(F32), n_p // ts, bs, ts, br, 0,
                      col_major=col_major)

        bb, cc, lam = _s5_params(s5_a_re[l], s5_a_im[l], s5_log_dt[l], s5_b_re[l], s5_b_im[l],
                                 s5_c_re[l], s5_c_im[l])
        seq_per_blk = 8
        tb_p = seq_per_blk * tp
        nblk_p = n_p // tb_p
        br, sb = _s5(p_main, bb, cc, lam, s5_d[l], jnp.zeros((nblk_p, ngb, 2, 8, w2), F32),
                     0, nblk_p, tb_p, tp, tp, True, br, blk)
        s0s = state_s5[:, l].astype(F32)
        s0s = s0s.reshape(bs, 2, ngb, S5_GB * S5_STATE, 2).transpose(0, 2, 1, 4, 3).reshape(bs, ngb, 2, 1, w2)
        s0s = jnp.pad(s0s, ((0, 0), (0, 0), (0, 0), (0, 7), (0, 0)))
        br, _ = _s5(p_main, bb, cc, lam, s5_d[l], s0s, n_p // s5_tb, n_s // s5_tb, s5_tb, ts, S5_SEG, False, br, blk,
                    col_major=col_major)
        br = _glu(br, s5_glu_w, s5_glu_b, l, 1)
        sb = sb[:, :, :, :seq_per_blk].reshape(nblk_p, ngb, 2, seq_per_blk, 2, S5_GB, S5_STATE)
        sb = sb.transpose(0, 3, 2, 1, 5, 6, 4).reshape(bp, 2, S5_GROUPS, S5_STATE, 2)
        st_s5.append(sb)

        br, new_gdn = _gdn(p_main, p_ab, gdn_conv[l], gdn_a_log[l], gdn_dt_bias[l], gdn_norm[l], z_hgrn, 0, bp, tp,
                           br, 2 * blk, new_gdn, l, depth)
        br, _ = _gdn(p_main, p_ab, gdn_conv[l], gdn_a_log[l], gdn_dt_bias[l], gdn_norm[l],
                     state_gdn[:, l].astype(F32), n_p // ts, bs, ts, br, 2 * blk, col_major=col_major)

        x, h2 = _merge(br, p_mg, w_branch, w_out, norm_g[l, 1], l, x,
                       modv((l, 2), (l, 4), (l, 3), norm_g[l, 2]), n_p, ts, BF16 if l % 2 == 0 else F32)
        last = l + 1 == depth
        next_mod = modv((l, 5), None, None, None) if last else modv((l, 5), (l + 1, 1), (l + 1, 0), norm_g[l + 1, 0])

        if l % 2 == 0:
            e = l // 2
            hid = _swiglu_up(h2, ffn_wg, ffn_wu, e, ffn_wg.shape[-1] // 2)
            x, h = _down_res(hid, ffn_wd, e, norm_g[l, 3], x, next_mod, n_p, ts, BF16)
            if last:
                y_prompt, y_sample = x[:n_p], x[n_p:]
            continue
        else:
            e = l // 2
            n_tok = n_p + n_s
            max_tiles = 2 * n_tok // MOE_TILE + ne
            aux, cnt = _router(h2, jnp.pad(moe_router[e], ((0, 0), (0, LANES - ne))), ne)
            slot_tok, slot_gate, tile_e, tile_valid = _moe_plan(aux, cnt, n_tok, ne, MOE_TILE, max_tiles)
            hid = _moe_up(h2, slot_tok, tile_e, tile_valid, moe_wg, moe_wu, e, MOE_TILE)
            r_f = _moe_down(hid, slot_tok, slot_gate, tile_e, tile_valid, moe_wd, e, norm_g[l, 3], n_tok, MOE_TILE)

        if not last:
            x, h = _resnorm(x, r_f, next_mod, n_p, ts, True)
        else:
            (y_prompt,) = _resnorm(x, r_f, next_mod, n_p, ts, False, rows=(0, n_p))
            (y_sample,) = _resnorm(x, r_f, next_mod, n_p, ts, False, rows=(n_p, n_s))

    y_prompt = y_prompt.reshape(bp, tp, d)
    y_sample = y_sample.reshape(bs, ts, d)
    new_s5 = jnp.stack(st_s5, axis=1).astype(x_prompt.dtype)
    return (y_prompt, y_sample, new_hgrn.astype(x_prompt.dtype), new_s5, new_gdn.astype(x_prompt.dtype))
```

```python
import functools
import math

import jax
import jax.numpy as jnp
from jax import lax
from jax.experimental import pallas as pl
from jax.experimental.pallas import tpu as pltpu

F32 = jnp.float32
BF16 = jnp.bfloat16

EPS = 1e-6
F_FLOOR = 1e-6
LOG2E = 1.4426950408889634
GRID_W = 64
HEAD_D = 128
N_HEADS = 4
BR_WIDTH = 512
S5_GROUPS = 32
S5_GROUP = 16
S5_STATE = 64
LANES = 128
V7X_VMEM_BYTES = 64 * 1024 * 1024
VMEM_LIMIT = V7X_VMEM_BYTES - 8 * 1024 * 1024
MOE_UP_VMEM_LIMIT = V7X_VMEM_BYTES - 3 * 1024 * 1024
S5_GB = 8
S5_SEG = 64
GLA_CHUNK = 256
GDN_CHUNK = 64
GDN_SPAN = 256
GDN_HEADS_PER_STEP = 4
GDN_MAX_ROWS_PER_STEP = 2048
TM = 512
MOE_TILE = 256


def _cp(sem, vmem=VMEM_LIMIT):
    return pltpu.CompilerParams(dimension_semantics=sem, vmem_limit_bytes=vmem)


def _dot(a, b):
    return jnp.dot(a, b, preferred_element_type=F32)


def _dot_nt(a, b):
    return lax.dot_general(a, b, (((1,), (1,)), ((), ())), preferred_element_type=F32)


def _dot_tn(a, b):
    return lax.dot_general(a, b, (((0,), (0,)), ((), ())), preferred_element_type=F32)


def _silu(x):
    return x * jax.nn.sigmoid(x)


def _rms(x, g):
    return x * lax.rsqrt(jnp.mean(x * x, axis=-1, keepdims=True) + EPS) * g


def _chunk_cumsum(x, d, c):
    n = x.shape[0]
    pos = lax.broadcasted_iota(jnp.int32, x.shape, 0) & (c - 1)
    s = 1
    while s < c:
        if d == 0:
            x = x + jnp.where(pos >= s, pltpu.roll(x, s, 0), 0.0)
        else:
            x = x + jnp.where(pos < c - s, pltpu.roll(x, n - s, 0), 0.0)
        s *= 2
    return x


def _ada_kernel(c_ref, w_ref, b_ref, o_ref):
    c = c_ref[...]
    a = _silu(c).astype(BF16)
    o_ref[...] = _dot(a, w_ref[...].astype(BF16)) + b_ref[...]


def _ada_mod(c8, ada_w, ada_b):
    depth, d, n6 = ada_w.shape
    tn = 512
    return pl.pallas_call(
        _ada_kernel,
        grid=(depth, n6 // tn),
        in_specs=[
            pl.BlockSpec((8, d), lambda l, j: (0, 0)),
            pl.BlockSpec((None, d, tn), lambda l, j: (l, 0, j)),
            pl.BlockSpec((None, 1, tn), lambda l, j: (l, 0, j)),
        ],
        out_specs=pl.BlockSpec((None, 8, tn), lambda l, j: (l, 0, j)),
        out_shape=jax.ShapeDtypeStruct((depth, 8, n6), F32),
        compiler_params=_cp(("parallel", "parallel")),
        name="ada_mod",
    )(c8, ada_w, ada_b.reshape(depth, 1, n6))


def _two_source_specs(tm, d, na_tiles):
    return [pl.BlockSpec((tm, d), lambda i: (jnp.minimum(i, na_tiles - 1), 0)),
            pl.BlockSpec((tm, d), lambda i: (jnp.maximum(i - na_tiles, 0), 0))]


def _pick_source(xa_ref, xb_ref, na_tiles):
    return jnp.where(pl.program_id(0) < na_tiles, xa_ref[...], xb_ref[...])


def _resnorm_kernel(*refs, has_y, has_h, na_tiles):
    it = iter(refs)
    x_ref = next(it)
    xb_ref = next(it) if na_tiles else None
    y_ref = next(it) if has_y else None
    mv_ref = next(it)
    xo_ref = next(it) if has_y else None
    h_ref = next(it) if has_h else None
    x = _pick_source(x_ref, xb_ref, na_tiles) if na_tiles else x_ref[...]
    mv = mv_ref[...]
    if has_y:
        x = x + mv[0:1] * y_ref[...]
        xo_ref[...] = x
    if has_h:
        h = _rms(x, mv[3:4]) * (1.0 + mv[1:2]) + mv[2:3]
        h_ref[...] = h.astype(h_ref.dtype)


def _resnorm(x, y, modv, n_prompt, seq_s, has_h, rows=None, h_dtype=BF16):
    xs = x if isinstance(x, tuple) else (x,)
    d = xs[0].shape[1]
    row0, n = (0, sum(a.shape[0] for a in xs)) if rows is None else rows
    i0 = row0 // TM
    has_y = y is not None
    na_tiles = xs[0].shape[0] // TM if len(xs) == 2 else 0

    def group(i):
        r = (i + i0) * TM
        return jnp.where(r < n_prompt, 0, 1 + (r - n_prompt) // seq_s)

    tile = pl.BlockSpec((TM, d), lambda i: (i + i0, 0))
    x_specs = _two_source_specs(TM, d, na_tiles) if na_tiles else [tile]
    in_specs = x_specs + ([tile] if has_y else []) + [pl.BlockSpec((None, 8, d), lambda i: (group(i), 0, 0))]
    out_tile = pl.BlockSpec((TM, d), lambda i: (i, 0))
    out_specs, out_shape = [], []
    if has_y:
        out_specs.append(out_tile)
        out_shape.append(jax.ShapeDtypeStruct((n, d), F32))
    if has_h:
        out_specs.append(out_tile)
        out_shape.append(jax.ShapeDtypeStruct((n, d), h_dtype))
    args = list(xs) + ([y] if has_y else []) + [modv]
    outs = pl.pallas_call(
        functools.partial(_resnorm_kernel, has_y=has_y, has_h=has_h, na_tiles=na_tiles),
        grid=(n // TM,),
        in_specs=in_specs,
        out_specs=out_specs,
        out_shape=out_shape,
        compiler_params=_cp(("parallel",)),
        name="resnorm",
    )(*args)
    return outs


def _mm_kernel(x_ref, w_ref, o_ref, wb_ref):
    @pl.when(pl.program_id(1) == 0)
    def _():
        wb_ref[...] = w_ref[...].astype(BF16)

    o_ref[...] = _dot_nt(x_ref[...], wb_ref[...]).astype(o_ref.dtype)


def _mm_t(x, wt, l, nrows, tn, name, tm=TM, out_dtype=F32):
    m, k = x.shape
    if l is None:
        wspec = pl.BlockSpec((tn, k), lambda j, i: (j, 0))
    else:
        wspec = pl.BlockSpec((None, tn, k), lambda j, i: (l, j, 0))
    return pl.pallas_call(
        _mm_kernel,
        grid=(nrows // tn, m // tm),
        in_specs=[pl.BlockSpec((tm, k), lambda j, i: (i, 0)), wspec],
        out_specs=pl.BlockSpec((tm, tn), lambda j, i: (i, j)),
        out_shape=jax.ShapeDtypeStruct((m, nrows), out_dtype),
        scratch_shapes=[pltpu.VMEM((tn, k), BF16)],
        compiler_params=_cp(("arbitrary", "arbitrary")),
        name=name,
    )(x, wt)


def _gla_intra(q, k, b, d, c):
    rowi = lax.broadcasted_iota(jnp.int32, (c, LANES), 0)
    ri = lax.broadcasted_iota(jnp.int32, (c, c), 0)
    ci = lax.broadcasted_iota(jnp.int32, (c, c), 1)
    a = jnp.zeros((c, c), F32)
    size = c
    while size >= 8:
        half = size // 2
        nb = c // size
        b3 = b.reshape(nb, size, LANES)
        pos = rowi & (size - 1)
        if d == 0:
            ref = b3[:, half - 1:half, :]
            qsel = pos >= half
        else:
            ref = b3[:, half:half + 1, :]
            qsel = pos < half
        ref = jnp.broadcast_to(ref, (nb, size, LANES)).reshape(c, LANES)
        z = jnp.where(qsel, q, k) * jnp.exp2(-jnp.abs(b - ref))
        qs = jnp.where(qsel, z, 0.0).astype(BF16)
        ks = jnp.where(qsel, 0.0, z).astype(BF16)
        p = _dot_nt(qs, ks)
        if nb > 1:
            shift = int(math.log2(size))
            p = jnp.where((ri >> shift) == (ci >> shift), p, 0.0)
        a = a + p
        size = half
    b3 = b.reshape(c // 8, 8, LANES)
    lo, hi = (0, 4) if d == 0 else (3, 7)
    ref_lo = jnp.broadcast_to(b3[:, lo:lo + 1, :], (c // 8, 8, LANES)).reshape(c, LANES)
    ref_hi = jnp.broadcast_to(b3[:, hi:hi + 1, :], (c // 8, 8, LANES)).reshape(c, LANES)
    ref = jnp.where((rowi & 7) >= 4, ref_hi, ref_lo)
    qs = (q * jnp.exp2(jnp.minimum(b - ref, 0.0))).astype(BF16)
    ks = (k * jnp.exp2(ref - b)).astype(BF16)
    p = _dot_nt(qs, ks)
    keep = ((ri >> 2) == (ci >> 2)) & ((ci <= ri) if d == 0 else (ci >= ri))
    return a + jnp.where(keep, p, 0.0)


def _load_column_major(dst_ref, src_ref, t, stage=None):
    rows = t // GRID_W
    for k in range(src_ref.shape[1] // LANES):
        lanes = slice(k * LANES, (k + 1) * LANES)
        if stage is None:
            strided = src_ref
        else:
            stage[...] = src_ref[:, lanes]
            strided = stage
        for col in range(GRID_W):
            dst_ref[col * rows:(col + 1) * rows, lanes] = strided[pl.ds(col, rows, stride=GRID_W), :]


def _store_row_major(dst_ref, y, t, stage=None):
    rows = t // GRID_W
    for k in range(y.shape[1] // LANES):
        lanes = slice(k * LANES, (k + 1) * LANES)
        strided = dst_ref if stage is None else stage
        for col in range(GRID_W):
            strided[pl.ds(col, rows, stride=GRID_W), :] = y[col * rows:(col + 1) * rows, lanes]
        if stage is not None:
            dst_ref[:, lanes] = stage[...].astype(dst_ref.dtype)


def _hgrn_kernel(aq_ref, af_ref, ab_ref, ai_ref, ag_ref, lb_ref, g_ref, s0_ref, y_ref, s_ref, o_acc, *perm_bufs,
                 t, c):
    stage = None
    if perm_bufs:
        *perm_bufs, stage = perm_bufs
        srcs = (aq_ref, af_ref, ab_ref, ai_ref, ag_ref)
        for dst, src in zip(perm_bufs, srcs):
            _load_column_major(dst, src, t)
        aq_ref, af_ref, ab_ref, ai_ref, ag_ref = perm_bufs
    nchunk = t // c
    scale = HEAD_D ** -0.5
    for d in (0, 1):
        z_ref = af_ref if d == 0 else ab_ref
        lbd = lb_ref[d:d + 1, :]

        def chunk(i, st, d=d, z_ref=z_ref, lbd=lbd):
            cidx = i if d == 0 else nchunk - 1 - i
            r0 = pl.multiple_of(cidx * c, c)
            z = z_ref[pl.ds(r0, c), :]
            f = lbd + (1.0 - lbd) * jax.nn.sigmoid(z)
            lf = jnp.log2(jnp.maximum(f, F_FLOOR))
            k = (1.0 - lbd) * jax.nn.sigmoid(-z)
            q = _silu(aq_ref[pl.ds(r0, c), :]) * scale
            v = ai_ref[pl.ds(r0, c), :].astype(BF16)
            b = _chunk_cumsum(lf, d, c)
            a = _gla_intra(q, k, b, d, c)
            o = _dot(a.astype(BF16), v) + _dot_nt((q * jnp.exp2(b)).astype(BF16), st.astype(BF16))
            if d == 0:
                o_acc[pl.ds(r0, c), :] = o
            else:
                o_acc[pl.ds(r0, c), :] += o
            blast = b[c - 1:c, :] if d == 0 else b[0:1, :]
            kd = (k * jnp.exp2(blast - b)).astype(BF16)
            return st * jnp.exp2(blast) + _dot_tn(v, kd)

        st = s0_ref[0, d, 0].T
        if nchunk == 1:
            st = chunk(0, st)
        else:
            st = lax.fori_loop(0, nchunk, chunk, st)
        s_ref[0, d, 0] = st.T
    y = _rms(o_acc[...], g_ref[...]) * _silu(ag_ref[...])
    if perm_bufs:
        _store_row_major(y_ref, y, t, stage)
    else:
        y_ref[...] = y.astype(y_ref.dtype)


def _shared_out_call(kernel_fn, prevs, in_specs, args, **kw):
    extra = [p for p in prevs if p is not None]
    if not extra:
        return pl.pallas_call(kernel_fn, in_specs=in_specs, **kw)(*args)
    n_in = len(in_specs)
    aliases, k = {}, 0
    for out_idx, p in enumerate(prevs):
        if p is not None:
            aliases[n_in + k] = out_idx
            k += 1

    def body(*refs):
        kernel_fn(*refs[:n_in], *refs[n_in + len(extra):])

    return pl.pallas_call(body, in_specs=in_specs + [pl.BlockSpec(memory_space=pl.ANY)] * len(extra),
                          input_output_aliases=aliases, **kw)(*args, *extra)


def _state_out(nseq, slot, nslots, hg):
    spec = pl.BlockSpec((1, None, 2, hg, HEAD_D, HEAD_D), lambda s, h: (s, slot, 0, h, 0, 0))
    return spec, jax.ShapeDtypeStruct((nseq, nslots, 2, N_HEADS, HEAD_D, HEAD_D), F32)


def _hgrn(p_main, lb, norm_g, s0, row_blk0, nseq, t, y_prev, col0, s_prev=None, slot=0, nslots=1,
          col_major=False):
    c = min(GLA_CHUNK, t)
    seq_buf = pltpu.VMEM((t, HEAD_D), F32)

    def col(base):
        return pl.BlockSpec((t, HEAD_D), lambda s, h: (row_blk0 + s, base + h))

    st_spec = pl.BlockSpec((1, 2, 1, HEAD_D, HEAD_D), lambda s, h: (s, 0, h, 0, 0))
    so_spec, so_shape = _state_out(nseq, slot, nslots, 1)
    return _shared_out_call(
        functools.partial(_hgrn_kernel, t=t, c=c), [y_prev, s_prev],
        [col(0), col(4), col(8), col(12), col(16),
         pl.BlockSpec((2, HEAD_D), lambda s, h: (0, h)),
         pl.BlockSpec((1, HEAD_D), lambda s, h: (0, 0)),
         st_spec],
        (p_main, p_main, p_main, p_main, p_main, lb, norm_g.reshape(1, HEAD_D), s0),
        grid=(nseq, N_HEADS),
        out_specs=[pl.BlockSpec((t, HEAD_D), lambda s, h: (row_blk0 + s, col0 + h)), so_spec],
        out_shape=[jax.ShapeDtypeStruct(y_prev.shape, y_prev.dtype), so_shape],
        scratch_shapes=[seq_buf] * (7 if col_major else 1),
        compiler_params=_cp(("parallel", "parallel")),
        name="hgrn2",
    )


def _dot_b(a, b):
    return _dot(a.astype(BF16), b.astype(BF16))


def _tri_solve(ms, xs, c):
    ri = lax.broadcasted_iota(jnp.int32, (c, c), 0)
    ci = lax.broadcasted_iota(jnp.int32, (c, c), 1)
    eye = jnp.where(ri == ci, 1.0, 0.0)
    m8 = [jnp.where((ri >> 3) == (ci >> 3), m, 0.0) for m in ms]
    ts = [eye - m for m in m8]
    ps = [_dot_b(m, m) for m in m8]
    ts = [t + _dot_b(t, p) for t, p in zip(ts, ps)]
    ps = [_dot_b(p, p) for p in ps]
    ts = [t + _dot_b(t, p) for t, p in zip(ts, ps)]
    shift = 3
    while (1 << shift) < c:
        join = ((ri >> (shift + 1)) == (ci >> (shift + 1))) & ((ri >> shift) != (ci >> shift))
        lts = [_dot_b(jnp.where(join, m, 0.0), t) for m, t in zip(ms, ts)]
        ts = [t - _dot_b(t, lt) for t, lt in zip(ts, lts)]
        shift += 1
    return [_dot_b(t, x) for t, x in zip(ts, xs)]


def _gdn_kernel(cq_ref, ck_ref, cv_ref, cg_ref, ab_ref, cw_ref, alog_ref, dtb_ref, g_ref, s0_ref,
                y_ref, s_ref, q_s, k_s, v_s, la_s, be_s, w_s, qk_s, qe_s, kd_s, dec_s, o_acc, *perm_bufs,
                t, c, span, hg):
    stage = None
    if perm_bufs:
        *perm_bufs, stage = perm_bufs
        srcs = (cq_ref, ck_ref, cv_ref, cg_ref, ab_ref)
        for dst, src in zip(perm_bufs, srcs):
            _load_column_major(dst, src, t, stage)
        cq_ref, ck_ref, cv_ref, cg_ref, ab_ref = perm_bufs
    nchunk = t // c
    cps = span // c
    rowt = lax.broadcasted_iota(jnp.int32, (t, HEAD_D), 0)
    lane = lax.broadcasted_iota(jnp.int32, (t, LANES), 1)
    ri = lax.broadcasted_iota(jnp.int32, (span, span), 0)
    ci = lax.broadcasted_iota(jnp.int32, (span, span), 1)
    shift = int(math.log2(c))
    same = (ri >> shift) == (ci >> shift)
    heads = range(hg)
    dirs = (0, 1)

    def conv(x_ref, j, hh):
        x = x_ref[:, hh * HEAD_D:(hh + 1) * HEAD_D]
        w = cw_ref[j][:, hh * HEAD_D:(hh + 1) * HEAD_D]
        xm = jnp.where(rowt == 0, 0.0, pltpu.roll(x, 1, 0))
        xp = jnp.where(rowt == t - 1, 0.0, pltpu.roll(x, t - 1, 0))
        return xm * w[0:1] + x * w[1:2] + xp * w[2:3]

    ab = ab_ref[...]
    lane1 = lax.broadcasted_iota(jnp.int32, (1, LANES), 1)
    alog_l = jnp.zeros((1, LANES), F32)
    dtb_l = jnp.zeros((1, LANES), F32)
    for d in dirs:
        for hd in range(N_HEADS):
            alog_l = jnp.where(lane1 == d * N_HEADS + hd, alog_ref[d, hd], alog_l)
            dtb_l = jnp.where(lane1 == d * N_HEADS + hd, dtb_ref[d, hd], dtb_l)
    xs = ab + dtb_l
    log_a_all = (-LOG2E) * jnp.exp(alog_l) * (jnp.maximum(xs, 0.0) + jnp.log1p(jnp.exp(-jnp.abs(xs))))
    beta_all = jax.nn.sigmoid(ab)
    for hh in heads:
        h = pl.program_id(1) * hg + hh
        q = _silu(conv(cq_ref, 0, hh))
        q_s[hh] = q * lax.rsqrt(jnp.sum(q * q, axis=-1, keepdims=True) + EPS) * (HEAD_D ** -0.5)
        k = _silu(conv(ck_ref, 1, hh))
        k_s[hh] = k * lax.rsqrt(jnp.sum(k * k, axis=-1, keepdims=True) + EPS)
        v_s[hh] = _silu(conv(cv_ref, 2, hh))
        for d in dirs:
            la_col = jnp.sum(jnp.where(lane == d * N_HEADS + h, log_a_all, 0.0), axis=-1, keepdims=True)
            be_col = jnp.sum(jnp.where(lane == 2 * N_HEADS + d * N_HEADS + h, beta_all, 0.0), axis=-1,
                             keepdims=True)
            la_s[hh, d] = jnp.broadcast_to(la_col, (t, LANES))
            be_s[hh, d] = jnp.broadcast_to(be_col, (t, LANES))
    o_acc[...] = jnp.zeros(o_acc.shape, F32)

    def wy_span(sp, carry):
        r0 = pl.multiple_of(sp * span, span)
        ms, xs, dst = [], [], []
        for hh in heads:
            qsp = q_s[hh, pl.ds(r0, span), :]
            ksp = k_s[hh, pl.ds(r0, span), :]
            vsp = v_s[hh, pl.ds(r0, span), :]
            kb = ksp.astype(BF16)
            kk = _dot_nt(kb, kb)
            qk = _dot_nt(qsp.astype(BF16), kb)
            for d in dirs:
                be = be_s[hh, d, pl.ds(r0, span), :]
                g = _chunk_cumsum(la_s[hh, d, pl.ds(r0, span), :], d, c)
                gt = g.T
                gcol = jnp.concatenate([g] * (span // LANES), axis=1)
                grow = jnp.concatenate([gt] * (span // LANES), axis=0)
                incl = same & ((ci <= ri) if d == 0 else (ci >= ri))
                gamma = jnp.where(incl, jnp.exp2(jnp.minimum(gcol - grow, 0.0)), 0.0)
                bcol = jnp.concatenate([be] * (span // LANES), axis=1)
                m_sp = jnp.where(ci == ri, 0.0, bcol * kk * gamma)
                qkg = (qk * gamma).astype(BF16)
                eg = jnp.exp2(g)
                xv = be * vsp
                xk = be * eg * ksp
                qe_s[hh, d, pl.ds(r0, span), :] = (qsp * eg).astype(BF16)
                for j in range(cps):
                    lo = j * c
                    rj = pl.multiple_of(r0 + lo, c)
                    ms.append(m_sp[lo:lo + c, lo:lo + c])
                    xs.append(jnp.concatenate([xv[lo:lo + c], xk[lo:lo + c]], axis=1))
                    dst.append((hh, d, rj))
                    qk_s[hh, d, pl.ds(rj, c), 0:c] = qkg[lo:lo + c, lo:lo + c]
                    gj = g[lo:lo + c]
                    glast = gj[c - 1:c] if d == 0 else gj[0:1]
                    kd_s[hh, d, pl.ds(rj, c), :] = (ksp[lo:lo + c] * jnp.exp2(glast - gj)).astype(BF16)
                    dec_s[hh, d, pl.ds(sp * cps + j, 1), :] = jnp.exp2(glast)
        for (hh, d, rj), w in zip(dst, _tri_solve(ms, xs, c)):
            w_s[hh, d, pl.ds(rj, c), :] = w
        return carry

    if t == span:
        wy_span(0, 0)
    else:
        lax.fori_loop(0, t // span, wy_span, 0)

    chains = [(hh, d) for hh in heads for d in dirs]

    def scan_step(i, carry):
        cidx = (i, nchunk - 1 - i)
        r0 = [pl.multiple_of(ck * c, c) for ck in cidx]
        sb = [s.astype(BF16) for s in carry]
        ws = [w_s[hh, d, pl.ds(r0[d], c), :] for hh, d in chains]
        wks = [_dot(w[:, HEAD_D:].astype(BF16), s) for w, s in zip(ws, sb)]
        oq = [_dot(qe_s[hh, d, pl.ds(r0[d], c), :], s) for (hh, d), s in zip(chains, sb)]
        vnb = [(w[:, :HEAD_D] - wk).astype(BF16) for w, wk in zip(ws, wks)]
        new = [s * dec_s[hh, d, pl.ds(cidx[d], 1), :] + _dot_tn(kd_s[hh, d, pl.ds(r0[d], c), :], v)
               for (hh, d), s, v in zip(chains, carry, vnb)]
        for (hh, d), o, v in zip(chains, oq, vnb):
            o_acc[hh, pl.ds(r0[d], c), :] += o + _dot(qk_s[hh, d, pl.ds(r0[d], c), 0:c], v)
        return tuple(new)

    finals = lax.fori_loop(0, nchunk, scan_step, tuple(s0_ref[0, d, hh] for hh, d in chains),
                           unroll=nchunk <= 4)
    for (hh, d), s in zip(chains, finals):
        s_ref[0, d, hh] = s
    ys = []
    for hh in heads:
        sl = slice(hh * HEAD_D, (hh + 1) * HEAD_D)
        ys.append(_rms(o_acc[hh], g_ref[...]) * _silu(cg_ref[:, sl]))
    y = jnp.concatenate(ys, axis=1)
    if perm_bufs:
        _store_row_major(y_ref, y, t, stage)
    else:
        y_ref[...] = y.astype(y_ref.dtype)


def _gdn(p_main, p_ab, conv_w, a_log, dt_bias, norm_g, s0, row_blk0, nseq, t, y_prev, col0, s_prev=None,
         slot=0, nslots=1, col_major=False):
    c = GDN_CHUNK
    hg = GDN_HEADS_PER_STEP if t * GDN_HEADS_PER_STEP <= GDN_MAX_ROWS_PER_STEP else GDN_HEADS_PER_STEP // 2
    wid = hg * HEAD_D
    blk = BR_WIDTH // wid

    def col(base):
        return pl.BlockSpec((t, wid), lambda s, h: (row_blk0 + s, base * blk + h))

    st_spec = pl.BlockSpec((1, 2, hg, HEAD_D, HEAD_D), lambda s, h: (s, 0, h, 0, 0))
    smem = pl.BlockSpec(memory_space=pltpu.SMEM)
    seq_buf = pltpu.VMEM((hg, t, HEAD_D), F32)
    dir_f32 = pltpu.VMEM((hg, 2, t, LANES), F32)
    dir_bf16 = pltpu.VMEM((hg, 2, t, LANES), BF16)
    scratch = [seq_buf, seq_buf, seq_buf, dir_f32, dir_f32,
               pltpu.VMEM((hg, 2, t, 2 * HEAD_D), F32), dir_bf16, dir_bf16, dir_bf16,
               pltpu.VMEM((hg, 2, max(t // c, 8), LANES), F32), seq_buf]
    if col_major:
        scratch += [pltpu.VMEM((t, wid), F32)] * 4 + [pltpu.VMEM((t, LANES), F32)] * 2
    so_spec, so_shape = _state_out(nseq, slot, nslots, hg)
    return _shared_out_call(
        functools.partial(_gdn_kernel, t=t, c=c, span=GDN_SPAN, hg=hg), [y_prev, s_prev],
        [col(6), col(7), col(8), col(9),
         pl.BlockSpec((t, LANES), lambda s, h: (row_blk0 + s, 0)),
         pl.BlockSpec((3, 3, wid), lambda s, h: (0, 0, h)),
         smem, smem,
         pl.BlockSpec((1, HEAD_D), lambda s, h: (0, 0)),
         st_spec],
        (p_main, p_main, p_main, p_main, p_ab, conv_w, a_log, dt_bias, norm_g.reshape(1, HEAD_D), s0),
        grid=(nseq, N_HEADS // hg),
        out_specs=[pl.BlockSpec((t, wid), lambda s, h: (row_blk0 + s, col0 * LANES // wid + h)), so_spec],
        out_shape=[jax.ShapeDtypeStruct(y_prev.shape, y_prev.dtype), so_shape],
        scratch_shapes=scratch,
        compiler_params=_cp(("parallel", "parallel")),
        name="gdn",
    )


def _cmul(ar, ai, br, bi):
    return ar * br - ai * bi, ar * bi + ai * br


def _s5_kernel(u_ref, bb_ref, cc_ref, lam_ref, dsk_ref, s0_ref, y_ref, sf_ref, up_s, x_s, yp_s,
               *, tb, seg, segs_per_seq, zero_init, col_major):
    r_par = tb // seg
    half = S5_GB * S5_STATE
    if col_major:
        rows = tb // GRID_W
        assert segs_per_seq == r_par and seg % rows == 0

        def step_rows(r):
            return pl.ds((r % rows) * GRID_W + r // rows, r_par, stride=seg // rows)
    else:
        def step_rows(r):
            return pl.ds(r, r_par, stride=seg)

    for r in range(seg):
        up_s[r * r_par:(r + 1) * r_par, :] = u_ref[step_rows(r), :]
    ub = up_s[...].astype(BF16)
    sf_ref[...] = jnp.zeros(sf_ref.shape, F32)
    nseq = r_par // segs_per_seq
    lam = [(lam_ref[0, d, 0:1, :], lam_ref[0, d, 1:2, :]) for d in (0, 1)]
    for d in (0, 1):
        x_s[d] = _dot(ub, bb_ref[0, d])

    def offsets(i):
        return [pl.multiple_of(r * r_par, r_par) for r in (i, seg - 1 - i)]

    def local_step(i, carry):
        off = offsets(i)
        new = []
        for d in (0, 1):
            (lre, lim), (hre, him) = lam[d], carry[d]
            nre = lre * hre - lim * him + x_s[d, pl.ds(off[d], r_par), 0:half]
            nim = lre * him + lim * hre + x_s[d, pl.ds(off[d], r_par), half:2 * half]
            x_s[d, pl.ds(off[d], r_par), 0:half] = nre
            x_s[d, pl.ds(off[d], r_par), half:2 * half] = nim
            new.append((nre, nim))
        return tuple(new)

    zero = jnp.zeros((r_par, half), F32)
    lax.fori_loop(0, seg, local_step, ((zero, zero), (zero, zero)), unroll=4)

    end_rows = ((seg - 1) * r_par, 0)
    needs_carry = not (zero_init and segs_per_seq == 1)
    hins = []
    for d in (0, 1):
        if not needs_carry:
            break
        lre, lim = lam[d]
        pre, pim = lre, lim
        n = 1
        while n < seg:
            pre, pim = _cmul(pre, pim, pre, pim)
            n *= 2
        end_row = end_rows[d]
        ere = x_s[d, end_row:end_row + r_par, 0:half]
        eim = x_s[d, end_row:end_row + r_par, half:2 * half]
        hin = [None] * r_par
        order = range(r_par) if d == 0 else range(r_par - 1, -1, -1)
        for j in order:
            sq, pos = divmod(j, segs_per_seq)
            first = pos == (0 if d == 0 else segs_per_seq - 1)
            if first:
                hin[j] = (s0_ref[0, 0, d, sq:sq + 1, 0:half], s0_ref[0, 0, d, sq:sq + 1, half:2 * half])
            else:
                pj = j - 1 if d == 0 else j + 1
                cre, cim = _cmul(pre, pim, hin[pj][0], hin[pj][1])
                hin[j] = (cre + ere[pj:pj + 1], cim + eim[pj:pj + 1])
        hins.append((jnp.concatenate([hj[0] for hj in hin], axis=0),
                     jnp.concatenate([hj[1] for hj in hin], axis=0)))

    def carry_step(i, pws):
        off = offsets(i)
        new = []
        for d in (0, 1):
            (lre, lim), (wre, wim) = lam[d], pws[d]
            are, aim = _cmul(wre, wim, hins[d][0], hins[d][1])
            x_s[d, pl.ds(off[d], r_par), 0:half] += are
            x_s[d, pl.ds(off[d], r_par), half:2 * half] += aim
            new.append(_cmul(wre, wim, lre, lim))
        return tuple(new)

    if needs_carry:
        lax.fori_loop(0, seg, carry_step, (lam[0], lam[1]), unroll=4)

    for d in (0, 1):
        for sq in range(nseq):
            j = (sq + 1) * segs_per_seq - 1 if d == 0 else sq * segs_per_seq
            row = end_rows[d] + j
            sf_ref[0, 0, d, sq:sq + 1, :] = x_s[d, row:row + 1, :]
    yp_s[...] = (_dot(x_s[0].astype(BF16), cc_ref[0, 0]) + _dot(x_s[1].astype(BF16), cc_ref[0, 1])
                 + up_s[...] * dsk_ref[...])
    for r in range(seg):
        up_s[step_rows(r), :] = yp_s[r * r_par:(r + 1) * r_par, :]
    y_ref[...] = up_s[...].astype(y_ref.dtype)


def _s5(p_main, bb, cc, lam, dskip, s0, row_blk0, nblk, tb, t, seg, zero_init, y_prev, col0, col_major=False):
    ngb = S5_GROUPS // S5_GB
    w2 = 2 * S5_GB * S5_STATE
    return _shared_out_call(
        functools.partial(_s5_kernel, tb=tb, seg=seg, segs_per_seq=t // seg, zero_init=zero_init,
                          col_major=col_major), [y_prev, None],
        [
            pl.BlockSpec((tb, LANES), lambda i, gb: (row_blk0 + i, 20 + gb)),
            pl.BlockSpec((1, 2, LANES, w2), lambda i, gb: (gb, 0, 0, 0)),
            pl.BlockSpec((1, 2, w2, LANES), lambda i, gb: (gb, 0, 0, 0)),
            pl.BlockSpec((1, 2, 2, w2 // 2), lambda i, gb: (gb, 0, 0, 0)),
            pl.BlockSpec((1, LANES), lambda i, gb: (0, gb)),
            pl.BlockSpec((1, 1, 2, 8, w2), lambda i, gb: (i, gb, 0, 0, 0)),
        ],
        (p_main, bb, cc, lam, dskip.reshape(1, BR_WIDTH), s0),
        grid=(nblk, ngb),
        out_specs=[pl.BlockSpec((tb, LANES), lambda i, gb: (row_blk0 + i, col0 + gb)),
                   pl.BlockSpec((1, 1, 2, 8, w2), lambda i, gb: (i, gb, 0, 0, 0))],
        out_shape=[jax.ShapeDtypeStruct(y_prev.shape, y_prev.dtype),
                   jax.ShapeDtypeStruct((nblk, ngb, 2, 8, w2), F32)],
        scratch_shapes=[pltpu.VMEM((tb, LANES), F32), pltpu.VMEM((2, tb, w2), F32), pltpu.VMEM((tb, LANES), F32)],
        compiler_params=_cp(("parallel", "parallel")),
        name="s5",
    )


def _s5_params(a_re, a_im, log_dt, b_re, b_im, c_re, c_im):
    dt = jnp.exp(log_dt)[..., None]
    mag = jnp.exp(dt * a_re)
    lam_re = mag * jnp.cos(dt * a_im)
    lam_im = mag * jnp.sin(dt * a_im)
    den = a_re * a_re + a_im * a_im
    co_re = ((lam_re - 1.0) * a_re + lam_im * a_im) / den
    co_im = (lam_im * a_re - (lam_re - 1.0) * a_im) / den
    bb_re = co_re[..., None] * b_re - co_im[..., None] * b_im
    bb_im = co_re[..., None] * b_im + co_im[..., None] * b_re
    ngb = S5_GROUPS // S5_GB
    eye = jnp.eye(S5_GB, dtype=F32)

    def pack_in(w):
        w = w.reshape(2, ngb, S5_GB, S5_STATE, S5_GROUP)
        blk = jnp.einsum('dbgpi,gh->bdgihp', w, eye)
        return blk.reshape(ngb, 2, S5_GB * S5_GROUP, S5_GB * S5_STATE)

    def pack_out(w):
        w = w.reshape(2, ngb, S5_GB, S5_GROUP, S5_STATE)
        blk = jnp.einsum('dbgip,gh->bdgphi', w, eye)
        return blk.reshape(ngb, 2, S5_GB * S5_STATE, S5_GB * S5_GROUP)

    bb = jnp.concatenate([pack_in(bb_re), pack_in(bb_im)], axis=-1).astype(BF16)
    cc = jnp.concatenate([pack_out(c_re), -pack_out(c_im)], axis=-2).astype(BF16)
    lam = jnp.stack([lam_re, lam_im], axis=1)
    lam = lam.reshape(2, 2, ngb, S5_GB * S5_STATE).transpose(2, 0, 1, 3)
    return bb, cc, lam


def _glu_kernel(y_ref, w_ref, b_ref, o_ref, wb_ref):
    @pl.when(pl.program_id(0) == 0)
    def _():
        wb_ref[...] = w_ref[...].astype(BF16)

    z = _dot(y_ref[...].astype(BF16), wb_ref[...]) + b_ref[...]
    half = z.shape[1] // 2
    o_ref[...] = (z[:, :half] * jax.nn.sigmoid(z[:, half:])).astype(o_ref.dtype)


def _glu(y, w, b, l, blk):
    n = y.shape[0]
    k, n2 = w.shape[-2:]
    assert n2 // 2 == k
    spec = pl.BlockSpec((TM, k), lambda i: (i, blk))
    return pl.pallas_call(
        _glu_kernel,
        grid=(n // TM,),
        in_specs=[spec,
                  pl.BlockSpec((None, k, n2), lambda i: (l, 0, 0)),
                  pl.BlockSpec((None, 1, n2), lambda i: (l, 0, 0))],
        out_specs=spec,
        out_shape=jax.ShapeDtypeStruct(y.shape, y.dtype),
        input_output_aliases={0: 0},
        scratch_shapes=[pltpu.VMEM((k, n2), BF16)],
        compiler_params=_cp(("arbitrary",)),
        name="s5_glu",
    )(y, w, b.reshape(b.shape[0], 1, n2))


def _residual_epilogue(y, g_ref, x_old, mv_ref, xo_ref, h_ref):
    mv = mv_ref[...]
    x = x_old + mv[0:1] * _rms(y, g_ref[...])
    xo_ref[...] = x
    h_ref[...] = (_rms(x, mv[3:4]) * (1.0 + mv[1:2]) + mv[2:3]).astype(h_ref.dtype)


def _group_of_tile(tm, n_prompt, seq_s):
    def group(i):
        return jnp.where(i * tm < n_prompt, 0, 1 + (i * tm - n_prompt) // seq_s)
    return group


def _merge_kernel(ya_ref, yb_ref, yc_ref, mg_ref, wbr_ref, wout_ref, g_ref, *rest, na_tiles):
    if na_tiles:
        x_ref, xb_ref, mv_ref, xo_ref, h_ref, wbr_s, wout_s = rest
    else:
        x_ref, mv_ref, xo_ref, h_ref, wbr_s, wout_s = rest

    @pl.when(pl.program_id(0) == 0)
    def _():
        wbr_s[...] = wbr_ref[...].astype(BF16)
        wout_s[...] = wout_ref[...].astype(BF16)

    d = wout_s.shape[0]
    acc = None
    for n, br_ref in enumerate((ya_ref, yb_ref, yc_ref)):
        proj = _dot(br_ref[...].astype(BF16), wbr_s[n])
        term = proj * jax.nn.sigmoid(mg_ref[:, n * d:(n + 1) * d].astype(F32))
        acc = term if acc is None else acc + term
    y = _dot(acc.astype(BF16), wout_s[...])
    x_old = _pick_source(x_ref, xb_ref, na_tiles) if na_tiles else x_ref[...]
    _residual_epilogue(y, g_ref, x_old, mv_ref, xo_ref, h_ref)


def _merge(branches, mg, w_branch, w_out, g, l, x, modv, n_prompt, seq_s, h_dtype):
    n = branches.shape[0]
    d = w_out.shape[-1]
    tm = TM
    group = _group_of_tile(tm, n_prompt, seq_s)
    tile = pl.BlockSpec((tm, d), lambda i: (i, 0))
    xs = x if isinstance(x, tuple) else (x,)
    na_tiles = xs[0].shape[0] // tm if len(xs) == 2 else 0
    x_specs = _two_source_specs(tm, d, na_tiles) if na_tiles else [tile]
    return pl.pallas_call(
        functools.partial(_merge_kernel, na_tiles=na_tiles),
        grid=(n // tm,),
        in_specs=[pl.BlockSpec((tm, BR_WIDTH), lambda i: (i, 0)),
                  pl.BlockSpec((tm, BR_WIDTH), lambda i: (i, 1)),
                  pl.BlockSpec((tm, BR_WIDTH), lambda i: (i, 2)),
                  pl.BlockSpec((tm, 3 * d), lambda i: (i, 0)),
                  pl.BlockSpec((None, 3, BR_WIDTH, d), lambda i: (l, 0, 0, 0), pipeline_mode=pl.Buffered(1)),
                  pl.BlockSpec((None, d, d), lambda i: (l, 0, 0), pipeline_mode=pl.Buffered(1)),
                  pl.BlockSpec((1, d), lambda i: (0, 0)),
                  *x_specs,
                  pl.BlockSpec((None, 8, d), lambda i: (group(i), 0, 0))],
        out_specs=[tile, tile],
        out_shape=[jax.ShapeDtypeStruct((n, d), F32), jax.ShapeDtypeStruct((n, d), h_dtype)],
        scratch_shapes=[pltpu.VMEM((3, BR_WIDTH, d), BF16), pltpu.VMEM((d, d), BF16)],
        compiler_params=_cp(("arbitrary",)),
        name="merge",
    )(branches, branches, branches, mg, w_branch, w_out, g.reshape(1, d), *xs, modv)


def _swiglu_kernel(x_ref, wg_ref, wu_ref, o_ref, wg_s, wu_s):
    @pl.when(pl.program_id(1) == 0)
    def _():
        wg_s[...] = wg_ref[...].astype(BF16)
        wu_s[...] = wu_ref[...].astype(BF16)

    x = x_ref[...]
    o_ref[...] = (_silu(_dot(x, wg_s[...])) * _dot(x, wu_s[...])).astype(BF16)


def _swiglu_up(x, wg, wu, e, f):
    n, d = x.shape
    nf = wg.shape[-1] // f
    wspec = pl.BlockSpec((None, d, f), lambda j, i: (e, 0, j))
    return pl.pallas_call(
        _swiglu_kernel,
        grid=(nf, n // TM),
        in_specs=[pl.BlockSpec((TM, d), lambda j, i: (i, 0)), wspec, wspec],
        out_specs=pl.BlockSpec((TM, f), lambda j, i: (i, j)),
        out_shape=jax.ShapeDtypeStruct((n, nf * f), BF16),
        scratch_shapes=[pltpu.VMEM((d, f), BF16), pltpu.VMEM((d, f), BF16)],
        compiler_params=_cp(("arbitrary", "arbitrary")),
        name="swiglu_up",
    )(x, wg, wu)


def _down_kernel(hid_ref, w_ref, g_ref, x_ref, mv_ref, xo_ref, h_ref, w_s):
    @pl.when(pl.program_id(0) == 0)
    def _():
        w_s[...] = w_ref[...].astype(BF16)

    _residual_epilogue(_dot(hid_ref[...], w_s[...]), g_ref, x_ref[...], mv_ref, xo_ref, h_ref)


def _down_res(hid, wd, e, g, x, modv, n_prompt, seq_s, h_dtype):
    n, f = hid.shape
    d = wd.shape[-1]
    group = _group_of_tile(TM, n_prompt, seq_s)
    tile = pl.BlockSpec((TM, d), lambda i: (i, 0))
    return pl.pallas_call(
        _down_kernel,
        grid=(n // TM,),
        in_specs=[pl.BlockSpec((TM, f), lambda i: (i, 0)),
                  pl.BlockSpec((None, f, d), lambda i: (e, 0, 0)),
                  pl.BlockSpec((1, d), lambda i: (0, 0)),
                  tile,
                  pl.BlockSpec((None, 8, d), lambda i: (group(i), 0, 0))],
        out_specs=[tile, tile],
        out_shape=[jax.ShapeDtypeStruct((n, d), F32), jax.ShapeDtypeStruct((n, d), h_dtype)],
        scratch_shapes=[pltpu.VMEM((f, d), BF16)],
        compiler_params=_cp(("arbitrary",)),
        name="down_res",
    )(hid, wd, g.reshape(1, d), x, modv)


def _router_kernel(x_ref, w_ref, aux_ref, cnt_ref, base_s, *, ne):
    @pl.when(pl.program_id(0) == 0)
    def _():
        base_s[...] = jnp.zeros(base_s.shape, F32)

    w = w_ref[...]
    w_hi = w.astype(BF16)
    w_lo = (w - w_hi.astype(F32)).astype(BF16)
    x = x_ref[...].astype(BF16)
    logits = _dot(x, w_hi) + _dot(x, w_lo)
    tm = logits.shape[0]
    lane = lax.broadcasted_iota(jnp.int32, logits.shape, 1).astype(F32)
    neg = jnp.float32(-jnp.inf)
    logits = jnp.where(lane < ne, logits, neg)
    m1 = jnp.max(logits, axis=-1, keepdims=True)
    i1 = jnp.min(jnp.where(logits == m1, lane, float(LANES)), axis=-1, keepdims=True)
    rest = jnp.where(lane == i1, neg, logits)
    m2 = jnp.max(rest, axis=-1, keepdims=True)
    i2 = jnp.min(jnp.where(rest == m2, lane, float(LANES)), axis=-1, keepdims=True)
    e2 = jnp.exp(m2 - m1)
    w1 = 1.0 / (1.0 + e2)
    w2 = e2 / (1.0 + e2)
    sel = jnp.where((lane == i1) | (lane == i2), 1.0, 0.0)
    ri = lax.broadcasted_iota(jnp.int32, (tm, tm), 0)
    ci = lax.broadcasted_iota(jnp.int32, (tm, tm), 1)
    before = jnp.where(ci < ri, 1.0, 0.0).astype(BF16)
    rank = _dot(before, sel.astype(BF16)) + base_s[0:1, :]
    r1 = jnp.sum(jnp.where(lane == i1, rank, 0.0), axis=-1, keepdims=True)
    r2 = jnp.sum(jnp.where(lane == i2, rank, 0.0), axis=-1, keepdims=True)
    aux = jnp.zeros(logits.shape, F32)
    for k, v in enumerate((i1, i2, r1, r2, w1, w2)):
        aux = jnp.where(lane == float(k), v, aux)
    aux_ref[...] = aux
    base_s[...] = base_s[...] + jnp.sum(sel, axis=0, keepdims=True)
    cnt_ref[...] = base_s[...]


def _router(x, w_pad, ne):
    n, d = x.shape
    return pl.pallas_call(
        functools.partial(_router_kernel, ne=ne),
        grid=(n // TM,),
        in_specs=[pl.BlockSpec((TM, d), lambda i: (i, 0)),
                  pl.BlockSpec((d, LANES), lambda i: (0, 0))],
        out_specs=[pl.BlockSpec((TM, LANES), lambda i: (i, 0)),
                   pl.BlockSpec((8, LANES), lambda i: (0, 0))],
        out_shape=[jax.ShapeDtypeStruct((n, LANES), F32), jax.ShapeDtypeStruct((8, LANES), F32)],
        scratch_shapes=[pltpu.VMEM((8, LANES), F32)],
        compiler_params=_cp(("arbitrary",)),
        name="router",
    )(x, w_pad)


def _moe_plan(aux, cnt, n, ne, tile, max_tiles):
    e = aux[:, 0:2].astype(jnp.int32)
    r = aux[:, 2:4].astype(jnp.int32)
    c = cnt[0, :ne].astype(jnp.int32)
    pc = (c + tile - 1) // tile * tile
    ends = jnp.cumsum(pc)
    slot = ((ends - pc)[e] + r).reshape(-1)
    tok = jnp.broadcast_to(jnp.arange(n, dtype=F32)[:, None], (n, 2)).reshape(-1)
    m = max_tiles * tile
    pad = jnp.broadcast_to(jnp.array([float(n), 0.0], F32), (m, 2))
    table = pad.at[slot].set(jnp.stack([tok, aux[:, 4:6].reshape(-1)], axis=1))
    slot_tok = table[:, 0].astype(jnp.int32)
    slot_gate = table[:, 1]
    tstart = jnp.arange(max_tiles, dtype=jnp.int32) * tile
    tile_e = jnp.minimum(jnp.sum(tstart[:, None] >= ends[None, :], axis=1), ne - 1).astype(jnp.int32)
    tile_valid = (tstart < ends[-1]).astype(jnp.int32)
    return slot_tok.reshape(max_tiles, 1, tile), slot_gate.reshape(max_tiles, 1, tile), tile_e, tile_valid


def _expert_changed(t, te_ref):
    return jnp.logical_or(t == 0, te_ref[t] != te_ref[jnp.maximum(t - 1, 0)])


def _moe_up_kernel(te_ref, tv_ref, idx_ref, nxt_ref, h_ref, wg_ref, wu_ref, o_ref, xbuf, wg_s, wu_s, *, tile):
    t = pl.program_id(0)
    valid = tv_ref[t] == 1
    last = h_ref.shape[0] - 1
    cur = t % 2

    def gather_row(src_idx_ref, buf, j):
        xbuf[buf, pl.ds(j, 1), :] = h_ref[pl.ds(jnp.minimum(src_idx_ref[0, 0, j], last), 1), :]

    @pl.when(t == 0)
    def _():
        def first(j, carry):
            gather_row(idx_ref, 0, j)
            return carry

        lax.fori_loop(0, tile, first, 0, unroll=8)

    @pl.when(jnp.logical_and(valid, _expert_changed(t, te_ref)))
    def _():
        wg_s[...] = wg_ref[...].astype(BF16)
        wu_s[...] = wu_ref[...].astype(BF16)

    @pl.when(valid)
    def _():
        x = xbuf[cur].astype(BF16)
        o_ref[...] = (_silu(_dot(x, wg_s[...])) * _dot(x, wu_s[...])).astype(BF16)
        for j in range(tile):
            gather_row(nxt_ref, 1 - cur, j)

    @pl.when(jnp.logical_not(valid))
    def _():
        o_ref[...] = jnp.zeros(o_ref.shape, BF16)


def _moe_up(h, slot_tok, tile_e, tile_valid, wg, wu, e_layer, tile):
    max_tiles = slot_tok.shape[0]
    _, _, d, f = wg.shape
    wspec = pl.BlockSpec((None, None, d, f), lambda t, te, tv: (e_layer, te[t], 0, 0))
    grid_spec = pltpu.PrefetchScalarGridSpec(
        num_scalar_prefetch=2,
        grid=(max_tiles,),
        in_specs=[pl.BlockSpec((1, 1, tile), lambda t, te, tv: (t, 0, 0), memory_space=pltpu.SMEM),
                  pl.BlockSpec((1, 1, tile), lambda t, te, tv: (jnp.minimum(t + 1, max_tiles - 1), 0, 0),
                               memory_space=pltpu.SMEM),
                  pl.BlockSpec(h.shape, lambda t, te, tv: (0, 0), pipeline_mode=pl.Buffered(1)),
                  wspec, wspec],
        out_specs=pl.BlockSpec((tile, f), lambda t, te, tv: (t, 0)),
        scratch_shapes=[pltpu.VMEM((2, tile, d), F32),
                        pltpu.VMEM((d, f), BF16), pltpu.VMEM((d, f), BF16)],
    )
    return pl.pallas_call(
        functools.partial(_moe_up_kernel, tile=tile),
        grid_spec=grid_spec,
        out_shape=jax.ShapeDtypeStruct((max_tiles * tile, f), BF16),
        compiler_params=_cp(("arbitrary",), vmem=MOE_UP_VMEM_LIMIT),
        name="moe_up",
    )(tile_e, tile_valid, slot_tok, slot_tok, h, wg, wu)


def _moe_down_kernel(te_ref, tv_ref, idx_ref, gt_ref, hid_ref, wd_ref, g_ref, o_ref, acc, ybuf, tbuf, wd_s,
                     *, tile, n_tiles, tm_out):
    t = pl.program_id(0)
    tt = jnp.minimum(t, n_tiles - 1)

    @pl.when(t == 0)
    def _():
        acc[...] = jnp.zeros(acc.shape, F32)

    is_tile = jnp.logical_and(t < n_tiles, tv_ref[tt] == 1)

    @pl.when(jnp.logical_and(is_tile, _expert_changed(tt, te_ref)))
    def _():
        wd_s[...] = wd_ref[...].astype(BF16)

    @pl.when(is_tile)
    def _():
        for j in range(tile):
            tbuf[j:j + 1, :] = acc[pl.ds(idx_ref[0, 0, j], 1), :]
        ybuf[...] = _dot(hid_ref[...], wd_s[...])

        def put(j, carry):
            acc[pl.ds(idx_ref[0, 0, j], 1), :] = tbuf[pl.ds(j, 1), :] + gt_ref[0, 0, j] * ybuf[pl.ds(j, 1), :]
            return carry

        lax.fori_loop(0, tile, put, 0, unroll=8)

    @pl.when(t >= n_tiles)
    def _():
        r0 = pl.multiple_of((t - n_tiles) * tm_out, tm_out)
        o_ref[...] = _rms(acc[pl.ds(r0, tm_out), :], g_ref[...])


def _moe_down(hid, slot_tok, slot_gate, tile_e, tile_valid, wd, e_layer, g, n, tile):
    n_tiles = slot_tok.shape[0]
    _, _, f, d = wd.shape
    tm_out = TM
    n_flush = n // tm_out

    def tix(t):
        return jnp.minimum(t, n_tiles - 1)

    smem_tile = pl.BlockSpec((1, 1, tile), lambda t, te, tv: (tix(t), 0, 0), memory_space=pltpu.SMEM)
    grid_spec = pltpu.PrefetchScalarGridSpec(
        num_scalar_prefetch=2,
        grid=(n_tiles + n_flush,),
        in_specs=[smem_tile, smem_tile,
                  pl.BlockSpec((tile, f), lambda t, te, tv: (tix(t), 0)),
                  pl.BlockSpec((None, None, f, d), lambda t, te, tv: (e_layer, te[tix(t)], 0, 0)),
                  pl.BlockSpec((1, d), lambda t, te, tv: (0, 0))],
        out_specs=pl.BlockSpec((tm_out, d), lambda t, te, tv: (jnp.maximum(t - n_tiles, 0), 0)),
        scratch_shapes=[pltpu.VMEM((n + 8, d), F32), pltpu.VMEM((tile, d), F32), pltpu.VMEM((tile, d), F32),
                        pltpu.VMEM((f, d), BF16)],
    )
    return pl.pallas_call(
        functools.partial(_moe_down_kernel, tile=tile, n_tiles=n_tiles, tm_out=tm_out),
        grid_spec=grid_spec,
        out_shape=jax.ShapeDtypeStruct((n, d), F32),
        compiler_params=_cp(("arbitrary",)),
        name="moe_down",
    )(tile_e, tile_valid, slot_tok, slot_gate, hid, wd, g.reshape(1, d))


def kernel(x_prompt, x_sample, state_hgrn, state_s5, state_gdn, c, c_ctx, ada_w, ada_b, norm_g, w_in, hgrn_lb, hgrn_norm, s5_a_re, s5_a_im, s5_log_dt, s5_b_re, s5_b_im, s5_c_re, s5_c_im, s5_d, s5_glu_w, s5_glu_b, gdn_conv, gdn_a_log, gdn_dt_bias, gdn_norm, w_branch, w_out, ffn_wg, ffn_wu, ffn_wd, moe_router, moe_wg, moe_wu, moe_wd):
    bp, tp, d = x_prompt.shape
    bs, ts, _ = x_sample.shape
    depth = w_in.shape[0]
    n_p, n_s = bp * tp, bs * ts
    ne = moe_router.shape[-1]
    main_cols = 10 * BR_WIDTH
    ab_cols = 4 * N_HEADS

    c8 = jnp.zeros((8, d), F32).at[0].set(c_ctx).at[1:1 + bs].set(c)
    mod_all = _ada_mod(c8, ada_w, ada_b)

    lb_soft = jax.nn.softmax(hgrn_lb.astype(F32), axis=0)
    lb_all = jnp.cumsum(lb_soft, axis=0) - lb_soft[:1]

    mods = mod_all[:, :1 + bs].reshape(depth, 1 + bs, 6, d)
    zero_row = jnp.zeros((1 + bs, d), F32)

    def modv(gate, scale, shift, g):
        rows = [zero_row if sel is None else mods[sel[0], :, sel[1]] for sel in (gate, scale, shift)]
        rows.append(zero_row if g is None else jnp.broadcast_to(g, (1 + bs, d)))
        return jnp.stack(rows + [zero_row] * 4, axis=1)

    x = (x_prompt.reshape(n_p, d), x_sample.reshape(n_s, d))
    (h,) = _resnorm(x, None, modv(None, (0, 1), (0, 0), norm_g[0, 0]), n_p, ts, True)

    z_hgrn = jnp.zeros((bp, 2, N_HEADS, HEAD_D, HEAD_D), F32)
    ngb = S5_GROUPS // S5_GB
    w2 = 2 * S5_GB * S5_STATE
    st_s5 = []
    y_zero = jnp.zeros((n_p + n_s, 3 * BR_WIDTH), BF16)
    new_hgrn = new_gdn = jnp.zeros((bp, depth, 2, N_HEADS, HEAD_D, HEAD_D), F32)
    w_in_t = jnp.swapaxes(w_in, 1, 2)
    s5_tb = 1024

    for l in range(depth):
        col_major = l % 2 == 1
        p_main = _mm_t(h, w_in_t, l, main_cols, 2 * BR_WIDTH, "w_in_main", tm=1024)
        w_ab = jnp.pad(w_in_t[l, main_cols:main_cols + ab_cols], ((0, LANES - ab_cols), (0, 0)))
        p_ab = _mm_t(h, w_ab, None, LANES, LANES, "w_in_ab", tm=1024)
        p_mg = _mm_t(h, w_in_t[l, main_cols + ab_cols:], None, 3 * d, 2 * BR_WIDTH, "w_in_merge", tm=1024,
                     out_dtype=BF16)

        blk = BR_WIDTH // LANES
        br = y_zero

        br, new_hgrn = _hgrn(p_main, lb_all[l], hgrn_norm[l], z_hgrn, 0, bp, tp, br, 0, new_hgrn, l, depth)
        br, _ = _hgrn(p_main, lb_all[l], hgrn_norm[l], state_hgrn[:, l].astype(F32), n_p // ts, bs, ts, br, 0,
                      col_major=col_major)

        bb, cc, lam = _s5_params(s5_a_re[l], s5_a_im[l], s5_log_dt[l], s5_b_re[l], s5_b_im[l],
                                 s5_c_re[l], s5_c_im[l])
        seq_per_blk = 8
        tb_p = seq_per_blk * tp
        nblk_p = n_p // tb_p
        br, sb = _s5(p_main, bb, cc, lam, s5_d[l], jnp.zeros((nblk_p, ngb, 2, 8, w2), F32),
                     0, nblk_p, tb_p, tp, tp, True, br, blk)
        s0s = state_s5[:, l].astype(F32)
        s0s = s0s.reshape(bs, 2, ngb, S5_GB * S5_STATE, 2).transpose(0, 2, 1, 4, 3).reshape(bs, ngb, 2, 1, w2)
        s0s = jnp.pad(s0s, ((0, 0), (0, 0), (0, 0), (0, 7), (0, 0)))
        br, _ = _s5(p_main, bb, cc, lam, s5_d[l], s0s, n_p // s5_tb, n_s // s5_tb, s5_tb, ts, S5_SEG, False, br, blk,
                    col_major=col_major)
        br = _glu(br, s5_glu_w, s5_glu_b, l, 1)
        sb = sb[:, :, :, :seq_per_blk].reshape(nblk_p, ngb, 2, seq_per_blk, 2, S5_GB, S5_STATE)
        sb = sb.transpose(0, 3, 2, 1, 5, 6, 4).reshape(bp, 2, S5_GROUPS, S5_STATE, 2)
        st_s5.append(sb)

        br, new_gdn = _gdn(p_main, p_ab, gdn_conv[l], gdn_a_log[l], gdn_dt_bias[l], gdn_norm[l], z_hgrn, 0, bp, tp,
                           br, 2 * blk, new_gdn, l, depth)
        br, _ = _gdn(p_main, p_ab, gdn_conv[l], gdn_a_log[l], gdn_dt_bias[l], gdn_norm[l],
                     state_gdn[:, l].astype(F32), n_p // ts, bs, ts, br, 2 * blk, col_major=col_major)

        x, h2 = _merge(br, p_mg, w_branch, w_out, norm_g[l, 1], l, x,
                       modv((l, 2), (l, 4), (l, 3), norm_g[l, 2]), n_p, ts, BF16 if l % 2 == 0 else F32)
        last = l + 1 == depth
        next_mod = modv((l, 5), None, None, None) if last else modv((l, 5), (l + 1, 1), (l + 1, 0), norm_g[l + 1, 0])

        if l % 2 == 0:
            e = l // 2
            hid = _swiglu_up(h2, ffn_wg, ffn_wu, e, ffn_wg.shape[-1] // 2)
            x, h = _down_res(hid, ffn_wd, e, norm_g[l, 3], x, next_mod, n_p, ts, BF16)
            if last:
                y_prompt, y_sample = x[:n_p], x[n_p:]
            continue
        else:
            e = l // 2
            n_tok = n_p + n_s
            max_tiles = 2 * n_tok // MOE_TILE + ne
            aux, cnt = _router(h2, jnp.pad(moe_router[e], ((0, 0), (0, LANES - ne))), ne)
            slot_tok, slot_gate, tile_e, tile_valid = _moe_plan(aux, cnt, n_tok, ne, MOE_TILE, max_tiles)
            hid = _moe_up(h2, slot_tok, tile_e, tile_valid, moe_wg, moe_wu, e, MOE_TILE)
            r_f = _moe_down(hid, slot_tok, slot_gate, tile_e, tile_valid, moe_wd, e, norm_g[l, 3], n_tok, MOE_TILE)

        if not last:
            x, h = _resnorm(x, r_f, next_mod, n_p, ts, True)
        else:
            (y_prompt,) = _resnorm(x, r_f, next_mod, n_p, ts, False, rows=(0, n_p))
            (y_sample,) = _resnorm(x, r_f, next_mod, n_p, ts, False, rows=(n_p, n_s))

    y_prompt = y_prompt.reshape(bp, tp, d)
    y_sample = y_sample.reshape(bs, ts, d)
    new_s5 = jnp.stack(st_s5, axis=1).astype(x_prompt.dtype)
    return (y_prompt, y_sample, new_hgrn.astype(x_prompt.dtype), new_s5, new_gdn.astype(x_prompt.dtype))
```

```python
import functools
import math

import jax
import jax.numpy as jnp
from jax import lax
from jax.experimental import pallas as pl
from jax.experimental.pallas import tpu as pltpu

F32 = jnp.float32
BF16 = jnp.bfloat16

EPS = 1e-6
F_FLOOR = 1e-6
LOG2E = 1.4426950408889634
GRID_W = 64
HEAD_D = 128
N_HEADS = 4
BR_WIDTH = 512
S5_GROUPS = 32
S5_GROUP = 16
S5_STATE = 64
LANES = 128
V7X_VMEM_BYTES = 64 * 1024 * 1024
VMEM_LIMIT = V7X_VMEM_BYTES - 8 * 1024 * 1024
MOE_UP_VMEM_LIMIT = V7X_VMEM_BYTES - 3 * 1024 * 1024
S5_GB = 8
S5_SEG = 64
GLA_CHUNK = 256
GDN_CHUNK = 64
GDN_SPAN = 256
GDN_HEADS_PER_STEP = 4
GDN_MAX_ROWS_PER_STEP = 2048
TM = 512
MOE_TILE = 256


def _cp(sem, vmem=VMEM_LIMIT):
    return pltpu.CompilerParams(dimension_semantics=sem, vmem_limit_bytes=vmem)


def _dot(a, b):
    return jnp.dot(a, b, preferred_element_type=F32)


def _dot_nt(a, b):
    return lax.dot_general(a, b, (((1,), (1,)), ((), ())), preferred_element_type=F32)


def _dot_tn(a, b):
    return lax.dot_general(a, b, (((0,), (0,)), ((), ())), preferred_element_type=F32)


def _silu(x):
    return x * jax.nn.sigmoid(x)


def _rms(x, g):
    return x * lax.rsqrt(jnp.mean(x * x, axis=-1, keepdims=True) + EPS) * g


def _chunk_cumsum(x, d, c):
    n = x.shape[0]
    pos = lax.broadcasted_iota(jnp.int32, x.shape, 0) & (c - 1)
    s = 1
    while s < c:
        if d == 0:
            x = x + jnp.where(pos >= s, pltpu.roll(x, s, 0), 0.0)
        else:
            x = x + jnp.where(pos < c - s, pltpu.roll(x, n - s, 0), 0.0)
        s *= 2
    return x


def _ada_kernel(c_ref, w_ref, b_ref, o_ref):
    c = c_ref[...]
    a = _silu(c).astype(BF16)
    o_ref[...] = _dot(a, w_ref[...].astype(BF16)) + b_ref[...]


def _ada_mod(c8, ada_w, ada_b):
    depth, d, n6 = ada_w.shape
    tn = 1536
    return pl.pallas_call(
        _ada_kernel,
        grid=(depth, n6 // tn),
        in_specs=[
            pl.BlockSpec((8, d), lambda l, j: (0, 0)),
            pl.BlockSpec((None, d, tn), lambda l, j: (l, 0, j)),
            pl.BlockSpec((None, 1, tn), lambda l, j: (l, 0, j)),
        ],
        out_specs=pl.BlockSpec((None, 8, tn), lambda l, j: (l, 0, j)),
        out_shape=jax.ShapeDtypeStruct((depth, 8, n6), F32),
        compiler_params=_cp(("parallel", "parallel")),
        name="ada_mod",
    )(c8, ada_w, ada_b.reshape(depth, 1, n6))


def _two_source_specs(tm, d, na_tiles):
    return [pl.BlockSpec((tm, d), lambda i: (jnp.minimum(i, na_tiles - 1), 0)),
            pl.BlockSpec((tm, d), lambda i: (jnp.maximum(i - na_tiles, 0), 0))]


def _pick_source(xa_ref, xb_ref, na_tiles):
    return jnp.where(pl.program_id(0) < na_tiles, xa_ref[...], xb_ref[...])


def _resnorm_kernel(*refs, has_y, has_h, na_tiles):
    it = iter(refs)
    x_ref = next(it)
    xb_ref = next(it) if na_tiles else None
    y_ref = next(it) if has_y else None
    mv_ref = next(it)
    xo_ref = next(it) if has_y else None
    h_ref = next(it) if has_h else None
    x = _pick_source(x_ref, xb_ref, na_tiles) if na_tiles else x_ref[...]
    mv = mv_ref[...]
    if has_y:
        x = x + mv[0:1] * y_ref[...]
        xo_ref[...] = x
    if has_h:
        h = _rms(x, mv[3:4]) * (1.0 + mv[1:2]) + mv[2:3]
        h_ref[...] = h.astype(h_ref.dtype)


def _resnorm(x, y, modv, n_prompt, seq_s, has_h, rows=None, h_dtype=BF16):
    xs = x if isinstance(x, tuple) else (x,)
    d = xs[0].shape[1]
    row0, n = (0, sum(a.shape[0] for a in xs)) if rows is None else rows
    i0 = row0 // TM
    has_y = y is not None
    na_tiles = xs[0].shape[0] // TM if len(xs) == 2 else 0

    def group(i):
        r = (i + i0) * TM
        return jnp.where(r < n_prompt, 0, 1 + (r - n_prompt) // seq_s)

    tile = pl.BlockSpec((TM, d), lambda i: (i + i0, 0))
    x_specs = _two_source_specs(TM, d, na_tiles) if na_tiles else [tile]
    in_specs = x_specs + ([tile] if has_y else []) + [pl.BlockSpec((None, 8, d), lambda i: (group(i), 0, 0))]
    out_tile = pl.BlockSpec((TM, d), lambda i: (i, 0))
    out_specs, out_shape = [], []
    if has_y:
        out_specs.append(out_tile)
        out_shape.append(jax.ShapeDtypeStruct((n, d), F32))
    if has_h:
        out_specs.append(out_tile)
        out_shape.append(jax.ShapeDtypeStruct((n, d), h_dtype))
    args = list(xs) + ([y] if has_y else []) + [modv]
    outs = pl.pallas_call(
        functools.partial(_resnorm_kernel, has_y=has_y, has_h=has_h, na_tiles=na_tiles),
        grid=(n // TM,),
        in_specs=in_specs,
        out_specs=out_specs,
        out_shape=out_shape,
        compiler_params=_cp(("parallel",)),
        name="resnorm",
    )(*args)
    return outs


def _mm_kernel(x_ref, w_ref, o_ref, wb_ref):
    @pl.when(pl.program_id(1) == 0)
    def _():
        wb_ref[...] = w_ref[...].astype(BF16)

    o_ref[...] = _dot_nt(x_ref[...], wb_ref[...]).astype(o_ref.dtype)


def _mm_t(x, wt, l, nrows, tn, name, tm=TM, out_dtype=F32):
    m, k = x.shape
    if l is None:
        wspec = pl.BlockSpec((tn, k), lambda j, i: (j, 0))
    else:
        wspec = pl.BlockSpec((None, tn, k), lambda j, i: (l, j, 0))
    return pl.pallas_call(
        _mm_kernel,
        grid=(nrows // tn, m // tm),
        in_specs=[pl.BlockSpec((tm, k), lambda j, i: (i, 0)), wspec],
        out_specs=pl.BlockSpec((tm, tn), lambda j, i: (i, j)),
        out_shape=jax.ShapeDtypeStruct((m, nrows), out_dtype),
        scratch_shapes=[pltpu.VMEM((tn, k), BF16)],
        compiler_params=_cp(("arbitrary", "arbitrary")),
        name=name,
    )(x, wt)


def _gla_intra(q, k, b, d, c):
    rowi = lax.broadcasted_iota(jnp.int32, (c, LANES), 0)
    ri = lax.broadcasted_iota(jnp.int32, (c, c), 0)
    ci = lax.broadcasted_iota(jnp.int32, (c, c), 1)
    a = jnp.zeros((c, c), F32)
    size = c
    while size >= 8:
        half = size // 2
        nb = c // size
        b3 = b.reshape(nb, size, LANES)
        pos = rowi & (size - 1)
        if d == 0:
            ref = b3[:, half - 1:half, :]
            qsel = pos >= half
        else:
            ref = b3[:, half:half + 1, :]
            qsel = pos < half
        ref = jnp.broadcast_to(ref, (nb, size, LANES)).reshape(c, LANES)
        z = jnp.where(qsel, q, k) * jnp.exp2(-jnp.abs(b - ref))
        qs = jnp.where(qsel, z, 0.0).astype(BF16)
        ks = jnp.where(qsel, 0.0, z).astype(BF16)
        p = _dot_nt(qs, ks)
        if nb > 1:
            shift = int(math.log2(size))
            p = jnp.where((ri >> shift) == (ci >> shift), p, 0.0)
        a = a + p
        size = half
    b3 = b.reshape(c // 8, 8, LANES)
    lo, hi = (0, 4) if d == 0 else (3, 7)
    ref_lo = jnp.broadcast_to(b3[:, lo:lo + 1, :], (c // 8, 8, LANES)).reshape(c, LANES)
    ref_hi = jnp.broadcast_to(b3[:, hi:hi + 1, :], (c // 8, 8, LANES)).reshape(c, LANES)
    ref = jnp.where((rowi & 7) >= 4, ref_hi, ref_lo)
    qs = (q * jnp.exp2(jnp.minimum(b - ref, 0.0))).astype(BF16)
    ks = (k * jnp.exp2(ref - b)).astype(BF16)
    p = _dot_nt(qs, ks)
    keep = ((ri >> 2) == (ci >> 2)) & ((ci <= ri) if d == 0 else (ci >= ri))
    return a + jnp.where(keep, p, 0.0)


def _load_column_major(dst_ref, src_ref, t, stage=None):
    rows = t // GRID_W
    for k in range(src_ref.shape[1] // LANES):
        lanes = slice(k * LANES, (k + 1) * LANES)
        if stage is None:
            strided = src_ref
        else:
            stage[...] = src_ref[:, lanes]
            strided = stage
        for col in range(GRID_W):
            dst_ref[col * rows:(col + 1) * rows, lanes] = strided[pl.ds(col, rows, stride=GRID_W), :]


def _store_row_major(dst_ref, y, t, stage=None):
    rows = t // GRID_W
    for k in range(y.shape[1] // LANES):
        lanes = slice(k * LANES, (k + 1) * LANES)
        strided = dst_ref if stage is None else stage
        for col in range(GRID_W):
            strided[pl.ds(col, rows, stride=GRID_W), :] = y[col * rows:(col + 1) * rows, lanes]
        if stage is not None:
            dst_ref[:, lanes] = stage[...].astype(dst_ref.dtype)


def _hgrn_kernel(aq_ref, af_ref, ab_ref, ai_ref, ag_ref, lb_ref, g_ref, s0_ref, y_ref, s_ref, o_acc, *perm_bufs,
                 t, c):
    stage = None
    if perm_bufs:
        *perm_bufs, stage = perm_bufs
        srcs = (aq_ref, af_ref, ab_ref, ai_ref, ag_ref)
        for dst, src in zip(perm_bufs, srcs):
            _load_column_major(dst, src, t)
        aq_ref, af_ref, ab_ref, ai_ref, ag_ref = perm_bufs
    nchunk = t // c
    scale = HEAD_D ** -0.5
    for d in (0, 1):
        z_ref = af_ref if d == 0 else ab_ref
        lbd = lb_ref[d:d + 1, :]

        def chunk(i, st, d=d, z_ref=z_ref, lbd=lbd):
            cidx = i if d == 0 else nchunk - 1 - i
            r0 = pl.multiple_of(cidx * c, c)
            z = z_ref[pl.ds(r0, c), :]
            f = lbd + (1.0 - lbd) * jax.nn.sigmoid(z)
            lf = jnp.log2(jnp.maximum(f, F_FLOOR))
            k = (1.0 - lbd) * jax.nn.sigmoid(-z)
            q = _silu(aq_ref[pl.ds(r0, c), :]) * scale
            v = ai_ref[pl.ds(r0, c), :].astype(BF16)
            b = _chunk_cumsum(lf, d, c)
            a = _gla_intra(q, k, b, d, c)
            o = _dot(a.astype(BF16), v) + _dot_nt((q * jnp.exp2(b)).astype(BF16), st.astype(BF16))
            if d == 0:
                o_acc[pl.ds(r0, c), :] = o
            else:
                o_acc[pl.ds(r0, c), :] += o
            blast = b[c - 1:c, :] if d == 0 else b[0:1, :]
            kd = (k * jnp.exp2(blast - b)).astype(BF16)
            return st * jnp.exp2(blast) + _dot_tn(v, kd)

        st = s0_ref[0, d, 0].T
        if nchunk == 1:
            st = chunk(0, st)
        else:
            st = lax.fori_loop(0, nchunk, chunk, st)
        s_ref[0, d, 0] = st.T
    y = _rms(o_acc[...], g_ref[...]) * _silu(ag_ref[...])
    if perm_bufs:
        _store_row_major(y_ref, y, t, stage)
    else:
        y_ref[...] = y.astype(y_ref.dtype)


def _shared_out_call(kernel_fn, prevs, in_specs, args, **kw):
    extra = [p for p in prevs if p is not None]
    if not extra:
        return pl.pallas_call(kernel_fn, in_specs=in_specs, **kw)(*args)
    n_in = len(in_specs)
    aliases, k = {}, 0
    for out_idx, p in enumerate(prevs):
        if p is not None:
            aliases[n_in + k] = out_idx
            k += 1

    def body(*refs):
        kernel_fn(*refs[:n_in], *refs[n_in + len(extra):])

    return pl.pallas_call(body, in_specs=in_specs + [pl.BlockSpec(memory_space=pl.ANY)] * len(extra),
                          input_output_aliases=aliases, **kw)(*args, *extra)


def _state_out(nseq, slot, nslots, hg):
    spec = pl.BlockSpec((1, None, 2, hg, HEAD_D, HEAD_D), lambda s, h: (s, slot, 0, h, 0, 0))
    return spec, jax.ShapeDtypeStruct((nseq, nslots, 2, N_HEADS, HEAD_D, HEAD_D), F32)


def _hgrn(p_main, lb, norm_g, s0, row_blk0, nseq, t, y_prev, col0, s_prev=None, slot=0, nslots=1,
          col_major=False):
    c = min(GLA_CHUNK, t)
    seq_buf = pltpu.VMEM((t, HEAD_D), F32)

    def col(base):
        return pl.BlockSpec((t, HEAD_D), lambda s, h: (row_blk0 + s, base + h))

    st_spec = pl.BlockSpec((1, 2, 1, HEAD_D, HEAD_D), lambda s, h: (s, 0, h, 0, 0))
    so_spec, so_shape = _state_out(nseq, slot, nslots, 1)
    return _shared_out_call(
        functools.partial(_hgrn_kernel, t=t, c=c), [y_prev, s_prev],
        [col(0), col(4), col(8), col(12), col(16),
         pl.BlockSpec((2, HEAD_D), lambda s, h: (0, h)),
         pl.BlockSpec((1, HEAD_D), lambda s, h: (0, 0)),
         st_spec],
        (p_main, p_main, p_main, p_main, p_main, lb, norm_g.reshape(1, HEAD_D), s0),
        grid=(nseq, N_HEADS),
        out_specs=[pl.BlockSpec((t, HEAD_D), lambda s, h: (row_blk0 + s, col0 + h)), so_spec],
        out_shape=[jax.ShapeDtypeStruct(y_prev.shape, y_prev.dtype), so_shape],
        scratch_shapes=[seq_buf] * (7 if col_major else 1),
        compiler_params=_cp(("parallel", "parallel")),
        name="hgrn2",
    )


def _dot_b(a, b):
    return _dot(a.astype(BF16), b.astype(BF16))


def _tri_solve(ms, xs, c):
    ri = lax.broadcasted_iota(jnp.int32, (c, c), 0)
    ci = lax.broadcasted_iota(jnp.int32, (c, c), 1)
    eye = jnp.where(ri == ci, 1.0, 0.0)
    m8 = [jnp.where((ri >> 3) == (ci >> 3), m, 0.0) for m in ms]
    ts = [eye - m for m in m8]
    ps = [_dot_b(m, m) for m in m8]
    ts = [t + _dot_b(t, p) for t, p in zip(ts, ps)]
    ps = [_dot_b(p, p) for p in ps]
    ts = [t + _dot_b(t, p) for t, p in zip(ts, ps)]
    shift = 3
    while (1 << shift) < c:
        join = ((ri >> (shift + 1)) == (ci >> (shift + 1))) & ((ri >> shift) != (ci >> shift))
        lts = [_dot_b(jnp.where(join, m, 0.0), t) for m, t in zip(ms, ts)]
        ts = [t - _dot_b(t, lt) for t, lt in zip(ts, lts)]
        shift += 1
    return [_dot_b(t, x) for t, x in zip(ts, xs)]


def _gdn_kernel(cq_ref, ck_ref, cv_ref, cg_ref, ab_ref, cw_ref, alog_ref, dtb_ref, g_ref, s0_ref,
                y_ref, s_ref, q_s, k_s, v_s, la_s, be_s, w_s, qk_s, qe_s, kd_s, dec_s, o_acc, *perm_bufs,
                t, c, span, hg):
    stage = None
    if perm_bufs:
        *perm_bufs, stage = perm_bufs
        srcs = (cq_ref, ck_ref, cv_ref, cg_ref, ab_ref)
        for dst, src in zip(perm_bufs, srcs):
            _load_column_major(dst, src, t, stage)
        cq_ref, ck_ref, cv_ref, cg_ref, ab_ref = perm_bufs
    nchunk = t // c
    cps = span // c
    rowt = lax.broadcasted_iota(jnp.int32, (t, HEAD_D), 0)
    lane = lax.broadcasted_iota(jnp.int32, (t, LANES), 1)
    ri = lax.broadcasted_iota(jnp.int32, (span, span), 0)
    ci = lax.broadcasted_iota(jnp.int32, (span, span), 1)
    shift = int(math.log2(c))
    same = (ri >> shift) == (ci >> shift)
    heads = range(hg)
    dirs = (0, 1)

    def conv(x_ref, j, hh):
        x = x_ref[:, hh * HEAD_D:(hh + 1) * HEAD_D]
        w = cw_ref[j][:, hh * HEAD_D:(hh + 1) * HEAD_D]
        xm = jnp.where(rowt == 0, 0.0, pltpu.roll(x, 1, 0))
        xp = jnp.where(rowt == t - 1, 0.0, pltpu.roll(x, t - 1, 0))
        return xm * w[0:1] + x * w[1:2] + xp * w[2:3]

    ab = ab_ref[...]
    lane1 = lax.broadcasted_iota(jnp.int32, (1, LANES), 1)
    alog_l = jnp.zeros((1, LANES), F32)
    dtb_l = jnp.zeros((1, LANES), F32)
    for d in dirs:
        for hd in range(N_HEADS):
            alog_l = jnp.where(lane1 == d * N_HEADS + hd, alog_ref[d, hd], alog_l)
            dtb_l = jnp.where(lane1 == d * N_HEADS + hd, dtb_ref[d, hd], dtb_l)
    xs = ab + dtb_l
    log_a_all = (-LOG2E) * jnp.exp(alog_l) * (jnp.maximum(xs, 0.0) + jnp.log1p(jnp.exp(-jnp.abs(xs))))
    beta_all = jax.nn.sigmoid(ab)
    for hh in heads:
        h = pl.program_id(1) * hg + hh
        q = _silu(conv(cq_ref, 0, hh))
        q_s[hh] = q * lax.rsqrt(jnp.sum(q * q, axis=-1, keepdims=True) + EPS) * (HEAD_D ** -0.5)
        k = _silu(conv(ck_ref, 1, hh))
        k_s[hh] = k * lax.rsqrt(jnp.sum(k * k, axis=-1, keepdims=True) + EPS)
        v_s[hh] = _silu(conv(cv_ref, 2, hh))
        for d in dirs:
            la_col = jnp.sum(jnp.where(lane == d * N_HEADS + h, log_a_all, 0.0), axis=-1, keepdims=True)
            be_col = jnp.sum(jnp.where(lane == 2 * N_HEADS + d * N_HEADS + h, beta_all, 0.0), axis=-1,
                             keepdims=True)
            la_s[hh, d] = jnp.broadcast_to(la_col, (t, LANES))
            be_s[hh, d] = jnp.broadcast_to(be_col, (t, LANES))
    o_acc[...] = jnp.zeros(o_acc.shape, F32)

    def wy_span(sp, carry):
        r0 = pl.multiple_of(sp * span, span)
        ms, xs, dst = [], [], []
        for hh in heads:
            qsp = q_s[hh, pl.ds(r0, span), :]
            ksp = k_s[hh, pl.ds(r0, span), :]
            vsp = v_s[hh, pl.ds(r0, span), :]
            kb = ksp.astype(BF16)
            kk = _dot_nt(kb, kb)
            qk = _dot_nt(qsp.astype(BF16), kb)
            for d in dirs:
                be = be_s[hh, d, pl.ds(r0, span), :]
                g = _chunk_cumsum(la_s[hh, d, pl.ds(r0, span), :], d, c)
                gt = g.T
                gcol = jnp.concatenate([g] * (span // LANES), axis=1)
                grow = jnp.concatenate([gt] * (span // LANES), axis=0)
                incl = same & ((ci <= ri) if d == 0 else (ci >= ri))
                gamma = jnp.where(incl, jnp.exp2(jnp.minimum(gcol - grow, 0.0)), 0.0)
                bcol = jnp.concatenate([be] * (span // LANES), axis=1)
                m_sp = jnp.where(ci == ri, 0.0, bcol * kk * gamma)
                qkg = (qk * gamma).astype(BF16)
                eg = jnp.exp2(g)
                xv = be * vsp
                xk = be * eg * ksp
                qe_s[hh, d, pl.ds(r0, span), :] = (qsp * eg).astype(BF16)
                for j in range(cps):
                    lo = j * c
                    rj = pl.multiple_of(r0 + lo, c)
                    ms.append(m_sp[lo:lo + c, lo:lo + c])
                    xs.append(jnp.concatenate([xv[lo:lo + c], xk[lo:lo + c]], axis=1))
                    dst.append((hh, d, rj))
                    qk_s[hh, d, pl.ds(rj, c), 0:c] = qkg[lo:lo + c, lo:lo + c]
                    gj = g[lo:lo + c]
                    glast = gj[c - 1:c] if d == 0 else gj[0:1]
                    kd_s[hh, d, pl.ds(rj, c), :] = (ksp[lo:lo + c] * jnp.exp2(glast - gj)).astype(BF16)
                    dec_s[hh, d, pl.ds(sp * cps + j, 1), :] = jnp.exp2(glast)
        for (hh, d, rj), w in zip(dst, _tri_solve(ms, xs, c)):
            w_s[hh, d, pl.ds(rj, c), :] = w
        return carry

    if t == span:
        wy_span(0, 0)
    else:
        lax.fori_loop(0, t // span, wy_span, 0)

    chains = [(hh, d) for hh in heads for d in dirs]

    def scan_step(i, carry):
        cidx = (i, nchunk - 1 - i)
        r0 = [pl.multiple_of(ck * c, c) for ck in cidx]
        sb = [s.astype(BF16) for s in carry]
        ws = [w_s[hh, d, pl.ds(r0[d], c), :] for hh, d in chains]
        wks = [_dot(w[:, HEAD_D:].astype(BF16), s) for w, s in zip(ws, sb)]
        oq = [_dot(qe_s[hh, d, pl.ds(r0[d], c), :], s) for (hh, d), s in zip(chains, sb)]
        vnb = [(w[:, :HEAD_D] - wk).astype(BF16) for w, wk in zip(ws, wks)]
        new = [s * dec_s[hh, d, pl.ds(cidx[d], 1), :] + _dot_tn(kd_s[hh, d, pl.ds(r0[d], c), :], v)
               for (hh, d), s, v in zip(chains, carry, vnb)]
        for (hh, d), o, v in zip(chains, oq, vnb):
            o_acc[hh, pl.ds(r0[d], c), :] += o + _dot(qk_s[hh, d, pl.ds(r0[d], c), 0:c], v)
        return tuple(new)

    finals = lax.fori_loop(0, nchunk, scan_step, tuple(s0_ref[0, d, hh] for hh, d in chains),
                           unroll=nchunk <= 4)
    for (hh, d), s in zip(chains, finals):
        s_ref[0, d, hh] = s
    ys = []
    for hh in heads:
        sl = slice(hh * HEAD_D, (hh + 1) * HEAD_D)
        ys.append(_rms(o_acc[hh], g_ref[...]) * _silu(cg_ref[:, sl]))
    y = jnp.concatenate(ys, axis=1)
    if perm_bufs:
        _store_row_major(y_ref, y, t, stage)
    else:
        y_ref[...] = y.astype(y_ref.dtype)


def _gdn(p_main, p_ab, conv_w, a_log, dt_bias, norm_g, s0, row_blk0, nseq, t, y_prev, col0, s_prev=None,
         slot=0, nslots=1, col_major=False):
    c = GDN_CHUNK
    hg = GDN_HEADS_PER_STEP if t * GDN_HEADS_PER_STEP <= GDN_MAX_ROWS_PER_STEP else GDN_HEADS_PER_STEP // 2
    wid = hg * HEAD_D
    blk = BR_WIDTH // wid

    def col(base):
        return pl.BlockSpec((t, wid), lambda s, h: (row_blk0 + s, base * blk + h))

    st_spec = pl.BlockSpec((1, 2, hg, HEAD_D, HEAD_D), lambda s, h: (s, 0, h, 0, 0))
    smem = pl.BlockSpec(memory_space=pltpu.SMEM)
    seq_buf = pltpu.VMEM((hg, t, HEAD_D), F32)
    dir_f32 = pltpu.VMEM((hg, 2, t, LANES), F32)
    dir_bf16 = pltpu.VMEM((hg, 2, t, LANES), BF16)
    scratch = [seq_buf, seq_buf, seq_buf, dir_f32, dir_f32,
               pltpu.VMEM((hg, 2, t, 2 * HEAD_D), F32), dir_bf16, dir_bf16, dir_bf16,
               pltpu.VMEM((hg, 2, max(t // c, 8), LANES), F32), seq_buf]
    if col_major:
        scratch += [pltpu.VMEM((t, wid), F32)] * 4 + [pltpu.VMEM((t, LANES), F32)] * 2
    so_spec, so_shape = _state_out(nseq, slot, nslots, hg)
    return _shared_out_call(
        functools.partial(_gdn_kernel, t=t, c=c, span=GDN_SPAN, hg=hg), [y_prev, s_prev],
        [col(6), col(7), col(8), col(9),
         pl.BlockSpec((t, LANES), lambda s, h: (row_blk0 + s, 0)),
         pl.BlockSpec((3, 3, wid), lambda s, h: (0, 0, h)),
         smem, smem,
         pl.BlockSpec((1, HEAD_D), lambda s, h: (0, 0)),
         st_spec],
        (p_main, p_main, p_main, p_main, p_ab, conv_w, a_log, dt_bias, norm_g.reshape(1, HEAD_D), s0),
        grid=(nseq, N_HEADS // hg),
        out_specs=[pl.BlockSpec((t, wid), lambda s, h: (row_blk0 + s, col0 * LANES // wid + h)), so_spec],
        out_shape=[jax.ShapeDtypeStruct(y_prev.shape, y_prev.dtype), so_shape],
        scratch_shapes=scratch,
        compiler_params=_cp(("parallel", "parallel")),
        name="gdn",
    )


def _cmul(ar, ai, br, bi):
    return ar * br - ai * bi, ar * bi + ai * br


def _s5_kernel(u_ref, bb_ref, cc_ref, lam_ref, dsk_ref, s0_ref, y_ref, sf_ref, up_s, x_s, yp_s,
               *, tb, seg, segs_per_seq, zero_init, col_major):
    r_par = tb // seg
    half = S5_GB * S5_STATE
    if col_major:
        rows = tb // GRID_W
        assert segs_per_seq == r_par and seg % rows == 0

        def step_rows(r):
            return pl.ds((r % rows) * GRID_W + r // rows, r_par, stride=seg // rows)
    else:
        def step_rows(r):
            return pl.ds(r, r_par, stride=seg)

    for r in range(seg):
        up_s[r * r_par:(r + 1) * r_par, :] = u_ref[step_rows(r), :]
    ub = up_s[...].astype(BF16)
    sf_ref[...] = jnp.zeros(sf_ref.shape, F32)
    nseq = r_par // segs_per_seq
    lam = [(lam_ref[0, d, 0:1, :], lam_ref[0, d, 1:2, :]) for d in (0, 1)]
    for d in (0, 1):
        x_s[d] = _dot(ub, bb_ref[0, d])

    def offsets(i):
        return [pl.multiple_of(r * r_par, r_par) for r in (i, seg - 1 - i)]

    def local_step(i, carry):
        off = offsets(i)
        new = []
        for d in (0, 1):
            (lre, lim), (hre, him) = lam[d], carry[d]
            nre = lre * hre - lim * him + x_s[d, pl.ds(off[d], r_par), 0:half]
            nim = lre * him + lim * hre + x_s[d, pl.ds(off[d], r_par), half:2 * half]
            x_s[d, pl.ds(off[d], r_par), 0:half] = nre
            x_s[d, pl.ds(off[d], r_par), half:2 * half] = nim
            new.append((nre, nim))
        return tuple(new)

    zero = jnp.zeros((r_par, half), F32)
    lax.fori_loop(0, seg, local_step, ((zero, zero), (zero, zero)), unroll=4)

    end_rows = ((seg - 1) * r_par, 0)
    needs_carry = not (zero_init and segs_per_seq == 1)
    hins = []
    for d in (0, 1):
        if not needs_carry:
            break
        lre, lim = lam[d]
        pre, pim = lre, lim
        n = 1
        while n < seg:
            pre, pim = _cmul(pre, pim, pre, pim)
            n *= 2
        end_row = end_rows[d]
        ere = x_s[d, end_row:end_row + r_par, 0:half]
        eim = x_s[d, end_row:end_row + r_par, half:2 * half]
        hin = [None] * r_par
        order = range(r_par) if d == 0 else range(r_par - 1, -1, -1)
        for j in order:
            sq, pos = divmod(j, segs_per_seq)
            first = pos == (0 if d == 0 else segs_per_seq - 1)
            if first:
                hin[j] = (s0_ref[0, 0, d, sq:sq + 1, 0:half], s0_ref[0, 0, d, sq:sq + 1, half:2 * half])
            else:
                pj = j - 1 if d == 0 else j + 1
                cre, cim = _cmul(pre, pim, hin[pj][0], hin[pj][1])
                hin[j] = (cre + ere[pj:pj + 1], cim + eim[pj:pj + 1])
        hins.append((jnp.concatenate([hj[0] for hj in hin], axis=0),
                     jnp.concatenate([hj[1] for hj in hin], axis=0)))

    def carry_step(i, pws):
        off = offsets(i)
        new = []
        for d in (0, 1):
            (lre, lim), (wre, wim) = lam[d], pws[d]
            are, aim = _cmul(wre, wim, hins[d][0], hins[d][1])
            x_s[d, pl.ds(off[d], r_par), 0:half] += are
            x_s[d, pl.ds(off[d], r_par), half:2 * half] += aim
            new.append(_cmul(wre, wim, lre, lim))
        return tuple(new)

    if needs_carry:
        lax.fori_loop(0, seg, carry_step, (lam[0], lam[1]), unroll=4)

    for d in (0, 1):
        for sq in range(nseq):
            j = (sq + 1) * segs_per_seq - 1 if d == 0 else sq * segs_per_seq
            row = end_rows[d] + j
            sf_ref[0, 0, d, sq:sq + 1, :] = x_s[d, row:row + 1, :]
    yp_s[...] = (_dot(x_s[0].astype(BF16), cc_ref[0, 0]) + _dot(x_s[1].astype(BF16), cc_ref[0, 1])
                 + up_s[...] * dsk_ref[...])
    for r in range(seg):
        up_s[step_rows(r), :] = yp_s[r * r_par:(r + 1) * r_par, :]
    y_ref[...] = up_s[...].astype(y_ref.dtype)


def _s5(p_main, bb, cc, lam, dskip, s0, row_blk0, nblk, tb, t, seg, zero_init, y_prev, col0, col_major=False):
    ngb = S5_GROUPS // S5_GB
    w2 = 2 * S5_GB * S5_STATE
    return _shared_out_call(
        functools.partial(_s5_kernel, tb=tb, seg=seg, segs_per_seq=t // seg, zero_init=zero_init,
                          col_major=col_major), [y_prev, None],
        [
            pl.BlockSpec((tb, LANES), lambda i, gb: (row_blk0 + i, 20 + gb)),
            pl.BlockSpec((1, 2, LANES, w2), lambda i, gb: (gb, 0, 0, 0)),
            pl.BlockSpec((1, 2, w2, LANES), lambda i, gb: (gb, 0, 0, 0)),
            pl.BlockSpec((1, 2, 2, w2 // 2), lambda i, gb: (gb, 0, 0, 0)),
            pl.BlockSpec((1, LANES), lambda i, gb: (0, gb)),
            pl.BlockSpec((1, 1, 2, 8, w2), lambda i, gb: (i, gb, 0, 0, 0)),
        ],
        (p_main, bb, cc, lam, dskip.reshape(1, BR_WIDTH), s0),
        grid=(nblk, ngb),
        out_specs=[pl.BlockSpec((tb, LANES), lambda i, gb: (row_blk0 + i, col0 + gb)),
                   pl.BlockSpec((1, 1, 2, 8, w2), lambda i, gb: (i, gb, 0, 0, 0))],
        out_shape=[jax.ShapeDtypeStruct(y_prev.shape, y_prev.dtype),
                   jax.ShapeDtypeStruct((nblk, ngb, 2, 8, w2), F32)],
        scratch_shapes=[pltpu.VMEM((tb, LANES), F32), pltpu.VMEM((2, tb, w2), F32), pltpu.VMEM((tb, LANES), F32)],
        compiler_params=_cp(("parallel", "parallel")),
        name="s5",
    )


def _s5_params(a_re, a_im, log_dt, b_re, b_im, c_re, c_im):
    dt = jnp.exp(log_dt)[..., None]
    mag = jnp.exp(dt * a_re)
    lam_re = mag * jnp.cos(dt * a_im)
    lam_im = mag * jnp.sin(dt * a_im)
    den = a_re * a_re + a_im * a_im
    co_re = ((lam_re - 1.0) * a_re + lam_im * a_im) / den
    co_im = (lam_im * a_re - (lam_re - 1.0) * a_im) / den
    bb_re = co_re[..., None] * b_re - co_im[..., None] * b_im
    bb_im = co_re[..., None] * b_im + co_im[..., None] * b_re
    ngb = S5_GROUPS // S5_GB
    eye = jnp.eye(S5_GB, dtype=F32)

    def pack_in(w):
        w = w.reshape(2, ngb, S5_GB, S5_STATE, S5_GROUP)
        blk = jnp.einsum('dbgpi,gh->bdgihp', w, eye)
        return blk.reshape(ngb, 2, S5_GB * S5_GROUP, S5_GB * S5_STATE)

    def pack_out(w):
        w = w.reshape(2, ngb, S5_GB, S5_GROUP, S5_STATE)
        blk = jnp.einsum('dbgip,gh->bdgphi', w, eye)
        return blk.reshape(ngb, 2, S5_GB * S5_STATE, S5_GB * S5_GROUP)

    bb = jnp.concatenate([pack_in(bb_re), pack_in(bb_im)], axis=-1).astype(BF16)
    cc = jnp.concatenate([pack_out(c_re), -pack_out(c_im)], axis=-2).astype(BF16)
    lam = jnp.stack([lam_re, lam_im], axis=1)
    lam = lam.reshape(2, 2, ngb, S5_GB * S5_STATE).transpose(2, 0, 1, 3)
    return bb, cc, lam


def _glu_kernel(y_ref, w_ref, b_ref, o_ref, wb_ref):
    @pl.when(pl.program_id(0) == 0)
    def _():
        wb_ref[...] = w_ref[...].astype(BF16)

    z = _dot(y_ref[...].astype(BF16), wb_ref[...]) + b_ref[...]
    half = z.shape[1] // 2
    o_ref[...] = (z[:, :half] * jax.nn.sigmoid(z[:, half:])).astype(o_ref.dtype)


def _glu(y, w, b, l, blk):
    n = y.shape[0]
    k, n2 = w.shape[-2:]
    assert n2 // 2 == k
    spec = pl.BlockSpec((TM, k), lambda i: (i, blk))
    return pl.pallas_call(
        _glu_kernel,
        grid=(n // TM,),
        in_specs=[spec,
                  pl.BlockSpec((None, k, n2), lambda i: (l, 0, 0)),
                  pl.BlockSpec((None, 1, n2), lambda i: (l, 0, 0))],
        out_specs=spec,
        out_shape=jax.ShapeDtypeStruct(y.shape, y.dtype),
        input_output_aliases={0: 0},
        scratch_shapes=[pltpu.VMEM((k, n2), BF16)],
        compiler_params=_cp(("arbitrary",)),
        name="s5_glu",
    )(y, w, b.reshape(b.shape[0], 1, n2))


def _residual_epilogue(y, g_ref, x_old, mv_ref, xo_ref, h_ref):
    mv = mv_ref[...]
    x = x_old + mv[0:1] * _rms(y, g_ref[...])
    xo_ref[...] = x
    h_ref[...] = (_rms(x, mv[3:4]) * (1.0 + mv[1:2]) + mv[2:3]).astype(h_ref.dtype)


def _group_of_tile(tm, n_prompt, seq_s):
    def group(i):
        return jnp.where(i * tm < n_prompt, 0, 1 + (i * tm - n_prompt) // seq_s)
    return group


def _merge_kernel(ya_ref, yb_ref, yc_ref, mg_ref, wbr_ref, wout_ref, g_ref, *rest, na_tiles):
    if na_tiles:
        x_ref, xb_ref, mv_ref, xo_ref, h_ref, wbr_s, wout_s = rest
    else:
        x_ref, mv_ref, xo_ref, h_ref, wbr_s, wout_s = rest

    @pl.when(pl.program_id(0) == 0)
    def _():
        wbr_s[...] = wbr_ref[...].astype(BF16)
        wout_s[...] = wout_ref[...].astype(BF16)

    d = wout_s.shape[0]
    acc = None
    for n, br_ref in enumerate((ya_ref, yb_ref, yc_ref)):
        proj = _dot(br_ref[...].astype(BF16), wbr_s[n])
        term = proj * jax.nn.sigmoid(mg_ref[:, n * d:(n + 1) * d].astype(F32))
        acc = term if acc is None else acc + term
    y = _dot(acc.astype(BF16), wout_s[...])
    x_old = _pick_source(x_ref, xb_ref, na_tiles) if na_tiles else x_ref[...]
    _residual_epilogue(y, g_ref, x_old, mv_ref, xo_ref, h_ref)


def _merge(branches, mg, w_branch, w_out, g, l, x, modv, n_prompt, seq_s, h_dtype):
    n = branches.shape[0]
    d = w_out.shape[-1]
    tm = TM
    group = _group_of_tile(tm, n_prompt, seq_s)
    tile = pl.BlockSpec((tm, d), lambda i: (i, 0))
    xs = x if isinstance(x, tuple) else (x,)
    na_tiles = xs[0].shape[0] // tm if len(xs) == 2 else 0
    x_specs = _two_source_specs(tm, d, na_tiles) if na_tiles else [tile]
    return pl.pallas_call(
        functools.partial(_merge_kernel, na_tiles=na_tiles),
        grid=(n // tm,),
        in_specs=[pl.BlockSpec((tm, BR_WIDTH), lambda i: (i, 0)),
                  pl.BlockSpec((tm, BR_WIDTH), lambda i: (i, 1)),
                  pl.BlockSpec((tm, BR_WIDTH), lambda i: (i, 2)),
                  pl.BlockSpec((tm, 3 * d), lambda i: (i, 0)),
                  pl.BlockSpec((None, 3, BR_WIDTH, d), lambda i: (l, 0, 0, 0), pipeline_mode=pl.Buffered(1)),
                  pl.BlockSpec((None, d, d), lambda i: (l, 0, 0), pipeline_mode=pl.Buffered(1)),
                  pl.BlockSpec((1, d), lambda i: (0, 0)),
                  *x_specs,
                  pl.BlockSpec((None, 8, d), lambda i: (group(i), 0, 0))],
        out_specs=[tile, tile],
        out_shape=[jax.ShapeDtypeStruct((n, d), F32), jax.ShapeDtypeStruct((n, d), h_dtype)],
        scratch_shapes=[pltpu.VMEM((3, BR_WIDTH, d), BF16), pltpu.VMEM((d, d), BF16)],
        compiler_params=_cp(("arbitrary",)),
        name="merge",
    )(branches, branches, branches, mg, w_branch, w_out, g.reshape(1, d), *xs, modv)


def _swiglu_kernel(x_ref, wg_ref, wu_ref, o_ref, wg_s, wu_s):
    @pl.when(pl.program_id(1) == 0)
    def _():
        wg_s[...] = wg_ref[...].astype(BF16)
        wu_s[...] = wu_ref[...].astype(BF16)

    x = x_ref[...]
    o_ref[...] = (_silu(_dot(x, wg_s[...])) * _dot(x, wu_s[...])).astype(BF16)


def _swiglu_up(x, wg, wu, e, f):
    n, d = x.shape
    nf = wg.shape[-1] // f
    wspec = pl.BlockSpec((None, d, f), lambda j, i: (e, 0, j))
    return pl.pallas_call(
        _swiglu_kernel,
        grid=(nf, n // TM),
        in_specs=[pl.BlockSpec((TM, d), lambda j, i: (i, 0)), wspec, wspec],
        out_specs=pl.BlockSpec((TM, f), lambda j, i: (i, j)),
        out_shape=jax.ShapeDtypeStruct((n, nf * f), BF16),
        scratch_shapes=[pltpu.VMEM((d, f), BF16), pltpu.VMEM((d, f), BF16)],
        compiler_params=_cp(("arbitrary", "arbitrary")),
        name="swiglu_up",
    )(x, wg, wu)


def _down_kernel(hid_ref, w_ref, g_ref, x_ref, mv_ref, xo_ref, h_ref, w_s):
    @pl.when(pl.program_id(0) == 0)
    def _():
        w_s[...] = w_ref[...].astype(BF16)

    _residual_epilogue(_dot(hid_ref[...], w_s[...]), g_ref, x_ref[...], mv_ref, xo_ref, h_ref)


def _down_res(hid, wd, e, g, x, modv, n_prompt, seq_s, h_dtype):
    n, f = hid.shape
    d = wd.shape[-1]
    group = _group_of_tile(TM, n_prompt, seq_s)
    tile = pl.BlockSpec((TM, d), lambda i: (i, 0))
    return pl.pallas_call(
        _down_kernel,
        grid=(n // TM,),
        in_specs=[pl.BlockSpec((TM, f), lambda i: (i, 0)),
                  pl.BlockSpec((None, f, d), lambda i: (e, 0, 0)),
                  pl.BlockSpec((1, d), lambda i: (0, 0)),
                  tile,
                  pl.BlockSpec((None, 8, d), lambda i: (group(i), 0, 0))],
        out_specs=[tile, tile],
        out_shape=[jax.ShapeDtypeStruct((n, d), F32), jax.ShapeDtypeStruct((n, d), h_dtype)],
        scratch_shapes=[pltpu.VMEM((f, d), BF16)],
        compiler_params=_cp(("arbitrary",)),
        name="down_res",
    )(hid, wd, g.reshape(1, d), x, modv)


def _router_kernel(x_ref, w_ref, aux_ref, cnt_ref, base_s, *, ne):
    @pl.when(pl.program_id(0) == 0)
    def _():
        base_s[...] = jnp.zeros(base_s.shape, F32)

    w = w_ref[...]
    w_hi = w.astype(BF16)
    w_lo = (w - w_hi.astype(F32)).astype(BF16)
    x = x_ref[...].astype(BF16)
    logits = _dot(x, w_hi) + _dot(x, w_lo)
    tm = logits.shape[0]
    lane = lax.broadcasted_iota(jnp.int32, logits.shape, 1).astype(F32)
    neg = jnp.float32(-jnp.inf)
    logits = jnp.where(lane < ne, logits, neg)
    m1 = jnp.max(logits, axis=-1, keepdims=True)
    i1 = jnp.min(jnp.where(logits == m1, lane, float(LANES)), axis=-1, keepdims=True)
    rest = jnp.where(lane == i1, neg, logits)
    m2 = jnp.max(rest, axis=-1, keepdims=True)
    i2 = jnp.min(jnp.where(rest == m2, lane, float(LANES)), axis=-1, keepdims=True)
    e2 = jnp.exp(m2 - m1)
    w1 = 1.0 / (1.0 + e2)
    w2 = e2 / (1.0 + e2)
    sel = jnp.where((lane == i1) | (lane == i2), 1.0, 0.0)
    ri = lax.broadcasted_iota(jnp.int32, (tm, tm), 0)
    ci = lax.broadcasted_iota(jnp.int32, (tm, tm), 1)
    before = jnp.where(ci < ri, 1.0, 0.0).astype(BF16)
    rank = _dot(before, sel.astype(BF16)) + base_s[0:1, :]
    r1 = jnp.sum(jnp.where(lane == i1, rank, 0.0), axis=-1, keepdims=True)
    r2 = jnp.sum(jnp.where(lane == i2, rank, 0.0), axis=-1, keepdims=True)
    aux = jnp.zeros(logits.shape, F32)
    for k, v in enumerate((i1, i2, r1, r2, w1, w2)):
        aux = jnp.where(lane == float(k), v, aux)
    aux_ref[...] = aux
    base_s[...] = base_s[...] + jnp.sum(sel, axis=0, keepdims=True)
    cnt_ref[...] = base_s[...]


def _router(x, w_pad, ne):
    n, d = x.shape
    return pl.pallas_call(
        functools.partial(_router_kernel, ne=ne),
        grid=(n // TM,),
        in_specs=[pl.BlockSpec((TM, d), lambda i: (i, 0)),
                  pl.BlockSpec((d, LANES), lambda i: (0, 0))],
        out_specs=[pl.BlockSpec((TM, LANES), lambda i: (i, 0)),
                   pl.BlockSpec((8, LANES), lambda i: (0, 0))],
        out_shape=[jax.ShapeDtypeStruct((n, LANES), F32), jax.ShapeDtypeStruct((8, LANES), F32)],
        scratch_shapes=[pltpu.VMEM((8, LANES), F32)],
        compiler_params=_cp(("arbitrary",)),
        name="router",
    )(x, w_pad)


def _moe_plan(aux, cnt, n, ne, tile, max_tiles):
    e = aux[:, 0:2].astype(jnp.int32)
    r = aux[:, 2:4].astype(jnp.int32)
    c = cnt[0, :ne].astype(jnp.int32)
    pc = (c + tile - 1) // tile * tile
    ends = jnp.cumsum(pc)
    slot = ((ends - pc)[e] + r).reshape(-1)
    tok = jnp.broadcast_to(jnp.arange(n, dtype=F32)[:, None], (n, 2)).reshape(-1)
    m = max_tiles * tile
    pad = jnp.broadcast_to(jnp.array([float(n), 0.0], F32), (m, 2))
    table = pad.at[slot].set(jnp.stack([tok, aux[:, 4:6].reshape(-1)], axis=1))
    slot_tok = table[:, 0].astype(jnp.int32)
    slot_gate = table[:, 1]
    tstart = jnp.arange(max_tiles, dtype=jnp.int32) * tile
    tile_e = jnp.minimum(jnp.sum(tstart[:, None] >= ends[None, :], axis=1), ne - 1).astype(jnp.int32)
    tile_valid = (tstart < ends[-1]).astype(jnp.int32)
    return slot_tok.reshape(max_tiles, 1, tile), slot_gate.reshape(max_tiles, 1, tile), tile_e, tile_valid


def _expert_changed(t, te_ref):
    return jnp.logical_or(t == 0, te_ref[t] != te_ref[jnp.maximum(t - 1, 0)])


def _moe_up_kernel(te_ref, tv_ref, idx_ref, nxt_ref, h_ref, wg_ref, wu_ref, o_ref, xbuf, wg_s, wu_s, *, tile):
    t = pl.program_id(0)
    valid = tv_ref[t] == 1
    last = h_ref.shape[0] - 1
    cur = t % 2

    def gather_row(src_idx_ref, buf, j):
        xbuf[buf, pl.ds(j, 1), :] = h_ref[pl.ds(jnp.minimum(src_idx_ref[0, 0, j], last), 1), :]

    @pl.when(t == 0)
    def _():
        def first(j, carry):
            gather_row(idx_ref, 0, j)
            return carry

        lax.fori_loop(0, tile, first, 0, unroll=8)

    @pl.when(jnp.logical_and(valid, _expert_changed(t, te_ref)))
    def _():
        wg_s[...] = wg_ref[...].astype(BF16)
        wu_s[...] = wu_ref[...].astype(BF16)

    @pl.when(valid)
    def _():
        x = xbuf[cur].astype(BF16)
        o_ref[...] = (_silu(_dot(x, wg_s[...])) * _dot(x, wu_s[...])).astype(BF16)
        for j in range(tile):
            gather_row(nxt_ref, 1 - cur, j)

    @pl.when(jnp.logical_not(valid))
    def _():
        o_ref[...] = jnp.zeros(o_ref.shape, BF16)


def _moe_up(h, slot_tok, tile_e, tile_valid, wg, wu, e_layer, tile):
    max_tiles = slot_tok.shape[0]
    _, _, d, f = wg.shape
    wspec = pl.BlockSpec((None, None, d, f), lambda t, te, tv: (e_layer, te[t], 0, 0))
    grid_spec = pltpu.PrefetchScalarGridSpec(
        num_scalar_prefetch=2,
        grid=(max_tiles,),
        in_specs=[pl.BlockSpec((1, 1, tile), lambda t, te, tv: (t, 0, 0), memory_space=pltpu.SMEM),
                  pl.BlockSpec((1, 1, tile), lambda t, te, tv: (jnp.minimum(t + 1, max_tiles - 1), 0, 0),
                               memory_space=pltpu.SMEM),
                  pl.BlockSpec(h.shape, lambda t, te, tv: (0, 0), pipeline_mode=pl.Buffered(1)),
                  wspec, wspec],
        out_specs=pl.BlockSpec((tile, f), lambda t, te, tv: (t, 0)),
        scratch_shapes=[pltpu.VMEM((2, tile, d), F32),
                        pltpu.VMEM((d, f), BF16), pltpu.VMEM((d, f), BF16)],
    )
    return pl.pallas_call(
        functools.partial(_moe_up_kernel, tile=tile),
        grid_spec=grid_spec,
        out_shape=jax.ShapeDtypeStruct((max_tiles * tile, f), BF16),
        compiler_params=_cp(("arbitrary",), vmem=MOE_UP_VMEM_LIMIT),
        name="moe_up",
    )(tile_e, tile_valid, slot_tok, slot_tok, h, wg, wu)


def _moe_down_kernel(te_ref, tv_ref, idx_ref, gt_ref, hid_ref, wd_ref, g_ref, o_ref, acc, ybuf, tbuf, wd_s,
                     *, tile, n_tiles, tm_out):
    t = pl.program_id(0)
    tt = jnp.minimum(t, n_tiles - 1)

    @pl.when(t == 0)
    def _():
        acc[...] = jnp.zeros(acc.shape, F32)

    is_tile = jnp.logical_and(t < n_tiles, tv_ref[tt] == 1)

    @pl.when(jnp.logical_and(is_tile, _expert_changed(tt, te_ref)))
    def _():
        wd_s[...] = wd_ref[...].astype(BF16)

    @pl.when(is_tile)
    def _():
        for j in range(tile):
            tbuf[j:j + 1, :] = acc[pl.ds(idx_ref[0, 0, j], 1), :]
        ybuf[...] = _dot(hid_ref[...], wd_s[...])

        def put(j, carry):
            acc[pl.ds(idx_ref[0, 0, j], 1), :] = tbuf[pl.ds(j, 1), :] + gt_ref[0, 0, j] * ybuf[pl.ds(j, 1), :]
            return carry

        lax.fori_loop(0, tile, put, 0, unroll=8)

    @pl.when(t >= n_tiles)
    def _():
        r0 = pl.multiple_of((t - n_tiles) * tm_out, tm_out)
        o_ref[...] = _rms(acc[pl.ds(r0, tm_out), :], g_ref[...])


def _moe_down(hid, slot_tok, slot_gate, tile_e, tile_valid, wd, e_layer, g, n, tile):
    n_tiles = slot_tok.shape[0]
    _, _, f, d = wd.shape
    tm_out = TM
    n_flush = n // tm_out

    def tix(t):
        return jnp.minimum(t, n_tiles - 1)

    smem_tile = pl.BlockSpec((1, 1, tile), lambda t, te, tv: (tix(t), 0, 0), memory_space=pltpu.SMEM)
    grid_spec = pltpu.PrefetchScalarGridSpec(
        num_scalar_prefetch=2,
        grid=(n_tiles + n_flush,),
        in_specs=[smem_tile, smem_tile,
                  pl.BlockSpec((tile, f), lambda t, te, tv: (tix(t), 0)),
                  pl.BlockSpec((None, None, f, d), lambda t, te, tv: (e_layer, te[tix(t)], 0, 0)),
                  pl.BlockSpec((1, d), lambda t, te, tv: (0, 0))],
        out_specs=pl.BlockSpec((tm_out, d), lambda t, te, tv: (jnp.maximum(t - n_tiles, 0), 0)),
        scratch_shapes=[pltpu.VMEM((n + 8, d), F32), pltpu.VMEM((tile, d), F32), pltpu.VMEM((tile, d), F32),
                        pltpu.VMEM((f, d), BF16)],
    )
    return pl.pallas_call(
        functools.partial(_moe_down_kernel, tile=tile, n_tiles=n_tiles, tm_out=tm_out),
        grid_spec=grid_spec,
        out_shape=jax.ShapeDtypeStruct((n, d), F32),
        compiler_params=_cp(("arbitrary",)),
        name="moe_down",
    )(tile_e, tile_valid, slot_tok, slot_gate, hid, wd, g.reshape(1, d))


def kernel(x_prompt, x_sample, state_hgrn, state_s5, state_gdn, c, c_ctx, ada_w, ada_b, norm_g, w_in, hgrn_lb, hgrn_norm, s5_a_re, s5_a_im, s5_log_dt, s5_b_re, s5_b_im, s5_c_re, s5_c_im, s5_d, s5_glu_w, s5_glu_b, gdn_conv, gdn_a_log, gdn_dt_bias, gdn_norm, w_branch, w_out, ffn_wg, ffn_wu, ffn_wd, moe_router, moe_wg, moe_wu, moe_wd):
    bp, tp, d = x_prompt.shape
    bs, ts, _ = x_sample.shape
    depth = w_in.shape[0]
    n_p, n_s = bp * tp, bs * ts
    ne = moe_router.shape[-1]
    main_cols = 10 * BR_WIDTH
    ab_cols = 4 * N_HEADS

    c8 = jnp.zeros((8, d), F32).at[0].set(c_ctx).at[1:1 + bs].set(c)
    mod_all = _ada_mod(c8, ada_w, ada_b)

    lb_soft = jax.nn.softmax(hgrn_lb.astype(F32), axis=0)
    lb_all = jnp.cumsum(lb_soft, axis=0) - lb_soft[:1]

    mods = mod_all[:, :1 + bs].reshape(depth, 1 + bs, 6, d)
    zero_row = jnp.zeros((1 + bs, d), F32)

    def modv(gate, scale, shift, g):
        rows = [zero_row if sel is None else mods[sel[0], :, sel[1]] for sel in (gate, scale, shift)]
        rows.append(zero_row if g is None else jnp.broadcast_to(g, (1 + bs, d)))
        return jnp.stack(rows + [zero_row] * 4, axis=1)

    x = (x_prompt.reshape(n_p, d), x_sample.reshape(n_s, d))
    (h,) = _resnorm(x, None, modv(None, (0, 1), (0, 0), norm_g[0, 0]), n_p, ts, True)

    z_hgrn = jnp.zeros((bp, 2, N_HEADS, HEAD_D, HEAD_D), F32)
    ngb = S5_GROUPS // S5_GB
    w2 = 2 * S5_GB * S5_STATE
    st_s5 = []
    y_zero = jnp.zeros((n_p + n_s, 3 * BR_WIDTH), BF16)
    new_hgrn = new_gdn = jnp.zeros((bp, depth, 2, N_HEADS, HEAD_D, HEAD_D), F32)
    w_in_t = jnp.swapaxes(w_in, 1, 2)
    s5_tb = 1024

    for l in range(depth):
        col_major = l % 2 == 1
        p_main = _mm_t(h, w_in_t, l, main_cols, 2 * BR_WIDTH, "w_in_main", tm=2048)
        w_ab = jnp.pad(w_in_t[l, main_cols:main_cols + ab_cols], ((0, LANES - ab_cols), (0, 0)))
        p_ab = _mm_t(h, w_ab, None, LANES, LANES, "w_in_ab", tm=1024)
        p_mg = _mm_t(h, w_in_t[l, main_cols + ab_cols:], None, 3 * d, 2 * BR_WIDTH, "w_in_merge", tm=2048,
                     out_dtype=BF16)

        blk = BR_WIDTH // LANES
        br = y_zero

        br, new_hgrn = _hgrn(p_main, lb_all[l], hgrn_norm[l], z_hgrn, 0, bp, tp, br, 0, new_hgrn, l, depth)
        br, _ = _hgrn(p_main, lb_all[l], hgrn_norm[l], state_hgrn[:, l].astype(F32), n_p // ts, bs, ts, br, 0,
                      col_major=col_major)

        bb, cc, lam = _s5_params(s5_a_re[l], s5_a_im[l], s5_log_dt[l], s5_b_re[l], s5_b_im[l],
                                 s5_c_re[l], s5_c_im[l])
        seq_per_blk = 8
        tb_p = seq_per_blk * tp
        nblk_p = n_p // tb_p
        br, sb = _s5(p_main, bb, cc, lam, s5_d[l], jnp.zeros((nblk_p, ngb, 2, 8, w2), F32),
                     0, nblk_p, tb_p, tp, tp, True, br, blk)
        s0s = state_s5[:, l].astype(F32)
        s0s = s0s.reshape(bs, 2, ngb, S5_GB * S5_STATE, 2).transpose(0, 2, 1, 4, 3).reshape(bs, ngb, 2, 1, w2)
        s0s = jnp.pad(s0s, ((0, 0), (0, 0), (0, 0), (0, 7), (0, 0)))
        br, _ = _s5(p_main, bb, cc, lam, s5_d[l], s0s, n_p // s5_tb, n_s // s5_tb, s5_tb, ts, S5_SEG, False, br, blk,
                    col_major=col_major)
        br = _glu(br, s5_glu_w, s5_glu_b, l, 1)
        sb = sb[:, :, :, :seq_per_blk].reshape(nblk_p, ngb, 2, seq_per_blk, 2, S5_GB, S5_STATE)
        sb = sb.transpose(0, 3, 2, 1, 5, 6, 4).reshape(bp, 2, S5_GROUPS, S5_STATE, 2)
        st_s5.append(sb)

        br, new_gdn = _gdn(p_main, p_ab, gdn_conv[l], gdn_a_log[l], gdn_dt_bias[l], gdn_norm[l], z_hgrn, 0, bp, tp,
                           br, 2 * blk, new_gdn, l, depth)
        br, _ = _gdn(p_main, p_ab, gdn_conv[l], gdn_a_log[l], gdn_dt_bias[l], gdn_norm[l],
                     state_gdn[:, l].astype(F32), n_p // ts, bs, ts, br, 2 * blk, col_major=col_major)

        x, h2 = _merge(br, p_mg, w_branch, w_out, norm_g[l, 1], l, x,
                       modv((l, 2), (l, 4), (l, 3), norm_g[l, 2]), n_p, ts, BF16 if l % 2 == 0 else F32)
        last = l + 1 == depth
        next_mod = modv((l, 5), None, None, None) if last else modv((l, 5), (l + 1, 1), (l + 1, 0), norm_g[l + 1, 0])

        if l % 2 == 0:
            e = l // 2
            hid = _swiglu_up(h2, ffn_wg, ffn_wu, e, ffn_wg.shape[-1] // 2)
            x, h = _down_res(hid, ffn_wd, e, norm_g[l, 3], x, next_mod, n_p, ts, BF16)
            if last:
                y_prompt, y_sample = x[:n_p], x[n_p:]
            continue
        else:
            e = l // 2
            n_tok = n_p + n_s
            max_tiles = 2 * n_tok // MOE_TILE + ne
            aux, cnt = _router(h2, jnp.pad(moe_router[e], ((0, 0), (0, LANES - ne))), ne)
            slot_tok, slot_gate, tile_e, tile_valid = _moe_plan(aux, cnt, n_tok, ne, MOE_TILE, max_tiles)
            hid = _moe_up(h2, slot_tok, tile_e, tile_valid, moe_wg, moe_wu, e, MOE_TILE)
            r_f = _moe_down(hid, slot_tok, slot_gate, tile_e, tile_valid, moe_wd, e, norm_g[l, 3], n_tok, MOE_TILE)

        if not last:
            x, h = _resnorm(x, r_f, next_mod, n_p, ts, True)
        else:
            (y_prompt,) = _resnorm(x, r_f, next_mod, n_p, ts, False, rows=(0, n_p))
            (y_sample,) = _resnorm(x, r_f, next_mod, n_p, ts, False, rows=(n_p, n_s))

    y_prompt = y_prompt.reshape(bp, tp, d)
    y_sample = y_sample.reshape(bs, ts, d)
    new_s5 = jnp.stack(st_s5, axis=1).astype(x_prompt.dtype)
    return (y_prompt, y_sample, new_hgrn.astype(x_prompt.dtype), new_s5, new_gdn.astype(x_prompt.dtype))
```

```python
import functools
import math

import jax
import jax.numpy as jnp
from jax import lax
from jax.experimental import pallas as pl
from jax.experimental.pallas import tpu as pltpu

F32 = jnp.float32
BF16 = jnp.bfloat16

EPS = 1e-6
F_FLOOR = 1e-6
LOG2E = 1.4426950408889634
GRID_W = 64
HEAD_D = 128
N_HEADS = 4
BR_WIDTH = 512
S5_GROUPS = 32
S5_GROUP = 16
S5_STATE = 64
LANES = 128
V7X_VMEM_BYTES = 64 * 1024 * 1024
VMEM_LIMIT = V7X_VMEM_BYTES - 8 * 1024 * 1024
MOE_UP_VMEM_LIMIT = V7X_VMEM_BYTES - 3 * 1024 * 1024
S5_GB = 8
S5_SEG = 64
GLA_CHUNK = 256
GDN_CHUNK = 64
GDN_SPAN = 256
GDN_HEADS_PER_STEP = 4
GDN_MAX_ROWS_PER_STEP = 2048
TM = 512
MOE_TILE = 256


def _cp(sem, vmem=VMEM_LIMIT):
    return pltpu.CompilerParams(dimension_semantics=sem, vmem_limit_bytes=vmem)


def _dot(a, b):
    return jnp.dot(a, b, preferred_element_type=F32)


def _dot_nt(a, b):
    return lax.dot_general(a, b, (((1,), (1,)), ((), ())), preferred_element_type=F32)


def _dot_tn(a, b):
    return lax.dot_general(a, b, (((0,), (0,)), ((), ())), preferred_element_type=F32)


def _silu(x):
    return x * jax.nn.sigmoid(x)


def _rms(x, g):
    return x * lax.rsqrt(jnp.mean(x * x, axis=-1, keepdims=True) + EPS) * g


def _chunk_cumsum(x, d, c):
    n = x.shape[0]
    pos = lax.broadcasted_iota(jnp.int32, x.shape, 0) & (c - 1)
    s = 1
    while s < c:
        if d == 0:
            x = x + jnp.where(pos >= s, pltpu.roll(x, s, 0), 0.0)
        else:
            x = x + jnp.where(pos < c - s, pltpu.roll(x, n - s, 0), 0.0)
        s *= 2
    return x


def _ada_kernel(c_ref, w_ref, b_ref, o_ref):
    c = c_ref[...]
    a = _silu(c).astype(BF16)
    o_ref[...] = _dot(a, w_ref[...].astype(BF16)) + b_ref[...]


def _ada_mod(c8, ada_w, ada_b):
    depth, d, n6 = ada_w.shape
    tn = 1536
    return pl.pallas_call(
        _ada_kernel,
        grid=(depth, n6 // tn),
        in_specs=[
            pl.BlockSpec((8, d), lambda l, j: (0, 0)),
            pl.BlockSpec((None, d, tn), lambda l, j: (l, 0, j)),
            pl.BlockSpec((None, 1, tn), lambda l, j: (l, 0, j)),
        ],
        out_specs=pl.BlockSpec((None, 8, tn), lambda l, j: (l, 0, j)),
        out_shape=jax.ShapeDtypeStruct((depth, 8, n6), F32),
        compiler_params=_cp(("parallel", "parallel")),
        name="ada_mod",
    )(c8, ada_w, ada_b.reshape(depth, 1, n6))


def _two_source_specs(tm, d, na_tiles):
    return [pl.BlockSpec((tm, d), lambda i: (jnp.minimum(i, na_tiles - 1), 0)),
            pl.BlockSpec((tm, d), lambda i: (jnp.maximum(i - na_tiles, 0), 0))]


def _pick_source(xa_ref, xb_ref, na_tiles):
    return jnp.where(pl.program_id(0) < na_tiles, xa_ref[...], xb_ref[...])


def _resnorm_kernel(*refs, has_y, has_h, na_tiles):
    it = iter(refs)
    x_ref = next(it)
    xb_ref = next(it) if na_tiles else None
    y_ref = next(it) if has_y else None
    mv_ref = next(it)
    xo_ref = next(it) if has_y else None
    h_ref = next(it) if has_h else None
    x = _pick_source(x_ref, xb_ref, na_tiles) if na_tiles else x_ref[...]
    mv = mv_ref[...]
    if has_y:
        x = x + mv[0:1] * y_ref[...]
        xo_ref[...] = x
    if has_h:
        h = _rms(x, mv[3:4]) * (1.0 + mv[1:2]) + mv[2:3]
        h_ref[...] = h.astype(h_ref.dtype)


def _resnorm(x, y, modv, n_prompt, seq_s, has_h, rows=None, h_dtype=BF16):
    xs = x if isinstance(x, tuple) else (x,)
    d = xs[0].shape[1]
    row0, n = (0, sum(a.shape[0] for a in xs)) if rows is None else rows
    i0 = row0 // TM
    has_y = y is not None
    na_tiles = xs[0].shape[0] // TM if len(xs) == 2 else 0

    def group(i):
        r = (i + i0) * TM
        return jnp.where(r < n_prompt, 0, 1 + (r - n_prompt) // seq_s)

    tile = pl.BlockSpec((TM, d), lambda i: (i + i0, 0))
    x_specs = _two_source_specs(TM, d, na_tiles) if na_tiles else [tile]
    in_specs = x_specs + ([tile] if has_y else []) + [pl.BlockSpec((None, 8, d), lambda i: (group(i), 0, 0))]
    out_tile = pl.BlockSpec((TM, d), lambda i: (i, 0))
    out_specs, out_shape = [], []
    if has_y:
        out_specs.append(out_tile)
        out_shape.append(jax.ShapeDtypeStruct((n, d), F32))
    if has_h:
        out_specs.append(out_tile)
        out_shape.append(jax.ShapeDtypeStruct((n, d), h_dtype))
    args = list(xs) + ([y] if has_y else []) + [modv]
    outs = pl.pallas_call(
        functools.partial(_resnorm_kernel, has_y=has_y, has_h=has_h, na_tiles=na_tiles),
        grid=(n // TM,),
        in_specs=in_specs,
        out_specs=out_specs,
        out_shape=out_shape,
        compiler_params=_cp(("parallel",)),
        name="resnorm",
    )(*args)
    return outs


def _mm_kernel(x_ref, w_ref, o_ref, wb_ref):
    @pl.when(pl.program_id(1) == 0)
    def _():
        wb_ref[...] = w_ref[...].astype(BF16)

    o_ref[...] = _dot_nt(x_ref[...], wb_ref[...]).astype(o_ref.dtype)


def _mm_t(x, wt, l, nrows, tn, name, tm=TM, out_dtype=F32):
    m, k = x.shape
    if l is None:
        wspec = pl.BlockSpec((tn, k), lambda j, i: (j, 0))
    else:
        wspec = pl.BlockSpec((None, tn, k), lambda j, i: (l, j, 0))
    return pl.pallas_call(
        _mm_kernel,
        grid=(nrows // tn, m // tm),
        in_specs=[pl.BlockSpec((tm, k), lambda j, i: (i, 0)), wspec],
        out_specs=pl.BlockSpec((tm, tn), lambda j, i: (i, j)),
        out_shape=jax.ShapeDtypeStruct((m, nrows), out_dtype),
        scratch_shapes=[pltpu.VMEM((tn, k), BF16)],
        compiler_params=_cp(("arbitrary", "arbitrary")),
        name=name,
    )(x, wt)


def _gla_intra(q, k, b, d, c):
    rowi = lax.broadcasted_iota(jnp.int32, (c, LANES), 0)
    ri = lax.broadcasted_iota(jnp.int32, (c, c), 0)
    ci = lax.broadcasted_iota(jnp.int32, (c, c), 1)
    a = jnp.zeros((c, c), F32)
    size = c
    while size >= 8:
        half = size // 2
        nb = c // size
        b3 = b.reshape(nb, size, LANES)
        pos = rowi & (size - 1)
        if d == 0:
            ref = b3[:, half - 1:half, :]
            qsel = pos >= half
        else:
            ref = b3[:, half:half + 1, :]
            qsel = pos < half
        ref = jnp.broadcast_to(ref, (nb, size, LANES)).reshape(c, LANES)
        z = jnp.where(qsel, q, k) * jnp.exp2(-jnp.abs(b - ref))
        qs = jnp.where(qsel, z, 0.0).astype(BF16)
        ks = jnp.where(qsel, 0.0, z).astype(BF16)
        p = _dot_nt(qs, ks)
        if nb > 1:
            shift = int(math.log2(size))
            p = jnp.where((ri >> shift) == (ci >> shift), p, 0.0)
        a = a + p
        size = half
    b3 = b.reshape(c // 8, 8, LANES)
    lo, hi = (0, 4) if d == 0 else (3, 7)
    ref_lo = jnp.broadcast_to(b3[:, lo:lo + 1, :], (c // 8, 8, LANES)).reshape(c, LANES)
    ref_hi = jnp.broadcast_to(b3[:, hi:hi + 1, :], (c // 8, 8, LANES)).reshape(c, LANES)
    ref = jnp.where((rowi & 7) >= 4, ref_hi, ref_lo)
    qs = (q * jnp.exp2(jnp.minimum(b - ref, 0.0))).astype(BF16)
    ks = (k * jnp.exp2(ref - b)).astype(BF16)
    p = _dot_nt(qs, ks)
    keep = ((ri >> 2) == (ci >> 2)) & ((ci <= ri) if d == 0 else (ci >= ri))
    return a + jnp.where(keep, p, 0.0)


def _load_column_major(dst_ref, src_ref, t, stage=None):
    rows = t // GRID_W
    for k in range(src_ref.shape[1] // LANES):
        lanes = slice(k * LANES, (k + 1) * LANES)
        if stage is None:
            strided = src_ref
        else:
            stage[...] = src_ref[:, lanes]
            strided = stage
        for col in range(GRID_W):
            dst_ref[col * rows:(col + 1) * rows, lanes] = strided[pl.ds(col, rows, stride=GRID_W), :]


def _store_row_major(dst_ref, y, t, stage=None):
    rows = t // GRID_W
    for k in range(y.shape[1] // LANES):
        lanes = slice(k * LANES, (k + 1) * LANES)
        strided = dst_ref if stage is None else stage
        for col in range(GRID_W):
            strided[pl.ds(col, rows, stride=GRID_W), :] = y[col * rows:(col + 1) * rows, lanes]
        if stage is not None:
            dst_ref[:, lanes] = stage[...].astype(dst_ref.dtype)


def _hgrn_kernel(aq_ref, af_ref, ab_ref, ai_ref, ag_ref, lb_ref, g_ref, s0_ref, y_ref, s_ref, o_acc, *perm_bufs,
                 t, c):
    stage = None
    if perm_bufs:
        *perm_bufs, stage = perm_bufs
        srcs = (aq_ref, af_ref, ab_ref, ai_ref, ag_ref)
        for dst, src in zip(perm_bufs, srcs):
            _load_column_major(dst, src, t)
        aq_ref, af_ref, ab_ref, ai_ref, ag_ref = perm_bufs
    nchunk = t // c
    scale = HEAD_D ** -0.5
    for d in (0, 1):
        z_ref = af_ref if d == 0 else ab_ref
        lbd = lb_ref[d:d + 1, :]

        def chunk(i, st, d=d, z_ref=z_ref, lbd=lbd):
            cidx = i if d == 0 else nchunk - 1 - i
            r0 = pl.multiple_of(cidx * c, c)
            z = z_ref[pl.ds(r0, c), :]
            f = lbd + (1.0 - lbd) * jax.nn.sigmoid(z)
            lf = jnp.log2(jnp.maximum(f, F_FLOOR))
            k = (1.0 - lbd) * jax.nn.sigmoid(-z)
            q = _silu(aq_ref[pl.ds(r0, c), :]) * scale
            v = ai_ref[pl.ds(r0, c), :].astype(BF16)
            b = _chunk_cumsum(lf, d, c)
            a = _gla_intra(q, k, b, d, c)
            o = _dot(a.astype(BF16), v) + _dot_nt((q * jnp.exp2(b)).astype(BF16), st.astype(BF16))
            if d == 0:
                o_acc[pl.ds(r0, c), :] = o
            else:
                o_acc[pl.ds(r0, c), :] += o
            blast = b[c - 1:c, :] if d == 0 else b[0:1, :]
            kd = (k * jnp.exp2(blast - b)).astype(BF16)
            return st * jnp.exp2(blast) + _dot_tn(v, kd)

        st = s0_ref[0, d, 0].T
        if nchunk == 1:
            st = chunk(0, st)
        else:
            st = lax.fori_loop(0, nchunk, chunk, st)
        s_ref[0, d, 0] = st.T
    y = _rms(o_acc[...], g_ref[...]) * _silu(ag_ref[...])
    if perm_bufs:
        _store_row_major(y_ref, y, t, stage)
    else:
        y_ref[...] = y.astype(y_ref.dtype)


def _shared_out_call(kernel_fn, prevs, in_specs, args, **kw):
    extra = [p for p in prevs if p is not None]
    if not extra:
        return pl.pallas_call(kernel_fn, in_specs=in_specs, **kw)(*args)
    n_in = len(in_specs)
    aliases, k = {}, 0
    for out_idx, p in enumerate(prevs):
        if p is not None:
            aliases[n_in + k] = out_idx
            k += 1

    def body(*refs):
        kernel_fn(*refs[:n_in], *refs[n_in + len(extra):])

    return pl.pallas_call(body, in_specs=in_specs + [pl.BlockSpec(memory_space=pl.ANY)] * len(extra),
                          input_output_aliases=aliases, **kw)(*args, *extra)


def _state_out(nseq, slot, nslots, hg):
    spec = pl.BlockSpec((1, None, 2, hg, HEAD_D, HEAD_D), lambda s, h: (s, slot, 0, h, 0, 0))
    return spec, jax.ShapeDtypeStruct((nseq, nslots, 2, N_HEADS, HEAD_D, HEAD_D), F32)


def _hgrn(p_main, lb, norm_g, s0, row_blk0, nseq, t, y_prev, col0, s_prev=None, slot=0, nslots=1,
          col_major=False):
    c = min(GLA_CHUNK, t)
    seq_buf = pltpu.VMEM((t, HEAD_D), F32)

    def col(base):
        return pl.BlockSpec((t, HEAD_D), lambda s, h: (row_blk0 + s, base + h))

    st_spec = pl.BlockSpec((1, 2, 1, HEAD_D, HEAD_D), lambda s, h: (s, 0, h, 0, 0))
    so_spec, so_shape = _state_out(nseq, slot, nslots, 1)
    return _shared_out_call(
        functools.partial(_hgrn_kernel, t=t, c=c), [y_prev, s_prev],
        [col(0), col(4), col(8), col(12), col(16),
         pl.BlockSpec((2, HEAD_D), lambda s, h: (0, h)),
         pl.BlockSpec((1, HEAD_D), lambda s, h: (0, 0)),
         st_spec],
        (p_main, p_main, p_main, p_main, p_main, lb, norm_g.reshape(1, HEAD_D), s0),
        grid=(nseq, N_HEADS),
        out_specs=[pl.BlockSpec((t, HEAD_D), lambda s, h: (row_blk0 + s, col0 + h)), so_spec],
        out_shape=[jax.ShapeDtypeStruct(y_prev.shape, y_prev.dtype), so_shape],
        scratch_shapes=[seq_buf] * (7 if col_major else 1),
        compiler_params=_cp(("parallel", "parallel")),
        name="hgrn2",
    )


def _dot_b(a, b):
    return _dot(a.astype(BF16), b.astype(BF16))


def _tri_solve(ms, xs, c):
    ri = lax.broadcasted_iota(jnp.int32, (c, c), 0)
    ci = lax.broadcasted_iota(jnp.int32, (c, c), 1)
    eye = jnp.where(ri == ci, 1.0, 0.0)
    m8 = [jnp.where((ri >> 3) == (ci >> 3), m, 0.0) for m in ms]
    ts = [eye - m for m in m8]
    ps = [_dot_b(m, m) for m in m8]
    ts = [t + _dot_b(t, p) for t, p in zip(ts, ps)]
    ps = [_dot_b(p, p) for p in ps]
    ts = [t + _dot_b(t, p) for t, p in zip(ts, ps)]
    shift = 3
    while (1 << shift) < c:
        join = ((ri >> (shift + 1)) == (ci >> (shift + 1))) & ((ri >> shift) != (ci >> shift))
        lts = [_dot_b(jnp.where(join, m, 0.0), t) for m, t in zip(ms, ts)]
        ts = [t - _dot_b(t, lt) for t, lt in zip(ts, lts)]
        shift += 1
    return [_dot_b(t, x) for t, x in zip(ts, xs)]


def _gdn_kernel(cq_ref, ck_ref, cv_ref, cg_ref, ab_ref, cw_ref, alog_ref, dtb_ref, g_ref, s0_ref,
                y_ref, s_ref, q_s, k_s, v_s, la_s, be_s, w_s, qk_s, qe_s, kd_s, dec_s, o_acc, *perm_bufs,
                t, c, span, hg):
    stage = None
    if perm_bufs:
        *perm_bufs, stage = perm_bufs
        srcs = (cq_ref, ck_ref, cv_ref, cg_ref, ab_ref)
        for dst, src in zip(perm_bufs, srcs):
            _load_column_major(dst, src, t, stage)
        cq_ref, ck_ref, cv_ref, cg_ref, ab_ref = perm_bufs
    nchunk = t // c
    cps = span // c
    rowt = lax.broadcasted_iota(jnp.int32, (t, HEAD_D), 0)
    lane = lax.broadcasted_iota(jnp.int32, (t, LANES), 1)
    ri = lax.broadcasted_iota(jnp.int32, (span, span), 0)
    ci = lax.broadcasted_iota(jnp.int32, (span, span), 1)
    shift = int(math.log2(c))
    same = (ri >> shift) == (ci >> shift)
    heads = range(hg)
    dirs = (0, 1)

    def conv(x_ref, j, hh):
        x = x_ref[:, hh * HEAD_D:(hh + 1) * HEAD_D]
        w = cw_ref[j][:, hh * HEAD_D:(hh + 1) * HEAD_D]
        xm = jnp.where(rowt == 0, 0.0, pltpu.roll(x, 1, 0))
        xp = jnp.where(rowt == t - 1, 0.0, pltpu.roll(x, t - 1, 0))
        return xm * w[0:1] + x * w[1:2] + xp * w[2:3]

    ab = ab_ref[...]
    lane1 = lax.broadcasted_iota(jnp.int32, (1, LANES), 1)
    alog_l = jnp.zeros((1, LANES), F32)
    dtb_l = jnp.zeros((1, LANES), F32)
    for d in dirs:
        for hd in range(N_HEADS):
            alog_l = jnp.where(lane1 == d * N_HEADS + hd, alog_ref[d, hd], alog_l)
            dtb_l = jnp.where(lane1 == d * N_HEADS + hd, dtb_ref[d, hd], dtb_l)
    xs = ab + dtb_l
    log_a_all = (-LOG2E) * jnp.exp(alog_l) * (jnp.maximum(xs, 0.0) + jnp.log1p(jnp.exp(-jnp.abs(xs))))
    beta_all = jax.nn.sigmoid(ab)
    for hh in heads:
        h = pl.program_id(1) * hg + hh
        q = _silu(conv(cq_ref, 0, hh))
        q_s[hh] = q * lax.rsqrt(jnp.sum(q * q, axis=-1, keepdims=True) + EPS) * (HEAD_D ** -0.5)
        k = _silu(conv(ck_ref, 1, hh))
        k_s[hh] = k * lax.rsqrt(jnp.sum(k * k, axis=-1, keepdims=True) + EPS)
        v_s[hh] = _silu(conv(cv_ref, 2, hh))
        for d in dirs:
            la_col = jnp.sum(jnp.where(lane == d * N_HEADS + h, log_a_all, 0.0), axis=-1, keepdims=True)
            be_col = jnp.sum(jnp.where(lane == 2 * N_HEADS + d * N_HEADS + h, beta_all, 0.0), axis=-1,
                             keepdims=True)
            la_s[hh, d] = jnp.broadcast_to(la_col, (t, LANES))
            be_s[hh, d] = jnp.broadcast_to(be_col, (t, LANES))
    o_acc[...] = jnp.zeros(o_acc.shape, F32)

    def wy_span(sp, carry):
        r0 = pl.multiple_of(sp * span, span)
        ms, xs, dst = [], [], []
        for hh in heads:
            qsp = q_s[hh, pl.ds(r0, span), :]
            ksp = k_s[hh, pl.ds(r0, span), :]
            vsp = v_s[hh, pl.ds(r0, span), :]
            kb = ksp.astype(BF16)
            kk = _dot_nt(kb, kb)
            qk = _dot_nt(qsp.astype(BF16), kb)
            for d in dirs:
                be = be_s[hh, d, pl.ds(r0, span), :]
                g = _chunk_cumsum(la_s[hh, d, pl.ds(r0, span), :], d, c)
                gt = g.T
                gcol = jnp.concatenate([g] * (span // LANES), axis=1)
                grow = jnp.concatenate([gt] * (span // LANES), axis=0)
                incl = same & ((ci <= ri) if d == 0 else (ci >= ri))
                gamma = jnp.where(incl, jnp.exp2(jnp.minimum(gcol - grow, 0.0)), 0.0)
                bcol = jnp.concatenate([be] * (span // LANES), axis=1)
                m_sp = jnp.where(ci == ri, 0.0, bcol * kk * gamma)
                qkg = (qk * gamma).astype(BF16)
                eg = jnp.exp2(g)
                xv = be * vsp
                xk = be * eg * ksp
                qe_s[hh, d, pl.ds(r0, span), :] = (qsp * eg).astype(BF16)
                for j in range(cps):
                    lo = j * c
                    rj = pl.multiple_of(r0 + lo, c)
                    ms.append(m_sp[lo:lo + c, lo:lo + c])
                    xs.append(jnp.concatenate([xv[lo:lo + c], xk[lo:lo + c]], axis=1))
                    dst.append((hh, d, rj))
                    qk_s[hh, d, pl.ds(rj, c), 0:c] = qkg[lo:lo + c, lo:lo + c]
                    gj = g[lo:lo + c]
                    glast = gj[c - 1:c] if d == 0 else gj[0:1]
                    kd_s[hh, d, pl.ds(rj, c), :] = (ksp[lo:lo + c] * jnp.exp2(glast - gj)).astype(BF16)
                    dec_s[hh, d, pl.ds(sp * cps + j, 1), :] = jnp.exp2(glast)
        for (hh, d, rj), w in zip(dst, _tri_solve(ms, xs, c)):
            w_s[hh, d, pl.ds(rj, c), :] = w
        return carry

    if t == span:
        wy_span(0, 0)
    else:
        lax.fori_loop(0, t // span, wy_span, 0)

    chains = [(hh, d) for hh in heads for d in dirs]

    def scan_step(i, carry):
        cidx = (i, nchunk - 1 - i)
        r0 = [pl.multiple_of(ck * c, c) for ck in cidx]
        sb = [s.astype(BF16) for s in carry]
        ws = [w_s[hh, d, pl.ds(r0[d], c), :] for hh, d in chains]
        wks = [_dot(w[:, HEAD_D:].astype(BF16), s) for w, s in zip(ws, sb)]
        oq = [_dot(qe_s[hh, d, pl.ds(r0[d], c), :], s) for (hh, d), s in zip(chains, sb)]
        vnb = [(w[:, :HEAD_D] - wk).astype(BF16) for w, wk in zip(ws, wks)]
        new = [s * dec_s[hh, d, pl.ds(cidx[d], 1), :] + _dot_tn(kd_s[hh, d, pl.ds(r0[d], c), :], v)
               for (hh, d), s, v in zip(chains, carry, vnb)]
        for (hh, d), o, v in zip(chains, oq, vnb):
            o_acc[hh, pl.ds(r0[d], c), :] += o + _dot(qk_s[hh, d, pl.ds(r0[d], c), 0:c], v)
        return tuple(new)

    finals = lax.fori_loop(0, nchunk, scan_step, tuple(s0_ref[0, d, hh] for hh, d in chains),
                           unroll=nchunk <= 4)
    for (hh, d), s in zip(chains, finals):
        s_ref[0, d, hh] = s
    ys = []
    for hh in heads:
        sl = slice(hh * HEAD_D, (hh + 1) * HEAD_D)
        ys.append(_rms(o_acc[hh], g_ref[...]) * _silu(cg_ref[:, sl]))
    y = jnp.concatenate(ys, axis=1)
    if perm_bufs:
        _store_row_major(y_ref, y, t, stage)
    else:
        y_ref[...] = y.astype(y_ref.dtype)


def _gdn(p_main, p_ab, conv_w, a_log, dt_bias, norm_g, s0, row_blk0, nseq, t, y_prev, col0, s_prev=None,
         slot=0, nslots=1, col_major=False):
    c = GDN_CHUNK
    hg = GDN_HEADS_PER_STEP if t * GDN_HEADS_PER_STEP <= GDN_MAX_ROWS_PER_STEP else GDN_HEADS_PER_STEP // 2
    wid = hg * HEAD_D
    blk = BR_WIDTH // wid

    def col(base):
        return pl.BlockSpec((t, wid), lambda s, h: (row_blk0 + s, base * blk + h))

    st_spec = pl.BlockSpec((1, 2, hg, HEAD_D, HEAD_D), lambda s, h: (s, 0, h, 0, 0))
    smem = pl.BlockSpec(memory_space=pltpu.SMEM)
    seq_buf = pltpu.VMEM((hg, t, HEAD_D), F32)
    dir_f32 = pltpu.VMEM((hg, 2, t, LANES), F32)
    dir_bf16 = pltpu.VMEM((hg, 2, t, LANES), BF16)
    scratch = [seq_buf, seq_buf, seq_buf, dir_f32, dir_f32,
               pltpu.VMEM((hg, 2, t, 2 * HEAD_D), F32), dir_bf16, dir_bf16, dir_bf16,
               pltpu.VMEM((hg, 2, max(t // c, 8), LANES), F32), seq_buf]
    if col_major:
        scratch += [pltpu.VMEM((t, wid), F32)] * 4 + [pltpu.VMEM((t, LANES), F32)] * 2
    so_spec, so_shape = _state_out(nseq, slot, nslots, hg)
    return _shared_out_call(
        functools.partial(_gdn_kernel, t=t, c=c, span=GDN_SPAN, hg=hg), [y_prev, s_prev],
        [col(6), col(7), col(8), col(9),
         pl.BlockSpec((t, LANES), lambda s, h: (row_blk0 + s, 0)),
         pl.BlockSpec((3, 3, wid), lambda s, h: (0, 0, h)),
         smem, smem,
         pl.BlockSpec((1, HEAD_D), lambda s, h: (0, 0)),
         st_spec],
        (p_main, p_main, p_main, p_main, p_ab, conv_w, a_log, dt_bias, norm_g.reshape(1, HEAD_D), s0),
        grid=(nseq, N_HEADS // hg),
        out_specs=[pl.BlockSpec((t, wid), lambda s, h: (row_blk0 + s, col0 * LANES // wid + h)), so_spec],
        out_shape=[jax.ShapeDtypeStruct(y_prev.shape, y_prev.dtype), so_shape],
        scratch_shapes=scratch,
        compiler_params=_cp(("parallel", "parallel")),
        name="gdn",
    )


def _cmul(ar, ai, br, bi):
    return ar * br - ai * bi, ar * bi + ai * br


def _s5_kernel(u_ref, bb_ref, cc_ref, lam_ref, dsk_ref, s0_ref, y_ref, sf_ref, up_s, x_s, yp_s,
               *, tb, seg, segs_per_seq, zero_init, col_major):
    r_par = tb // seg
    half = S5_GB * S5_STATE
    if col_major:
        rows = tb // GRID_W
        assert segs_per_seq == r_par and seg % rows == 0

        def step_rows(r):
            return pl.ds((r % rows) * GRID_W + r // rows, r_par, stride=seg // rows)
    else:
        def step_rows(r):
            return pl.ds(r, r_par, stride=seg)

    for r in range(seg):
        up_s[r * r_par:(r + 1) * r_par, :] = u_ref[step_rows(r), :]
    ub = up_s[...].astype(BF16)
    sf_ref[...] = jnp.zeros(sf_ref.shape, F32)
    nseq = r_par // segs_per_seq
    lam = [(lam_ref[0, d, 0:1, :], lam_ref[0, d, 1:2, :]) for d in (0, 1)]
    for d in (0, 1):
        x_s[d] = _dot(ub, bb_ref[0, d])

    def offsets(i):
        return [pl.multiple_of(r * r_par, r_par) for r in (i, seg - 1 - i)]

    def local_step(i, carry):
        off = offsets(i)
        new = []
        for d in (0, 1):
            (lre, lim), (hre, him) = lam[d], carry[d]
            nre = lre * hre - lim * him + x_s[d, pl.ds(off[d], r_par), 0:half]
            nim = lre * him + lim * hre + x_s[d, pl.ds(off[d], r_par), half:2 * half]
            x_s[d, pl.ds(off[d], r_par), 0:half] = nre
            x_s[d, pl.ds(off[d], r_par), half:2 * half] = nim
            new.append((nre, nim))
        return tuple(new)

    zero = jnp.zeros((r_par, half), F32)
    lax.fori_loop(0, seg, local_step, ((zero, zero), (zero, zero)), unroll=4)

    end_rows = ((seg - 1) * r_par, 0)
    needs_carry = not (zero_init and segs_per_seq == 1)
    hins = []
    for d in (0, 1):
        if not needs_carry:
            break
        lre, lim = lam[d]
        pre, pim = lre, lim
        n = 1
        while n < seg:
            pre, pim = _cmul(pre, pim, pre, pim)
            n *= 2
        end_row = end_rows[d]
        ere = x_s[d, end_row:end_row + r_par, 0:half]
        eim = x_s[d, end_row:end_row + r_par, half:2 * half]
        hin = [None] * r_par
        order = range(r_par) if d == 0 else range(r_par - 1, -1, -1)
        for j in order:
            sq, pos = divmod(j, segs_per_seq)
            first = pos == (0 if d == 0 else segs_per_seq - 1)
            if first:
                hin[j] = (s0_ref[0, 0, d, sq:sq + 1, 0:half], s0_ref[0, 0, d, sq:sq + 1, half:2 * half])
            else:
                pj = j - 1 if d == 0 else j + 1
                cre, cim = _cmul(pre, pim, hin[pj][0], hin[pj][1])
                hin[j] = (cre + ere[pj:pj + 1], cim + eim[pj:pj + 1])
        hins.append((jnp.concatenate([hj[0] for hj in hin], axis=0),
                     jnp.concatenate([hj[1] for hj in hin], axis=0)))

    def carry_step(i, pws):
        off = offsets(i)
        new = []
        for d in (0, 1):
            (lre, lim), (wre, wim) = lam[d], pws[d]
            are, aim = _cmul(wre, wim, hins[d][0], hins[d][1])
            x_s[d, pl.ds(off[d], r_par), 0:half] += are
            x_s[d, pl.ds(off[d], r_par), half:2 * half] += aim
            new.append(_cmul(wre, wim, lre, lim))
        return tuple(new)

    if needs_carry:
        lax.fori_loop(0, seg, carry_step, (lam[0], lam[1]), unroll=4)

    for d in (0, 1):
        for sq in range(nseq):
            j = (sq + 1) * segs_per_seq - 1 if d == 0 else sq * segs_per_seq
            row = end_rows[d] + j
            sf_ref[0, 0, d, sq:sq + 1, :] = x_s[d, row:row + 1, :]
    yp_s[...] = (_dot(x_s[0].astype(BF16), cc_ref[0, 0]) + _dot(x_s[1].astype(BF16), cc_ref[0, 1])
                 + up_s[...] * dsk_ref[...])
    for r in range(seg):
        up_s[step_rows(r), :] = yp_s[r * r_par:(r + 1) * r_par, :]
    y_ref[...] = up_s[...].astype(y_ref.dtype)


def _s5(p_main, bb, cc, lam, dskip, s0, row_blk0, nblk, tb, t, seg, zero_init, y_prev, col0, col_major=False):
    ngb = S5_GROUPS // S5_GB
    w2 = 2 * S5_GB * S5_STATE
    return _shared_out_call(
        functools.partial(_s5_kernel, tb=tb, seg=seg, segs_per_seq=t // seg, zero_init=zero_init,
                          col_major=col_major), [y_prev, None],
        [
            pl.BlockSpec((tb, LANES), lambda i, gb: (row_blk0 + i, 20 + gb)),
            pl.BlockSpec((1, 2, LANES, w2), lambda i, gb: (gb, 0, 0, 0)),
            pl.BlockSpec((1, 2, w2, LANES), lambda i, gb: (gb, 0, 0, 0)),
            pl.BlockSpec((1, 2, 2, w2 // 2), lambda i, gb: (gb, 0, 0, 0)),
            pl.BlockSpec((1, LANES), lambda i, gb: (0, gb)),
            pl.BlockSpec((1, 1, 2, 8, w2), lambda i, gb: (i, gb, 0, 0, 0)),
        ],
        (p_main, bb, cc, lam, dskip.reshape(1, BR_WIDTH), s0),
        grid=(nblk, ngb),
        out_specs=[pl.BlockSpec((tb, LANES), lambda i, gb: (row_blk0 + i, col0 + gb)),
                   pl.BlockSpec((1, 1, 2, 8, w2), lambda i, gb: (i, gb, 0, 0, 0))],
        out_shape=[jax.ShapeDtypeStruct(y_prev.shape, y_prev.dtype),
                   jax.ShapeDtypeStruct((nblk, ngb, 2, 8, w2), F32)],
        scratch_shapes=[pltpu.VMEM((tb, LANES), F32), pltpu.VMEM((2, tb, w2), F32), pltpu.VMEM((tb, LANES), F32)],
        compiler_params=_cp(("parallel", "parallel")),
        name="s5",
    )


def _s5_params(a_re, a_im, log_dt, b_re, b_im, c_re, c_im):
    dt = jnp.exp(log_dt)[..., None]
    mag = jnp.exp(dt * a_re)
    lam_re = mag * jnp.cos(dt * a_im)
    lam_im = mag * jnp.sin(dt * a_im)
    den = a_re * a_re + a_im * a_im
    co_re = ((lam_re - 1.0) * a_re + lam_im * a_im) / den
    co_im = (lam_im * a_re - (lam_re - 1.0) * a_im) / den
    bb_re = co_re[..., None] * b_re - co_im[..., None] * b_im
    bb_im = co_re[..., None] * b_im + co_im[..., None] * b_re
    ngb = S5_GROUPS // S5_GB
    eye = jnp.eye(S5_GB, dtype=F32)

    def pack_in(w):
        w = w.reshape(2, ngb, S5_GB, S5_STATE, S5_GROUP)
        blk = jnp.einsum('dbgpi,gh->bdgihp', w, eye)
        return blk.reshape(ngb, 2, S5_GB * S5_GROUP, S5_GB * S5_STATE)

    def pack_out(w):
        w = w.reshape(2, ngb, S5_GB, S5_GROUP, S5_STATE)
        blk = jnp.einsum('dbgip,gh->bdgphi', w, eye)
        return blk.reshape(ngb, 2, S5_GB * S5_STATE, S5_GB * S5_GROUP)

    bb = jnp.concatenate([pack_in(bb_re), pack_in(bb_im)], axis=-1).astype(BF16)
    cc = jnp.concatenate([pack_out(c_re), -pack_out(c_im)], axis=-2).astype(BF16)
    lam = jnp.stack([lam_re, lam_im], axis=1)
    lam = lam.reshape(2, 2, ngb, S5_GB * S5_STATE).transpose(2, 0, 1, 3)
    return bb, cc, lam


def _residual_epilogue(y, g_ref, x_old, mv_ref, xo_ref, h_ref):
    mv = mv_ref[...]
    x = x_old + mv[0:1] * _rms(y, g_ref[...])
    xo_ref[...] = x
    h_ref[...] = (_rms(x, mv[3:4]) * (1.0 + mv[1:2]) + mv[2:3]).astype(h_ref.dtype)


def _group_of_tile(tm, n_prompt, seq_s):
    def group(i):
        return jnp.where(i * tm < n_prompt, 0, 1 + (i * tm - n_prompt) // seq_s)
    return group


def _merge_kernel(ya_ref, yb_ref, yc_ref, mg_ref, wbr_ref, wout_ref, wglu_ref, bglu_ref, g_ref, *rest, na_tiles):
    if na_tiles:
        x_ref, xb_ref, mv_ref, xo_ref, h_ref, wbr_s, wout_s, wglu_s = rest
    else:
        x_ref, mv_ref, xo_ref, h_ref, wbr_s, wout_s, wglu_s = rest

    @pl.when(pl.program_id(0) == 0)
    def _():
        wbr_s[...] = wbr_ref[...].astype(BF16)
        wout_s[...] = wout_ref[...].astype(BF16)
        wglu_s[...] = wglu_ref[...].astype(BF16)

    d = wout_s.shape[0]
    z = _dot(yb_ref[...], wglu_s[...]) + bglu_ref[...]
    half = z.shape[1] // 2
    branches = (ya_ref[...], (z[:, :half] * jax.nn.sigmoid(z[:, half:])).astype(BF16), yc_ref[...])
    acc = None
    for n, br in enumerate(branches):
        proj = _dot(br.astype(BF16), wbr_s[n])
        term = proj * jax.nn.sigmoid(mg_ref[:, n * d:(n + 1) * d].astype(F32))
        acc = term if acc is None else acc + term
    y = _dot(acc.astype(BF16), wout_s[...])
    x_old = _pick_source(x_ref, xb_ref, na_tiles) if na_tiles else x_ref[...]
    _residual_epilogue(y, g_ref, x_old, mv_ref, xo_ref, h_ref)


def _merge(branches, mg, w_branch, w_out, glu_w, glu_b, g, l, x, modv, n_prompt, seq_s, h_dtype):
    n = branches.shape[0]
    d = w_out.shape[-1]
    tm = TM
    group = _group_of_tile(tm, n_prompt, seq_s)
    tile = pl.BlockSpec((tm, d), lambda i: (i, 0))
    xs = x if isinstance(x, tuple) else (x,)
    na_tiles = xs[0].shape[0] // tm if len(xs) == 2 else 0
    x_specs = _two_source_specs(tm, d, na_tiles) if na_tiles else [tile]
    return pl.pallas_call(
        functools.partial(_merge_kernel, na_tiles=na_tiles),
        grid=(n // tm,),
        in_specs=[pl.BlockSpec((tm, BR_WIDTH), lambda i: (i, 0)),
                  pl.BlockSpec((tm, BR_WIDTH), lambda i: (i, 1)),
                  pl.BlockSpec((tm, BR_WIDTH), lambda i: (i, 2)),
                  pl.BlockSpec((tm, 3 * d), lambda i: (i, 0)),
                  pl.BlockSpec((None, 3, BR_WIDTH, d), lambda i: (l, 0, 0, 0), pipeline_mode=pl.Buffered(1)),
                  pl.BlockSpec((None, d, d), lambda i: (l, 0, 0), pipeline_mode=pl.Buffered(1)),
                  pl.BlockSpec((None, BR_WIDTH, 2 * BR_WIDTH), lambda i: (l, 0, 0), pipeline_mode=pl.Buffered(1)),
                  pl.BlockSpec((None, 1, 2 * BR_WIDTH), lambda i: (l, 0, 0)),
                  pl.BlockSpec((1, d), lambda i: (0, 0)),
                  *x_specs,
                  pl.BlockSpec((None, 8, d), lambda i: (group(i), 0, 0))],
        out_specs=[tile, tile],
        out_shape=[jax.ShapeDtypeStruct((n, d), F32), jax.ShapeDtypeStruct((n, d), h_dtype)],
        scratch_shapes=[pltpu.VMEM((3, BR_WIDTH, d), BF16), pltpu.VMEM((d, d), BF16),
                        pltpu.VMEM((BR_WIDTH, 2 * BR_WIDTH), BF16)],
        compiler_params=_cp(("arbitrary",)),
        name="merge",
    )(branches, branches, branches, mg, w_branch, w_out, glu_w, glu_b.reshape(glu_b.shape[0], 1, 2 * BR_WIDTH),
      g.reshape(1, d), *xs, modv)


def _swiglu_kernel(x_ref, wg_ref, wu_ref, o_ref, wg_s, wu_s):
    @pl.when(pl.program_id(1) == 0)
    def _():
        wg_s[...] = wg_ref[...].astype(BF16)
        wu_s[...] = wu_ref[...].astype(BF16)

    x = x_ref[...]
    o_ref[...] = (_silu(_dot(x, wg_s[...])) * _dot(x, wu_s[...])).astype(BF16)


def _swiglu_up(x, wg, wu, e, f):
    n, d = x.shape
    nf = wg.shape[-1] // f
    wspec = pl.BlockSpec((None, d, f), lambda j, i: (e, 0, j))
    return pl.pallas_call(
        _swiglu_kernel,
        grid=(nf, n // TM),
        in_specs=[pl.BlockSpec((TM, d), lambda j, i: (i, 0)), wspec, wspec],
        out_specs=pl.BlockSpec((TM, f), lambda j, i: (i, j)),
        out_shape=jax.ShapeDtypeStruct((n, nf * f), BF16),
        scratch_shapes=[pltpu.VMEM((d, f), BF16), pltpu.VMEM((d, f), BF16)],
        compiler_params=_cp(("arbitrary", "arbitrary")),
        name="swiglu_up",
    )(x, wg, wu)


def _down_kernel(hid_ref, w_ref, g_ref, x_ref, mv_ref, xo_ref, h_ref, w_s):
    @pl.when(pl.program_id(0) == 0)
    def _():
        w_s[...] = w_ref[...].astype(BF16)

    _residual_epilogue(_dot(hid_ref[...], w_s[...]), g_ref, x_ref[...], mv_ref, xo_ref, h_ref)


def _down_res(hid, wd, e, g, x, modv, n_prompt, seq_s, h_dtype):
    n, f = hid.shape
    d = wd.shape[-1]
    group = _group_of_tile(TM, n_prompt, seq_s)
    tile = pl.BlockSpec((TM, d), lambda i: (i, 0))
    return pl.pallas_call(
        _down_kernel,
        grid=(n // TM,),
        in_specs=[pl.BlockSpec((TM, f), lambda i: (i, 0)),
                  pl.BlockSpec((None, f, d), lambda i: (e, 0, 0)),
                  pl.BlockSpec((1, d), lambda i: (0, 0)),
                  tile,
                  pl.BlockSpec((None, 8, d), lambda i: (group(i), 0, 0))],
        out_specs=[tile, tile],
        out_shape=[jax.ShapeDtypeStruct((n, d), F32), jax.ShapeDtypeStruct((n, d), h_dtype)],
        scratch_shapes=[pltpu.VMEM((f, d), BF16)],
        compiler_params=_cp(("arbitrary",)),
        name="down_res",
    )(hid, wd, g.reshape(1, d), x, modv)


def _router_kernel(x_ref, w_ref, aux_ref, cnt_ref, base_s, *, ne):
    @pl.when(pl.program_id(0) == 0)
    def _():
        base_s[...] = jnp.zeros(base_s.shape, F32)

    w = w_ref[...]
    w_hi = w.astype(BF16)
    w_lo = (w - w_hi.astype(F32)).astype(BF16)
    x = x_ref[...].astype(BF16)
    logits = _dot(x, w_hi) + _dot(x, w_lo)
    tm = logits.shape[0]
    lane = lax.broadcasted_iota(jnp.int32, logits.shape, 1).astype(F32)
    neg = jnp.float32(-jnp.inf)
    logits = jnp.where(lane < ne, logits, neg)
    m1 = jnp.max(logits, axis=-1, keepdims=True)
    i1 = jnp.min(jnp.where(logits == m1, lane, float(LANES)), axis=-1, keepdims=True)
    rest = jnp.where(lane == i1, neg, logits)
    m2 = jnp.max(rest, axis=-1, keepdims=True)
    i2 = jnp.min(jnp.where(rest == m2, lane, float(LANES)), axis=-1, keepdims=True)
    e2 = jnp.exp(m2 - m1)
    w1 = 1.0 / (1.0 + e2)
    w2 = e2 / (1.0 + e2)
    sel = jnp.where((lane == i1) | (lane == i2), 1.0, 0.0)
    ri = lax.broadcasted_iota(jnp.int32, (tm, tm), 0)
    ci = lax.broadcasted_iota(jnp.int32, (tm, tm), 1)
    before = jnp.where(ci < ri, 1.0, 0.0).astype(BF16)
    rank = _dot(before, sel.astype(BF16)) + base_s[0:1, :]
    r1 = jnp.sum(jnp.where(lane == i1, rank, 0.0), axis=-1, keepdims=True)
    r2 = jnp.sum(jnp.where(lane == i2, rank, 0.0), axis=-1, keepdims=True)
    aux = jnp.zeros(logits.shape, F32)
    for k, v in enumerate((i1, i2, r1, r2, w1, w2)):
        aux = jnp.where(lane == float(k), v, aux)
    aux_ref[...] = aux
    base_s[...] = base_s[...] + jnp.sum(sel, axis=0, keepdims=True)
    cnt_ref[...] = base_s[...]


def _router(x, w_pad, ne):
    n, d = x.shape
    return pl.pallas_call(
        functools.partial(_router_kernel, ne=ne),
        grid=(n // TM,),
        in_specs=[pl.BlockSpec((TM, d), lambda i: (i, 0)),
                  pl.BlockSpec((d, LANES), lambda i: (0, 0))],
        out_specs=[pl.BlockSpec((TM, LANES), lambda i: (i, 0)),
                   pl.BlockSpec((8, LANES), lambda i: (0, 0))],
        out_shape=[jax.ShapeDtypeStruct((n, LANES), F32), jax.ShapeDtypeStruct((8, LANES), F32)],
        scratch_shapes=[pltpu.VMEM((8, LANES), F32)],
        compiler_params=_cp(("arbitrary",)),
        name="router",
    )(x, w_pad)


def _moe_plan(aux, cnt, n, ne, tile, max_tiles):
    e = aux[:, 0:2].astype(jnp.int32)
    r = aux[:, 2:4].astype(jnp.int32)
    c = cnt[0, :ne].astype(jnp.int32)
    pc = (c + tile - 1) // tile * tile
    ends = jnp.cumsum(pc)
    slot = ((ends - pc)[e] + r).reshape(-1)
    tok = jnp.broadcast_to(jnp.arange(n, dtype=F32)[:, None], (n, 2)).reshape(-1)
    m = max_tiles * tile
    pad = jnp.broadcast_to(jnp.array([float(n), 0.0], F32), (m, 2))
    table = pad.at[slot].set(jnp.stack([tok, aux[:, 4:6].reshape(-1)], axis=1))
    slot_tok = table[:, 0].astype(jnp.int32)
    slot_gate = table[:, 1]
    tstart = jnp.arange(max_tiles, dtype=jnp.int32) * tile
    tile_e = jnp.minimum(jnp.sum(tstart[:, None] >= ends[None, :], axis=1), ne - 1).astype(jnp.int32)
    tile_valid = (tstart < ends[-1]).astype(jnp.int32)
    return slot_tok.reshape(max_tiles, 1, tile), slot_gate.reshape(max_tiles, 1, tile), tile_e, tile_valid


def _expert_changed(t, te_ref):
    return jnp.logical_or(t == 0, te_ref[t] != te_ref[jnp.maximum(t - 1, 0)])


def _moe_up_kernel(te_ref, tv_ref, idx_ref, nxt_ref, h_ref, wg_ref, wu_ref, o_ref, xbuf, wg_s, wu_s, *, tile):
    t = pl.program_id(0)
    valid = tv_ref[t] == 1
    last = h_ref.shape[0] - 1
    cur = t % 2

    def gather_row(src_idx_ref, buf, j):
        xbuf[buf, pl.ds(j, 1), :] = h_ref[pl.ds(jnp.minimum(src_idx_ref[0, 0, j], last), 1), :]

    @pl.when(t == 0)
    def _():
        def first(j, carry):
            gather_row(idx_ref, 0, j)
            return carry

        lax.fori_loop(0, tile, first, 0, unroll=8)

    @pl.when(jnp.logical_and(valid, _expert_changed(t, te_ref)))
    def _():
        wg_s[...] = wg_ref[...].astype(BF16)
        wu_s[...] = wu_ref[...].astype(BF16)

    @pl.when(valid)
    def _():
        x = xbuf[cur].astype(BF16)
        o_ref[...] = (_silu(_dot(x, wg_s[...])) * _dot(x, wu_s[...])).astype(BF16)
        for j in range(tile):
            gather_row(nxt_ref, 1 - cur, j)

    @pl.when(jnp.logical_not(valid))
    def _():
        o_ref[...] = jnp.zeros(o_ref.shape, BF16)


def _moe_up(h, slot_tok, tile_e, tile_valid, wg, wu, e_layer, tile):
    max_tiles = slot_tok.shape[0]
    _, _, d, f = wg.shape
    wspec = pl.BlockSpec((None, None, d, f), lambda t, te, tv: (e_layer, te[t], 0, 0))
    grid_spec = pltpu.PrefetchScalarGridSpec(
        num_scalar_prefetch=2,
        grid=(max_tiles,),
        in_specs=[pl.BlockSpec((1, 1, tile), lambda t, te, tv: (t, 0, 0), memory_space=pltpu.SMEM),
                  pl.BlockSpec((1, 1, tile), lambda t, te, tv: (jnp.minimum(t + 1, max_tiles - 1), 0, 0),
                               memory_space=pltpu.SMEM),
                  pl.BlockSpec(h.shape, lambda t, te, tv: (0, 0), pipeline_mode=pl.Buffered(1)),
                  wspec, wspec],
        out_specs=pl.BlockSpec((tile, f), lambda t, te, tv: (t, 0)),
        scratch_shapes=[pltpu.VMEM((2, tile, d), F32),
                        pltpu.VMEM((d, f), BF16), pltpu.VMEM((d, f), BF16)],
    )
    return pl.pallas_call(
        functools.partial(_moe_up_kernel, tile=tile),
        grid_spec=grid_spec,
        out_shape=jax.ShapeDtypeStruct((max_tiles * tile, f), BF16),
        compiler_params=_cp(("arbitrary",), vmem=MOE_UP_VMEM_LIMIT),
        name="moe_up",
    )(tile_e, tile_valid, slot_tok, slot_tok, h, wg, wu)


def _moe_down_kernel(te_ref, tv_ref, idx_ref, gt_ref, hid_ref, wd_ref, g_ref, o_ref, acc, ybuf, tbuf, wd_s,
                     *, tile, n_tiles, tm_out):
    t = pl.program_id(0)
    tt = jnp.minimum(t, n_tiles - 1)

    @pl.when(t == 0)
    def _():
        acc[...] = jnp.zeros(acc.shape, F32)

    is_tile = jnp.logical_and(t < n_tiles, tv_ref[tt] == 1)

    @pl.when(jnp.logical_and(is_tile, _expert_changed(tt, te_ref)))
    def _():
        wd_s[...] = wd_ref[...].astype(BF16)

    @pl.when(is_tile)
    def _():
        for j in range(tile):
            tbuf[j:j + 1, :] = acc[pl.ds(idx_ref[0, 0, j], 1), :]
        ybuf[...] = _dot(hid_ref[...], wd_s[...])

        def put(j, carry):
            acc[pl.ds(idx_ref[0, 0, j], 1), :] = tbuf[pl.ds(j, 1), :] + gt_ref[0, 0, j] * ybuf[pl.ds(j, 1), :]
            return carry

        lax.fori_loop(0, tile, put, 0, unroll=8)

    @pl.when(t >= n_tiles)
    def _():
        r0 = pl.multiple_of((t - n_tiles) * tm_out, tm_out)
        o_ref[...] = _rms(acc[pl.ds(r0, tm_out), :], g_ref[...])


def _moe_down(hid, slot_tok, slot_gate, tile_e, tile_valid, wd, e_layer, g, n, tile):
    n_tiles = slot_tok.shape[0]
    _, _, f, d = wd.shape
    tm_out = TM
    n_flush = n // tm_out

    def tix(t):
        return jnp.minimum(t, n_tiles - 1)

    smem_tile = pl.BlockSpec((1, 1, tile), lambda t, te, tv: (tix(t), 0, 0), memory_space=pltpu.SMEM)
    grid_spec = pltpu.PrefetchScalarGridSpec(
        num_scalar_prefetch=2,
        grid=(n_tiles + n_flush,),
        in_specs=[smem_tile, smem_tile,
                  pl.BlockSpec((tile, f), lambda t, te, tv: (tix(t), 0)),
                  pl.BlockSpec((None, None, f, d), lambda t, te, tv: (e_layer, te[tix(t)], 0, 0)),
                  pl.BlockSpec((1, d), lambda t, te, tv: (0, 0))],
        out_specs=pl.BlockSpec((tm_out, d), lambda t, te, tv: (jnp.maximum(t - n_tiles, 0), 0)),
        scratch_shapes=[pltpu.VMEM((n + 8, d), F32), pltpu.VMEM((tile, d), F32), pltpu.VMEM((tile, d), F32),
                        pltpu.VMEM((f, d), BF16)],
    )
    return pl.pallas_call(
        functools.partial(_moe_down_kernel, tile=tile, n_tiles=n_tiles, tm_out=tm_out),
        grid_spec=grid_spec,
        out_shape=jax.ShapeDtypeStruct((n, d), F32),
        compiler_params=_cp(("arbitrary",)),
        name="moe_down",
    )(tile_e, tile_valid, slot_tok, slot_gate, hid, wd, g.reshape(1, d))


def kernel(x_prompt, x_sample, state_hgrn, state_s5, state_gdn, c, c_ctx, ada_w, ada_b, norm_g, w_in, hgrn_lb, hgrn_norm, s5_a_re, s5_a_im, s5_log_dt, s5_b_re, s5_b_im, s5_c_re, s5_c_im, s5_d, s5_glu_w, s5_glu_b, gdn_conv, gdn_a_log, gdn_dt_bias, gdn_norm, w_branch, w_out, ffn_wg, ffn_wu, ffn_wd, moe_router, moe_wg, moe_wu, moe_wd):
    bp, tp, d = x_prompt.shape
    bs, ts, _ = x_sample.shape
    depth = w_in.shape[0]
    n_p, n_s = bp * tp, bs * ts
    ne = moe_router.shape[-1]
    main_cols = 10 * BR_WIDTH
    ab_cols = 4 * N_HEADS

    c8 = jnp.zeros((8, d), F32).at[0].set(c_ctx).at[1:1 + bs].set(c)
    mod_all = _ada_mod(c8, ada_w, ada_b)

    lb_soft = jax.nn.softmax(hgrn_lb.astype(F32), axis=0)
    lb_all = jnp.cumsum(lb_soft, axis=0) - lb_soft[:1]

    mods = mod_all[:, :1 + bs].reshape(depth, 1 + bs, 6, d)
    zero_row = jnp.zeros((1 + bs, d), F32)

    def modv(gate, scale, shift, g):
        rows = [zero_row if sel is None else mods[sel[0], :, sel[1]] for sel in (gate, scale, shift)]
        rows.append(zero_row if g is None else jnp.broadcast_to(g, (1 + bs, d)))
        return jnp.stack(rows + [zero_row] * 4, axis=1)

    x = (x_prompt.reshape(n_p, d), x_sample.reshape(n_s, d))
    (h,) = _resnorm(x, None, modv(None, (0, 1), (0, 0), norm_g[0, 0]), n_p, ts, True)

    z_hgrn = jnp.zeros((bp, 2, N_HEADS, HEAD_D, HEAD_D), F32)
    ngb = S5_GROUPS // S5_GB
    w2 = 2 * S5_GB * S5_STATE
    st_s5 = []
    y_zero = jnp.zeros((n_p + n_s, 3 * BR_WIDTH), BF16)
    new_hgrn = new_gdn = jnp.zeros((bp, depth, 2, N_HEADS, HEAD_D, HEAD_D), F32)
    w_in_t = jnp.swapaxes(w_in, 1, 2)
    s5_tb = 1024

    for l in range(depth):
        col_major = l % 2 == 1
        p_main = _mm_t(h, w_in_t, l, main_cols, 2 * BR_WIDTH, "w_in_main", tm=2048)
        w_ab = jnp.pad(w_in_t[l, main_cols:main_cols + ab_cols], ((0, LANES - ab_cols), (0, 0)))
        p_ab = _mm_t(h, w_ab, None, LANES, LANES, "w_in_ab", tm=1024)
        p_mg = _mm_t(h, w_in_t[l, main_cols + ab_cols:], None, 3 * d, 2 * BR_WIDTH, "w_in_merge", tm=2048,
                     out_dtype=BF16)

        blk = BR_WIDTH // LANES
        br = y_zero

        br, new_hgrn = _hgrn(p_main, lb_all[l], hgrn_norm[l], z_hgrn, 0, bp, tp, br, 0, new_hgrn, l, depth)
        br, _ = _hgrn(p_main, lb_all[l], hgrn_norm[l], state_hgrn[:, l].astype(F32), n_p // ts, bs, ts, br, 0,
                      col_major=col_major)

        bb, cc, lam = _s5_params(s5_a_re[l], s5_a_im[l], s5_log_dt[l], s5_b_re[l], s5_b_im[l],
                                 s5_c_re[l], s5_c_im[l])
        seq_per_blk = 8
        tb_p = seq_per_blk * tp
        nblk_p = n_p // tb_p
        br, sb = _s5(p_main, bb, cc, lam, s5_d[l], jnp.zeros((nblk_p, ngb, 2, 8, w2), F32),
                     0, nblk_p, tb_p, tp, tp, True, br, blk)
        s0s = state_s5[:, l].astype(F32)
        s0s = s0s.reshape(bs, 2, ngb, S5_GB * S5_STATE, 2).transpose(0, 2, 1, 4, 3).reshape(bs, ngb, 2, 1, w2)
        s0s = jnp.pad(s0s, ((0, 0), (0, 0), (0, 0), (0, 7), (0, 0)))
        br, _ = _s5(p_main, bb, cc, lam, s5_d[l], s0s, n_p // s5_tb, n_s // s5_tb, s5_tb, ts, S5_SEG, False, br, blk,
                    col_major=col_major)
        sb = sb[:, :, :, :seq_per_blk].reshape(nblk_p, ngb, 2, seq_per_blk, 2, S5_GB, S5_STATE)
        sb = sb.transpose(0, 3, 2, 1, 5, 6, 4).reshape(bp, 2, S5_GROUPS, S5_STATE, 2)
        st_s5.append(sb)

        br, new_gdn = _gdn(p_main, p_ab, gdn_conv[l], gdn_a_log[l], gdn_dt_bias[l], gdn_norm[l], z_hgrn, 0, bp, tp,
                           br, 2 * blk, new_gdn, l, depth)
        br, _ = _gdn(p_main, p_ab, gdn_conv[l], gdn_a_log[l], gdn_dt_bias[l], gdn_norm[l],
                     state_gdn[:, l].astype(F32), n_p // ts, bs, ts, br, 2 * blk, col_major=col_major)

        x, h2 = _merge(br, p_mg, w_branch, w_out, s5_glu_w, s5_glu_b, norm_g[l, 1], l, x,
                       modv((l, 2), (l, 4), (l, 3), norm_g[l, 2]), n_p, ts, BF16 if l % 2 == 0 else F32)
        last = l + 1 == depth
        next_mod = modv((l, 5), None, None, None) if last else modv((l, 5), (l + 1, 1), (l + 1, 0), norm_g[l + 1, 0])

        if l % 2 == 0:
            e = l // 2
            hid = _swiglu_up(h2, ffn_wg, ffn_wu, e, ffn_wg.shape[-1] // 2)
            x, h = _down_res(hid, ffn_wd, e, norm_g[l, 3], x, next_mod, n_p, ts, BF16)
            if last:
                y_prompt, y_sample = x[:n_p], x[n_p:]
            continue
        else:
            e = l // 2
            n_tok = n_p + n_s
            max_tiles = 2 * n_tok // MOE_TILE + ne
            aux, cnt = _router(h2, jnp.pad(moe_router[e], ((0, 0), (0, LANES - ne))), ne)
            slot_tok, slot_gate, tile_e, tile_valid = _moe_plan(aux, cnt, n_tok, ne, MOE_TILE, max_tiles)
            hid = _moe_up(h2, slot_tok, tile_e, tile_valid, moe_wg, moe_wu, e, MOE_TILE)
            r_f = _moe_down(hid, slot_tok, slot_gate, tile_e, tile_valid, moe_wd, e, norm_g[l, 3], n_tok, MOE_TILE)

        if not last:
            x, h = _resnorm(x, r_f, next_mod, n_p, ts, True)
        else:
            (y_prompt,) = _resnorm(x, r_f, next_mod, n_p, ts, False, rows=(0, n_p))
            (y_sample,) = _resnorm(x, r_f, next_mod, n_p, ts, False, rows=(n_p, n_s))

    y_prompt = y_prompt.reshape(bp, tp, d)
    y_sample = y_sample.reshape(bs, ts, d)
    new_s5 = jnp.stack(st_s5, axis=1).astype(x_prompt.dtype)
    return (y_prompt, y_sample, new_hgrn.astype(x_prompt.dtype), new_s5, new_gdn.astype(x_prompt.dtype))
```

```python
import functools
import math

import jax
import jax.numpy as jnp
from jax import lax
from jax.experimental import pallas as pl
from jax.experimental.pallas import tpu as pltpu

F32 = jnp.float32
BF16 = jnp.bfloat16

EPS = 1e-6
F_FLOOR = 1e-6
LOG2E = 1.4426950408889634
GRID_W = 64
HEAD_D = 128
N_HEADS = 4
BR_WIDTH = 512
S5_GROUPS = 32
S5_GROUP = 16
S5_STATE = 64
LANES = 128
V7X_VMEM_BYTES = 64 * 1024 * 1024
VMEM_LIMIT = V7X_VMEM_BYTES - 8 * 1024 * 1024
MOE_UP_VMEM_LIMIT = V7X_VMEM_BYTES - 3 * 1024 * 1024
S5_GB = 8
S5_SEG = 64
GLA_CHUNK = 256
GDN_CHUNK = 64
GDN_SPAN = 256
GDN_HEADS_PER_STEP = 4
GDN_MAX_ROWS_PER_STEP = 2048
TM = 512
MOE_TILE = 256


def _cp(sem, vmem=VMEM_LIMIT):
    return pltpu.CompilerParams(dimension_semantics=sem, vmem_limit_bytes=vmem)


def _dot(a, b):
    return jnp.dot(a, b, preferred_element_type=F32)


def _dot_nt(a, b):
    return lax.dot_general(a, b, (((1,), (1,)), ((), ())), preferred_element_type=F32)


def _dot_tn(a, b):
    return lax.dot_general(a, b, (((0,), (0,)), ((), ())), preferred_element_type=F32)


def _silu(x):
    return x * jax.nn.sigmoid(x)


def _rms(x, g):
    return x * lax.rsqrt(jnp.mean(x * x, axis=-1, keepdims=True) + EPS) * g


def _chunk_cumsum(x, d, c):
    n = x.shape[0]
    pos = lax.broadcasted_iota(jnp.int32, x.shape, 0) & (c - 1)
    s = 1
    while s < c:
        if d == 0:
            x = x + jnp.where(pos >= s, pltpu.roll(x, s, 0), 0.0)
        else:
            x = x + jnp.where(pos < c - s, pltpu.roll(x, n - s, 0), 0.0)
        s *= 2
    return x


def _ada_kernel(c_ref, w_ref, b_ref, o_ref):
    c = c_ref[...]
    a = _silu(c).astype(BF16)
    o_ref[...] = _dot(a, w_ref[...].astype(BF16)) + b_ref[...]


def _ada_mod(c8, ada_w, ada_b):
    depth, d, n6 = ada_w.shape
    tn = 1536
    return pl.pallas_call(
        _ada_kernel,
        grid=(depth, n6 // tn),
        in_specs=[
            pl.BlockSpec((8, d), lambda l, j: (0, 0)),
            pl.BlockSpec((None, d, tn), lambda l, j: (l, 0, j)),
            pl.BlockSpec((None, 1, tn), lambda l, j: (l, 0, j)),
        ],
        out_specs=pl.BlockSpec((None, 8, tn), lambda l, j: (l, 0, j)),
        out_shape=jax.ShapeDtypeStruct((depth, 8, n6), F32),
        compiler_params=_cp(("parallel", "parallel")),
        name="ada_mod",
    )(c8, ada_w, ada_b.reshape(depth, 1, n6))


def _two_source_specs(tm, d, na_tiles):
    return [pl.BlockSpec((tm, d), lambda i: (jnp.minimum(i, na_tiles - 1), 0)),
            pl.BlockSpec((tm, d), lambda i: (jnp.maximum(i - na_tiles, 0), 0))]


def _pick_source(xa_ref, xb_ref, na_tiles):
    return jnp.where(pl.program_id(0) < na_tiles, xa_ref[...], xb_ref[...])


def _resnorm_kernel(*refs, has_y, has_h, na_tiles):
    it = iter(refs)
    x_ref = next(it)
    xb_ref = next(it) if na_tiles else None
    y_ref = next(it) if has_y else None
    mv_ref = next(it)
    xo_ref = next(it) if has_y else None
    h_ref = next(it) if has_h else None
    x = _pick_source(x_ref, xb_ref, na_tiles) if na_tiles else x_ref[...]
    mv = mv_ref[...]
    if has_y:
        x = x + mv[0:1] * y_ref[...]
        xo_ref[...] = x
    if has_h:
        h = _rms(x, mv[3:4]) * (1.0 + mv[1:2]) + mv[2:3]
        h_ref[...] = h.astype(h_ref.dtype)


def _resnorm(x, y, modv, n_prompt, seq_s, has_h, rows=None, h_dtype=BF16):
    xs = x if isinstance(x, tuple) else (x,)
    d = xs[0].shape[1]
    row0, n = (0, sum(a.shape[0] for a in xs)) if rows is None else rows
    i0 = row0 // TM
    has_y = y is not None
    na_tiles = xs[0].shape[0] // TM if len(xs) == 2 else 0

    def group(i):
        r = (i + i0) * TM
        return jnp.where(r < n_prompt, 0, 1 + (r - n_prompt) // seq_s)

    tile = pl.BlockSpec((TM, d), lambda i: (i + i0, 0))
    x_specs = _two_source_specs(TM, d, na_tiles) if na_tiles else [tile]
    in_specs = x_specs + ([tile] if has_y else []) + [pl.BlockSpec((None, 8, d), lambda i: (group(i), 0, 0))]
    out_tile = pl.BlockSpec((TM, d), lambda i: (i, 0))
    out_specs, out_shape = [], []
    if has_y:
        out_specs.append(out_tile)
        out_shape.append(jax.ShapeDtypeStruct((n, d), F32))
    if has_h:
        out_specs.append(out_tile)
        out_shape.append(jax.ShapeDtypeStruct((n, d), h_dtype))
    args = list(xs) + ([y] if has_y else []) + [modv]
    outs = pl.pallas_call(
        functools.partial(_resnorm_kernel, has_y=has_y, has_h=has_h, na_tiles=na_tiles),
        grid=(n // TM,),
        in_specs=in_specs,
        out_specs=out_specs,
        out_shape=out_shape,
        compiler_params=_cp(("parallel",)),
        name="resnorm",
    )(*args)
    return outs


def _mm_kernel(x_ref, w_ref, o_ref, wb_ref):
    @pl.when(pl.program_id(1) == 0)
    def _():
        wb_ref[...] = w_ref[...].astype(BF16)

    o_ref[...] = _dot_nt(x_ref[...], wb_ref[...]).astype(o_ref.dtype)


def _mm_t(x, wt, l, nrows, tn, name, tm=TM, out_dtype=F32):
    m, k = x.shape
    if l is None:
        wspec = pl.BlockSpec((tn, k), lambda j, i: (j, 0))
    else:
        wspec = pl.BlockSpec((None, tn, k), lambda j, i: (l, j, 0))
    return pl.pallas_call(
        _mm_kernel,
        grid=(nrows // tn, m // tm),
        in_specs=[pl.BlockSpec((tm, k), lambda j, i: (i, 0)), wspec],
        out_specs=pl.BlockSpec((tm, tn), lambda j, i: (i, j)),
        out_shape=jax.ShapeDtypeStruct((m, nrows), out_dtype),
        scratch_shapes=[pltpu.VMEM((tn, k), BF16)],
        compiler_params=_cp(("arbitrary", "arbitrary")),
        name=name,
    )(x, wt)


def _gla_intra(q, k, b, d, c):
    rowi = lax.broadcasted_iota(jnp.int32, (c, LANES), 0)
    ri = lax.broadcasted_iota(jnp.int32, (c, c), 0)
    ci = lax.broadcasted_iota(jnp.int32, (c, c), 1)
    a = jnp.zeros((c, c), F32)
    size = c
    while size >= 8:
        half = size // 2
        nb = c // size
        b3 = b.reshape(nb, size, LANES)
        pos = rowi & (size - 1)
        if d == 0:
            ref = b3[:, half - 1:half, :]
            qsel = pos >= half
        else:
            ref = b3[:, half:half + 1, :]
            qsel = pos < half
        ref = jnp.broadcast_to(ref, (nb, size, LANES)).reshape(c, LANES)
        z = jnp.where(qsel, q, k) * jnp.exp2(-jnp.abs(b - ref))
        qs = jnp.where(qsel, z, 0.0).astype(BF16)
        ks = jnp.where(qsel, 0.0, z).astype(BF16)
        p = _dot_nt(qs, ks)
        if nb > 1:
            shift = int(math.log2(size))
            p = jnp.where((ri >> shift) == (ci >> shift), p, 0.0)
        a = a + p
        size = half
    b3 = b.reshape(c // 8, 8, LANES)
    lo, hi = (0, 4) if d == 0 else (3, 7)
    ref_lo = jnp.broadcast_to(b3[:, lo:lo + 1, :], (c // 8, 8, LANES)).reshape(c, LANES)
    ref_hi = jnp.broadcast_to(b3[:, hi:hi + 1, :], (c // 8, 8, LANES)).reshape(c, LANES)
    ref = jnp.where((rowi & 7) >= 4, ref_hi, ref_lo)
    qs = (q * jnp.exp2(jnp.minimum(b - ref, 0.0))).astype(BF16)
    ks = (k * jnp.exp2(ref - b)).astype(BF16)
    p = _dot_nt(qs, ks)
    keep = ((ri >> 2) == (ci >> 2)) & ((ci <= ri) if d == 0 else (ci >= ri))
    return a + jnp.where(keep, p, 0.0)


def _load_column_major(dst_ref, src_ref, t, stage=None):
    rows = t // GRID_W
    for k in range(src_ref.shape[1] // LANES):
        lanes = slice(k * LANES, (k + 1) * LANES)
        if stage is None:
            strided = src_ref
        else:
            stage[...] = src_ref[:, lanes]
            strided = stage
        for col in range(GRID_W):
            dst_ref[col * rows:(col + 1) * rows, lanes] = strided[pl.ds(col, rows, stride=GRID_W), :]


def _store_row_major(dst_ref, y, t, stage=None):
    rows = t // GRID_W
    for k in range(y.shape[1] // LANES):
        lanes = slice(k * LANES, (k + 1) * LANES)
        strided = dst_ref if stage is None else stage
        for col in range(GRID_W):
            strided[pl.ds(col, rows, stride=GRID_W), :] = y[col * rows:(col + 1) * rows, lanes]
        if stage is not None:
            dst_ref[:, lanes] = stage[...].astype(dst_ref.dtype)


def _hgrn_kernel(aq_ref, af_ref, ab_ref, ai_ref, ag_ref, lb_ref, g_ref, s0_ref, y_ref, s_ref, o_acc, *perm_bufs,
                 t, c):
    stage = None
    if perm_bufs:
        *perm_bufs, stage = perm_bufs
        srcs = (aq_ref, af_ref, ab_ref, ai_ref, ag_ref)
        for dst, src in zip(perm_bufs, srcs):
            _load_column_major(dst, src, t)
        aq_ref, af_ref, ab_ref, ai_ref, ag_ref = perm_bufs
    nchunk = t // c
    scale = HEAD_D ** -0.5
    for d in (0, 1):
        z_ref = af_ref if d == 0 else ab_ref
        lbd = lb_ref[d:d + 1, :]

        def chunk(i, st, d=d, z_ref=z_ref, lbd=lbd):
            cidx = i if d == 0 else nchunk - 1 - i
            r0 = pl.multiple_of(cidx * c, c)
            z = z_ref[pl.ds(r0, c), :]
            f = lbd + (1.0 - lbd) * jax.nn.sigmoid(z)
            lf = jnp.log2(jnp.maximum(f, F_FLOOR))
            k = (1.0 - lbd) * jax.nn.sigmoid(-z)
            q = _silu(aq_ref[pl.ds(r0, c), :]) * scale
            v = ai_ref[pl.ds(r0, c), :].astype(BF16)
            b = _chunk_cumsum(lf, d, c)
            a = _gla_intra(q, k, b, d, c)
            o = _dot(a.astype(BF16), v) + _dot_nt((q * jnp.exp2(b)).astype(BF16), st.astype(BF16))
            if d == 0:
                o_acc[pl.ds(r0, c), :] = o
            else:
                o_acc[pl.ds(r0, c), :] += o
            blast = b[c - 1:c, :] if d == 0 else b[0:1, :]
            kd = (k * jnp.exp2(blast - b)).astype(BF16)
            return st * jnp.exp2(blast) + _dot_tn(v, kd)

        st = s0_ref[0, d, 0].T
        if nchunk == 1:
            st = chunk(0, st)
        else:
            st = lax.fori_loop(0, nchunk, chunk, st)
        s_ref[0, d, 0] = st.T
    y = _rms(o_acc[...], g_ref[...]) * _silu(ag_ref[...])
    if perm_bufs:
        _store_row_major(y_ref, y, t, stage)
    else:
        y_ref[...] = y.astype(y_ref.dtype)


def _shared_out_call(kernel_fn, prevs, in_specs, args, **kw):
    extra = [p for p in prevs if p is not None]
    if not extra:
        return pl.pallas_call(kernel_fn, in_specs=in_specs, **kw)(*args)
    n_in = len(in_specs)
    aliases, k = {}, 0
    for out_idx, p in enumerate(prevs):
        if p is not None:
            aliases[n_in + k] = out_idx
            k += 1

    def body(*refs):
        kernel_fn(*refs[:n_in], *refs[n_in + len(extra):])

    return pl.pallas_call(body, in_specs=in_specs + [pl.BlockSpec(memory_space=pl.ANY)] * len(extra),
                          input_output_aliases=aliases, **kw)(*args, *extra)


def _state_out(nseq, slot, nslots, hg):
    spec = pl.BlockSpec((1, None, 2, hg, HEAD_D, HEAD_D), lambda s, h: (s, slot, 0, h, 0, 0))
    return spec, jax.ShapeDtypeStruct((nseq, nslots, 2, N_HEADS, HEAD_D, HEAD_D), F32)


def _hgrn(p_main, lb, norm_g, s0, row_blk0, nseq, t, y_prev, col0, s_prev=None, slot=0, nslots=1,
          col_major=False):
    c = min(GLA_CHUNK, t)
    seq_buf = pltpu.VMEM((t, HEAD_D), F32)

    def col(base):
        return pl.BlockSpec((t, HEAD_D), lambda s, h: (row_blk0 + s, base + h))

    st_spec = pl.BlockSpec((1, 2, 1, HEAD_D, HEAD_D), lambda s, h: (s, 0, h, 0, 0))
    so_spec, so_shape = _state_out(nseq, slot, nslots, 1)
    return _shared_out_call(
        functools.partial(_hgrn_kernel, t=t, c=c), [y_prev, s_prev],
        [col(0), col(4), col(8), col(12), col(16),
         pl.BlockSpec((2, HEAD_D), lambda s, h: (0, h)),
         pl.BlockSpec((1, HEAD_D), lambda s, h: (0, 0)),
         st_spec],
        (p_main, p_main, p_main, p_main, p_main, lb, norm_g.reshape(1, HEAD_D), s0),
        grid=(nseq, N_HEADS),
        out_specs=[pl.BlockSpec((t, HEAD_D), lambda s, h: (row_blk0 + s, col0 + h)), so_spec],
        out_shape=[jax.ShapeDtypeStruct(y_prev.shape, y_prev.dtype), so_shape],
        scratch_shapes=[seq_buf] * (7 if col_major else 1),
        compiler_params=_cp(("parallel", "parallel")),
        name="hgrn2",
    )


def _dot_b(a, b):
    return _dot(a.astype(BF16), b.astype(BF16))


def _tri_solve(ms, xs, c):
    ri = lax.broadcasted_iota(jnp.int32, (c, c), 0)
    ci = lax.broadcasted_iota(jnp.int32, (c, c), 1)
    eye = jnp.where(ri == ci, 1.0, 0.0)
    m8 = [jnp.where((ri >> 3) == (ci >> 3), m, 0.0) for m in ms]
    ts = [eye - m for m in m8]
    ps = [_dot_b(m, m) for m in m8]
    ts = [t + _dot_b(t, p) for t, p in zip(ts, ps)]
    ps = [_dot_b(p, p) for p in ps]
    ts = [t + _dot_b(t, p) for t, p in zip(ts, ps)]
    shift = 3
    while (1 << shift) < c:
        join = ((ri >> (shift + 1)) == (ci >> (shift + 1))) & ((ri >> shift) != (ci >> shift))
        lts = [_dot_b(jnp.where(join, m, 0.0), t) for m, t in zip(ms, ts)]
        ts = [t - _dot_b(t, lt) for t, lt in zip(ts, lts)]
        shift += 1
    return [_dot_b(t, x) for t, x in zip(ts, xs)]


def _gdn_kernel(cq_ref, ck_ref, cv_ref, cg_ref, ab_ref, cw_ref, alog_ref, dtb_ref, g_ref, s0_ref,
                y_ref, s_ref, q_s, k_s, v_s, la_s, be_s, w_s, qk_s, qe_s, kd_s, dec_s, o_acc, *perm_bufs,
                t, c, span, hg):
    stage = None
    if perm_bufs:
        *perm_bufs, stage = perm_bufs
        srcs = (cq_ref, ck_ref, cv_ref, cg_ref, ab_ref)
        for dst, src in zip(perm_bufs, srcs):
            _load_column_major(dst, src, t, stage)
        cq_ref, ck_ref, cv_ref, cg_ref, ab_ref = perm_bufs
    nchunk = t // c
    cps = span // c
    rowt = lax.broadcasted_iota(jnp.int32, (t, HEAD_D), 0)
    lane = lax.broadcasted_iota(jnp.int32, (t, LANES), 1)
    ri = lax.broadcasted_iota(jnp.int32, (span, span), 0)
    ci = lax.broadcasted_iota(jnp.int32, (span, span), 1)
    shift = int(math.log2(c))
    same = (ri >> shift) == (ci >> shift)
    heads = range(hg)
    dirs = (0, 1)

    def conv(x_ref, j, hh):
        x = x_ref[:, hh * HEAD_D:(hh + 1) * HEAD_D]
        w = cw_ref[j][:, hh * HEAD_D:(hh + 1) * HEAD_D]
        xm = jnp.where(rowt == 0, 0.0, pltpu.roll(x, 1, 0))
        xp = jnp.where(rowt == t - 1, 0.0, pltpu.roll(x, t - 1, 0))
        return xm * w[0:1] + x * w[1:2] + xp * w[2:3]

    ab = ab_ref[...]
    lane1 = lax.broadcasted_iota(jnp.int32, (1, LANES), 1)
    alog_l = jnp.zeros((1, LANES), F32)
    dtb_l = jnp.zeros((1, LANES), F32)
    for d in dirs:
        for hd in range(N_HEADS):
            alog_l = jnp.where(lane1 == d * N_HEADS + hd, alog_ref[d, hd], alog_l)
            dtb_l = jnp.where(lane1 == d * N_HEADS + hd, dtb_ref[d, hd], dtb_l)
    xs = ab + dtb_l
    log_a_all = (-LOG2E) * jnp.exp(alog_l) * (jnp.maximum(xs, 0.0) + jnp.log1p(jnp.exp(-jnp.abs(xs))))
    beta_all = jax.nn.sigmoid(ab)
    for hh in heads:
        h = pl.program_id(1) * hg + hh
        q = _silu(conv(cq_ref, 0, hh))
        q_s[hh] = q * lax.rsqrt(jnp.sum(q * q, axis=-1, keepdims=True) + EPS) * (HEAD_D ** -0.5)
        k = _silu(conv(ck_ref, 1, hh))
        k_s[hh] = k * lax.rsqrt(jnp.sum(k * k, axis=-1, keepdims=True) + EPS)
        v_s[hh] = _silu(conv(cv_ref, 2, hh))
        for d in dirs:
            la_col = jnp.sum(jnp.where(lane == d * N_HEADS + h, log_a_all, 0.0), axis=-1, keepdims=True)
            be_col = jnp.sum(jnp.where(lane == 2 * N_HEADS + d * N_HEADS + h, beta_all, 0.0), axis=-1,
                             keepdims=True)
            la_s[hh, d] = jnp.broadcast_to(la_col, (t, LANES))
            be_s[hh, d] = jnp.broadcast_to(be_col, (t, LANES))
    o_acc[...] = jnp.zeros(o_acc.shape, F32)

    def wy_span(sp, carry):
        r0 = pl.multiple_of(sp * span, span)
        ms, xs, dst = [], [], []
        for hh in heads:
            qsp = q_s[hh, pl.ds(r0, span), :]
            ksp = k_s[hh, pl.ds(r0, span), :]
            vsp = v_s[hh, pl.ds(r0, span), :]
            kb = ksp.astype(BF16)
            kk = _dot_nt(kb, kb)
            qk = _dot_nt(qsp.astype(BF16), kb)
            for d in dirs:
                be = be_s[hh, d, pl.ds(r0, span), :]
                g = _chunk_cumsum(la_s[hh, d, pl.ds(r0, span), :], d, c)
                gt = g.T
                gcol = jnp.concatenate([g] * (span // LANES), axis=1)
                grow = jnp.concatenate([gt] * (span // LANES), axis=0)
                incl = same & ((ci <= ri) if d == 0 else (ci >= ri))
                gamma = jnp.where(incl, jnp.exp2(jnp.minimum(gcol - grow, 0.0)), 0.0)
                bcol = jnp.concatenate([be] * (span // LANES), axis=1)
                m_sp = jnp.where(ci == ri, 0.0, bcol * kk * gamma)
                qkg = (qk * gamma).astype(BF16)
                eg = jnp.exp2(g)
                xv = be * vsp
                xk = be * eg * ksp
                qe_s[hh, d, pl.ds(r0, span), :] = (qsp * eg).astype(BF16)
                for j in range(cps):
                    lo = j * c
                    rj = pl.multiple_of(r0 + lo, c)
                    ms.append(m_sp[lo:lo + c, lo:lo + c])
                    xs.append(jnp.concatenate([xv[lo:lo + c], xk[lo:lo + c]], axis=1))
                    dst.append((hh, d, rj))
                    qk_s[hh, d, pl.ds(rj, c), 0:c] = qkg[lo:lo + c, lo:lo + c]
                    gj = g[lo:lo + c]
                    glast = gj[c - 1:c] if d == 0 else gj[0:1]
                    kd_s[hh, d, pl.ds(rj, c), :] = (ksp[lo:lo + c] * jnp.exp2(glast - gj)).astype(BF16)
                    dec_s[hh, d, pl.ds(sp * cps + j, 1), :] = jnp.exp2(glast)
        for (hh, d, rj), w in zip(dst, _tri_solve(ms, xs, c)):
            w_s[hh, d, pl.ds(rj, c), :] = w
        return carry

    if t == span:
        wy_span(0, 0)
    else:
        lax.fori_loop(0, t // span, wy_span, 0)

    chains = [(hh, d) for hh in heads for d in dirs]

    def scan_step(i, carry):
        cidx = (i, nchunk - 1 - i)
        r0 = [pl.multiple_of(ck * c, c) for ck in cidx]
        sb = [s.astype(BF16) for s in carry]
        ws = [w_s[hh, d, pl.ds(r0[d], c), :] for hh, d in chains]
        wks = [_dot(w[:, HEAD_D:].astype(BF16), s) for w, s in zip(ws, sb)]
        oq = [_dot(qe_s[hh, d, pl.ds(r0[d], c), :], s) for (hh, d), s in zip(chains, sb)]
        vnb = [(w[:, :HEAD_D] - wk).astype(BF16) for w, wk in zip(ws, wks)]
        new = [s * dec_s[hh, d, pl.ds(cidx[d], 1), :] + _dot_tn(kd_s[hh, d, pl.ds(r0[d], c), :], v)
               for (hh, d), s, v in zip(chains, carry, vnb)]
        for (hh, d), o, v in zip(chains, oq, vnb):
            o_acc[hh, pl.ds(r0[d], c), :] += o + _dot(qk_s[hh, d, pl.ds(r0[d], c), 0:c], v)
        return tuple(new)

    finals = lax.fori_loop(0, nchunk, scan_step, tuple(s0_ref[0, d, hh] for hh, d in chains),
                           unroll=nchunk <= 4)
    for (hh, d), s in zip(chains, finals):
        s_ref[0, d, hh] = s
    ys = []
    for hh in heads:
        sl = slice(hh * HEAD_D, (hh + 1) * HEAD_D)
        ys.append(_rms(o_acc[hh], g_ref[...]) * _silu(cg_ref[:, sl]))
    y = jnp.concatenate(ys, axis=1)
    if perm_bufs:
        _store_row_major(y_ref, y, t, stage)
    else:
        y_ref[...] = y.astype(y_ref.dtype)


def _gdn(p_main, p_ab, conv_w, a_log, dt_bias, norm_g, s0, row_blk0, nseq, t, y_prev, col0, s_prev=None,
         slot=0, nslots=1, col_major=False):
    c = GDN_CHUNK
    hg = GDN_HEADS_PER_STEP if t * GDN_HEADS_PER_STEP <= GDN_MAX_ROWS_PER_STEP else GDN_HEADS_PER_STEP // 2
    wid = hg * HEAD_D
    blk = BR_WIDTH // wid

    def col(base):
        return pl.BlockSpec((t, wid), lambda s, h: (row_blk0 + s, base * blk + h))

    st_spec = pl.BlockSpec((1, 2, hg, HEAD_D, HEAD_D), lambda s, h: (s, 0, h, 0, 0))
    smem = pl.BlockSpec(memory_space=pltpu.SMEM)
    seq_buf = pltpu.VMEM((hg, t, HEAD_D), F32)
    dir_f32 = pltpu.VMEM((hg, 2, t, LANES), F32)
    dir_bf16 = pltpu.VMEM((hg, 2, t, LANES), BF16)
    scratch = [seq_buf, seq_buf, seq_buf, dir_f32, dir_f32,
               pltpu.VMEM((hg, 2, t, 2 * HEAD_D), F32), dir_bf16, dir_bf16, dir_bf16,
               pltpu.VMEM((hg, 2, max(t // c, 8), LANES), F32), seq_buf]
    if col_major:
        scratch += [pltpu.VMEM((t, wid), F32)] * 4 + [pltpu.VMEM((t, LANES), F32)] * 2
    so_spec, so_shape = _state_out(nseq, slot, nslots, hg)
    return _shared_out_call(
        functools.partial(_gdn_kernel, t=t, c=c, span=GDN_SPAN, hg=hg), [y_prev, s_prev],
        [col(6), col(7), col(8), col(9),
         pl.BlockSpec((t, LANES), lambda s, h: (row_blk0 + s, 0)),
         pl.BlockSpec((3, 3, wid), lambda s, h: (0, 0, h)),
         smem, smem,
         pl.BlockSpec((1, HEAD_D), lambda s, h: (0, 0)),
         st_spec],
        (p_main, p_main, p_main, p_main, p_ab, conv_w, a_log, dt_bias, norm_g.reshape(1, HEAD_D), s0),
        grid=(nseq, N_HEADS // hg),
        out_specs=[pl.BlockSpec((t, wid), lambda s, h: (row_blk0 + s, col0 * LANES // wid + h)), so_spec],
        out_shape=[jax.ShapeDtypeStruct(y_prev.shape, y_prev.dtype), so_shape],
        scratch_shapes=scratch,
        compiler_params=_cp(("parallel", "parallel")),
        name="gdn",
    )


def _cmul(ar, ai, br, bi):
    return ar * br - ai * bi, ar * bi + ai * br


def _s5_kernel(u_ref, bb_ref, cc_ref, lam_ref, dsk_ref, s0_ref, y_ref, sf_ref, up_s, x_s, yp_s,
               *, tb, seg, segs_per_seq, zero_init, col_major):
    r_par = tb // seg
    half = S5_GB * S5_STATE
    if col_major:
        rows = tb // GRID_W
        assert segs_per_seq == r_par and seg % rows == 0

        def step_rows(r):
            return pl.ds((r % rows) * GRID_W + r // rows, r_par, stride=seg // rows)
    else:
        def step_rows(r):
            return pl.ds(r, r_par, stride=seg)

    for r in range(seg):
        up_s[r * r_par:(r + 1) * r_par, :] = u_ref[step_rows(r), :]
    ub = up_s[...].astype(BF16)
    sf_ref[...] = jnp.zeros(sf_ref.shape, F32)
    nseq = r_par // segs_per_seq
    lam = [(lam_ref[0, d, 0:1, :], lam_ref[0, d, 1:2, :]) for d in (0, 1)]
    for d in (0, 1):
        x_s[d] = _dot(ub, bb_ref[0, d])

    def offsets(i):
        return [pl.multiple_of(r * r_par, r_par) for r in (i, seg - 1 - i)]

    def local_step(i, carry):
        off = offsets(i)
        new = []
        for d in (0, 1):
            (lre, lim), (hre, him) = lam[d], carry[d]
            nre = lre * hre - lim * him + x_s[d, pl.ds(off[d], r_par), 0:half]
            nim = lre * him + lim * hre + x_s[d, pl.ds(off[d], r_par), half:2 * half]
            x_s[d, pl.ds(off[d], r_par), 0:half] = nre
            x_s[d, pl.ds(off[d], r_par), half:2 * half] = nim
            new.append((nre, nim))
        return tuple(new)

    zero = jnp.zeros((r_par, half), F32)
    lax.fori_loop(0, seg, local_step, ((zero, zero), (zero, zero)), unroll=4)

    end_rows = ((seg - 1) * r_par, 0)
    needs_carry = not (zero_init and segs_per_seq == 1)
    hins = []
    for d in (0, 1):
        if not needs_carry:
            break
        lre, lim = lam[d]
        pre, pim = lre, lim
        n = 1
        while n < seg:
            pre, pim = _cmul(pre, pim, pre, pim)
            n *= 2
        end_row = end_rows[d]
        ere = x_s[d, end_row:end_row + r_par, 0:half]
        eim = x_s[d, end_row:end_row + r_par, half:2 * half]
        hin = [None] * r_par
        order = range(r_par) if d == 0 else range(r_par - 1, -1, -1)
        for j in order:
            sq, pos = divmod(j, segs_per_seq)
            first = pos == (0 if d == 0 else segs_per_seq - 1)
            if first:
                hin[j] = (s0_ref[0, 0, d, sq:sq + 1, 0:half], s0_ref[0, 0, d, sq:sq + 1, half:2 * half])
            else:
                pj = j - 1 if d == 0 else j + 1
                cre, cim = _cmul(pre, pim, hin[pj][0], hin[pj][1])
                hin[j] = (cre + ere[pj:pj + 1], cim + eim[pj:pj + 1])
        hins.append((jnp.concatenate([hj[0] for hj in hin], axis=0),
                     jnp.concatenate([hj[1] for hj in hin], axis=0)))

    def carry_step(i, pws):
        off = offsets(i)
        new = []
        for d in (0, 1):
            (lre, lim), (wre, wim) = lam[d], pws[d]
            are, aim = _cmul(wre, wim, hins[d][0], hins[d][1])
            x_s[d, pl.ds(off[d], r_par), 0:half] += are
            x_s[d, pl.ds(off[d], r_par), half:2 * half] += aim
            new.append(_cmul(wre, wim, lre, lim))
        return tuple(new)

    if needs_carry:
        lax.fori_loop(0, seg, carry_step, (lam[0], lam[1]), unroll=4)

    for d in (0, 1):
        for sq in range(nseq):
            j = (sq + 1) * segs_per_seq - 1 if d == 0 else sq * segs_per_seq
            row = end_rows[d] + j
            sf_ref[0, 0, d, sq:sq + 1, :] = x_s[d, row:row + 1, :]
    yp_s[...] = (_dot(x_s[0].astype(BF16), cc_ref[0, 0]) + _dot(x_s[1].astype(BF16), cc_ref[0, 1])
                 + up_s[...] * dsk_ref[...])
    for r in range(seg):
        up_s[step_rows(r), :] = yp_s[r * r_par:(r + 1) * r_par, :]
    y_ref[...] = up_s[...].astype(y_ref.dtype)


def _s5(p_main, bb, cc, lam, l, dskip, s0, row_blk0, nblk, tb, t, seg, zero_init, y_prev, col0, col_major=False):
    ngb = S5_GROUPS // S5_GB
    w2 = 2 * S5_GB * S5_STATE
    return _shared_out_call(
        functools.partial(_s5_kernel, tb=tb, seg=seg, segs_per_seq=t // seg, zero_init=zero_init,
                          col_major=col_major), [y_prev, None],
        [
            pl.BlockSpec((tb, LANES), lambda i, gb: (row_blk0 + i, 20 + gb)),
            pl.BlockSpec((None, 1, 2, LANES, w2), lambda i, gb: (l, gb, 0, 0, 0)),
            pl.BlockSpec((None, 1, 2, w2, LANES), lambda i, gb: (l, gb, 0, 0, 0)),
            pl.BlockSpec((None, 1, 2, 2, w2 // 2), lambda i, gb: (l, gb, 0, 0, 0)),
            pl.BlockSpec((1, LANES), lambda i, gb: (0, gb)),
            pl.BlockSpec((1, 1, 2, 8, w2), lambda i, gb: (i, gb, 0, 0, 0)),
        ],
        (p_main, bb, cc, lam, dskip.reshape(1, BR_WIDTH), s0),
        grid=(nblk, ngb),
        out_specs=[pl.BlockSpec((tb, LANES), lambda i, gb: (row_blk0 + i, col0 + gb)),
                   pl.BlockSpec((1, 1, 2, 8, w2), lambda i, gb: (i, gb, 0, 0, 0))],
        out_shape=[jax.ShapeDtypeStruct(y_prev.shape, y_prev.dtype),
                   jax.ShapeDtypeStruct((nblk, ngb, 2, 8, w2), F32)],
        scratch_shapes=[pltpu.VMEM((tb, LANES), F32), pltpu.VMEM((2, tb, w2), F32), pltpu.VMEM((tb, LANES), F32)],
        compiler_params=_cp(("parallel", "parallel")),
        name="s5",
    )


def _s5_params(a_re, a_im, log_dt, b_re, b_im, c_re, c_im):
    depth = a_re.shape[0]
    dt = jnp.exp(log_dt)[..., None]
    mag = jnp.exp(dt * a_re)
    lam_re = mag * jnp.cos(dt * a_im)
    lam_im = mag * jnp.sin(dt * a_im)
    den = a_re * a_re + a_im * a_im
    co_re = ((lam_re - 1.0) * a_re + lam_im * a_im) / den
    co_im = (lam_im * a_re - (lam_re - 1.0) * a_im) / den
    bb_re = co_re[..., None] * b_re - co_im[..., None] * b_im
    bb_im = co_re[..., None] * b_im + co_im[..., None] * b_re
    ngb = S5_GROUPS // S5_GB
    eye = jnp.eye(S5_GB, dtype=F32)

    def pack_in(w):
        w = w.reshape(depth, 2, ngb, S5_GB, S5_STATE, S5_GROUP)
        blk = jnp.einsum('ldbgpi,gh->lbdgihp', w, eye)
        return blk.reshape(depth, ngb, 2, S5_GB * S5_GROUP, S5_GB * S5_STATE)

    def pack_out(w):
        w = w.reshape(depth, 2, ngb, S5_GB, S5_GROUP, S5_STATE)
        blk = jnp.einsum('ldbgip,gh->lbdgphi', w, eye)
        return blk.reshape(depth, ngb, 2, S5_GB * S5_STATE, S5_GB * S5_GROUP)

    bb = jnp.concatenate([pack_in(bb_re), pack_in(bb_im)], axis=-1).astype(BF16)
    cc = jnp.concatenate([pack_out(c_re), -pack_out(c_im)], axis=-2).astype(BF16)
    lam = jnp.stack([lam_re, lam_im], axis=2)
    lam = lam.reshape(depth, 2, 2, ngb, S5_GB * S5_STATE).transpose(0, 3, 1, 2, 4)
    return bb, cc, lam


def _residual_epilogue(y, g_ref, x_old, mv_ref, xo_ref, h_ref):
    mv = mv_ref[...]
    x = x_old + mv[0:1] * _rms(y, g_ref[...])
    xo_ref[...] = x
    h_ref[...] = (_rms(x, mv[3:4]) * (1.0 + mv[1:2]) + mv[2:3]).astype(h_ref.dtype)


def _group_of_tile(tm, n_prompt, seq_s):
    def group(i):
        return jnp.where(i * tm < n_prompt, 0, 1 + (i * tm - n_prompt) // seq_s)
    return group


def _merge_kernel(ya_ref, yb_ref, yc_ref, mg_ref, wbr_ref, wout_ref, wglu_ref, bglu_ref, g_ref, *rest, na_tiles):
    if na_tiles:
        x_ref, xb_ref, mv_ref, xo_ref, h_ref, wbr_s, wout_s, wglu_s = rest
    else:
        x_ref, mv_ref, xo_ref, h_ref, wbr_s, wout_s, wglu_s = rest

    @pl.when(pl.program_id(0) == 0)
    def _():
        wbr_s[...] = wbr_ref[...].astype(BF16)
        wout_s[...] = wout_ref[...].astype(BF16)
        wglu_s[...] = wglu_ref[...].astype(BF16)

    d = wout_s.shape[0]
    z = _dot(yb_ref[...], wglu_s[...]) + bglu_ref[...]
    half = z.shape[1] // 2
    branches = (ya_ref[...], (z[:, :half] * jax.nn.sigmoid(z[:, half:])).astype(BF16), yc_ref[...])
    acc = None
    for n, br in enumerate(branches):
        proj = _dot(br.astype(BF16), wbr_s[n])
        term = proj * jax.nn.sigmoid(mg_ref[:, n * d:(n + 1) * d].astype(F32))
        acc = term if acc is None else acc + term
    y = _dot(acc.astype(BF16), wout_s[...])
    x_old = _pick_source(x_ref, xb_ref, na_tiles) if na_tiles else x_ref[...]
    _residual_epilogue(y, g_ref, x_old, mv_ref, xo_ref, h_ref)


def _merge(branches, mg, w_branch, w_out, glu_w, glu_b, g, l, x, modv, n_prompt, seq_s, h_dtype):
    n = branches.shape[0]
    d = w_out.shape[-1]
    tm = TM
    group = _group_of_tile(tm, n_prompt, seq_s)
    tile = pl.BlockSpec((tm, d), lambda i: (i, 0))
    xs = x if isinstance(x, tuple) else (x,)
    na_tiles = xs[0].shape[0] // tm if len(xs) == 2 else 0
    x_specs = _two_source_specs(tm, d, na_tiles) if na_tiles else [tile]
    return pl.pallas_call(
        functools.partial(_merge_kernel, na_tiles=na_tiles),
        grid=(n // tm,),
        in_specs=[pl.BlockSpec((tm, BR_WIDTH), lambda i: (i, 0)),
                  pl.BlockSpec((tm, BR_WIDTH), lambda i: (i, 1)),
                  pl.BlockSpec((tm, BR_WIDTH), lambda i: (i, 2)),
                  pl.BlockSpec((tm, 3 * d), lambda i: (i, 0)),
                  pl.BlockSpec((None, 3, BR_WIDTH, d), lambda i: (l, 0, 0, 0), pipeline_mode=pl.Buffered(1)),
                  pl.BlockSpec((None, d, d), lambda i: (l, 0, 0), pipeline_mode=pl.Buffered(1)),
                  pl.BlockSpec((None, BR_WIDTH, 2 * BR_WIDTH), lambda i: (l, 0, 0), pipeline_mode=pl.Buffered(1)),
                  pl.BlockSpec((None, 1, 2 * BR_WIDTH), lambda i: (l, 0, 0)),
                  pl.BlockSpec((1, d), lambda i: (0, 0)),
                  *x_specs,
                  pl.BlockSpec((None, 8, d), lambda i: (group(i), 0, 0))],
        out_specs=[tile, tile],
        out_shape=[jax.ShapeDtypeStruct((n, d), F32), jax.ShapeDtypeStruct((n, d), h_dtype)],
        scratch_shapes=[pltpu.VMEM((3, BR_WIDTH, d), BF16), pltpu.VMEM((d, d), BF16),
                        pltpu.VMEM((BR_WIDTH, 2 * BR_WIDTH), BF16)],
        compiler_params=_cp(("arbitrary",)),
        name="merge",
    )(branches, branches, branches, mg, w_branch, w_out, glu_w, glu_b.reshape(glu_b.shape[0], 1, 2 * BR_WIDTH),
      g.reshape(1, d), *xs, modv)


def _swiglu_kernel(x_ref, wg_ref, wu_ref, o_ref, wg_s, wu_s):
    @pl.when(pl.program_id(1) == 0)
    def _():
        wg_s[...] = wg_ref[...].astype(BF16)
        wu_s[...] = wu_ref[...].astype(BF16)

    x = x_ref[...]
    o_ref[...] = (_silu(_dot(x, wg_s[...])) * _dot(x, wu_s[...])).astype(BF16)


def _swiglu_up(x, wg, wu, e, f):
    n, d = x.shape
    nf = wg.shape[-1] // f
    wspec = pl.BlockSpec((None, d, f), lambda j, i: (e, 0, j))
    return pl.pallas_call(
        _swiglu_kernel,
        grid=(nf, n // TM),
        in_specs=[pl.BlockSpec((TM, d), lambda j, i: (i, 0)), wspec, wspec],
        out_specs=pl.BlockSpec((TM, f), lambda j, i: (i, j)),
        out_shape=jax.ShapeDtypeStruct((n, nf * f), BF16),
        scratch_shapes=[pltpu.VMEM((d, f), BF16), pltpu.VMEM((d, f), BF16)],
        compiler_params=_cp(("arbitrary", "arbitrary")),
        name="swiglu_up",
    )(x, wg, wu)


def _down_kernel(hid_ref, w_ref, g_ref, x_ref, mv_ref, xo_ref, h_ref, w_s):
    @pl.when(pl.program_id(0) == 0)
    def _():
        w_s[...] = w_ref[...].astype(BF16)

    _residual_epilogue(_dot(hid_ref[...], w_s[...]), g_ref, x_ref[...], mv_ref, xo_ref, h_ref)


def _down_res(hid, wd, e, g, x, modv, n_prompt, seq_s, h_dtype):
    n, f = hid.shape
    d = wd.shape[-1]
    group = _group_of_tile(TM, n_prompt, seq_s)
    tile = pl.BlockSpec((TM, d), lambda i: (i, 0))
    return pl.pallas_call(
        _down_kernel,
        grid=(n // TM,),
        in_specs=[pl.BlockSpec((TM, f), lambda i: (i, 0)),
                  pl.BlockSpec((None, f, d), lambda i: (e, 0, 0)),
                  pl.BlockSpec((1, d), lambda i: (0, 0)),
                  tile,
                  pl.BlockSpec((None, 8, d), lambda i: (group(i), 0, 0))],
        out_specs=[tile, tile],
        out_shape=[jax.ShapeDtypeStruct((n, d), F32), jax.ShapeDtypeStruct((n, d), h_dtype)],
        scratch_shapes=[pltpu.VMEM((f, d), BF16)],
        compiler_params=_cp(("arbitrary",)),
        name="down_res",
    )(hid, wd, g.reshape(1, d), x, modv)


def _router_kernel(x_ref, w_ref, aux_ref, cnt_ref, base_s, *, ne):
    @pl.when(pl.program_id(0) == 0)
    def _():
        base_s[...] = jnp.zeros(base_s.shape, F32)

    w = w_ref[...]
    w_hi = w.astype(BF16)
    w_lo = (w - w_hi.astype(F32)).astype(BF16)
    x = x_ref[...].astype(BF16)
    logits = _dot(x, w_hi) + _dot(x, w_lo)
    tm = logits.shape[0]
    lane = lax.broadcasted_iota(jnp.int32, logits.shape, 1).astype(F32)
    neg = jnp.float32(-jnp.inf)
    logits = jnp.where(lane < ne, logits, neg)
    m1 = jnp.max(logits, axis=-1, keepdims=True)
    i1 = jnp.min(jnp.where(logits == m1, lane, float(LANES)), axis=-1, keepdims=True)
    rest = jnp.where(lane == i1, neg, logits)
    m2 = jnp.max(rest, axis=-1, keepdims=True)
    i2 = jnp.min(jnp.where(rest == m2, lane, float(LANES)), axis=-1, keepdims=True)
    e2 = jnp.exp(m2 - m1)
    w1 = 1.0 / (1.0 + e2)
    w2 = e2 / (1.0 + e2)
    sel = jnp.where((lane == i1) | (lane == i2), 1.0, 0.0)
    ri = lax.broadcasted_iota(jnp.int32, (tm, tm), 0)
    ci = lax.broadcasted_iota(jnp.int32, (tm, tm), 1)
    before = jnp.where(ci < ri, 1.0, 0.0).astype(BF16)
    rank = _dot(before, sel.astype(BF16)) + base_s[0:1, :]
    r1 = jnp.sum(jnp.where(lane == i1, rank, 0.0), axis=-1, keepdims=True)
    r2 = jnp.sum(jnp.where(lane == i2, rank, 0.0), axis=-1, keepdims=True)
    aux = jnp.zeros(logits.shape, F32)
    for k, v in enumerate((i1, i2, r1, r2, w1, w2)):
        aux = jnp.where(lane == float(k), v, aux)
    aux_ref[...] = aux
    base_s[...] = base_s[...] + jnp.sum(sel, axis=0, keepdims=True)
    cnt_ref[...] = base_s[...]


def _router(x, w_pad, ne):
    n, d = x.shape
    return pl.pallas_call(
        functools.partial(_router_kernel, ne=ne),
        grid=(n // TM,),
        in_specs=[pl.BlockSpec((TM, d), lambda i: (i, 0)),
                  pl.BlockSpec((d, LANES), lambda i: (0, 0))],
        out_specs=[pl.BlockSpec((TM, LANES), lambda i: (i, 0)),
                   pl.BlockSpec((8, LANES), lambda i: (0, 0))],
        out_shape=[jax.ShapeDtypeStruct((n, LANES), F32), jax.ShapeDtypeStruct((8, LANES), F32)],
        scratch_shapes=[pltpu.VMEM((8, LANES), F32)],
        compiler_params=_cp(("arbitrary",)),
        name="router",
    )(x, w_pad)


def _moe_plan(aux, cnt, n, ne, tile, max_tiles):
    e = aux[:, 0:2].astype(jnp.int32)
    r = aux[:, 2:4].astype(jnp.int32)
    c = cnt[0, :ne].astype(jnp.int32)
    pc = (c + tile - 1) // tile * tile
    ends = jnp.cumsum(pc)
    slot = ((ends - pc)[e] + r).reshape(-1)
    tok = jnp.broadcast_to(jnp.arange(n, dtype=F32)[:, None], (n, 2)).reshape(-1)
    m = max_tiles * tile
    pad = jnp.broadcast_to(jnp.array([float(n), 0.0], F32), (m, 2))
    table = pad.at[slot].set(jnp.stack([tok, aux[:, 4:6].reshape(-1)], axis=1))
    slot_tok = table[:, 0].astype(jnp.int32)
    slot_gate = table[:, 1]
    tstart = jnp.arange(max_tiles, dtype=jnp.int32) * tile
    tile_e = jnp.minimum(jnp.sum(tstart[:, None] >= ends[None, :], axis=1), ne - 1).astype(jnp.int32)
    tile_valid = (tstart < ends[-1]).astype(jnp.int32)
    return slot_tok.reshape(max_tiles, 1, tile), slot_gate.reshape(max_tiles, 1, tile), tile_e, tile_valid


def _expert_changed(t, te_ref):
    return jnp.logical_or(t == 0, te_ref[t] != te_ref[jnp.maximum(t - 1, 0)])


def _moe_up_kernel(te_ref, tv_ref, idx_ref, nxt_ref, h_ref, wg_ref, wu_ref, o_ref, xbuf, wg_s, wu_s, *, tile):
    t = pl.program_id(0)
    valid = tv_ref[t] == 1
    last = h_ref.shape[0] - 1
    cur = t % 2

    def gather_row(src_idx_ref, buf, j):
        xbuf[buf, pl.ds(j, 1), :] = h_ref[pl.ds(jnp.minimum(src_idx_ref[0, 0, j], last), 1), :]

    @pl.when(t == 0)
    def _():
        def first(j, carry):
            gather_row(idx_ref, 0, j)
            return carry

        lax.fori_loop(0, tile, first, 0, unroll=8)

    @pl.when(jnp.logical_and(valid, _expert_changed(t, te_ref)))
    def _():
        wg_s[...] = wg_ref[...].astype(BF16)
        wu_s[...] = wu_ref[...].astype(BF16)

    @pl.when(valid)
    def _():
        x = xbuf[cur].astype(BF16)
        o_ref[...] = (_silu(_dot(x, wg_s[...])) * _dot(x, wu_s[...])).astype(BF16)
        for j in range(tile):
            gather_row(nxt_ref, 1 - cur, j)

    @pl.when(jnp.logical_not(valid))
    def _():
        o_ref[...] = jnp.zeros(o_ref.shape, BF16)


def _moe_up(h, slot_tok, tile_e, tile_valid, wg, wu, e_layer, tile):
    max_tiles = slot_tok.shape[0]
    _, _, d, f = wg.shape
    wspec = pl.BlockSpec((None, None, d, f), lambda t, te, tv: (e_layer, te[t], 0, 0))
    grid_spec = pltpu.PrefetchScalarGridSpec(
        num_scalar_prefetch=2,
        grid=(max_tiles,),
        in_specs=[pl.BlockSpec((1, 1, tile), lambda t, te, tv: (t, 0, 0), memory_space=pltpu.SMEM),
                  pl.BlockSpec((1, 1, tile), lambda t, te, tv: (jnp.minimum(t + 1, max_tiles - 1), 0, 0),
                               memory_space=pltpu.SMEM),
                  pl.BlockSpec(h.shape, lambda t, te, tv: (0, 0), pipeline_mode=pl.Buffered(1)),
                  wspec, wspec],
        out_specs=pl.BlockSpec((tile, f), lambda t, te, tv: (t, 0)),
        scratch_shapes=[pltpu.VMEM((2, tile, d), F32),
                        pltpu.VMEM((d, f), BF16), pltpu.VMEM((d, f), BF16)],
    )
    return pl.pallas_call(
        functools.partial(_moe_up_kernel, tile=tile),
        grid_spec=grid_spec,
        out_shape=jax.ShapeDtypeStruct((max_tiles * tile, f), BF16),
        compiler_params=_cp(("arbitrary",), vmem=MOE_UP_VMEM_LIMIT),
        name="moe_up",
    )(tile_e, tile_valid, slot_tok, slot_tok, h, wg, wu)


def _moe_down_kernel(te_ref, tv_ref, idx_ref, gt_ref, hid_ref, wd_ref, g_ref, o_ref, acc, ybuf, tbuf, wd_s,
                     *, tile, n_tiles, tm_out):
    t = pl.program_id(0)
    tt = jnp.minimum(t, n_tiles - 1)

    @pl.when(t == 0)
    def _():
        acc[...] = jnp.zeros(acc.shape, F32)

    is_tile = jnp.logical_and(t < n_tiles, tv_ref[tt] == 1)

    @pl.when(jnp.logical_and(is_tile, _expert_changed(tt, te_ref)))
    def _():
        wd_s[...] = wd_ref[...].astype(BF16)

    @pl.when(is_tile)
    def _():
        for j in range(tile):
            tbuf[j:j + 1, :] = acc[pl.ds(idx_ref[0, 0, j], 1), :]
        ybuf[...] = _dot(hid_ref[...], wd_s[...])

        def put(j, carry):
            acc[pl.ds(idx_ref[0, 0, j], 1), :] = tbuf[pl.ds(j, 1), :] + gt_ref[0, 0, j] * ybuf[pl.ds(j, 1), :]
            return carry

        lax.fori_loop(0, tile, put, 0, unroll=8)

    @pl.when(t >= n_tiles)
    def _():
        r0 = pl.multiple_of((t - n_tiles) * tm_out, tm_out)
        o_ref[...] = _rms(acc[pl.ds(r0, tm_out), :], g_ref[...])


def _moe_down(hid, slot_tok, slot_gate, tile_e, tile_valid, wd, e_layer, g, n, tile):
    n_tiles = slot_tok.shape[0]
    _, _, f, d = wd.shape
    tm_out = TM
    n_flush = n // tm_out

    def tix(t):
        return jnp.minimum(t, n_tiles - 1)

    smem_tile = pl.BlockSpec((1, 1, tile), lambda t, te, tv: (tix(t), 0, 0), memory_space=pltpu.SMEM)
    grid_spec = pltpu.PrefetchScalarGridSpec(
        num_scalar_prefetch=2,
        grid=(n_tiles + n_flush,),
        in_specs=[smem_tile, smem_tile,
                  pl.BlockSpec((tile, f), lambda t, te, tv: (tix(t), 0)),
                  pl.BlockSpec((None, None, f, d), lambda t, te, tv: (e_layer, te[tix(t)], 0, 0)),
                  pl.BlockSpec((1, d), lambda t, te, tv: (0, 0))],
        out_specs=pl.BlockSpec((tm_out, d), lambda t, te, tv: (jnp.maximum(t - n_tiles, 0), 0)),
        scratch_shapes=[pltpu.VMEM((n + 8, d), F32), pltpu.VMEM((tile, d), F32), pltpu.VMEM((tile, d), F32),
                        pltpu.VMEM((f, d), BF16)],
    )
    return pl.pallas_call(
        functools.partial(_moe_down_kernel, tile=tile, n_tiles=n_tiles, tm_out=tm_out),
        grid_spec=grid_spec,
        out_shape=jax.ShapeDtypeStruct((n, d), F32),
        compiler_params=_cp(("arbitrary",)),
        name="moe_down",
    )(tile_e, tile_valid, slot_tok, slot_gate, hid, wd, g.reshape(1, d))


def kernel(x_prompt, x_sample, state_hgrn, state_s5, state_gdn, c, c_ctx, ada_w, ada_b, norm_g, w_in, hgrn_lb, hgrn_norm, s5_a_re, s5_a_im, s5_log_dt, s5_b_re, s5_b_im, s5_c_re, s5_c_im, s5_d, s5_glu_w, s5_glu_b, gdn_conv, gdn_a_log, gdn_dt_bias, gdn_norm, w_branch, w_out, ffn_wg, ffn_wu, ffn_wd, moe_router, moe_wg, moe_wu, moe_wd):
    bp, tp, d = x_prompt.shape
    bs, ts, _ = x_sample.shape
    depth = w_in.shape[0]
    n_p, n_s = bp * tp, bs * ts
    ne = moe_router.shape[-1]
    main_cols = 10 * BR_WIDTH
    ab_cols = 4 * N_HEADS

    c8 = jnp.zeros((8, d), F32).at[0].set(c_ctx).at[1:1 + bs].set(c)
    mod_all = _ada_mod(c8, ada_w, ada_b)

    lb_soft = jax.nn.softmax(hgrn_lb.astype(F32), axis=0)
    lb_all = jnp.cumsum(lb_soft, axis=0) - lb_soft[:1]

    mods = mod_all[:, :1 + bs].reshape(depth, 1 + bs, 6, d)
    zero_row = jnp.zeros((1 + bs, d), F32)

    def modv(gate, scale, shift, g):
        rows = [zero_row if sel is None else mods[sel[0], :, sel[1]] for sel in (gate, scale, shift)]
        rows.append(zero_row if g is None else jnp.broadcast_to(g, (1 + bs, d)))
        return jnp.stack(rows + [zero_row] * 4, axis=1)

    x = (x_prompt.reshape(n_p, d), x_sample.reshape(n_s, d))
    (h,) = _resnorm(x, None, modv(None, (0, 1), (0, 0), norm_g[0, 0]), n_p, ts, True)

    z_hgrn = jnp.zeros((bp, 2, N_HEADS, HEAD_D, HEAD_D), F32)
    ngb = S5_GROUPS // S5_GB
    w2 = 2 * S5_GB * S5_STATE
    st_s5 = []
    y_zero = jnp.zeros((n_p + n_s, 3 * BR_WIDTH), BF16)
    new_hgrn = new_gdn = jnp.zeros((bp, depth, 2, N_HEADS, HEAD_D, HEAD_D), F32)
    w_in_t = jnp.swapaxes(w_in, 1, 2)
    s5_tb = 1024
    bb, cc, lam = _s5_params(s5_a_re, s5_a_im, s5_log_dt, s5_b_re, s5_b_im, s5_c_re, s5_c_im)

    for l in range(depth):
        col_major = l % 2 == 1
        p_main = _mm_t(h, w_in_t, l, main_cols, 2 * BR_WIDTH, "w_in_main", tm=2048)
        w_ab = jnp.pad(w_in_t[l, main_cols:main_cols + ab_cols], ((0, LANES - ab_cols), (0, 0)))
        p_ab = _mm_t(h, w_ab, None, LANES, LANES, "w_in_ab", tm=1024)
        p_mg = _mm_t(h, w_in_t[l, main_cols + ab_cols:], None, 3 * d, 2 * BR_WIDTH, "w_in_merge", tm=2048,
                     out_dtype=BF16)

        blk = BR_WIDTH // LANES
        br = y_zero

        br, new_hgrn = _hgrn(p_main, lb_all[l], hgrn_norm[l], z_hgrn, 0, bp, tp, br, 0, new_hgrn, l, depth)
        br, _ = _hgrn(p_main, lb_all[l], hgrn_norm[l], state_hgrn[:, l].astype(F32), n_p // ts, bs, ts, br, 0,
                      col_major=col_major)

        seq_per_blk = 8
        tb_p = seq_per_blk * tp
        nblk_p = n_p // tb_p
        br, sb = _s5(p_main, bb, cc, lam, l, s5_d[l], jnp.zeros((nblk_p, ngb, 2, 8, w2), F32),
                     0, nblk_p, tb_p, tp, tp, True, br, blk)
        s0s = state_s5[:, l].astype(F32)
        s0s = s0s.reshape(bs, 2, ngb, S5_GB * S5_STATE, 2).transpose(0, 2, 1, 4, 3).reshape(bs, ngb, 2, 1, w2)
        s0s = jnp.pad(s0s, ((0, 0), (0, 0), (0, 0), (0, 7), (0, 0)))
        br, _ = _s5(p_main, bb, cc, lam, l, s5_d[l], s0s, n_p // s5_tb, n_s // s5_tb, s5_tb, ts, S5_SEG, False, br, blk,
                    col_major=col_major)
        sb = sb[:, :, :, :seq_per_blk].reshape(nblk_p, ngb, 2, seq_per_blk, 2, S5_GB, S5_STATE)
        sb = sb.transpose(0, 3, 2, 1, 5, 6, 4).reshape(bp, 2, S5_GROUPS, S5_STATE, 2)
        st_s5.append(sb)

        br, new_gdn = _gdn(p_main, p_ab, gdn_conv[l], gdn_a_log[l], gdn_dt_bias[l], gdn_norm[l], z_hgrn, 0, bp, tp,
                           br, 2 * blk, new_gdn, l, depth)
        br, _ = _gdn(p_main, p_ab, gdn_conv[l], gdn_a_log[l], gdn_dt_bias[l], gdn_norm[l],
                     state_gdn[:, l].astype(F32), n_p // ts, bs, ts, br, 2 * blk, col_major=col_major)

        x, h2 = _merge(br, p_mg, w_branch, w_out, s5_glu_w, s5_glu_b, norm_g[l, 1], l, x,
                       modv((l, 2), (l, 4), (l, 3), norm_g[l, 2]), n_p, ts, BF16 if l % 2 == 0 else F32)
        last = l + 1 == depth
        next_mod = modv((l, 5), None, None, None) if last else modv((l, 5), (l + 1, 1), (l + 1, 0), norm_g[l + 1, 0])

        if l % 2 == 0:
            e = l // 2
            hid = _swiglu_up(h2, ffn_wg, ffn_wu, e, ffn_wg.shape[-1] // 2)
            x, h = _down_res(hid, ffn_wd, e, norm_g[l, 3], x, next_mod, n_p, ts, BF16)
            if last:
                y_prompt, y_sample = x[:n_p], x[n_p:]
            continue
        else:
            e = l // 2
            n_tok = n_p + n_s
            max_tiles = 2 * n_tok // MOE_TILE + ne
            aux, cnt = _router(h2, jnp.pad(moe_router[e], ((0, 0), (0, LANES - ne))), ne)
            slot_tok, slot_gate, tile_e, tile_valid = _moe_plan(aux, cnt, n_tok, ne, MOE_TILE, max_tiles)
            hid = _moe_up(h2, slot_tok, tile_e, tile_valid, moe_wg, moe_wu, e, MOE_TILE)
            r_f = _moe_down(hid, slot_tok, slot_gate, tile_e, tile_valid, moe_wd, e, norm_g[l, 3], n_tok, MOE_TILE)

        if not last:
            x, h = _resnorm(x, r_f, next_mod, n_p, ts, True)
        else:
            (y_prompt,) = _resnorm(x, r_f, next_mod, n_p, ts, False, rows=(0, n_p))
            (y_sample,) = _resnorm(x, r_f, next_mod, n_p, ts, False, rows=(n_p, n_s))

    y_prompt = y_prompt.reshape(bp, tp, d)
    y_sample = y_sample.reshape(bs, ts, d)
    new_s5 = jnp.stack(st_s5, axis=1).astype(x_prompt.dtype)
    return (y_prompt, y_sample, new_hgrn.astype(x_prompt.dtype), new_s5, new_gdn.astype(x_prompt.dtype))
```
